```python
import jax, jax.numpy as jnp
from jax import lax
import numpy as np

D_MODEL = 2048
BATCH = 8
SEQ = 8192
DEPTH = 1

D_MIX = D_MODEL
GM_WIDTH = D_MIX // 2
LRU_WIDTH = D_MIX - GM_WIDTH
CHUNK = 128
GM_HEADS = 8
GM_HEAD_DIM = GM_WIDTH // GM_HEADS
LRU_HEADS = 8
LRU_BLOCK = LRU_WIDTH // LRU_HEADS
LRU_CONV = 4
LRU_C = 8.0
FFN_MULT = 3
D_FF = FFN_MULT * D_MODEL
FFN_CONV = 3
IN_COLS = 2 * GM_WIDTH + 2 * LRU_WIDTH
RMS_EPS = 1e-6
LN_EPS = 1e-5

kernel_name = "hymba_style_gmlp_rglru_convffn"


def _rmsnorm(x, g):
    xf = x.astype(jnp.float32)
    y = xf * lax.rsqrt(jnp.mean(xf * xf, axis=-1, keepdims=True) + RMS_EPS)
    return (y * g.astype(jnp.float32)).astype(x.dtype)


def _layernorm(x, g, b):
    xf = x.astype(jnp.float32)
    mu = jnp.mean(xf, axis=-1, keepdims=True)
    xc = xf - mu
    y = xc * lax.rsqrt(jnp.mean(xc * xc, axis=-1, keepdims=True) + LN_EPS)
    return (y * g.astype(jnp.float32) + b.astype(jnp.float32)).astype(x.dtype)


def _causal_dwconv(x, w, b):
    k_width = w.shape[0]
    s = x.shape[1]
    xp = jnp.pad(x, ((0, 0), (k_width - 1, 0), (0, 0)))
    y = b
    for k in range(k_width):
        y = y + xp[:, k:k + s] * w[k]
    return y


def _spatial_gating(z, v_g, v_b, ws, bs):
    u, v = jnp.split(z, 2, axis=-1)
    v = _layernorm(v, v_g, v_b)
    bsz, s, _ = v.shape
    vc = v.reshape(bsz, s // CHUNK, CHUNK, GM_HEADS, GM_HEAD_DIM)
    mask = jnp.tril(jnp.ones((CHUNK, CHUNK), dtype=bool))
    w = jnp.where(mask[None], ws, jnp.zeros((), ws.dtype))
    mixed = jnp.einsum('hts,bcshd->bcthd', w, vc) + bs.T[None, None, :, :, None]
    return u * mixed.reshape(bsz, s, GM_WIDTH)


def _lru_combine(left, right):
    a1, b1 = left
    a2, b2 = right
    return a1 * a2, a2 * b1 + b2


def _rg_lru(x, wa, ba, wx, bx, lam):
    bsz, s, w = x.shape
    xh = x.reshape(bsz, s, LRU_HEADS, LRU_BLOCK)
    r = jax.nn.sigmoid(jnp.einsum('bshi,hij->bshj', xh, wa) + ba).reshape(bsz, s, w)
    i = jax.nn.sigmoid(jnp.einsum('bshi,hij->bshj', xh, wx) + bx).reshape(bsz, s, w)
    log_a = -LRU_C * r.astype(jnp.float32) * jax.nn.softplus(-lam.astype(jnp.float32))
    a = jnp.exp(log_a)
    mult = jnp.sqrt(-jnp.expm1(2.0 * log_a))
    b = mult * (i * x).astype(jnp.float32)
    _, h = lax.associative_scan(_lru_combine, (a, b), axis=1)
    return h.astype(x.dtype)


def _fwd_setup_inputs(seed: int = 0) -> dict:
    key = jax.random.key(seed)
    ks = jax.random.split(key, 24)
    f32 = jnp.float32
    L = DEPTH

    def nrm(k, shape, scale):
        return jax.random.normal(k, shape, f32) * scale

    x = jax.random.normal(ks[0], (BATCH, SEQ, D_MODEL), f32)
    norm1_g = 1.0 + nrm(ks[1], (L, D_MODEL), 0.05)
    w_in = nrm(ks[2], (L, D_MODEL, IN_COLS), D_MODEL ** -0.5)
    gm_v_g = 1.0 + nrm(ks[3], (L, GM_WIDTH), 0.05)
    gm_v_b = nrm(ks[4], (L, GM_WIDTH), 0.02)
    gm_ws = nrm(ks[5], (L, GM_HEADS, CHUNK, CHUNK), CHUNK ** -0.5)
    gm_bs = 1.0 + nrm(ks[6], (L, GM_HEADS, CHUNK), 0.1)
    lru_conv_w = nrm(ks[7], (L, LRU_CONV, LRU_WIDTH), LRU_CONV ** -0.5)
    lru_conv_b = nrm(ks[8], (L, LRU_WIDTH), 0.02)
    lru_wa = nrm(ks[9], (L, LRU_HEADS, LRU_BLOCK, LRU_BLOCK), LRU_BLOCK ** -0.5)
    lru_ba = nrm(ks[10], (L, LRU_HEADS, LRU_BLOCK), 0.02)
    lru_wx = nrm(ks[11], (L, LRU_HEADS, LRU_BLOCK, LRU_BLOCK), LRU_BLOCK ** -0.5)
    lru_bx = nrm(ks[12], (L, LRU_HEADS, LRU_BLOCK), 0.02)
    a_c = jax.random.uniform(ks[13], (L, LRU_WIDTH), f32, 0.9, 0.999)
    a_base = a_c ** (1.0 / LRU_C)
    lru_lambda = jnp.log(a_base) - jnp.log1p(-a_base)
    gm_out_g = 1.0 + nrm(ks[14], (L, GM_WIDTH), 0.05)
    lru_out_g = 1.0 + nrm(ks[15], (L, LRU_WIDTH), 0.05)
    w_out = nrm(ks[16], (L, D_MIX, D_MODEL), D_MIX ** -0.5)
    norm2_g = 1.0 + nrm(ks[17], (L, D_MODEL), 0.05)
    ffn_w_up = nrm(ks[18], (L, D_MODEL, 2 * D_FF), D_MODEL ** -0.5)
    ffn_conv_w = nrm(ks[19], (L, FFN_CONV, 2 * D_FF), FFN_CONV ** -0.5)
    ffn_conv_b = nrm(ks[20], (L, 2 * D_FF), 0.02)
    ffn_w_down = nrm(ks[21], (L, D_FF, D_MODEL), D_FF ** -0.5)
    final_g = 1.0 + nrm(ks[22], (D_MODEL,), 0.05)
    return {
        "x": x, "norm1_g": norm1_g, "w_in": w_in,
        "gm_v_g": gm_v_g, "gm_v_b": gm_v_b, "gm_ws": gm_ws, "gm_bs": gm_bs,
        "lru_conv_w": lru_conv_w, "lru_conv_b": lru_conv_b,
        "lru_wa": lru_wa, "lru_ba": lru_ba, "lru_wx": lru_wx, "lru_bx": lru_bx,
        "lru_lambda": lru_lambda, "gm_out_g": gm_out_g, "lru_out_g": lru_out_g,
        "w_out": w_out, "norm2_g": norm2_g, "ffn_w_up": ffn_w_up,
        "ffn_conv_w": ffn_conv_w, "ffn_conv_b": ffn_conv_b, "ffn_w_down": ffn_w_down,
        "final_g": final_g,
    }


def _fwd_reference(x, norm1_g, w_in, gm_v_g, gm_v_b, gm_ws, gm_bs, lru_conv_w, lru_conv_b,
              lru_wa, lru_ba, lru_wx, lru_bx, lru_lambda, gm_out_g, lru_out_g, w_out,
              norm2_g, ffn_w_up, ffn_conv_w, ffn_conv_b, ffn_w_down, final_g):
    for l in range(DEPTH):
        h = _rmsnorm(x, norm1_g[l])
        p = jnp.einsum('bsd,de->bse', h, w_in[l])
        z_gm = p[..., :2 * GM_WIDTH]
        g_lru = p[..., 2 * GM_WIDTH:2 * GM_WIDTH + LRU_WIDTH]
        x_lru = p[..., 2 * GM_WIDTH + LRU_WIDTH:]
        y_gm = _spatial_gating(jax.nn.gelu(z_gm), gm_v_g[l], gm_v_b[l], gm_ws[l], gm_bs[l])
        xr = _causal_dwconv(x_lru, lru_conv_w[l], lru_conv_b[l])
        y_lru = _rg_lru(xr, lru_wa[l], lru_ba[l], lru_wx[l], lru_bx[l], lru_lambda[l])
        y_lru = y_lru * jax.nn.gelu(g_lru)
        y = jnp.concatenate([_rmsnorm(y_gm, gm_out_g[l]), _rmsnorm(y_lru, lru_out_g[l])], axis=-1)
        x = x + jnp.einsum('bse,ed->bsd', y, w_out[l])
        h = _rmsnorm(x, norm2_g[l])
        up = jnp.einsum('bsd,df->bsf', h, ffn_w_up[l])
        up = _causal_dwconv(up, ffn_conv_w[l], ffn_conv_b[l])
        gate, val = jnp.split(up, 2, axis=-1)
        x = x + jnp.einsum('bsf,fd->bsd', jax.nn.gelu(gate) * val, ffn_w_down[l])
    return _rmsnorm(x, final_g)


import jax as _jax
import jax.numpy as _jnp

TWIN_FORMAT = 'train_step'
FWD_PARAMS = ['x', 'norm1_g', 'w_in', 'gm_v_g', 'gm_v_b', 'gm_ws', 'gm_bs', 'lru_conv_w', 'lru_conv_b', 'lru_wa', 'lru_ba', 'lru_wx', 'lru_bx', 'lru_lambda', 'gm_out_g', 'lru_out_g', 'w_out', 'norm2_g', 'ffn_w_up', 'ffn_conv_w', 'ffn_conv_b', 'ffn_w_down', 'final_g']
TWIN_WEIGHTS = ['norm1_g', 'w_in', 'gm_v_g', 'gm_v_b', 'gm_ws', 'gm_bs', 'lru_conv_w', 'lru_conv_b', 'lru_wa', 'lru_ba', 'lru_wx', 'lru_bx', 'lru_lambda', 'gm_out_g', 'lru_out_g', 'w_out', 'norm2_g', 'ffn_w_up', 'ffn_conv_w', 'ffn_conv_b', 'ffn_w_down', 'final_g']
TWIN_DIFF_INPUT = 'x'
TWIN_INPUTS = ['x', 'norm1_g', 'w_in', 'gm_v_g', 'gm_v_b', 'gm_ws', 'gm_bs', 'lru_conv_w', 'lru_conv_b', 'lru_wa', 'lru_ba', 'lru_wx', 'lru_bx', 'lru_lambda', 'gm_out_g', 'lru_out_g', 'w_out', 'norm2_g', 'ffn_w_up', 'ffn_conv_w', 'ffn_conv_b', 'ffn_w_down', 'final_g', 'loss_target', 'm_norm1_g', 'm_w_in', 'm_gm_v_g', 'm_gm_v_b', 'm_gm_ws', 'm_gm_bs', 'm_lru_conv_w', 'm_lru_conv_b', 'm_lru_wa', 'm_lru_ba', 'm_lru_wx', 'm_lru_bx', 'm_lru_lambda', 'm_gm_out_g', 'm_lru_out_g', 'm_w_out', 'm_norm2_g', 'm_ffn_w_up', 'm_ffn_conv_w', 'm_ffn_conv_b', 'm_ffn_w_down', 'm_final_g', 'v_norm1_g', 'v_w_in', 'v_gm_v_g', 'v_gm_v_b', 'v_gm_ws', 'v_gm_bs', 'v_lru_conv_w', 'v_lru_conv_b', 'v_lru_wa', 'v_lru_ba', 'v_lru_wx', 'v_lru_bx', 'v_lru_lambda', 'v_gm_out_g', 'v_lru_out_g', 'v_w_out', 'v_norm2_g', 'v_ffn_w_up', 'v_ffn_conv_w', 'v_ffn_conv_b', 'v_ffn_w_down', 'v_final_g']
TWIN_OUTPUTS = ['loss', 'grad_x', 'grad_norm1_g', 'grad_w_in', 'grad_gm_v_g', 'grad_gm_v_b', 'grad_gm_ws', 'grad_gm_bs', 'grad_lru_conv_w', 'grad_lru_conv_b', 'grad_lru_wa', 'grad_lru_ba', 'grad_lru_wx', 'grad_lru_bx', 'grad_lru_lambda', 'grad_gm_out_g', 'grad_lru_out_g', 'grad_w_out', 'grad_norm2_g', 'grad_ffn_w_up', 'grad_ffn_conv_w', 'grad_ffn_conv_b', 'grad_ffn_w_down', 'grad_final_g', 'delta_norm1_g', 'delta_w_in', 'delta_gm_v_g', 'delta_gm_v_b', 'delta_gm_ws', 'delta_gm_bs', 'delta_lru_conv_w', 'delta_lru_conv_b', 'delta_lru_wa', 'delta_lru_ba', 'delta_lru_wx', 'delta_lru_bx', 'delta_lru_lambda', 'delta_gm_out_g', 'delta_lru_out_g', 'delta_w_out', 'delta_norm2_g', 'delta_ffn_w_up', 'delta_ffn_conv_w', 'delta_ffn_conv_b', 'delta_ffn_w_down', 'delta_final_g', 'new_m_norm1_g', 'new_m_w_in', 'new_m_gm_v_g', 'new_m_gm_v_b', 'new_m_gm_ws', 'new_m_gm_bs', 'new_m_lru_conv_w', 'new_m_lru_conv_b', 'new_m_lru_wa', 'new_m_lru_ba', 'new_m_lru_wx', 'new_m_lru_bx', 'new_m_lru_lambda', 'new_m_gm_out_g', 'new_m_lru_out_g', 'new_m_w_out', 'new_m_norm2_g', 'new_m_ffn_w_up', 'new_m_ffn_conv_w', 'new_m_ffn_conv_b', 'new_m_ffn_w_down', 'new_m_final_g', 'new_v_norm1_g', 'new_v_w_in', 'new_v_gm_v_g', 'new_v_gm_v_b', 'new_v_gm_ws', 'new_v_gm_bs', 'new_v_lru_conv_w', 'new_v_lru_conv_b', 'new_v_lru_wa', 'new_v_lru_ba', 'new_v_lru_wx', 'new_v_lru_bx', 'new_v_lru_lambda', 'new_v_gm_out_g', 'new_v_lru_out_g', 'new_v_w_out', 'new_v_norm2_g', 'new_v_ffn_w_up', 'new_v_ffn_conv_w', 'new_v_ffn_conv_b', 'new_v_ffn_w_down', 'new_v_final_g']
TWIN_LEAF_KINDS = {'loss': 'loss', 'grad_x': 'grad_x', 'grad_norm1_g': 'grad_w', 'grad_w_in': 'grad_w', 'grad_gm_v_g': 'grad_w', 'grad_gm_v_b': 'grad_w', 'grad_gm_ws': 'grad_w', 'grad_gm_bs': 'grad_w', 'grad_lru_conv_w': 'grad_w', 'grad_lru_conv_b': 'grad_w', 'grad_lru_wa': 'grad_w', 'grad_lru_ba': 'grad_w', 'grad_lru_wx': 'grad_w', 'grad_lru_bx': 'grad_w', 'grad_lru_lambda': 'grad_w', 'grad_gm_out_g': 'grad_w', 'grad_lru_out_g': 'grad_w', 'grad_w_out': 'grad_w', 'grad_norm2_g': 'grad_w', 'grad_ffn_w_up': 'grad_w', 'grad_ffn_conv_w': 'grad_w', 'grad_ffn_conv_b': 'grad_w', 'grad_ffn_w_down': 'grad_w', 'grad_final_g': 'grad_w', 'delta_norm1_g': 'delta_w', 'delta_w_in': 'delta_w', 'delta_gm_v_g': 'delta_w', 'delta_gm_v_b': 'delta_w', 'delta_gm_ws': 'delta_w', 'delta_gm_bs': 'delta_w', 'delta_lru_conv_w': 'delta_w', 'delta_lru_conv_b': 'delta_w', 'delta_lru_wa': 'delta_w', 'delta_lru_ba': 'delta_w', 'delta_lru_wx': 'delta_w', 'delta_lru_bx': 'delta_w', 'delta_lru_lambda': 'delta_w', 'delta_gm_out_g': 'delta_w', 'delta_lru_out_g': 'delta_w', 'delta_w_out': 'delta_w', 'delta_norm2_g': 'delta_w', 'delta_ffn_w_up': 'delta_w', 'delta_ffn_conv_w': 'delta_w', 'delta_ffn_conv_b': 'delta_w', 'delta_ffn_w_down': 'delta_w', 'delta_final_g': 'delta_w', 'new_m_norm1_g': 'new_m', 'new_m_w_in': 'new_m', 'new_m_gm_v_g': 'new_m', 'new_m_gm_v_b': 'new_m', 'new_m_gm_ws': 'new_m', 'new_m_gm_bs': 'new_m', 'new_m_lru_conv_w': 'new_m', 'new_m_lru_conv_b': 'new_m', 'new_m_lru_wa': 'new_m', 'new_m_lru_ba': 'new_m', 'new_m_lru_wx': 'new_m', 'new_m_lru_bx': 'new_m', 'new_m_lru_lambda': 'new_m', 'new_m_gm_out_g': 'new_m', 'new_m_lru_out_g': 'new_m', 'new_m_w_out': 'new_m', 'new_m_norm2_g': 'new_m', 'new_m_ffn_w_up': 'new_m', 'new_m_ffn_conv_w': 'new_m', 'new_m_ffn_conv_b': 'new_m', 'new_m_ffn_w_down': 'new_m', 'new_m_final_g': 'new_m', 'new_v_norm1_g': 'new_v', 'new_v_w_in': 'new_v', 'new_v_gm_v_g': 'new_v', 'new_v_gm_v_b': 'new_v', 'new_v_gm_ws': 'new_v', 'new_v_gm_bs': 'new_v', 'new_v_lru_conv_w': 'new_v', 'new_v_lru_conv_b': 'new_v', 'new_v_lru_wa': 'new_v', 'new_v_lru_ba': 'new_v', 'new_v_lru_wx': 'new_v', 'new_v_lru_bx': 'new_v', 'new_v_lru_lambda': 'new_v', 'new_v_gm_out_g': 'new_v', 'new_v_lru_out_g': 'new_v', 'new_v_w_out': 'new_v', 'new_v_norm2_g': 'new_v', 'new_v_ffn_w_up': 'new_v', 'new_v_ffn_conv_w': 'new_v', 'new_v_ffn_conv_b': 'new_v', 'new_v_ffn_w_down': 'new_v', 'new_v_final_g': 'new_v'}


def _forward(args):
    return _fwd_reference(*[args[k] for k in FWD_PARAMS])


def _output_shape():
    def fwd():
        inp = _fwd_setup_inputs(0)
        return _fwd_reference(*[inp[k] for k in FWD_PARAMS])
    out = _jax.eval_shape(fwd)
    return out.shape, out.dtype

N_MICROBATCH = 1
ADAM_LR = 0.001
ADAM_B1 = 0.9
ADAM_B2 = 0.999
ADAM_EPS = 1e-08
ADAM_WD = 0.01
ADAM_STEP = 10
PER_EXAMPLE_BATCH_AXIS = {'x': 0, 'loss_target': 0}
SHARED_INPUTS = []
_WEIGHT_DTYPES = {'norm1_g': _jnp.float32, 'w_in': _jnp.float32, 'gm_v_g': _jnp.float32, 'gm_v_b': _jnp.float32, 'gm_ws': _jnp.float32, 'gm_bs': _jnp.float32, 'lru_conv_w': _jnp.float32, 'lru_conv_b': _jnp.float32, 'lru_wa': _jnp.float32, 'lru_ba': _jnp.float32, 'lru_wx': _jnp.float32, 'lru_bx': _jnp.float32, 'lru_lambda': _jnp.float32, 'gm_out_g': _jnp.float32, 'lru_out_g': _jnp.float32, 'w_out': _jnp.float32, 'norm2_g': _jnp.float32, 'ffn_w_up': _jnp.float32, 'ffn_conv_w': _jnp.float32, 'ffn_conv_b': _jnp.float32, 'ffn_w_down': _jnp.float32, 'final_g': _jnp.float32}
MOMENT_SCALE = {'norm1_g': 1.340750e-01, 'w_in': 9.247504e-02, 'gm_v_g': 5.334843e-02, 'gm_v_b': 5.276284e-02, 'gm_ws': 5.387854e-02, 'gm_bs': 7.857307e-02, 'lru_conv_w': 1.114589e-01, 'lru_conv_b': 1.456802e+00, 'lru_wa': 3.860244e-02, 'lru_ba': 3.140555e-02, 'lru_wx': 6.922620e-02, 'lru_bx': 3.622961e-02, 'lru_lambda': 5.863251e-02, 'gm_out_g': 1.219593e-01, 'lru_out_g': 1.073915e-01, 'w_out': 1.200393e-01, 'norm2_g': 8.226645e-02, 'ffn_w_up': 3.138557e-02, 'ffn_conv_w': 3.217120e-02, 'ffn_conv_b': 4.333283e-02, 'ffn_w_down': 5.528406e-02, 'final_g': 3.203635e+01}


def _to_microbatches(a, axis):
    t = _jnp.moveaxis(a, axis, 0)
    t = t.reshape((N_MICROBATCH, t.shape[0] // N_MICROBATCH) + t.shape[1:])
    return _jnp.moveaxis(t, 1, axis + 1)


def setup_inputs(seed: int = 0) -> dict:
    inp = _fwd_setup_inputs(seed)
    key = _jax.random.fold_in(_jax.random.key(seed), 7919)
    shape, _ = _output_shape()
    out = dict(inp)
    out["loss_target"] = _jax.random.normal(_jax.random.fold_in(key, 0), shape, _jnp.float32)
    for i, name in enumerate(TWIN_WEIGHTS):
        w = inp[name].astype(_jnp.float32)
        if MOMENT_SCALE is None:
            s = _jnp.sqrt(_jnp.mean(_jnp.square(w)) + 1e-30)
        else:
            s = MOMENT_SCALE[name]
        km, kv = _jax.random.split(_jax.random.fold_in(key, i + 1))
        out[name] = w
        out["m_" + name] = s * _jax.random.normal(km, w.shape, _jnp.float32)
        out["v_" + name] = (s * s) * _jax.random.uniform(kv, w.shape, _jnp.float32, 0.5, 1.5)
    if N_MICROBATCH > 1:
        for name, axis in PER_EXAMPLE_BATCH_AXIS.items():
            out[name] = _to_microbatches(out[name], axis)
    return {'x': out['x'], 'norm1_g': out['norm1_g'], 'w_in': out['w_in'], 'gm_v_g': out['gm_v_g'], 'gm_v_b': out['gm_v_b'], 'gm_ws': out['gm_ws'], 'gm_bs': out['gm_bs'], 'lru_conv_w': out['lru_conv_w'], 'lru_conv_b': out['lru_conv_b'], 'lru_wa': out['lru_wa'], 'lru_ba': out['lru_ba'], 'lru_wx': out['lru_wx'], 'lru_bx': out['lru_bx'], 'lru_lambda': out['lru_lambda'], 'gm_out_g': out['gm_out_g'], 'lru_out_g': out['lru_out_g'], 'w_out': out['w_out'], 'norm2_g': out['norm2_g'], 'ffn_w_up': out['ffn_w_up'], 'ffn_conv_w': out['ffn_conv_w'], 'ffn_conv_b': out['ffn_conv_b'], 'ffn_w_down': out['ffn_w_down'], 'final_g': out['final_g'], 'loss_target': out['loss_target'], 'm_norm1_g': out['m_norm1_g'], 'm_w_in': out['m_w_in'], 'm_gm_v_g': out['m_gm_v_g'], 'm_gm_v_b': out['m_gm_v_b'], 'm_gm_ws': out['m_gm_ws'], 'm_gm_bs': out['m_gm_bs'], 'm_lru_conv_w': out['m_lru_conv_w'], 'm_lru_conv_b': out['m_lru_conv_b'], 'm_lru_wa': out['m_lru_wa'], 'm_lru_ba': out['m_lru_ba'], 'm_lru_wx': out['m_lru_wx'], 'm_lru_bx': out['m_lru_bx'], 'm_lru_lambda': out['m_lru_lambda'], 'm_gm_out_g': out['m_gm_out_g'], 'm_lru_out_g': out['m_lru_out_g'], 'm_w_out': out['m_w_out'], 'm_norm2_g': out['m_norm2_g'], 'm_ffn_w_up': out['m_ffn_w_up'], 'm_ffn_conv_w': out['m_ffn_conv_w'], 'm_ffn_conv_b': out['m_ffn_conv_b'], 'm_ffn_w_down': out['m_ffn_w_down'], 'm_final_g': out['m_final_g'], 'v_norm1_g': out['v_norm1_g'], 'v_w_in': out['v_w_in'], 'v_gm_v_g': out['v_gm_v_g'], 'v_gm_v_b': out['v_gm_v_b'], 'v_gm_ws': out['v_gm_ws'], 'v_gm_bs': out['v_gm_bs'], 'v_lru_conv_w': out['v_lru_conv_w'], 'v_lru_conv_b': out['v_lru_conv_b'], 'v_lru_wa': out['v_lru_wa'], 'v_lru_ba': out['v_lru_ba'], 'v_lru_wx': out['v_lru_wx'], 'v_lru_bx': out['v_lru_bx'], 'v_lru_lambda': out['v_lru_lambda'], 'v_gm_out_g': out['v_gm_out_g'], 'v_lru_out_g': out['v_lru_out_g'], 'v_w_out': out['v_w_out'], 'v_norm2_g': out['v_norm2_g'], 'v_ffn_w_up': out['v_ffn_w_up'], 'v_ffn_conv_w': out['v_ffn_conv_w'], 'v_ffn_conv_b': out['v_ffn_conv_b'], 'v_ffn_w_down': out['v_ffn_w_down'], 'v_final_g': out['v_final_g']}


def _loss(weights, diff, rest, loss_target):
    with _jax.named_scope("forward"):
        args = {**rest, TWIN_DIFF_INPUT: diff, **{k: w.astype(_WEIGHT_DTYPES[k]) for k, w in weights.items()}}
        y = _forward(args)
    with _jax.named_scope("loss_head"):
        err = _jnp.square(y.astype(_jnp.float32) - loss_target)
        return 0.5 * _jnp.sum(_jnp.mean(err, axis=-1)) if err.ndim else 0.5 * err


def _adamw(w, g, m, v):
    m = ADAM_B1 * m + (1.0 - ADAM_B1) * g
    v = ADAM_B2 * v + (1.0 - ADAM_B2) * _jnp.square(g)
    m_hat = m / (1.0 - ADAM_B1 ** ADAM_STEP)
    v_hat = v / (1.0 - ADAM_B2 ** ADAM_STEP)
    delta = -ADAM_LR * (m_hat / (_jnp.sqrt(v_hat) + ADAM_EPS) + ADAM_WD * w)
    return delta, m, v


def reference(x, norm1_g, w_in, gm_v_g, gm_v_b, gm_ws, gm_bs, lru_conv_w, lru_conv_b, lru_wa, lru_ba, lru_wx, lru_bx, lru_lambda, gm_out_g, lru_out_g, w_out, norm2_g, ffn_w_up, ffn_conv_w, ffn_conv_b, ffn_w_down, final_g, loss_target, m_norm1_g, m_w_in, m_gm_v_g, m_gm_v_b, m_gm_ws, m_gm_bs, m_lru_conv_w, m_lru_conv_b, m_lru_wa, m_lru_ba, m_lru_wx, m_lru_bx, m_lru_lambda, m_gm_out_g, m_lru_out_g, m_w_out, m_norm2_g, m_ffn_w_up, m_ffn_conv_w, m_ffn_conv_b, m_ffn_w_down, m_final_g, v_norm1_g, v_w_in, v_gm_v_g, v_gm_v_b, v_gm_ws, v_gm_bs, v_lru_conv_w, v_lru_conv_b, v_lru_wa, v_lru_ba, v_lru_wx, v_lru_bx, v_lru_lambda, v_gm_out_g, v_lru_out_g, v_w_out, v_norm2_g, v_ffn_w_up, v_ffn_conv_w, v_ffn_conv_b, v_ffn_w_down, v_final_g):
    given = dict(x=x, norm1_g=norm1_g, w_in=w_in, gm_v_g=gm_v_g, gm_v_b=gm_v_b, gm_ws=gm_ws, gm_bs=gm_bs, lru_conv_w=lru_conv_w, lru_conv_b=lru_conv_b, lru_wa=lru_wa, lru_ba=lru_ba, lru_wx=lru_wx, lru_bx=lru_bx, lru_lambda=lru_lambda, gm_out_g=gm_out_g, lru_out_g=lru_out_g, w_out=w_out, norm2_g=norm2_g, ffn_w_up=ffn_w_up, ffn_conv_w=ffn_conv_w, ffn_conv_b=ffn_conv_b, ffn_w_down=ffn_w_down, final_g=final_g, loss_target=loss_target, m_norm1_g=m_norm1_g, m_w_in=m_w_in, m_gm_v_g=m_gm_v_g, m_gm_v_b=m_gm_v_b, m_gm_ws=m_gm_ws, m_gm_bs=m_gm_bs, m_lru_conv_w=m_lru_conv_w, m_lru_conv_b=m_lru_conv_b, m_lru_wa=m_lru_wa, m_lru_ba=m_lru_ba, m_lru_wx=m_lru_wx, m_lru_bx=m_lru_bx, m_lru_lambda=m_lru_lambda, m_gm_out_g=m_gm_out_g, m_lru_out_g=m_lru_out_g, m_w_out=m_w_out, m_norm2_g=m_norm2_g, m_ffn_w_up=m_ffn_w_up, m_ffn_conv_w=m_ffn_conv_w, m_ffn_conv_b=m_ffn_conv_b, m_ffn_w_down=m_ffn_w_down, m_final_g=m_final_g, v_norm1_g=v_norm1_g, v_w_in=v_w_in, v_gm_v_g=v_gm_v_g, v_gm_v_b=v_gm_v_b, v_gm_ws=v_gm_ws, v_gm_bs=v_gm_bs, v_lru_conv_w=v_lru_conv_w, v_lru_conv_b=v_lru_conv_b, v_lru_wa=v_lru_wa, v_lru_ba=v_lru_ba, v_lru_wx=v_lru_wx, v_lru_bx=v_lru_bx, v_lru_lambda=v_lru_lambda, v_gm_out_g=v_gm_out_g, v_lru_out_g=v_lru_out_g, v_w_out=v_w_out, v_norm2_g=v_norm2_g, v_ffn_w_up=v_ffn_w_up, v_ffn_conv_w=v_ffn_conv_w, v_ffn_conv_b=v_ffn_conv_b, v_ffn_w_down=v_ffn_w_down, v_final_g=v_final_g)
    weights = {n: given[n] for n in TWIN_WEIGHTS}
    shared = {n: given[n] for n in SHARED_INPUTS}
    per_example = {n: given[n] for n in ['x']}
    grad_fn = _jax.value_and_grad(_loss, argnums=(0, 1))

    def one_microbatch(ex, loss_target):
        ex = dict(ex)
        diff = ex.pop(TWIN_DIFF_INPUT)
        return grad_fn(weights, diff, {**shared, **ex}, loss_target)

    if N_MICROBATCH == 1:
        loss, (grad_w, grad_x) = one_microbatch(per_example, given["loss_target"])
    else:
        def body(carry, xs):
            loss_sum, grad_sum = carry
            l_k, (gw_k, gx_k) = one_microbatch(xs[0], xs[1])
            with _jax.named_scope("update"):
                return (loss_sum + l_k, _jax.tree.map(_jnp.add, grad_sum, gw_k)), gx_k

        init = (_jnp.zeros((), _jnp.float32), _jax.tree.map(_jnp.zeros_like, weights))
        (loss, grad_w), grad_x = _jax.lax.scan(body, init, (per_example, given["loss_target"]))
    with _jax.named_scope("update"):
        delta_w, new_m, new_v = {}, {}, {}
        for n in TWIN_WEIGHTS:
            delta_w[n], new_m[n], new_v[n] = _adamw(weights[n], grad_w[n], given["m_" + n], given["v_" + n])
    return (loss, grad_x, *[grad_w[n] for n in TWIN_WEIGHTS], *[delta_w[n] for n in TWIN_WEIGHTS],
            *[new_m[n] for n in TWIN_WEIGHTS], *[new_v[n] for n in TWIN_WEIGHTS])
```

```python
import functools
import math

import jax
import jax.numpy as jnp
from jax import lax
from jax.experimental import pallas as pl
from jax.experimental.pallas import tpu as pltpu

F32 = jnp.float32
BF16 = jnp.bfloat16

RMS_EPS = 1e-6
LN_EPS = 1e-5
LRU_C = 8.0
CHUNK = 128
ADAM_LR = 0.001
ADAM_B1 = 0.9
ADAM_B2 = 0.999
ADAM_EPS = 1e-08
ADAM_WD = 0.01
ADAM_STEP = 10
N_DEV = 8
LANES = 128
MIB = 1024 * 1024

WEIGHTS = ['norm1_g', 'w_in', 'gm_v_g', 'gm_v_b', 'gm_ws', 'gm_bs', 'lru_conv_w', 'lru_conv_b', 'lru_wa', 'lru_ba',
           'lru_wx', 'lru_bx', 'lru_lambda', 'gm_out_g', 'lru_out_g', 'w_out', 'norm2_g', 'ffn_w_up', 'ffn_conv_w',
           'ffn_conv_b', 'ffn_w_down', 'final_g']
BIG = ['w_in', 'w_out', 'ffn_w_up', 'ffn_w_down']
CONV = ['lru_conv_w', 'ffn_conv_w']
SMALL = [n for n in WEIGHTS if n not in BIG and n not in CONV]

_DN_NT = (((1,), (1,)), ((), ()))
_DN_TN = (((0,), (0,)), ((), ()))
_GELU_C = 0.7978845608028654


def _cp(n_axes, vmem_mib=48):
    return pltpu.CompilerParams(dimension_semantics=("arbitrary",) * n_axes, vmem_limit_bytes=vmem_mib * MIB)


def _tile(n, pref, mult=8):
    t = min(pref, n)
    t -= t % mult
    while t >= mult:
        if n % t == 0:
            return t
        t -= mult
    return n


def _gelu(z):
    return 0.5 * z * (1.0 + jnp.tanh(_GELU_C * z * (1.0 + 0.044715 * z * z)))


def _gelu_parts(z):
    z2 = z * z
    t = jnp.tanh(_GELU_C * z * (1.0 + 0.044715 * z2))
    g = 0.5 * z * (1.0 + t)
    dg = 0.5 * (1.0 + t) + 0.5 * z * (1.0 - t * t) * (_GELU_C * (1.0 + 0.134145 * z2))
    return g, dg


def _sigmoid(z):
    return 1.0 / (1.0 + jnp.exp(-z))


def _softplus(z):
    t = jnp.exp(-jnp.abs(z))
    u = 1.0 + t
    log1p = jnp.where(u == 1.0, t, jnp.log(u) * t / (u - 1.0))
    return jnp.maximum(z, 0.0) + log1p


def _neg_expm1(y):
    e = jnp.exp(y)
    small = jnp.where(e == 1.0, y, (e - 1.0) * y / jnp.log(e))
    return -jnp.where(y < -0.5, e - 1.0, small)


def _rows_mean(v):
    return jnp.mean(v, axis=-1, keepdims=True)


def _col_sum(v):
    return jnp.sum(v, axis=0, keepdims=True)


def _shift_down(prev8, cur, k):
    if k == 0:
        return cur
    z = jnp.concatenate([prev8, cur], axis=0)
    return pltpu.roll(z, k, 0)[8:]


def _shift_up(cur, next8, k):
    if k == 0:
        return cur
    n = cur.shape[0]
    z = jnp.concatenate([cur, next8], axis=0)
    return pltpu.roll(z, n + 8 - k, 0)[:n]


def _mesh_pos():
    return lax.axis_index("x"), lax.axis_index("y"), lax.axis_index("c")


def _any_specs(n):
    return [pl.BlockSpec(memory_space=pl.ANY)] * n


def _all_gather_call(shards, name):
    n = len(shards)

    def body(*refs):
        x_refs, o_refs = refs[:n], refs[n:2 * n]
        send_sems, recv_sems, local_sems = refs[2 * n:]
        x, y, c = _mesh_pos()
        me, sib = (x, y, c), (x, y, 1 - c)
        chips = [(1 - x, y), (x, 1 - y), (1 - x, 1 - y)]

        def copy(a, k, block, to, src=None):
            dst = o_refs[a].at[4 * block[0] + 2 * block[1] + block[2]]
            return pltpu.make_async_remote_copy(
                src_ref=dst if src is None else src, dst_ref=dst,
                send_sem=send_sems.at[a, k], recv_sem=recv_sems.at[a, k],
                device_id=to, device_id_type=pl.DeviceIdType.MESH)

        mine = [pltpu.make_async_copy(x_refs[a], o_refs[a].at[4 * x + 2 * y + c], local_sems.at[a]) for a in range(n)]
        for cp in mine:
            cp.start()
        first = []
        for a in range(n):
            first.append(copy(a, 0, me, sib, src=x_refs[a]))
            for j, chip in enumerate(chips):
                first.append(copy(a, 1 + j, me, (*chip, c), src=x_refs[a]))
        for cp in first:
            cp.start()
        passed = []
        for a in range(n):
            for j, chip in enumerate(chips):
                copy(a, 1 + j, (*chip, c), me).wait_recv()
                fwd = copy(a, 4 + j, (*chip, c), sib)
                fwd.start()
                passed.append(fwd)
        for a in range(n):
            copy(a, 0, sib, me).wait_recv()
            for j, chip in enumerate(chips):
                copy(a, 4 + j, (*chip, 1 - c), me).wait_recv()
        for cp in first + passed:
            cp.wait_send()
        for cp in mine:
            cp.wait()

    return pl.pallas_call(
        body, name=name,
        out_shape=[jax.ShapeDtypeStruct((N_DEV,) + s.shape, s.dtype) for s in shards],
        in_specs=_any_specs(n), out_specs=_any_specs(n),
        scratch_shapes=[pltpu.SemaphoreType.DMA((n, 7)), pltpu.SemaphoreType.DMA((n, 7)), pltpu.SemaphoreType.DMA((n,))],
    )(*shards)


def _pair_exchange_call(grads, name):
    n = len(grads)

    def body(*refs):
        g_refs, r_refs = refs[:n], refs[n:2 * n]
        send_sems, recv_sems = refs[2 * n:]
        x, y, c = _mesh_pos()
        copies = []
        for a in range(n):
            for k in range(4):
                copies.append(pltpu.make_async_remote_copy(
                    src_ref=g_refs[a].at[2 * k + 1 - c], dst_ref=r_refs[a].at[k],
                    send_sem=send_sems.at[a, k], recv_sem=recv_sems.at[a, k],
                    device_id=(x, y, 1 - c), device_id_type=pl.DeviceIdType.MESH))
        for cp in copies:
            cp.start()
        for cp in copies:
            cp.wait_recv()
        for cp in copies:
            cp.wait_send()

    return pl.pallas_call(
        body, name=name,
        out_shape=[jax.ShapeDtypeStruct((4,) + g.shape[1:], g.dtype) for g in grads],
        in_specs=_any_specs(n), out_specs=_any_specs(n),
        scratch_shapes=[pltpu.SemaphoreType.DMA((n, 4)), pltpu.SemaphoreType.DMA((n, 4))],
    )(*grads)


def _chip_exchange_call(parts, name):
    n = len(parts)

    def body(*refs):
        p_refs, r_refs = refs[:n], refs[n:2 * n]
        send_sems, recv_sems = refs[2 * n:]
        x, y, c = _mesh_pos()
        chips = [(1 - x, y), (x, 1 - y), (1 - x, 1 - y)]
        copies = []
        for a in range(n):
            for j, chip in enumerate(chips):
                copies.append(pltpu.make_async_remote_copy(
                    src_ref=p_refs[a].at[2 * chip[0] + chip[1]], dst_ref=r_refs[a].at[j],
                    send_sem=send_sems.at[a, j], recv_sem=recv_sems.at[a, j],
                    device_id=(*chip, c), device_id_type=pl.DeviceIdType.MESH))
        for cp in copies:
            cp.start()
        for cp in copies:
            cp.wait_recv()
        for cp in copies:
            cp.wait_send()

    return pl.pallas_call(
        body, name=name,
        out_shape=[jax.ShapeDtypeStruct((3,) + p.shape[1:], p.dtype) for p in parts],
        in_specs=_any_specs(n), out_specs=_any_specs(n),
        scratch_shapes=[pltpu.SemaphoreType.DMA((n, 3)), pltpu.SemaphoreType.DMA((n, 3))],
    )(*parts)


def _pair_add_call(g, r1, core_chip, name):
    _, R, C = g.shape
    tr = _tile(R, max(8, (MIB // 2) // (C * 4)), 16)

    def body(cc_ref, g_ref, r_ref, p32_ref, p16_ref):
        s = g_ref[...] + r_ref[...]
        p16_ref[...] = s.astype(BF16)

        @pl.when(pl.program_id(1) == cc_ref[1])
        def _():
            p32_ref[...] = s

    grid_spec = pltpu.PrefetchScalarGridSpec(
        num_scalar_prefetch=1, grid=(R // tr, 4),
        in_specs=[pl.BlockSpec((None, tr, C), lambda r, k, cc: (2 * k + cc[0], r, 0)),
                  pl.BlockSpec((None, tr, C), lambda r, k, cc: (k, r, 0))],
        out_specs=[pl.BlockSpec((tr, C), lambda r, k, cc: (r, 0)),
                   pl.BlockSpec((None, tr, C), lambda r, k, cc: (k, r, 0))])
    return pl.pallas_call(
        body, name=name, grid_spec=grid_spec,
        out_shape=[jax.ShapeDtypeStruct((R, C), F32), jax.ShapeDtypeStruct((4, R, C), BF16)],
        compiler_params=_cp(2))(core_chip, g, r1)


def _adamw_call(w, m, v, addends, name):
    R, C = w.shape
    tr = _tile(R, max(8, (MIB // 2) // (C * 4)), 16)
    na = len(addends)
    c1 = 1.0 - ADAM_B1 ** ADAM_STEP
    c2 = 1.0 - ADAM_B2 ** ADAM_STEP

    def body(*refs):
        w_ref, m_ref, v_ref = refs[:3]
        a_refs = refs[3:3 + na]
        g_ref, d_ref, nm_ref, nv_ref = refs[3 + na:]
        g = a_refs[0][...].astype(F32)
        for a_ref in a_refs[1:]:
            g = g + a_ref[...].astype(F32)
        nm = ADAM_B1 * m_ref[...] + (1.0 - ADAM_B1) * g
        nv = ADAM_B2 * v_ref[...] + (1.0 - ADAM_B2) * (g * g)
        g_ref[...] = g
        nm_ref[...] = nm
        nv_ref[...] = nv
        d_ref[...] = -ADAM_LR * ((nm / c1) / (jnp.sqrt(nv / c2) + ADAM_EPS) + ADAM_WD * w_ref[...])

    flat = pl.BlockSpec((tr, C), lambda r: (r, 0))
    a_specs = [flat if k is None else pl.BlockSpec((None, tr, C), functools.partial(lambda r, kk: (kk, r, 0), kk=k))
               for _, k in addends]
    out = jax.ShapeDtypeStruct((R, C), F32)
    return pl.pallas_call(
        body, name=name, grid=(R // tr,),
        in_specs=[flat, flat, flat] + a_specs, out_specs=[flat] * 4, out_shape=[out] * 4,
        compiler_params=_cp(1))(w, m, v, *[a for a, _ in addends])


def _sum_call(parts, row0, rows, name):
    n = parts.shape[0]
    tr = _tile(math.gcd(row0, rows), 256, 8)
    b0 = row0 // tr

    def body(p_ref, o_ref):
        s = p_ref[0]
        for k in range(1, n):
            s = s + p_ref[k]
        o_ref[...] = s

    return pl.pallas_call(
        body, name=name, grid=(rows // tr,),
        in_specs=[pl.BlockSpec((n, tr, LANES), lambda r: (0, r + b0, 0))],
        out_specs=pl.BlockSpec((tr, LANES), lambda r: (r, 0)),
        out_shape=jax.ShapeDtypeStruct((rows, LANES), F32), compiler_params=_cp(1))(parts)


def _rmsnorm_call(x, g, name):
    S, D = x.shape
    tm = _tile(S, 512, 16)

    def body(x_ref, g_ref, o_ref):
        xv = x_ref[...]
        r = lax.rsqrt(_rows_mean(xv * xv) + RMS_EPS)
        o_ref[...] = (xv * r * g_ref[...]).astype(BF16)

    return pl.pallas_call(
        body, name=name, grid=(S // tm,),
        in_specs=[pl.BlockSpec((tm, D), lambda i: (i, 0)), pl.BlockSpec((1, D), lambda i: (0, 0))],
        out_specs=pl.BlockSpec((tm, D), lambda i: (i, 0)),
        out_shape=jax.ShapeDtypeStruct((S, D), BF16), compiler_params=_cp(1))(x, g)


def _mm_blocked_call(a, wg, out_dtype, halves, name):
    S, K = a.shape
    nb, _, bn = wg.shape
    tm = _tile(S, 1024, 16)
    tn = _tile(bn, 768, LANES)
    nsub = bn // tn
    J = nb * nsub

    def body(a_ref, w_ref, o_ref):
        o_ref[...] = jnp.dot(a_ref[...], w_ref[...], preferred_element_type=F32).astype(out_dtype)

    if halves:
        nh = J // 2
        out_spec = pl.BlockSpec((None, tm, tn), lambda i, j: (j // nh, i, j % nh))
        out_shape = jax.ShapeDtypeStruct((2, S, nb * bn // 2), out_dtype)
    else:
        out_spec = pl.BlockSpec((tm, tn), lambda i, j: (i, j))
        out_shape = jax.ShapeDtypeStruct((S, nb * bn), out_dtype)
    return pl.pallas_call(
        body, name=name, grid=(S // tm, J),
        in_specs=[pl.BlockSpec((tm, K), lambda i, j: (i, 0)),
                  pl.BlockSpec((None, K, tn), lambda i, j: (j // nsub, 0, j % nsub))],
        out_specs=out_spec, out_shape=out_shape, compiler_params=_cp(2))(a, wg)


def _mm_out_call(x, y, w, name):
    S, D = x.shape
    tm = _tile(S, 512, 16)

    def body(x_ref, y_ref, w_ref, o_ref):
        o_ref[...] = x_ref[...] + jnp.dot(y_ref[...], w_ref[...], preferred_element_type=F32)

    return pl.pallas_call(
        body, name=name, grid=(S // tm,),
        in_specs=[pl.BlockSpec((tm, D), lambda i: (i, 0)), pl.BlockSpec((tm, D), lambda i: (i, 0)),
                  pl.BlockSpec((D, D), lambda i: (0, 0))],
        out_specs=pl.BlockSpec((tm, D), lambda i: (i, 0)),
        out_shape=jax.ShapeDtypeStruct((S, D), F32), compiler_params=_cp(1))(x, y, w)


def _mm_nt_call(a, w, out_dtype, name):
    S, K = a.shape
    N = w.shape[0]
    tm = _tile(S, 1024, 16)
    tn = _tile(N, 768, LANES)

    def body(a_ref, w_ref, o_ref):
        o_ref[...] = lax.dot_general(a_ref[...], w_ref[...], _DN_NT, preferred_element_type=F32).astype(out_dtype)

    return pl.pallas_call(
        body, name=name, grid=(S // tm, N // tn),
        in_specs=[pl.BlockSpec((tm, K), lambda i, j: (i, 0)), pl.BlockSpec((tn, K), lambda i, j: (j, 0))],
        out_specs=pl.BlockSpec((tm, tn), lambda i, j: (i, j)),
        out_shape=jax.ShapeDtypeStruct((S, N), out_dtype), compiler_params=_cp(2))(a, w)


def _mm_down_loss_call(x2, f, w, final_g, target, name):
    S, D = x2.shape
    Fd = f.shape[1]
    tm = _tile(S, 512, 16)
    tk = _tile(Fd, 768, LANES)
    nk = Fd // tk

    def body(x_ref, f_ref, w_ref, g_ref, t_ref, dx_ref, dxb_ref, loss_ref, dg_ref, acc):
        i, k = pl.program_id(0), pl.program_id(1)

        @pl.when(jnp.logical_and(i == 0, k == 0))
        def _():
            loss_ref[...] = jnp.zeros_like(loss_ref)
            dg_ref[...] = jnp.zeros_like(dg_ref)

        @pl.when(k == 0)
        def _():
            acc[...] = jnp.zeros_like(acc)

        acc[...] += jnp.dot(f_ref[...], w_ref[...], preferred_element_type=F32)

        @pl.when(k == nk - 1)
        def _():
            x3 = x_ref[...] + acc[...]
            r = lax.rsqrt(_rows_mean(x3 * x3) + RMS_EPS)
            g = g_ref[...]
            xn = x3 * r
            diff = xn * g - t_ref[...]
            loss_ref[...] += 0.5 * jnp.sum(_rows_mean(diff * diff))
            dout = diff * (1.0 / D)
            dg_ref[...] += _col_sum(dout * xn)
            dyg = dout * g
            dx = r * (dyg - xn * _rows_mean(dyg * xn))
            dx_ref[...] = dx
            dxb_ref[...] = dx.astype(BF16)

    row = lambda i, k: (i, 0)
    return pl.pallas_call(
        body, name=name, grid=(S // tm, nk),
        in_specs=[pl.BlockSpec((tm, D), row), pl.BlockSpec((tm, tk), lambda i, k: (i, k)),
                  pl.BlockSpec((tk, D), lambda i, k: (k, 0)), pl.BlockSpec((1, D), lambda i, k: (0, 0)),
                  pl.BlockSpec((tm, D), row)],
        out_specs=[pl.BlockSpec((tm, D), row), pl.BlockSpec((tm, D), row),
                   pl.BlockSpec((8, LANES), lambda i, k: (0, 0)), pl.BlockSpec((1, D), lambda i, k: (0, 0))],
        out_shape=[jax.ShapeDtypeStruct((S, D), F32), jax.ShapeDtypeStruct((S, D), BF16),
                   jax.ShapeDtypeStruct((8, LANES), F32), jax.ShapeDtypeStruct((1, D), F32)],
        scratch_shapes=[pltpu.VMEM((tm, D), F32)], compiler_params=_cp(2, 56))(x2, f, w, final_g, target)


def _mm_dx_norm_call(a3, wg, resid, xin, g, name):
    na, S, Fa = a3.shape
    nb, D, bn = wg.shape
    tm = _tile(S, 512, 16)
    tk = _tile(bn, 768, LANES)
    nsub = bn // tk
    nka = Fa // tk
    nk = nb * nsub
    assert na * nka == nk

    def body(a_ref, w_ref, r_ref, x_ref, g_ref, dx_ref, dxb_ref, dg_ref, acc):
        i, k = pl.program_id(0), pl.program_id(1)

        @pl.when(jnp.logical_and(i == 0, k == 0))
        def _():
            dg_ref[...] = jnp.zeros_like(dg_ref)

        @pl.when(k == 0)
        def _():
            acc[...] = jnp.zeros_like(acc)

        acc[...] += lax.dot_general(a_ref[...], w_ref[...], _DN_NT, preferred_element_type=F32)

        @pl.when(k == nk - 1)
        def _():
            dh = acc[...]
            xv = x_ref[...]
            r = lax.rsqrt(_rows_mean(xv * xv) + RMS_EPS)
            xn = xv * r
            dg_ref[...] += _col_sum(dh * xn)
            dyg = dh * g_ref[...]
            dx = r_ref[...] + r * (dyg - xn * _rows_mean(dyg * xn))
            dx_ref[...] = dx
            dxb_ref[...] = dx.astype(BF16)

    row = lambda i, k: (i, 0)
    return pl.pallas_call(
        body, name=name, grid=(S // tm, nk),
        in_specs=[pl.BlockSpec((None, tm, tk), lambda i, k: (k // nka, i, k % nka)),
                  pl.BlockSpec((None, D, tk), lambda i, k: (k // nsub, 0, k % nsub)),
                  pl.BlockSpec((tm, D), row), pl.BlockSpec((tm, D), row), pl.BlockSpec((1, D), lambda i, k: (0, 0))],
        out_specs=[pl.BlockSpec((tm, D), row), pl.BlockSpec((tm, D), row), pl.BlockSpec((1, D), lambda i, k: (0, 0))],
        out_shape=[jax.ShapeDtypeStruct((S, D), F32), jax.ShapeDtypeStruct((S, D), BF16),
                   jax.ShapeDtypeStruct((1, D), F32)],
        scratch_shapes=[pltpu.VMEM((tm, D), F32)], compiler_params=_cp(2, 56))(a3, wg, resid, xin, g)


def _mm_tn_cols_call(a, b3, nb, bn, name):
    S, Ka = a.shape
    nh, _, Fb = b3.shape
    tm = _tile(S, 512, 16)
    tn = _tile(bn, 768, LANES)
    nsub = bn // tn
    njb = Fb // tn
    J = nb * nsub
    assert nh * njb == J

    def body(a_ref, b_ref, o_ref):
        @pl.when(pl.program_id(1) == 0)
        def _():
            o_ref[...] = jnp.zeros_like(o_ref)

        o_ref[...] += lax.dot_general(a_ref[...], b_ref[...], _DN_TN, preferred_element_type=F32)

    return pl.pallas_call(
        body, name=name, grid=(J, S // tm),
        in_specs=[pl.BlockSpec((tm, Ka), lambda j, i: (i, 0)),
                  pl.BlockSpec((None, tm, tn), lambda j, i: (j // njb, i, j % njb))],
        out_specs=pl.BlockSpec((None, Ka, tn), lambda j, i: (j // nsub, 0, j % nsub)),
        out_shape=jax.ShapeDtypeStruct((nb, Ka, bn), F32), compiler_params=_cp(2, 56))(a, b3)


def _mm_tn_rows_call(a, b, name):
    S, E = a.shape
    D = b.shape[1]
    tm = _tile(S, 512, 16)
    te = _tile(E, 768, LANES)

    def body(a_ref, b_ref, o_ref):
        @pl.when(pl.program_id(1) == 0)
        def _():
            o_ref[...] = jnp.zeros_like(o_ref)

        o_ref[...] += lax.dot_general(a_ref[...], b_ref[...], _DN_TN, preferred_element_type=F32)

    return pl.pallas_call(
        body, name=name, grid=(E // te, S // tm),
        in_specs=[pl.BlockSpec((tm, te), lambda j, i: (i, j)), pl.BlockSpec((tm, D), lambda j, i: (i, 0))],
        out_specs=pl.BlockSpec((te, D), lambda j, i: (j, 0)),
        out_shape=jax.ShapeDtypeStruct((E, D), F32), compiler_params=_cp(2, 56))(a, b)


def _ffn_tiles(S, Fd):
    return _tile(S, 512, 16), _tile(Fd, 768, LANES)


def _taps(cw_ref):
    return [cw_ref[k:k + 1, :] for k in range(cw_ref.shape[0])]


def _conv3(prev8, cur, taps):
    s1 = _shift_down(prev8, cur, 1)
    s2 = _shift_down(prev8, cur, 2)
    return taps[2] * cur + taps[1] * s1 + taps[0] * s2, s1, s2


def _ffn_act_call(up3, cw, cb, name):
    _, S, Fd = up3.shape
    tm, tc = _ffn_tiles(S, Fd)
    nj = Fd // tc
    hb = tm // 16

    def body(g_ref, v_ref, gp_ref, vp_ref, cwg_ref, cwv_ref, cbg_ref, cbv_ref, o_ref):
        first = pl.program_id(0) == 0
        keep = jnp.where(first, 0.0, 1.0)
        gp = gp_ref[...].astype(F32)[8:] * keep
        vp = vp_ref[...].astype(F32)[8:] * keep
        cg, _, _ = _conv3(gp, g_ref[...].astype(F32), _taps(cwg_ref))
        cv, _, _ = _conv3(vp, v_ref[...].astype(F32), _taps(cwv_ref))
        o_ref[...] = (_gelu(cg + cbg_ref[...]) * (cv + cbv_ref[...])).astype(BF16)

    prev = lambda i, j: (jnp.maximum(i * hb - 1, 0), j)
    return pl.pallas_call(
        body, name=name, grid=(S // tm, nj),
        in_specs=[pl.BlockSpec((None, tm, tc), lambda i, j: (0, i, j)), pl.BlockSpec((None, tm, tc), lambda i, j: (1, i, j)),
                  pl.BlockSpec((None, 16, tc), lambda i, j: (0,) + prev(i, j)),
                  pl.BlockSpec((None, 16, tc), lambda i, j: (1,) + prev(i, j)),
                  pl.BlockSpec((3, tc), lambda i, j: (0, j)), pl.BlockSpec((3, tc), lambda i, j: (0, j + nj)),
                  pl.BlockSpec((1, tc), lambda i, j: (0, j)), pl.BlockSpec((1, tc), lambda i, j: (0, j + nj))],
        out_specs=pl.BlockSpec((tm, tc), lambda i, j: (i, j)),
        out_shape=jax.ShapeDtypeStruct((S, Fd), BF16), compiler_params=_cp(2))(up3, up3, up3, up3, cw, cw, cb, cb)


def _ffn_act_bwd_call(up3, d_f, cw, cb, name):
    _, S, Fd = up3.shape
    tm, tc = _ffn_tiles(S, Fd)
    nj = Fd // tc
    ni = S // tm
    hb = tm // 16

    def body(g_ref, v_ref, gp_ref, vp_ref, gn_ref, vn_ref, df_ref, dfn_ref, cwg_ref, cwv_ref, cbg_ref, cbv_ref,
             dup_ref, dcwg_ref, dcwv_ref, dcbg_ref, dcbv_ref):
        i = pl.program_id(1)
        keep_prev = jnp.where(i == 0, 0.0, 1.0)
        keep_next = jnp.where(i == ni - 1, 0.0, 1.0)

        @pl.when(i == 0)
        def _():
            for r in (dcwg_ref, dcwv_ref, dcbg_ref, dcbv_ref):
                r[...] = jnp.zeros_like(r)

        df = jnp.concatenate([df_ref[...].astype(F32), dfn_ref[...].astype(F32)[:8] * keep_next], axis=0)

        def half(x_ref, xp_ref, xn_ref, cw_ref, cb_ref):
            cw_ = _taps(cw_ref)
            prev8 = xp_ref[...].astype(F32)[8:] * keep_prev
            ext = jnp.concatenate([x_ref[...].astype(F32), xn_ref[...].astype(F32)[:8]], axis=0)
            conv, s1, s2 = _conv3(prev8, ext, cw_)
            return cw_, conv + cb_ref[...], (ext, s1, s2)

        cwg, cg, gsh = half(g_ref, gp_ref, gn_ref, cwg_ref, cbg_ref)
        cwv, cv, vsh = half(v_ref, vp_ref, vn_ref, cwv_ref, cbv_ref)
        gel, dgel = _gelu_parts(cg)
        dg = df * cv * dgel
        dv = df * gel
        zeros8 = jnp.zeros((8, tc), F32)

        def back(d, cw_, shifts, dcw_ref, dcb_ref, slab):
            dup = cw_[2] * d + cw_[1] * _shift_up(d, zeros8, 1) + cw_[0] * _shift_up(d, zeros8, 2)
            dup_ref[slab] = dup[:tm].astype(BF16)
            dm = d[:tm]
            ext, s1, s2 = shifts
            dcw_ref[2:3, :] += _col_sum(ext[:tm] * dm)
            dcw_ref[1:2, :] += _col_sum(s1[:tm] * dm)
            dcw_ref[0:1, :] += _col_sum(s2[:tm] * dm)
            dcb_ref[...] += _col_sum(dm)

        back(dg, cwg, gsh, dcwg_ref, dcbg_ref, 0)
        back(dv, cwv, vsh, dcwv_ref, dcbv_ref, 1)

    prev = lambda j, i: (jnp.maximum(i * hb - 1, 0), j)
    nxt = lambda j, i: (jnp.minimum((i + 1) * hb, S // 16 - 1), j)
    main = lambda s: pl.BlockSpec((None, tm, tc), lambda j, i: (s, i, j))
    halo = lambda s, f: pl.BlockSpec((None, 16, tc), lambda j, i: (s,) + f(j, i))
    acc3 = pl.BlockSpec((3, tc), lambda j, i: (0, j))
    acc1 = pl.BlockSpec((1, tc), lambda j, i: (0, j))
    return pl.pallas_call(
        body, name=name, grid=(nj, ni),
        in_specs=[main(0), main(1), halo(0, prev), halo(1, prev), halo(0, nxt), halo(1, nxt),
                  pl.BlockSpec((tm, tc), lambda j, i: (i, j)), pl.BlockSpec((16, tc), nxt),
                  pl.BlockSpec((3, tc), lambda j, i: (0, j)), pl.BlockSpec((3, tc), lambda j, i: (0, j + nj)),
                  pl.BlockSpec((1, tc), lambda j, i: (0, j)), pl.BlockSpec((1, tc), lambda j, i: (0, j + nj))],
        out_specs=[pl.BlockSpec((2, tm, tc), lambda j, i: (0, i, j)), acc3, acc3, acc1, acc1],
        out_shape=[jax.ShapeDtypeStruct((2, S, Fd), BF16), jax.ShapeDtypeStruct((3, Fd), F32),
                   jax.ShapeDtypeStruct((3, Fd), F32), jax.ShapeDtypeStruct((1, Fd), F32),
                   jax.ShapeDtypeStruct((1, Fd), F32)],
        compiler_params=_cp(2, 56))(up3, up3, up3, up3, up3, up3, d_f, d_f, cw, cw, cb, cb)


def _gm_forward_tile(pv, vg, vb, ws_ref, bsb_ref, mbuf, H, nc):
    W = H * CHUNK
    z, dz = _gelu_parts(pv)
    u, v0 = z[:, :W], z[:, W:]
    xc = v0 - _rows_mean(v0)
    rs = lax.rsqrt(_rows_mean(xc * xc) + LN_EPS)
    vh = xc * rs
    vnb = (vh * vg + vb).astype(BF16)
    mask = lax.broadcasted_iota(jnp.int32, (CHUNK, CHUNK), 0) >= lax.broadcasted_iota(jnp.int32, (CHUNK, CHUNK), 1)
    for h in range(H):
        cs = slice(h * CHUNK, (h + 1) * CHUNK)
        wm = jnp.where(mask, ws_ref[h], 0.0).astype(BF16)
        vcat = jnp.concatenate([vnb[c * CHUNK:(c + 1) * CHUNK, cs] for c in range(nc)], axis=1)
        mix = jnp.dot(wm, vcat, preferred_element_type=F32)
        for c in range(nc):
            mbuf[c * CHUNK:(c + 1) * CHUNK, cs] = mix[:, c * CHUNK:(c + 1) * CHUNK] + bsb_ref[h]
    return dz, u, vh, rs, vnb, mask


def _gm_fwd_call(p, v_g, v_b, ws, bsb, out_g, name):
    S = p.shape[0]
    H = ws.shape[0]
    W = H * CHUNK
    tm = _tile(S, 256, CHUNK)
    nc = tm // CHUNK

    def body(p_ref, vg_ref, vb_ref, ws_ref, bsb_ref, og_ref, y_ref, mbuf):
        _, u, _, _, _, _ = _gm_forward_tile(p_ref[...], vg_ref[...], vb_ref[...], ws_ref, bsb_ref, mbuf, H, nc)
        yg = u * mbuf[...]
        r = lax.rsqrt(_rows_mean(yg * yg) + RMS_EPS)
        y_ref[...] = (yg * r * og_ref[...]).astype(BF16)

    vec = pl.BlockSpec((1, W), lambda i: (0, 0))
    mat = pl.BlockSpec((H, CHUNK, CHUNK), lambda i: (0, 0, 0))
    return pl.pallas_call(
        body, name=name, grid=(S // tm,),
        in_specs=[pl.BlockSpec((tm, 2 * W), lambda i: (i, 0)), vec, vec, mat, mat, vec],
        out_specs=pl.BlockSpec((tm, W), lambda i: (i, 0)),
        out_shape=jax.ShapeDtypeStruct((S, W), BF16),
        scratch_shapes=[pltpu.VMEM((tm, W), F32)], compiler_params=_cp(1))(p, v_g, v_b, ws, bsb, out_g)


def _gm_bwd_call(p, d_y, v_g, v_b, ws, bsb, out_g, name):
    S = p.shape[0]
    H = ws.shape[0]
    W = H * CHUNK
    tm = _tile(S, 256, CHUNK)
    nc = tm // CHUNK
    ni = S // tm

    def body(p_ref, dy_ref, vg_ref, vb_ref, ws_ref, bsb_ref, og_ref,
             dp_ref, dvg_ref, dvb_ref, dws_ref, dbs_ref, dog_ref, mbuf, dvbuf):
        i = pl.program_id(0)

        @pl.when(i == 0)
        def _():
            for r in (dvg_ref, dvb_ref, dws_ref, dbs_ref, dog_ref):
                r[...] = jnp.zeros_like(r)

        vg = vg_ref[...]
        dz, u, vh, rs, vnb, mask = _gm_forward_tile(p_ref[...], vg, vb_ref[...], ws_ref, bsb_ref, mbuf, H, nc)
        mixed = mbuf[...]
        yg = u * mixed
        r = lax.rsqrt(_rows_mean(yg * yg) + RMS_EPS)
        yn = yg * r
        dya = dy_ref[...]
        dog_ref[...] += _col_sum(dya * yn)
        dyg = dya * og_ref[...]
        dygm = r * (dyg - yn * _rows_mean(dyg * yn))
        du = dygm * mixed
        dmix = dygm * u
        dmb = dmix.astype(BF16)
        for h in range(H):
            cs = slice(h * CHUNK, (h + 1) * CHUNK)
            wm = jnp.where(mask, ws_ref[h], 0.0).astype(BF16)
            dcat = jnp.concatenate([dmb[c * CHUNK:(c + 1) * CHUNK, cs] for c in range(nc)], axis=1)
            vcat = jnp.concatenate([vnb[c * CHUNK:(c + 1) * CHUNK, cs] for c in range(nc)], axis=1)
            dvn = lax.dot_general(wm, dcat, _DN_TN, preferred_element_type=F32)
            dws_ref[h] += jnp.where(mask, lax.dot_general(dcat, vcat, _DN_NT, preferred_element_type=F32), 0.0)
            dbs = dmix[0:CHUNK, cs]
            for c in range(1, nc):
                dbs = dbs + dmix[c * CHUNK:(c + 1) * CHUNK, cs]
            dbs_ref[h] += dbs
            for c in range(nc):
                dvbuf[c * CHUNK:(c + 1) * CHUNK, cs] = dvn[:, c * CHUNK:(c + 1) * CHUNK]
        dvn_all = dvbuf[...]
        dvg_ref[...] += _col_sum(dvn_all * vh)
        dvb_ref[...] += _col_sum(dvn_all)
        dvh = dvn_all * vg
        dv0 = rs * (dvh - _rows_mean(dvh) - vh * _rows_mean(dvh * vh))
        dp_ref[...] = (jnp.concatenate([du, dv0], axis=1) * dz).astype(BF16)

        @pl.when(i == ni - 1)
        def _():
            for h in range(H):
                dbs_ref[h] = jnp.broadcast_to(jnp.sum(dbs_ref[h], axis=1, keepdims=True), (CHUNK, CHUNK))

    vec = pl.BlockSpec((1, W), lambda i: (0, 0))
    mat = pl.BlockSpec((H, CHUNK, CHUNK), lambda i: (0, 0, 0))
    vshape = jax.ShapeDtypeStruct((1, W), F32)
    mshape = jax.ShapeDtypeStruct((H, CHUNK, CHUNK), F32)
    return pl.pallas_call(
        body, name=name, grid=(ni,),
        in_specs=[pl.BlockSpec((tm, 2 * W), lambda i: (i, 0)), pl.BlockSpec((tm, W), lambda i: (i, 0)),
                  vec, vec, mat, mat, vec],
        out_specs=[pl.BlockSpec((tm, 2 * W), lambda i: (i, 0)), vec, vec, mat, mat, vec],
        out_shape=[jax.ShapeDtypeStruct((S, 2 * W), BF16), vshape, vshape, mshape, mshape, vshape],
        scratch_shapes=[pltpu.VMEM((tm, W), F32), pltpu.VMEM((tm, W), F32)],
        compiler_params=_cp(1))(p, d_y, v_g, v_b, ws, bsb, out_g)


def _lru_gates(prev8, xl, cw, cb, wa_ref, ba, wx_ref, bx, lam, H):
    sh = [_shift_down(prev8, xl, k) for k in range(4)]
    xr = cw[3] * sh[0] + cw[2] * sh[1] + cw[1] * sh[2] + cw[0] * sh[3] + cb
    xrb = xr.astype(BF16)
    rp, ip = [], []
    for h in range(H):
        cs = slice(h * CHUNK, (h + 1) * CHUNK)
        rp.append(jnp.dot(xrb[:, cs], wa_ref[h].astype(BF16), preferred_element_type=F32))
        ip.append(jnp.dot(xrb[:, cs], wx_ref[h].astype(BF16), preferred_element_type=F32))
    r = _sigmoid(jnp.concatenate(rp, axis=1) + ba)
    ig = _sigmoid(jnp.concatenate(ip, axis=1) + bx)
    sp = _softplus(-lam)
    la = (-LRU_C) * r * sp
    a = jnp.exp(la)
    mult = jnp.sqrt(jnp.maximum(_neg_expm1(2.0 * la), 0.0))
    return xr, xrb, r, ig, sp, a, mult, sh


def _lru_fwd_call(p, cw, cb, wa, ba, wx, bx, lam, out_g, name):
    S = p.shape[0]
    H = wa.shape[0]
    W = H * CHUNK
    tm = _tile(S, 256, 16)
    ng = tm // 8

    def body(pg_ref, px_ref, cw_ref, cb_ref, wa_ref, ba_ref, wx_ref, bx_ref, lam_ref, og_ref,
             y_ref, h_ref, xprev, hcar, abuf, bbuf):
        @pl.when(pl.program_id(0) == 0)
        def _():
            xprev[...] = jnp.zeros_like(xprev)
            hcar[...] = jnp.zeros_like(hcar)

        xl = px_ref[...]
        xr, _, _, ig, _, a, mult, _ = _lru_gates(xprev[...], xl, _taps(cw_ref), cb_ref[...], wa_ref, ba_ref[...],
                                                 wx_ref, bx_ref[...], lam_ref[...], H)
        xprev[...] = xl[tm - 8:]
        b = mult * (ig * xr)
        sub = lax.broadcasted_iota(jnp.int32, (tm, W), 0) & 7
        for d in (1, 2, 4):
            m = sub >= d
            a_s = jnp.where(m, pltpu.roll(a, d, 0), 1.0)
            b_s = jnp.where(m, pltpu.roll(b, d, 0), 0.0)
            b = a * b_s + b
            a = a * a_s
        abuf[...] = a
        bbuf[...] = b

        def step(g, carry):
            r0 = pl.multiple_of(g * 8, 8)
            h_ref[pl.ds(r0, 8), :] = abuf[pl.ds(r0, 8), :] * carry + bbuf[pl.ds(r0, 8), :]
            return jnp.broadcast_to(h_ref[pl.ds(r0 + 7, 1), :], (8, W))

        hcar[...] = lax.fori_loop(0, ng, step, hcar[...])
        yl = h_ref[...] * _gelu(pg_ref[...])
        r = lax.rsqrt(_rows_mean(yl * yl) + RMS_EPS)
        y_ref[...] = (yl * r * og_ref[...]).astype(BF16)

    vec = pl.BlockSpec((1, W), lambda i: (0, 0))
    mat = pl.BlockSpec((H, CHUNK, CHUNK), lambda i: (0, 0, 0))
    return pl.pallas_call(
        body, name=name, grid=(S // tm,),
        in_specs=[pl.BlockSpec((tm, W), lambda i: (i, 2)), pl.BlockSpec((tm, W), lambda i: (i, 3)),
                  pl.BlockSpec((4, W), lambda i: (0, 0)), vec, mat, vec, mat, vec, vec, vec],
        out_specs=[pl.BlockSpec((tm, W), lambda i: (i, 0)), pl.BlockSpec((tm, W), lambda i: (i, 0))],
        out_shape=[jax.ShapeDtypeStruct((S, W), BF16), jax.ShapeDtypeStruct((S, W), F32)],
        scratch_shapes=[pltpu.VMEM((8, W), F32), pltpu.VMEM((8, W), F32), pltpu.VMEM((tm, W), F32),
                        pltpu.VMEM((tm, W), F32)],
        compiler_params=_cp(1))(p, p, cw, cb, wa, ba, wx, bx, lam, out_g)


def _lru_bwd_call(p, hs, d_y, cw, cb, wa, ba, wx, bx, lam, out_g, name):
    S = p.shape[0]
    H = wa.shape[0]
    W = H * CHUNK
    tm = _tile(S, 256, 16)
    ng = tm // 8
    ni = S // tm
    hb = tm // 8

    def body(pg_ref, px_ref, pxp_ref, h_ref, hp_ref, dy_ref, cw_ref, cb_ref, wa_ref, ba_ref, wx_ref, bx_ref,
             lam_ref, og_ref,
             dp_ref, dcw_ref, dcb_ref, dwa_ref, dba_ref, dwx_ref, dbx_ref, dlam_ref, dog_ref,
             a_next, e_next, dxr_next, abuf, bbuf, ebuf):
        i = pl.program_id(0)
        ri = ni - 1 - i

        @pl.when(i == 0)
        def _():
            for r in (dcw_ref, dcb_ref, dwa_ref, dba_ref, dwx_ref, dbx_ref, dlam_ref, dog_ref,
                      a_next, e_next, dxr_next):
                r[...] = jnp.zeros_like(r)

        keep_prev = jnp.where(ri == 0, 0.0, 1.0)
        cw_ = _taps(cw_ref)
        lam_ = lam_ref[...]
        xl = px_ref[...]
        xr, xrb, r, ig, sp, a, mult, sh = _lru_gates(pxp_ref[...] * keep_prev, xl, cw_, cb_ref[...], wa_ref,
                                                     ba_ref[...], wx_ref, bx_ref[...], lam_, H)
        gg, dgg = _gelu_parts(pg_ref[...])
        hv = h_ref[...]
        yl = hv * gg
        rr = lax.rsqrt(_rows_mean(yl * yl) + RMS_EPS)
        yn = yl * rr
        dyb = dy_ref[...]
        dog_ref[...] += _col_sum(dyb * yn)
        dyg = dyb * og_ref[...]
        dyl = rr * (dyg - yn * _rows_mean(dyg * yn))
        dh = dyl * gg
        dgl = dyl * hv * dgg

        an = _shift_up(a, a_next[...], 1)
        eb = dh
        sub = lax.broadcasted_iota(jnp.int32, (tm, W), 0) & 7
        for d in (1, 2, 4):
            m = sub < 8 - d
            a_s = jnp.where(m, pltpu.roll(an, tm - d, 0), 1.0)
            e_s = jnp.where(m, pltpu.roll(eb, tm - d, 0), 0.0)
            eb = an * e_s + eb
            an = an * a_s
        abuf[...] = an
        bbuf[...] = eb

        def step(g, carry):
            r0 = pl.multiple_of((ng - 1 - g) * 8, 8)
            ebuf[pl.ds(r0, 8), :] = abuf[pl.ds(r0, 8), :] * carry + bbuf[pl.ds(r0, 8), :]
            return jnp.broadcast_to(ebuf[pl.ds(r0, 1), :], (8, W))

        lax.fori_loop(0, ng, step, jnp.broadcast_to(e_next[0:1, :], (8, W)))
        e = ebuf[...]
        a_next[...] = a[0:8]
        e_next[...] = e[0:8]

        hm1 = _shift_down(hp_ref[...] * keep_prev, hv, 1)
        da = e * hm1
        dmult = e * ig * xr
        di = e * mult * xr
        dxr = e * mult * ig
        dla = da * a - dmult * (a * a) / mult
        dr = dla * ((-LRU_C) * sp)
        dlam_ref[...] += _col_sum(dla * ((-LRU_C) * r))
        dpr = dr * r * (1.0 - r)
        dpi = di * ig * (1.0 - ig)
        dba_ref[...] += _col_sum(dpr)
        dbx_ref[...] += _col_sum(dpi)
        dprb = dpr.astype(BF16)
        dpib = dpi.astype(BF16)
        back = []
        for h in range(H):
            cs = slice(h * CHUNK, (h + 1) * CHUNK)
            wab = wa_ref[h].astype(BF16)
            wxb = wx_ref[h].astype(BF16)
            back.append(lax.dot_general(dprb[:, cs], wab, _DN_NT, preferred_element_type=F32)
                        + lax.dot_general(dpib[:, cs], wxb, _DN_NT, preferred_element_type=F32))
            dwa_ref[h] += lax.dot_general(xrb[:, cs], dprb[:, cs], _DN_TN, preferred_element_type=F32)
            dwx_ref[h] += lax.dot_general(xrb[:, cs], dpib[:, cs], _DN_TN, preferred_element_type=F32)
        dxr = dxr + jnp.concatenate(back, axis=1)

        nxt = dxr_next[...]
        dxl = (cw_[3] * dxr + cw_[2] * _shift_up(dxr, nxt, 1) + cw_[1] * _shift_up(dxr, nxt, 2)
               + cw_[0] * _shift_up(dxr, nxt, 3))
        dxr_next[...] = dxr[0:8]
        for k in range(4):
            dcw_ref[k:k + 1, :] += _col_sum(sh[3 - k] * dxr)
        dcb_ref[...] += _col_sum(dxr)
        dp_ref[...] = jnp.concatenate([dgl, dxl], axis=1).astype(BF16)

        @pl.when(i == ni - 1)
        def _():
            dlam_ref[...] = -dlam_ref[...] * _sigmoid(-lam_)

    vec = pl.BlockSpec((1, W), lambda i: (0, 0))
    mat = pl.BlockSpec((H, CHUNK, CHUNK), lambda i: (0, 0, 0))
    rev = lambda i: ni - 1 - i
    prev = lambda i: jnp.maximum(rev(i) * hb - 1, 0)
    vshape = jax.ShapeDtypeStruct((1, W), F32)
    mshape = jax.ShapeDtypeStruct((H, CHUNK, CHUNK), F32)
    tile = lambda: pltpu.VMEM((tm, W), F32)
    car = lambda: pltpu.VMEM((8, W), F32)
    return pl.pallas_call(
        body, name=name, grid=(ni,),
        in_specs=[pl.BlockSpec((tm, W), lambda i: (rev(i), 2)), pl.BlockSpec((tm, W), lambda i: (rev(i), 3)),
                  pl.BlockSpec((8, W), lambda i: (prev(i), 3)),
                  pl.BlockSpec((tm, W), lambda i: (rev(i), 0)), pl.BlockSpec((8, W), lambda i: (prev(i), 0)),
                  pl.BlockSpec((tm, W), lambda i: (rev(i), 1)),
                  pl.BlockSpec((4, W), lambda i: (0, 0)), vec, mat, vec, mat, vec, vec, vec],
        out_specs=[pl.BlockSpec((tm, 2 * W), lambda i: (rev(i), 0)), pl.BlockSpec((4, W), lambda i: (0, 0)), vec,
                   mat, vec, mat, vec, vec, vec],
        out_shape=[jax.ShapeDtypeStruct((S, 2 * W), BF16), jax.ShapeDtypeStruct((4, W), F32), vshape,
                   mshape, vshape, mshape, vshape, vshape, vshape],
        scratch_shapes=[car(), car(), car(), tile(), tile(), tile()],
        compiler_params=_cp(1, 56))(p, p, p, hs, hs, d_y, cw, cb, wa, ba, wx, bx, lam, out_g)


def _rows128(a):
    return a.reshape(-1, LANES).astype(F32)


def _pack(arrays, pad_to=256):
    flat = jnp.concatenate([_rows128(a) for a in arrays], axis=0)
    pad = (-flat.shape[0]) % pad_to
    if pad:
        flat = jnp.concatenate([flat, jnp.zeros((pad, LANES), F32)], axis=0)
    return flat


def _unpack(flat, shapes):
    out, r = [], 0
    for s in shapes:
        n = 1
        for d in s:
            n *= d
        out.append(flat[r:r + n // LANES].reshape(s))
        r += n // LANES
    return out


def kernel(x, norm1_g, w_in, gm_v_g, gm_v_b, gm_ws, gm_bs, lru_conv_w, lru_conv_b, lru_wa, lru_ba, lru_wx, lru_bx, lru_lambda, gm_out_g, lru_out_g, w_out, norm2_g, ffn_w_up, ffn_conv_w, ffn_conv_b, ffn_w_down, final_g, loss_target, m_norm1_g, m_w_in, m_gm_v_g, m_gm_v_b, m_gm_ws, m_gm_bs, m_lru_conv_w, m_lru_conv_b, m_lru_wa, m_lru_ba, m_lru_wx, m_lru_bx, m_lru_lambda, m_gm_out_g, m_lru_out_g, m_w_out, m_norm2_g, m_ffn_w_up, m_ffn_conv_w, m_ffn_conv_b, m_ffn_w_down, m_final_g, v_norm1_g, v_w_in, v_gm_v_g, v_gm_v_b, v_gm_ws, v_gm_bs, v_lru_conv_w, v_lru_conv_b, v_lru_wa, v_lru_ba, v_lru_wx, v_lru_bx, v_lru_lambda, v_gm_out_g, v_lru_out_g, v_w_out, v_norm2_g, v_ffn_w_up, v_ffn_conv_w, v_ffn_conv_b, v_ffn_w_down, v_final_g):
    wts = dict(norm1_g=norm1_g, w_in=w_in, gm_v_g=gm_v_g, gm_v_b=gm_v_b, gm_ws=gm_ws, gm_bs=gm_bs,
               lru_conv_w=lru_conv_w, lru_conv_b=lru_conv_b, lru_wa=lru_wa, lru_ba=lru_ba, lru_wx=lru_wx,
               lru_bx=lru_bx, lru_lambda=lru_lambda, gm_out_g=gm_out_g, lru_out_g=lru_out_g, w_out=w_out,
               norm2_g=norm2_g, ffn_w_up=ffn_w_up, ffn_conv_w=ffn_conv_w, ffn_conv_b=ffn_conv_b,
               ffn_w_down=ffn_w_down, final_g=final_g)
    mom = dict(norm1_g=m_norm1_g, w_in=m_w_in, gm_v_g=m_gm_v_g, gm_v_b=m_gm_v_b, gm_ws=m_gm_ws, gm_bs=m_gm_bs,
               lru_conv_w=m_lru_conv_w, lru_conv_b=m_lru_conv_b, lru_wa=m_lru_wa, lru_ba=m_lru_ba, lru_wx=m_lru_wx,
               lru_bx=m_lru_bx, lru_lambda=m_lru_lambda, gm_out_g=m_gm_out_g, lru_out_g=m_lru_out_g, w_out=m_w_out,
               norm2_g=m_norm2_g, ffn_w_up=m_ffn_w_up, ffn_conv_w=m_ffn_conv_w, ffn_conv_b=m_ffn_conv_b,
               ffn_w_down=m_ffn_w_down, final_g=m_final_g)
    var = dict(norm1_g=v_norm1_g, w_in=v_w_in, gm_v_g=v_gm_v_g, gm_v_b=v_gm_v_b, gm_ws=v_gm_ws, gm_bs=v_gm_bs,
               lru_conv_w=v_lru_conv_w, lru_conv_b=v_lru_conv_b, lru_wa=v_lru_wa, lru_ba=v_lru_ba, lru_wx=v_lru_wx,
               lru_bx=v_lru_bx, lru_lambda=v_lru_lambda, gm_out_g=v_gm_out_g, lru_out_g=v_lru_out_g, w_out=v_w_out,
               norm2_g=v_norm2_g, ffn_w_up=v_ffn_w_up, ffn_conv_w=v_ffn_conv_w, ffn_conv_b=v_ffn_conv_b,
               ffn_w_down=v_ffn_w_down, final_g=v_final_g)

    xi, yi, ci = lax.axis_index("x"), lax.axis_index("y"), lax.axis_index("c")
    chip = 2 * xi + yi
    dev = 2 * chip + ci
    core_chip = jnp.stack([ci, chip]).astype(jnp.int32)

    xs = x[0]
    tgt = loss_target[0]
    S, D = xs.shape
    H = gm_ws.shape[1]
    W = H * CHUNK
    Fd = ffn_w_down.shape[1] * N_DEV
    lcw_cols = lru_conv_w.shape[2]
    fcw_cols = ffn_conv_w.shape[2]

    conv_pack = _pack([lru_conv_w[0], ffn_conv_w[0]], pad_to=8)
    win_g, wout_g, wup_g, wdown_g, conv_g = _all_gather_call(
        [w_in[0].astype(BF16), w_out[0].astype(BF16), ffn_w_up[0].astype(BF16), ffn_w_down[0].astype(BF16), conv_pack],
        "gather_weights")
    wout_full = wout_g.reshape(D, D)
    wdown_full = wdown_g.reshape(Fd, D)
    n_l = 4 * lcw_cols // LANES
    n_f = 3 * fcw_cols // LANES
    lcw = conv_g[:, :n_l].reshape(N_DEV, 4, lcw_cols).transpose(1, 0, 2).reshape(4, N_DEV * lcw_cols)
    fcw = conv_g[:, n_l:n_l + n_f].reshape(N_DEV, 3, fcw_cols).transpose(1, 0, 2).reshape(3, N_DEV * fcw_cols)

    vgm_g, vgm_b = gm_v_g, gm_v_b
    ws, wa, wx = gm_ws[0], lru_wa[0], lru_wx[0]
    bsb = jnp.broadcast_to(gm_bs[0][:, :, None], (H, CHUNK, CHUNK))
    ba, bx = lru_ba.reshape(1, W), lru_bx.reshape(1, W)
    fcb = ffn_conv_b
    fing = final_g.reshape(1, D)

    h1 = _rmsnorm_call(xs, norm1_g, "norm1")
    p = _mm_blocked_call(h1, win_g, F32, False, "in_proj")
    ya = _gm_fwd_call(p, vgm_g, vgm_b, ws, bsb, gm_out_g, "gmlp_fwd")
    yb, hs = _lru_fwd_call(p, lcw, lru_conv_b, wa, ba, wx, bx, lru_lambda, lru_out_g, "lru_fwd")
    y = jnp.concatenate([ya, yb], axis=1)
    x2 = _mm_out_call(xs, y, wout_full, "out_proj")
    h2 = _rmsnorm_call(x2, norm2_g, "norm2")
    up3 = _mm_blocked_call(h2, wup_g, BF16, True, "ffn_up")
    f = _ffn_act_call(up3, fcw, fcb, "ffn_act")
    dx3, dx3b, loss_acc, d_final = _mm_down_loss_call(x2, f, wdown_full, fing, tgt, "ffn_down_loss")

    d_f = _mm_nt_call(dx3b, wdown_full, BF16, "ffn_down_dx")
    d_up3, dfcw_g, dfcw_v, dfcb_g, dfcb_v = _ffn_act_bwd_call(up3, d_f, fcw, fcb, "ffn_act_bwd")
    dx2, dx2b, d_norm2 = _mm_dx_norm_call(d_up3, wup_g, dx3, x2, norm2_g, "ffn_up_dx")
    g_wdown = _mm_tn_rows_call(f, dx3b, "ffn_down_dw")
    g_wup = _mm_tn_cols_call(h2, d_up3, N_DEV, ffn_w_up.shape[2], "ffn_up_dw")
    d_y = _mm_nt_call(dx2b, wout_full, F32, "out_proj_dx")
    g_wout = _mm_tn_rows_call(y, dx2b, "out_proj_dw")
    dp_gm, d_vg, d_vb, d_ws, d_bs, d_gog = _gm_bwd_call(p, d_y, vgm_g, vgm_b, ws, bsb, gm_out_g, "gmlp_bwd")
    dp_lru, d_lcw, d_lcb, d_wa, d_ba, d_wx, d_bx, d_lam, d_log = _lru_bwd_call(
        p, hs, d_y, lcw, lru_conv_b, wa, ba, wx, bx, lru_lambda, lru_out_g, "lru_bwd")
    d_p = jnp.concatenate([dp_gm, dp_lru], axis=1)[None]
    grad_x, _, d_norm1 = _mm_dx_norm_call(d_p, win_g, dx2, xs, norm1_g, "in_proj_dx")
    g_win = _mm_tn_cols_call(h1, d_p, N_DEV, w_in.shape[2], "in_proj_dw")

    big_g = dict(w_in=g_win, w_out=g_wout.reshape((N_DEV,) + w_out.shape[1:]), ffn_w_up=g_wup,
                 ffn_w_down=g_wdown.reshape((N_DEV,) + ffn_w_down.shape[1:]))
    from_sib = _pair_exchange_call([big_g[n] for n in BIG], "grads_pair_exchange")
    own32, chip16 = {}, []
    for n, r1 in zip(BIG, from_sib):
        own32[n], p16 = _pair_add_call(big_g[n], r1, core_chip, "pair_add_" + n)
        chip16.append(p16)
    from_chips = _chip_exchange_call(chip16, "grads_chip_exchange")
    res = {}
    for n, r2 in zip(BIG, from_chips):
        res[n] = _adamw_call(wts[n][0], mom[n][0], var[n][0], [(own32[n], None), (r2, 0), (r2, 1), (r2, 2)],
                             "adamw_" + n)

    small_g = dict(norm1_g=d_norm1, gm_v_g=d_vg, gm_v_b=d_vb, gm_ws=d_ws, gm_bs=d_bs[:, :, 0], lru_conv_b=d_lcb,
                   lru_wa=d_wa, lru_ba=d_ba, lru_wx=d_wx, lru_bx=d_bx, lru_lambda=d_lam, gm_out_g=d_gog,
                   lru_out_g=d_log, norm2_g=d_norm2,
                   ffn_conv_b=jnp.concatenate([dfcb_g, dfcb_v], axis=1), final_g=d_final)
    rep = _pack([small_g[n] for n in SMALL])
    conv_part = _pack([d_lcw, jnp.concatenate([dfcw_g, dfcw_v], axis=1)], pad_to=8)
    n_rep, n_conv = rep.shape[0], conv_part.shape[0]
    (parts,) = _all_gather_call([jnp.concatenate([rep, conv_part], axis=0)], "gather_small_grads")
    g_rep, d_rep, m_rep, v_rep = _adamw_call(
        _pack([wts[n] for n in SMALL]), _pack([mom[n] for n in SMALL]), _pack([var[n] for n in SMALL]),
        [(parts, k) for k in range(N_DEV)], "adamw_small")
    shapes = [wts[n].shape for n in SMALL]
    for n, g_, d_, m_, v_ in zip(SMALL, _unpack(g_rep, shapes), _unpack(d_rep, shapes), _unpack(m_rep, shapes),
                                 _unpack(v_rep, shapes)):
        res[n] = (g_, d_, m_, v_)
    conv_sum = _sum_call(parts, n_rep, n_conv, "sum_conv_grads")
    g_lcw = conv_sum[:4 * W // LANES].reshape(4, W)
    g_fcw = conv_sum[4 * W // LANES:4 * W // LANES + 6 * Fd // LANES].reshape(3, 2 * Fd)
    for n, full in (("lru_conv_w", g_lcw), ("ffn_conv_w", g_fcw)):
        cols = wts[n].shape[2]
        mine = lax.dynamic_slice_in_dim(full, dev * cols, cols, axis=1)
        res[n] = _adamw_call(wts[n][0], mom[n][0], var[n][0], [(mine, None)], "adamw_" + n)

    loss = lax.psum(loss_acc[0, 0], ("x", "y", "c"))
    outs = [[], [], [], []]
    for n in WEIGHTS:
        for k in range(4):
            outs[k].append(res[n][k].reshape(wts[n].shape))
    return (loss, grad_x[None], *outs[0], *outs[1], *outs[2], *outs[3])
```

```python
import functools
import math

import jax
import jax.numpy as jnp
from jax import lax
from jax.experimental import pallas as pl
from jax.experimental.pallas import tpu as pltpu

F32 = jnp.float32
BF16 = jnp.bfloat16

RMS_EPS = 1e-6
LN_EPS = 1e-5
LRU_C = 8.0
CHUNK = 128
ADAM_LR = 0.001
ADAM_B1 = 0.9
ADAM_B2 = 0.999
ADAM_EPS = 1e-08
ADAM_WD = 0.01
ADAM_STEP = 10
N_DEV = 8
LANES = 128
MIB = 1024 * 1024

WEIGHTS = ['norm1_g', 'w_in', 'gm_v_g', 'gm_v_b', 'gm_ws', 'gm_bs', 'lru_conv_w', 'lru_conv_b', 'lru_wa', 'lru_ba',
           'lru_wx', 'lru_bx', 'lru_lambda', 'gm_out_g', 'lru_out_g', 'w_out', 'norm2_g', 'ffn_w_up', 'ffn_conv_w',
           'ffn_conv_b', 'ffn_w_down', 'final_g']
BIG = ['w_in', 'w_out', 'ffn_w_up', 'ffn_w_down']
CONV = ['lru_conv_w', 'ffn_conv_w']
SMALL = [n for n in WEIGHTS if n not in BIG and n not in CONV]

_DN_NT = (((1,), (1,)), ((), ()))
_DN_TN = (((0,), (0,)), ((), ()))
_GELU_C = 0.7978845608028654


def _cp(n_axes, vmem_mib=48):
    return pltpu.CompilerParams(dimension_semantics=("arbitrary",) * n_axes, vmem_limit_bytes=vmem_mib * MIB)


def _tile(n, pref, mult=8):
    t = min(pref, n)
    t -= t % mult
    while t >= mult:
        if n % t == 0:
            return t
        t -= mult
    return n


def _gelu(z):
    return 0.5 * z * (1.0 + jnp.tanh(_GELU_C * z * (1.0 + 0.044715 * z * z)))


def _gelu_parts(z):
    z2 = z * z
    t = jnp.tanh(_GELU_C * z * (1.0 + 0.044715 * z2))
    g = 0.5 * z * (1.0 + t)
    dg = 0.5 * (1.0 + t) + 0.5 * z * (1.0 - t * t) * (_GELU_C * (1.0 + 0.134145 * z2))
    return g, dg


def _sigmoid(z):
    return 1.0 / (1.0 + jnp.exp(-z))


def _softplus(z):
    t = jnp.exp(-jnp.abs(z))
    u = 1.0 + t
    log1p = jnp.where(u == 1.0, t, jnp.log(u) * t / (u - 1.0))
    return jnp.maximum(z, 0.0) + log1p


def _neg_expm1(y):
    e = jnp.exp(y)
    small = jnp.where(e == 1.0, y, (e - 1.0) * y / jnp.log(e))
    return -jnp.where(y < -0.5, e - 1.0, small)


def _rows_mean(v):
    return jnp.mean(v, axis=-1, keepdims=True)


def _col_sum(v):
    return jnp.sum(v, axis=0, keepdims=True)


def _shift_down(prev8, cur, k):
    if k == 0:
        return cur
    z = jnp.concatenate([prev8, cur], axis=0)
    return pltpu.roll(z, k, 0)[8:]


def _shift_up(cur, next8, k):
    if k == 0:
        return cur
    n = cur.shape[0]
    z = jnp.concatenate([cur, next8], axis=0)
    return pltpu.roll(z, n + 8 - k, 0)[:n]


def _mesh_pos():
    return lax.axis_index("x"), lax.axis_index("y"), lax.axis_index("c")


def _any_specs(n):
    return [pl.BlockSpec(memory_space=pl.ANY)] * n


def _pallas(body, n_in, deps, **kw):
    nd = len(deps)
    if not nd:
        return pl.pallas_call(body, **kw)

    def ordered(*refs):
        body(*refs[:n_in], *refs[n_in + nd:])

    kw["in_specs"] = list(kw["in_specs"]) + _any_specs(nd)
    return pl.pallas_call(ordered, **kw)


_HBM = pl.BlockSpec(memory_space=pltpu.HBM)
_SEM = pl.BlockSpec(memory_space=pltpu.SEMAPHORE)
_EFFECT = pltpu.SideEffectType.DATAFLOW_SIDE_EFFECTING


class _InFlight:
    def __init__(self, sems, bufs, token, n_src, n_copies, make_copies, name):
        self.sems, self.bufs, self.token = sems, bufs, token
        self.n_src, self.n_copies, self.make_copies, self.name = n_src, n_copies, make_copies, name


def _exchange_start(srcs, lands, n_copies, make_copies, name, after=()):
    bufs = list(srcs) + list(lands)
    nb, na = len(bufs), len(after)
    ns = len(srcs)

    def body(*refs):
        b_refs = refs[:nb]
        outs = refs[nb + na:]
        send, recv = outs[:n_copies], outs[n_copies:2 * n_copies]
        token = outs[-1]
        for cp in make_copies(b_refs[:ns], b_refs[ns:], send, recv):
            cp.start()
        token[...] = jnp.zeros_like(token)

    out = pl.pallas_call(
        body, name=name,
        out_shape=[pltpu.SemaphoreType.DMA(())] * (2 * n_copies) + [pltpu.HBM(b.shape, b.dtype) for b in bufs]
        + [jax.ShapeDtypeStruct((8, LANES), F32)],
        in_specs=[_HBM] * nb + _any_specs(na),
        out_specs=[_SEM] * (2 * n_copies) + [_HBM] * nb + [pl.BlockSpec(memory_space=pltpu.VMEM)],
        input_output_aliases={i: 2 * n_copies + i for i in range(nb)},
        compiler_params=pltpu.CompilerParams(has_side_effects=_EFFECT),
    )(*[pltpu.with_memory_space_constraint(b, pltpu.HBM) for b in bufs], *after)
    return _InFlight(out[:2 * n_copies], out[2 * n_copies:2 * n_copies + nb], out[-1], ns, n_copies, make_copies, name)


def _exchange_wait(fl, after=()):
    nb, na, nc, ns = len(fl.bufs), len(after), fl.n_copies, fl.n_src

    def body(*refs):
        b_refs = refs[:nb]
        sems = refs[nb:nb + 2 * nc]
        copies = fl.make_copies(b_refs[:ns], b_refs[ns:], sems[:nc], sems[nc:])
        for cp in copies:
            cp.wait_send()
        for cp in copies:
            cp.wait_recv()

    out = pl.pallas_call(
        body, name=fl.name + "_wait",
        out_shape=[pltpu.HBM(b.shape, b.dtype) for b in fl.bufs],
        in_specs=[_HBM] * nb + [_SEM] * (2 * nc) + _any_specs(na),
        out_specs=[_HBM] * nb,
        input_output_aliases={i: i for i in range(nb)},
        compiler_params=pltpu.CompilerParams(has_side_effects=_EFFECT),
    )(*fl.bufs, *fl.sems, *after)
    return list(out[:ns]), list(out[ns:])


def _remote(src, dst, send_sem, recv_sem, to):
    return pltpu.make_async_remote_copy(src_ref=src, dst_ref=dst, send_sem=send_sem, recv_sem=recv_sem,
                                        device_id=to, device_id_type=pl.DeviceIdType.MESH)


def _gather_stage1_copies(n):
    def make(s_refs, l_refs, send, recv):
        x, y, c = _mesh_pos()
        own = 4 * x + 2 * y + c
        targets = [(x, y, 1 - c), (1 - x, y, c), (x, 1 - y, c), (1 - x, 1 - y, c)]
        return [_remote(s_refs[a], l_refs[a].at[own], send[4 * a + k], recv[4 * a + k], to)
                for a in range(n) for k, to in enumerate(targets)]
    return make


def _gather_stage2_copies(n):
    def make(s_refs, l_refs, send, recv):
        x, y, c = _mesh_pos()
        blocks = [4 * (1 - x) + 2 * y + c, 4 * x + 2 * (1 - y) + c, 4 * (1 - x) + 2 * (1 - y) + c]
        return [_remote(l_refs[a].at[b], l_refs[a].at[b], send[3 * a + j], recv[3 * a + j], (x, y, 1 - c))
                for a in range(n) for j, b in enumerate(blocks)]
    return make


def _pair_copies(n):
    def make(s_refs, l_refs, send, recv):
        x, y, c = _mesh_pos()
        return [_remote(s_refs[a].at[2 * k + 1 - c], l_refs[a].at[k], send[4 * a + k], recv[4 * a + k], (x, y, 1 - c))
                for a in range(n) for k in range(4)]
    return make


def _chip_copies(n):
    def make(s_refs, l_refs, send, recv):
        x, y, c = _mesh_pos()
        chips = [(1 - x, y), (x, 1 - y), (1 - x, 1 - y)]
        return [_remote(s_refs[a].at[2 * ch[0] + ch[1]], l_refs[a].at[j], send[3 * a + j], recv[3 * a + j], (*ch, c))
                for a in range(n) for j, ch in enumerate(chips)]
    return make


def _place_own_call(shard, dev, name):
    R, C = shard.shape
    tr = _tile(R, max(16, MIB // (C * shard.dtype.itemsize)), 16)

    def body(d_ref, s_ref, o_ref):
        o_ref[...] = s_ref[...]

    grid_spec = pltpu.PrefetchScalarGridSpec(
        num_scalar_prefetch=1, grid=(R // tr,),
        in_specs=[pl.BlockSpec((tr, C), lambda r, d: (r, 0))],
        out_specs=pl.BlockSpec((None, tr, C), lambda r, d: (d[0], r, 0)))
    return pl.pallas_call(body, name=name, grid_spec=grid_spec,
                          out_shape=jax.ShapeDtypeStruct((N_DEV, R, C), shard.dtype), compiler_params=_cp(1))(dev, shard)


def _all_gather_call(shards, name):
    n = len(shards)

    def body(*refs):
        x_refs, o_refs = refs[:n], refs[n:2 * n]
        send_sems, recv_sems, local_sems = refs[2 * n:]
        x, y, c = _mesh_pos()
        me, sib = (x, y, c), (x, y, 1 - c)
        chips = [(1 - x, y), (x, 1 - y), (1 - x, 1 - y)]

        def copy(a, k, block, to, src=None):
            dst = o_refs[a].at[4 * block[0] + 2 * block[1] + block[2]]
            return pltpu.make_async_remote_copy(
                src_ref=dst if src is None else src, dst_ref=dst,
                send_sem=send_sems.at[a, k], recv_sem=recv_sems.at[a, k],
                device_id=to, device_id_type=pl.DeviceIdType.MESH)

        mine = [pltpu.make_async_copy(x_refs[a], o_refs[a].at[4 * x + 2 * y + c], local_sems.at[a]) for a in range(n)]
        for cp in mine:
            cp.start()
        first = []
        for a in range(n):
            first.append(copy(a, 0, me, sib, src=x_refs[a]))
            for j, chip in enumerate(chips):
                first.append(copy(a, 1 + j, me, (*chip, c), src=x_refs[a]))
        for cp in first:
            cp.start()
        passed = []
        for a in range(n):
            for j, chip in enumerate(chips):
                copy(a, 1 + j, (*chip, c), me).wait_recv()
                fwd = copy(a, 4 + j, (*chip, c), sib)
                fwd.start()
                passed.append(fwd)
        for a in range(n):
            copy(a, 0, sib, me).wait_recv()
            for j, chip in enumerate(chips):
                copy(a, 4 + j, (*chip, 1 - c), me).wait_recv()
        for cp in first + passed:
            cp.wait_send()
        for cp in mine:
            cp.wait()

    return pl.pallas_call(
        body, name=name,
        out_shape=[jax.ShapeDtypeStruct((N_DEV,) + s.shape, s.dtype) for s in shards],
        in_specs=_any_specs(n), out_specs=_any_specs(n),
        scratch_shapes=[pltpu.SemaphoreType.DMA((n, 7)), pltpu.SemaphoreType.DMA((n, 7)), pltpu.SemaphoreType.DMA((n,))],
    )(*shards)


def _pair_exchange_call(grads, name):
    n = len(grads)

    def body(*refs):
        g_refs, r_refs = refs[:n], refs[n:2 * n]
        send_sems, recv_sems = refs[2 * n:]
        x, y, c = _mesh_pos()
        copies = []
        for a in range(n):
            for k in range(4):
                copies.append(pltpu.make_async_remote_copy(
                    src_ref=g_refs[a].at[2 * k + 1 - c], dst_ref=r_refs[a].at[k],
                    send_sem=send_sems.at[a, k], recv_sem=recv_sems.at[a, k],
                    device_id=(x, y, 1 - c), device_id_type=pl.DeviceIdType.MESH))
        for cp in copies:
            cp.start()
        for cp in copies:
            cp.wait_recv()
        for cp in copies:
            cp.wait_send()

    return pl.pallas_call(
        body, name=name,
        out_shape=[jax.ShapeDtypeStruct((4,) + g.shape[1:], g.dtype) for g in grads],
        in_specs=_any_specs(n), out_specs=_any_specs(n),
        scratch_shapes=[pltpu.SemaphoreType.DMA((n, 4)), pltpu.SemaphoreType.DMA((n, 4))],
    )(*grads)


def _chip_exchange_call(parts, name):
    n = len(parts)

    def body(*refs):
        p_refs, r_refs = refs[:n], refs[n:2 * n]
        send_sems, recv_sems = refs[2 * n:]
        x, y, c = _mesh_pos()
        chips = [(1 - x, y), (x, 1 - y), (1 - x, 1 - y)]
        copies = []
        for a in range(n):
            for j, chip in enumerate(chips):
                copies.append(pltpu.make_async_remote_copy(
                    src_ref=p_refs[a].at[2 * chip[0] + chip[1]], dst_ref=r_refs[a].at[j],
                    send_sem=send_sems.at[a, j], recv_sem=recv_sems.at[a, j],
                    device_id=(*chip, c), device_id_type=pl.DeviceIdType.MESH))
        for cp in copies:
            cp.start()
        for cp in copies:
            cp.wait_recv()
        for cp in copies:
            cp.wait_send()

    return pl.pallas_call(
        body, name=name,
        out_shape=[jax.ShapeDtypeStruct((3,) + p.shape[1:], p.dtype) for p in parts],
        in_specs=_any_specs(n), out_specs=_any_specs(n),
        scratch_shapes=[pltpu.SemaphoreType.DMA((n, 3)), pltpu.SemaphoreType.DMA((n, 3))],
    )(*parts)


def _pair_add_call(g, r1, core_chip, name):
    _, R, C = g.shape
    tr = _tile(R, max(8, (MIB // 2) // (C * 4)), 16)

    def body(cc_ref, g_ref, r_ref, p32_ref, p16_ref):
        s = g_ref[...] + r_ref[...]
        p16_ref[...] = s.astype(BF16)

        @pl.when(pl.program_id(1) == cc_ref[1])
        def _():
            p32_ref[...] = s

    grid_spec = pltpu.PrefetchScalarGridSpec(
        num_scalar_prefetch=1, grid=(R // tr, 4),
        in_specs=[pl.BlockSpec((None, tr, C), lambda r, k, cc: (2 * k + cc[0], r, 0)),
                  pl.BlockSpec((None, tr, C), lambda r, k, cc: (k, r, 0))],
        out_specs=[pl.BlockSpec((tr, C), lambda r, k, cc: (r, 0)),
                   pl.BlockSpec((None, tr, C), lambda r, k, cc: (k, r, 0))])
    return pl.pallas_call(
        body, name=name, grid_spec=grid_spec,
        out_shape=[jax.ShapeDtypeStruct((R, C), F32), jax.ShapeDtypeStruct((4, R, C), BF16)],
        compiler_params=_cp(2))(core_chip, g, r1)


def _adamw_call(w, m, v, addends, name, deps=()):
    R, C = w.shape
    tr = _tile(R, max(8, (MIB // 2) // (C * 4)), 16)
    na = len(addends)
    c1 = 1.0 - ADAM_B1 ** ADAM_STEP
    c2 = 1.0 - ADAM_B2 ** ADAM_STEP

    def body(*refs):
        w_ref, m_ref, v_ref = refs[:3]
        a_refs = refs[3:3 + na]
        g_ref, d_ref, nm_ref, nv_ref = refs[3 + na:]
        g = a_refs[0][...].astype(F32)
        for a_ref in a_refs[1:]:
            g = g + a_ref[...].astype(F32)
        nm = ADAM_B1 * m_ref[...] + (1.0 - ADAM_B1) * g
        nv = ADAM_B2 * v_ref[...] + (1.0 - ADAM_B2) * (g * g)
        g_ref[...] = g
        nm_ref[...] = nm
        nv_ref[...] = nv
        d_ref[...] = -ADAM_LR * ((nm / c1) / (jnp.sqrt(nv / c2) + ADAM_EPS) + ADAM_WD * w_ref[...])

    flat = pl.BlockSpec((tr, C), lambda r: (r, 0))
    a_specs = [flat if k is None else pl.BlockSpec((None, tr, C), functools.partial(lambda r, kk: (kk, r, 0), kk=k))
               for _, k in addends]
    out = jax.ShapeDtypeStruct((R, C), F32)
    return _pallas(
        body, 3 + na, deps, name=name, grid=(R // tr,),
        in_specs=[flat, flat, flat] + a_specs, out_specs=[flat] * 4, out_shape=[out] * 4,
        compiler_params=_cp(1))(w, m, v, *[a for a, _ in addends], *deps)


def _sum_call(parts, row0, rows, name):
    n = parts.shape[0]
    tr = _tile(math.gcd(row0, rows), 256, 8)
    b0 = row0 // tr

    def body(p_ref, o_ref):
        s = p_ref[0]
        for k in range(1, n):
            s = s + p_ref[k]
        o_ref[...] = s

    return pl.pallas_call(
        body, name=name, grid=(rows // tr,),
        in_specs=[pl.BlockSpec((n, tr, LANES), lambda r: (0, r + b0, 0))],
        out_specs=pl.BlockSpec((tr, LANES), lambda r: (r, 0)),
        out_shape=jax.ShapeDtypeStruct((rows, LANES), F32), compiler_params=_cp(1))(parts)


def _rmsnorm_call(x, g, name, deps=()):
    S, D = x.shape
    tm = _tile(S, 512, 16)

    def body(x_ref, g_ref, o_ref):
        xv = x_ref[...]
        r = lax.rsqrt(_rows_mean(xv * xv) + RMS_EPS)
        o_ref[...] = (xv * r * g_ref[...]).astype(BF16)

    return _pallas(
        body, 2, deps, name=name, grid=(S // tm,),
        in_specs=[pl.BlockSpec((tm, D), lambda i: (i, 0)), pl.BlockSpec((1, D), lambda i: (0, 0))],
        out_specs=pl.BlockSpec((tm, D), lambda i: (i, 0)),
        out_shape=jax.ShapeDtypeStruct((S, D), BF16), compiler_params=_cp(1))(x, g, *deps)


def _mm_blocked_call(a, wg, out_dtype, halves, name, deps=()):
    S, K = a.shape
    nb, _, bn = wg.shape
    tm = _tile(S, 1024, 16)
    tn = _tile(bn, 768, LANES)
    nsub = bn // tn
    J = nb * nsub

    def body(a_ref, w_ref, o_ref):
        o_ref[...] = jnp.dot(a_ref[...], w_ref[...], preferred_element_type=F32).astype(out_dtype)

    if halves:
        nh = J // 2
        out_spec = pl.BlockSpec((None, tm, tn), lambda i, j: (j // nh, i, j % nh))
        out_shape = jax.ShapeDtypeStruct((2, S, nb * bn // 2), out_dtype)
    else:
        out_spec = pl.BlockSpec((tm, tn), lambda i, j: (i, j))
        out_shape = jax.ShapeDtypeStruct((S, nb * bn), out_dtype)
    return _pallas(
        body, 2, deps, name=name, grid=(S // tm, J),
        in_specs=[pl.BlockSpec((tm, K), lambda i, j: (i, 0)),
                  pl.BlockSpec((None, K, tn), lambda i, j: (j // nsub, 0, j % nsub))],
        out_specs=out_spec, out_shape=out_shape, compiler_params=_cp(2))(a, wg, *deps)


def _mm_out_call(x, y, w, name):
    S, D = x.shape
    tm = _tile(S, 512, 16)

    def body(x_ref, y_ref, w_ref, o_ref):
        o_ref[...] = x_ref[...] + jnp.dot(y_ref[...], w_ref[...], preferred_element_type=F32)

    return pl.pallas_call(
        body, name=name, grid=(S // tm,),
        in_specs=[pl.BlockSpec((tm, D), lambda i: (i, 0)), pl.BlockSpec((tm, D), lambda i: (i, 0)),
                  pl.BlockSpec((D, D), lambda i: (0, 0))],
        out_specs=pl.BlockSpec((tm, D), lambda i: (i, 0)),
        out_shape=jax.ShapeDtypeStruct((S, D), F32), compiler_params=_cp(1))(x, y, w)


def _mm_nt_call(a, w, out_dtype, name, deps=()):
    S, K = a.shape
    N = w.shape[0]
    tm = _tile(S, 1024, 16)
    tn = _tile(N, 768, LANES)

    def body(a_ref, w_ref, o_ref):
        o_ref[...] = lax.dot_general(a_ref[...], w_ref[...], _DN_NT, preferred_element_type=F32).astype(out_dtype)

    return _pallas(
        body, 2, deps, name=name, grid=(S // tm, N // tn),
        in_specs=[pl.BlockSpec((tm, K), lambda i, j: (i, 0)), pl.BlockSpec((tn, K), lambda i, j: (j, 0))],
        out_specs=pl.BlockSpec((tm, tn), lambda i, j: (i, j)),
        out_shape=jax.ShapeDtypeStruct((S, N), out_dtype), compiler_params=_cp(2))(a, w, *deps)


def _mm_down_loss_call(x2, f, w, final_g, target, name):
    S, D = x2.shape
    Fd = f.shape[1]
    tm = _tile(S, 512, 16)
    tk = _tile(Fd, 768, LANES)
    nk = Fd // tk

    def body(x_ref, f_ref, w_ref, g_ref, t_ref, dx_ref, dxb_ref, loss_ref, dg_ref, acc):
        i, k = pl.program_id(0), pl.program_id(1)

        @pl.when(jnp.logical_and(i == 0, k == 0))
        def _():
            loss_ref[...] = jnp.zeros_like(loss_ref)
            dg_ref[...] = jnp.zeros_like(dg_ref)

        @pl.when(k == 0)
        def _():
            acc[...] = jnp.zeros_like(acc)

        acc[...] += jnp.dot(f_ref[...], w_ref[...], preferred_element_type=F32)

        @pl.when(k == nk - 1)
        def _():
            x3 = x_ref[...] + acc[...]
            r = lax.rsqrt(_rows_mean(x3 * x3) + RMS_EPS)
            g = g_ref[...]
            xn = x3 * r
            diff = xn * g - t_ref[...]
            loss_ref[...] += 0.5 * jnp.sum(_rows_mean(diff * diff))
            dout = diff * (1.0 / D)
            dg_ref[...] += _col_sum(dout * xn)
            dyg = dout * g
            dx = r * (dyg - xn * _rows_mean(dyg * xn))
            dx_ref[...] = dx
            dxb_ref[...] = dx.astype(BF16)

    row = lambda i, k: (i, 0)
    return pl.pallas_call(
        body, name=name, grid=(S // tm, nk),
        in_specs=[pl.BlockSpec((tm, D), row), pl.BlockSpec((tm, tk), lambda i, k: (i, k)),
                  pl.BlockSpec((tk, D), lambda i, k: (k, 0)), pl.BlockSpec((1, D), lambda i, k: (0, 0)),
                  pl.BlockSpec((tm, D), row)],
        out_specs=[pl.BlockSpec((tm, D), row), pl.BlockSpec((tm, D), row),
                   pl.BlockSpec((8, LANES), lambda i, k: (0, 0)), pl.BlockSpec((1, D), lambda i, k: (0, 0))],
        out_shape=[jax.ShapeDtypeStruct((S, D), F32), jax.ShapeDtypeStruct((S, D), BF16),
                   jax.ShapeDtypeStruct((8, LANES), F32), jax.ShapeDtypeStruct((1, D), F32)],
        scratch_shapes=[pltpu.VMEM((tm, D), F32)], compiler_params=_cp(2, 56))(x2, f, w, final_g, target)


def _mm_dx_norm_call(a3, wg, resid, xin, g, name, deps=()):
    na, S, Fa = a3.shape
    nb, D, bn = wg.shape
    tm = _tile(S, 512, 16)
    tk = _tile(bn, 768, LANES)
    nsub = bn // tk
    nka = Fa // tk
    nk = nb * nsub
    assert na * nka == nk

    def body(a_ref, w_ref, r_ref, x_ref, g_ref, dx_ref, dxb_ref, dg_ref, acc):
        i, k = pl.program_id(0), pl.program_id(1)

        @pl.when(jnp.logical_and(i == 0, k == 0))
        def _():
            dg_ref[...] = jnp.zeros_like(dg_ref)

        @pl.when(k == 0)
        def _():
            acc[...] = jnp.zeros_like(acc)

        acc[...] += lax.dot_general(a_ref[...], w_ref[...], _DN_NT, preferred_element_type=F32)

        @pl.when(k == nk - 1)
        def _():
            dh = acc[...]
            xv = x_ref[...]
            r = lax.rsqrt(_rows_mean(xv * xv) + RMS_EPS)
            xn = xv * r
            dg_ref[...] += _col_sum(dh * xn)
            dyg = dh * g_ref[...]
            dx = r_ref[...] + r * (dyg - xn * _rows_mean(dyg * xn))
            dx_ref[...] = dx
            dxb_ref[...] = dx.astype(BF16)

    row = lambda i, k: (i, 0)
    return _pallas(
        body, 5, deps, name=name, grid=(S // tm, nk),
        in_specs=[pl.BlockSpec((None, tm, tk), lambda i, k: (k // nka, i, k % nka)),
                  pl.BlockSpec((None, D, tk), lambda i, k: (k // nsub, 0, k % nsub)),
                  pl.BlockSpec((tm, D), row), pl.BlockSpec((tm, D), row), pl.BlockSpec((1, D), lambda i, k: (0, 0))],
        out_specs=[pl.BlockSpec((tm, D), row), pl.BlockSpec((tm, D), row), pl.BlockSpec((1, D), lambda i, k: (0, 0))],
        out_shape=[jax.ShapeDtypeStruct((S, D), F32), jax.ShapeDtypeStruct((S, D), BF16),
                   jax.ShapeDtypeStruct((1, D), F32)],
        scratch_shapes=[pltpu.VMEM((tm, D), F32)], compiler_params=_cp(2, 56))(a3, wg, resid, xin, g, *deps)


def _mm_tn_cols_call(a, b3, nb, bn, name, deps=()):
    S, Ka = a.shape
    nh, _, Fb = b3.shape
    tm = _tile(S, 512, 16)
    tn = _tile(bn, 768, LANES)
    nsub = bn // tn
    njb = Fb // tn
    J = nb * nsub
    assert nh * njb == J

    def body(a_ref, b_ref, o_ref):
        @pl.when(pl.program_id(1) == 0)
        def _():
            o_ref[...] = jnp.zeros_like(o_ref)

        o_ref[...] += lax.dot_general(a_ref[...], b_ref[...], _DN_TN, preferred_element_type=F32)

    return _pallas(
        body, 2, deps, name=name, grid=(J, S // tm),
        in_specs=[pl.BlockSpec((tm, Ka), lambda j, i: (i, 0)),
                  pl.BlockSpec((None, tm, tn), lambda j, i: (j // njb, i, j % njb))],
        out_specs=pl.BlockSpec((None, Ka, tn), lambda j, i: (j // nsub, 0, j % nsub)),
        out_shape=jax.ShapeDtypeStruct((nb, Ka, bn), F32), compiler_params=_cp(2, 56))(a, b3, *deps)


def _mm_tn_rows_call(a, b, name, deps=()):
    S, E = a.shape
    D = b.shape[1]
    tm = _tile(S, 512, 16)
    te = _tile(E, 768, LANES)

    def body(a_ref, b_ref, o_ref):
        @pl.when(pl.program_id(1) == 0)
        def _():
            o_ref[...] = jnp.zeros_like(o_ref)

        o_ref[...] += lax.dot_general(a_ref[...], b_ref[...], _DN_TN, preferred_element_type=F32)

    return _pallas(
        body, 2, deps, name=name, grid=(E // te, S // tm),
        in_specs=[pl.BlockSpec((tm, te), lambda j, i: (i, j)), pl.BlockSpec((tm, D), lambda j, i: (i, 0))],
        out_specs=pl.BlockSpec((te, D), lambda j, i: (j, 0)),
        out_shape=jax.ShapeDtypeStruct((E, D), F32), compiler_params=_cp(2, 56))(a, b, *deps)


def _ffn_tiles(S, Fd):
    return _tile(S, 512, 16), _tile(Fd, 768, LANES)


def _taps(cw_ref):
    return [cw_ref[k:k + 1, :] for k in range(cw_ref.shape[0])]


def _conv3(prev8, cur, taps):
    s1 = _shift_down(prev8, cur, 1)
    s2 = _shift_down(prev8, cur, 2)
    return taps[2] * cur + taps[1] * s1 + taps[0] * s2, s1, s2


def _ffn_act_call(up3, cw, cb, name):
    _, S, Fd = up3.shape
    tm, tc = _ffn_tiles(S, Fd)
    nj = Fd // tc
    hb = tm // 16

    def body(g_ref, v_ref, gp_ref, vp_ref, cwg_ref, cwv_ref, cbg_ref, cbv_ref, o_ref):
        first = pl.program_id(0) == 0
        keep = jnp.where(first, 0.0, 1.0)
        gp = gp_ref[...].astype(F32)[8:] * keep
        vp = vp_ref[...].astype(F32)[8:] * keep
        cg, _, _ = _conv3(gp, g_ref[...].astype(F32), _taps(cwg_ref))
        cv, _, _ = _conv3(vp, v_ref[...].astype(F32), _taps(cwv_ref))
        o_ref[...] = (_gelu(cg + cbg_ref[...]) * (cv + cbv_ref[...])).astype(BF16)

    prev = lambda i, j: (jnp.maximum(i * hb - 1, 0), j)
    return pl.pallas_call(
        body, name=name, grid=(S // tm, nj),
        in_specs=[pl.BlockSpec((None, tm, tc), lambda i, j: (0, i, j)), pl.BlockSpec((None, tm, tc), lambda i, j: (1, i, j)),
                  pl.BlockSpec((None, 16, tc), lambda i, j: (0,) + prev(i, j)),
                  pl.BlockSpec((None, 16, tc), lambda i, j: (1,) + prev(i, j)),
                  pl.BlockSpec((3, tc), lambda i, j: (0, j)), pl.BlockSpec((3, tc), lambda i, j: (0, j + nj)),
                  pl.BlockSpec((1, tc), lambda i, j: (0, j)), pl.BlockSpec((1, tc), lambda i, j: (0, j + nj))],
        out_specs=pl.BlockSpec((tm, tc), lambda i, j: (i, j)),
        out_shape=jax.ShapeDtypeStruct((S, Fd), BF16), compiler_params=_cp(2))(up3, up3, up3, up3, cw, cw, cb, cb)


def _ffn_act_bwd_call(up3, d_f, cw, cb, name):
    _, S, Fd = up3.shape
    tm, tc = _ffn_tiles(S, Fd)
    nj = Fd // tc
    ni = S // tm
    hb = tm // 16

    def body(g_ref, v_ref, gp_ref, vp_ref, gn_ref, vn_ref, df_ref, dfn_ref, cwg_ref, cwv_ref, cbg_ref, cbv_ref,
             dup_ref, dcwg_ref, dcwv_ref, dcbg_ref, dcbv_ref):
        i = pl.program_id(1)
        keep_prev = jnp.where(i == 0, 0.0, 1.0)
        keep_next = jnp.where(i == ni - 1, 0.0, 1.0)

        @pl.when(i == 0)
        def _():
            for r in (dcwg_ref, dcwv_ref, dcbg_ref, dcbv_ref):
                r[...] = jnp.zeros_like(r)

        df = jnp.concatenate([df_ref[...].astype(F32), dfn_ref[...].astype(F32)[:8] * keep_next], axis=0)

        def half(x_ref, xp_ref, xn_ref, cw_ref, cb_ref):
            cw_ = _taps(cw_ref)
            prev8 = xp_ref[...].astype(F32)[8:] * keep_prev
            ext = jnp.concatenate([x_ref[...].astype(F32), xn_ref[...].astype(F32)[:8]], axis=0)
            conv, s1, s2 = _conv3(prev8, ext, cw_)
            return cw_, conv + cb_ref[...], (ext, s1, s2)

        cwg, cg, gsh = half(g_ref, gp_ref, gn_ref, cwg_ref, cbg_ref)
        cwv, cv, vsh = half(v_ref, vp_ref, vn_ref, cwv_ref, cbv_ref)
        gel, dgel = _gelu_parts(cg)
        dg = df * cv * dgel
        dv = df * gel
        zeros8 = jnp.zeros((8, tc), F32)

        def back(d, cw_, shifts, dcw_ref, dcb_ref, slab):
            dup = cw_[2] * d + cw_[1] * _shift_up(d, zeros8, 1) + cw_[0] * _shift_up(d, zeros8, 2)
            dup_ref[slab] = dup[:tm].astype(BF16)
            dm = d[:tm]
            ext, s1, s2 = shifts
            dcw_ref[2:3, :] += _col_sum(ext[:tm] * dm)
            dcw_ref[1:2, :] += _col_sum(s1[:tm] * dm)
            dcw_ref[0:1, :] += _col_sum(s2[:tm] * dm)
            dcb_ref[...] += _col_sum(dm)

        back(dg, cwg, gsh, dcwg_ref, dcbg_ref, 0)
        back(dv, cwv, vsh, dcwv_ref, dcbv_ref, 1)

    prev = lambda j, i: (jnp.maximum(i * hb - 1, 0), j)
    nxt = lambda j, i: (jnp.minimum((i + 1) * hb, S // 16 - 1), j)
    main = lambda s: pl.BlockSpec((None, tm, tc), lambda j, i: (s, i, j))
    halo = lambda s, f: pl.BlockSpec((None, 16, tc), lambda j, i: (s,) + f(j, i))
    acc3 = pl.BlockSpec((3, tc), lambda j, i: (0, j))
    acc1 = pl.BlockSpec((1, tc), lambda j, i: (0, j))
    return pl.pallas_call(
        body, name=name, grid=(nj, ni),
        in_specs=[main(0), main(1), halo(0, prev), halo(1, prev), halo(0, nxt), halo(1, nxt),
                  pl.BlockSpec((tm, tc), lambda j, i: (i, j)), pl.BlockSpec((16, tc), nxt),
                  pl.BlockSpec((3, tc), lambda j, i: (0, j)), pl.BlockSpec((3, tc), lambda j, i: (0, j + nj)),
                  pl.BlockSpec((1, tc), lambda j, i: (0, j)), pl.BlockSpec((1, tc), lambda j, i: (0, j + nj))],
        out_specs=[pl.BlockSpec((2, tm, tc), lambda j, i: (0, i, j)), acc3, acc3, acc1, acc1],
        out_shape=[jax.ShapeDtypeStruct((2, S, Fd), BF16), jax.ShapeDtypeStruct((3, Fd), F32),
                   jax.ShapeDtypeStruct((3, Fd), F32), jax.ShapeDtypeStruct((1, Fd), F32),
                   jax.ShapeDtypeStruct((1, Fd), F32)],
        compiler_params=_cp(2, 56))(up3, up3, up3, up3, up3, up3, d_f, d_f, cw, cw, cb, cb)


def _gm_forward_tile(pv, vg, vb, ws_ref, bsb_ref, mbuf, H, nc):
    W = H * CHUNK
    z, dz = _gelu_parts(pv)
    u, v0 = z[:, :W], z[:, W:]
    xc = v0 - _rows_mean(v0)
    rs = lax.rsqrt(_rows_mean(xc * xc) + LN_EPS)
    vh = xc * rs
    vnb = (vh * vg + vb).astype(BF16)
    mask = lax.broadcasted_iota(jnp.int32, (CHUNK, CHUNK), 0) >= lax.broadcasted_iota(jnp.int32, (CHUNK, CHUNK), 1)
    for h in range(H):
        cs = slice(h * CHUNK, (h + 1) * CHUNK)
        wm = jnp.where(mask, ws_ref[h], 0.0).astype(BF16)
        vcat = jnp.concatenate([vnb[c * CHUNK:(c + 1) * CHUNK, cs] for c in range(nc)], axis=1)
        mix = jnp.dot(wm, vcat, preferred_element_type=F32)
        for c in range(nc):
            mbuf[c * CHUNK:(c + 1) * CHUNK, cs] = mix[:, c * CHUNK:(c + 1) * CHUNK] + bsb_ref[h]
    return dz, u, vh, rs, vnb, mask


def _gm_fwd_call(p, v_g, v_b, ws, bsb, out_g, name, deps=()):
    S = p.shape[0]
    H = ws.shape[0]
    W = H * CHUNK
    tm = _tile(S, 256, CHUNK)
    nc = tm // CHUNK

    def body(p_ref, vg_ref, vb_ref, ws_ref, bsb_ref, og_ref, y_ref, mbuf):
        _, u, _, _, _, _ = _gm_forward_tile(p_ref[...], vg_ref[...], vb_ref[...], ws_ref, bsb_ref, mbuf, H, nc)
        yg = u * mbuf[...]
        r = lax.rsqrt(_rows_mean(yg * yg) + RMS_EPS)
        y_ref[...] = (yg * r * og_ref[...]).astype(BF16)

    vec = pl.BlockSpec((1, W), lambda i: (0, 0))
    mat = pl.BlockSpec((H, CHUNK, CHUNK), lambda i: (0, 0, 0))
    return _pallas(
        body, 6, deps, name=name, grid=(S // tm,),
        in_specs=[pl.BlockSpec((tm, 2 * W), lambda i: (i, 0)), vec, vec, mat, mat, vec],
        out_specs=pl.BlockSpec((tm, W), lambda i: (i, 0)),
        out_shape=jax.ShapeDtypeStruct((S, W), BF16),
        scratch_shapes=[pltpu.VMEM((tm, W), F32)], compiler_params=_cp(1))(p, v_g, v_b, ws, bsb, out_g, *deps)


def _gm_bwd_call(p, d_y, v_g, v_b, ws, bsb, out_g, name):
    S = p.shape[0]
    H = ws.shape[0]
    W = H * CHUNK
    tm = _tile(S, 256, CHUNK)
    nc = tm // CHUNK
    ni = S // tm

    def body(p_ref, dy_ref, vg_ref, vb_ref, ws_ref, bsb_ref, og_ref,
             dp_ref, dvg_ref, dvb_ref, dws_ref, dbs_ref, dog_ref, mbuf, dvbuf):
        i = pl.program_id(0)

        @pl.when(i == 0)
        def _():
            for r in (dvg_ref, dvb_ref, dws_ref, dbs_ref, dog_ref):
                r[...] = jnp.zeros_like(r)

        vg = vg_ref[...]
        dz, u, vh, rs, vnb, mask = _gm_forward_tile(p_ref[...], vg, vb_ref[...], ws_ref, bsb_ref, mbuf, H, nc)
        mixed = mbuf[...]
        yg = u * mixed
        r = lax.rsqrt(_rows_mean(yg * yg) + RMS_EPS)
        yn = yg * r
        dya = dy_ref[...]
        dog_ref[...] += _col_sum(dya * yn)
        dyg = dya * og_ref[...]
        dygm = r * (dyg - yn * _rows_mean(dyg * yn))
        du = dygm * mixed
        dmix = dygm * u
        dmb = dmix.astype(BF16)
        for h in range(H):
            cs = slice(h * CHUNK, (h + 1) * CHUNK)
            wm = jnp.where(mask, ws_ref[h], 0.0).astype(BF16)
            dcat = jnp.concatenate([dmb[c * CHUNK:(c + 1) * CHUNK, cs] for c in range(nc)], axis=1)
            vcat = jnp.concatenate([vnb[c * CHUNK:(c + 1) * CHUNK, cs] for c in range(nc)], axis=1)
            dvn = lax.dot_general(wm, dcat, _DN_TN, preferred_element_type=F32)
            dws_ref[h] += jnp.where(mask, lax.dot_general(dcat, vcat, _DN_NT, preferred_element_type=F32), 0.0)
            dbs = dmix[0:CHUNK, cs]
            for c in range(1, nc):
                dbs = dbs + dmix[c * CHUNK:(c + 1) * CHUNK, cs]
            dbs_ref[h] += dbs
            for c in range(nc):
                dvbuf[c * CHUNK:(c + 1) * CHUNK, cs] = dvn[:, c * CHUNK:(c + 1) * CHUNK]
        dvn_all = dvbuf[...]
        dvg_ref[...] += _col_sum(dvn_all * vh)
        dvb_ref[...] += _col_sum(dvn_all)
        dvh = dvn_all * vg
        dv0 = rs * (dvh - _rows_mean(dvh) - vh * _rows_mean(dvh * vh))
        dp_ref[...] = (jnp.concatenate([du, dv0], axis=1) * dz).astype(BF16)

        @pl.when(i == ni - 1)
        def _():
            for h in range(H):
                dbs_ref[h] = jnp.broadcast_to(jnp.sum(dbs_ref[h], axis=1, keepdims=True), (CHUNK, CHUNK))

    vec = pl.BlockSpec((1, W), lambda i: (0, 0))
    mat = pl.BlockSpec((H, CHUNK, CHUNK), lambda i: (0, 0, 0))
    vshape = jax.ShapeDtypeStruct((1, W), F32)
    mshape = jax.ShapeDtypeStruct((H, CHUNK, CHUNK), F32)
    return pl.pallas_call(
        body, name=name, grid=(ni,),
        in_specs=[pl.BlockSpec((tm, 2 * W), lambda i: (i, 0)), pl.BlockSpec((tm, W), lambda i: (i, 0)),
                  vec, vec, mat, mat, vec],
        out_specs=[pl.BlockSpec((tm, 2 * W), lambda i: (i, 0)), vec, vec, mat, mat, vec],
        out_shape=[jax.ShapeDtypeStruct((S, 2 * W), BF16), vshape, vshape, mshape, mshape, vshape],
        scratch_shapes=[pltpu.VMEM((tm, W), F32), pltpu.VMEM((tm, W), F32)],
        compiler_params=_cp(1))(p, d_y, v_g, v_b, ws, bsb, out_g)


def _lru_gates(prev8, xl, cw, cb, wa_ref, ba, wx_ref, bx, lam, H):
    sh = [_shift_down(prev8, xl, k) for k in range(4)]
    xr = cw[3] * sh[0] + cw[2] * sh[1] + cw[1] * sh[2] + cw[0] * sh[3] + cb
    xrb = xr.astype(BF16)
    rp, ip = [], []
    for h in range(H):
        cs = slice(h * CHUNK, (h + 1) * CHUNK)
        rp.append(jnp.dot(xrb[:, cs], wa_ref[h].astype(BF16), preferred_element_type=F32))
        ip.append(jnp.dot(xrb[:, cs], wx_ref[h].astype(BF16), preferred_element_type=F32))
    r = _sigmoid(jnp.concatenate(rp, axis=1) + ba)
    ig = _sigmoid(jnp.concatenate(ip, axis=1) + bx)
    sp = _softplus(-lam)
    la = (-LRU_C) * r * sp
    a = jnp.exp(la)
    mult = jnp.sqrt(jnp.maximum(_neg_expm1(2.0 * la), 0.0))
    return xr, xrb, r, ig, sp, a, mult, sh


def _lru_fwd_call(p, cw, cb, wa, ba, wx, bx, lam, out_g, name):
    S = p.shape[0]
    H = wa.shape[0]
    W = H * CHUNK
    tm = _tile(S, 256, 16)
    ng = tm // 8

    def body(pg_ref, px_ref, cw_ref, cb_ref, wa_ref, ba_ref, wx_ref, bx_ref, lam_ref, og_ref,
             y_ref, h_ref, xprev, hcar, abuf, bbuf):
        @pl.when(pl.program_id(0) == 0)
        def _():
            xprev[...] = jnp.zeros_like(xprev)
            hcar[...] = jnp.zeros_like(hcar)

        xl = px_ref[...]
        xr, _, _, ig, _, a, mult, _ = _lru_gates(xprev[...], xl, _taps(cw_ref), cb_ref[...], wa_ref, ba_ref[...],
                                                 wx_ref, bx_ref[...], lam_ref[...], H)
        xprev[...] = xl[tm - 8:]
        b = mult * (ig * xr)
        sub = lax.broadcasted_iota(jnp.int32, (tm, W), 0) & 7
        for d in (1, 2, 4):
            m = sub >= d
            a_s = jnp.where(m, pltpu.roll(a, d, 0), 1.0)
            b_s = jnp.where(m, pltpu.roll(b, d, 0), 0.0)
            b = a * b_s + b
            a = a * a_s
        abuf[...] = a
        bbuf[...] = b

        def step(g, carry):
            r0 = pl.multiple_of(g * 8, 8)
            h_ref[pl.ds(r0, 8), :] = abuf[pl.ds(r0, 8), :] * carry + bbuf[pl.ds(r0, 8), :]
            return jnp.broadcast_to(h_ref[pl.ds(r0 + 7, 1), :], (8, W))

        hcar[...] = lax.fori_loop(0, ng, step, hcar[...])
        yl = h_ref[...] * _gelu(pg_ref[...])
        r = lax.rsqrt(_rows_mean(yl * yl) + RMS_EPS)
        y_ref[...] = (yl * r * og_ref[...]).astype(BF16)

    vec = pl.BlockSpec((1, W), lambda i: (0, 0))
    mat = pl.BlockSpec((H, CHUNK, CHUNK), lambda i: (0, 0, 0))
    return pl.pallas_call(
        body, name=name, grid=(S // tm,),
        in_specs=[pl.BlockSpec((tm, W), lambda i: (i, 2)), pl.BlockSpec((tm, W), lambda i: (i, 3)),
                  pl.BlockSpec((4, W), lambda i: (0, 0)), vec, mat, vec, mat, vec, vec, vec],
        out_specs=[pl.BlockSpec((tm, W), lambda i: (i, 0)), pl.BlockSpec((tm, W), lambda i: (i, 0))],
        out_shape=[jax.ShapeDtypeStruct((S, W), BF16), jax.ShapeDtypeStruct((S, W), F32)],
        scratch_shapes=[pltpu.VMEM((8, W), F32), pltpu.VMEM((8, W), F32), pltpu.VMEM((tm, W), F32),
                        pltpu.VMEM((tm, W), F32)],
        compiler_params=_cp(1))(p, p, cw, cb, wa, ba, wx, bx, lam, out_g)


def _lru_bwd_call(p, hs, d_y, cw, cb, wa, ba, wx, bx, lam, out_g, name):
    S = p.shape[0]
    H = wa.shape[0]
    W = H * CHUNK
    tm = _tile(S, 256, 16)
    ng = tm // 8
    ni = S // tm
    hb = tm // 8

    def body(pg_ref, px_ref, pxp_ref, h_ref, hp_ref, dy_ref, cw_ref, cb_ref, wa_ref, ba_ref, wx_ref, bx_ref,
             lam_ref, og_ref,
             dp_ref, dcw_ref, dcb_ref, dwa_ref, dba_ref, dwx_ref, dbx_ref, dlam_ref, dog_ref,
             a_next, e_next, dxr_next, abuf, bbuf, ebuf):
        i = pl.program_id(0)
        ri = ni - 1 - i

        @pl.when(i == 0)
        def _():
            for r in (dcw_ref, dcb_ref, dwa_ref, dba_ref, dwx_ref, dbx_ref, dlam_ref, dog_ref,
                      a_next, e_next, dxr_next):
                r[...] = jnp.zeros_like(r)

        keep_prev = jnp.where(ri == 0, 0.0, 1.0)
        cw_ = _taps(cw_ref)
        lam_ = lam_ref[...]
        xl = px_ref[...]
        xr, xrb, r, ig, sp, a, mult, sh = _lru_gates(pxp_ref[...] * keep_prev, xl, cw_, cb_ref[...], wa_ref,
                                                     ba_ref[...], wx_ref, bx_ref[...], lam_, H)
        gg, dgg = _gelu_parts(pg_ref[...])
        hv = h_ref[...]
        yl = hv * gg
        rr = lax.rsqrt(_rows_mean(yl * yl) + RMS_EPS)
        yn = yl * rr
        dyb = dy_ref[...]
        dog_ref[...] += _col_sum(dyb * yn)
        dyg = dyb * og_ref[...]
        dyl = rr * (dyg - yn * _rows_mean(dyg * yn))
        dh = dyl * gg
        dgl = dyl * hv * dgg

        an = _shift_up(a, a_next[...], 1)
        eb = dh
        sub = lax.broadcasted_iota(jnp.int32, (tm, W), 0) & 7
        for d in (1, 2, 4):
            m = sub < 8 - d
            a_s = jnp.where(m, pltpu.roll(an, tm - d, 0), 1.0)
            e_s = jnp.where(m, pltpu.roll(eb, tm - d, 0), 0.0)
            eb = an * e_s + eb
            an = an * a_s
        abuf[...] = an
        bbuf[...] = eb

        def step(g, carry):
            r0 = pl.multiple_of((ng - 1 - g) * 8, 8)
            ebuf[pl.ds(r0, 8), :] = abuf[pl.ds(r0, 8), :] * carry + bbuf[pl.ds(r0, 8), :]
            return jnp.broadcast_to(ebuf[pl.ds(r0, 1), :], (8, W))

        lax.fori_loop(0, ng, step, jnp.broadcast_to(e_next[0:1, :], (8, W)))
        e = ebuf[...]
        a_next[...] = a[0:8]
        e_next[...] = e[0:8]

        hm1 = _shift_down(hp_ref[...] * keep_prev, hv, 1)
        da = e * hm1
        dmult = e * ig * xr
        di = e * mult * xr
        dxr = e * mult * ig
        dla = da * a - dmult * (a * a) / mult
        dr = dla * ((-LRU_C) * sp)
        dlam_ref[...] += _col_sum(dla * ((-LRU_C) * r))
        dpr = dr * r * (1.0 - r)
        dpi = di * ig * (1.0 - ig)
        dba_ref[...] += _col_sum(dpr)
        dbx_ref[...] += _col_sum(dpi)
        dprb = dpr.astype(BF16)
        dpib = dpi.astype(BF16)
        back = []
        for h in range(H):
            cs = slice(h * CHUNK, (h + 1) * CHUNK)
            wab = wa_ref[h].astype(BF16)
            wxb = wx_ref[h].astype(BF16)
            back.append(lax.dot_general(dprb[:, cs], wab, _DN_NT, preferred_element_type=F32)
                        + lax.dot_general(dpib[:, cs], wxb, _DN_NT, preferred_element_type=F32))
            dwa_ref[h] += lax.dot_general(xrb[:, cs], dprb[:, cs], _DN_TN, preferred_element_type=F32)
            dwx_ref[h] += lax.dot_general(xrb[:, cs], dpib[:, cs], _DN_TN, preferred_element_type=F32)
        dxr = dxr + jnp.concatenate(back, axis=1)

        nxt = dxr_next[...]
        dxl = (cw_[3] * dxr + cw_[2] * _shift_up(dxr, nxt, 1) + cw_[1] * _shift_up(dxr, nxt, 2)
               + cw_[0] * _shift_up(dxr, nxt, 3))
        dxr_next[...] = dxr[0:8]
        for k in range(4):
            dcw_ref[k:k + 1, :] += _col_sum(sh[3 - k] * dxr)
        dcb_ref[...] += _col_sum(dxr)
        dp_ref[...] = jnp.concatenate([dgl, dxl], axis=1).astype(BF16)

        @pl.when(i == ni - 1)
        def _():
            dlam_ref[...] = -dlam_ref[...] * _sigmoid(-lam_)

    vec = pl.BlockSpec((1, W), lambda i: (0, 0))
    mat = pl.BlockSpec((H, CHUNK, CHUNK), lambda i: (0, 0, 0))
    rev = lambda i: ni - 1 - i
    prev = lambda i: jnp.maximum(rev(i) * hb - 1, 0)
    vshape = jax.ShapeDtypeStruct((1, W), F32)
    mshape = jax.ShapeDtypeStruct((H, CHUNK, CHUNK), F32)
    tile = lambda: pltpu.VMEM((tm, W), F32)
    car = lambda: pltpu.VMEM((8, W), F32)
    return pl.pallas_call(
        body, name=name, grid=(ni,),
        in_specs=[pl.BlockSpec((tm, W), lambda i: (rev(i), 2)), pl.BlockSpec((tm, W), lambda i: (rev(i), 3)),
                  pl.BlockSpec((8, W), lambda i: (prev(i), 3)),
                  pl.BlockSpec((tm, W), lambda i: (rev(i), 0)), pl.BlockSpec((8, W), lambda i: (prev(i), 0)),
                  pl.BlockSpec((tm, W), lambda i: (rev(i), 1)),
                  pl.BlockSpec((4, W), lambda i: (0, 0)), vec, mat, vec, mat, vec, vec, vec],
        out_specs=[pl.BlockSpec((tm, 2 * W), lambda i: (rev(i), 0)), pl.BlockSpec((4, W), lambda i: (0, 0)), vec,
                   mat, vec, mat, vec, vec, vec],
        out_shape=[jax.ShapeDtypeStruct((S, 2 * W), BF16), jax.ShapeDtypeStruct((4, W), F32), vshape,
                   mshape, vshape, mshape, vshape, vshape, vshape],
        scratch_shapes=[car(), car(), car(), tile(), tile(), tile()],
        compiler_params=_cp(1, 56))(p, p, p, hs, hs, d_y, cw, cb, wa, ba, wx, bx, lam, out_g)


def _rows128(a):
    return a.reshape(-1, LANES).astype(F32)


def _pack(arrays, pad_to=256):
    flat = jnp.concatenate([_rows128(a) for a in arrays], axis=0)
    pad = (-flat.shape[0]) % pad_to
    if pad:
        flat = jnp.concatenate([flat, jnp.zeros((pad, LANES), F32)], axis=0)
    return flat


def _unpack(flat, shapes):
    out, r = [], 0
    for s in shapes:
        n = 1
        for d in s:
            n *= d
        out.append(flat[r:r + n // LANES].reshape(s))
        r += n // LANES
    return out


def kernel(x, norm1_g, w_in, gm_v_g, gm_v_b, gm_ws, gm_bs, lru_conv_w, lru_conv_b, lru_wa, lru_ba, lru_wx, lru_bx, lru_lambda, gm_out_g, lru_out_g, w_out, norm2_g, ffn_w_up, ffn_conv_w, ffn_conv_b, ffn_w_down, final_g, loss_target, m_norm1_g, m_w_in, m_gm_v_g, m_gm_v_b, m_gm_ws, m_gm_bs, m_lru_conv_w, m_lru_conv_b, m_lru_wa, m_lru_ba, m_lru_wx, m_lru_bx, m_lru_lambda, m_gm_out_g, m_lru_out_g, m_w_out, m_norm2_g, m_ffn_w_up, m_ffn_conv_w, m_ffn_conv_b, m_ffn_w_down, m_final_g, v_norm1_g, v_w_in, v_gm_v_g, v_gm_v_b, v_gm_ws, v_gm_bs, v_lru_conv_w, v_lru_conv_b, v_lru_wa, v_lru_ba, v_lru_wx, v_lru_bx, v_lru_lambda, v_gm_out_g, v_lru_out_g, v_w_out, v_norm2_g, v_ffn_w_up, v_ffn_conv_w, v_ffn_conv_b, v_ffn_w_down, v_final_g):
    wts = dict(norm1_g=norm1_g, w_in=w_in, gm_v_g=gm_v_g, gm_v_b=gm_v_b, gm_ws=gm_ws, gm_bs=gm_bs,
               lru_conv_w=lru_conv_w, lru_conv_b=lru_conv_b, lru_wa=lru_wa, lru_ba=lru_ba, lru_wx=lru_wx,
               lru_bx=lru_bx, lru_lambda=lru_lambda, gm_out_g=gm_out_g, lru_out_g=lru_out_g, w_out=w_out,
               norm2_g=norm2_g, ffn_w_up=ffn_w_up, ffn_conv_w=ffn_conv_w, ffn_conv_b=ffn_conv_b,
               ffn_w_down=ffn_w_down, final_g=final_g)
    mom = dict(norm1_g=m_norm1_g, w_in=m_w_in, gm_v_g=m_gm_v_g, gm_v_b=m_gm_v_b, gm_ws=m_gm_ws, gm_bs=m_gm_bs,
               lru_conv_w=m_lru_conv_w, lru_conv_b=m_lru_conv_b, lru_wa=m_lru_wa, lru_ba=m_lru_ba, lru_wx=m_lru_wx,
               lru_bx=m_lru_bx, lru_lambda=m_lru_lambda, gm_out_g=m_gm_out_g, lru_out_g=m_lru_out_g, w_out=m_w_out,
               norm2_g=m_norm2_g, ffn_w_up=m_ffn_w_up, ffn_conv_w=m_ffn_conv_w, ffn_conv_b=m_ffn_conv_b,
               ffn_w_down=m_ffn_w_down, final_g=m_final_g)
    var = dict(norm1_g=v_norm1_g, w_in=v_w_in, gm_v_g=v_gm_v_g, gm_v_b=v_gm_v_b, gm_ws=v_gm_ws, gm_bs=v_gm_bs,
               lru_conv_w=v_lru_conv_w, lru_conv_b=v_lru_conv_b, lru_wa=v_lru_wa, lru_ba=v_lru_ba, lru_wx=v_lru_wx,
               lru_bx=v_lru_bx, lru_lambda=v_lru_lambda, gm_out_g=v_gm_out_g, lru_out_g=v_lru_out_g, w_out=v_w_out,
               norm2_g=v_norm2_g, ffn_w_up=v_ffn_w_up, ffn_conv_w=v_ffn_conv_w, ffn_conv_b=v_ffn_conv_b,
               ffn_w_down=v_ffn_w_down, final_g=v_final_g)

    xi, yi, ci = lax.axis_index("x"), lax.axis_index("y"), lax.axis_index("c")
    chip = 2 * xi + yi
    dev = 2 * chip + ci
    core_chip = jnp.stack([ci, chip]).astype(jnp.int32)

    xs = x[0]
    tgt = loss_target[0]
    S, D = xs.shape
    H = gm_ws.shape[1]
    W = H * CHUNK
    Fd = ffn_w_down.shape[1] * N_DEV
    lcw_cols = lru_conv_w.shape[2]
    fcw_cols = ffn_conv_w.shape[2]

    dev1 = jnp.reshape(dev, (1,)).astype(jnp.int32)

    def gather_start(shards, name, after=()):
        lands = [_place_own_call(s, dev1, "%s_own%d" % (name, k)) for k, s in enumerate(shards)]
        return _exchange_start(shards, lands, 4 * len(shards), _gather_stage1_copies(len(shards)), name + "_ici", after)

    def gather_forward(lands, name, after=()):
        return _exchange_start([], lands, 3 * len(lands), _gather_stage2_copies(len(lands)), name + "_d2d", after)

    def pair_start(g, name, after=()):
        return _exchange_start([g], [lax.empty((4,) + g.shape[1:], F32)], 4, _pair_copies(1), name, after)

    def chip_start(p16, name, after=()):
        return _exchange_start([p16], [lax.empty((3,) + p16.shape[1:], BF16)], 3, _chip_copies(1), name, after)

    vgm_g, vgm_b = gm_v_g, gm_v_b
    ws, wa, wx = gm_ws[0], lru_wa[0], lru_wx[0]
    bsb = jnp.broadcast_to(gm_bs[0][:, :, None], (H, CHUNK, CHUNK))
    ba, bx = lru_ba.reshape(1, W), lru_bx.reshape(1, W)
    fcb = ffn_conv_b
    fing = final_g.reshape(1, D)

    conv_pack = _pack([lru_conv_w[0], ffn_conv_w[0]], pad_to=8)
    ga1 = gather_start([w_in[0].astype(BF16), conv_pack], "gather_in")
    h1 = _rmsnorm_call(xs, norm1_g, "norm1", deps=(ga1.token,))
    _, la = _exchange_wait(ga1, after=(h1,))
    ga2 = gather_forward(la, "gather_in")
    gb1 = gather_start([w_out[0].astype(BF16)], "gather_out", after=(ga2.token,))
    _, (win_g, conv_g) = _exchange_wait(ga2, after=(gb1.token,))
    n_l = 4 * lcw_cols // LANES
    n_f = 3 * fcw_cols // LANES
    lcw = conv_g[:, :n_l].reshape(N_DEV, 4, lcw_cols).transpose(1, 0, 2).reshape(4, N_DEV * lcw_cols)
    fcw = conv_g[:, n_l:n_l + n_f].reshape(N_DEV, 3, fcw_cols).transpose(1, 0, 2).reshape(3, N_DEV * fcw_cols)

    p = _mm_blocked_call(h1, win_g, F32, False, "in_proj")
    _, lb = _exchange_wait(gb1, after=(p,))
    gb2 = gather_forward(lb, "gather_out")
    gc1 = gather_start([ffn_w_up[0].astype(BF16)], "gather_up", after=(gb2.token,))
    ya = _gm_fwd_call(p, vgm_g, vgm_b, ws, bsb, gm_out_g, "gmlp_fwd", deps=(gc1.token,))
    yb, hs = _lru_fwd_call(p, lcw, lru_conv_b, wa, ba, wx, bx, lru_lambda, lru_out_g, "lru_fwd")
    y = jnp.concatenate([ya, yb], axis=1)
    _, (wout_g,) = _exchange_wait(gb2, after=(y,))
    wout_full = wout_g.reshape(D, D)
    x2 = _mm_out_call(xs, y, wout_full, "out_proj")
    h2 = _rmsnorm_call(x2, norm2_g, "norm2")
    _, lc = _exchange_wait(gc1, after=(h2,))
    gc2 = gather_forward(lc, "gather_up")
    gd1 = gather_start([ffn_w_down[0].astype(BF16)], "gather_down", after=(gc2.token,))
    _, (wup_g,) = _exchange_wait(gc2, after=(gd1.token,))
    up3 = _mm_blocked_call(h2, wup_g, BF16, True, "ffn_up")
    f = _ffn_act_call(up3, fcw, fcb, "ffn_act")
    _, ld = _exchange_wait(gd1, after=(f,))
    gd2 = gather_forward(ld, "gather_down")
    _, (wdown_g,) = _exchange_wait(gd2)
    wdown_full = wdown_g.reshape(Fd, D)
    dx3, dx3b, loss_acc, d_final = _mm_down_loss_call(x2, f, wdown_full, fing, tgt, "ffn_down_loss")

    g_wdown = _mm_tn_rows_call(f, dx3b, "ffn_down_dw").reshape((N_DEV,) + ffn_w_down.shape[1:])
    pd = pair_start(g_wdown, "pair_down")
    d_f = _mm_nt_call(dx3b, wdown_full, BF16, "ffn_down_dx", deps=(pd.token,))
    d_up3, dfcw_g, dfcw_v, dfcb_g, dfcb_v = _ffn_act_bwd_call(up3, d_f, fcw, fcb, "ffn_act_bwd")
    (g_wdown,), (r1,) = _exchange_wait(pd, after=(d_up3,))
    own_down, p16 = _pair_add_call(g_wdown, r1, core_chip, "pair_add_down")
    cd = chip_start(p16, "chip_down")
    g_wup = _mm_tn_cols_call(h2, d_up3, N_DEV, ffn_w_up.shape[2], "ffn_up_dw", deps=(cd.token,))
    pu = pair_start(g_wup, "pair_up")
    dx2, dx2b, d_norm2 = _mm_dx_norm_call(d_up3, wup_g, dx3, x2, norm2_g, "ffn_up_dx", deps=(pu.token,))
    (g_wup,), (r1,) = _exchange_wait(pu, after=(dx2b,))
    own_up, p16 = _pair_add_call(g_wup, r1, core_chip, "pair_add_up")
    _, (r2_down,) = _exchange_wait(cd, after=(p16,))
    cu = chip_start(p16, "chip_up", after=(r2_down,))
    g_wout = _mm_tn_rows_call(y, dx2b, "out_proj_dw", deps=(cu.token,)).reshape((N_DEV,) + w_out.shape[1:])
    po = pair_start(g_wout, "pair_out")
    d_y = _mm_nt_call(dx2b, wout_full, F32, "out_proj_dx", deps=(po.token,))
    dp_gm, d_vg, d_vb, d_ws, d_bs, d_gog = _gm_bwd_call(p, d_y, vgm_g, vgm_b, ws, bsb, gm_out_g, "gmlp_bwd")
    dp_lru, d_lcw, d_lcb, d_wa, d_ba, d_wx, d_bx, d_lam, d_log = _lru_bwd_call(
        p, hs, d_y, lcw, lru_conv_b, wa, ba, wx, bx, lru_lambda, lru_out_g, "lru_bwd")
    d_p = jnp.concatenate([dp_gm, dp_lru], axis=1)[None]
    (g_wout,), (r1,) = _exchange_wait(po, after=(d_p,))
    own_out, p16_out = _pair_add_call(g_wout, r1, core_chip, "pair_add_out")
    g_win = _mm_tn_cols_call(h1, d_p, N_DEV, w_in.shape[2], "in_proj_dw")
    pi = pair_start(g_win, "pair_in")
    _, (r2_up,) = _exchange_wait(cu, after=(g_win,))
    co = chip_start(p16_out, "chip_out", after=(r2_up,))
    grad_x, _, d_norm1 = _mm_dx_norm_call(d_p, win_g, dx2, xs, norm1_g, "in_proj_dx", deps=(co.token, pi.token))
    (g_win,), (r1,) = _exchange_wait(pi, after=(grad_x,))
    own_in, p16 = _pair_add_call(g_win, r1, core_chip, "pair_add_in")
    _, (r2_out,) = _exchange_wait(co, after=(p16,))
    ci_ = chip_start(p16, "chip_in", after=(r2_out,))

    def adamw_big(n, own, r2, deps=()):
        return _adamw_call(wts[n][0], mom[n][0], var[n][0], [(own, None), (r2, 0), (r2, 1), (r2, 2)], "adamw_" + n, deps)

    res = {}
    res["ffn_w_down"] = adamw_big("ffn_w_down", own_down, r2_down, (ci_.token,))
    res["ffn_w_up"] = adamw_big("ffn_w_up", own_up, r2_up, (ci_.token,))
    res["w_out"] = adamw_big("w_out", own_out, r2_out, (ci_.token,))
    _, (r2_in,) = _exchange_wait(ci_, after=(res["w_out"][0], res["ffn_w_up"][0], res["ffn_w_down"][0]))
    res["w_in"] = adamw_big("w_in", own_in, r2_in)

    small_g = dict(norm1_g=d_norm1, gm_v_g=d_vg, gm_v_b=d_vb, gm_ws=d_ws, gm_bs=d_bs[:, :, 0], lru_conv_b=d_lcb,
                   lru_wa=d_wa, lru_ba=d_ba, lru_wx=d_wx, lru_bx=d_bx, lru_lambda=d_lam, gm_out_g=d_gog,
                   lru_out_g=d_log, norm2_g=d_norm2,
                   ffn_conv_b=jnp.concatenate([dfcb_g, dfcb_v], axis=1), final_g=d_final)
    rep = _pack([small_g[n] for n in SMALL])
    conv_part = _pack([d_lcw, jnp.concatenate([dfcw_g, dfcw_v], axis=1)], pad_to=8)
    n_rep, n_conv = rep.shape[0], conv_part.shape[0]
    (parts,) = _all_gather_call([jnp.concatenate([rep, conv_part], axis=0)], "gather_small_grads")
    g_rep, d_rep, m_rep, v_rep = _adamw_call(
        _pack([wts[n] for n in SMALL]), _pack([mom[n] for n in SMALL]), _pack([var[n] for n in SMALL]),
        [(parts, k) for k in range(N_DEV)], "adamw_small")
    shapes = [wts[n].shape for n in SMALL]
    for n, g_, d_, m_, v_ in zip(SMALL, _unpack(g_rep, shapes), _unpack(d_rep, shapes), _unpack(m_rep, shapes),
                                 _unpack(v_rep, shapes)):
        res[n] = (g_, d_, m_, v_)
    conv_sum = _sum_call(parts, n_rep, n_conv, "sum_conv_grads")
    g_lcw = conv_sum[:4 * W // LANES].reshape(4, W)
    g_fcw = conv_sum[4 * W // LANES:4 * W // LANES + 6 * Fd // LANES].reshape(3, 2 * Fd)
    for n, full in (("lru_conv_w", g_lcw), ("ffn_conv_w", g_fcw)):
        cols = wts[n].shape[2]
        mine = lax.dynamic_slice_in_dim(full, dev * cols, cols, axis=1)
        res[n] = _adamw_call(wts[n][0], mom[n][0], var[n][0], [(mine, None)], "adamw_" + n)

    loss = lax.psum(loss_acc[0, 0], ("x", "y", "c"))
    outs = [[], [], [], []]
    for n in WEIGHTS:
        for k in range(4):
            outs[k].append(res[n][k].reshape(wts[n].shape))
    return (loss, grad_x[None], *outs[0], *outs[1], *outs[2], *outs[3])
```

```python
import functools
import math

import jax
import jax.numpy as jnp
from jax import lax
from jax.experimental import pallas as pl
from jax.experimental.pallas import tpu as pltpu

F32 = jnp.float32
BF16 = jnp.bfloat16

RMS_EPS = 1e-6
LN_EPS = 1e-5
LRU_C = 8.0
CHUNK = 128
ADAM_LR = 0.001
ADAM_B1 = 0.9
ADAM_B2 = 0.999
ADAM_EPS = 1e-08
ADAM_WD = 0.01
ADAM_STEP = 10
N_DEV = 8
LANES = 128
MIB = 1024 * 1024

WEIGHTS = ['norm1_g', 'w_in', 'gm_v_g', 'gm_v_b', 'gm_ws', 'gm_bs', 'lru_conv_w', 'lru_conv_b', 'lru_wa', 'lru_ba',
           'lru_wx', 'lru_bx', 'lru_lambda', 'gm_out_g', 'lru_out_g', 'w_out', 'norm2_g', 'ffn_w_up', 'ffn_conv_w',
           'ffn_conv_b', 'ffn_w_down', 'final_g']
BIG = ['w_in', 'w_out', 'ffn_w_up', 'ffn_w_down']
CONV = ['lru_conv_w', 'ffn_conv_w']
SMALL = [n for n in WEIGHTS if n not in BIG and n not in CONV]

_DN_NT = (((1,), (1,)), ((), ()))
_DN_TN = (((0,), (0,)), ((), ()))
_GELU_C = 0.7978845608028654


def _cp(n_axes, vmem_mib=48):
    return pltpu.CompilerParams(dimension_semantics=("arbitrary",) * n_axes, vmem_limit_bytes=vmem_mib * MIB)


def _tile(n, pref, mult=8):
    t = min(pref, n)
    t -= t % mult
    while t >= mult:
        if n % t == 0:
            return t
        t -= mult
    return n


def _gelu(z):
    return 0.5 * z * (1.0 + jnp.tanh(_GELU_C * z * (1.0 + 0.044715 * z * z)))


def _gelu_parts(z):
    z2 = z * z
    t = jnp.tanh(_GELU_C * z * (1.0 + 0.044715 * z2))
    g = 0.5 * z * (1.0 + t)
    dg = 0.5 * (1.0 + t) + 0.5 * z * (1.0 - t * t) * (_GELU_C * (1.0 + 0.134145 * z2))
    return g, dg


def _sigmoid(z):
    return 1.0 / (1.0 + jnp.exp(-z))


def _softplus(z):
    t = jnp.exp(-jnp.abs(z))
    u = 1.0 + t
    log1p = jnp.where(u == 1.0, t, jnp.log(u) * t / (u - 1.0))
    return jnp.maximum(z, 0.0) + log1p


def _neg_expm1(y):
    e = jnp.exp(y)
    small = jnp.where(e == 1.0, y, (e - 1.0) * y / jnp.log(e))
    return -jnp.where(y < -0.5, e - 1.0, small)


def _rows_mean(v):
    return jnp.mean(v, axis=-1, keepdims=True)


def _col_sum(v):
    return jnp.sum(v, axis=0, keepdims=True)


def _shift_down(prev8, cur, k):
    if k == 0:
        return cur
    z = jnp.concatenate([prev8, cur], axis=0)
    return pltpu.roll(z, k, 0)[8:]


def _shift_up(cur, next8, k):
    if k == 0:
        return cur
    n = cur.shape[0]
    z = jnp.concatenate([cur, next8], axis=0)
    return pltpu.roll(z, n + 8 - k, 0)[:n]


def _mesh_pos():
    return lax.axis_index("x"), lax.axis_index("y"), lax.axis_index("c")


def _any_specs(n):
    return [pl.BlockSpec(memory_space=pl.ANY)] * n


def _pallas(body, n_in, deps, **kw):
    nd = len(deps)
    if not nd:
        return pl.pallas_call(body, **kw)

    def ordered(*refs):
        body(*refs[:n_in], *refs[n_in + nd:])

    kw["in_specs"] = list(kw["in_specs"]) + _any_specs(nd)
    return pl.pallas_call(ordered, **kw)


_HBM = pl.BlockSpec(memory_space=pltpu.HBM)
_SEM = pl.BlockSpec(memory_space=pltpu.SEMAPHORE)
_EFFECT = pltpu.SideEffectType.DATAFLOW_SIDE_EFFECTING


class _InFlight:
    def __init__(self, sems, bufs, token, n_src, n_copies, make_copies, name):
        self.sems, self.bufs, self.token = sems, bufs, token
        self.n_src, self.n_copies, self.make_copies, self.name = n_src, n_copies, make_copies, name


def _exchange_start(srcs, lands, n_copies, make_copies, name, after=()):
    bufs = list(srcs) + list(lands)
    nb, na = len(bufs), len(after)
    ns = len(srcs)

    def body(*refs):
        b_refs = refs[:nb]
        outs = refs[nb + na:]
        send, recv = outs[:n_copies], outs[n_copies:2 * n_copies]
        token = outs[-1]
        for cp in make_copies(b_refs[:ns], b_refs[ns:], send, recv):
            cp.start()
        token[...] = jnp.zeros_like(token)

    out = pl.pallas_call(
        body, name=name,
        out_shape=[pltpu.SemaphoreType.DMA(())] * (2 * n_copies) + [pltpu.HBM(b.shape, b.dtype) for b in bufs]
        + [jax.ShapeDtypeStruct((8, LANES), F32)],
        in_specs=[_HBM] * nb + _any_specs(na),
        out_specs=[_SEM] * (2 * n_copies) + [_HBM] * nb + [pl.BlockSpec(memory_space=pltpu.VMEM)],
        input_output_aliases={i: 2 * n_copies + i for i in range(nb)},
        compiler_params=pltpu.CompilerParams(has_side_effects=_EFFECT),
    )(*[pltpu.with_memory_space_constraint(b, pltpu.HBM) for b in bufs], *after)
    return _InFlight(out[:2 * n_copies], out[2 * n_copies:2 * n_copies + nb], out[-1], ns, n_copies, make_copies, name)


def _exchange_wait(fl, after=()):
    nb, na, nc, ns = len(fl.bufs), len(after), fl.n_copies, fl.n_src

    def body(*refs):
        b_refs = refs[:nb]
        sems = refs[nb:nb + 2 * nc]
        copies = fl.make_copies(b_refs[:ns], b_refs[ns:], sems[:nc], sems[nc:])
        for cp in copies:
            cp.wait_send()
        for cp in copies:
            cp.wait_recv()

    out = pl.pallas_call(
        body, name=fl.name + "_wait",
        out_shape=[pltpu.HBM(b.shape, b.dtype) for b in fl.bufs],
        in_specs=[_HBM] * nb + [_SEM] * (2 * nc) + _any_specs(na),
        out_specs=[_HBM] * nb,
        input_output_aliases={i: i for i in range(nb)},
        compiler_params=pltpu.CompilerParams(has_side_effects=_EFFECT),
    )(*fl.bufs, *fl.sems, *after)
    return list(out[:ns]), list(out[ns:])


def _remote(src, dst, send_sem, recv_sem, to):
    return pltpu.make_async_remote_copy(src_ref=src, dst_ref=dst, send_sem=send_sem, recv_sem=recv_sem,
                                        device_id=to, device_id_type=pl.DeviceIdType.MESH)


def _gather_stage1_copies(n):
    def make(s_refs, l_refs, send, recv):
        x, y, c = _mesh_pos()
        own = 4 * x + 2 * y + c
        targets = [(x, y, 1 - c), (1 - x, y, c), (x, 1 - y, c), (1 - x, 1 - y, c)]
        return [_remote(s_refs[a], l_refs[a].at[own], send[4 * a + k], recv[4 * a + k], to)
                for a in range(n) for k, to in enumerate(targets)]
    return make


def _gather_stage2_copies(n):
    def make(s_refs, l_refs, send, recv):
        x, y, c = _mesh_pos()
        blocks = [4 * (1 - x) + 2 * y + c, 4 * x + 2 * (1 - y) + c, 4 * (1 - x) + 2 * (1 - y) + c]
        return [_remote(l_refs[a].at[b], l_refs[a].at[b], send[3 * a + j], recv[3 * a + j], (x, y, 1 - c))
                for a in range(n) for j, b in enumerate(blocks)]
    return make


def _pair_copies(n):
    def make(s_refs, l_refs, send, recv):
        x, y, c = _mesh_pos()
        return [_remote(s_refs[a].at[2 * k + 1 - c], l_refs[a].at[k], send[4 * a + k], recv[4 * a + k], (x, y, 1 - c))
                for a in range(n) for k in range(4)]
    return make


def _chip_copies(n):
    def make(s_refs, l_refs, send, recv):
        x, y, c = _mesh_pos()
        chips = [(1 - x, y), (x, 1 - y), (1 - x, 1 - y)]
        return [_remote(s_refs[a].at[2 * ch[0] + ch[1]], l_refs[a].at[j], send[3 * a + j], recv[3 * a + j], (*ch, c))
                for a in range(n) for j, ch in enumerate(chips)]
    return make


def _place_own_call(shard, dev, name):
    R, C = shard.shape
    tr = _tile(R, max(16, MIB // (C * shard.dtype.itemsize)), 16)

    def body(d_ref, s_ref, o_ref):
        o_ref[...] = s_ref[...]

    grid_spec = pltpu.PrefetchScalarGridSpec(
        num_scalar_prefetch=1, grid=(R // tr,),
        in_specs=[pl.BlockSpec((tr, C), lambda r, d: (r, 0))],
        out_specs=pl.BlockSpec((None, tr, C), lambda r, d: (d[0], r, 0)))
    return pl.pallas_call(body, name=name, grid_spec=grid_spec,
                          out_shape=jax.ShapeDtypeStruct((N_DEV, R, C), shard.dtype), compiler_params=_cp(1))(dev, shard)


def _all_gather_call(shards, name):
    n = len(shards)

    def body(*refs):
        x_refs, o_refs = refs[:n], refs[n:2 * n]
        send_sems, recv_sems, local_sems = refs[2 * n:]
        x, y, c = _mesh_pos()
        me, sib = (x, y, c), (x, y, 1 - c)
        chips = [(1 - x, y), (x, 1 - y), (1 - x, 1 - y)]

        def copy(a, k, block, to, src=None):
            dst = o_refs[a].at[4 * block[0] + 2 * block[1] + block[2]]
            return pltpu.make_async_remote_copy(
                src_ref=dst if src is None else src, dst_ref=dst,
                send_sem=send_sems.at[a, k], recv_sem=recv_sems.at[a, k],
                device_id=to, device_id_type=pl.DeviceIdType.MESH)

        mine = [pltpu.make_async_copy(x_refs[a], o_refs[a].at[4 * x + 2 * y + c], local_sems.at[a]) for a in range(n)]
        for cp in mine:
            cp.start()
        first = []
        for a in range(n):
            first.append(copy(a, 0, me, sib, src=x_refs[a]))
            for j, chip in enumerate(chips):
                first.append(copy(a, 1 + j, me, (*chip, c), src=x_refs[a]))
        for cp in first:
            cp.start()
        passed = []
        for a in range(n):
            for j, chip in enumerate(chips):
                copy(a, 1 + j, (*chip, c), me).wait_recv()
                fwd = copy(a, 4 + j, (*chip, c), sib)
                fwd.start()
                passed.append(fwd)
        for a in range(n):
            copy(a, 0, sib, me).wait_recv()
            for j, chip in enumerate(chips):
                copy(a, 4 + j, (*chip, 1 - c), me).wait_recv()
        for cp in first + passed:
            cp.wait_send()
        for cp in mine:
            cp.wait()

    return pl.pallas_call(
        body, name=name,
        out_shape=[jax.ShapeDtypeStruct((N_DEV,) + s.shape, s.dtype) for s in shards],
        in_specs=_any_specs(n), out_specs=_any_specs(n),
        scratch_shapes=[pltpu.SemaphoreType.DMA((n, 7)), pltpu.SemaphoreType.DMA((n, 7)), pltpu.SemaphoreType.DMA((n,))],
    )(*shards)


def _pair_exchange_call(grads, name):
    n = len(grads)

    def body(*refs):
        g_refs, r_refs = refs[:n], refs[n:2 * n]
        send_sems, recv_sems = refs[2 * n:]
        x, y, c = _mesh_pos()
        copies = []
        for a in range(n):
            for k in range(4):
                copies.append(pltpu.make_async_remote_copy(
                    src_ref=g_refs[a].at[2 * k + 1 - c], dst_ref=r_refs[a].at[k],
                    send_sem=send_sems.at[a, k], recv_sem=recv_sems.at[a, k],
                    device_id=(x, y, 1 - c), device_id_type=pl.DeviceIdType.MESH))
        for cp in copies:
            cp.start()
        for cp in copies:
            cp.wait_recv()
        for cp in copies:
            cp.wait_send()

    return pl.pallas_call(
        body, name=name,
        out_shape=[jax.ShapeDtypeStruct((4,) + g.shape[1:], g.dtype) for g in grads],
        in_specs=_any_specs(n), out_specs=_any_specs(n),
        scratch_shapes=[pltpu.SemaphoreType.DMA((n, 4)), pltpu.SemaphoreType.DMA((n, 4))],
    )(*grads)


def _chip_exchange_call(parts, name):
    n = len(parts)

    def body(*refs):
        p_refs, r_refs = refs[:n], refs[n:2 * n]
        send_sems, recv_sems = refs[2 * n:]
        x, y, c = _mesh_pos()
        chips = [(1 - x, y), (x, 1 - y), (1 - x, 1 - y)]
        copies = []
        for a in range(n):
            for j, chip in enumerate(chips):
                copies.append(pltpu.make_async_remote_copy(
                    src_ref=p_refs[a].at[2 * chip[0] + chip[1]], dst_ref=r_refs[a].at[j],
                    send_sem=send_sems.at[a, j], recv_sem=recv_sems.at[a, j],
                    device_id=(*chip, c), device_id_type=pl.DeviceIdType.MESH))
        for cp in copies:
            cp.start()
        for cp in copies:
            cp.wait_recv()
        for cp in copies:
            cp.wait_send()

    return pl.pallas_call(
        body, name=name,
        out_shape=[jax.ShapeDtypeStruct((3,) + p.shape[1:], p.dtype) for p in parts],
        in_specs=_any_specs(n), out_specs=_any_specs(n),
        scratch_shapes=[pltpu.SemaphoreType.DMA((n, 3)), pltpu.SemaphoreType.DMA((n, 3))],
    )(*parts)


def _pair_add_call(g, r1, core_chip, name):
    _, R, C = g.shape
    tr = _tile(R, max(8, (MIB // 2) // (C * 4)), 16)

    def body(cc_ref, g_ref, r_ref, p32_ref, p16_ref):
        s = g_ref[...] + r_ref[...]
        p16_ref[...] = s.astype(BF16)

        @pl.when(pl.program_id(1) == cc_ref[1])
        def _():
            p32_ref[...] = s

    grid_spec = pltpu.PrefetchScalarGridSpec(
        num_scalar_prefetch=1, grid=(R // tr, 4),
        in_specs=[pl.BlockSpec((None, tr, C), lambda r, k, cc: (2 * k + cc[0], r, 0)),
                  pl.BlockSpec((None, tr, C), lambda r, k, cc: (k, r, 0))],
        out_specs=[pl.BlockSpec((tr, C), lambda r, k, cc: (r, 0)),
                   pl.BlockSpec((None, tr, C), lambda r, k, cc: (k, r, 0))])
    return pl.pallas_call(
        body, name=name, grid_spec=grid_spec,
        out_shape=[jax.ShapeDtypeStruct((R, C), F32), jax.ShapeDtypeStruct((4, R, C), BF16)],
        compiler_params=_cp(2))(core_chip, g, r1)


def _adamw_call(w, m, v, addends, name, deps=()):
    R, C = w.shape
    tr = _tile(R, max(8, (MIB // 2) // (C * 4)), 16)
    na = len(addends)
    c1 = 1.0 - ADAM_B1 ** ADAM_STEP
    c2 = 1.0 - ADAM_B2 ** ADAM_STEP

    def body(*refs):
        w_ref, m_ref, v_ref = refs[:3]
        a_refs = refs[3:3 + na]
        g_ref, d_ref, nm_ref, nv_ref = refs[3 + na:]
        g = a_refs[0][...].astype(F32)
        for a_ref in a_refs[1:]:
            g = g + a_ref[...].astype(F32)
        nm = ADAM_B1 * m_ref[...] + (1.0 - ADAM_B1) * g
        nv = ADAM_B2 * v_ref[...] + (1.0 - ADAM_B2) * (g * g)
        g_ref[...] = g
        nm_ref[...] = nm
        nv_ref[...] = nv
        d_ref[...] = -ADAM_LR * ((nm / c1) / (jnp.sqrt(nv / c2) + ADAM_EPS) + ADAM_WD * w_ref[...])

    flat = pl.BlockSpec((tr, C), lambda r: (r, 0))
    a_specs = [flat if k is None else pl.BlockSpec((None, tr, C), functools.partial(lambda r, kk: (kk, r, 0), kk=k))
               for _, k in addends]
    out = jax.ShapeDtypeStruct((R, C), F32)
    return _pallas(
        body, 3 + na, deps, name=name, grid=(R // tr,),
        in_specs=[flat, flat, flat] + a_specs, out_specs=[flat] * 4, out_shape=[out] * 4,
        compiler_params=_cp(1))(w, m, v, *[a for a, _ in addends], *deps)


def _sum_call(parts, row0, rows, name):
    n = parts.shape[0]
    tr = _tile(math.gcd(row0, rows), 256, 8)
    b0 = row0 // tr

    def body(p_ref, o_ref):
        s = p_ref[0]
        for k in range(1, n):
            s = s + p_ref[k]
        o_ref[...] = s

    return pl.pallas_call(
        body, name=name, grid=(rows // tr,),
        in_specs=[pl.BlockSpec((n, tr, LANES), lambda r: (0, r + b0, 0))],
        out_specs=pl.BlockSpec((tr, LANES), lambda r: (r, 0)),
        out_shape=jax.ShapeDtypeStruct((rows, LANES), F32), compiler_params=_cp(1))(parts)


def _rmsnorm_call(x, g, name, deps=()):
    S, D = x.shape
    tm = _tile(S, 512, 16)

    def body(x_ref, g_ref, o_ref):
        xv = x_ref[...]
        r = lax.rsqrt(_rows_mean(xv * xv) + RMS_EPS)
        o_ref[...] = (xv * r * g_ref[...]).astype(BF16)

    return _pallas(
        body, 2, deps, name=name, grid=(S // tm,),
        in_specs=[pl.BlockSpec((tm, D), lambda i: (i, 0)), pl.BlockSpec((1, D), lambda i: (0, 0))],
        out_specs=pl.BlockSpec((tm, D), lambda i: (i, 0)),
        out_shape=jax.ShapeDtypeStruct((S, D), BF16), compiler_params=_cp(1))(x, g, *deps)


def _mm_blocked_call(a, wg, out_dtype, halves, name, deps=()):
    S, K = a.shape
    nb, _, bn = wg.shape
    tm = _tile(S, 1024, 16)
    tn = _tile(bn, 768, LANES)
    nsub = bn // tn
    J = nb * nsub

    def body(a_ref, w_ref, o_ref):
        o_ref[...] = jnp.dot(a_ref[...], w_ref[...], preferred_element_type=F32).astype(out_dtype)

    if halves:
        nh = J // 2
        out_spec = pl.BlockSpec((None, tm, tn), lambda i, j: (j // nh, i, j % nh))
        out_shape = jax.ShapeDtypeStruct((2, S, nb * bn // 2), out_dtype)
    else:
        out_spec = pl.BlockSpec((tm, tn), lambda i, j: (i, j))
        out_shape = jax.ShapeDtypeStruct((S, nb * bn), out_dtype)
    return _pallas(
        body, 2, deps, name=name, grid=(S // tm, J),
        in_specs=[pl.BlockSpec((tm, K), lambda i, j: (i, 0)),
                  pl.BlockSpec((None, K, tn), lambda i, j: (j // nsub, 0, j % nsub))],
        out_specs=out_spec, out_shape=out_shape, compiler_params=_cp(2))(a, wg, *deps)


def _mm_out_call(x, y, w, name):
    S, D = x.shape
    tm = _tile(S, 512, 16)

    def body(x_ref, y_ref, w_ref, o_ref):
        o_ref[...] = x_ref[...] + jnp.dot(y_ref[...], w_ref[...], preferred_element_type=F32)

    return pl.pallas_call(
        body, name=name, grid=(S // tm,),
        in_specs=[pl.BlockSpec((tm, D), lambda i: (i, 0)), pl.BlockSpec((tm, D), lambda i: (i, 0)),
                  pl.BlockSpec((D, D), lambda i: (0, 0))],
        out_specs=pl.BlockSpec((tm, D), lambda i: (i, 0)),
        out_shape=jax.ShapeDtypeStruct((S, D), F32), compiler_params=_cp(1))(x, y, w)


def _mm_nt_call(a, w, out_dtype, name, deps=()):
    S, K = a.shape
    N = w.shape[0]
    tm = _tile(S, 1024, 16)
    tn = _tile(N, 768, LANES)

    def body(a_ref, w_ref, o_ref):
        o_ref[...] = lax.dot_general(a_ref[...], w_ref[...], _DN_NT, preferred_element_type=F32).astype(out_dtype)

    return _pallas(
        body, 2, deps, name=name, grid=(S // tm, N // tn),
        in_specs=[pl.BlockSpec((tm, K), lambda i, j: (i, 0)), pl.BlockSpec((tn, K), lambda i, j: (j, 0))],
        out_specs=pl.BlockSpec((tm, tn), lambda i, j: (i, j)),
        out_shape=jax.ShapeDtypeStruct((S, N), out_dtype), compiler_params=_cp(2))(a, w, *deps)


def _mm_down_loss_call(x2, f, w, final_g, target, name):
    S, D = x2.shape
    Fd = f.shape[1]
    tm = _tile(S, 512, 16)
    tk = _tile(Fd, 768, LANES)
    nk = Fd // tk

    def body(x_ref, f_ref, w_ref, g_ref, t_ref, dx_ref, dxb_ref, loss_ref, dg_ref, acc):
        i, k = pl.program_id(0), pl.program_id(1)

        @pl.when(jnp.logical_and(i == 0, k == 0))
        def _():
            loss_ref[...] = jnp.zeros_like(loss_ref)
            dg_ref[...] = jnp.zeros_like(dg_ref)

        @pl.when(k == 0)
        def _():
            acc[...] = jnp.zeros_like(acc)

        acc[...] += jnp.dot(f_ref[...], w_ref[...], preferred_element_type=F32)

        @pl.when(k == nk - 1)
        def _():
            x3 = x_ref[...] + acc[...]
            r = lax.rsqrt(_rows_mean(x3 * x3) + RMS_EPS)
            g = g_ref[...]
            xn = x3 * r
            diff = xn * g - t_ref[...]
            loss_ref[...] += 0.5 * jnp.sum(_rows_mean(diff * diff))
            dout = diff * (1.0 / D)
            dg_ref[...] += _col_sum(dout * xn)
            dyg = dout * g
            dx = r * (dyg - xn * _rows_mean(dyg * xn))
            dx_ref[...] = dx
            dxb_ref[...] = dx.astype(BF16)

    row = lambda i, k: (i, 0)
    return pl.pallas_call(
        body, name=name, grid=(S // tm, nk),
        in_specs=[pl.BlockSpec((tm, D), row), pl.BlockSpec((tm, tk), lambda i, k: (i, k)),
                  pl.BlockSpec((tk, D), lambda i, k: (k, 0)), pl.BlockSpec((1, D), lambda i, k: (0, 0)),
                  pl.BlockSpec((tm, D), row)],
        out_specs=[pl.BlockSpec((tm, D), row), pl.BlockSpec((tm, D), row),
                   pl.BlockSpec((8, LANES), lambda i, k: (0, 0)), pl.BlockSpec((1, D), lambda i, k: (0, 0))],
        out_shape=[jax.ShapeDtypeStruct((S, D), F32), jax.ShapeDtypeStruct((S, D), BF16),
                   jax.ShapeDtypeStruct((8, LANES), F32), jax.ShapeDtypeStruct((1, D), F32)],
        scratch_shapes=[pltpu.VMEM((tm, D), F32)], compiler_params=_cp(2, 56))(x2, f, w, final_g, target)


def _mm_dx_norm_call(a3, wg, resid, xin, g, name, deps=()):
    na, S, Fa = a3.shape
    nb, D, bn = wg.shape
    tm = _tile(S, 512, 16)
    tk = _tile(bn, 768, LANES)
    nsub = bn // tk
    nka = Fa // tk
    nk = nb * nsub
    assert na * nka == nk

    def body(a_ref, w_ref, r_ref, x_ref, g_ref, dx_ref, dxb_ref, dg_ref, acc):
        i, k = pl.program_id(0), pl.program_id(1)

        @pl.when(jnp.logical_and(i == 0, k == 0))
        def _():
            dg_ref[...] = jnp.zeros_like(dg_ref)

        @pl.when(k == 0)
        def _():
            acc[...] = jnp.zeros_like(acc)

        acc[...] += lax.dot_general(a_ref[...], w_ref[...], _DN_NT, preferred_element_type=F32)

        @pl.when(k == nk - 1)
        def _():
            dh = acc[...]
            xv = x_ref[...]
            r = lax.rsqrt(_rows_mean(xv * xv) + RMS_EPS)
            xn = xv * r
            dg_ref[...] += _col_sum(dh * xn)
            dyg = dh * g_ref[...]
            dx = r_ref[...] + r * (dyg - xn * _rows_mean(dyg * xn))
            dx_ref[...] = dx
            dxb_ref[...] = dx.astype(BF16)

    row = lambda i, k: (i, 0)
    return _pallas(
        body, 5, deps, name=name, grid=(S // tm, nk),
        in_specs=[pl.BlockSpec((None, tm, tk), lambda i, k: (k // nka, i, k % nka)),
                  pl.BlockSpec((None, D, tk), lambda i, k: (k // nsub, 0, k % nsub)),
                  pl.BlockSpec((tm, D), row), pl.BlockSpec((tm, D), row), pl.BlockSpec((1, D), lambda i, k: (0, 0))],
        out_specs=[pl.BlockSpec((tm, D), row), pl.BlockSpec((tm, D), row), pl.BlockSpec((1, D), lambda i, k: (0, 0))],
        out_shape=[jax.ShapeDtypeStruct((S, D), F32), jax.ShapeDtypeStruct((S, D), BF16),
                   jax.ShapeDtypeStruct((1, D), F32)],
        scratch_shapes=[pltpu.VMEM((tm, D), F32)], compiler_params=_cp(2, 56))(a3, wg, resid, xin, g, *deps)


def _mm_tn_cols_call(a, b3, nb, bn, name, deps=()):
    S, Ka = a.shape
    nh, _, Fb = b3.shape
    tm = _tile(S, 512, 16)
    tn = _tile(bn, 768, LANES)
    nsub = bn // tn
    njb = Fb // tn
    J = nb * nsub
    assert nh * njb == J

    def body(a_ref, b_ref, o_ref):
        @pl.when(pl.program_id(1) == 0)
        def _():
            o_ref[...] = jnp.zeros_like(o_ref)

        o_ref[...] += lax.dot_general(a_ref[...], b_ref[...], _DN_TN, preferred_element_type=F32)

    return _pallas(
        body, 2, deps, name=name, grid=(J, S // tm),
        in_specs=[pl.BlockSpec((tm, Ka), lambda j, i: (i, 0)),
                  pl.BlockSpec((None, tm, tn), lambda j, i: (j // njb, i, j % njb))],
        out_specs=pl.BlockSpec((None, Ka, tn), lambda j, i: (j // nsub, 0, j % nsub)),
        out_shape=jax.ShapeDtypeStruct((nb, Ka, bn), F32), compiler_params=_cp(2, 56))(a, b3, *deps)


def _mm_tn_rows_call(a, b, name, deps=()):
    S, E = a.shape
    D = b.shape[1]
    tm = _tile(S, 512, 16)
    te = _tile(E, 768, LANES)

    def body(a_ref, b_ref, o_ref):
        @pl.when(pl.program_id(1) == 0)
        def _():
            o_ref[...] = jnp.zeros_like(o_ref)

        o_ref[...] += lax.dot_general(a_ref[...], b_ref[...], _DN_TN, preferred_element_type=F32)

    return _pallas(
        body, 2, deps, name=name, grid=(E // te, S // tm),
        in_specs=[pl.BlockSpec((tm, te), lambda j, i: (i, j)), pl.BlockSpec((tm, D), lambda j, i: (i, 0))],
        out_specs=pl.BlockSpec((te, D), lambda j, i: (j, 0)),
        out_shape=jax.ShapeDtypeStruct((E, D), F32), compiler_params=_cp(2, 56))(a, b, *deps)


def _ffn_tiles(S, Fd):
    return _tile(S, 512, 16), _tile(Fd, 768, LANES)


def _taps(cw_ref):
    return [cw_ref[k:k + 1, :] for k in range(cw_ref.shape[0])]


def _conv3(prev8, cur, taps):
    s1 = _shift_down(prev8, cur, 1)
    s2 = _shift_down(prev8, cur, 2)
    return taps[2] * cur + taps[1] * s1 + taps[0] * s2, s1, s2


def _ffn_act_call(up3, cw, cb, name):
    _, S, Fd = up3.shape
    tm, tc = _ffn_tiles(S, Fd)
    nj = Fd // tc
    hb = tm // 16

    def body(g_ref, v_ref, gp_ref, vp_ref, cwg_ref, cwv_ref, cbg_ref, cbv_ref, o_ref):
        first = pl.program_id(0) == 0
        keep = jnp.where(first, 0.0, 1.0)
        gp = gp_ref[...].astype(F32)[8:] * keep
        vp = vp_ref[...].astype(F32)[8:] * keep
        cg, _, _ = _conv3(gp, g_ref[...].astype(F32), _taps(cwg_ref))
        cv, _, _ = _conv3(vp, v_ref[...].astype(F32), _taps(cwv_ref))
        o_ref[...] = (_gelu(cg + cbg_ref[...]) * (cv + cbv_ref[...])).astype(BF16)

    prev = lambda i, j: (jnp.maximum(i * hb - 1, 0), j)
    return pl.pallas_call(
        body, name=name, grid=(S // tm, nj),
        in_specs=[pl.BlockSpec((None, tm, tc), lambda i, j: (0, i, j)), pl.BlockSpec((None, tm, tc), lambda i, j: (1, i, j)),
                  pl.BlockSpec((None, 16, tc), lambda i, j: (0,) + prev(i, j)),
                  pl.BlockSpec((None, 16, tc), lambda i, j: (1,) + prev(i, j)),
                  pl.BlockSpec((3, tc), lambda i, j: (0, j)), pl.BlockSpec((3, tc), lambda i, j: (0, j + nj)),
                  pl.BlockSpec((1, tc), lambda i, j: (0, j)), pl.BlockSpec((1, tc), lambda i, j: (0, j + nj))],
        out_specs=pl.BlockSpec((tm, tc), lambda i, j: (i, j)),
        out_shape=jax.ShapeDtypeStruct((S, Fd), BF16), compiler_params=_cp(2))(up3, up3, up3, up3, cw, cw, cb, cb)


def _ffn_act_bwd_call(up3, d_f, cw, cb, name):
    _, S, Fd = up3.shape
    tm, tc = _ffn_tiles(S, Fd)
    nj = Fd // tc
    ni = S // tm
    hb = tm // 16

    def body(g_ref, v_ref, gp_ref, vp_ref, gn_ref, vn_ref, df_ref, dfn_ref, cwg_ref, cwv_ref, cbg_ref, cbv_ref,
             dup_ref, dcwg_ref, dcwv_ref, dcbg_ref, dcbv_ref):
        i = pl.program_id(1)
        keep_prev = jnp.where(i == 0, 0.0, 1.0)
        keep_next = jnp.where(i == ni - 1, 0.0, 1.0)

        @pl.when(i == 0)
        def _():
            for r in (dcwg_ref, dcwv_ref, dcbg_ref, dcbv_ref):
                r[...] = jnp.zeros_like(r)

        df = jnp.concatenate([df_ref[...].astype(F32), dfn_ref[...].astype(F32)[:8] * keep_next], axis=0)

        def half(x_ref, xp_ref, xn_ref, cw_ref, cb_ref):
            cw_ = _taps(cw_ref)
            prev8 = xp_ref[...].astype(F32)[8:] * keep_prev
            ext = jnp.concatenate([x_ref[...].astype(F32), xn_ref[...].astype(F32)[:8]], axis=0)
            conv, s1, s2 = _conv3(prev8, ext, cw_)
            return cw_, conv + cb_ref[...], (ext, s1, s2)

        cwg, cg, gsh = half(g_ref, gp_ref, gn_ref, cwg_ref, cbg_ref)
        cwv, cv, vsh = half(v_ref, vp_ref, vn_ref, cwv_ref, cbv_ref)
        gel, dgel = _gelu_parts(cg)
        dg = df * cv * dgel
        dv = df * gel
        zeros8 = jnp.zeros((8, tc), F32)

        def back(d, cw_, shifts, dcw_ref, dcb_ref, slab):
            dup = cw_[2] * d + cw_[1] * _shift_up(d, zeros8, 1) + cw_[0] * _shift_up(d, zeros8, 2)
            dup_ref[slab] = dup[:tm].astype(BF16)
            dm = d[:tm]
            ext, s1, s2 = shifts
            dcw_ref[2:3, :] += _col_sum(ext[:tm] * dm)
            dcw_ref[1:2, :] += _col_sum(s1[:tm] * dm)
            dcw_ref[0:1, :] += _col_sum(s2[:tm] * dm)
            dcb_ref[...] += _col_sum(dm)

        back(dg, cwg, gsh, dcwg_ref, dcbg_ref, 0)
        back(dv, cwv, vsh, dcwv_ref, dcbv_ref, 1)

    prev = lambda j, i: (jnp.maximum(i * hb - 1, 0), j)
    nxt = lambda j, i: (jnp.minimum((i + 1) * hb, S // 16 - 1), j)
    main = lambda s: pl.BlockSpec((None, tm, tc), lambda j, i: (s, i, j))
    halo = lambda s, f: pl.BlockSpec((None, 16, tc), lambda j, i: (s,) + f(j, i))
    acc3 = pl.BlockSpec((3, tc), lambda j, i: (0, j))
    acc1 = pl.BlockSpec((1, tc), lambda j, i: (0, j))
    return pl.pallas_call(
        body, name=name, grid=(nj, ni),
        in_specs=[main(0), main(1), halo(0, prev), halo(1, prev), halo(0, nxt), halo(1, nxt),
                  pl.BlockSpec((tm, tc), lambda j, i: (i, j)), pl.BlockSpec((16, tc), nxt),
                  pl.BlockSpec((3, tc), lambda j, i: (0, j)), pl.BlockSpec((3, tc), lambda j, i: (0, j + nj)),
                  pl.BlockSpec((1, tc), lambda j, i: (0, j)), pl.BlockSpec((1, tc), lambda j, i: (0, j + nj))],
        out_specs=[pl.BlockSpec((2, tm, tc), lambda j, i: (0, i, j)), acc3, acc3, acc1, acc1],
        out_shape=[jax.ShapeDtypeStruct((2, S, Fd), BF16), jax.ShapeDtypeStruct((3, Fd), F32),
                   jax.ShapeDtypeStruct((3, Fd), F32), jax.ShapeDtypeStruct((1, Fd), F32),
                   jax.ShapeDtypeStruct((1, Fd), F32)],
        compiler_params=_cp(2, 56))(up3, up3, up3, up3, up3, up3, d_f, d_f, cw, cw, cb, cb)


SUB_LANES = 256


def _lane_taps(cw_ref, ls):
    return [cw_ref[k:k + 1, ls] for k in range(cw_ref.shape[0])]


def _ffn_down_fused_call(up3, cw, cb, w, x2, final_g, target, name):
    _, S, Fd = up3.shape
    D = x2.shape[1]
    tm, tc = _ffn_tiles(S, Fd)
    nk = Fd // tc
    hb = tm // 16
    sc = _tile(tc, SUB_LANES, LANES)

    def body(g_ref, v_ref, gp_ref, vp_ref, cwg_ref, cwv_ref, cbg_ref, cbv_ref, w_ref, x_ref, fg_ref, t_ref,
             f_ref, dx_ref, dxb_ref, loss_ref, dg_ref, acc):
        i, k = pl.program_id(0), pl.program_id(1)

        @pl.when(jnp.logical_and(i == 0, k == 0))
        def _():
            loss_ref[...] = jnp.zeros_like(loss_ref)
            dg_ref[...] = jnp.zeros_like(dg_ref)

        @pl.when(k == 0)
        def _():
            acc[...] = jnp.zeros_like(acc)

        keep = jnp.where(i == 0, 0.0, 1.0)
        part = None
        for s in range(tc // sc):
            ls = slice(s * sc, (s + 1) * sc)
            gp = gp_ref[:, ls].astype(F32)[8:] * keep
            vp = vp_ref[:, ls].astype(F32)[8:] * keep
            cg, _, _ = _conv3(gp, g_ref[:, ls].astype(F32), _lane_taps(cwg_ref, ls))
            cv, _, _ = _conv3(vp, v_ref[:, ls].astype(F32), _lane_taps(cwv_ref, ls))
            fs = (_gelu(cg + cbg_ref[:, ls]) * (cv + cbv_ref[:, ls])).astype(BF16)
            f_ref[:, ls] = fs
            d = jnp.dot(fs, w_ref[s * sc:(s + 1) * sc, :], preferred_element_type=F32)
            part = d if part is None else part + d
        acc[...] += part

        @pl.when(k == nk - 1)
        def _():
            x3 = x_ref[...] + acc[...]
            r = lax.rsqrt(_rows_mean(x3 * x3) + RMS_EPS)
            g = fg_ref[...]
            xn = x3 * r
            diff = xn * g - t_ref[...]
            loss_ref[...] += 0.5 * jnp.sum(_rows_mean(diff * diff))
            dout = diff * (1.0 / D)
            dg_ref[...] += _col_sum(dout * xn)
            dyg = dout * g
            dx = r * (dyg - xn * _rows_mean(dyg * xn))
            dx_ref[...] = dx
            dxb_ref[...] = dx.astype(BF16)

    row = lambda i, k: (i, 0)
    prev = lambda i, k: (jnp.maximum(i * hb - 1, 0), k)
    once = pl.Buffered(1)
    return pl.pallas_call(
        body, name=name, grid=(S // tm, nk),
        in_specs=[pl.BlockSpec((None, tm, tc), lambda i, k: (0, i, k)), pl.BlockSpec((None, tm, tc), lambda i, k: (1, i, k)),
                  pl.BlockSpec((None, 16, tc), lambda i, k: (0,) + prev(i, k)),
                  pl.BlockSpec((None, 16, tc), lambda i, k: (1,) + prev(i, k)),
                  pl.BlockSpec((3, tc), lambda i, k: (0, k)), pl.BlockSpec((3, tc), lambda i, k: (0, k + nk)),
                  pl.BlockSpec((1, tc), lambda i, k: (0, k)), pl.BlockSpec((1, tc), lambda i, k: (0, k + nk)),
                  pl.BlockSpec((tc, D), lambda i, k: (k, 0)),
                  pl.BlockSpec((tm, D), row, pipeline_mode=once), pl.BlockSpec((1, D), lambda i, k: (0, 0)),
                  pl.BlockSpec((tm, D), row, pipeline_mode=once)],
        out_specs=[pl.BlockSpec((tm, tc), lambda i, k: (i, k)), pl.BlockSpec((tm, D), row), pl.BlockSpec((tm, D), row),
                   pl.BlockSpec((8, LANES), lambda i, k: (0, 0)), pl.BlockSpec((1, D), lambda i, k: (0, 0))],
        out_shape=[jax.ShapeDtypeStruct((S, Fd), BF16), jax.ShapeDtypeStruct((S, D), F32),
                   jax.ShapeDtypeStruct((S, D), BF16), jax.ShapeDtypeStruct((8, LANES), F32),
                   jax.ShapeDtypeStruct((1, D), F32)],
        scratch_shapes=[pltpu.VMEM((tm, D), F32)],
        compiler_params=_cp(2, 56))(up3, up3, up3, up3, cw, cw, cb, cb, w, x2, final_g, target)


def _ffn_bwd_fused_call(up3, d_f, cw, cb, wg, resid, xin, g, name, deps=()):
    _, S, Fd = up3.shape
    nb, D, bn = wg.shape
    tm, tc = _tile(S, 256, 16), _ffn_tiles(S, Fd)[1]
    nk = Fd // tc
    ni = S // tm
    hb = tm // 16
    nsubw = bn // tc
    half = nb // 2
    sc = _tile(tc, SUB_LANES, LANES)

    def body(g_ref, v_ref, gp_ref, vp_ref, gn_ref, vn_ref, df_ref, dfn_ref, cwg_ref, cwv_ref, cbg_ref, cbv_ref,
             wgate_ref, wval_ref, r_ref, x_ref, ng_ref,
             dup_ref, dx_ref, dxb_ref, dg_ref, dcwg_ref, dcwv_ref, dcbg_ref, dcbv_ref, acc):
        i, k = pl.program_id(0), pl.program_id(1)

        @pl.when(jnp.logical_and(i == 0, k == 0))
        def _():
            for r in (dg_ref, dcwg_ref, dcwv_ref, dcbg_ref, dcbv_ref):
                r[...] = jnp.zeros_like(r)

        @pl.when(k == 0)
        def _():
            acc[...] = jnp.zeros_like(acc)

        keep_prev = jnp.where(i == 0, 0.0, 1.0)
        keep_next = jnp.where(i == ni - 1, 0.0, 1.0)
        zeros8 = jnp.zeros((8, sc), F32)
        part = None
        for s in range(tc // sc):
            ls = slice(s * sc, (s + 1) * sc)
            off = pl.multiple_of(k * tc + s * sc, LANES)
            df = jnp.concatenate([df_ref[:, ls].astype(F32), dfn_ref[:, ls].astype(F32)[:8] * keep_next], axis=0)

            def half_fwd(x_ref_, xp_ref, xn_ref, cw_ref, cb_ref):
                taps = _lane_taps(cw_ref, ls)
                prev8 = xp_ref[:, ls].astype(F32)[8:] * keep_prev
                ext = jnp.concatenate([x_ref_[:, ls].astype(F32), xn_ref[:, ls].astype(F32)[:8]], axis=0)
                conv, s1, s2 = _conv3(prev8, ext, taps)
                return taps, conv + cb_ref[:, ls], (ext, s1, s2)

            tg, cg, gsh = half_fwd(g_ref, gp_ref, gn_ref, cwg_ref, cbg_ref)
            tv, cv, vsh = half_fwd(v_ref, vp_ref, vn_ref, cwv_ref, cbv_ref)
            gel, dgel = _gelu_parts(cg)
            d_gate = df * cv * dgel
            d_val = df * gel

            def half_bwd(d, taps, shifts, dcw_ref, dcb_ref, slab, w_ref):
                dup = (taps[2] * d + taps[1] * _shift_up(d, zeros8, 1) + taps[0] * _shift_up(d, zeros8, 2))[:tm]
                dupb = dup.astype(BF16)
                dup_ref[slab, :, ls] = dupb
                dm = d[:tm]
                ext, s1, s2 = shifts
                dcw_ref[2:3, pl.ds(off, sc)] += _col_sum(ext[:tm] * dm)
                dcw_ref[1:2, pl.ds(off, sc)] += _col_sum(s1[:tm] * dm)
                dcw_ref[0:1, pl.ds(off, sc)] += _col_sum(s2[:tm] * dm)
                dcb_ref[:, pl.ds(off, sc)] += _col_sum(dm)
                return lax.dot_general(dupb, w_ref[:, ls], _DN_NT, preferred_element_type=F32)

            d = (half_bwd(d_gate, tg, gsh, dcwg_ref, dcbg_ref, 0, wgate_ref)
                 + half_bwd(d_val, tv, vsh, dcwv_ref, dcbv_ref, 1, wval_ref))
            part = d if part is None else part + d
        acc[...] += part

        @pl.when(k == nk - 1)
        def _():
            dh = acc[...]
            xv = x_ref[...]
            r = lax.rsqrt(_rows_mean(xv * xv) + RMS_EPS)
            xn = xv * r
            dg_ref[...] += _col_sum(dh * xn)
            dyg = dh * ng_ref[...]
            dx = r_ref[...] + r * (dyg - xn * _rows_mean(dyg * xn))
            dx_ref[...] = dx
            dxb_ref[...] = dx.astype(BF16)

    row = lambda i, k: (i, 0)
    prev = lambda i, k: (jnp.maximum(i * hb - 1, 0), k)
    nxt = lambda i, k: (jnp.minimum((i + 1) * hb, S // 16 - 1), k)
    main = lambda s: pl.BlockSpec((None, tm, tc), lambda i, k: (s, i, k))
    halo = lambda s, f: pl.BlockSpec((None, 16, tc), lambda i, k: (s,) + f(i, k))
    full3 = pl.BlockSpec((3, Fd), lambda i, k: (0, 0))
    full1 = pl.BlockSpec((1, Fd), lambda i, k: (0, 0))
    once = pl.Buffered(1)
    return _pallas(
        body, 17, deps, name=name, grid=(ni, nk),
        in_specs=[main(0), main(1), halo(0, prev), halo(1, prev), halo(0, nxt), halo(1, nxt),
                  pl.BlockSpec((tm, tc), lambda i, k: (i, k)), pl.BlockSpec((16, tc), nxt),
                  pl.BlockSpec((3, tc), lambda i, k: (0, k)), pl.BlockSpec((3, tc), lambda i, k: (0, k + nk)),
                  pl.BlockSpec((1, tc), lambda i, k: (0, k)), pl.BlockSpec((1, tc), lambda i, k: (0, k + nk)),
                  pl.BlockSpec((None, D, tc), lambda i, k: (k // nsubw, 0, k % nsubw)),
                  pl.BlockSpec((None, D, tc), lambda i, k: (half + k // nsubw, 0, k % nsubw)),
                  pl.BlockSpec((tm, D), row, pipeline_mode=once), pl.BlockSpec((tm, D), row, pipeline_mode=once),
                  pl.BlockSpec((1, D), lambda i, k: (0, 0))],
        out_specs=[pl.BlockSpec((2, tm, tc), lambda i, k: (0, i, k)), pl.BlockSpec((tm, D), row),
                   pl.BlockSpec((tm, D), row), pl.BlockSpec((1, D), lambda i, k: (0, 0)), full3, full3, full1, full1],
        out_shape=[jax.ShapeDtypeStruct((2, S, Fd), BF16), jax.ShapeDtypeStruct((S, D), F32),
                   jax.ShapeDtypeStruct((S, D), BF16), jax.ShapeDtypeStruct((1, D), F32),
                   jax.ShapeDtypeStruct((3, Fd), F32), jax.ShapeDtypeStruct((3, Fd), F32),
                   jax.ShapeDtypeStruct((1, Fd), F32), jax.ShapeDtypeStruct((1, Fd), F32)],
        scratch_shapes=[pltpu.VMEM((tm, D), F32)],
        compiler_params=_cp(2, 60))(up3, up3, up3, up3, up3, up3, d_f, d_f, cw, cw, cb, cb, wg, wg, resid, xin, g, *deps)


def _gm_forward_tile(pv, vg, vb, ws_ref, bsb_ref, mbuf, H, nc):
    W = H * CHUNK
    z, dz = _gelu_parts(pv)
    u, v0 = z[:, :W], z[:, W:]
    xc = v0 - _rows_mean(v0)
    rs = lax.rsqrt(_rows_mean(xc * xc) + LN_EPS)
    vh = xc * rs
    vnb = (vh * vg + vb).astype(BF16)
    mask = lax.broadcasted_iota(jnp.int32, (CHUNK, CHUNK), 0) >= lax.broadcasted_iota(jnp.int32, (CHUNK, CHUNK), 1)
    for h in range(H):
        cs = slice(h * CHUNK, (h + 1) * CHUNK)
        wm = jnp.where(mask, ws_ref[h], 0.0).astype(BF16)
        vcat = jnp.concatenate([vnb[c * CHUNK:(c + 1) * CHUNK, cs] for c in range(nc)], axis=1)
        mix = jnp.dot(wm, vcat, preferred_element_type=F32)
        for c in range(nc):
            mbuf[c * CHUNK:(c + 1) * CHUNK, cs] = mix[:, c * CHUNK:(c + 1) * CHUNK] + bsb_ref[h]
    return dz, u, vh, rs, vnb, mask


def _gm_fwd_call(p, v_g, v_b, ws, bsb, out_g, name, deps=()):
    S = p.shape[0]
    H = ws.shape[0]
    W = H * CHUNK
    tm = _tile(S, 256, CHUNK)
    nc = tm // CHUNK

    def body(p_ref, vg_ref, vb_ref, ws_ref, bsb_ref, og_ref, y_ref, mbuf):
        _, u, _, _, _, _ = _gm_forward_tile(p_ref[...], vg_ref[...], vb_ref[...], ws_ref, bsb_ref, mbuf, H, nc)
        yg = u * mbuf[...]
        r = lax.rsqrt(_rows_mean(yg * yg) + RMS_EPS)
        y_ref[...] = (yg * r * og_ref[...]).astype(BF16)

    vec = pl.BlockSpec((1, W), lambda i: (0, 0))
    mat = pl.BlockSpec((H, CHUNK, CHUNK), lambda i: (0, 0, 0))
    return _pallas(
        body, 6, deps, name=name, grid=(S // tm,),
        in_specs=[pl.BlockSpec((tm, 2 * W), lambda i: (i, 0)), vec, vec, mat, mat, vec],
        out_specs=pl.BlockSpec((tm, W), lambda i: (i, 0)),
        out_shape=jax.ShapeDtypeStruct((S, W), BF16),
        scratch_shapes=[pltpu.VMEM((tm, W), F32)], compiler_params=_cp(1))(p, v_g, v_b, ws, bsb, out_g, *deps)


def _gm_bwd_call(p, d_y, v_g, v_b, ws, bsb, out_g, name, deps=()):
    S = p.shape[0]
    H = ws.shape[0]
    W = H * CHUNK
    tm = _tile(S, 256, CHUNK)
    nc = tm // CHUNK
    ni = S // tm

    def body(p_ref, dy_ref, vg_ref, vb_ref, ws_ref, bsb_ref, og_ref,
             dp_ref, dvg_ref, dvb_ref, dws_ref, dbs_ref, dog_ref, mbuf, dvbuf):
        i = pl.program_id(0)

        @pl.when(i == 0)
        def _():
            for r in (dvg_ref, dvb_ref, dws_ref, dbs_ref, dog_ref):
                r[...] = jnp.zeros_like(r)

        vg = vg_ref[...]
        dz, u, vh, rs, vnb, mask = _gm_forward_tile(p_ref[...], vg, vb_ref[...], ws_ref, bsb_ref, mbuf, H, nc)
        mixed = mbuf[...]
        yg = u * mixed
        r = lax.rsqrt(_rows_mean(yg * yg) + RMS_EPS)
        yn = yg * r
        dya = dy_ref[...]
        dog_ref[...] += _col_sum(dya * yn)
        dyg = dya * og_ref[...]
        dygm = r * (dyg - yn * _rows_mean(dyg * yn))
        du = dygm * mixed
        dmix = dygm * u
        dmb = dmix.astype(BF16)
        for h in range(H):
            cs = slice(h * CHUNK, (h + 1) * CHUNK)
            wm = jnp.where(mask, ws_ref[h], 0.0).astype(BF16)
            dcat = jnp.concatenate([dmb[c * CHUNK:(c + 1) * CHUNK, cs] for c in range(nc)], axis=1)
            vcat = jnp.concatenate([vnb[c * CHUNK:(c + 1) * CHUNK, cs] for c in range(nc)], axis=1)
            dvn = lax.dot_general(wm, dcat, _DN_TN, preferred_element_type=F32)
            dws_ref[h] += jnp.where(mask, lax.dot_general(dcat, vcat, _DN_NT, preferred_element_type=F32), 0.0)
            dbs = dmix[0:CHUNK, cs]
            for c in range(1, nc):
                dbs = dbs + dmix[c * CHUNK:(c + 1) * CHUNK, cs]
            dbs_ref[h] += dbs
            for c in range(nc):
                dvbuf[c * CHUNK:(c + 1) * CHUNK, cs] = dvn[:, c * CHUNK:(c + 1) * CHUNK]
        dvn_all = dvbuf[...]
        dvg_ref[...] += _col_sum(dvn_all * vh)
        dvb_ref[...] += _col_sum(dvn_all)
        dvh = dvn_all * vg
        dv0 = rs * (dvh - _rows_mean(dvh) - vh * _rows_mean(dvh * vh))
        dp_ref[...] = (jnp.concatenate([du, dv0], axis=1) * dz).astype(BF16)

        @pl.when(i == ni - 1)
        def _():
            for h in range(H):
                dbs_ref[h] = jnp.broadcast_to(jnp.sum(dbs_ref[h], axis=1, keepdims=True), (CHUNK, CHUNK))

    vec = pl.BlockSpec((1, W), lambda i: (0, 0))
    mat = pl.BlockSpec((H, CHUNK, CHUNK), lambda i: (0, 0, 0))
    vshape = jax.ShapeDtypeStruct((1, W), F32)
    mshape = jax.ShapeDtypeStruct((H, CHUNK, CHUNK), F32)
    return _pallas(
        body, 7, deps, name=name, grid=(ni,),
        in_specs=[pl.BlockSpec((tm, 2 * W), lambda i: (i, 0)), pl.BlockSpec((tm, W), lambda i: (i, 0)),
                  vec, vec, mat, mat, vec],
        out_specs=[pl.BlockSpec((tm, 2 * W), lambda i: (i, 0)), vec, vec, mat, mat, vec],
        out_shape=[jax.ShapeDtypeStruct((S, 2 * W), BF16), vshape, vshape, mshape, mshape, vshape],
        scratch_shapes=[pltpu.VMEM((tm, W), F32), pltpu.VMEM((tm, W), F32)],
        compiler_params=_cp(1))(p, d_y, v_g, v_b, ws, bsb, out_g, *deps)


def _lru_gates(prev8, xl, cw, cb, wa_ref, ba, wx_ref, bx, lam, H):
    sh = [_shift_down(prev8, xl, k) for k in range(4)]
    xr = cw[3] * sh[0] + cw[2] * sh[1] + cw[1] * sh[2] + cw[0] * sh[3] + cb
    xrb = xr.astype(BF16)
    rp, ip = [], []
    for h in range(H):
        cs = slice(h * CHUNK, (h + 1) * CHUNK)
        rp.append(jnp.dot(xrb[:, cs], wa_ref[h].astype(BF16), preferred_element_type=F32))
        ip.append(jnp.dot(xrb[:, cs], wx_ref[h].astype(BF16), preferred_element_type=F32))
    r = _sigmoid(jnp.concatenate(rp, axis=1) + ba)
    ig = _sigmoid(jnp.concatenate(ip, axis=1) + bx)
    sp = _softplus(-lam)
    la = (-LRU_C) * r * sp
    a = jnp.exp(la)
    mult = jnp.sqrt(jnp.maximum(_neg_expm1(2.0 * la), 0.0))
    return xr, xrb, r, ig, sp, a, mult, sh


def _lru_fwd_call(p, cw, cb, wa, ba, wx, bx, lam, out_g, name):
    S = p.shape[0]
    H = wa.shape[0]
    W = H * CHUNK
    tm = _tile(S, 256, 16)
    ng = tm // 8

    def body(pg_ref, px_ref, cw_ref, cb_ref, wa_ref, ba_ref, wx_ref, bx_ref, lam_ref, og_ref,
             y_ref, h_ref, xprev, hcar, abuf, bbuf):
        @pl.when(pl.program_id(0) == 0)
        def _():
            xprev[...] = jnp.zeros_like(xprev)
            hcar[...] = jnp.zeros_like(hcar)

        xl = px_ref[...]
        xr, _, _, ig, _, a, mult, _ = _lru_gates(xprev[...], xl, _taps(cw_ref), cb_ref[...], wa_ref, ba_ref[...],
                                                 wx_ref, bx_ref[...], lam_ref[...], H)
        xprev[...] = xl[tm - 8:]
        b = mult * (ig * xr)
        sub = lax.broadcasted_iota(jnp.int32, (tm, W), 0) & 7
        for d in (1, 2, 4):
            m = sub >= d
            a_s = jnp.where(m, pltpu.roll(a, d, 0), 1.0)
            b_s = jnp.where(m, pltpu.roll(b, d, 0), 0.0)
            b = a * b_s + b
            a = a * a_s
        abuf[...] = a
        bbuf[...] = b

        def step(g, carry):
            r0 = pl.multiple_of(g * 8, 8)
            h_ref[pl.ds(r0, 8), :] = abuf[pl.ds(r0, 8), :] * carry + bbuf[pl.ds(r0, 8), :]
            return jnp.broadcast_to(h_ref[pl.ds(r0 + 7, 1), :], (8, W))

        hcar[...] = lax.fori_loop(0, ng, step, hcar[...])
        yl = h_ref[...] * _gelu(pg_ref[...])
        r = lax.rsqrt(_rows_mean(yl * yl) + RMS_EPS)
        y_ref[...] = (yl * r * og_ref[...]).astype(BF16)

    vec = pl.BlockSpec((1, W), lambda i: (0, 0))
    mat = pl.BlockSpec((H, CHUNK, CHUNK), lambda i: (0, 0, 0))
    return pl.pallas_call(
        body, name=name, grid=(S // tm,),
        in_specs=[pl.BlockSpec((tm, W), lambda i: (i, 2)), pl.BlockSpec((tm, W), lambda i: (i, 3)),
                  pl.BlockSpec((4, W), lambda i: (0, 0)), vec, mat, vec, mat, vec, vec, vec],
        out_specs=[pl.BlockSpec((tm, W), lambda i: (i, 0)), pl.BlockSpec((tm, W), lambda i: (i, 0))],
        out_shape=[jax.ShapeDtypeStruct((S, W), BF16), jax.ShapeDtypeStruct((S, W), F32)],
        scratch_shapes=[pltpu.VMEM((8, W), F32), pltpu.VMEM((8, W), F32), pltpu.VMEM((tm, W), F32),
                        pltpu.VMEM((tm, W), F32)],
        compiler_params=_cp(1))(p, p, cw, cb, wa, ba, wx, bx, lam, out_g)


def _lru_bwd_call(p, hs, d_y, cw, cb, wa, ba, wx, bx, lam, out_g, name):
    S = p.shape[0]
    H = wa.shape[0]
    W = H * CHUNK
    tm = _tile(S, 256, 16)
    ng = tm // 8
    ni = S // tm
    hb = tm // 8

    def body(pg_ref, px_ref, pxp_ref, h_ref, hp_ref, dy_ref, cw_ref, cb_ref, wa_ref, ba_ref, wx_ref, bx_ref,
             lam_ref, og_ref,
             dp_ref, dcw_ref, dcb_ref, dwa_ref, dba_ref, dwx_ref, dbx_ref, dlam_ref, dog_ref,
             a_next, e_next, dxr_next, abuf, bbuf, ebuf):
        i = pl.program_id(0)
        ri = ni - 1 - i

        @pl.when(i == 0)
        def _():
            for r in (dcw_ref, dcb_ref, dwa_ref, dba_ref, dwx_ref, dbx_ref, dlam_ref, dog_ref,
                      a_next, e_next, dxr_next):
                r[...] = jnp.zeros_like(r)

        keep_prev = jnp.where(ri == 0, 0.0, 1.0)
        cw_ = _taps(cw_ref)
        lam_ = lam_ref[...]
        xl = px_ref[...]
        xr, xrb, r, ig, sp, a, mult, sh = _lru_gates(pxp_ref[...] * keep_prev, xl, cw_, cb_ref[...], wa_ref,
                                                     ba_ref[...], wx_ref, bx_ref[...], lam_, H)
        gg, dgg = _gelu_parts(pg_ref[...])
        hv = h_ref[...]
        yl = hv * gg
        rr = lax.rsqrt(_rows_mean(yl * yl) + RMS_EPS)
        yn = yl * rr
        dyb = dy_ref[...]
        dog_ref[...] += _col_sum(dyb * yn)
        dyg = dyb * og_ref[...]
        dyl = rr * (dyg - yn * _rows_mean(dyg * yn))
        dh = dyl * gg
        dgl = dyl * hv * dgg

        an = _shift_up(a, a_next[...], 1)
        eb = dh
        sub = lax.broadcasted_iota(jnp.int32, (tm, W), 0) & 7
        for d in (1, 2, 4):
            m = sub < 8 - d
            a_s = jnp.where(m, pltpu.roll(an, tm - d, 0), 1.0)
            e_s = jnp.where(m, pltpu.roll(eb, tm - d, 0), 0.0)
            eb = an * e_s + eb
            an = an * a_s
        abuf[...] = an
        bbuf[...] = eb

        def step(g, carry):
            r0 = pl.multiple_of((ng - 1 - g) * 8, 8)
            ebuf[pl.ds(r0, 8), :] = abuf[pl.ds(r0, 8), :] * carry + bbuf[pl.ds(r0, 8), :]
            return jnp.broadcast_to(ebuf[pl.ds(r0, 1), :], (8, W))

        lax.fori_loop(0, ng, step, jnp.broadcast_to(e_next[0:1, :], (8, W)))
        e = ebuf[...]
        a_next[...] = a[0:8]
        e_next[...] = e[0:8]

        hm1 = _shift_down(hp_ref[...] * keep_prev, hv, 1)
        da = e * hm1
        dmult = e * ig * xr
        di = e * mult * xr
        dxr = e * mult * ig
        dla = da * a - dmult * (a * a) / mult
        dr = dla * ((-LRU_C) * sp)
        dlam_ref[...] += _col_sum(dla * ((-LRU_C) * r))
        dpr = dr * r * (1.0 - r)
        dpi = di * ig * (1.0 - ig)
        dba_ref[...] += _col_sum(dpr)
        dbx_ref[...] += _col_sum(dpi)
        dprb = dpr.astype(BF16)
        dpib = dpi.astype(BF16)
        back = []
        for h in range(H):
            cs = slice(h * CHUNK, (h + 1) * CHUNK)
            wab = wa_ref[h].astype(BF16)
            wxb = wx_ref[h].astype(BF16)
            back.append(lax.dot_general(dprb[:, cs], wab, _DN_NT, preferred_element_type=F32)
                        + lax.dot_general(dpib[:, cs], wxb, _DN_NT, preferred_element_type=F32))
            dwa_ref[h] += lax.dot_general(xrb[:, cs], dprb[:, cs], _DN_TN, preferred_element_type=F32)
            dwx_ref[h] += lax.dot_general(xrb[:, cs], dpib[:, cs], _DN_TN, preferred_element_type=F32)
        dxr = dxr + jnp.concatenate(back, axis=1)

        nxt = dxr_next[...]
        dxl = (cw_[3] * dxr + cw_[2] * _shift_up(dxr, nxt, 1) + cw_[1] * _shift_up(dxr, nxt, 2)
               + cw_[0] * _shift_up(dxr, nxt, 3))
        dxr_next[...] = dxr[0:8]
        for k in range(4):
            dcw_ref[k:k + 1, :] += _col_sum(sh[3 - k] * dxr)
        dcb_ref[...] += _col_sum(dxr)
        dp_ref[...] = jnp.concatenate([dgl, dxl], axis=1).astype(BF16)

        @pl.when(i == ni - 1)
        def _():
            dlam_ref[...] = -dlam_ref[...] * _sigmoid(-lam_)

    vec = pl.BlockSpec((1, W), lambda i: (0, 0))
    mat = pl.BlockSpec((H, CHUNK, CHUNK), lambda i: (0, 0, 0))
    rev = lambda i: ni - 1 - i
    prev = lambda i: jnp.maximum(rev(i) * hb - 1, 0)
    vshape = jax.ShapeDtypeStruct((1, W), F32)
    mshape = jax.ShapeDtypeStruct((H, CHUNK, CHUNK), F32)
    tile = lambda: pltpu.VMEM((tm, W), F32)
    car = lambda: pltpu.VMEM((8, W), F32)
    return pl.pallas_call(
        body, name=name, grid=(ni,),
        in_specs=[pl.BlockSpec((tm, W), lambda i: (rev(i), 2)), pl.BlockSpec((tm, W), lambda i: (rev(i), 3)),
                  pl.BlockSpec((8, W), lambda i: (prev(i), 3)),
                  pl.BlockSpec((tm, W), lambda i: (rev(i), 0)), pl.BlockSpec((8, W), lambda i: (prev(i), 0)),
                  pl.BlockSpec((tm, W), lambda i: (rev(i), 1)),
                  pl.BlockSpec((4, W), lambda i: (0, 0)), vec, mat, vec, mat, vec, vec, vec],
        out_specs=[pl.BlockSpec((tm, 2 * W), lambda i: (rev(i), 0)), pl.BlockSpec((4, W), lambda i: (0, 0)), vec,
                   mat, vec, mat, vec, vec, vec],
        out_shape=[jax.ShapeDtypeStruct((S, 2 * W), BF16), jax.ShapeDtypeStruct((4, W), F32), vshape,
                   mshape, vshape, mshape, vshape, vshape, vshape],
        scratch_shapes=[car(), car(), car(), tile(), tile(), tile()],
        compiler_params=_cp(1, 56))(p, p, p, hs, hs, d_y, cw, cb, wa, ba, wx, bx, lam, out_g)


def _rows128(a):
    return a.reshape(-1, LANES).astype(F32)


def _pack(arrays, pad_to=256):
    flat = jnp.concatenate([_rows128(a) for a in arrays], axis=0)
    pad = (-flat.shape[0]) % pad_to
    if pad:
        flat = jnp.concatenate([flat, jnp.zeros((pad, LANES), F32)], axis=0)
    return flat


def _unpack(flat, shapes):
    out, r = [], 0
    for s in shapes:
        n = 1
        for d in s:
            n *= d
        out.append(flat[r:r + n // LANES].reshape(s))
        r += n // LANES
    return out


def kernel(x, norm1_g, w_in, gm_v_g, gm_v_b, gm_ws, gm_bs, lru_conv_w, lru_conv_b, lru_wa, lru_ba, lru_wx, lru_bx, lru_lambda, gm_out_g, lru_out_g, w_out, norm2_g, ffn_w_up, ffn_conv_w, ffn_conv_b, ffn_w_down, final_g, loss_target, m_norm1_g, m_w_in, m_gm_v_g, m_gm_v_b, m_gm_ws, m_gm_bs, m_lru_conv_w, m_lru_conv_b, m_lru_wa, m_lru_ba, m_lru_wx, m_lru_bx, m_lru_lambda, m_gm_out_g, m_lru_out_g, m_w_out, m_norm2_g, m_ffn_w_up, m_ffn_conv_w, m_ffn_conv_b, m_ffn_w_down, m_final_g, v_norm1_g, v_w_in, v_gm_v_g, v_gm_v_b, v_gm_ws, v_gm_bs, v_lru_conv_w, v_lru_conv_b, v_lru_wa, v_lru_ba, v_lru_wx, v_lru_bx, v_lru_lambda, v_gm_out_g, v_lru_out_g, v_w_out, v_norm2_g, v_ffn_w_up, v_ffn_conv_w, v_ffn_conv_b, v_ffn_w_down, v_final_g):
    wts = dict(norm1_g=norm1_g, w_in=w_in, gm_v_g=gm_v_g, gm_v_b=gm_v_b, gm_ws=gm_ws, gm_bs=gm_bs,
               lru_conv_w=lru_conv_w, lru_conv_b=lru_conv_b, lru_wa=lru_wa, lru_ba=lru_ba, lru_wx=lru_wx,
               lru_bx=lru_bx, lru_lambda=lru_lambda, gm_out_g=gm_out_g, lru_out_g=lru_out_g, w_out=w_out,
               norm2_g=norm2_g, ffn_w_up=ffn_w_up, ffn_conv_w=ffn_conv_w, ffn_conv_b=ffn_conv_b,
               ffn_w_down=ffn_w_down, final_g=final_g)
    mom = dict(norm1_g=m_norm1_g, w_in=m_w_in, gm_v_g=m_gm_v_g, gm_v_b=m_gm_v_b, gm_ws=m_gm_ws, gm_bs=m_gm_bs,
               lru_conv_w=m_lru_conv_w, lru_conv_b=m_lru_conv_b, lru_wa=m_lru_wa, lru_ba=m_lru_ba, lru_wx=m_lru_wx,
               lru_bx=m_lru_bx, lru_lambda=m_lru_lambda, gm_out_g=m_gm_out_g, lru_out_g=m_lru_out_g, w_out=m_w_out,
               norm2_g=m_norm2_g, ffn_w_up=m_ffn_w_up, ffn_conv_w=m_ffn_conv_w, ffn_conv_b=m_ffn_conv_b,
               ffn_w_down=m_ffn_w_down, final_g=m_final_g)
    var = dict(norm1_g=v_norm1_g, w_in=v_w_in, gm_v_g=v_gm_v_g, gm_v_b=v_gm_v_b, gm_ws=v_gm_ws, gm_bs=v_gm_bs,
               lru_conv_w=v_lru_conv_w, lru_conv_b=v_lru_conv_b, lru_wa=v_lru_wa, lru_ba=v_lru_ba, lru_wx=v_lru_wx,
               lru_bx=v_lru_bx, lru_lambda=v_lru_lambda, gm_out_g=v_gm_out_g, lru_out_g=v_lru_out_g, w_out=v_w_out,
               norm2_g=v_norm2_g, ffn_w_up=v_ffn_w_up, ffn_conv_w=v_ffn_conv_w, ffn_conv_b=v_ffn_conv_b,
               ffn_w_down=v_ffn_w_down, final_g=v_final_g)

    xi, yi, ci = lax.axis_index("x"), lax.axis_index("y"), lax.axis_index("c")
    chip = 2 * xi + yi
    dev = 2 * chip + ci
    core_chip = jnp.stack([ci, chip]).astype(jnp.int32)

    xs = x[0]
    tgt = loss_target[0]
    S, D = xs.shape
    H = gm_ws.shape[1]
    W = H * CHUNK
    Fd = ffn_w_down.shape[1] * N_DEV
    lcw_cols = lru_conv_w.shape[2]
    fcw_cols = ffn_conv_w.shape[2]

    dev1 = jnp.reshape(dev, (1,)).astype(jnp.int32)

    def gather_start(shards, name, after=()):
        lands = [_place_own_call(s, dev1, "%s_own%d" % (name, k)) for k, s in enumerate(shards)]
        return _exchange_start(shards, lands, 4 * len(shards), _gather_stage1_copies(len(shards)), name + "_ici", after)

    def gather_forward(lands, name, after=()):
        return _exchange_start([], lands, 3 * len(lands), _gather_stage2_copies(len(lands)), name + "_d2d", after)

    def pair_start(g, name, after=()):
        return _exchange_start([g], [lax.empty((4,) + g.shape[1:], F32)], 4, _pair_copies(1), name, after)

    def chip_start(p16, name, after=()):
        return _exchange_start([p16], [lax.empty((3,) + p16.shape[1:], BF16)], 3, _chip_copies(1), name, after)

    vgm_g, vgm_b = gm_v_g, gm_v_b
    ws, wa, wx = gm_ws[0], lru_wa[0], lru_wx[0]
    bsb = jnp.broadcast_to(gm_bs[0][:, :, None], (H, CHUNK, CHUNK))
    ba, bx = lru_ba.reshape(1, W), lru_bx.reshape(1, W)
    fcb = ffn_conv_b
    fing = final_g.reshape(1, D)

    conv_pack = _pack([lru_conv_w[0], ffn_conv_w[0]], pad_to=8)
    ga1 = gather_start([w_in[0].astype(BF16), conv_pack], "gather_in")
    h1 = _rmsnorm_call(xs, norm1_g, "norm1", deps=(ga1.token,))
    _, la = _exchange_wait(ga1, after=(h1,))
    ga2 = gather_forward(la, "gather_in")
    gb1 = gather_start([w_out[0].astype(BF16)], "gather_out", after=(ga2.token,))
    _, (win_g, conv_g) = _exchange_wait(ga2, after=(gb1.token,))
    n_l = 4 * lcw_cols // LANES
    n_f = 3 * fcw_cols // LANES
    lcw = conv_g[:, :n_l].reshape(N_DEV, 4, lcw_cols).transpose(1, 0, 2).reshape(4, N_DEV * lcw_cols)
    fcw = conv_g[:, n_l:n_l + n_f].reshape(N_DEV, 3, fcw_cols).transpose(1, 0, 2).reshape(3, N_DEV * fcw_cols)

    p = _mm_blocked_call(h1, win_g, F32, False, "in_proj")
    _, lb = _exchange_wait(gb1, after=(p,))
    gb2 = gather_forward(lb, "gather_out")
    gc1 = gather_start([ffn_w_up[0].astype(BF16)], "gather_up", after=(gb2.token,))
    ya = _gm_fwd_call(p, vgm_g, vgm_b, ws, bsb, gm_out_g, "gmlp_fwd", deps=(gc1.token,))
    yb, hs = _lru_fwd_call(p, lcw, lru_conv_b, wa, ba, wx, bx, lru_lambda, lru_out_g, "lru_fwd")
    y = jnp.concatenate([ya, yb], axis=1)
    _, (wout_g,) = _exchange_wait(gb2, after=(y,))
    wout_full = wout_g.reshape(D, D)
    x2 = _mm_out_call(xs, y, wout_full, "out_proj")
    h2 = _rmsnorm_call(x2, norm2_g, "norm2")
    _, lc = _exchange_wait(gc1, after=(h2,))
    gc2 = gather_forward(lc, "gather_up")
    gd1 = gather_start([ffn_w_down[0].astype(BF16)], "gather_down", after=(gc2.token,))
    _, (wup_g,) = _exchange_wait(gc2, after=(gd1.token,))
    up3 = _mm_blocked_call(h2, wup_g, BF16, True, "ffn_up")
    _, ld = _exchange_wait(gd1, after=(up3,))
    gd2 = gather_forward(ld, "gather_down")
    _, (wdown_g,) = _exchange_wait(gd2)
    wdown_full = wdown_g.reshape(Fd, D)
    f, dx3, dx3b, loss_acc, d_final = _ffn_down_fused_call(up3, fcw, fcb, wdown_full, x2, fing, tgt, "ffn_down_loss")

    g_wdown = _mm_tn_rows_call(f, dx3b, "ffn_down_dw").reshape((N_DEV,) + ffn_w_down.shape[1:])
    pd = pair_start(g_wdown, "pair_down")
    d_f = _mm_nt_call(dx3b, wdown_full, BF16, "ffn_down_dx", deps=(pd.token,))
    d_up3, dx2, dx2b, d_norm2, dfcw_g, dfcw_v, dfcb_g, dfcb_v = _ffn_bwd_fused_call(
        up3, d_f, fcw, fcb, wup_g, dx3, x2, norm2_g, "ffn_up_dx")
    (g_wdown,), (r1,) = _exchange_wait(pd, after=(d_up3,))
    own_down, p16 = _pair_add_call(g_wdown, r1, core_chip, "pair_add_down")
    cd = chip_start(p16, "chip_down")
    g_wup = _mm_tn_cols_call(h2, d_up3, N_DEV, ffn_w_up.shape[2], "ffn_up_dw", deps=(cd.token,))
    pu = pair_start(g_wup, "pair_up")
    g_wout = _mm_tn_rows_call(y, dx2b, "out_proj_dw", deps=(pu.token,)).reshape((N_DEV,) + w_out.shape[1:])
    po = pair_start(g_wout, "pair_out")
    d_y = _mm_nt_call(dx2b, wout_full, F32, "out_proj_dx", deps=(po.token,))
    (g_wup,), (r1,) = _exchange_wait(pu, after=(d_y,))
    own_up, p16 = _pair_add_call(g_wup, r1, core_chip, "pair_add_up")
    _, (r2_down,) = _exchange_wait(cd, after=(p16,))
    cu = chip_start(p16, "chip_up", after=(r2_down,))
    dp_gm, d_vg, d_vb, d_ws, d_bs, d_gog = _gm_bwd_call(p, d_y, vgm_g, vgm_b, ws, bsb, gm_out_g, "gmlp_bwd",
                                                        deps=(cu.token,))
    dp_lru, d_lcw, d_lcb, d_wa, d_ba, d_wx, d_bx, d_lam, d_log = _lru_bwd_call(
        p, hs, d_y, lcw, lru_conv_b, wa, ba, wx, bx, lru_lambda, lru_out_g, "lru_bwd")
    d_p = jnp.concatenate([dp_gm, dp_lru], axis=1)[None]
    (g_wout,), (r1,) = _exchange_wait(po, after=(d_p,))
    own_out, p16_out = _pair_add_call(g_wout, r1, core_chip, "pair_add_out")
    g_win = _mm_tn_cols_call(h1, d_p, N_DEV, w_in.shape[2], "in_proj_dw")
    pi = pair_start(g_win, "pair_in")
    _, (r2_up,) = _exchange_wait(cu, after=(g_win,))
    co = chip_start(p16_out, "chip_out", after=(r2_up,))
    grad_x, _, d_norm1 = _mm_dx_norm_call(d_p, win_g, dx2, xs, norm1_g, "in_proj_dx", deps=(co.token, pi.token))
    (g_win,), (r1,) = _exchange_wait(pi, after=(grad_x,))
    own_in, p16 = _pair_add_call(g_win, r1, core_chip, "pair_add_in")
    _, (r2_out,) = _exchange_wait(co, after=(p16,))
    ci_ = chip_start(p16, "chip_in", after=(r2_out,))

    def adamw_big(n, own, r2, deps=()):
        return _adamw_call(wts[n][0], mom[n][0], var[n][0], [(own, None), (r2, 0), (r2, 1), (r2, 2)], "adamw_" + n, deps)

    res = {}
    res["ffn_w_down"] = adamw_big("ffn_w_down", own_down, r2_down, (ci_.token,))
    res["ffn_w_up"] = adamw_big("ffn_w_up", own_up, r2_up, (ci_.token,))
    res["w_out"] = adamw_big("w_out", own_out, r2_out, (ci_.token,))
    _, (r2_in,) = _exchange_wait(ci_, after=(res["w_out"][0], res["ffn_w_up"][0], res["ffn_w_down"][0]))
    res["w_in"] = adamw_big("w_in", own_in, r2_in)

    small_g = dict(norm1_g=d_norm1, gm_v_g=d_vg, gm_v_b=d_vb, gm_ws=d_ws, gm_bs=d_bs[:, :, 0], lru_conv_b=d_lcb,
                   lru_wa=d_wa, lru_ba=d_ba, lru_wx=d_wx, lru_bx=d_bx, lru_lambda=d_lam, gm_out_g=d_gog,
                   lru_out_g=d_log, norm2_g=d_norm2,
                   ffn_conv_b=jnp.concatenate([dfcb_g, dfcb_v], axis=1), final_g=d_final)
    rep = _pack([small_g[n] for n in SMALL])
    conv_part = _pack([d_lcw, jnp.concatenate([dfcw_g, dfcw_v], axis=1)], pad_to=8)
    n_rep, n_conv = rep.shape[0], conv_part.shape[0]
    (parts,) = _all_gather_call([jnp.concatenate([rep, conv_part], axis=0)], "gather_small_grads")
    g_rep, d_rep, m_rep, v_rep = _adamw_call(
        _pack([wts[n] for n in SMALL]), _pack([mom[n] for n in SMALL]), _pack([var[n] for n in SMALL]),
        [(parts, k) for k in range(N_DEV)], "adamw_small")
    shapes = [wts[n].shape for n in SMALL]
    for n, g_, d_, m_, v_ in zip(SMALL, _unpack(g_rep, shapes), _unpack(d_rep, shapes), _unpack(m_rep, shapes),
                                 _unpack(v_rep, shapes)):
        res[n] = (g_, d_, m_, v_)
    conv_sum = _sum_call(parts, n_rep, n_conv, "sum_conv_grads")
    g_lcw = conv_sum[:4 * W // LANES].reshape(4, W)
    g_fcw = conv_sum[4 * W // LANES:4 * W // LANES + 6 * Fd // LANES].reshape(3, 2 * Fd)
    for n, full in (("lru_conv_w", g_lcw), ("ffn_conv_w", g_fcw)):
        cols = wts[n].shape[2]
        mine = lax.dynamic_slice_in_dim(full, dev * cols, cols, axis=1)
        res[n] = _adamw_call(wts[n][0], mom[n][0], var[n][0], [(mine, None)], "adamw_" + n)

    loss = lax.psum(loss_acc[0, 0], ("x", "y", "c"))
    outs = [[], [], [], []]
    for n in WEIGHTS:
        for k in range(4):
            outs[k].append(res[n][k].reshape(wts[n].shape))
    return (loss, grad_x[None], *outs[0], *outs[1], *outs[2], *outs[3])
```

```python
import functools
import math

import jax
import jax.numpy as jnp
from jax import lax
from jax.experimental import pallas as pl
from jax.experimental.pallas import tpu as pltpu

F32 = jnp.float32
BF16 = jnp.bfloat16

RMS_EPS = 1e-6
LN_EPS = 1e-5
LRU_C = 8.0
CHUNK = 128
ADAM_LR = 0.001
ADAM_B1 = 0.9
ADAM_B2 = 0.999
ADAM_EPS = 1e-08
ADAM_WD = 0.01
ADAM_STEP = 10
N_DEV = 8
LANES = 128
MIB = 1024 * 1024

WEIGHTS = ['norm1_g', 'w_in', 'gm_v_g', 'gm_v_b', 'gm_ws', 'gm_bs', 'lru_conv_w', 'lru_conv_b', 'lru_wa', 'lru_ba',
           'lru_wx', 'lru_bx', 'lru_lambda', 'gm_out_g', 'lru_out_g', 'w_out', 'norm2_g', 'ffn_w_up', 'ffn_conv_w',
           'ffn_conv_b', 'ffn_w_down', 'final_g']
BIG = ['w_in', 'w_out', 'ffn_w_up', 'ffn_w_down']
CONV = ['lru_conv_w', 'ffn_conv_w']
SMALL = [n for n in WEIGHTS if n not in BIG and n not in CONV]

_DN_NT = (((1,), (1,)), ((), ()))
_DN_TN = (((0,), (0,)), ((), ()))
_GELU_C = 0.7978845608028654


def _cp(n_axes, vmem_mib=48):
    return pltpu.CompilerParams(dimension_semantics=("arbitrary",) * n_axes, vmem_limit_bytes=vmem_mib * MIB)


def _tile(n, pref, mult=8):
    t = min(pref, n)
    t -= t % mult
    while t >= mult:
        if n % t == 0:
            return t
        t -= mult
    return n


def _gelu(z):
    return 0.5 * z * (1.0 + jnp.tanh(_GELU_C * z * (1.0 + 0.044715 * z * z)))


def _gelu_parts(z):
    z2 = z * z
    t = jnp.tanh(_GELU_C * z * (1.0 + 0.044715 * z2))
    g = 0.5 * z * (1.0 + t)
    dg = 0.5 * (1.0 + t) + 0.5 * z * (1.0 - t * t) * (_GELU_C * (1.0 + 0.134145 * z2))
    return g, dg


def _sigmoid(z):
    return 1.0 / (1.0 + jnp.exp(-z))


def _softplus(z):
    t = jnp.exp(-jnp.abs(z))
    u = 1.0 + t
    log1p = jnp.where(u == 1.0, t, jnp.log(u) * t / (u - 1.0))
    return jnp.maximum(z, 0.0) + log1p


def _neg_expm1(y):
    e = jnp.exp(y)
    small = jnp.where(e == 1.0, y, (e - 1.0) * y / jnp.log(e))
    return -jnp.where(y < -0.5, e - 1.0, small)


def _rows_mean(v):
    return jnp.mean(v, axis=-1, keepdims=True)


def _col_sum(v):
    return jnp.sum(v, axis=0, keepdims=True)


def _shift_down(prev8, cur, k):
    if k == 0:
        return cur
    z = jnp.concatenate([prev8, cur], axis=0)
    return pltpu.roll(z, k, 0)[8:]


def _shift_up(cur, next8, k):
    if k == 0:
        return cur
    n = cur.shape[0]
    z = jnp.concatenate([cur, next8], axis=0)
    return pltpu.roll(z, n + 8 - k, 0)[:n]


def _mesh_pos():
    return lax.axis_index("x"), lax.axis_index("y"), lax.axis_index("c")


def _any_specs(n):
    return [pl.BlockSpec(memory_space=pl.ANY)] * n


def _pallas(body, n_in, deps, **kw):
    nd = len(deps)
    if not nd:
        return pl.pallas_call(body, **kw)

    def ordered(*refs):
        body(*refs[:n_in], *refs[n_in + nd:])

    kw["in_specs"] = list(kw["in_specs"]) + _any_specs(nd)
    return pl.pallas_call(ordered, **kw)


_HBM = pl.BlockSpec(memory_space=pltpu.HBM)
_SEM = pl.BlockSpec(memory_space=pltpu.SEMAPHORE)
_EFFECT = pltpu.SideEffectType.DATAFLOW_SIDE_EFFECTING


class _InFlight:
    def __init__(self, sems, bufs, token, n_src, n_copies, make_copies, name):
        self.sems, self.bufs, self.token = sems, bufs, token
        self.n_src, self.n_copies, self.make_copies, self.name = n_src, n_copies, make_copies, name


def _exchange_start(srcs, lands, n_copies, make_copies, name, after=()):
    bufs = list(srcs) + list(lands)
    nb, na = len(bufs), len(after)
    ns = len(srcs)

    def body(*refs):
        b_refs = refs[:nb]
        outs = refs[nb + na:]
        send, recv = outs[:n_copies], outs[n_copies:2 * n_copies]
        token = outs[-1]
        for cp in make_copies(b_refs[:ns], b_refs[ns:], send, recv):
            cp.start()
        token[...] = jnp.zeros_like(token)

    out = pl.pallas_call(
        body, name=name,
        out_shape=[pltpu.SemaphoreType.DMA(())] * (2 * n_copies) + [pltpu.HBM(b.shape, b.dtype) for b in bufs]
        + [jax.ShapeDtypeStruct((8, LANES), F32)],
        in_specs=[_HBM] * nb + _any_specs(na),
        out_specs=[_SEM] * (2 * n_copies) + [_HBM] * nb + [pl.BlockSpec(memory_space=pltpu.VMEM)],
        input_output_aliases={i: 2 * n_copies + i for i in range(nb)},
        compiler_params=pltpu.CompilerParams(has_side_effects=_EFFECT),
    )(*[pltpu.with_memory_space_constraint(b, pltpu.HBM) for b in bufs], *after)
    return _InFlight(out[:2 * n_copies], out[2 * n_copies:2 * n_copies + nb], out[-1], ns, n_copies, make_copies, name)


def _exchange_wait(fl, after=()):
    nb, na, nc, ns = len(fl.bufs), len(after), fl.n_copies, fl.n_src

    def body(*refs):
        b_refs = refs[:nb]
        sems = refs[nb:nb + 2 * nc]
        copies = fl.make_copies(b_refs[:ns], b_refs[ns:], sems[:nc], sems[nc:])
        for cp in copies:
            cp.wait_send()
        for cp in copies:
            cp.wait_recv()

    out = pl.pallas_call(
        body, name=fl.name + "_wait",
        out_shape=[pltpu.HBM(b.shape, b.dtype) for b in fl.bufs],
        in_specs=[_HBM] * nb + [_SEM] * (2 * nc) + _any_specs(na),
        out_specs=[_HBM] * nb,
        input_output_aliases={i: i for i in range(nb)},
        compiler_params=pltpu.CompilerParams(has_side_effects=_EFFECT),
    )(*fl.bufs, *fl.sems, *after)
    return list(out[:ns]), list(out[ns:])


def _remote(src, dst, send_sem, recv_sem, to):
    return pltpu.make_async_remote_copy(src_ref=src, dst_ref=dst, send_sem=send_sem, recv_sem=recv_sem,
                                        device_id=to, device_id_type=pl.DeviceIdType.MESH)


def _gather_stage1_copies(n):
    def make(s_refs, l_refs, send, recv):
        x, y, c = _mesh_pos()
        own = 4 * x + 2 * y + c
        targets = [(x, y, 1 - c), (1 - x, y, c), (x, 1 - y, c), (1 - x, 1 - y, c)]
        return [_remote(s_refs[a], l_refs[a].at[own], send[4 * a + k], recv[4 * a + k], to)
                for a in range(n) for k, to in enumerate(targets)]
    return make


def _gather_stage2_copies(n):
    def make(s_refs, l_refs, send, recv):
        x, y, c = _mesh_pos()
        blocks = [4 * (1 - x) + 2 * y + c, 4 * x + 2 * (1 - y) + c, 4 * (1 - x) + 2 * (1 - y) + c]
        return [_remote(l_refs[a].at[b], l_refs[a].at[b], send[3 * a + j], recv[3 * a + j], (x, y, 1 - c))
                for a in range(n) for j, b in enumerate(blocks)]
    return make


def _pair_copies(n):
    def make(s_refs, l_refs, send, recv):
        x, y, c = _mesh_pos()
        return [_remote(s_refs[a].at[2 * k + 1 - c], l_refs[a].at[k], send[4 * a + k], recv[4 * a + k], (x, y, 1 - c))
                for a in range(n) for k in range(4)]
    return make


def _chip_copies(n):
    def make(s_refs, l_refs, send, recv):
        x, y, c = _mesh_pos()
        chips = [(1 - x, y), (x, 1 - y), (1 - x, 1 - y)]
        return [_remote(s_refs[a].at[2 * ch[0] + ch[1]], l_refs[a].at[j], send[3 * a + j], recv[3 * a + j], (*ch, c))
                for a in range(n) for j, ch in enumerate(chips)]
    return make


def _place_own_call(shard, dev, name):
    R, C = shard.shape
    tr = _tile(R, max(16, MIB // (C * shard.dtype.itemsize)), 16)

    def body(d_ref, s_ref, o_ref):
        o_ref[...] = s_ref[...]

    grid_spec = pltpu.PrefetchScalarGridSpec(
        num_scalar_prefetch=1, grid=(R // tr,),
        in_specs=[pl.BlockSpec((tr, C), lambda r, d: (r, 0))],
        out_specs=pl.BlockSpec((None, tr, C), lambda r, d: (d[0], r, 0)))
    return pl.pallas_call(body, name=name, grid_spec=grid_spec,
                          out_shape=jax.ShapeDtypeStruct((N_DEV, R, C), shard.dtype), compiler_params=_cp(1))(dev, shard)


def _all_gather_call(shards, name):
    n = len(shards)

    def body(*refs):
        x_refs, o_refs = refs[:n], refs[n:2 * n]
        send_sems, recv_sems, local_sems = refs[2 * n:]
        x, y, c = _mesh_pos()
        me, sib = (x, y, c), (x, y, 1 - c)
        chips = [(1 - x, y), (x, 1 - y), (1 - x, 1 - y)]

        def copy(a, k, block, to, src=None):
            dst = o_refs[a].at[4 * block[0] + 2 * block[1] + block[2]]
            return pltpu.make_async_remote_copy(
                src_ref=dst if src is None else src, dst_ref=dst,
                send_sem=send_sems.at[a, k], recv_sem=recv_sems.at[a, k],
                device_id=to, device_id_type=pl.DeviceIdType.MESH)

        mine = [pltpu.make_async_copy(x_refs[a], o_refs[a].at[4 * x + 2 * y + c], local_sems.at[a]) for a in range(n)]
        for cp in mine:
            cp.start()
        first = []
        for a in range(n):
            first.append(copy(a, 0, me, sib, src=x_refs[a]))
            for j, chip in enumerate(chips):
                first.append(copy(a, 1 + j, me, (*chip, c), src=x_refs[a]))
        for cp in first:
            cp.start()
        passed = []
        for a in range(n):
            for j, chip in enumerate(chips):
                copy(a, 1 + j, (*chip, c), me).wait_recv()
                fwd = copy(a, 4 + j, (*chip, c), sib)
                fwd.start()
                passed.append(fwd)
        for a in range(n):
            copy(a, 0, sib, me).wait_recv()
            for j, chip in enumerate(chips):
                copy(a, 4 + j, (*chip, 1 - c), me).wait_recv()
        for cp in first + passed:
            cp.wait_send()
        for cp in mine:
            cp.wait()

    return pl.pallas_call(
        body, name=name,
        out_shape=[jax.ShapeDtypeStruct((N_DEV,) + s.shape, s.dtype) for s in shards],
        in_specs=_any_specs(n), out_specs=_any_specs(n),
        scratch_shapes=[pltpu.SemaphoreType.DMA((n, 7)), pltpu.SemaphoreType.DMA((n, 7)), pltpu.SemaphoreType.DMA((n,))],
    )(*shards)


def _pair_exchange_call(grads, name):
    n = len(grads)

    def body(*refs):
        g_refs, r_refs = refs[:n], refs[n:2 * n]
        send_sems, recv_sems = refs[2 * n:]
        x, y, c = _mesh_pos()
        copies = []
        for a in range(n):
            for k in range(4):
                copies.append(pltpu.make_async_remote_copy(
                    src_ref=g_refs[a].at[2 * k + 1 - c], dst_ref=r_refs[a].at[k],
                    send_sem=send_sems.at[a, k], recv_sem=recv_sems.at[a, k],
                    device_id=(x, y, 1 - c), device_id_type=pl.DeviceIdType.MESH))
        for cp in copies:
            cp.start()
        for cp in copies:
            cp.wait_recv()
        for cp in copies:
            cp.wait_send()

    return pl.pallas_call(
        body, name=name,
        out_shape=[jax.ShapeDtypeStruct((4,) + g.shape[1:], g.dtype) for g in grads],
        in_specs=_any_specs(n), out_specs=_any_specs(n),
        scratch_shapes=[pltpu.SemaphoreType.DMA((n, 4)), pltpu.SemaphoreType.DMA((n, 4))],
    )(*grads)


def _chip_exchange_call(parts, name):
    n = len(parts)

    def body(*refs):
        p_refs, r_refs = refs[:n], refs[n:2 * n]
        send_sems, recv_sems = refs[2 * n:]
        x, y, c = _mesh_pos()
        chips = [(1 - x, y), (x, 1 - y), (1 - x, 1 - y)]
        copies = []
        for a in range(n):
            for j, chip in enumerate(chips):
                copies.append(pltpu.make_async_remote_copy(
                    src_ref=p_refs[a].at[2 * chip[0] + chip[1]], dst_ref=r_refs[a].at[j],
                    send_sem=send_sems.at[a, j], recv_sem=recv_sems.at[a, j],
                    device_id=(*chip, c), device_id_type=pl.DeviceIdType.MESH))
        for cp in copies:
            cp.start()
        for cp in copies:
            cp.wait_recv()
        for cp in copies:
            cp.wait_send()

    return pl.pallas_call(
        body, name=name,
        out_shape=[jax.ShapeDtypeStruct((3,) + p.shape[1:], p.dtype) for p in parts],
        in_specs=_any_specs(n), out_specs=_any_specs(n),
        scratch_shapes=[pltpu.SemaphoreType.DMA((n, 3)), pltpu.SemaphoreType.DMA((n, 3))],
    )(*parts)


def _pair_add_call(g, r1, core_chip, name):
    _, R, C = g.shape
    tr = _tile(R, max(8, (MIB // 2) // (C * 4)), 16)

    def body(cc_ref, g_ref, r_ref, p32_ref, p16_ref):
        s = g_ref[...] + r_ref[...]
        p16_ref[...] = s.astype(BF16)

        @pl.when(pl.program_id(1) == cc_ref[1])
        def _():
            p32_ref[...] = s

    grid_spec = pltpu.PrefetchScalarGridSpec(
        num_scalar_prefetch=1, grid=(R // tr, 4),
        in_specs=[pl.BlockSpec((None, tr, C), lambda r, k, cc: (2 * k + cc[0], r, 0)),
                  pl.BlockSpec((None, tr, C), lambda r, k, cc: (k, r, 0))],
        out_specs=[pl.BlockSpec((tr, C), lambda r, k, cc: (r, 0)),
                   pl.BlockSpec((None, tr, C), lambda r, k, cc: (k, r, 0))])
    return pl.pallas_call(
        body, name=name, grid_spec=grid_spec,
        out_shape=[jax.ShapeDtypeStruct((R, C), F32), jax.ShapeDtypeStruct((4, R, C), BF16)],
        compiler_params=_cp(2))(core_chip, g, r1)


def _adamw_call(w, m, v, addends, name, deps=()):
    R, C = w.shape
    tr = _tile(R, max(8, (MIB // 2) // (C * 4)), 16)
    na = len(addends)
    c1 = 1.0 - ADAM_B1 ** ADAM_STEP
    c2 = 1.0 - ADAM_B2 ** ADAM_STEP

    def body(*refs):
        w_ref, m_ref, v_ref = refs[:3]
        a_refs = refs[3:3 + na]
        g_ref, d_ref, nm_ref, nv_ref = refs[3 + na:]
        g = a_refs[0][...].astype(F32)
        for a_ref in a_refs[1:]:
            g = g + a_ref[...].astype(F32)
        nm = ADAM_B1 * m_ref[...] + (1.0 - ADAM_B1) * g
        nv = ADAM_B2 * v_ref[...] + (1.0 - ADAM_B2) * (g * g)
        g_ref[...] = g
        nm_ref[...] = nm
        nv_ref[...] = nv
        d_ref[...] = -ADAM_LR * ((nm / c1) / (jnp.sqrt(nv / c2) + ADAM_EPS) + ADAM_WD * w_ref[...])

    flat = pl.BlockSpec((tr, C), lambda r: (r, 0))
    a_specs = [flat if k is None else pl.BlockSpec((None, tr, C), functools.partial(lambda r, kk: (kk, r, 0), kk=k))
               for _, k in addends]
    out = jax.ShapeDtypeStruct((R, C), F32)
    return _pallas(
        body, 3 + na, deps, name=name, grid=(R // tr,),
        in_specs=[flat, flat, flat] + a_specs, out_specs=[flat] * 4, out_shape=[out] * 4,
        compiler_params=_cp(1))(w, m, v, *[a for a, _ in addends], *deps)


def _sum_call(parts, row0, rows, name):
    n = parts.shape[0]
    tr = _tile(math.gcd(row0, rows), 256, 8)
    b0 = row0 // tr

    def body(p_ref, o_ref):
        s = p_ref[0]
        for k in range(1, n):
            s = s + p_ref[k]
        o_ref[...] = s

    return pl.pallas_call(
        body, name=name, grid=(rows // tr,),
        in_specs=[pl.BlockSpec((n, tr, LANES), lambda r: (0, r + b0, 0))],
        out_specs=pl.BlockSpec((tr, LANES), lambda r: (r, 0)),
        out_shape=jax.ShapeDtypeStruct((rows, LANES), F32), compiler_params=_cp(1))(parts)


def _rmsnorm_call(x, g, name, deps=()):
    S, D = x.shape
    tm = _tile(S, 512, 16)

    def body(x_ref, g_ref, o_ref):
        xv = x_ref[...]
        r = lax.rsqrt(_rows_mean(xv * xv) + RMS_EPS)
        o_ref[...] = (xv * r * g_ref[...]).astype(BF16)

    return _pallas(
        body, 2, deps, name=name, grid=(S // tm,),
        in_specs=[pl.BlockSpec((tm, D), lambda i: (i, 0)), pl.BlockSpec((1, D), lambda i: (0, 0))],
        out_specs=pl.BlockSpec((tm, D), lambda i: (i, 0)),
        out_shape=jax.ShapeDtypeStruct((S, D), BF16), compiler_params=_cp(1))(x, g, *deps)


def _mm_blocked_call(a, wg, out_dtype, halves, name, deps=()):
    S, K = a.shape
    nb, _, bn = wg.shape
    tm = _tile(S, 1024, 16)
    tn = _tile(bn, 768, LANES)
    nsub = bn // tn
    J = nb * nsub

    def body(a_ref, w_ref, o_ref):
        o_ref[...] = jnp.dot(a_ref[...], w_ref[...], preferred_element_type=F32).astype(out_dtype)

    if halves:
        nh = J // 2
        out_spec = pl.BlockSpec((None, tm, tn), lambda i, j: (j // nh, i, j % nh))
        out_shape = jax.ShapeDtypeStruct((2, S, nb * bn // 2), out_dtype)
    else:
        out_spec = pl.BlockSpec((tm, tn), lambda i, j: (i, j))
        out_shape = jax.ShapeDtypeStruct((S, nb * bn), out_dtype)
    return _pallas(
        body, 2, deps, name=name, grid=(S // tm, J),
        in_specs=[pl.BlockSpec((tm, K), lambda i, j: (i, 0)),
                  pl.BlockSpec((None, K, tn), lambda i, j: (j // nsub, 0, j % nsub))],
        out_specs=out_spec, out_shape=out_shape, compiler_params=_cp(2))(a, wg, *deps)


def _mm_out_call(x, y, w, name):
    S, D = x.shape
    tm = _tile(S, 512, 16)

    def body(x_ref, y_ref, w_ref, o_ref):
        o_ref[...] = x_ref[...] + jnp.dot(y_ref[...], w_ref[...], preferred_element_type=F32)

    return pl.pallas_call(
        body, name=name, grid=(S // tm,),
        in_specs=[pl.BlockSpec((tm, D), lambda i: (i, 0)), pl.BlockSpec((tm, D), lambda i: (i, 0)),
                  pl.BlockSpec((D, D), lambda i: (0, 0))],
        out_specs=pl.BlockSpec((tm, D), lambda i: (i, 0)),
        out_shape=jax.ShapeDtypeStruct((S, D), F32), compiler_params=_cp(1))(x, y, w)


def _mm_nt_call(a, w, out_dtype, name, deps=()):
    S, K = a.shape
    N = w.shape[0]
    tm = _tile(S, 1024, 16)
    tn = _tile(N, 768, LANES)

    def body(a_ref, w_ref, o_ref):
        o_ref[...] = lax.dot_general(a_ref[...], w_ref[...], _DN_NT, preferred_element_type=F32).astype(out_dtype)

    return _pallas(
        body, 2, deps, name=name, grid=(S // tm, N // tn),
        in_specs=[pl.BlockSpec((tm, K), lambda i, j: (i, 0)), pl.BlockSpec((tn, K), lambda i, j: (j, 0))],
        out_specs=pl.BlockSpec((tm, tn), lambda i, j: (i, j)),
        out_shape=jax.ShapeDtypeStruct((S, N), out_dtype), compiler_params=_cp(2))(a, w, *deps)


def _mm_down_loss_call(x2, f, w, final_g, target, name):
    S, D = x2.shape
    Fd = f.shape[1]
    tm = _tile(S, 512, 16)
    tk = _tile(Fd, 1536, LANES)
    nk = Fd // tk

    def body(x_ref, f_ref, w_ref, g_ref, t_ref, dx_ref, dxb_ref, loss_ref, dg_ref, acc):
        i, k = pl.program_id(0), pl.program_id(1)

        @pl.when(jnp.logical_and(i == 0, k == 0))
        def _():
            loss_ref[...] = jnp.zeros_like(loss_ref)
            dg_ref[...] = jnp.zeros_like(dg_ref)

        @pl.when(k == 0)
        def _():
            acc[...] = jnp.zeros_like(acc)

        acc[...] += jnp.dot(f_ref[...], w_ref[...], preferred_element_type=F32)

        @pl.when(k == nk - 1)
        def _():
            x3 = x_ref[...] + acc[...]
            r = lax.rsqrt(_rows_mean(x3 * x3) + RMS_EPS)
            g = g_ref[...]
            xn = x3 * r
            diff = xn * g - t_ref[...]
            loss_ref[...] += 0.5 * jnp.sum(_rows_mean(diff * diff))
            dout = diff * (1.0 / D)
            dg_ref[...] += _col_sum(dout * xn)
            dyg = dout * g
            dx = r * (dyg - xn * _rows_mean(dyg * xn))
            dx_ref[...] = dx
            dxb_ref[...] = dx.astype(BF16)

    row = lambda i, k: (i, 0)
    return pl.pallas_call(
        body, name=name, grid=(S // tm, nk),
        in_specs=[pl.BlockSpec((tm, D), row, pipeline_mode=pl.Buffered(1)), pl.BlockSpec((tm, tk), lambda i, k: (i, k)),
                  pl.BlockSpec((tk, D), lambda i, k: (k, 0)), pl.BlockSpec((1, D), lambda i, k: (0, 0)),
                  pl.BlockSpec((tm, D), row, pipeline_mode=pl.Buffered(1))],
        out_specs=[pl.BlockSpec((tm, D), row), pl.BlockSpec((tm, D), row),
                   pl.BlockSpec((8, LANES), lambda i, k: (0, 0)), pl.BlockSpec((1, D), lambda i, k: (0, 0))],
        out_shape=[jax.ShapeDtypeStruct((S, D), F32), jax.ShapeDtypeStruct((S, D), BF16),
                   jax.ShapeDtypeStruct((8, LANES), F32), jax.ShapeDtypeStruct((1, D), F32)],
        scratch_shapes=[pltpu.VMEM((tm, D), F32)], compiler_params=_cp(2, 56))(x2, f, w, final_g, target)


def _mm_dx_norm_call(a3, wg, resid, xin, g, name, deps=()):
    na, S, Fa = a3.shape
    nb, D, bn = wg.shape
    tm = _tile(S, 512, 16)
    tk = _tile(bn, 1536, LANES)
    nsub = bn // tk
    nka = Fa // tk
    nk = nb * nsub
    assert na * nka == nk

    def body(a_ref, w_ref, r_ref, x_ref, g_ref, dx_ref, dxb_ref, dg_ref, acc):
        i, k = pl.program_id(0), pl.program_id(1)

        @pl.when(jnp.logical_and(i == 0, k == 0))
        def _():
            dg_ref[...] = jnp.zeros_like(dg_ref)

        @pl.when(k == 0)
        def _():
            acc[...] = jnp.zeros_like(acc)

        acc[...] += lax.dot_general(a_ref[...], w_ref[...], _DN_NT, preferred_element_type=F32)

        @pl.when(k == nk - 1)
        def _():
            dh = acc[...]
            xv = x_ref[...]
            r = lax.rsqrt(_rows_mean(xv * xv) + RMS_EPS)
            xn = xv * r
            dg_ref[...] += _col_sum(dh * xn)
            dyg = dh * g_ref[...]
            dx = r_ref[...] + r * (dyg - xn * _rows_mean(dyg * xn))
            dx_ref[...] = dx
            dxb_ref[...] = dx.astype(BF16)

    row = lambda i, k: (i, 0)
    return _pallas(
        body, 5, deps, name=name, grid=(S // tm, nk),
        in_specs=[pl.BlockSpec((None, tm, tk), lambda i, k: (k // nka, i, k % nka)),
                  pl.BlockSpec((None, D, tk), lambda i, k: (k // nsub, 0, k % nsub)),
                  pl.BlockSpec((tm, D), row, pipeline_mode=pl.Buffered(1)),
                  pl.BlockSpec((tm, D), row, pipeline_mode=pl.Buffered(1)), pl.BlockSpec((1, D), lambda i, k: (0, 0))],
        out_specs=[pl.BlockSpec((tm, D), row), pl.BlockSpec((tm, D), row), pl.BlockSpec((1, D), lambda i, k: (0, 0))],
        out_shape=[jax.ShapeDtypeStruct((S, D), F32), jax.ShapeDtypeStruct((S, D), BF16),
                   jax.ShapeDtypeStruct((1, D), F32)],
        scratch_shapes=[pltpu.VMEM((tm, D), F32)], compiler_params=_cp(2, 56))(a3, wg, resid, xin, g, *deps)


def _unblock_call(wg, name):
    nb, K, bn = wg.shape

    def body(w_ref, o_ref):
        o_ref[...] = w_ref[...]

    return pl.pallas_call(
        body, name=name, grid=(nb,),
        in_specs=[pl.BlockSpec((None, K, bn), lambda o: (o, 0, 0))],
        out_specs=pl.BlockSpec((K, bn), lambda o: (0, o)),
        out_shape=jax.ShapeDtypeStruct((K, nb * bn), wg.dtype), compiler_params=_cp(1))(wg)


def _mm_nt_norm_call(a, w, resid, xin, g, name, deps=()):
    S, K = a.shape
    D = w.shape[0]
    tm = _tile(S, 256, 16)

    def body(a_ref, w_ref, r_ref, x_ref, g_ref, dx_ref, dg_ref):
        @pl.when(pl.program_id(0) == 0)
        def _():
            dg_ref[...] = jnp.zeros_like(dg_ref)

        dh = lax.dot_general(a_ref[...], w_ref[...], _DN_NT, preferred_element_type=F32)
        xv = x_ref[...]
        r = lax.rsqrt(_rows_mean(xv * xv) + RMS_EPS)
        xn = xv * r
        dg_ref[...] += _col_sum(dh * xn)
        dyg = dh * g_ref[...]
        dx_ref[...] = r_ref[...] + r * (dyg - xn * _rows_mean(dyg * xn))

    row = lambda i: (i, 0)
    fixed = lambda i: (0, 0)
    return _pallas(
        body, 5, deps, name=name, grid=(S // tm,),
        in_specs=[pl.BlockSpec((tm, K), row), pl.BlockSpec((D, K), fixed, pipeline_mode=pl.Buffered(1)),
                  pl.BlockSpec((tm, D), row), pl.BlockSpec((tm, D), row), pl.BlockSpec((1, D), fixed)],
        out_specs=[pl.BlockSpec((tm, D), row), pl.BlockSpec((1, D), fixed)],
        out_shape=[jax.ShapeDtypeStruct((S, D), F32), jax.ShapeDtypeStruct((1, D), F32)],
        compiler_params=_cp(1, 56))(a, w, resid, xin, g, *deps)


def _mm_tn_cols_call(a, b3, nb, bn, name, deps=()):
    S, Ka = a.shape
    nh, _, Fb = b3.shape
    tm = _tile(S, 2048, 16)
    tn = _tile(bn, 768, LANES)
    nsub = bn // tn
    njb = Fb // tn
    J = nb * nsub
    assert nh * njb == J

    def body(a_ref, b_ref, o_ref):
        @pl.when(pl.program_id(1) == 0)
        def _():
            o_ref[...] = jnp.zeros_like(o_ref)

        o_ref[...] += lax.dot_general(a_ref[...], b_ref[...], _DN_TN, preferred_element_type=F32)

    return _pallas(
        body, 2, deps, name=name, grid=(J, S // tm),
        in_specs=[pl.BlockSpec((tm, Ka), lambda j, i: (i, 0)),
                  pl.BlockSpec((None, tm, tn), lambda j, i: (j // njb, i, j % njb))],
        out_specs=pl.BlockSpec((None, Ka, tn), lambda j, i: (j // nsub, 0, j % nsub)),
        out_shape=jax.ShapeDtypeStruct((nb, Ka, bn), F32), compiler_params=_cp(2, 56))(a, b3, *deps)


def _mm_tn_rows_call(a, b, name, deps=()):
    S, E = a.shape
    D = b.shape[1]
    tm = _tile(S, 2048, 16)
    te = _tile(E, 768, LANES)

    def body(a_ref, b_ref, o_ref):
        @pl.when(pl.program_id(1) == 0)
        def _():
            o_ref[...] = jnp.zeros_like(o_ref)

        o_ref[...] += lax.dot_general(a_ref[...], b_ref[...], _DN_TN, preferred_element_type=F32)

    return _pallas(
        body, 2, deps, name=name, grid=(E // te, S // tm),
        in_specs=[pl.BlockSpec((tm, te), lambda j, i: (i, j)), pl.BlockSpec((tm, D), lambda j, i: (i, 0))],
        out_specs=pl.BlockSpec((te, D), lambda j, i: (j, 0)),
        out_shape=jax.ShapeDtypeStruct((E, D), F32), compiler_params=_cp(2, 56))(a, b, *deps)


def _ffn_tiles(S, Fd):
    return _tile(S, 512, 16), _tile(Fd, 768, LANES)


def _taps(cw_ref):
    return [cw_ref[k:k + 1, :] for k in range(cw_ref.shape[0])]


def _conv3(prev8, cur, taps):
    s1 = _shift_down(prev8, cur, 1)
    s2 = _shift_down(prev8, cur, 2)
    return taps[2] * cur + taps[1] * s1 + taps[0] * s2, s1, s2


def _ffn_act_call(up3, cw, cb, name):
    _, S, Fd = up3.shape
    tm, tc = _ffn_tiles(S, Fd)
    nj = Fd // tc
    hb = tm // 16

    def body(g_ref, v_ref, gp_ref, vp_ref, cwg_ref, cwv_ref, cbg_ref, cbv_ref, o_ref, c_ref):
        first = pl.program_id(0) == 0
        keep = jnp.where(first, 0.0, 1.0)
        gp = gp_ref[...].astype(F32)[8:] * keep
        vp = vp_ref[...].astype(F32)[8:] * keep
        cg, _, _ = _conv3(gp, g_ref[...].astype(F32), _taps(cwg_ref))
        cv, _, _ = _conv3(vp, v_ref[...].astype(F32), _taps(cwv_ref))
        cg = cg + cbg_ref[...]
        cv = cv + cbv_ref[...]
        o_ref[...] = (_gelu(cg) * cv).astype(BF16)
        c_ref[0] = cg.astype(BF16)
        c_ref[1] = cv.astype(BF16)

    prev = lambda i, j: (jnp.maximum(i * hb - 1, 0), j)
    return pl.pallas_call(
        body, name=name, grid=(S // tm, nj),
        in_specs=[pl.BlockSpec((None, tm, tc), lambda i, j: (0, i, j)), pl.BlockSpec((None, tm, tc), lambda i, j: (1, i, j)),
                  pl.BlockSpec((None, 16, tc), lambda i, j: (0,) + prev(i, j)),
                  pl.BlockSpec((None, 16, tc), lambda i, j: (1,) + prev(i, j)),
                  pl.BlockSpec((3, tc), lambda i, j: (0, j)), pl.BlockSpec((3, tc), lambda i, j: (0, j + nj)),
                  pl.BlockSpec((1, tc), lambda i, j: (0, j)), pl.BlockSpec((1, tc), lambda i, j: (0, j + nj))],
        out_specs=[pl.BlockSpec((tm, tc), lambda i, j: (i, j)), pl.BlockSpec((2, tm, tc), lambda i, j: (0, i, j))],
        out_shape=[jax.ShapeDtypeStruct((S, Fd), BF16), jax.ShapeDtypeStruct((2, S, Fd), BF16)],
        compiler_params=_cp(2))(up3, up3, up3, up3, cw, cw, cb, cb)


def _ffn_act_bwd_call(up3, upc3, d_f, cw, name):
    _, S, Fd = up3.shape
    tm, tc = _ffn_tiles(S, Fd)
    nj = Fd // tc
    ni = S // tm
    hb = tm // 16

    def body(g_ref, v_ref, cg_ref, cv_ref, cgn_ref, cvn_ref, df_ref, dfn_ref, cwg_ref, cwv_ref,
             dup_ref, dcwg_ref, dcwv_ref, dcbg_ref, dcbv_ref):
        i = pl.program_id(1)
        keep_next = jnp.where(i == ni - 1, 0.0, 1.0)

        @pl.when(i == 0)
        def _():
            for r in (dcwg_ref, dcwv_ref, dcbg_ref, dcbv_ref):
                r[...] = jnp.zeros_like(r)

        df = jnp.concatenate([df_ref[...].astype(F32), dfn_ref[...].astype(F32)[:8] * keep_next], axis=0)
        cg = jnp.concatenate([cg_ref[...].astype(F32), cgn_ref[...].astype(F32)[:8]], axis=0)
        cv = jnp.concatenate([cv_ref[...].astype(F32), cvn_ref[...].astype(F32)[:8]], axis=0)
        gel, dgel = _gelu_parts(cg)

        def back(d, cw_ref, x_ref, dcw_ref, dcb_ref, slab):
            taps = _taps(cw_ref)
            d0 = d[:tm]
            d1 = pltpu.roll(d, tm + 8 - 1, 0)[:tm]
            d2 = pltpu.roll(d, tm + 8 - 2, 0)[:tm]
            dup_ref[slab] = (taps[2] * d0 + taps[1] * d1 + taps[0] * d2).astype(BF16)
            xv = x_ref[...].astype(F32)
            dcw_ref[2:3, :] += _col_sum(xv * d0)
            dcw_ref[1:2, :] += _col_sum(xv * d1)
            dcw_ref[0:1, :] += _col_sum(xv * d2)
            dcb_ref[...] += _col_sum(d0)

        back(df * cv * dgel, cwg_ref, g_ref, dcwg_ref, dcbg_ref, 0)
        back(df * gel, cwv_ref, v_ref, dcwv_ref, dcbv_ref, 1)

    nxt = lambda j, i: (jnp.minimum((i + 1) * hb, S // 16 - 1), j)
    main = lambda s: pl.BlockSpec((None, tm, tc), lambda j, i: (s, i, j))
    halo = lambda s: pl.BlockSpec((None, 16, tc), lambda j, i: (s,) + nxt(j, i))
    acc3 = pl.BlockSpec((3, tc), lambda j, i: (0, j))
    acc1 = pl.BlockSpec((1, tc), lambda j, i: (0, j))
    return pl.pallas_call(
        body, name=name, grid=(nj, ni),
        in_specs=[main(0), main(1), main(0), main(1), halo(0), halo(1),
                  pl.BlockSpec((tm, tc), lambda j, i: (i, j)), pl.BlockSpec((16, tc), nxt),
                  pl.BlockSpec((3, tc), lambda j, i: (0, j)), pl.BlockSpec((3, tc), lambda j, i: (0, j + nj))],
        out_specs=[pl.BlockSpec((2, tm, tc), lambda j, i: (0, i, j)), acc3, acc3, acc1, acc1],
        out_shape=[jax.ShapeDtypeStruct((2, S, Fd), BF16), jax.ShapeDtypeStruct((3, Fd), F32),
                   jax.ShapeDtypeStruct((3, Fd), F32), jax.ShapeDtypeStruct((1, Fd), F32),
                   jax.ShapeDtypeStruct((1, Fd), F32)],
        compiler_params=_cp(2, 56))(up3, up3, upc3, upc3, upc3, upc3, d_f, d_f, cw, cw)


SUB_LANES = 256


def _lane_taps(cw_ref, ls):
    return [cw_ref[k:k + 1, ls] for k in range(cw_ref.shape[0])]


def _ffn_down_fused_call(up3, cw, cb, w, x2, final_g, target, name):
    _, S, Fd = up3.shape
    D = x2.shape[1]
    tm, tc = _ffn_tiles(S, Fd)
    nk = Fd // tc
    hb = tm // 16
    sc = _tile(tc, SUB_LANES, LANES)

    def body(g_ref, v_ref, gp_ref, vp_ref, cwg_ref, cwv_ref, cbg_ref, cbv_ref, w_ref, x_ref, fg_ref, t_ref,
             f_ref, dx_ref, dxb_ref, loss_ref, dg_ref, acc):
        i, k = pl.program_id(0), pl.program_id(1)

        @pl.when(jnp.logical_and(i == 0, k == 0))
        def _():
            loss_ref[...] = jnp.zeros_like(loss_ref)
            dg_ref[...] = jnp.zeros_like(dg_ref)

        @pl.when(k == 0)
        def _():
            acc[...] = jnp.zeros_like(acc)

        keep = jnp.where(i == 0, 0.0, 1.0)
        part = None
        for s in range(tc // sc):
            ls = slice(s * sc, (s + 1) * sc)
            gp = gp_ref[:, ls].astype(F32)[8:] * keep
            vp = vp_ref[:, ls].astype(F32)[8:] * keep
            cg, _, _ = _conv3(gp, g_ref[:, ls].astype(F32), _lane_taps(cwg_ref, ls))
            cv, _, _ = _conv3(vp, v_ref[:, ls].astype(F32), _lane_taps(cwv_ref, ls))
            fs = (_gelu(cg + cbg_ref[:, ls]) * (cv + cbv_ref[:, ls])).astype(BF16)
            f_ref[:, ls] = fs
            d = jnp.dot(fs, w_ref[s * sc:(s + 1) * sc, :], preferred_element_type=F32)
            part = d if part is None else part + d
        acc[...] += part

        @pl.when(k == nk - 1)
        def _():
            x3 = x_ref[...] + acc[...]
            r = lax.rsqrt(_rows_mean(x3 * x3) + RMS_EPS)
            g = fg_ref[...]
            xn = x3 * r
            diff = xn * g - t_ref[...]
            loss_ref[...] += 0.5 * jnp.sum(_rows_mean(diff * diff))
            dout = diff * (1.0 / D)
            dg_ref[...] += _col_sum(dout * xn)
            dyg = dout * g
            dx = r * (dyg - xn * _rows_mean(dyg * xn))
            dx_ref[...] = dx
            dxb_ref[...] = dx.astype(BF16)

    row = lambda i, k: (i, 0)
    prev = lambda i, k: (jnp.maximum(i * hb - 1, 0), k)
    once = pl.Buffered(1)
    return pl.pallas_call(
        body, name=name, grid=(S // tm, nk),
        in_specs=[pl.BlockSpec((None, tm, tc), lambda i, k: (0, i, k)), pl.BlockSpec((None, tm, tc), lambda i, k: (1, i, k)),
                  pl.BlockSpec((None, 16, tc), lambda i, k: (0,) + prev(i, k)),
                  pl.BlockSpec((None, 16, tc), lambda i, k: (1,) + prev(i, k)),
                  pl.BlockSpec((3, tc), lambda i, k: (0, k)), pl.BlockSpec((3, tc), lambda i, k: (0, k + nk)),
                  pl.BlockSpec((1, tc), lambda i, k: (0, k)), pl.BlockSpec((1, tc), lambda i, k: (0, k + nk)),
                  pl.BlockSpec((tc, D), lambda i, k: (k, 0)),
                  pl.BlockSpec((tm, D), row, pipeline_mode=once), pl.BlockSpec((1, D), lambda i, k: (0, 0)),
                  pl.BlockSpec((tm, D), row, pipeline_mode=once)],
        out_specs=[pl.BlockSpec((tm, tc), lambda i, k: (i, k)), pl.BlockSpec((tm, D), row), pl.BlockSpec((tm, D), row),
                   pl.BlockSpec((8, LANES), lambda i, k: (0, 0)), pl.BlockSpec((1, D), lambda i, k: (0, 0))],
        out_shape=[jax.ShapeDtypeStruct((S, Fd), BF16), jax.ShapeDtypeStruct((S, D), F32),
                   jax.ShapeDtypeStruct((S, D), BF16), jax.ShapeDtypeStruct((8, LANES), F32),
                   jax.ShapeDtypeStruct((1, D), F32)],
        scratch_shapes=[pltpu.VMEM((tm, D), F32)],
        compiler_params=_cp(2, 56))(up3, up3, up3, up3, cw, cw, cb, cb, w, x2, final_g, target)


def _ffn_bwd_fused_call(up3, d_f, cw, cb, wg, resid, xin, g, name, deps=()):
    _, S, Fd = up3.shape
    nb, D, bn = wg.shape
    tm, tc = _tile(S, 256, 16), _ffn_tiles(S, Fd)[1]
    nk = Fd // tc
    ni = S // tm
    hb = tm // 16
    nsubw = bn // tc
    half = nb // 2
    sc = _tile(tc, SUB_LANES, LANES)

    def body(g_ref, v_ref, gp_ref, vp_ref, gn_ref, vn_ref, df_ref, dfn_ref, cwg_ref, cwv_ref, cbg_ref, cbv_ref,
             wgate_ref, wval_ref, r_ref, x_ref, ng_ref,
             dup_ref, dx_ref, dxb_ref, dg_ref, dcwg_ref, dcwv_ref, dcbg_ref, dcbv_ref, acc):
        i, k = pl.program_id(0), pl.program_id(1)

        @pl.when(jnp.logical_and(i == 0, k == 0))
        def _():
            for r in (dg_ref, dcwg_ref, dcwv_ref, dcbg_ref, dcbv_ref):
                r[...] = jnp.zeros_like(r)

        @pl.when(k == 0)
        def _():
            acc[...] = jnp.zeros_like(acc)

        keep_prev = jnp.where(i == 0, 0.0, 1.0)
        keep_next = jnp.where(i == ni - 1, 0.0, 1.0)
        zeros8 = jnp.zeros((8, sc), F32)
        part = None
        for s in range(tc // sc):
            ls = slice(s * sc, (s + 1) * sc)
            off = pl.multiple_of(k * tc + s * sc, LANES)
            df = jnp.concatenate([df_ref[:, ls].astype(F32), dfn_ref[:, ls].astype(F32)[:8] * keep_next], axis=0)

            def half_fwd(x_ref_, xp_ref, xn_ref, cw_ref, cb_ref):
                taps = _lane_taps(cw_ref, ls)
                prev8 = xp_ref[:, ls].astype(F32)[8:] * keep_prev
                ext = jnp.concatenate([x_ref_[:, ls].astype(F32), xn_ref[:, ls].astype(F32)[:8]], axis=0)
                conv, s1, s2 = _conv3(prev8, ext, taps)
                return taps, conv + cb_ref[:, ls], (ext, s1, s2)

            tg, cg, gsh = half_fwd(g_ref, gp_ref, gn_ref, cwg_ref, cbg_ref)
            tv, cv, vsh = half_fwd(v_ref, vp_ref, vn_ref, cwv_ref, cbv_ref)
            gel, dgel = _gelu_parts(cg)
            d_gate = df * cv * dgel
            d_val = df * gel

            def half_bwd(d, taps, shifts, dcw_ref, dcb_ref, slab, w_ref):
                dup = (taps[2] * d + taps[1] * _shift_up(d, zeros8, 1) + taps[0] * _shift_up(d, zeros8, 2))[:tm]
                dupb = dup.astype(BF16)
                dup_ref[slab, :, ls] = dupb
                dm = d[:tm]
                ext, s1, s2 = shifts
                dcw_ref[2:3, pl.ds(off, sc)] += _col_sum(ext[:tm] * dm)
                dcw_ref[1:2, pl.ds(off, sc)] += _col_sum(s1[:tm] * dm)
                dcw_ref[0:1, pl.ds(off, sc)] += _col_sum(s2[:tm] * dm)
                dcb_ref[:, pl.ds(off, sc)] += _col_sum(dm)
                return lax.dot_general(dupb, w_ref[:, ls], _DN_NT, preferred_element_type=F32)

            d = (half_bwd(d_gate, tg, gsh, dcwg_ref, dcbg_ref, 0, wgate_ref)
                 + half_bwd(d_val, tv, vsh, dcwv_ref, dcbv_ref, 1, wval_ref))
            part = d if part is None else part + d
        acc[...] += part

        @pl.when(k == nk - 1)
        def _():
            dh = acc[...]
            xv = x_ref[...]
            r = lax.rsqrt(_rows_mean(xv * xv) + RMS_EPS)
            xn = xv * r
            dg_ref[...] += _col_sum(dh * xn)
            dyg = dh * ng_ref[...]
            dx = r_ref[...] + r * (dyg - xn * _rows_mean(dyg * xn))
            dx_ref[...] = dx
            dxb_ref[...] = dx.astype(BF16)

    row = lambda i, k: (i, 0)
    prev = lambda i, k: (jnp.maximum(i * hb - 1, 0), k)
    nxt = lambda i, k: (jnp.minimum((i + 1) * hb, S // 16 - 1), k)
    main = lambda s: pl.BlockSpec((None, tm, tc), lambda i, k: (s, i, k))
    halo = lambda s, f: pl.BlockSpec((None, 16, tc), lambda i, k: (s,) + f(i, k))
    full3 = pl.BlockSpec((3, Fd), lambda i, k: (0, 0))
    full1 = pl.BlockSpec((1, Fd), lambda i, k: (0, 0))
    once = pl.Buffered(1)
    return _pallas(
        body, 17, deps, name=name, grid=(ni, nk),
        in_specs=[main(0), main(1), halo(0, prev), halo(1, prev), halo(0, nxt), halo(1, nxt),
                  pl.BlockSpec((tm, tc), lambda i, k: (i, k)), pl.BlockSpec((16, tc), nxt),
                  pl.BlockSpec((3, tc), lambda i, k: (0, k)), pl.BlockSpec((3, tc), lambda i, k: (0, k + nk)),
                  pl.BlockSpec((1, tc), lambda i, k: (0, k)), pl.BlockSpec((1, tc), lambda i, k: (0, k + nk)),
                  pl.BlockSpec((None, D, tc), lambda i, k: (k // nsubw, 0, k % nsubw)),
                  pl.BlockSpec((None, D, tc), lambda i, k: (half + k // nsubw, 0, k % nsubw)),
                  pl.BlockSpec((tm, D), row, pipeline_mode=once), pl.BlockSpec((tm, D), row, pipeline_mode=once),
                  pl.BlockSpec((1, D), lambda i, k: (0, 0))],
        out_specs=[pl.BlockSpec((2, tm, tc), lambda i, k: (0, i, k)), pl.BlockSpec((tm, D), row),
                   pl.BlockSpec((tm, D), row), pl.BlockSpec((1, D), lambda i, k: (0, 0)), full3, full3, full1, full1],
        out_shape=[jax.ShapeDtypeStruct((2, S, Fd), BF16), jax.ShapeDtypeStruct((S, D), F32),
                   jax.ShapeDtypeStruct((S, D), BF16), jax.ShapeDtypeStruct((1, D), F32),
                   jax.ShapeDtypeStruct((3, Fd), F32), jax.ShapeDtypeStruct((3, Fd), F32),
                   jax.ShapeDtypeStruct((1, Fd), F32), jax.ShapeDtypeStruct((1, Fd), F32)],
        scratch_shapes=[pltpu.VMEM((tm, D), F32)],
        compiler_params=_cp(2, 60))(up3, up3, up3, up3, up3, up3, d_f, d_f, cw, cw, cb, cb, wg, wg, resid, xin, g, *deps)


def _gm_forward_tile(pv, vg, vb, ws_ref, bsb_ref, mbuf, H, nc):
    W = H * CHUNK
    z, dz = _gelu_parts(pv)
    u, v0 = z[:, :W], z[:, W:]
    xc = v0 - _rows_mean(v0)
    rs = lax.rsqrt(_rows_mean(xc * xc) + LN_EPS)
    vh = xc * rs
    vnb = (vh * vg + vb).astype(BF16)
    mask = lax.broadcasted_iota(jnp.int32, (CHUNK, CHUNK), 0) >= lax.broadcasted_iota(jnp.int32, (CHUNK, CHUNK), 1)
    for h in range(H):
        cs = slice(h * CHUNK, (h + 1) * CHUNK)
        wm = jnp.where(mask, ws_ref[h], 0.0).astype(BF16)
        vcat = jnp.concatenate([vnb[c * CHUNK:(c + 1) * CHUNK, cs] for c in range(nc)], axis=1)
        mix = jnp.dot(wm, vcat, preferred_element_type=F32)
        for c in range(nc):
            mbuf[c * CHUNK:(c + 1) * CHUNK, cs] = mix[:, c * CHUNK:(c + 1) * CHUNK] + bsb_ref[h]
    return dz, u, vh, rs, vnb, mask


def _gm_fwd_call(p, v_g, v_b, ws, bsb, out_g, name, deps=()):
    S = p.shape[0]
    H = ws.shape[0]
    W = H * CHUNK
    tm = _tile(S, 256, CHUNK)
    nc = tm // CHUNK

    def body(p_ref, vg_ref, vb_ref, ws_ref, bsb_ref, og_ref, y_ref, mbuf):
        _, u, _, _, _, _ = _gm_forward_tile(p_ref[...], vg_ref[...], vb_ref[...], ws_ref, bsb_ref, mbuf, H, nc)
        yg = u * mbuf[...]
        r = lax.rsqrt(_rows_mean(yg * yg) + RMS_EPS)
        y_ref[...] = (yg * r * og_ref[...]).astype(BF16)

    vec = pl.BlockSpec((1, W), lambda i: (0, 0))
    mat = pl.BlockSpec((H, CHUNK, CHUNK), lambda i: (0, 0, 0))
    return _pallas(
        body, 6, deps, name=name, grid=(S // tm,),
        in_specs=[pl.BlockSpec((tm, 2 * W), lambda i: (i, 0)), vec, vec, mat, mat, vec],
        out_specs=pl.BlockSpec((tm, W), lambda i: (i, 0)),
        out_shape=jax.ShapeDtypeStruct((S, W), BF16),
        scratch_shapes=[pltpu.VMEM((tm, W), F32)], compiler_params=_cp(1))(p, v_g, v_b, ws, bsb, out_g, *deps)


def _gm_bwd_call(p, d_y, v_g, v_b, ws, bsb, out_g, name, deps=()):
    S = p.shape[0]
    H = ws.shape[0]
    W = H * CHUNK
    tm = _tile(S, 256, CHUNK)
    nc = tm // CHUNK
    ni = S // tm

    def body(p_ref, dy_ref, vg_ref, vb_ref, ws_ref, bsb_ref, og_ref,
             dp_ref, dvg_ref, dvb_ref, dws_ref, dbs_ref, dog_ref, mbuf, dvbuf):
        i = pl.program_id(0)

        @pl.when(i == 0)
        def _():
            for r in (dvg_ref, dvb_ref, dws_ref, dbs_ref, dog_ref):
                r[...] = jnp.zeros_like(r)

        vg = vg_ref[...]
        dz, u, vh, rs, vnb, mask = _gm_forward_tile(p_ref[...], vg, vb_ref[...], ws_ref, bsb_ref, mbuf, H, nc)
        mixed = mbuf[...]
        yg = u * mixed
        r = lax.rsqrt(_rows_mean(yg * yg) + RMS_EPS)
        yn = yg * r
        dya = dy_ref[...]
        dog_ref[...] += _col_sum(dya * yn)
        dyg = dya * og_ref[...]
        dygm = r * (dyg - yn * _rows_mean(dyg * yn))
        du = dygm * mixed
        dmix = dygm * u
        dmb = dmix.astype(BF16)
        for h in range(H):
            cs = slice(h * CHUNK, (h + 1) * CHUNK)
            wm = jnp.where(mask, ws_ref[h], 0.0).astype(BF16)
            dcat = jnp.concatenate([dmb[c * CHUNK:(c + 1) * CHUNK, cs] for c in range(nc)], axis=1)
            vcat = jnp.concatenate([vnb[c * CHUNK:(c + 1) * CHUNK, cs] for c in range(nc)], axis=1)
            dvn = lax.dot_general(wm, dcat, _DN_TN, preferred_element_type=F32)
            dws_ref[h] += jnp.where(mask, lax.dot_general(dcat, vcat, _DN_NT, preferred_element_type=F32), 0.0)
            dbs = dmix[0:CHUNK, cs]
            for c in range(1, nc):
                dbs = dbs + dmix[c * CHUNK:(c + 1) * CHUNK, cs]
            dbs_ref[h] += dbs
            for c in range(nc):
                dvbuf[c * CHUNK:(c + 1) * CHUNK, cs] = dvn[:, c * CHUNK:(c + 1) * CHUNK]
        dvn_all = dvbuf[...]
        dvg_ref[...] += _col_sum(dvn_all * vh)
        dvb_ref[...] += _col_sum(dvn_all)
        dvh = dvn_all * vg
        dv0 = rs * (dvh - _rows_mean(dvh) - vh * _rows_mean(dvh * vh))
        dp_ref[...] = (jnp.concatenate([du, dv0], axis=1) * dz).astype(BF16)

        @pl.when(i == ni - 1)
        def _():
            for h in range(H):
                dbs_ref[h] = jnp.broadcast_to(jnp.sum(dbs_ref[h], axis=1, keepdims=True), (CHUNK, CHUNK))

    vec = pl.BlockSpec((1, W), lambda i: (0, 0))
    mat = pl.BlockSpec((H, CHUNK, CHUNK), lambda i: (0, 0, 0))
    vshape = jax.ShapeDtypeStruct((1, W), F32)
    mshape = jax.ShapeDtypeStruct((H, CHUNK, CHUNK), F32)
    return _pallas(
        body, 7, deps, name=name, grid=(ni,),
        in_specs=[pl.BlockSpec((tm, 2 * W), lambda i: (i, 0)), pl.BlockSpec((tm, W), lambda i: (i, 0)),
                  vec, vec, mat, mat, vec],
        out_specs=[pl.BlockSpec((tm, 2 * W), lambda i: (i, 0)), vec, vec, mat, mat, vec],
        out_shape=[jax.ShapeDtypeStruct((S, 2 * W), BF16), vshape, vshape, mshape, mshape, vshape],
        scratch_shapes=[pltpu.VMEM((tm, W), F32), pltpu.VMEM((tm, W), F32)],
        compiler_params=_cp(1))(p, d_y, v_g, v_b, ws, bsb, out_g, *deps)


def _lru_gates(prev8, xl, cw, cb, wa_ref, ba, wx_ref, bx, lam, H):
    sh = [_shift_down(prev8, xl, k) for k in range(4)]
    xr = cw[3] * sh[0] + cw[2] * sh[1] + cw[1] * sh[2] + cw[0] * sh[3] + cb
    xrb = xr.astype(BF16)
    rp, ip = [], []
    for h in range(H):
        cs = slice(h * CHUNK, (h + 1) * CHUNK)
        rp.append(jnp.dot(xrb[:, cs], wa_ref[h].astype(BF16), preferred_element_type=F32))
        ip.append(jnp.dot(xrb[:, cs], wx_ref[h].astype(BF16), preferred_element_type=F32))
    r = _sigmoid(jnp.concatenate(rp, axis=1) + ba)
    ig = _sigmoid(jnp.concatenate(ip, axis=1) + bx)
    sp = _softplus(-lam)
    la = (-LRU_C) * r * sp
    a = jnp.exp(la)
    mult = jnp.sqrt(jnp.maximum(_neg_expm1(2.0 * la), 0.0))
    return xr, xrb, r, ig, sp, a, mult, sh


def _lru_fwd_call(p, cw, cb, wa, ba, wx, bx, lam, out_g, name):
    S = p.shape[0]
    H = wa.shape[0]
    W = H * CHUNK
    tm = _tile(S, 256, 16)
    ng = tm // 8

    def body(pg_ref, px_ref, cw_ref, cb_ref, wa_ref, ba_ref, wx_ref, bx_ref, lam_ref, og_ref,
             y_ref, h_ref, xprev, hcar, abuf, bbuf):
        @pl.when(pl.program_id(0) == 0)
        def _():
            xprev[...] = jnp.zeros_like(xprev)
            hcar[...] = jnp.zeros_like(hcar)

        xl = px_ref[...]
        xr, _, _, ig, _, a, mult, _ = _lru_gates(xprev[...], xl, _taps(cw_ref), cb_ref[...], wa_ref, ba_ref[...],
                                                 wx_ref, bx_ref[...], lam_ref[...], H)
        xprev[...] = xl[tm - 8:]
        b = mult * (ig * xr)
        sub = lax.broadcasted_iota(jnp.int32, (tm, W), 0) & 7
        for d in (1, 2, 4):
            m = sub >= d
            a_s = jnp.where(m, pltpu.roll(a, d, 0), 1.0)
            b_s = jnp.where(m, pltpu.roll(b, d, 0), 0.0)
            b = a * b_s + b
            a = a * a_s
        abuf[...] = a
        bbuf[...] = b

        def step(g, carry):
            r0 = pl.multiple_of(g * 8, 8)
            h_ref[pl.ds(r0, 8), :] = abuf[pl.ds(r0, 8), :] * carry + bbuf[pl.ds(r0, 8), :]
            return jnp.broadcast_to(h_ref[pl.ds(r0 + 7, 1), :], (8, W))

        hcar[...] = lax.fori_loop(0, ng, step, hcar[...])
        yl = h_ref[...] * _gelu(pg_ref[...])
        r = lax.rsqrt(_rows_mean(yl * yl) + RMS_EPS)
        y_ref[...] = (yl * r * og_ref[...]).astype(BF16)

    vec = pl.BlockSpec((1, W), lambda i: (0, 0))
    mat = pl.BlockSpec((H, CHUNK, CHUNK), lambda i: (0, 0, 0))
    return pl.pallas_call(
        body, name=name, grid=(S // tm,),
        in_specs=[pl.BlockSpec((tm, W), lambda i: (i, 2)), pl.BlockSpec((tm, W), lambda i: (i, 3)),
                  pl.BlockSpec((4, W), lambda i: (0, 0)), vec, mat, vec, mat, vec, vec, vec],
        out_specs=[pl.BlockSpec((tm, W), lambda i: (i, 0)), pl.BlockSpec((tm, W), lambda i: (i, 0))],
        out_shape=[jax.ShapeDtypeStruct((S, W), BF16), jax.ShapeDtypeStruct((S, W), F32)],
        scratch_shapes=[pltpu.VMEM((8, W), F32), pltpu.VMEM((8, W), F32), pltpu.VMEM((tm, W), F32),
                        pltpu.VMEM((tm, W), F32)],
        compiler_params=_cp(1))(p, p, cw, cb, wa, ba, wx, bx, lam, out_g)


def _lru_bwd_call(p, hs, d_y, cw, cb, wa, ba, wx, bx, lam, out_g, name):
    S = p.shape[0]
    H = wa.shape[0]
    W = H * CHUNK
    tm = _tile(S, 256, 16)
    ng = tm // 8
    ni = S // tm
    hb = tm // 8

    def body(pg_ref, px_ref, pxp_ref, h_ref, hp_ref, dy_ref, cw_ref, cb_ref, wa_ref, ba_ref, wx_ref, bx_ref,
             lam_ref, og_ref,
             dp_ref, dcw_ref, dcb_ref, dwa_ref, dba_ref, dwx_ref, dbx_ref, dlam_ref, dog_ref,
             a_next, e_next, dxr_next, abuf, bbuf, ebuf):
        i = pl.program_id(0)
        ri = ni - 1 - i

        @pl.when(i == 0)
        def _():
            for r in (dcw_ref, dcb_ref, dwa_ref, dba_ref, dwx_ref, dbx_ref, dlam_ref, dog_ref,
                      a_next, e_next, dxr_next):
                r[...] = jnp.zeros_like(r)

        keep_prev = jnp.where(ri == 0, 0.0, 1.0)
        cw_ = _taps(cw_ref)
        lam_ = lam_ref[...]
        xl = px_ref[...]
        xr, xrb, r, ig, sp, a, mult, sh = _lru_gates(pxp_ref[...] * keep_prev, xl, cw_, cb_ref[...], wa_ref,
                                                     ba_ref[...], wx_ref, bx_ref[...], lam_, H)
        gg, dgg = _gelu_parts(pg_ref[...])
        hv = h_ref[...]
        yl = hv * gg
        rr = lax.rsqrt(_rows_mean(yl * yl) + RMS_EPS)
        yn = yl * rr
        dyb = dy_ref[...]
        dog_ref[...] += _col_sum(dyb * yn)
        dyg = dyb * og_ref[...]
        dyl = rr * (dyg - yn * _rows_mean(dyg * yn))
        dh = dyl * gg
        dgl = dyl * hv * dgg

        an = _shift_up(a, a_next[...], 1)
        eb = dh
        sub = lax.broadcasted_iota(jnp.int32, (tm, W), 0) & 7
        for d in (1, 2, 4):
            m = sub < 8 - d
            a_s = jnp.where(m, pltpu.roll(an, tm - d, 0), 1.0)
            e_s = jnp.where(m, pltpu.roll(eb, tm - d, 0), 0.0)
            eb = an * e_s + eb
            an = an * a_s
        abuf[...] = an
        bbuf[...] = eb

        def step(g, carry):
            r0 = pl.multiple_of((ng - 1 - g) * 8, 8)
            ebuf[pl.ds(r0, 8), :] = abuf[pl.ds(r0, 8), :] * carry + bbuf[pl.ds(r0, 8), :]
            return jnp.broadcast_to(ebuf[pl.ds(r0, 1), :], (8, W))

        lax.fori_loop(0, ng, step, jnp.broadcast_to(e_next[0:1, :], (8, W)))
        e = ebuf[...]
        a_next[...] = a[0:8]
        e_next[...] = e[0:8]

        hm1 = _shift_down(hp_ref[...] * keep_prev, hv, 1)
        da = e * hm1
        dmult = e * ig * xr
        di = e * mult * xr
        dxr = e * mult * ig
        dla = da * a - dmult * (a * a) / mult
        dr = dla * ((-LRU_C) * sp)
        dlam_ref[...] += _col_sum(dla * ((-LRU_C) * r))
        dpr = dr * r * (1.0 - r)
        dpi = di * ig * (1.0 - ig)
        dba_ref[...] += _col_sum(dpr)
        dbx_ref[...] += _col_sum(dpi)
        dprb = dpr.astype(BF16)
        dpib = dpi.astype(BF16)
        back = []
        for h in range(H):
            cs = slice(h * CHUNK, (h + 1) * CHUNK)
            wab = wa_ref[h].astype(BF16)
            wxb = wx_ref[h].astype(BF16)
            back.append(lax.dot_general(dprb[:, cs], wab, _DN_NT, preferred_element_type=F32)
                        + lax.dot_general(dpib[:, cs], wxb, _DN_NT, preferred_element_type=F32))
            dwa_ref[h] += lax.dot_general(xrb[:, cs], dprb[:, cs], _DN_TN, preferred_element_type=F32)
            dwx_ref[h] += lax.dot_general(xrb[:, cs], dpib[:, cs], _DN_TN, preferred_element_type=F32)
        dxr = dxr + jnp.concatenate(back, axis=1)

        nxt = dxr_next[...]
        dxl = (cw_[3] * dxr + cw_[2] * _shift_up(dxr, nxt, 1) + cw_[1] * _shift_up(dxr, nxt, 2)
               + cw_[0] * _shift_up(dxr, nxt, 3))
        dxr_next[...] = dxr[0:8]
        for k in range(4):
            dcw_ref[k:k + 1, :] += _col_sum(sh[3 - k] * dxr)
        dcb_ref[...] += _col_sum(dxr)
        dp_ref[...] = jnp.concatenate([dgl, dxl], axis=1).astype(BF16)

        @pl.when(i == ni - 1)
        def _():
            dlam_ref[...] = -dlam_ref[...] * _sigmoid(-lam_)

    vec = pl.BlockSpec((1, W), lambda i: (0, 0))
    mat = pl.BlockSpec((H, CHUNK, CHUNK), lambda i: (0, 0, 0))
    rev = lambda i: ni - 1 - i
    prev = lambda i: jnp.maximum(rev(i) * hb - 1, 0)
    vshape = jax.ShapeDtypeStruct((1, W), F32)
    mshape = jax.ShapeDtypeStruct((H, CHUNK, CHUNK), F32)
    tile = lambda: pltpu.VMEM((tm, W), F32)
    car = lambda: pltpu.VMEM((8, W), F32)
    return pl.pallas_call(
        body, name=name, grid=(ni,),
        in_specs=[pl.BlockSpec((tm, W), lambda i: (rev(i), 2)), pl.BlockSpec((tm, W), lambda i: (rev(i), 3)),
                  pl.BlockSpec((8, W), lambda i: (prev(i), 3)),
                  pl.BlockSpec((tm, W), lambda i: (rev(i), 0)), pl.BlockSpec((8, W), lambda i: (prev(i), 0)),
                  pl.BlockSpec((tm, W), lambda i: (rev(i), 1)),
                  pl.BlockSpec((4, W), lambda i: (0, 0)), vec, mat, vec, mat, vec, vec, vec],
        out_specs=[pl.BlockSpec((tm, 2 * W), lambda i: (rev(i), 0)), pl.BlockSpec((4, W), lambda i: (0, 0)), vec,
                   mat, vec, mat, vec, vec, vec],
        out_shape=[jax.ShapeDtypeStruct((S, 2 * W), BF16), jax.ShapeDtypeStruct((4, W), F32), vshape,
                   mshape, vshape, mshape, vshape, vshape, vshape],
        scratch_shapes=[car(), car(), car(), tile(), tile(), tile()],
        compiler_params=_cp(1, 56))(p, p, p, hs, hs, d_y, cw, cb, wa, ba, wx, bx, lam, out_g)


def _rows128(a):
    return a.reshape(-1, LANES).astype(F32)


def _pack(arrays, pad_to=256):
    flat = jnp.concatenate([_rows128(a) for a in arrays], axis=0)
    pad = (-flat.shape[0]) % pad_to
    if pad:
        flat = jnp.concatenate([flat, jnp.zeros((pad, LANES), F32)], axis=0)
    return flat


def _unpack(flat, shapes):
    out, r = [], 0
    for s in shapes:
        n = 1
        for d in s:
            n *= d
        out.append(flat[r:r + n // LANES].reshape(s))
        r += n // LANES
    return out


def kernel(x, norm1_g, w_in, gm_v_g, gm_v_b, gm_ws, gm_bs, lru_conv_w, lru_conv_b, lru_wa, lru_ba, lru_wx, lru_bx, lru_lambda, gm_out_g, lru_out_g, w_out, norm2_g, ffn_w_up, ffn_conv_w, ffn_conv_b, ffn_w_down, final_g, loss_target, m_norm1_g, m_w_in, m_gm_v_g, m_gm_v_b, m_gm_ws, m_gm_bs, m_lru_conv_w, m_lru_conv_b, m_lru_wa, m_lru_ba, m_lru_wx, m_lru_bx, m_lru_lambda, m_gm_out_g, m_lru_out_g, m_w_out, m_norm2_g, m_ffn_w_up, m_ffn_conv_w, m_ffn_conv_b, m_ffn_w_down, m_final_g, v_norm1_g, v_w_in, v_gm_v_g, v_gm_v_b, v_gm_ws, v_gm_bs, v_lru_conv_w, v_lru_conv_b, v_lru_wa, v_lru_ba, v_lru_wx, v_lru_bx, v_lru_lambda, v_gm_out_g, v_lru_out_g, v_w_out, v_norm2_g, v_ffn_w_up, v_ffn_conv_w, v_ffn_conv_b, v_ffn_w_down, v_final_g):
    wts = dict(norm1_g=norm1_g, w_in=w_in, gm_v_g=gm_v_g, gm_v_b=gm_v_b, gm_ws=gm_ws, gm_bs=gm_bs,
               lru_conv_w=lru_conv_w, lru_conv_b=lru_conv_b, lru_wa=lru_wa, lru_ba=lru_ba, lru_wx=lru_wx,
               lru_bx=lru_bx, lru_lambda=lru_lambda, gm_out_g=gm_out_g, lru_out_g=lru_out_g, w_out=w_out,
               norm2_g=norm2_g, ffn_w_up=ffn_w_up, ffn_conv_w=ffn_conv_w, ffn_conv_b=ffn_conv_b,
               ffn_w_down=ffn_w_down, final_g=final_g)
    mom = dict(norm1_g=m_norm1_g, w_in=m_w_in, gm_v_g=m_gm_v_g, gm_v_b=m_gm_v_b, gm_ws=m_gm_ws, gm_bs=m_gm_bs,
               lru_conv_w=m_lru_conv_w, lru_conv_b=m_lru_conv_b, lru_wa=m_lru_wa, lru_ba=m_lru_ba, lru_wx=m_lru_wx,
               lru_bx=m_lru_bx, lru_lambda=m_lru_lambda, gm_out_g=m_gm_out_g, lru_out_g=m_lru_out_g, w_out=m_w_out,
               norm2_g=m_norm2_g, ffn_w_up=m_ffn_w_up, ffn_conv_w=m_ffn_conv_w, ffn_conv_b=m_ffn_conv_b,
               ffn_w_down=m_ffn_w_down, final_g=m_final_g)
    var = dict(norm1_g=v_norm1_g, w_in=v_w_in, gm_v_g=v_gm_v_g, gm_v_b=v_gm_v_b, gm_ws=v_gm_ws, gm_bs=v_gm_bs,
               lru_conv_w=v_lru_conv_w, lru_conv_b=v_lru_conv_b, lru_wa=v_lru_wa, lru_ba=v_lru_ba, lru_wx=v_lru_wx,
               lru_bx=v_lru_bx, lru_lambda=v_lru_lambda, gm_out_g=v_gm_out_g, lru_out_g=v_lru_out_g, w_out=v_w_out,
               norm2_g=v_norm2_g, ffn_w_up=v_ffn_w_up, ffn_conv_w=v_ffn_conv_w, ffn_conv_b=v_ffn_conv_b,
               ffn_w_down=v_ffn_w_down, final_g=v_final_g)

    xi, yi, ci = lax.axis_index("x"), lax.axis_index("y"), lax.axis_index("c")
    chip = 2 * xi + yi
    dev = 2 * chip + ci
    core_chip = jnp.stack([ci, chip]).astype(jnp.int32)

    xs = x[0]
    tgt = loss_target[0]
    S, D = xs.shape
    H = gm_ws.shape[1]
    W = H * CHUNK
    Fd = ffn_w_down.shape[1] * N_DEV
    lcw_cols = lru_conv_w.shape[2]
    fcw_cols = ffn_conv_w.shape[2]

    dev1 = jnp.reshape(dev, (1,)).astype(jnp.int32)

    def gather_start(shards, name, after=()):
        lands = [_place_own_call(s, dev1, "%s_own%d" % (name, k)) for k, s in enumerate(shards)]
        return _exchange_start(shards, lands, 4 * len(shards), _gather_stage1_copies(len(shards)), name + "_ici", after)

    def gather_forward(lands, name, after=()):
        return _exchange_start([], lands, 3 * len(lands), _gather_stage2_copies(len(lands)), name + "_d2d", after)

    def pair_start(g, name, after=()):
        return _exchange_start([g], [lax.empty((4,) + g.shape[1:], F32)], 4, _pair_copies(1), name, after)

    def chip_start(p16, name, after=()):
        return _exchange_start([p16], [lax.empty((3,) + p16.shape[1:], BF16)], 3, _chip_copies(1), name, after)

    vgm_g, vgm_b = gm_v_g, gm_v_b
    ws, wa, wx = gm_ws[0], lru_wa[0], lru_wx[0]
    bsb = jnp.broadcast_to(gm_bs[0][:, :, None], (H, CHUNK, CHUNK))
    ba, bx = lru_ba.reshape(1, W), lru_bx.reshape(1, W)
    fcb = ffn_conv_b
    fing = final_g.reshape(1, D)

    conv_pack = _pack([lru_conv_w[0], ffn_conv_w[0]], pad_to=8)
    ga1 = gather_start([w_in[0].astype(BF16), conv_pack], "gather_in")
    h1 = _rmsnorm_call(xs, norm1_g, "norm1", deps=(ga1.token,))
    _, la = _exchange_wait(ga1, after=(h1,))
    ga2 = gather_forward(la, "gather_in")
    gb1 = gather_start([w_out[0].astype(BF16)], "gather_out", after=(ga2.token,))
    _, (win_g, conv_g) = _exchange_wait(ga2, after=(gb1.token,))
    n_l = 4 * lcw_cols // LANES
    n_f = 3 * fcw_cols // LANES
    lcw = conv_g[:, :n_l].reshape(N_DEV, 4, lcw_cols).transpose(1, 0, 2).reshape(4, N_DEV * lcw_cols)
    fcw = conv_g[:, n_l:n_l + n_f].reshape(N_DEV, 3, fcw_cols).transpose(1, 0, 2).reshape(3, N_DEV * fcw_cols)

    p = _mm_blocked_call(h1, win_g, F32, False, "in_proj")
    win_rows = _unblock_call(win_g, "w_in_rows")
    _, lb = _exchange_wait(gb1, after=(p,))
    gb2 = gather_forward(lb, "gather_out")
    gc1 = gather_start([ffn_w_up[0].astype(BF16)], "gather_up", after=(gb2.token,))
    ya = _gm_fwd_call(p, vgm_g, vgm_b, ws, bsb, gm_out_g, "gmlp_fwd", deps=(gc1.token,))
    yb, hs = _lru_fwd_call(p, lcw, lru_conv_b, wa, ba, wx, bx, lru_lambda, lru_out_g, "lru_fwd")
    y = jnp.concatenate([ya, yb], axis=1)
    _, (wout_g,) = _exchange_wait(gb2, after=(y,))
    wout_full = wout_g.reshape(D, D)
    x2 = _mm_out_call(xs, y, wout_full, "out_proj")
    h2 = _rmsnorm_call(x2, norm2_g, "norm2")
    _, lc = _exchange_wait(gc1, after=(h2,))
    gc2 = gather_forward(lc, "gather_up")
    gd1 = gather_start([ffn_w_down[0].astype(BF16)], "gather_down", after=(gc2.token,))
    _, (wup_g,) = _exchange_wait(gc2, after=(gd1.token,))
    up3 = _mm_blocked_call(h2, wup_g, BF16, True, "ffn_up")
    f, upc3 = _ffn_act_call(up3, fcw, fcb, "ffn_act")
    _, ld = _exchange_wait(gd1, after=(f,))
    gd2 = gather_forward(ld, "gather_down")
    _, (wdown_g,) = _exchange_wait(gd2)
    wdown_full = wdown_g.reshape(Fd, D)
    dx3, dx3b, loss_acc, d_final = _mm_down_loss_call(x2, f, wdown_full, fing, tgt, "ffn_down_loss")

    g_wdown = _mm_tn_rows_call(f, dx3b, "ffn_down_dw").reshape((N_DEV,) + ffn_w_down.shape[1:])
    pd = pair_start(g_wdown, "pair_down")
    d_f = _mm_nt_call(dx3b, wdown_full, BF16, "ffn_down_dx", deps=(pd.token,))
    d_up3, dfcw_g, dfcw_v, dfcb_g, dfcb_v = _ffn_act_bwd_call(up3, upc3, d_f, fcw, "ffn_act_bwd")
    (g_wdown,), (r1,) = _exchange_wait(pd, after=(d_up3,))
    own_down, p16 = _pair_add_call(g_wdown, r1, core_chip, "pair_add_down")
    cd = chip_start(p16, "chip_down")
    g_wup = _mm_tn_cols_call(h2, d_up3, N_DEV, ffn_w_up.shape[2], "ffn_up_dw", deps=(cd.token,))
    pu = pair_start(g_wup, "pair_up")
    dx2, dx2b, d_norm2 = _mm_dx_norm_call(d_up3, wup_g, dx3, x2, norm2_g, "ffn_up_dx", deps=(pu.token,))
    g_wout = _mm_tn_rows_call(y, dx2b, "out_proj_dw").reshape((N_DEV,) + w_out.shape[1:])
    po = pair_start(g_wout, "pair_out")
    d_y = _mm_nt_call(dx2b, wout_full, F32, "out_proj_dx", deps=(po.token,))
    (g_wup,), (r1,) = _exchange_wait(pu, after=(d_y,))
    own_up, p16 = _pair_add_call(g_wup, r1, core_chip, "pair_add_up")
    _, (r2_down,) = _exchange_wait(cd, after=(p16,))
    cu = chip_start(p16, "chip_up", after=(r2_down,))
    dp_gm, d_vg, d_vb, d_ws, d_bs, d_gog = _gm_bwd_call(p, d_y, vgm_g, vgm_b, ws, bsb, gm_out_g, "gmlp_bwd",
                                                        deps=(cu.token,))
    dp_lru, d_lcw, d_lcb, d_wa, d_ba, d_wx, d_bx, d_lam, d_log = _lru_bwd_call(
        p, hs, d_y, lcw, lru_conv_b, wa, ba, wx, bx, lru_lambda, lru_out_g, "lru_bwd")
    d_p = jnp.concatenate([dp_gm, dp_lru], axis=1)[None]
    (g_wout,), (r1,) = _exchange_wait(po, after=(d_p,))
    own_out, p16_out = _pair_add_call(g_wout, r1, core_chip, "pair_add_out")
    g_win = _mm_tn_cols_call(h1, d_p, N_DEV, w_in.shape[2], "in_proj_dw")
    pi = pair_start(g_win, "pair_in")
    _, (r2_up,) = _exchange_wait(cu, after=(g_win,))
    co = chip_start(p16_out, "chip_out", after=(r2_up,))
    grad_x, d_norm1 = _mm_nt_norm_call(d_p[0], win_rows, dx2, xs, norm1_g, "in_proj_dx", deps=(co.token, pi.token))
    (g_win,), (r1,) = _exchange_wait(pi, after=(grad_x,))
    own_in, p16 = _pair_add_call(g_win, r1, core_chip, "pair_add_in")
    _, (r2_out,) = _exchange_wait(co, after=(p16,))
    ci_ = chip_start(p16, "chip_in", after=(r2_out,))

    def adamw_big(n, own, r2, deps=()):
        return _adamw_call(wts[n][0], mom[n][0], var[n][0], [(own, None), (r2, 0), (r2, 1), (r2, 2)], "adamw_" + n, deps)

    res = {}
    res["ffn_w_down"] = adamw_big("ffn_w_down", own_down, r2_down, (ci_.token,))
    res["ffn_w_up"] = adamw_big("ffn_w_up", own_up, r2_up, (ci_.token,))
    res["w_out"] = adamw_big("w_out", own_out, r2_out, (ci_.token,))
    _, (r2_in,) = _exchange_wait(ci_, after=(res["w_out"][0], res["ffn_w_up"][0], res["ffn_w_down"][0]))
    res["w_in"] = adamw_big("w_in", own_in, r2_in)

    small_g = dict(norm1_g=d_norm1, gm_v_g=d_vg, gm_v_b=d_vb, gm_ws=d_ws, gm_bs=d_bs[:, :, 0], lru_conv_b=d_lcb,
                   lru_wa=d_wa, lru_ba=d_ba, lru_wx=d_wx, lru_bx=d_bx, lru_lambda=d_lam, gm_out_g=d_gog,
                   lru_out_g=d_log, norm2_g=d_norm2,
                   ffn_conv_b=jnp.concatenate([dfcb_g, dfcb_v], axis=1), final_g=d_final)
    rep = _pack([small_g[n] for n in SMALL])
    conv_part = _pack([d_lcw, jnp.concatenate([dfcw_g, dfcw_v], axis=1)], pad_to=8)
    n_rep, n_conv = rep.shape[0], conv_part.shape[0]
    (parts,) = _all_gather_call([jnp.concatenate([rep, conv_part], axis=0)], "gather_small_grads")
    g_rep, d_rep, m_rep, v_rep = _adamw_call(
        _pack([wts[n] for n in SMALL]), _pack([mom[n] for n in SMALL]), _pack([var[n] for n in SMALL]),
        [(parts, k) for k in range(N_DEV)], "adamw_small")
    shapes = [wts[n].shape for n in SMALL]
    for n, g_, d_, m_, v_ in zip(SMALL, _unpack(g_rep, shapes), _unpack(d_rep, shapes), _unpack(m_rep, shapes),
                                 _unpack(v_rep, shapes)):
        res[n] = (g_, d_, m_, v_)
    conv_sum = _sum_call(parts, n_rep, n_conv, "sum_conv_grads")
    g_lcw = conv_sum[:4 * W // LANES].reshape(4, W)
    g_fcw = conv_sum[4 * W // LANES:4 * W // LANES + 6 * Fd // LANES].reshape(3, 2 * Fd)
    for n, full in (("lru_conv_w", g_lcw), ("ffn_conv_w", g_fcw)):
        cols = wts[n].shape[2]
        mine = lax.dynamic_slice_in_dim(full, dev * cols, cols, axis=1)
        res[n] = _adamw_call(wts[n][0], mom[n][0], var[n][0], [(mine, None)], "adamw_" + n)

    loss = lax.psum(loss_acc[0, 0], ("x", "y", "c"))
    outs = [[], [], [], []]
    for n in WEIGHTS:
        for k in range(4):
            outs[k].append(res[n][k].reshape(wts[n].shape))
    return (loss, grad_x[None], *outs[0], *outs[1], *outs[2], *outs[3])
```

```python
import functools
import math

import jax
import jax.numpy as jnp
from jax import lax
from jax.experimental import pallas as pl
from jax.experimental.pallas import tpu as pltpu

F32 = jnp.float32
BF16 = jnp.bfloat16

RMS_EPS = 1e-6
LN_EPS = 1e-5
LRU_C = 8.0
CHUNK = 128
ADAM_LR = 0.001
ADAM_B1 = 0.9
ADAM_B2 = 0.999
ADAM_EPS = 1e-08
ADAM_WD = 0.01
ADAM_STEP = 10
N_DEV = 8
LANES = 128
MIB = 1024 * 1024

WEIGHTS = ['norm1_g', 'w_in', 'gm_v_g', 'gm_v_b', 'gm_ws', 'gm_bs', 'lru_conv_w', 'lru_conv_b', 'lru_wa', 'lru_ba',
           'lru_wx', 'lru_bx', 'lru_lambda', 'gm_out_g', 'lru_out_g', 'w_out', 'norm2_g', 'ffn_w_up', 'ffn_conv_w',
           'ffn_conv_b', 'ffn_w_down', 'final_g']
BIG = ['w_in', 'w_out', 'ffn_w_up', 'ffn_w_down']
CONV = ['lru_conv_w', 'ffn_conv_w']
SMALL = [n for n in WEIGHTS if n not in BIG and n not in CONV]

_DN_NT = (((1,), (1,)), ((), ()))
_DN_TN = (((0,), (0,)), ((), ()))
_GELU_C = 0.7978845608028654


def _cp(n_axes, vmem_mib=48):
    return pltpu.CompilerParams(dimension_semantics=("arbitrary",) * n_axes, vmem_limit_bytes=vmem_mib * MIB)


def _tile(n, pref, mult=8):
    t = min(pref, n)
    t -= t % mult
    while t >= mult:
        if n % t == 0:
            return t
        t -= mult
    return n


def _gelu(z):
    return 0.5 * z * (1.0 + jnp.tanh(_GELU_C * z * (1.0 + 0.044715 * z * z)))


def _gelu_parts(z):
    z2 = z * z
    t = jnp.tanh(_GELU_C * z * (1.0 + 0.044715 * z2))
    g = 0.5 * z * (1.0 + t)
    dg = 0.5 * (1.0 + t) + 0.5 * z * (1.0 - t * t) * (_GELU_C * (1.0 + 0.134145 * z2))
    return g, dg


def _sigmoid(z):
    return 0.5 + 0.5 * jnp.tanh(0.5 * z)


def _softplus(z):
    t = jnp.exp(-jnp.abs(z))
    u = 1.0 + t
    log1p = jnp.where(u == 1.0, t, jnp.log(u) * t / (u - 1.0))
    return jnp.maximum(z, 0.0) + log1p


def _neg_expm1(y):
    e = jnp.exp(y)
    small = jnp.where(e == 1.0, y, (e - 1.0) * y / jnp.log(e))
    return -jnp.where(y < -0.5, e - 1.0, small)


def _rows_mean(v):
    return jnp.mean(v, axis=-1, keepdims=True)


def _col_sum(v):
    return jnp.sum(v, axis=0, keepdims=True)


def _shift_down(prev8, cur, k):
    if k == 0:
        return cur
    z = jnp.concatenate([prev8, cur], axis=0)
    return pltpu.roll(z, k, 0)[8:]


def _shift_up(cur, next8, k):
    if k == 0:
        return cur
    n = cur.shape[0]
    z = jnp.concatenate([cur, next8], axis=0)
    return pltpu.roll(z, n + 8 - k, 0)[:n]


def _mesh_pos():
    return lax.axis_index("x"), lax.axis_index("y"), lax.axis_index("c")


def _any_specs(n):
    return [pl.BlockSpec(memory_space=pl.ANY)] * n


def _pallas(body, n_in, deps, **kw):
    nd = len(deps)
    if not nd:
        return pl.pallas_call(body, **kw)

    def ordered(*refs):
        body(*refs[:n_in], *refs[n_in + nd:])

    kw["in_specs"] = list(kw["in_specs"]) + _any_specs(nd)
    return pl.pallas_call(ordered, **kw)


_HBM = pl.BlockSpec(memory_space=pltpu.HBM)
_SEM = pl.BlockSpec(memory_space=pltpu.SEMAPHORE)
_EFFECT = pltpu.SideEffectType.DATAFLOW_SIDE_EFFECTING


class _InFlight:
    def __init__(self, sems, bufs, token, n_src, n_copies, make_copies, name):
        self.sems, self.bufs, self.token = sems, bufs, token
        self.n_src, self.n_copies, self.make_copies, self.name = n_src, n_copies, make_copies, name


def _exchange_start(srcs, lands, n_copies, make_copies, name, after=()):
    bufs = list(srcs) + list(lands)
    nb, na = len(bufs), len(after)
    ns = len(srcs)

    def body(*refs):
        b_refs = refs[:nb]
        outs = refs[nb + na:]
        send, recv = outs[:n_copies], outs[n_copies:2 * n_copies]
        token = outs[-1]
        for cp in make_copies(b_refs[:ns], b_refs[ns:], send, recv):
            cp.start()
        token[...] = jnp.zeros_like(token)

    out = pl.pallas_call(
        body, name=name,
        out_shape=[pltpu.SemaphoreType.DMA(())] * (2 * n_copies) + [pltpu.HBM(b.shape, b.dtype) for b in bufs]
        + [jax.ShapeDtypeStruct((8, LANES), F32)],
        in_specs=[_HBM] * nb + _any_specs(na),
        out_specs=[_SEM] * (2 * n_copies) + [_HBM] * nb + [pl.BlockSpec(memory_space=pltpu.VMEM)],
        input_output_aliases={i: 2 * n_copies + i for i in range(nb)},
        compiler_params=pltpu.CompilerParams(has_side_effects=_EFFECT),
    )(*[pltpu.with_memory_space_constraint(b, pltpu.HBM) for b in bufs], *after)
    return _InFlight(out[:2 * n_copies], out[2 * n_copies:2 * n_copies + nb], out[-1], ns, n_copies, make_copies, name)


def _exchange_wait(fl, after=()):
    nb, na, nc, ns = len(fl.bufs), len(after), fl.n_copies, fl.n_src

    def body(*refs):
        b_refs = refs[:nb]
        sems = refs[nb:nb + 2 * nc]
        copies = fl.make_copies(b_refs[:ns], b_refs[ns:], sems[:nc], sems[nc:])
        for cp in copies:
            cp.wait_send()
        for cp in copies:
            cp.wait_recv()

    out = pl.pallas_call(
        body, name=fl.name + "_wait",
        out_shape=[pltpu.HBM(b.shape, b.dtype) for b in fl.bufs],
        in_specs=[_HBM] * nb + [_SEM] * (2 * nc) + _any_specs(na),
        out_specs=[_HBM] * nb,
        input_output_aliases={i: i for i in range(nb)},
        compiler_params=pltpu.CompilerParams(has_side_effects=_EFFECT),
    )(*fl.bufs, *fl.sems, *after)
    return list(out[:ns]), list(out[ns:])


def _remote(src, dst, send_sem, recv_sem, to):
    return pltpu.make_async_remote_copy(src_ref=src, dst_ref=dst, send_sem=send_sem, recv_sem=recv_sem,
                                        device_id=to, device_id_type=pl.DeviceIdType.MESH)


def _gather_stage1_copies(n):
    def make(s_refs, l_refs, send, recv):
        x, y, c = _mesh_pos()
        own = 4 * x + 2 * y + c
        targets = [(x, y, 1 - c), (1 - x, y, c), (x, 1 - y, c), (1 - x, 1 - y, c)]
        return [_remote(s_refs[a], l_refs[a].at[own], send[4 * a + k], recv[4 * a + k], to)
                for a in range(n) for k, to in enumerate(targets)]
    return make


def _gather_stage2_copies(n):
    def make(s_refs, l_refs, send, recv):
        x, y, c = _mesh_pos()
        blocks = [4 * (1 - x) + 2 * y + c, 4 * x + 2 * (1 - y) + c, 4 * (1 - x) + 2 * (1 - y) + c]
        return [_remote(l_refs[a].at[b], l_refs[a].at[b], send[3 * a + j], recv[3 * a + j], (x, y, 1 - c))
                for a in range(n) for j, b in enumerate(blocks)]
    return make


def _pair_copies(n):
    def make(s_refs, l_refs, send, recv):
        x, y, c = _mesh_pos()
        return [_remote(s_refs[a].at[2 * k + 1 - c], l_refs[a].at[k], send[4 * a + k], recv[4 * a + k], (x, y, 1 - c))
                for a in range(n) for k in range(4)]
    return make


def _chip_copies(n):
    def make(s_refs, l_refs, send, recv):
        x, y, c = _mesh_pos()
        chips = [(1 - x, y), (x, 1 - y), (1 - x, 1 - y)]
        return [_remote(s_refs[a].at[2 * ch[0] + ch[1]], l_refs[a].at[j], send[3 * a + j], recv[3 * a + j], (*ch, c))
                for a in range(n) for j, ch in enumerate(chips)]
    return make


def _place_own_call(shard, dev, name):
    R, C = shard.shape
    tr = _tile(R, max(16, MIB // (C * shard.dtype.itemsize)), 16)

    def body(d_ref, s_ref, o_ref):
        o_ref[...] = s_ref[...]

    grid_spec = pltpu.PrefetchScalarGridSpec(
        num_scalar_prefetch=1, grid=(R // tr,),
        in_specs=[pl.BlockSpec((tr, C), lambda r, d: (r, 0))],
        out_specs=pl.BlockSpec((None, tr, C), lambda r, d: (d[0], r, 0)))
    return pl.pallas_call(body, name=name, grid_spec=grid_spec,
                          out_shape=jax.ShapeDtypeStruct((N_DEV, R, C), shard.dtype), compiler_params=_cp(1))(dev, shard)


def _all_gather_call(shards, name):
    n = len(shards)

    def body(*refs):
        x_refs, o_refs = refs[:n], refs[n:2 * n]
        send_sems, recv_sems, local_sems = refs[2 * n:]
        x, y, c = _mesh_pos()
        me, sib = (x, y, c), (x, y, 1 - c)
        chips = [(1 - x, y), (x, 1 - y), (1 - x, 1 - y)]

        def copy(a, k, block, to, src=None):
            dst = o_refs[a].at[4 * block[0] + 2 * block[1] + block[2]]
            return pltpu.make_async_remote_copy(
                src_ref=dst if src is None else src, dst_ref=dst,
                send_sem=send_sems.at[a, k], recv_sem=recv_sems.at[a, k],
                device_id=to, device_id_type=pl.DeviceIdType.MESH)

        mine = [pltpu.make_async_copy(x_refs[a], o_refs[a].at[4 * x + 2 * y + c], local_sems.at[a]) for a in range(n)]
        for cp in mine:
            cp.start()
        first = []
        for a in range(n):
            first.append(copy(a, 0, me, sib, src=x_refs[a]))
            for j, chip in enumerate(chips):
                first.append(copy(a, 1 + j, me, (*chip, c), src=x_refs[a]))
        for cp in first:
            cp.start()
        passed = []
        for a in range(n):
            for j, chip in enumerate(chips):
                copy(a, 1 + j, (*chip, c), me).wait_recv()
                fwd = copy(a, 4 + j, (*chip, c), sib)
                fwd.start()
                passed.append(fwd)
        for a in range(n):
            copy(a, 0, sib, me).wait_recv()
            for j, chip in enumerate(chips):
                copy(a, 4 + j, (*chip, 1 - c), me).wait_recv()
        for cp in first + passed:
            cp.wait_send()
        for cp in mine:
            cp.wait()

    return pl.pallas_call(
        body, name=name,
        out_shape=[jax.ShapeDtypeStruct((N_DEV,) + s.shape, s.dtype) for s in shards],
        in_specs=_any_specs(n), out_specs=_any_specs(n),
        scratch_shapes=[pltpu.SemaphoreType.DMA((n, 7)), pltpu.SemaphoreType.DMA((n, 7)), pltpu.SemaphoreType.DMA((n,))],
    )(*shards)


def _pair_exchange_call(grads, name):
    n = len(grads)

    def body(*refs):
        g_refs, r_refs = refs[:n], refs[n:2 * n]
        send_sems, recv_sems = refs[2 * n:]
        x, y, c = _mesh_pos()
        copies = []
        for a in range(n):
            for k in range(4):
                copies.append(pltpu.make_async_remote_copy(
                    src_ref=g_refs[a].at[2 * k + 1 - c], dst_ref=r_refs[a].at[k],
                    send_sem=send_sems.at[a, k], recv_sem=recv_sems.at[a, k],
                    device_id=(x, y, 1 - c), device_id_type=pl.DeviceIdType.MESH))
        for cp in copies:
            cp.start()
        for cp in copies:
            cp.wait_recv()
        for cp in copies:
            cp.wait_send()

    return pl.pallas_call(
        body, name=name,
        out_shape=[jax.ShapeDtypeStruct((4,) + g.shape[1:], g.dtype) for g in grads],
        in_specs=_any_specs(n), out_specs=_any_specs(n),
        scratch_shapes=[pltpu.SemaphoreType.DMA((n, 4)), pltpu.SemaphoreType.DMA((n, 4))],
    )(*grads)


def _chip_exchange_call(parts, name):
    n = len(parts)

    def body(*refs):
        p_refs, r_refs = refs[:n], refs[n:2 * n]
        send_sems, recv_sems = refs[2 * n:]
        x, y, c = _mesh_pos()
        chips = [(1 - x, y), (x, 1 - y), (1 - x, 1 - y)]
        copies = []
        for a in range(n):
            for j, chip in enumerate(chips):
                copies.append(pltpu.make_async_remote_copy(
                    src_ref=p_refs[a].at[2 * chip[0] + chip[1]], dst_ref=r_refs[a].at[j],
                    send_sem=send_sems.at[a, j], recv_sem=recv_sems.at[a, j],
                    device_id=(*chip, c), device_id_type=pl.DeviceIdType.MESH))
        for cp in copies:
            cp.start()
        for cp in copies:
            cp.wait_recv()
        for cp in copies:
            cp.wait_send()

    return pl.pallas_call(
        body, name=name,
        out_shape=[jax.ShapeDtypeStruct((3,) + p.shape[1:], p.dtype) for p in parts],
        in_specs=_any_specs(n), out_specs=_any_specs(n),
        scratch_shapes=[pltpu.SemaphoreType.DMA((n, 3)), pltpu.SemaphoreType.DMA((n, 3))],
    )(*parts)


def _pair_add_call(g, r1, core_chip, name):
    _, R, C = g.shape
    tr = _tile(R, max(16, (2 * MIB) // (C * 4)), 16)

    def body(cc_ref, g_ref, r_ref, p32_ref, p16_ref):
        s = g_ref[...] + r_ref[...]
        p16_ref[...] = s.astype(BF16)

        @pl.when(pl.program_id(1) == cc_ref[1])
        def _():
            p32_ref[...] = s

    grid_spec = pltpu.PrefetchScalarGridSpec(
        num_scalar_prefetch=1, grid=(R // tr, 4),
        in_specs=[pl.BlockSpec((None, tr, C), lambda r, k, cc: (2 * k + cc[0], r, 0)),
                  pl.BlockSpec((None, tr, C), lambda r, k, cc: (k, r, 0))],
        out_specs=[pl.BlockSpec((tr, C), lambda r, k, cc: (r, 0)),
                   pl.BlockSpec((None, tr, C), lambda r, k, cc: (k, r, 0))])
    return pl.pallas_call(
        body, name=name, grid_spec=grid_spec,
        out_shape=[jax.ShapeDtypeStruct((R, C), F32), jax.ShapeDtypeStruct((4, R, C), BF16)],
        compiler_params=_cp(2))(core_chip, g, r1)


def _adamw_call(w, m, v, addends, name, deps=()):
    R, C = w.shape
    tr = _tile(R, max(8, (MIB // 2) // (C * 4)), 16)
    na = len(addends)
    c1 = 1.0 - ADAM_B1 ** ADAM_STEP
    c2 = 1.0 - ADAM_B2 ** ADAM_STEP

    def body(*refs):
        w_ref, m_ref, v_ref = refs[:3]
        a_refs = refs[3:3 + na]
        g_ref, d_ref, nm_ref, nv_ref = refs[3 + na:]
        g = a_refs[0][...].astype(F32)
        for a_ref in a_refs[1:]:
            g = g + a_ref[...].astype(F32)
        nm = ADAM_B1 * m_ref[...] + (1.0 - ADAM_B1) * g
        nv = ADAM_B2 * v_ref[...] + (1.0 - ADAM_B2) * (g * g)
        g_ref[...] = g
        nm_ref[...] = nm
        nv_ref[...] = nv
        d_ref[...] = -ADAM_LR * ((nm / c1) / (jnp.sqrt(nv / c2) + ADAM_EPS) + ADAM_WD * w_ref[...])

    flat = pl.BlockSpec((tr, C), lambda r: (r, 0))
    a_specs = [flat if k is None else pl.BlockSpec((None, tr, C), functools.partial(lambda r, kk: (kk, r, 0), kk=k))
               for _, k in addends]
    out = jax.ShapeDtypeStruct((R, C), F32)
    return _pallas(
        body, 3 + na, deps, name=name, grid=(R // tr,),
        in_specs=[flat, flat, flat] + a_specs, out_specs=[flat] * 4, out_shape=[out] * 4,
        compiler_params=_cp(1))(w, m, v, *[a for a, _ in addends], *deps)


def _sum_call(parts, row0, rows, name):
    n = parts.shape[0]
    tr = _tile(math.gcd(row0, rows), 256, 8)
    b0 = row0 // tr

    def body(p_ref, o_ref):
        s = p_ref[0]
        for k in range(1, n):
            s = s + p_ref[k]
        o_ref[...] = s

    return pl.pallas_call(
        body, name=name, grid=(rows // tr,),
        in_specs=[pl.BlockSpec((n, tr, LANES), lambda r: (0, r + b0, 0))],
        out_specs=pl.BlockSpec((tr, LANES), lambda r: (r, 0)),
        out_shape=jax.ShapeDtypeStruct((rows, LANES), F32), compiler_params=_cp(1))(parts)


def _rmsnorm_call(x, g, name, deps=()):
    S, D = x.shape
    tm = _tile(S, 512, 16)

    def body(x_ref, g_ref, o_ref):
        xv = x_ref[...]
        r = lax.rsqrt(_rows_mean(xv * xv) + RMS_EPS)
        o_ref[...] = (xv * r * g_ref[...]).astype(BF16)

    return _pallas(
        body, 2, deps, name=name, grid=(S // tm,),
        in_specs=[pl.BlockSpec((tm, D), lambda i: (i, 0)), pl.BlockSpec((1, D), lambda i: (0, 0))],
        out_specs=pl.BlockSpec((tm, D), lambda i: (i, 0)),
        out_shape=jax.ShapeDtypeStruct((S, D), BF16), compiler_params=_cp(1))(x, g, *deps)


def _mm_blocked_call(a, wg, out_dtype, halves, name, deps=()):
    S, K = a.shape
    nb, _, bn = wg.shape
    tm = _tile(S, 1024, 16)
    tn = _tile(bn, 768, LANES)
    nsub = bn // tn
    J = nb * nsub

    def body(a_ref, w_ref, o_ref):
        o_ref[...] = jnp.dot(a_ref[...], w_ref[...], preferred_element_type=F32).astype(out_dtype)

    if halves:
        nh = J // 2
        out_spec = pl.BlockSpec((None, tm, tn), lambda i, j: (j // nh, i, j % nh))
        out_shape = jax.ShapeDtypeStruct((2, S, nb * bn // 2), out_dtype)
    else:
        out_spec = pl.BlockSpec((tm, tn), lambda i, j: (i, j))
        out_shape = jax.ShapeDtypeStruct((S, nb * bn), out_dtype)
    return _pallas(
        body, 2, deps, name=name, grid=(S // tm, J),
        in_specs=[pl.BlockSpec((tm, K), lambda i, j: (i, 0)),
                  pl.BlockSpec((None, K, tn), lambda i, j: (j // nsub, 0, j % nsub))],
        out_specs=out_spec, out_shape=out_shape, compiler_params=_cp(2))(a, wg, *deps)


def _mm_out_call(x, y, w, name):
    S, D = x.shape
    tm = _tile(S, 512, 16)

    def body(x_ref, y_ref, w_ref, o_ref):
        o_ref[...] = x_ref[...] + jnp.dot(y_ref[...], w_ref[...], preferred_element_type=F32)

    return pl.pallas_call(
        body, name=name, grid=(S // tm,),
        in_specs=[pl.BlockSpec((tm, D), lambda i: (i, 0)), pl.BlockSpec((tm, D), lambda i: (i, 0)),
                  pl.BlockSpec((D, D), lambda i: (0, 0))],
        out_specs=pl.BlockSpec((tm, D), lambda i: (i, 0)),
        out_shape=jax.ShapeDtypeStruct((S, D), F32), compiler_params=_cp(1))(x, y, w)


def _mm_nt_call(a, w, out_dtype, name, deps=()):
    S, K = a.shape
    N = w.shape[0]
    tm = _tile(S, 1024, 16)
    tn = _tile(N, 768, LANES)

    def body(a_ref, w_ref, o_ref):
        o_ref[...] = lax.dot_general(a_ref[...], w_ref[...], _DN_NT, preferred_element_type=F32).astype(out_dtype)

    return _pallas(
        body, 2, deps, name=name, grid=(S // tm, N // tn),
        in_specs=[pl.BlockSpec((tm, K), lambda i, j: (i, 0)), pl.BlockSpec((tn, K), lambda i, j: (j, 0))],
        out_specs=pl.BlockSpec((tm, tn), lambda i, j: (i, j)),
        out_shape=jax.ShapeDtypeStruct((S, N), out_dtype), compiler_params=_cp(2))(a, w, *deps)


def _mm_down_loss_call(x2, f, w, final_g, target, name):
    S, D = x2.shape
    Fd = f.shape[1]
    tm = _tile(S, 512, 16)
    tk = _tile(Fd, 768, LANES)
    nk = Fd // tk

    def body(x_ref, f_ref, w_ref, g_ref, t_ref, dx_ref, dxb_ref, loss_ref, dg_ref, acc):
        i, k = pl.program_id(0), pl.program_id(1)

        @pl.when(jnp.logical_and(i == 0, k == 0))
        def _():
            loss_ref[...] = jnp.zeros_like(loss_ref)
            dg_ref[...] = jnp.zeros_like(dg_ref)

        @pl.when(k == 0)
        def _():
            acc[...] = jnp.zeros_like(acc)

        acc[...] += jnp.dot(f_ref[...], w_ref[...], preferred_element_type=F32)

        @pl.when(k == nk - 1)
        def _():
            x3 = x_ref[...] + acc[...]
            r = lax.rsqrt(_rows_mean(x3 * x3) + RMS_EPS)
            g = g_ref[...]
            xn = x3 * r
            diff = xn * g - t_ref[...]
            loss_ref[...] += 0.5 * jnp.sum(_rows_mean(diff * diff))
            dout = diff * (1.0 / D)
            dg_ref[...] += _col_sum(dout * xn)
            dyg = dout * g
            dx = r * (dyg - xn * _rows_mean(dyg * xn))
            dx_ref[...] = dx
            dxb_ref[...] = dx.astype(BF16)

    row = lambda i, k: (i, 0)
    return pl.pallas_call(
        body, name=name, grid=(S // tm, nk),
        in_specs=[pl.BlockSpec((tm, D), row), pl.BlockSpec((tm, tk), lambda i, k: (i, k)),
                  pl.BlockSpec((tk, D), lambda i, k: (k, 0)), pl.BlockSpec((1, D), lambda i, k: (0, 0)),
                  pl.BlockSpec((tm, D), row)],
        out_specs=[pl.BlockSpec((tm, D), row), pl.BlockSpec((tm, D), row),
                   pl.BlockSpec((8, LANES), lambda i, k: (0, 0)), pl.BlockSpec((1, D), lambda i, k: (0, 0))],
        out_shape=[jax.ShapeDtypeStruct((S, D), F32), jax.ShapeDtypeStruct((S, D), BF16),
                   jax.ShapeDtypeStruct((8, LANES), F32), jax.ShapeDtypeStruct((1, D), F32)],
        scratch_shapes=[pltpu.VMEM((tm, D), F32)], compiler_params=_cp(2, 56))(x2, f, w, final_g, target)


def _mm_dx_norm_call(a3, wg, resid, xin, g, name, deps=()):
    na, S, Fa = a3.shape
    nb, D, bn = wg.shape
    tm = _tile(S, 512, 16)
    tk = _tile(bn, 1536, LANES)
    nsub = bn // tk
    nka = Fa // tk
    nk = nb * nsub
    assert na * nka == nk

    def body(a_ref, w_ref, r_ref, x_ref, g_ref, dx_ref, dxb_ref, dg_ref, acc):
        i, k = pl.program_id(0), pl.program_id(1)

        @pl.when(jnp.logical_and(i == 0, k == 0))
        def _():
            dg_ref[...] = jnp.zeros_like(dg_ref)

        @pl.when(k == 0)
        def _():
            acc[...] = jnp.zeros_like(acc)

        acc[...] += lax.dot_general(a_ref[...], w_ref[...], _DN_NT, preferred_element_type=F32)

        @pl.when(k == nk - 1)
        def _():
            dh = acc[...]
            xv = x_ref[...]
            r = lax.rsqrt(_rows_mean(xv * xv) + RMS_EPS)
            xn = xv * r
            dg_ref[...] += _col_sum(dh * xn)
            dyg = dh * g_ref[...]
            dx = r_ref[...] + r * (dyg - xn * _rows_mean(dyg * xn))
            dx_ref[...] = dx
            dxb_ref[...] = dx.astype(BF16)

    row = lambda i, k: (i, 0)
    return _pallas(
        body, 5, deps, name=name, grid=(S // tm, nk),
        in_specs=[pl.BlockSpec((None, tm, tk), lambda i, k: (k // nka, i, k % nka)),
                  pl.BlockSpec((None, D, tk), lambda i, k: (k // nsub, 0, k % nsub)),
                  pl.BlockSpec((tm, D), row, pipeline_mode=pl.Buffered(1)),
                  pl.BlockSpec((tm, D), row, pipeline_mode=pl.Buffered(1)), pl.BlockSpec((1, D), lambda i, k: (0, 0))],
        out_specs=[pl.BlockSpec((tm, D), row), pl.BlockSpec((tm, D), row), pl.BlockSpec((1, D), lambda i, k: (0, 0))],
        out_shape=[jax.ShapeDtypeStruct((S, D), F32), jax.ShapeDtypeStruct((S, D), BF16),
                   jax.ShapeDtypeStruct((1, D), F32)],
        scratch_shapes=[pltpu.VMEM((tm, D), F32)], compiler_params=_cp(2, 56))(a3, wg, resid, xin, g, *deps)


def _unblock_call(wg, name):
    nb, K, bn = wg.shape

    def body(w_ref, o_ref):
        o_ref[...] = w_ref[...]

    return pl.pallas_call(
        body, name=name, grid=(nb,),
        in_specs=[pl.BlockSpec((None, K, bn), lambda o: (o, 0, 0))],
        out_specs=pl.BlockSpec((K, bn), lambda o: (0, o)),
        out_shape=jax.ShapeDtypeStruct((K, nb * bn), wg.dtype), compiler_params=_cp(1))(wg)


def _mm_nt_norm_call(a, w, resid, xin, g, name, deps=()):
    S, K = a.shape
    D = w.shape[0]
    tm = _tile(S, 256, 16)

    def body(a_ref, w_ref, r_ref, x_ref, g_ref, dx_ref, dg_ref):
        @pl.when(pl.program_id(0) == 0)
        def _():
            dg_ref[...] = jnp.zeros_like(dg_ref)

        dh = lax.dot_general(a_ref[...], w_ref[...], _DN_NT, preferred_element_type=F32)
        xv = x_ref[...]
        r = lax.rsqrt(_rows_mean(xv * xv) + RMS_EPS)
        xn = xv * r
        dg_ref[...] += _col_sum(dh * xn)
        dyg = dh * g_ref[...]
        dx_ref[...] = r_ref[...] + r * (dyg - xn * _rows_mean(dyg * xn))

    row = lambda i: (i, 0)
    fixed = lambda i: (0, 0)
    return _pallas(
        body, 5, deps, name=name, grid=(S // tm,),
        in_specs=[pl.BlockSpec((tm, K), row), pl.BlockSpec((D, K), fixed, pipeline_mode=pl.Buffered(1)),
                  pl.BlockSpec((tm, D), row), pl.BlockSpec((tm, D), row), pl.BlockSpec((1, D), fixed)],
        out_specs=[pl.BlockSpec((tm, D), row), pl.BlockSpec((1, D), fixed)],
        out_shape=[jax.ShapeDtypeStruct((S, D), F32), jax.ShapeDtypeStruct((1, D), F32)],
        compiler_params=_cp(1, 56))(a, w, resid, xin, g, *deps)


def _mm_tn_cols_call(a, b3, nb, bn, name, deps=()):
    S, Ka = a.shape
    nh, _, Fb = b3.shape
    tm = _tile(S, 2048, 16)
    tn = _tile(bn, 768, LANES)
    nsub = bn // tn
    njb = Fb // tn
    J = nb * nsub
    assert nh * njb == J

    def body(a_ref, b_ref, o_ref):
        @pl.when(pl.program_id(1) == 0)
        def _():
            o_ref[...] = jnp.zeros_like(o_ref)

        o_ref[...] += lax.dot_general(a_ref[...], b_ref[...], _DN_TN, preferred_element_type=F32)

    return _pallas(
        body, 2, deps, name=name, grid=(J, S // tm),
        in_specs=[pl.BlockSpec((tm, Ka), lambda j, i: (i, 0)),
                  pl.BlockSpec((None, tm, tn), lambda j, i: (j // njb, i, j % njb))],
        out_specs=pl.BlockSpec((None, Ka, tn), lambda j, i: (j // nsub, 0, j % nsub)),
        out_shape=jax.ShapeDtypeStruct((nb, Ka, bn), F32), compiler_params=_cp(2, 56))(a, b3, *deps)


def _mm_tn_rows_call(a, b, name, deps=()):
    S, E = a.shape
    D = b.shape[1]
    tm = _tile(S, 2048, 16)
    te = _tile(E, 768, LANES)

    def body(a_ref, b_ref, o_ref):
        @pl.when(pl.program_id(1) == 0)
        def _():
            o_ref[...] = jnp.zeros_like(o_ref)

        o_ref[...] += lax.dot_general(a_ref[...], b_ref[...], _DN_TN, preferred_element_type=F32)

    return _pallas(
        body, 2, deps, name=name, grid=(E // te, S // tm),
        in_specs=[pl.BlockSpec((tm, te), lambda j, i: (i, j)), pl.BlockSpec((tm, D), lambda j, i: (i, 0))],
        out_specs=pl.BlockSpec((te, D), lambda j, i: (j, 0)),
        out_shape=jax.ShapeDtypeStruct((E, D), F32), compiler_params=_cp(2, 56))(a, b, *deps)


def _ffn_tiles(S, Fd):
    return _tile(S, 512, 16), _tile(Fd, 768, LANES)


def _taps(cw_ref):
    return [cw_ref[k:k + 1, :] for k in range(cw_ref.shape[0])]


def _conv3(prev8, cur, taps):
    s1 = _shift_down(prev8, cur, 1)
    s2 = _shift_down(prev8, cur, 2)
    return taps[2] * cur + taps[1] * s1 + taps[0] * s2, s1, s2


def _ffn_act_call(up3, cw, cb, name):
    _, S, Fd = up3.shape
    tm, tc = _ffn_tiles(S, Fd)
    nj = Fd // tc
    hb = tm // 16

    def body(g_ref, v_ref, gp_ref, vp_ref, cwg_ref, cwv_ref, cbg_ref, cbv_ref, o_ref, c_ref):
        first = pl.program_id(0) == 0
        keep = jnp.where(first, 0.0, 1.0)
        gp = gp_ref[...].astype(F32)[8:] * keep
        vp = vp_ref[...].astype(F32)[8:] * keep
        cg, _, _ = _conv3(gp, g_ref[...].astype(F32), _taps(cwg_ref))
        cv, _, _ = _conv3(vp, v_ref[...].astype(F32), _taps(cwv_ref))
        cg = cg + cbg_ref[...]
        cv = cv + cbv_ref[...]
        o_ref[...] = (_gelu(cg) * cv).astype(BF16)
        c_ref[0] = cg.astype(BF16)
        c_ref[1] = cv.astype(BF16)

    prev = lambda i, j: (jnp.maximum(i * hb - 1, 0), j)
    return pl.pallas_call(
        body, name=name, grid=(S // tm, nj),
        in_specs=[pl.BlockSpec((None, tm, tc), lambda i, j: (0, i, j)), pl.BlockSpec((None, tm, tc), lambda i, j: (1, i, j)),
                  pl.BlockSpec((None, 16, tc), lambda i, j: (0,) + prev(i, j)),
                  pl.BlockSpec((None, 16, tc), lambda i, j: (1,) + prev(i, j)),
                  pl.BlockSpec((3, tc), lambda i, j: (0, j)), pl.BlockSpec((3, tc), lambda i, j: (0, j + nj)),
                  pl.BlockSpec((1, tc), lambda i, j: (0, j)), pl.BlockSpec((1, tc), lambda i, j: (0, j + nj))],
        out_specs=[pl.BlockSpec((tm, tc), lambda i, j: (i, j)), pl.BlockSpec((2, tm, tc), lambda i, j: (0, i, j))],
        out_shape=[jax.ShapeDtypeStruct((S, Fd), BF16), jax.ShapeDtypeStruct((2, S, Fd), BF16)],
        compiler_params=_cp(2))(up3, up3, up3, up3, cw, cw, cb, cb)


def _ffn_act_bwd_call(up3, upc3, d_f, cw, name):
    _, S, Fd = up3.shape
    tm, tc = _ffn_tiles(S, Fd)
    nj = Fd // tc
    ni = S // tm
    hb = tm // 16

    def body(g_ref, v_ref, cg_ref, cv_ref, cgn_ref, cvn_ref, df_ref, dfn_ref, cwg_ref, cwv_ref,
             dup_ref, dcwg_ref, dcwv_ref, dcbg_ref, dcbv_ref):
        i = pl.program_id(1)
        keep_next = jnp.where(i == ni - 1, 0.0, 1.0)

        @pl.when(i == 0)
        def _():
            for r in (dcwg_ref, dcwv_ref, dcbg_ref, dcbv_ref):
                r[...] = jnp.zeros_like(r)

        df = jnp.concatenate([df_ref[...].astype(F32), dfn_ref[...].astype(F32)[:8] * keep_next], axis=0)
        cg = jnp.concatenate([cg_ref[...].astype(F32), cgn_ref[...].astype(F32)[:8]], axis=0)
        cv = jnp.concatenate([cv_ref[...].astype(F32), cvn_ref[...].astype(F32)[:8]], axis=0)
        gel, dgel = _gelu_parts(cg)

        def back(d, cw_ref, x_ref, dcw_ref, dcb_ref, slab):
            taps = _taps(cw_ref)
            d0 = d[:tm]
            d1 = pltpu.roll(d, tm + 8 - 1, 0)[:tm]
            d2 = pltpu.roll(d, tm + 8 - 2, 0)[:tm]
            dup_ref[slab] = (taps[2] * d0 + taps[1] * d1 + taps[0] * d2).astype(BF16)
            xv = x_ref[...].astype(F32)
            dcw_ref[2:3, :] += _col_sum(xv * d0)
            dcw_ref[1:2, :] += _col_sum(xv * d1)
            dcw_ref[0:1, :] += _col_sum(xv * d2)
            dcb_ref[...] += _col_sum(d0)

        back(df * cv * dgel, cwg_ref, g_ref, dcwg_ref, dcbg_ref, 0)
        back(df * gel, cwv_ref, v_ref, dcwv_ref, dcbv_ref, 1)

    nxt = lambda j, i: (jnp.minimum((i + 1) * hb, S // 16 - 1), j)
    main = lambda s: pl.BlockSpec((None, tm, tc), lambda j, i: (s, i, j))
    halo = lambda s: pl.BlockSpec((None, 16, tc), lambda j, i: (s,) + nxt(j, i))
    acc3 = pl.BlockSpec((3, tc), lambda j, i: (0, j))
    acc1 = pl.BlockSpec((1, tc), lambda j, i: (0, j))
    return pl.pallas_call(
        body, name=name, grid=(nj, ni),
        in_specs=[main(0), main(1), main(0), main(1), halo(0), halo(1),
                  pl.BlockSpec((tm, tc), lambda j, i: (i, j)), pl.BlockSpec((16, tc), nxt),
                  pl.BlockSpec((3, tc), lambda j, i: (0, j)), pl.BlockSpec((3, tc), lambda j, i: (0, j + nj))],
        out_specs=[pl.BlockSpec((2, tm, tc), lambda j, i: (0, i, j)), acc3, acc3, acc1, acc1],
        out_shape=[jax.ShapeDtypeStruct((2, S, Fd), BF16), jax.ShapeDtypeStruct((3, Fd), F32),
                   jax.ShapeDtypeStruct((3, Fd), F32), jax.ShapeDtypeStruct((1, Fd), F32),
                   jax.ShapeDtypeStruct((1, Fd), F32)],
        compiler_params=_cp(2, 56))(up3, up3, upc3, upc3, upc3, upc3, d_f, d_f, cw, cw)


SUB_LANES = 256


def _lane_taps(cw_ref, ls):
    return [cw_ref[k:k + 1, ls] for k in range(cw_ref.shape[0])]


def _ffn_down_fused_call(up3, cw, cb, w, x2, final_g, target, name):
    _, S, Fd = up3.shape
    D = x2.shape[1]
    tm, tc = _ffn_tiles(S, Fd)
    nk = Fd // tc
    hb = tm // 16
    sc = _tile(tc, SUB_LANES, LANES)

    def body(g_ref, v_ref, gp_ref, vp_ref, cwg_ref, cwv_ref, cbg_ref, cbv_ref, w_ref, x_ref, fg_ref, t_ref,
             f_ref, dx_ref, dxb_ref, loss_ref, dg_ref, acc):
        i, k = pl.program_id(0), pl.program_id(1)

        @pl.when(jnp.logical_and(i == 0, k == 0))
        def _():
            loss_ref[...] = jnp.zeros_like(loss_ref)
            dg_ref[...] = jnp.zeros_like(dg_ref)

        @pl.when(k == 0)
        def _():
            acc[...] = jnp.zeros_like(acc)

        keep = jnp.where(i == 0, 0.0, 1.0)
        part = None
        for s in range(tc // sc):
            ls = slice(s * sc, (s + 1) * sc)
            gp = gp_ref[:, ls].astype(F32)[8:] * keep
            vp = vp_ref[:, ls].astype(F32)[8:] * keep
            cg, _, _ = _conv3(gp, g_ref[:, ls].astype(F32), _lane_taps(cwg_ref, ls))
            cv, _, _ = _conv3(vp, v_ref[:, ls].astype(F32), _lane_taps(cwv_ref, ls))
            fs = (_gelu(cg + cbg_ref[:, ls]) * (cv + cbv_ref[:, ls])).astype(BF16)
            f_ref[:, ls] = fs
            d = jnp.dot(fs, w_ref[s * sc:(s + 1) * sc, :], preferred_element_type=F32)
            part = d if part is None else part + d
        acc[...] += part

        @pl.when(k == nk - 1)
        def _():
            x3 = x_ref[...] + acc[...]
            r = lax.rsqrt(_rows_mean(x3 * x3) + RMS_EPS)
            g = fg_ref[...]
            xn = x3 * r
            diff = xn * g - t_ref[...]
            loss_ref[...] += 0.5 * jnp.sum(_rows_mean(diff * diff))
            dout = diff * (1.0 / D)
            dg_ref[...] += _col_sum(dout * xn)
            dyg = dout * g
            dx = r * (dyg - xn * _rows_mean(dyg * xn))
            dx_ref[...] = dx
            dxb_ref[...] = dx.astype(BF16)

    row = lambda i, k: (i, 0)
    prev = lambda i, k: (jnp.maximum(i * hb - 1, 0), k)
    once = pl.Buffered(1)
    return pl.pallas_call(
        body, name=name, grid=(S // tm, nk),
        in_specs=[pl.BlockSpec((None, tm, tc), lambda i, k: (0, i, k)), pl.BlockSpec((None, tm, tc), lambda i, k: (1, i, k)),
                  pl.BlockSpec((None, 16, tc), lambda i, k: (0,) + prev(i, k)),
                  pl.BlockSpec((None, 16, tc), lambda i, k: (1,) + prev(i, k)),
                  pl.BlockSpec((3, tc), lambda i, k: (0, k)), pl.BlockSpec((3, tc), lambda i, k: (0, k + nk)),
                  pl.BlockSpec((1, tc), lambda i, k: (0, k)), pl.BlockSpec((1, tc), lambda i, k: (0, k + nk)),
                  pl.BlockSpec((tc, D), lambda i, k: (k, 0)),
                  pl.BlockSpec((tm, D), row, pipeline_mode=once), pl.BlockSpec((1, D), lambda i, k: (0, 0)),
                  pl.BlockSpec((tm, D), row, pipeline_mode=once)],
        out_specs=[pl.BlockSpec((tm, tc), lambda i, k: (i, k)), pl.BlockSpec((tm, D), row), pl.BlockSpec((tm, D), row),
                   pl.BlockSpec((8, LANES), lambda i, k: (0, 0)), pl.BlockSpec((1, D), lambda i, k: (0, 0))],
        out_shape=[jax.ShapeDtypeStruct((S, Fd), BF16), jax.ShapeDtypeStruct((S, D), F32),
                   jax.ShapeDtypeStruct((S, D), BF16), jax.ShapeDtypeStruct((8, LANES), F32),
                   jax.ShapeDtypeStruct((1, D), F32)],
        scratch_shapes=[pltpu.VMEM((tm, D), F32)],
        compiler_params=_cp(2, 56))(up3, up3, up3, up3, cw, cw, cb, cb, w, x2, final_g, target)


def _ffn_bwd_fused_call(up3, d_f, cw, cb, wg, resid, xin, g, name, deps=()):
    _, S, Fd = up3.shape
    nb, D, bn = wg.shape
    tm, tc = _tile(S, 256, 16), _ffn_tiles(S, Fd)[1]
    nk = Fd // tc
    ni = S // tm
    hb = tm // 16
    nsubw = bn // tc
    half = nb // 2
    sc = _tile(tc, SUB_LANES, LANES)

    def body(g_ref, v_ref, gp_ref, vp_ref, gn_ref, vn_ref, df_ref, dfn_ref, cwg_ref, cwv_ref, cbg_ref, cbv_ref,
             wgate_ref, wval_ref, r_ref, x_ref, ng_ref,
             dup_ref, dx_ref, dxb_ref, dg_ref, dcwg_ref, dcwv_ref, dcbg_ref, dcbv_ref, acc):
        i, k = pl.program_id(0), pl.program_id(1)

        @pl.when(jnp.logical_and(i == 0, k == 0))
        def _():
            for r in (dg_ref, dcwg_ref, dcwv_ref, dcbg_ref, dcbv_ref):
                r[...] = jnp.zeros_like(r)

        @pl.when(k == 0)
        def _():
            acc[...] = jnp.zeros_like(acc)

        keep_prev = jnp.where(i == 0, 0.0, 1.0)
        keep_next = jnp.where(i == ni - 1, 0.0, 1.0)
        zeros8 = jnp.zeros((8, sc), F32)
        part = None
        for s in range(tc // sc):
            ls = slice(s * sc, (s + 1) * sc)
            off = pl.multiple_of(k * tc + s * sc, LANES)
            df = jnp.concatenate([df_ref[:, ls].astype(F32), dfn_ref[:, ls].astype(F32)[:8] * keep_next], axis=0)

            def half_fwd(x_ref_, xp_ref, xn_ref, cw_ref, cb_ref):
                taps = _lane_taps(cw_ref, ls)
                prev8 = xp_ref[:, ls].astype(F32)[8:] * keep_prev
                ext = jnp.concatenate([x_ref_[:, ls].astype(F32), xn_ref[:, ls].astype(F32)[:8]], axis=0)
                conv, s1, s2 = _conv3(prev8, ext, taps)
                return taps, conv + cb_ref[:, ls], (ext, s1, s2)

            tg, cg, gsh = half_fwd(g_ref, gp_ref, gn_ref, cwg_ref, cbg_ref)
            tv, cv, vsh = half_fwd(v_ref, vp_ref, vn_ref, cwv_ref, cbv_ref)
            gel, dgel = _gelu_parts(cg)
            d_gate = df * cv * dgel
            d_val = df * gel

            def half_bwd(d, taps, shifts, dcw_ref, dcb_ref, slab, w_ref):
                dup = (taps[2] * d + taps[1] * _shift_up(d, zeros8, 1) + taps[0] * _shift_up(d, zeros8, 2))[:tm]
                dupb = dup.astype(BF16)
                dup_ref[slab, :, ls] = dupb
                dm = d[:tm]
                ext, s1, s2 = shifts
                dcw_ref[2:3, pl.ds(off, sc)] += _col_sum(ext[:tm] * dm)
                dcw_ref[1:2, pl.ds(off, sc)] += _col_sum(s1[:tm] * dm)
                dcw_ref[0:1, pl.ds(off, sc)] += _col_sum(s2[:tm] * dm)
                dcb_ref[:, pl.ds(off, sc)] += _col_sum(dm)
                return lax.dot_general(dupb, w_ref[:, ls], _DN_NT, preferred_element_type=F32)

            d = (half_bwd(d_gate, tg, gsh, dcwg_ref, dcbg_ref, 0, wgate_ref)
                 + half_bwd(d_val, tv, vsh, dcwv_ref, dcbv_ref, 1, wval_ref))
            part = d if part is None else part + d
        acc[...] += part

        @pl.when(k == nk - 1)
        def _():
            dh = acc[...]
            xv = x_ref[...]
            r = lax.rsqrt(_rows_mean(xv * xv) + RMS_EPS)
            xn = xv * r
            dg_ref[...] += _col_sum(dh * xn)
            dyg = dh * ng_ref[...]
            dx = r_ref[...] + r * (dyg - xn * _rows_mean(dyg * xn))
            dx_ref[...] = dx
            dxb_ref[...] = dx.astype(BF16)

    row = lambda i, k: (i, 0)
    prev = lambda i, k: (jnp.maximum(i * hb - 1, 0), k)
    nxt = lambda i, k: (jnp.minimum((i + 1) * hb, S // 16 - 1), k)
    main = lambda s: pl.BlockSpec((None, tm, tc), lambda i, k: (s, i, k))
    halo = lambda s, f: pl.BlockSpec((None, 16, tc), lambda i, k: (s,) + f(i, k))
    full3 = pl.BlockSpec((3, Fd), lambda i, k: (0, 0))
    full1 = pl.BlockSpec((1, Fd), lambda i, k: (0, 0))
    once = pl.Buffered(1)
    return _pallas(
        body, 17, deps, name=name, grid=(ni, nk),
        in_specs=[main(0), main(1), halo(0, prev), halo(1, prev), halo(0, nxt), halo(1, nxt),
                  pl.BlockSpec((tm, tc), lambda i, k: (i, k)), pl.BlockSpec((16, tc), nxt),
                  pl.BlockSpec((3, tc), lambda i, k: (0, k)), pl.BlockSpec((3, tc), lambda i, k: (0, k + nk)),
                  pl.BlockSpec((1, tc), lambda i, k: (0, k)), pl.BlockSpec((1, tc), lambda i, k: (0, k + nk)),
                  pl.BlockSpec((None, D, tc), lambda i, k: (k // nsubw, 0, k % nsubw)),
                  pl.BlockSpec((None, D, tc), lambda i, k: (half + k // nsubw, 0, k % nsubw)),
                  pl.BlockSpec((tm, D), row, pipeline_mode=once), pl.BlockSpec((tm, D), row, pipeline_mode=once),
                  pl.BlockSpec((1, D), lambda i, k: (0, 0))],
        out_specs=[pl.BlockSpec((2, tm, tc), lambda i, k: (0, i, k)), pl.BlockSpec((tm, D), row),
                   pl.BlockSpec((tm, D), row), pl.BlockSpec((1, D), lambda i, k: (0, 0)), full3, full3, full1, full1],
        out_shape=[jax.ShapeDtypeStruct((2, S, Fd), BF16), jax.ShapeDtypeStruct((S, D), F32),
                   jax.ShapeDtypeStruct((S, D), BF16), jax.ShapeDtypeStruct((1, D), F32),
                   jax.ShapeDtypeStruct((3, Fd), F32), jax.ShapeDtypeStruct((3, Fd), F32),
                   jax.ShapeDtypeStruct((1, Fd), F32), jax.ShapeDtypeStruct((1, Fd), F32)],
        scratch_shapes=[pltpu.VMEM((tm, D), F32)],
        compiler_params=_cp(2, 60))(up3, up3, up3, up3, up3, up3, d_f, d_f, cw, cw, cb, cb, wg, wg, resid, xin, g, *deps)


def _gm_forward_tile(pv, vg, vb, ws_ref, bsb_ref, mbuf, H, nc):
    W = H * CHUNK
    z, dz = _gelu_parts(pv)
    u, v0 = z[:, :W], z[:, W:]
    xc = v0 - _rows_mean(v0)
    rs = lax.rsqrt(_rows_mean(xc * xc) + LN_EPS)
    vh = xc * rs
    vnb = (vh * vg + vb).astype(BF16)
    mask = lax.broadcasted_iota(jnp.int32, (CHUNK, CHUNK), 0) >= lax.broadcasted_iota(jnp.int32, (CHUNK, CHUNK), 1)
    for h in range(H):
        cs = slice(h * CHUNK, (h + 1) * CHUNK)
        wm = jnp.where(mask, ws_ref[h], 0.0).astype(BF16)
        vcat = jnp.concatenate([vnb[c * CHUNK:(c + 1) * CHUNK, cs] for c in range(nc)], axis=1)
        mix = jnp.dot(wm, vcat, preferred_element_type=F32)
        for c in range(nc):
            mbuf[c * CHUNK:(c + 1) * CHUNK, cs] = mix[:, c * CHUNK:(c + 1) * CHUNK] + bsb_ref[h]
    return dz, u, vh, rs, vnb, mask


def _gm_fwd_call(p, v_g, v_b, ws, bsb, out_g, name, deps=()):
    S = p.shape[0]
    H = ws.shape[0]
    W = H * CHUNK
    tm = _tile(S, 256, CHUNK)
    nc = tm // CHUNK

    def body(p_ref, vg_ref, vb_ref, ws_ref, bsb_ref, og_ref, y_ref, mbuf):
        _, u, _, _, _, _ = _gm_forward_tile(p_ref[...], vg_ref[...], vb_ref[...], ws_ref, bsb_ref, mbuf, H, nc)
        yg = u * mbuf[...]
        r = lax.rsqrt(_rows_mean(yg * yg) + RMS_EPS)
        y_ref[...] = (yg * r * og_ref[...]).astype(BF16)

    vec = pl.BlockSpec((1, W), lambda i: (0, 0))
    mat = pl.BlockSpec((H, CHUNK, CHUNK), lambda i: (0, 0, 0))
    return _pallas(
        body, 6, deps, name=name, grid=(S // tm,),
        in_specs=[pl.BlockSpec((tm, 2 * W), lambda i: (i, 0)), vec, vec, mat, mat, vec],
        out_specs=pl.BlockSpec((tm, W), lambda i: (i, 0)),
        out_shape=jax.ShapeDtypeStruct((S, 2 * W), BF16),
        scratch_shapes=[pltpu.VMEM((tm, W), F32)], compiler_params=_cp(1))(p, v_g, v_b, ws, bsb, out_g, *deps)


def _gm_bwd_call(p, d_y, v_g, v_b, ws, bsb, out_g, name, deps=()):
    S = p.shape[0]
    H = ws.shape[0]
    W = H * CHUNK
    tm = _tile(S, 256, CHUNK)
    nc = tm // CHUNK
    ni = S // tm

    def body(p_ref, dy_ref, vg_ref, vb_ref, ws_ref, bsb_ref, og_ref,
             dp_ref, dvg_ref, dvb_ref, dws_ref, dbs_ref, dog_ref, mbuf, dvbuf):
        i = pl.program_id(0)

        @pl.when(i == 0)
        def _():
            for r in (dvg_ref, dvb_ref, dws_ref, dbs_ref, dog_ref):
                r[...] = jnp.zeros_like(r)

        vg = vg_ref[...]
        dz, u, vh, rs, vnb, mask = _gm_forward_tile(p_ref[...], vg, vb_ref[...], ws_ref, bsb_ref, mbuf, H, nc)
        mixed = mbuf[...]
        yg = u * mixed
        r = lax.rsqrt(_rows_mean(yg * yg) + RMS_EPS)
        yn = yg * r
        dya = dy_ref[...]
        dog_ref[...] += _col_sum(dya * yn)
        dyg = dya * og_ref[...]
        dygm = r * (dyg - yn * _rows_mean(dyg * yn))
        du = dygm * mixed
        dmix = dygm * u
        dmb = dmix.astype(BF16)
        for h in range(H):
            cs = slice(h * CHUNK, (h + 1) * CHUNK)
            wm = jnp.where(mask, ws_ref[h], 0.0).astype(BF16)
            dcat = jnp.concatenate([dmb[c * CHUNK:(c + 1) * CHUNK, cs] for c in range(nc)], axis=1)
            vcat = jnp.concatenate([vnb[c * CHUNK:(c + 1) * CHUNK, cs] for c in range(nc)], axis=1)
            dvn = lax.dot_general(wm, dcat, _DN_TN, preferred_element_type=F32)
            dws_ref[h] += jnp.where(mask, lax.dot_general(dcat, vcat, _DN_NT, preferred_element_type=F32), 0.0)
            dbs = dmix[0:CHUNK, cs]
            for c in range(1, nc):
                dbs = dbs + dmix[c * CHUNK:(c + 1) * CHUNK, cs]
            dbs_ref[h] += dbs
            for c in range(nc):
                dvbuf[c * CHUNK:(c + 1) * CHUNK, cs] = dvn[:, c * CHUNK:(c + 1) * CHUNK]
        dvn_all = dvbuf[...]
        dvg_ref[...] += _col_sum(dvn_all * vh)
        dvb_ref[...] += _col_sum(dvn_all)
        dvh = dvn_all * vg
        dv0 = rs * (dvh - _rows_mean(dvh) - vh * _rows_mean(dvh * vh))
        dp_ref[...] = (jnp.concatenate([du, dv0], axis=1) * dz).astype(BF16)

        @pl.when(i == ni - 1)
        def _():
            for h in range(H):
                dbs_ref[h] = jnp.broadcast_to(jnp.sum(dbs_ref[h], axis=1, keepdims=True), (CHUNK, CHUNK))

    vec = pl.BlockSpec((1, W), lambda i: (0, 0))
    mat = pl.BlockSpec((H, CHUNK, CHUNK), lambda i: (0, 0, 0))
    vshape = jax.ShapeDtypeStruct((1, W), F32)
    mshape = jax.ShapeDtypeStruct((H, CHUNK, CHUNK), F32)
    return _pallas(
        body, 7, deps, name=name, grid=(ni,),
        in_specs=[pl.BlockSpec((tm, 2 * W), lambda i: (i, 0)), pl.BlockSpec((tm, W), lambda i: (i, 0)),
                  vec, vec, mat, mat, vec],
        out_specs=[pl.BlockSpec((tm, 2 * W), lambda i: (i, 0)), vec, vec, mat, mat, vec],
        out_shape=[jax.ShapeDtypeStruct((S, 4 * W), BF16), vshape, vshape, mshape, mshape, vshape],
        scratch_shapes=[pltpu.VMEM((tm, W), F32), pltpu.VMEM((tm, W), F32)],
        compiler_params=_cp(1))(p, d_y, v_g, v_b, ws, bsb, out_g, *deps)


def _lru_gates(prev8, xl, cw, cb, wa_ref, ba, wx_ref, bx, lam, H):
    sh = [_shift_down(prev8, xl, k) for k in range(4)]
    xr = cw[3] * sh[0] + cw[2] * sh[1] + cw[1] * sh[2] + cw[0] * sh[3] + cb
    xrb = xr.astype(BF16)
    rp, ip = [], []
    for h in range(H):
        cs = slice(h * CHUNK, (h + 1) * CHUNK)
        rp.append(jnp.dot(xrb[:, cs], wa_ref[h].astype(BF16), preferred_element_type=F32))
        ip.append(jnp.dot(xrb[:, cs], wx_ref[h].astype(BF16), preferred_element_type=F32))
    r = _sigmoid(jnp.concatenate(rp, axis=1) + ba)
    ig = _sigmoid(jnp.concatenate(ip, axis=1) + bx)
    sp = _softplus(-lam)
    t = jnp.tanh((-LRU_C) * r * sp)
    q = lax.rsqrt(1.0 - t)
    a = jnp.sqrt(1.0 + t) * q
    mult = jnp.sqrt(-2.0 * t) * q
    a2_over_mult = (1.0 + t) * q * lax.rsqrt(-2.0 * t)
    return xr, xrb, r, ig, sp, a, mult, a2_over_mult, sh


def _lru_fwd_call(p, cw, cb, wa, ba, wx, bx, lam, out_g, y_half, name):
    S = p.shape[0]
    H = wa.shape[0]
    W = H * CHUNK
    tm = _tile(S, 256, 16)
    ng = tm // 8

    def body(pg_ref, px_ref, cw_ref, cb_ref, wa_ref, ba_ref, wx_ref, bx_ref, lam_ref, og_ref, y_in_ref,
             y_ref, h_ref, xprev, hcar, abuf, bbuf):
        @pl.when(pl.program_id(0) == 0)
        def _():
            xprev[...] = jnp.zeros_like(xprev)
            hcar[...] = jnp.zeros_like(hcar)

        xl = px_ref[...]
        xr, _, _, ig, _, a, mult, _, _ = _lru_gates(xprev[...], xl, _taps(cw_ref), cb_ref[...], wa_ref, ba_ref[...],
                                                    wx_ref, bx_ref[...], lam_ref[...], H)
        xprev[...] = xl[tm - 8:]
        b = mult * (ig * xr)
        sub = lax.broadcasted_iota(jnp.int32, (tm, W), 0) & 7
        for d in (1, 2, 4):
            m = sub >= d
            a_s = jnp.where(m, pltpu.roll(a, d, 0), 1.0)
            b_s = jnp.where(m, pltpu.roll(b, d, 0), 0.0)
            b = a * b_s + b
            a = a * a_s
        abuf[...] = a
        bbuf[...] = b

        def step(g, carry):
            r0 = pl.multiple_of(g * 8, 8)
            h_ref[pl.ds(r0, 8), :] = abuf[pl.ds(r0, 8), :] * carry + bbuf[pl.ds(r0, 8), :]
            return jnp.broadcast_to(h_ref[pl.ds(r0 + 7, 1), :], (8, W))

        hcar[...] = lax.fori_loop(0, ng, step, hcar[...])
        yl = h_ref[...] * _gelu(pg_ref[...])
        r = lax.rsqrt(_rows_mean(yl * yl) + RMS_EPS)
        y_ref[...] = (yl * r * og_ref[...]).astype(BF16)

    vec = pl.BlockSpec((1, W), lambda i: (0, 0))
    mat = pl.BlockSpec((H, CHUNK, CHUNK), lambda i: (0, 0, 0))
    return pl.pallas_call(
        body, name=name, grid=(S // tm,),
        in_specs=[pl.BlockSpec((tm, W), lambda i: (i, 2)), pl.BlockSpec((tm, W), lambda i: (i, 3)),
                  pl.BlockSpec((4, W), lambda i: (0, 0)), vec, mat, vec, mat, vec, vec, vec,
                  pl.BlockSpec(memory_space=pl.ANY)],
        out_specs=[pl.BlockSpec((tm, W), lambda i: (i, 1)), pl.BlockSpec((tm, W), lambda i: (i, 0))],
        out_shape=[jax.ShapeDtypeStruct((S, 2 * W), BF16), jax.ShapeDtypeStruct((S, W), F32)],
        input_output_aliases={10: 0},
        scratch_shapes=[pltpu.VMEM((8, W), F32), pltpu.VMEM((8, W), F32), pltpu.VMEM((tm, W), F32),
                        pltpu.VMEM((tm, W), F32)],
        compiler_params=_cp(1))(p, p, cw, cb, wa, ba, wx, bx, lam, out_g, y_half)


def _lru_bwd_call(p, hs, d_y, cw, cb, wa, ba, wx, bx, lam, out_g, dp_half, name):
    S = p.shape[0]
    H = wa.shape[0]
    W = H * CHUNK
    tm = _tile(S, 256, 16)
    ng = tm // 8
    ni = S // tm
    hb = tm // 8

    def body(pg_ref, px_ref, pxp_ref, h_ref, hp_ref, dy_ref, cw_ref, cb_ref, wa_ref, ba_ref, wx_ref, bx_ref,
             lam_ref, og_ref, dp_in_ref,
             dp_ref, dcw_ref, dcb_ref, dwa_ref, dba_ref, dwx_ref, dbx_ref, dlam_ref, dog_ref,
             a_next, e_next, dxr_next, abuf, bbuf, ebuf):
        i = pl.program_id(0)
        ri = ni - 1 - i

        @pl.when(i == 0)
        def _():
            for r in (dcw_ref, dcb_ref, dwa_ref, dba_ref, dwx_ref, dbx_ref, dlam_ref, dog_ref,
                      a_next, e_next, dxr_next):
                r[...] = jnp.zeros_like(r)

        keep_prev = jnp.where(ri == 0, 0.0, 1.0)
        cw_ = _taps(cw_ref)
        lam_ = lam_ref[...]
        xl = px_ref[...]
        xr, xrb, r, ig, sp, a, mult, a2m, sh = _lru_gates(pxp_ref[...] * keep_prev, xl, cw_, cb_ref[...], wa_ref,
                                                          ba_ref[...], wx_ref, bx_ref[...], lam_, H)
        gg, dgg = _gelu_parts(pg_ref[...])
        hv = h_ref[...]
        yl = hv * gg
        rr = lax.rsqrt(_rows_mean(yl * yl) + RMS_EPS)
        yn = yl * rr
        dyb = dy_ref[...]
        dog_ref[...] += _col_sum(dyb * yn)
        dyg = dyb * og_ref[...]
        dyl = rr * (dyg - yn * _rows_mean(dyg * yn))
        dh = dyl * gg
        dgl = dyl * hv * dgg

        an = _shift_up(a, a_next[...], 1)
        eb = dh
        sub = lax.broadcasted_iota(jnp.int32, (tm, W), 0) & 7
        for d in (1, 2, 4):
            m = sub < 8 - d
            a_s = jnp.where(m, pltpu.roll(an, tm - d, 0), 1.0)
            e_s = jnp.where(m, pltpu.roll(eb, tm - d, 0), 0.0)
            eb = an * e_s + eb
            an = an * a_s
        abuf[...] = an
        bbuf[...] = eb

        def step(g, carry):
            r0 = pl.multiple_of((ng - 1 - g) * 8, 8)
            ebuf[pl.ds(r0, 8), :] = abuf[pl.ds(r0, 8), :] * carry + bbuf[pl.ds(r0, 8), :]
            return jnp.broadcast_to(ebuf[pl.ds(r0, 1), :], (8, W))

        lax.fori_loop(0, ng, step, jnp.broadcast_to(e_next[0:1, :], (8, W)))
        e = ebuf[...]
        a_next[...] = a[0:8]
        e_next[...] = e[0:8]

        hm1 = _shift_down(hp_ref[...] * keep_prev, hv, 1)
        da = e * hm1
        dmult = e * ig * xr
        di = e * mult * xr
        dxr = e * mult * ig
        dla = da * a - dmult * a2m
        dr = dla * ((-LRU_C) * sp)
        dlam_ref[...] += _col_sum(dla * ((-LRU_C) * r))
        dpr = dr * r * (1.0 - r)
        dpi = di * ig * (1.0 - ig)
        dba_ref[...] += _col_sum(dpr)
        dbx_ref[...] += _col_sum(dpi)
        dprb = dpr.astype(BF16)
        dpib = dpi.astype(BF16)
        back = []
        for h in range(H):
            cs = slice(h * CHUNK, (h + 1) * CHUNK)
            wab = wa_ref[h].astype(BF16)
            wxb = wx_ref[h].astype(BF16)
            back.append(lax.dot_general(dprb[:, cs], wab, _DN_NT, preferred_element_type=F32)
                        + lax.dot_general(dpib[:, cs], wxb, _DN_NT, preferred_element_type=F32))
            dwa_ref[h] += lax.dot_general(xrb[:, cs], dprb[:, cs], _DN_TN, preferred_element_type=F32)
            dwx_ref[h] += lax.dot_general(xrb[:, cs], dpib[:, cs], _DN_TN, preferred_element_type=F32)
        dxr = dxr + jnp.concatenate(back, axis=1)

        nxt = dxr_next[...]
        dxl = (cw_[3] * dxr + cw_[2] * _shift_up(dxr, nxt, 1) + cw_[1] * _shift_up(dxr, nxt, 2)
               + cw_[0] * _shift_up(dxr, nxt, 3))
        dxr_next[...] = dxr[0:8]
        for k in range(4):
            dcw_ref[k:k + 1, :] += _col_sum(sh[3 - k] * dxr)
        dcb_ref[...] += _col_sum(dxr)
        dp_ref[...] = jnp.concatenate([dgl, dxl], axis=1).astype(BF16)

        @pl.when(i == ni - 1)
        def _():
            dlam_ref[...] = -dlam_ref[...] * _sigmoid(-lam_)

    vec = pl.BlockSpec((1, W), lambda i: (0, 0))
    mat = pl.BlockSpec((H, CHUNK, CHUNK), lambda i: (0, 0, 0))
    rev = lambda i: ni - 1 - i
    prev = lambda i: jnp.maximum(rev(i) * hb - 1, 0)
    vshape = jax.ShapeDtypeStruct((1, W), F32)
    mshape = jax.ShapeDtypeStruct((H, CHUNK, CHUNK), F32)
    tile = lambda: pltpu.VMEM((tm, W), F32)
    car = lambda: pltpu.VMEM((8, W), F32)
    return pl.pallas_call(
        body, name=name, grid=(ni,),
        in_specs=[pl.BlockSpec((tm, W), lambda i: (rev(i), 2)), pl.BlockSpec((tm, W), lambda i: (rev(i), 3)),
                  pl.BlockSpec((8, W), lambda i: (prev(i), 3)),
                  pl.BlockSpec((tm, W), lambda i: (rev(i), 0)), pl.BlockSpec((8, W), lambda i: (prev(i), 0)),
                  pl.BlockSpec((tm, W), lambda i: (rev(i), 1)),
                  pl.BlockSpec((4, W), lambda i: (0, 0)), vec, mat, vec, mat, vec, vec, vec,
                  pl.BlockSpec(memory_space=pl.ANY)],
        out_specs=[pl.BlockSpec((tm, 2 * W), lambda i: (rev(i), 1)), pl.BlockSpec((4, W), lambda i: (0, 0)), vec,
                   mat, vec, mat, vec, vec, vec],
        out_shape=[jax.ShapeDtypeStruct((S, 4 * W), BF16), jax.ShapeDtypeStruct((4, W), F32), vshape,
                   mshape, vshape, mshape, vshape, vshape, vshape],
        input_output_aliases={14: 0},
        scratch_shapes=[car(), car(), car(), tile(), tile(), tile()],
        compiler_params=_cp(1, 56))(p, p, p, hs, hs, d_y, cw, cb, wa, ba, wx, bx, lam, out_g, dp_half)


def _rows128(a):
    return a.reshape(-1, LANES).astype(F32)


def _pack(arrays, pad_to=256):
    flat = jnp.concatenate([_rows128(a) for a in arrays], axis=0)
    pad = (-flat.shape[0]) % pad_to
    if pad:
        flat = jnp.concatenate([flat, jnp.zeros((pad, LANES), F32)], axis=0)
    return flat


def _unpack(flat, shapes):
    out, r = [], 0
    for s in shapes:
        n = 1
        for d in s:
            n *= d
        out.append(flat[r:r + n // LANES].reshape(s))
        r += n // LANES
    return out


def kernel(x, norm1_g, w_in, gm_v_g, gm_v_b, gm_ws, gm_bs, lru_conv_w, lru_conv_b, lru_wa, lru_ba, lru_wx, lru_bx, lru_lambda, gm_out_g, lru_out_g, w_out, norm2_g, ffn_w_up, ffn_conv_w, ffn_conv_b, ffn_w_down, final_g, loss_target, m_norm1_g, m_w_in, m_gm_v_g, m_gm_v_b, m_gm_ws, m_gm_bs, m_lru_conv_w, m_lru_conv_b, m_lru_wa, m_lru_ba, m_lru_wx, m_lru_bx, m_lru_lambda, m_gm_out_g, m_lru_out_g, m_w_out, m_norm2_g, m_ffn_w_up, m_ffn_conv_w, m_ffn_conv_b, m_ffn_w_down, m_final_g, v_norm1_g, v_w_in, v_gm_v_g, v_gm_v_b, v_gm_ws, v_gm_bs, v_lru_conv_w, v_lru_conv_b, v_lru_wa, v_lru_ba, v_lru_wx, v_lru_bx, v_lru_lambda, v_gm_out_g, v_lru_out_g, v_w_out, v_norm2_g, v_ffn_w_up, v_ffn_conv_w, v_ffn_conv_b, v_ffn_w_down, v_final_g):
    wts = dict(norm1_g=norm1_g, w_in=w_in, gm_v_g=gm_v_g, gm_v_b=gm_v_b, gm_ws=gm_ws, gm_bs=gm_bs,
               lru_conv_w=lru_conv_w, lru_conv_b=lru_conv_b, lru_wa=lru_wa, lru_ba=lru_ba, lru_wx=lru_wx,
               lru_bx=lru_bx, lru_lambda=lru_lambda, gm_out_g=gm_out_g, lru_out_g=lru_out_g, w_out=w_out,
               norm2_g=norm2_g, ffn_w_up=ffn_w_up, ffn_conv_w=ffn_conv_w, ffn_conv_b=ffn_conv_b,
               ffn_w_down=ffn_w_down, final_g=final_g)
    mom = dict(norm1_g=m_norm1_g, w_in=m_w_in, gm_v_g=m_gm_v_g, gm_v_b=m_gm_v_b, gm_ws=m_gm_ws, gm_bs=m_gm_bs,
               lru_conv_w=m_lru_conv_w, lru_conv_b=m_lru_conv_b, lru_wa=m_lru_wa, lru_ba=m_lru_ba, lru_wx=m_lru_wx,
               lru_bx=m_lru_bx, lru_lambda=m_lru_lambda, gm_out_g=m_gm_out_g, lru_out_g=m_lru_out_g, w_out=m_w_out,
               norm2_g=m_norm2_g, ffn_w_up=m_ffn_w_up, ffn_conv_w=m_ffn_conv_w, ffn_conv_b=m_ffn_conv_b,
               ffn_w_down=m_ffn_w_down, final_g=m_final_g)
    var = dict(norm1_g=v_norm1_g, w_in=v_w_in, gm_v_g=v_gm_v_g, gm_v_b=v_gm_v_b, gm_ws=v_gm_ws, gm_bs=v_gm_bs,
               lru_conv_w=v_lru_conv_w, lru_conv_b=v_lru_conv_b, lru_wa=v_lru_wa, lru_ba=v_lru_ba, lru_wx=v_lru_wx,
               lru_bx=v_lru_bx, lru_lambda=v_lru_lambda, gm_out_g=v_gm_out_g, lru_out_g=v_lru_out_g, w_out=v_w_out,
               norm2_g=v_norm2_g, ffn_w_up=v_ffn_w_up, ffn_conv_w=v_ffn_conv_w, ffn_conv_b=v_ffn_conv_b,
               ffn_w_down=v_ffn_w_down, final_g=v_final_g)

    xi, yi, ci = lax.axis_index("x"), lax.axis_index("y"), lax.axis_index("c")
    chip = 2 * xi + yi
    dev = 2 * chip + ci
    core_chip = jnp.stack([ci, chip]).astype(jnp.int32)

    xs = x[0]
    tgt = loss_target[0]
    S, D = xs.shape
    H = gm_ws.shape[1]
    W = H * CHUNK
    Fd = ffn_w_down.shape[1] * N_DEV
    lcw_cols = lru_conv_w.shape[2]
    fcw_cols = ffn_conv_w.shape[2]

    dev1 = jnp.reshape(dev, (1,)).astype(jnp.int32)

    def gather_start(shards, name, after=()):
        lands = [_place_own_call(s, dev1, "%s_own%d" % (name, k)) for k, s in enumerate(shards)]
        return _exchange_start(shards, lands, 4 * len(shards), _gather_stage1_copies(len(shards)), name + "_ici", after)

    def gather_forward(lands, name, after=()):
        return _exchange_start([], lands, 3 * len(lands), _gather_stage2_copies(len(lands)), name + "_d2d", after)

    def pair_start(g, name, after=()):
        return _exchange_start([g], [lax.empty((4,) + g.shape[1:], F32)], 4, _pair_copies(1), name, after)

    def chip_start(p16, name, after=()):
        return _exchange_start([p16], [lax.empty((3,) + p16.shape[1:], BF16)], 3, _chip_copies(1), name, after)

    vgm_g, vgm_b = gm_v_g, gm_v_b
    ws, wa, wx = gm_ws[0], lru_wa[0], lru_wx[0]
    bsb = jnp.broadcast_to(gm_bs[0][:, :, None], (H, CHUNK, CHUNK))
    ba, bx = lru_ba.reshape(1, W), lru_bx.reshape(1, W)
    fcb = ffn_conv_b
    fing = final_g.reshape(1, D)

    conv_pack = _pack([lru_conv_w[0], ffn_conv_w[0]], pad_to=8)
    ga1 = gather_start([w_in[0].astype(BF16), conv_pack], "gather_in")
    h1 = _rmsnorm_call(xs, norm1_g, "norm1", deps=(ga1.token,))
    _, la = _exchange_wait(ga1, after=(h1,))
    ga2 = gather_forward(la, "gather_in")
    gb1 = gather_start([w_out[0].astype(BF16)], "gather_out", after=(ga2.token,))
    gc1 = gather_start([ffn_w_up[0].astype(BF16)], "gather_up", after=(gb1.token,))
    _, (win_g, conv_g) = _exchange_wait(ga2, after=(gc1.token,))
    n_l = 4 * lcw_cols // LANES
    n_f = 3 * fcw_cols // LANES
    lcw = conv_g[:, :n_l].reshape(N_DEV, 4, lcw_cols).transpose(1, 0, 2).reshape(4, N_DEV * lcw_cols)
    fcw = conv_g[:, n_l:n_l + n_f].reshape(N_DEV, 3, fcw_cols).transpose(1, 0, 2).reshape(3, N_DEV * fcw_cols)

    p = _mm_blocked_call(h1, win_g, F32, False, "in_proj")
    win_rows = _unblock_call(win_g, "w_in_rows")
    _, lb = _exchange_wait(gb1, after=(p,))
    gb2 = gather_forward(lb, "gather_out")
    y_half = _gm_fwd_call(p, vgm_g, vgm_b, ws, bsb, gm_out_g, "gmlp_fwd", deps=(gb2.token,))
    y, hs = _lru_fwd_call(p, lcw, lru_conv_b, wa, ba, wx, bx, lru_lambda, lru_out_g, y_half, "lru_fwd")
    _, (wout_g,) = _exchange_wait(gb2, after=(y,))
    wout_full = wout_g.reshape(D, D)
    x2 = _mm_out_call(xs, y, wout_full, "out_proj")
    h2 = _rmsnorm_call(x2, norm2_g, "norm2")
    _, lc = _exchange_wait(gc1, after=(h2,))
    gc2 = gather_forward(lc, "gather_up")
    gd1 = gather_start([ffn_w_down[0].astype(BF16)], "gather_down", after=(gc2.token,))
    _, (wup_g,) = _exchange_wait(gc2, after=(gd1.token,))
    up3 = _mm_blocked_call(h2, wup_g, BF16, True, "ffn_up")
    f, upc3 = _ffn_act_call(up3, fcw, fcb, "ffn_act")
    _, ld = _exchange_wait(gd1, after=(f,))
    gd2 = gather_forward(ld, "gather_down")
    _, (wdown_g,) = _exchange_wait(gd2)
    wdown_full = wdown_g.reshape(Fd, D)
    dx3, dx3b, loss_acc, d_final = _mm_down_loss_call(x2, f, wdown_full, fing, tgt, "ffn_down_loss")

    g_wdown = _mm_tn_rows_call(f, dx3b, "ffn_down_dw").reshape((N_DEV,) + ffn_w_down.shape[1:])
    pd = pair_start(g_wdown, "pair_down")
    d_f = _mm_nt_call(dx3b, wdown_full, BF16, "ffn_down_dx", deps=(pd.token,))
    d_up3, dfcw_g, dfcw_v, dfcb_g, dfcb_v = _ffn_act_bwd_call(up3, upc3, d_f, fcw, "ffn_act_bwd")
    (g_wdown,), (r1,) = _exchange_wait(pd, after=(d_up3,))
    own_down, p16 = _pair_add_call(g_wdown, r1, core_chip, "pair_add_down")
    cd = chip_start(p16, "chip_down")
    g_wup = _mm_tn_cols_call(h2, d_up3, N_DEV, ffn_w_up.shape[2], "ffn_up_dw", deps=(cd.token,))
    pu = pair_start(g_wup, "pair_up")
    dx2, dx2b, d_norm2 = _mm_dx_norm_call(d_up3, wup_g, dx3, x2, norm2_g, "ffn_up_dx", deps=(pu.token,))
    g_wout = _mm_tn_rows_call(y, dx2b, "out_proj_dw").reshape((N_DEV,) + w_out.shape[1:])
    po = pair_start(g_wout, "pair_out")
    d_y = _mm_nt_call(dx2b, wout_full, F32, "out_proj_dx", deps=(po.token,))
    (g_wup,), (r1,) = _exchange_wait(pu, after=(d_y,))
    own_up, p16 = _pair_add_call(g_wup, r1, core_chip, "pair_add_up")
    _, (r2_down,) = _exchange_wait(cd, after=(p16,))
    cu = chip_start(p16, "chip_up", after=(r2_down,))
    dp_half, d_vg, d_vb, d_ws, d_bs, d_gog = _gm_bwd_call(p, d_y, vgm_g, vgm_b, ws, bsb, gm_out_g, "gmlp_bwd",
                                                          deps=(cu.token,))
    d_p2, d_lcw, d_lcb, d_wa, d_ba, d_wx, d_bx, d_lam, d_log = _lru_bwd_call(
        p, hs, d_y, lcw, lru_conv_b, wa, ba, wx, bx, lru_lambda, lru_out_g, dp_half, "lru_bwd")
    d_p = d_p2[None]
    (g_wout,), (r1,) = _exchange_wait(po, after=(d_p,))
    own_out, p16_out = _pair_add_call(g_wout, r1, core_chip, "pair_add_out")
    g_win = _mm_tn_cols_call(h1, d_p, N_DEV, w_in.shape[2], "in_proj_dw")
    pi = pair_start(g_win, "pair_in")
    _, (r2_up,) = _exchange_wait(cu, after=(g_win,))
    co = chip_start(p16_out, "chip_out", after=(r2_up,))
    grad_x, d_norm1 = _mm_nt_norm_call(d_p[0], win_rows, dx2, xs, norm1_g, "in_proj_dx", deps=(co.token, pi.token))
    small_g = dict(norm1_g=d_norm1, gm_v_g=d_vg, gm_v_b=d_vb, gm_ws=d_ws, gm_bs=d_bs[:, :, 0], lru_conv_b=d_lcb,
                   lru_wa=d_wa, lru_ba=d_ba, lru_wx=d_wx, lru_bx=d_bx, lru_lambda=d_lam, gm_out_g=d_gog,
                   lru_out_g=d_log, norm2_g=d_norm2,
                   ffn_conv_b=jnp.concatenate([dfcb_g, dfcb_v], axis=1), final_g=d_final)
    rep = _pack([small_g[n] for n in SMALL])
    conv_part = _pack([d_lcw, jnp.concatenate([dfcw_g, dfcw_v], axis=1)], pad_to=8)
    n_rep, n_conv = rep.shape[0], conv_part.shape[0]
    gs1 = gather_start([jnp.concatenate([rep, conv_part], axis=0)], "gather_small")

    (g_win,), (r1,) = _exchange_wait(pi, after=(grad_x,))
    own_in, p16 = _pair_add_call(g_win, r1, core_chip, "pair_add_in")
    _, (r2_out,) = _exchange_wait(co, after=(p16,))
    ci_ = chip_start(p16, "chip_in", after=(r2_out, gs1.token))

    def adamw_big(n, own, r2, deps=()):
        return _adamw_call(wts[n][0], mom[n][0], var[n][0], [(own, None), (r2, 0), (r2, 1), (r2, 2)], "adamw_" + n, deps)

    res = {}
    res["ffn_w_down"] = adamw_big("ffn_w_down", own_down, r2_down, (ci_.token,))
    res["ffn_w_up"] = adamw_big("ffn_w_up", own_up, r2_up, (ci_.token,))
    res["w_out"] = adamw_big("w_out", own_out, r2_out, (ci_.token,))
    _, ls = _exchange_wait(gs1, after=(res["w_out"][0], res["ffn_w_up"][0], res["ffn_w_down"][0]))
    gs2 = gather_forward(ls, "gather_small")
    _, (r2_in,) = _exchange_wait(ci_, after=(gs2.token,))
    res["w_in"] = adamw_big("w_in", own_in, r2_in)
    _, (parts,) = _exchange_wait(gs2, after=(res["w_in"][0],))
    g_rep, d_rep, m_rep, v_rep = _adamw_call(
        _pack([wts[n] for n in SMALL]), _pack([mom[n] for n in SMALL]), _pack([var[n] for n in SMALL]),
        [(parts, k) for k in range(N_DEV)], "adamw_small")
    shapes = [wts[n].shape for n in SMALL]
    for n, g_, d_, m_, v_ in zip(SMALL, _unpack(g_rep, shapes), _unpack(d_rep, shapes), _unpack(m_rep, shapes),
                                 _unpack(v_rep, shapes)):
        res[n] = (g_, d_, m_, v_)
    conv_sum = _sum_call(parts, n_rep, n_conv, "sum_conv_grads")
    g_lcw = conv_sum[:4 * W // LANES].reshape(4, W)
    g_fcw = conv_sum[4 * W // LANES:4 * W // LANES + 6 * Fd // LANES].reshape(3, 2 * Fd)
    for n, full in (("lru_conv_w", g_lcw), ("ffn_conv_w", g_fcw)):
        cols = wts[n].shape[2]
        mine = lax.dynamic_slice_in_dim(full, dev * cols, cols, axis=1)
        res[n] = _adamw_call(wts[n][0], mom[n][0], var[n][0], [(mine, None)], "adamw_" + n)

    loss = lax.psum(loss_acc[0, 0], ("x", "y", "c"))
    outs = [[], [], [], []]
    for n in WEIGHTS:
        for k in range(4):
            outs[k].append(res[n][k].reshape(wts[n].shape))
    return (loss, grad_x[None], *outs[0], *outs[1], *outs[2], *outs[3])
```

```python
import functools
import math

import jax
import jax.numpy as jnp
from jax import lax
from jax.experimental import pallas as pl
from jax.experimental.pallas import tpu as pltpu

F32 = jnp.float32
BF16 = jnp.bfloat16

RMS_EPS = 1e-6
LN_EPS = 1e-5
LRU_C = 8.0
CHUNK = 128
ADAM_LR = 0.001
ADAM_B1 = 0.9
ADAM_B2 = 0.999
ADAM_EPS = 1e-08
ADAM_WD = 0.01
ADAM_STEP = 10
N_DEV = 8
LANES = 128
MIB = 1024 * 1024

WEIGHTS = ['norm1_g', 'w_in', 'gm_v_g', 'gm_v_b', 'gm_ws', 'gm_bs', 'lru_conv_w', 'lru_conv_b', 'lru_wa', 'lru_ba',
           'lru_wx', 'lru_bx', 'lru_lambda', 'gm_out_g', 'lru_out_g', 'w_out', 'norm2_g', 'ffn_w_up', 'ffn_conv_w',
           'ffn_conv_b', 'ffn_w_down', 'final_g']
BIG = ['w_in', 'w_out', 'ffn_w_up', 'ffn_w_down']
CONV = ['lru_conv_w', 'ffn_conv_w']
SMALL = [n for n in WEIGHTS if n not in BIG and n not in CONV]

_DN_NT = (((1,), (1,)), ((), ()))
_DN_TN = (((0,), (0,)), ((), ()))
_GELU_C = 0.7978845608028654


def _cp(n_axes, vmem_mib=48):
    return pltpu.CompilerParams(dimension_semantics=("arbitrary",) * n_axes, vmem_limit_bytes=vmem_mib * MIB)


def _tile(n, pref, mult=8):
    t = min(pref, n)
    t -= t % mult
    while t >= mult:
        if n % t == 0:
            return t
        t -= mult
    return n


def _gelu(z):
    return 0.5 * z * (1.0 + jnp.tanh(_GELU_C * z * (1.0 + 0.044715 * z * z)))


def _gelu_parts(z):
    z2 = z * z
    t = jnp.tanh(_GELU_C * z * (1.0 + 0.044715 * z2))
    g = 0.5 * z * (1.0 + t)
    dg = 0.5 * (1.0 + t) + 0.5 * z * (1.0 - t * t) * (_GELU_C * (1.0 + 0.134145 * z2))
    return g, dg


def _sigmoid(z):
    return 0.5 + 0.5 * jnp.tanh(0.5 * z)


def _softplus(z):
    t = jnp.exp(-jnp.abs(z))
    u = 1.0 + t
    log1p = jnp.where(u == 1.0, t, jnp.log(u) * t / (u - 1.0))
    return jnp.maximum(z, 0.0) + log1p


def _neg_expm1(y):
    e = jnp.exp(y)
    small = jnp.where(e == 1.0, y, (e - 1.0) * y / jnp.log(e))
    return -jnp.where(y < -0.5, e - 1.0, small)


def _rows_mean(v):
    return jnp.mean(v, axis=-1, keepdims=True)


def _col_sum(v):
    return jnp.sum(v, axis=0, keepdims=True)


def _shift_down(prev8, cur, k):
    if k == 0:
        return cur
    z = jnp.concatenate([prev8, cur], axis=0)
    return pltpu.roll(z, k, 0)[8:]


def _shift_up(cur, next8, k):
    if k == 0:
        return cur
    n = cur.shape[0]
    z = jnp.concatenate([cur, next8], axis=0)
    return pltpu.roll(z, n + 8 - k, 0)[:n]


def _mesh_pos():
    return lax.axis_index("x"), lax.axis_index("y"), lax.axis_index("c")


def _any_specs(n):
    return [pl.BlockSpec(memory_space=pl.ANY)] * n


def _pallas(body, n_in, deps, **kw):
    nd = len(deps)
    if not nd:
        return pl.pallas_call(body, **kw)

    def ordered(*refs):
        body(*refs[:n_in], *refs[n_in + nd:])

    kw["in_specs"] = list(kw["in_specs"]) + _any_specs(nd)
    return pl.pallas_call(ordered, **kw)


_HBM = pl.BlockSpec(memory_space=pltpu.HBM)
_SEM = pl.BlockSpec(memory_space=pltpu.SEMAPHORE)
_EFFECT = pltpu.SideEffectType.DATAFLOW_SIDE_EFFECTING


class _InFlight:
    def __init__(self, sems, bufs, token, n_src, n_copies, make_copies, name):
        self.sems, self.bufs, self.token = sems, bufs, token
        self.n_src, self.n_copies, self.make_copies, self.name = n_src, n_copies, make_copies, name


def _exchange_start(srcs, lands, n_copies, make_copies, name, after=()):
    bufs = list(srcs) + list(lands)
    nb, na = len(bufs), len(after)
    ns = len(srcs)

    def body(*refs):
        b_refs = refs[:nb]
        outs = refs[nb + na:]
        send, recv = outs[:n_copies], outs[n_copies:2 * n_copies]
        token = outs[-1]
        for cp in make_copies(b_refs[:ns], b_refs[ns:], send, recv):
            cp.start()
        token[...] = jnp.zeros_like(token)

    out = pl.pallas_call(
        body, name=name,
        out_shape=[pltpu.SemaphoreType.DMA(())] * (2 * n_copies) + [pltpu.HBM(b.shape, b.dtype) for b in bufs]
        + [jax.ShapeDtypeStruct((8, LANES), F32)],
        in_specs=[_HBM] * nb + _any_specs(na),
        out_specs=[_SEM] * (2 * n_copies) + [_HBM] * nb + [pl.BlockSpec(memory_space=pltpu.VMEM)],
        input_output_aliases={i: 2 * n_copies + i for i in range(nb)},
        compiler_params=pltpu.CompilerParams(has_side_effects=_EFFECT),
    )(*[pltpu.with_memory_space_constraint(b, pltpu.HBM) for b in bufs], *after)
    return _InFlight(out[:2 * n_copies], out[2 * n_copies:2 * n_copies + nb], out[-1], ns, n_copies, make_copies, name)


def _exchange_wait(fl, after=()):
    nb, na, nc, ns = len(fl.bufs), len(after), fl.n_copies, fl.n_src

    def body(*refs):
        b_refs = refs[:nb]
        sems = refs[nb:nb + 2 * nc]
        copies = fl.make_copies(b_refs[:ns], b_refs[ns:], sems[:nc], sems[nc:])
        for cp in copies:
            cp.wait_send()
        for cp in copies:
            cp.wait_recv()

    out = pl.pallas_call(
        body, name=fl.name + "_wait",
        out_shape=[pltpu.HBM(b.shape, b.dtype) for b in fl.bufs],
        in_specs=[_HBM] * nb + [_SEM] * (2 * nc) + _any_specs(na),
        out_specs=[_HBM] * nb,
        input_output_aliases={i: i for i in range(nb)},
        compiler_params=pltpu.CompilerParams(has_side_effects=_EFFECT),
    )(*fl.bufs, *fl.sems, *after)
    return list(out[:ns]), list(out[ns:])


def _remote(src, dst, send_sem, recv_sem, to):
    return pltpu.make_async_remote_copy(src_ref=src, dst_ref=dst, send_sem=send_sem, recv_sem=recv_sem,
                                        device_id=to, device_id_type=pl.DeviceIdType.MESH)


def _gather_stage1_copies(n):
    def make(s_refs, l_refs, send, recv):
        x, y, c = _mesh_pos()
        own = 4 * x + 2 * y + c
        targets = [(x, y, 1 - c), (1 - x, y, c), (x, 1 - y, c), (1 - x, 1 - y, c)]
        return [_remote(s_refs[a], l_refs[a].at[own], send[4 * a + k], recv[4 * a + k], to)
                for a in range(n) for k, to in enumerate(targets)]
    return make


def _gather_stage2_copies(n):
    def make(s_refs, l_refs, send, recv):
        x, y, c = _mesh_pos()
        blocks = [4 * (1 - x) + 2 * y + c, 4 * x + 2 * (1 - y) + c, 4 * (1 - x) + 2 * (1 - y) + c]
        return [_remote(l_refs[a].at[b], l_refs[a].at[b], send[3 * a + j], recv[3 * a + j], (x, y, 1 - c))
                for a in range(n) for j, b in enumerate(blocks)]
    return make


def _pair_copies(n):
    def make(s_refs, l_refs, send, recv):
        x, y, c = _mesh_pos()
        return [_remote(s_refs[a].at[2 * k + 1 - c], l_refs[a].at[k], send[4 * a + k], recv[4 * a + k], (x, y, 1 - c))
                for a in range(n) for k in range(4)]
    return make


def _chip_copies(n):
    def make(s_refs, l_refs, send, recv):
        x, y, c = _mesh_pos()
        chips = [(1 - x, y), (x, 1 - y), (1 - x, 1 - y)]
        return [_remote(s_refs[a].at[2 * ch[0] + ch[1]], l_refs[a].at[j], send[3 * a + j], recv[3 * a + j], (*ch, c))
                for a in range(n) for j, ch in enumerate(chips)]
    return make


def _place_own_call(shard, dev, name):
    R, C = shard.shape
    tr = _tile(R, max(16, MIB // (C * shard.dtype.itemsize)), 16)

    def body(d_ref, s_ref, o_ref):
        o_ref[...] = s_ref[...]

    grid_spec = pltpu.PrefetchScalarGridSpec(
        num_scalar_prefetch=1, grid=(R // tr,),
        in_specs=[pl.BlockSpec((tr, C), lambda r, d: (r, 0))],
        out_specs=pl.BlockSpec((None, tr, C), lambda r, d: (d[0], r, 0)))
    return pl.pallas_call(body, name=name, grid_spec=grid_spec,
                          out_shape=jax.ShapeDtypeStruct((N_DEV, R, C), shard.dtype), compiler_params=_cp(1))(dev, shard)


def _all_gather_call(shards, name):
    n = len(shards)

    def body(*refs):
        x_refs, o_refs = refs[:n], refs[n:2 * n]
        send_sems, recv_sems, local_sems = refs[2 * n:]
        x, y, c = _mesh_pos()
        me, sib = (x, y, c), (x, y, 1 - c)
        chips = [(1 - x, y), (x, 1 - y), (1 - x, 1 - y)]

        def copy(a, k, block, to, src=None):
            dst = o_refs[a].at[4 * block[0] + 2 * block[1] + block[2]]
            return pltpu.make_async_remote_copy(
                src_ref=dst if src is None else src, dst_ref=dst,
                send_sem=send_sems.at[a, k], recv_sem=recv_sems.at[a, k],
                device_id=to, device_id_type=pl.DeviceIdType.MESH)

        mine = [pltpu.make_async_copy(x_refs[a], o_refs[a].at[4 * x + 2 * y + c], local_sems.at[a]) for a in range(n)]
        for cp in mine:
            cp.start()
        first = []
        for a in range(n):
            first.append(copy(a, 0, me, sib, src=x_refs[a]))
            for j, chip in enumerate(chips):
                first.append(copy(a, 1 + j, me, (*chip, c), src=x_refs[a]))
        for cp in first:
            cp.start()
        passed = []
        for a in range(n):
            for j, chip in enumerate(chips):
                copy(a, 1 + j, (*chip, c), me).wait_recv()
                fwd = copy(a, 4 + j, (*chip, c), sib)
                fwd.start()
                passed.append(fwd)
        for a in range(n):
            copy(a, 0, sib, me).wait_recv()
            for j, chip in enumerate(chips):
                copy(a, 4 + j, (*chip, 1 - c), me).wait_recv()
        for cp in first + passed:
            cp.wait_send()
        for cp in mine:
            cp.wait()

    return pl.pallas_call(
        body, name=name,
        out_shape=[jax.ShapeDtypeStruct((N_DEV,) + s.shape, s.dtype) for s in shards],
        in_specs=_any_specs(n), out_specs=_any_specs(n),
        scratch_shapes=[pltpu.SemaphoreType.DMA((n, 7)), pltpu.SemaphoreType.DMA((n, 7)), pltpu.SemaphoreType.DMA((n,))],
    )(*shards)


def _pair_exchange_call(grads, name):
    n = len(grads)

    def body(*refs):
        g_refs, r_refs = refs[:n], refs[n:2 * n]
        send_sems, recv_sems = refs[2 * n:]
        x, y, c = _mesh_pos()
        copies = []
        for a in range(n):
            for k in range(4):
                copies.append(pltpu.make_async_remote_copy(
                    src_ref=g_refs[a].at[2 * k + 1 - c], dst_ref=r_refs[a].at[k],
                    send_sem=send_sems.at[a, k], recv_sem=recv_sems.at[a, k],
                    device_id=(x, y, 1 - c), device_id_type=pl.DeviceIdType.MESH))
        for cp in copies:
            cp.start()
        for cp in copies:
            cp.wait_recv()
        for cp in copies:
            cp.wait_send()

    return pl.pallas_call(
        body, name=name,
        out_shape=[jax.ShapeDtypeStruct((4,) + g.shape[1:], g.dtype) for g in grads],
        in_specs=_any_specs(n), out_specs=_any_specs(n),
        scratch_shapes=[pltpu.SemaphoreType.DMA((n, 4)), pltpu.SemaphoreType.DMA((n, 4))],
    )(*grads)


def _chip_exchange_call(parts, name):
    n = len(parts)

    def body(*refs):
        p_refs, r_refs = refs[:n], refs[n:2 * n]
        send_sems, recv_sems = refs[2 * n:]
        x, y, c = _mesh_pos()
        chips = [(1 - x, y), (x, 1 - y), (1 - x, 1 - y)]
        copies = []
        for a in range(n):
            for j, chip in enumerate(chips):
                copies.append(pltpu.make_async_remote_copy(
                    src_ref=p_refs[a].at[2 * chip[0] + chip[1]], dst_ref=r_refs[a].at[j],
                    send_sem=send_sems.at[a, j], recv_sem=recv_sems.at[a, j],
                    device_id=(*chip, c), device_id_type=pl.DeviceIdType.MESH))
        for cp in copies:
            cp.start()
        for cp in copies:
            cp.wait_recv()
        for cp in copies:
            cp.wait_send()

    return pl.pallas_call(
        body, name=name,
        out_shape=[jax.ShapeDtypeStruct((3,) + p.shape[1:], p.dtype) for p in parts],
        in_specs=_any_specs(n), out_specs=_any_specs(n),
        scratch_shapes=[pltpu.SemaphoreType.DMA((n, 3)), pltpu.SemaphoreType.DMA((n, 3))],
    )(*parts)


def _pair_add_call(g, r1, core_chip, name):
    _, R, C = g.shape
    tr = _tile(R, max(16, (2 * MIB) // (C * 4)), 16)

    def body(cc_ref, g_ref, r_ref, p32_ref, p16_ref):
        s = g_ref[...] + r_ref[...]
        p16_ref[...] = s.astype(BF16)

        @pl.when(pl.program_id(1) == cc_ref[1])
        def _():
            p32_ref[...] = s

    grid_spec = pltpu.PrefetchScalarGridSpec(
        num_scalar_prefetch=1, grid=(R // tr, 4),
        in_specs=[pl.BlockSpec((None, tr, C), lambda r, k, cc: (2 * k + cc[0], r, 0)),
                  pl.BlockSpec((None, tr, C), lambda r, k, cc: (k, r, 0))],
        out_specs=[pl.BlockSpec((tr, C), lambda r, k, cc: (r, 0)),
                   pl.BlockSpec((None, tr, C), lambda r, k, cc: (k, r, 0))])
    return pl.pallas_call(
        body, name=name, grid_spec=grid_spec,
        out_shape=[jax.ShapeDtypeStruct((R, C), F32), jax.ShapeDtypeStruct((4, R, C), BF16)],
        compiler_params=_cp(2))(core_chip, g, r1)


def _adamw_call(w, m, v, addends, name, deps=()):
    R, C = w.shape
    tr = _tile(R, max(8, (MIB // 2) // (C * 4)), 16)
    na = len(addends)
    c1 = 1.0 - ADAM_B1 ** ADAM_STEP
    c2 = 1.0 - ADAM_B2 ** ADAM_STEP

    def body(*refs):
        w_ref, m_ref, v_ref = refs[:3]
        a_refs = refs[3:3 + na]
        g_ref, d_ref, nm_ref, nv_ref = refs[3 + na:]
        g = a_refs[0][...].astype(F32)
        for a_ref in a_refs[1:]:
            g = g + a_ref[...].astype(F32)
        nm = ADAM_B1 * m_ref[...] + (1.0 - ADAM_B1) * g
        nv = ADAM_B2 * v_ref[...] + (1.0 - ADAM_B2) * (g * g)
        g_ref[...] = g
        nm_ref[...] = nm
        nv_ref[...] = nv
        d_ref[...] = -ADAM_LR * ((nm / c1) / (jnp.sqrt(nv / c2) + ADAM_EPS) + ADAM_WD * w_ref[...])

    flat = pl.BlockSpec((tr, C), lambda r: (r, 0))
    a_specs = [flat if k is None else pl.BlockSpec((None, tr, C), functools.partial(lambda r, kk: (kk, r, 0), kk=k))
               for _, k in addends]
    out = jax.ShapeDtypeStruct((R, C), F32)
    return _pallas(
        body, 3 + na, deps, name=name, grid=(R // tr,),
        in_specs=[flat, flat, flat] + a_specs, out_specs=[flat] * 4, out_shape=[out] * 4,
        compiler_params=_cp(1))(w, m, v, *[a for a, _ in addends], *deps)


def _sum_call(parts, row0, rows, name):
    n = parts.shape[0]
    tr = _tile(math.gcd(row0, rows), 256, 8)
    b0 = row0 // tr

    def body(p_ref, o_ref):
        s = p_ref[0]
        for k in range(1, n):
            s = s + p_ref[k]
        o_ref[...] = s

    return pl.pallas_call(
        body, name=name, grid=(rows // tr,),
        in_specs=[pl.BlockSpec((n, tr, LANES), lambda r: (0, r + b0, 0))],
        out_specs=pl.BlockSpec((tr, LANES), lambda r: (r, 0)),
        out_shape=jax.ShapeDtypeStruct((rows, LANES), F32), compiler_params=_cp(1))(parts)


def _rmsnorm_call(x, g, name, deps=()):
    S, D = x.shape
    tm = _tile(S, 512, 16)

    def body(x_ref, g_ref, o_ref):
        xv = x_ref[...]
        r = lax.rsqrt(_rows_mean(xv * xv) + RMS_EPS)
        o_ref[...] = (xv * r * g_ref[...]).astype(BF16)

    return _pallas(
        body, 2, deps, name=name, grid=(S // tm,),
        in_specs=[pl.BlockSpec((tm, D), lambda i: (i, 0)), pl.BlockSpec((1, D), lambda i: (0, 0))],
        out_specs=pl.BlockSpec((tm, D), lambda i: (i, 0)),
        out_shape=jax.ShapeDtypeStruct((S, D), BF16), compiler_params=_cp(1))(x, g, *deps)


def _mm_blocked_call(a, wg, out_dtype, halves, name, deps=()):
    S, K = a.shape
    nb, _, bn = wg.shape
    tm = _tile(S, 1024, 16)
    tn = _tile(bn, 768, LANES)
    nsub = bn // tn
    J = nb * nsub

    def body(a_ref, w_ref, o_ref):
        o_ref[...] = jnp.dot(a_ref[...], w_ref[...], preferred_element_type=F32).astype(out_dtype)

    if halves:
        nh = J // 2
        out_spec = pl.BlockSpec((None, tm, tn), lambda i, j: (j // nh, i, j % nh))
        out_shape = jax.ShapeDtypeStruct((2, S, nb * bn // 2), out_dtype)
    else:
        out_spec = pl.BlockSpec((tm, tn), lambda i, j: (i, j))
        out_shape = jax.ShapeDtypeStruct((S, nb * bn), out_dtype)
    return _pallas(
        body, 2, deps, name=name, grid=(S // tm, J),
        in_specs=[pl.BlockSpec((tm, K), lambda i, j: (i, 0)),
                  pl.BlockSpec((None, K, tn), lambda i, j: (j // nsub, 0, j % nsub))],
        out_specs=out_spec, out_shape=out_shape, compiler_params=_cp(2))(a, wg, *deps)


def _mm_out_call(x, y, w, name):
    S, D = x.shape
    tm = _tile(S, 512, 16)

    def body(x_ref, y_ref, w_ref, o_ref):
        o_ref[...] = x_ref[...] + jnp.dot(y_ref[...], w_ref[...], preferred_element_type=F32)

    return pl.pallas_call(
        body, name=name, grid=(S // tm,),
        in_specs=[pl.BlockSpec((tm, D), lambda i: (i, 0)), pl.BlockSpec((tm, D), lambda i: (i, 0)),
                  pl.BlockSpec((D, D), lambda i: (0, 0))],
        out_specs=pl.BlockSpec((tm, D), lambda i: (i, 0)),
        out_shape=jax.ShapeDtypeStruct((S, D), F32), compiler_params=_cp(1))(x, y, w)


def _mm_nt_call(a, w, out_dtype, name, deps=()):
    S, K = a.shape
    N = w.shape[0]
    tm = _tile(S, 1024, 16)
    tn = _tile(N, 768, LANES)

    def body(a_ref, w_ref, o_ref):
        o_ref[...] = lax.dot_general(a_ref[...], w_ref[...], _DN_NT, preferred_element_type=F32).astype(out_dtype)

    return _pallas(
        body, 2, deps, name=name, grid=(S // tm, N // tn),
        in_specs=[pl.BlockSpec((tm, K), lambda i, j: (i, 0)), pl.BlockSpec((tn, K), lambda i, j: (j, 0))],
        out_specs=pl.BlockSpec((tm, tn), lambda i, j: (i, j)),
        out_shape=jax.ShapeDtypeStruct((S, N), out_dtype), compiler_params=_cp(2))(a, w, *deps)


def _mm_down_loss_call(x2, f, w, final_g, target, name):
    S, D = x2.shape
    Fd = f.shape[1]
    tm = _tile(S, 512, 16)
    tk = _tile(Fd, 768, LANES)
    nk = Fd // tk

    def body(x_ref, f_ref, w_ref, g_ref, t_ref, dx_ref, dxb_ref, loss_ref, dg_ref, acc):
        i, k = pl.program_id(0), pl.program_id(1)

        @pl.when(jnp.logical_and(i == 0, k == 0))
        def _():
            loss_ref[...] = jnp.zeros_like(loss_ref)
            dg_ref[...] = jnp.zeros_like(dg_ref)

        @pl.when(k == 0)
        def _():
            acc[...] = jnp.zeros_like(acc)

        acc[...] += jnp.dot(f_ref[...], w_ref[...], preferred_element_type=F32)

        @pl.when(k == nk - 1)
        def _():
            x3 = x_ref[...] + acc[...]
            r = lax.rsqrt(_rows_mean(x3 * x3) + RMS_EPS)
            g = g_ref[...]
            xn = x3 * r
            diff = xn * g - t_ref[...]
            loss_ref[...] += 0.5 * jnp.sum(_rows_mean(diff * diff))
            dout = diff * (1.0 / D)
            dg_ref[...] += _col_sum(dout * xn)
            dyg = dout * g
            dx = r * (dyg - xn * _rows_mean(dyg * xn))
            dx_ref[...] = dx
            dxb_ref[...] = dx.astype(BF16)

    row = lambda i, k: (i, 0)
    return pl.pallas_call(
        body, name=name, grid=(S // tm, nk),
        in_specs=[pl.BlockSpec((tm, D), row), pl.BlockSpec((tm, tk), lambda i, k: (i, k)),
                  pl.BlockSpec((tk, D), lambda i, k: (k, 0)), pl.BlockSpec((1, D), lambda i, k: (0, 0)),
                  pl.BlockSpec((tm, D), row)],
        out_specs=[pl.BlockSpec((tm, D), row), pl.BlockSpec((tm, D), row),
                   pl.BlockSpec((8, LANES), lambda i, k: (0, 0)), pl.BlockSpec((1, D), lambda i, k: (0, 0))],
        out_shape=[jax.ShapeDtypeStruct((S, D), F32), jax.ShapeDtypeStruct((S, D), BF16),
                   jax.ShapeDtypeStruct((8, LANES), F32), jax.ShapeDtypeStruct((1, D), F32)],
        scratch_shapes=[pltpu.VMEM((tm, D), F32)], compiler_params=_cp(2, 56))(x2, f, w, final_g, target)


def _mm_dx_norm_call(a3, wg, resid, xin, g, name, deps=()):
    na, S, Fa = a3.shape
    nb, D, bn = wg.shape
    tm = _tile(S, 512, 16)
    tk = _tile(bn, 1536, LANES)
    nsub = bn // tk
    nka = Fa // tk
    nk = nb * nsub
    assert na * nka == nk

    def body(a_ref, w_ref, r_ref, x_ref, g_ref, dx_ref, dxb_ref, dg_ref, acc):
        i, k = pl.program_id(0), pl.program_id(1)

        @pl.when(jnp.logical_and(i == 0, k == 0))
        def _():
            dg_ref[...] = jnp.zeros_like(dg_ref)

        @pl.when(k == 0)
        def _():
            acc[...] = jnp.zeros_like(acc)

        acc[...] += lax.dot_general(a_ref[...], w_ref[...], _DN_NT, preferred_element_type=F32)

        @pl.when(k == nk - 1)
        def _():
            dh = acc[...]
            xv = x_ref[...]
            r = lax.rsqrt(_rows_mean(xv * xv) + RMS_EPS)
            xn = xv * r
            dg_ref[...] += _col_sum(dh * xn)
            dyg = dh * g_ref[...]
            dx = r_ref[...] + r * (dyg - xn * _rows_mean(dyg * xn))
            dx_ref[...] = dx
            dxb_ref[...] = dx.astype(BF16)

    row = lambda i, k: (i, 0)
    return _pallas(
        body, 5, deps, name=name, grid=(S // tm, nk),
        in_specs=[pl.BlockSpec((None, tm, tk), lambda i, k: (k // nka, i, k % nka)),
                  pl.BlockSpec((None, D, tk), lambda i, k: (k // nsub, 0, k % nsub)),
                  pl.BlockSpec((tm, D), row, pipeline_mode=pl.Buffered(1)),
                  pl.BlockSpec((tm, D), row, pipeline_mode=pl.Buffered(1)), pl.BlockSpec((1, D), lambda i, k: (0, 0))],
        out_specs=[pl.BlockSpec((tm, D), row), pl.BlockSpec((tm, D), row), pl.BlockSpec((1, D), lambda i, k: (0, 0))],
        out_shape=[jax.ShapeDtypeStruct((S, D), F32), jax.ShapeDtypeStruct((S, D), BF16),
                   jax.ShapeDtypeStruct((1, D), F32)],
        scratch_shapes=[pltpu.VMEM((tm, D), F32)], compiler_params=_cp(2, 56))(a3, wg, resid, xin, g, *deps)


def _unblock_call(wg, name):
    nb, K, bn = wg.shape

    def body(w_ref, o_ref):
        o_ref[...] = w_ref[...]

    return pl.pallas_call(
        body, name=name, grid=(nb,),
        in_specs=[pl.BlockSpec((None, K, bn), lambda o: (o, 0, 0))],
        out_specs=pl.BlockSpec((K, bn), lambda o: (0, o)),
        out_shape=jax.ShapeDtypeStruct((K, nb * bn), wg.dtype), compiler_params=_cp(1))(wg)


def _mm_nt_norm_call(a, w, resid, xin, g, name, deps=()):
    S, K = a.shape
    D = w.shape[0]
    tm = _tile(S, 256, 16)

    def body(a_ref, w_ref, r_ref, x_ref, g_ref, dx_ref, dg_ref):
        @pl.when(pl.program_id(0) == 0)
        def _():
            dg_ref[...] = jnp.zeros_like(dg_ref)

        dh = lax.dot_general(a_ref[...], w_ref[...], _DN_NT, preferred_element_type=F32)
        xv = x_ref[...]
        r = lax.rsqrt(_rows_mean(xv * xv) + RMS_EPS)
        xn = xv * r
        dg_ref[...] += _col_sum(dh * xn)
        dyg = dh * g_ref[...]
        dx_ref[...] = r_ref[...] + r * (dyg - xn * _rows_mean(dyg * xn))

    row = lambda i: (i, 0)
    fixed = lambda i: (0, 0)
    return _pallas(
        body, 5, deps, name=name, grid=(S // tm,),
        in_specs=[pl.BlockSpec((tm, K), row), pl.BlockSpec((D, K), fixed, pipeline_mode=pl.Buffered(1)),
                  pl.BlockSpec((tm, D), row), pl.BlockSpec((tm, D), row), pl.BlockSpec((1, D), fixed)],
        out_specs=[pl.BlockSpec((tm, D), row), pl.BlockSpec((1, D), fixed)],
        out_shape=[jax.ShapeDtypeStruct((S, D), F32), jax.ShapeDtypeStruct((1, D), F32)],
        compiler_params=_cp(1, 56))(a, w, resid, xin, g, *deps)


def _mm_tn_cols_call(a, b3, nb, bn, name, deps=()):
    S, Ka = a.shape
    nh, _, Fb = b3.shape
    tm = _tile(S, 2048, 16)
    tn = _tile(bn, 768, LANES)
    nsub = bn // tn
    njb = Fb // tn
    J = nb * nsub
    assert nh * njb == J

    def body(a_ref, b_ref, o_ref):
        @pl.when(pl.program_id(1) == 0)
        def _():
            o_ref[...] = jnp.zeros_like(o_ref)

        o_ref[...] += lax.dot_general(a_ref[...], b_ref[...], _DN_TN, preferred_element_type=F32)

    return _pallas(
        body, 2, deps, name=name, grid=(J, S // tm),
        in_specs=[pl.BlockSpec((tm, Ka), lambda j, i: (i, 0)),
                  pl.BlockSpec((None, tm, tn), lambda j, i: (j // njb, i, j % njb))],
        out_specs=pl.BlockSpec((None, Ka, tn), lambda j, i: (j // nsub, 0, j % nsub)),
        out_shape=jax.ShapeDtypeStruct((nb, Ka, bn), F32), compiler_params=_cp(2, 56))(a, b3, *deps)


def _mm_tn_rows_call(a, b, name, deps=()):
    S, E = a.shape
    D = b.shape[1]
    tm = _tile(S, 2048, 16)
    te = _tile(E, 768, LANES)

    def body(a_ref, b_ref, o_ref):
        @pl.when(pl.program_id(1) == 0)
        def _():
            o_ref[...] = jnp.zeros_like(o_ref)

        o_ref[...] += lax.dot_general(a_ref[...], b_ref[...], _DN_TN, preferred_element_type=F32)

    return _pallas(
        body, 2, deps, name=name, grid=(E // te, S // tm),
        in_specs=[pl.BlockSpec((tm, te), lambda j, i: (i, j)), pl.BlockSpec((tm, D), lambda j, i: (i, 0))],
        out_specs=pl.BlockSpec((te, D), lambda j, i: (j, 0)),
        out_shape=jax.ShapeDtypeStruct((E, D), F32), compiler_params=_cp(2, 56))(a, b, *deps)


def _ffn_tiles(S, Fd):
    return _tile(S, 512, 16), _tile(Fd, 768, LANES)


def _taps(cw_ref):
    return [cw_ref[k:k + 1, :] for k in range(cw_ref.shape[0])]


def _conv3(prev8, cur, taps):
    s1 = _shift_down(prev8, cur, 1)
    s2 = _shift_down(prev8, cur, 2)
    return taps[2] * cur + taps[1] * s1 + taps[0] * s2, s1, s2


def _ffn_act_call(up3, cw, cb, name):
    _, S, Fd = up3.shape
    tm, tc = _ffn_tiles(S, Fd)
    nj = Fd // tc
    hb = tm // 16

    def body(g_ref, v_ref, gp_ref, vp_ref, cwg_ref, cwv_ref, cbg_ref, cbv_ref, o_ref, c_ref):
        first = pl.program_id(0) == 0
        keep = jnp.where(first, 0.0, 1.0)
        gp = gp_ref[...].astype(F32)[8:] * keep
        vp = vp_ref[...].astype(F32)[8:] * keep
        cg, _, _ = _conv3(gp, g_ref[...].astype(F32), _taps(cwg_ref))
        cv, _, _ = _conv3(vp, v_ref[...].astype(F32), _taps(cwv_ref))
        cg = cg + cbg_ref[...]
        cv = cv + cbv_ref[...]
        o_ref[...] = (_gelu(cg) * cv).astype(BF16)
        c_ref[0] = cg.astype(BF16)
        c_ref[1] = cv.astype(BF16)

    prev = lambda i, j: (jnp.maximum(i * hb - 1, 0), j)
    return pl.pallas_call(
        body, name=name, grid=(S // tm, nj),
        in_specs=[pl.BlockSpec((None, tm, tc), lambda i, j: (0, i, j)), pl.BlockSpec((None, tm, tc), lambda i, j: (1, i, j)),
                  pl.BlockSpec((None, 16, tc), lambda i, j: (0,) + prev(i, j)),
                  pl.BlockSpec((None, 16, tc), lambda i, j: (1,) + prev(i, j)),
                  pl.BlockSpec((3, tc), lambda i, j: (0, j)), pl.BlockSpec((3, tc), lambda i, j: (0, j + nj)),
                  pl.BlockSpec((1, tc), lambda i, j: (0, j)), pl.BlockSpec((1, tc), lambda i, j: (0, j + nj))],
        out_specs=[pl.BlockSpec((tm, tc), lambda i, j: (i, j)), pl.BlockSpec((2, tm, tc), lambda i, j: (0, i, j))],
        out_shape=[jax.ShapeDtypeStruct((S, Fd), BF16), jax.ShapeDtypeStruct((2, S, Fd), BF16)],
        compiler_params=_cp(2))(up3, up3, up3, up3, cw, cw, cb, cb)


def _ffn_act_bwd_call(up3, upc3, d_f, cw, name):
    _, S, Fd = up3.shape
    tm, tc = _ffn_tiles(S, Fd)
    nj = Fd // tc
    ni = S // tm
    hb = tm // 16

    def body(g_ref, v_ref, cg_ref, cv_ref, cgn_ref, cvn_ref, df_ref, dfn_ref, cwg_ref, cwv_ref,
             dup_ref, dcwg_ref, dcwv_ref, dcbg_ref, dcbv_ref):
        i = pl.program_id(1)
        keep_next = jnp.where(i == ni - 1, 0.0, 1.0)

        @pl.when(i == 0)
        def _():
            for r in (dcwg_ref, dcwv_ref, dcbg_ref, dcbv_ref):
                r[...] = jnp.zeros_like(r)

        df = jnp.concatenate([df_ref[...].astype(F32), dfn_ref[...].astype(F32)[:8] * keep_next], axis=0)
        cg = jnp.concatenate([cg_ref[...].astype(F32), cgn_ref[...].astype(F32)[:8]], axis=0)
        cv = jnp.concatenate([cv_ref[...].astype(F32), cvn_ref[...].astype(F32)[:8]], axis=0)
        gel, dgel = _gelu_parts(cg)

        def back(d, cw_ref, x_ref, dcw_ref, dcb_ref, slab):
            taps = _taps(cw_ref)
            d0 = d[:tm]
            d1 = pltpu.roll(d, tm + 8 - 1, 0)[:tm]
            d2 = pltpu.roll(d, tm + 8 - 2, 0)[:tm]
            dup_ref[slab] = (taps[2] * d0 + taps[1] * d1 + taps[0] * d2).astype(BF16)
            xv = x_ref[...].astype(F32)
            dcw_ref[2:3, :] += _col_sum(xv * d0)
            dcw_ref[1:2, :] += _col_sum(xv * d1)
            dcw_ref[0:1, :] += _col_sum(xv * d2)
            dcb_ref[...] += _col_sum(d0)

        back(df * cv * dgel, cwg_ref, g_ref, dcwg_ref, dcbg_ref, 0)
        back(df * gel, cwv_ref, v_ref, dcwv_ref, dcbv_ref, 1)

    nxt = lambda j, i: (jnp.minimum((i + 1) * hb, S // 16 - 1), j)
    main = lambda s: pl.BlockSpec((None, tm, tc), lambda j, i: (s, i, j))
    halo = lambda s: pl.BlockSpec((None, 16, tc), lambda j, i: (s,) + nxt(j, i))
    acc3 = pl.BlockSpec((3, tc), lambda j, i: (0, j))
    acc1 = pl.BlockSpec((1, tc), lambda j, i: (0, j))
    return pl.pallas_call(
        body, name=name, grid=(nj, ni),
        in_specs=[main(0), main(1), main(0), main(1), halo(0), halo(1),
                  pl.BlockSpec((tm, tc), lambda j, i: (i, j)), pl.BlockSpec((16, tc), nxt),
                  pl.BlockSpec((3, tc), lambda j, i: (0, j)), pl.BlockSpec((3, tc), lambda j, i: (0, j + nj))],
        out_specs=[pl.BlockSpec((2, tm, tc), lambda j, i: (0, i, j)), acc3, acc3, acc1, acc1],
        out_shape=[jax.ShapeDtypeStruct((2, S, Fd), BF16), jax.ShapeDtypeStruct((3, Fd), F32),
                   jax.ShapeDtypeStruct((3, Fd), F32), jax.ShapeDtypeStruct((1, Fd), F32),
                   jax.ShapeDtypeStruct((1, Fd), F32)],
        compiler_params=_cp(2, 56))(up3, up3, upc3, upc3, upc3, upc3, d_f, d_f, cw, cw)


SUB_LANES = 256


def _ffn_up_act_call(h2, wg, cw, cb, name, deps=()):
    S, D = h2.shape
    nb, _, bn = wg.shape
    Fd = nb * bn // 2
    tm, tc = _ffn_tiles(S, Fd)
    nk = Fd // tc
    hb = tm // 16
    nsubw = bn // tc
    half = nb // 2
    sc = _tile(tc, SUB_LANES, LANES)

    def body(a_ref, ap_ref, wgate_ref, wval_ref, cwg_ref, cwv_ref, cbg_ref, cbv_ref, up_ref, upc_ref, f_ref):
        keep = jnp.where(pl.program_id(0) == 0, 0.0, 1.0)
        a_ext = jnp.concatenate([ap_ref[...], a_ref[...]], axis=0)
        for s in range(tc // sc):
            ls = slice(s * sc, (s + 1) * sc)

            def conv_half(w_ref, cw_ref, cb_ref, slab):
                ub = jnp.dot(a_ext, w_ref[:, ls], preferred_element_type=F32).astype(BF16)
                up_ref[slab, :, ls] = ub[16:]
                u = ub.astype(F32)
                conv, _, _ = _conv3(u[8:16] * keep, u[16:], _lane_taps(cw_ref, ls))
                c = conv + cb_ref[:, ls]
                upc_ref[slab, :, ls] = c.astype(BF16)
                return c

            cg = conv_half(wgate_ref, cwg_ref, cbg_ref, 0)
            cv = conv_half(wval_ref, cwv_ref, cbv_ref, 1)
            f_ref[:, ls] = (_gelu(cg) * cv).astype(BF16)

    return _pallas(
        body, 8, deps, name=name, grid=(S // tm, nk),
        in_specs=[pl.BlockSpec((tm, D), lambda i, k: (i, 0)),
                  pl.BlockSpec((16, D), lambda i, k: (jnp.maximum(i * hb - 1, 0), 0)),
                  pl.BlockSpec((None, D, tc), lambda i, k: (k // nsubw, 0, k % nsubw)),
                  pl.BlockSpec((None, D, tc), lambda i, k: (half + k // nsubw, 0, k % nsubw)),
                  pl.BlockSpec((3, tc), lambda i, k: (0, k)), pl.BlockSpec((3, tc), lambda i, k: (0, k + nk)),
                  pl.BlockSpec((1, tc), lambda i, k: (0, k)), pl.BlockSpec((1, tc), lambda i, k: (0, k + nk))],
        out_specs=[pl.BlockSpec((2, tm, tc), lambda i, k: (0, i, k)), pl.BlockSpec((2, tm, tc), lambda i, k: (0, i, k)),
                   pl.BlockSpec((tm, tc), lambda i, k: (i, k))],
        out_shape=[jax.ShapeDtypeStruct((2, S, Fd), BF16), jax.ShapeDtypeStruct((2, S, Fd), BF16),
                   jax.ShapeDtypeStruct((S, Fd), BF16)],
        compiler_params=_cp(2, 56))(h2, h2, wg, wg, cw, cw, cb, cb, *deps)


def _ffn_down_dx_act_bwd_call(dxb, w, up3, upc3, cw, name, deps=()):
    S, D = dxb.shape
    _, _, Fd = up3.shape
    tm, tc = _ffn_tiles(S, Fd)
    nj = Fd // tc
    ni = S // tm
    hb = tm // 16
    sc = _tile(tc, SUB_LANES, LANES)

    def body(a_ref, an_ref, w_ref, g_ref, v_ref, cg_ref, cv_ref, cgn_ref, cvn_ref, cwg_ref, cwv_ref,
             dup_ref, dcwg_ref, dcwv_ref, dcbg_ref, dcbv_ref):
        i = pl.program_id(1)
        keep_next = jnp.where(i == ni - 1, 0.0, 1.0)

        @pl.when(i == 0)
        def _():
            for r in (dcwg_ref, dcwv_ref, dcbg_ref, dcbv_ref):
                r[...] = jnp.zeros_like(r)

        a_ext = jnp.concatenate([a_ref[...], an_ref[...]], axis=0)
        for s in range(tc // sc):
            ls = slice(s * sc, (s + 1) * sc)
            df_ext = lax.dot_general(a_ext, w_ref[s * sc:(s + 1) * sc, :], _DN_NT, preferred_element_type=F32)
            df = jnp.concatenate([df_ext[:tm], df_ext[tm:tm + 8] * keep_next], axis=0)
            cg = jnp.concatenate([cg_ref[:, ls].astype(F32), cgn_ref[:, ls].astype(F32)[:8]], axis=0)
            cv = jnp.concatenate([cv_ref[:, ls].astype(F32), cvn_ref[:, ls].astype(F32)[:8]], axis=0)
            gel, dgel = _gelu_parts(cg)

            def back(d, cw_ref, x_ref, dcw_ref, dcb_ref, slab):
                taps = _lane_taps(cw_ref, ls)
                d0 = d[:tm]
                d1 = pltpu.roll(d, tm + 8 - 1, 0)[:tm]
                d2 = pltpu.roll(d, tm + 8 - 2, 0)[:tm]
                dup_ref[slab, :, ls] = (taps[2] * d0 + taps[1] * d1 + taps[0] * d2).astype(BF16)
                xv = x_ref[:, ls].astype(F32)
                dcw_ref[2:3, ls] += _col_sum(xv * d0)
                dcw_ref[1:2, ls] += _col_sum(xv * d1)
                dcw_ref[0:1, ls] += _col_sum(xv * d2)
                dcb_ref[:, ls] += _col_sum(d0)

            back(df * cv * dgel, cwg_ref, g_ref, dcwg_ref, dcbg_ref, 0)
            back(df * gel, cwv_ref, v_ref, dcwv_ref, dcbv_ref, 1)

    nxt = lambda j, i: jnp.minimum((i + 1) * hb, S // 16 - 1)
    main = lambda s: pl.BlockSpec((None, tm, tc), lambda j, i: (s, i, j))
    halo = lambda s: pl.BlockSpec((None, 16, tc), lambda j, i: (s, nxt(j, i), j))
    acc3 = pl.BlockSpec((3, tc), lambda j, i: (0, j))
    acc1 = pl.BlockSpec((1, tc), lambda j, i: (0, j))
    return _pallas(
        body, 11, deps, name=name, grid=(nj, ni),
        in_specs=[pl.BlockSpec((tm, D), lambda j, i: (i, 0)), pl.BlockSpec((16, D), lambda j, i: (nxt(j, i), 0)),
                  pl.BlockSpec((tc, D), lambda j, i: (j, 0)),
                  main(0), main(1), main(0), main(1), halo(0), halo(1),
                  pl.BlockSpec((3, tc), lambda j, i: (0, j)), pl.BlockSpec((3, tc), lambda j, i: (0, j + nj))],
        out_specs=[pl.BlockSpec((2, tm, tc), lambda j, i: (0, i, j)), acc3, acc3, acc1, acc1],
        out_shape=[jax.ShapeDtypeStruct((2, S, Fd), BF16), jax.ShapeDtypeStruct((3, Fd), F32),
                   jax.ShapeDtypeStruct((3, Fd), F32), jax.ShapeDtypeStruct((1, Fd), F32),
                   jax.ShapeDtypeStruct((1, Fd), F32)],
        compiler_params=_cp(2, 56))(dxb, dxb, w, up3, up3, upc3, upc3, upc3, upc3, cw, cw, *deps)


def _lane_taps(cw_ref, ls):
    return [cw_ref[k:k + 1, ls] for k in range(cw_ref.shape[0])]


def _ffn_down_fused_call(up3, cw, cb, w, x2, final_g, target, name):
    _, S, Fd = up3.shape
    D = x2.shape[1]
    tm, tc = _ffn_tiles(S, Fd)
    nk = Fd // tc
    hb = tm // 16
    sc = _tile(tc, SUB_LANES, LANES)

    def body(g_ref, v_ref, gp_ref, vp_ref, cwg_ref, cwv_ref, cbg_ref, cbv_ref, w_ref, x_ref, fg_ref, t_ref,
             f_ref, dx_ref, dxb_ref, loss_ref, dg_ref, acc):
        i, k = pl.program_id(0), pl.program_id(1)

        @pl.when(jnp.logical_and(i == 0, k == 0))
        def _():
            loss_ref[...] = jnp.zeros_like(loss_ref)
            dg_ref[...] = jnp.zeros_like(dg_ref)

        @pl.when(k == 0)
        def _():
            acc[...] = jnp.zeros_like(acc)

        keep = jnp.where(i == 0, 0.0, 1.0)
        part = None
        for s in range(tc // sc):
            ls = slice(s * sc, (s + 1) * sc)
            gp = gp_ref[:, ls].astype(F32)[8:] * keep
            vp = vp_ref[:, ls].astype(F32)[8:] * keep
            cg, _, _ = _conv3(gp, g_ref[:, ls].astype(F32), _lane_taps(cwg_ref, ls))
            cv, _, _ = _conv3(vp, v_ref[:, ls].astype(F32), _lane_taps(cwv_ref, ls))
            fs = (_gelu(cg + cbg_ref[:, ls]) * (cv + cbv_ref[:, ls])).astype(BF16)
            f_ref[:, ls] = fs
            d = jnp.dot(fs, w_ref[s * sc:(s + 1) * sc, :], preferred_element_type=F32)
            part = d if part is None else part + d
        acc[...] += part

        @pl.when(k == nk - 1)
        def _():
            x3 = x_ref[...] + acc[...]
            r = lax.rsqrt(_rows_mean(x3 * x3) + RMS_EPS)
            g = fg_ref[...]
            xn = x3 * r
            diff = xn * g - t_ref[...]
            loss_ref[...] += 0.5 * jnp.sum(_rows_mean(diff * diff))
            dout = diff * (1.0 / D)
            dg_ref[...] += _col_sum(dout * xn)
            dyg = dout * g
            dx = r * (dyg - xn * _rows_mean(dyg * xn))
            dx_ref[...] = dx
            dxb_ref[...] = dx.astype(BF16)

    row = lambda i, k: (i, 0)
    prev = lambda i, k: (jnp.maximum(i * hb - 1, 0), k)
    once = pl.Buffered(1)
    return pl.pallas_call(
        body, name=name, grid=(S // tm, nk),
        in_specs=[pl.BlockSpec((None, tm, tc), lambda i, k: (0, i, k)), pl.BlockSpec((None, tm, tc), lambda i, k: (1, i, k)),
                  pl.BlockSpec((None, 16, tc), lambda i, k: (0,) + prev(i, k)),
                  pl.BlockSpec((None, 16, tc), lambda i, k: (1,) + prev(i, k)),
                  pl.BlockSpec((3, tc), lambda i, k: (0, k)), pl.BlockSpec((3, tc), lambda i, k: (0, k + nk)),
                  pl.BlockSpec((1, tc), lambda i, k: (0, k)), pl.BlockSpec((1, tc), lambda i, k: (0, k + nk)),
                  pl.BlockSpec((tc, D), lambda i, k: (k, 0)),
                  pl.BlockSpec((tm, D), row, pipeline_mode=once), pl.BlockSpec((1, D), lambda i, k: (0, 0)),
                  pl.BlockSpec((tm, D), row, pipeline_mode=once)],
        out_specs=[pl.BlockSpec((tm, tc), lambda i, k: (i, k)), pl.BlockSpec((tm, D), row), pl.BlockSpec((tm, D), row),
                   pl.BlockSpec((8, LANES), lambda i, k: (0, 0)), pl.BlockSpec((1, D), lambda i, k: (0, 0))],
        out_shape=[jax.ShapeDtypeStruct((S, Fd), BF16), jax.ShapeDtypeStruct((S, D), F32),
                   jax.ShapeDtypeStruct((S, D), BF16), jax.ShapeDtypeStruct((8, LANES), F32),
                   jax.ShapeDtypeStruct((1, D), F32)],
        scratch_shapes=[pltpu.VMEM((tm, D), F32)],
        compiler_params=_cp(2, 56))(up3, up3, up3, up3, cw, cw, cb, cb, w, x2, final_g, target)


def _ffn_bwd_fused_call(up3, d_f, cw, cb, wg, resid, xin, g, name, deps=()):
    _, S, Fd = up3.shape
    nb, D, bn = wg.shape
    tm, tc = _tile(S, 256, 16), _ffn_tiles(S, Fd)[1]
    nk = Fd // tc
    ni = S // tm
    hb = tm // 16
    nsubw = bn // tc
    half = nb // 2
    sc = _tile(tc, SUB_LANES, LANES)

    def body(g_ref, v_ref, gp_ref, vp_ref, gn_ref, vn_ref, df_ref, dfn_ref, cwg_ref, cwv_ref, cbg_ref, cbv_ref,
             wgate_ref, wval_ref, r_ref, x_ref, ng_ref,
             dup_ref, dx_ref, dxb_ref, dg_ref, dcwg_ref, dcwv_ref, dcbg_ref, dcbv_ref, acc):
        i, k = pl.program_id(0), pl.program_id(1)

        @pl.when(jnp.logical_and(i == 0, k == 0))
        def _():
            for r in (dg_ref, dcwg_ref, dcwv_ref, dcbg_ref, dcbv_ref):
                r[...] = jnp.zeros_like(r)

        @pl.when(k == 0)
        def _():
            acc[...] = jnp.zeros_like(acc)

        keep_prev = jnp.where(i == 0, 0.0, 1.0)
        keep_next = jnp.where(i == ni - 1, 0.0, 1.0)
        zeros8 = jnp.zeros((8, sc), F32)
        part = None
        for s in range(tc // sc):
            ls = slice(s * sc, (s + 1) * sc)
            off = pl.multiple_of(k * tc + s * sc, LANES)
            df = jnp.concatenate([df_ref[:, ls].astype(F32), dfn_ref[:, ls].astype(F32)[:8] * keep_next], axis=0)

            def half_fwd(x_ref_, xp_ref, xn_ref, cw_ref, cb_ref):
                taps = _lane_taps(cw_ref, ls)
                prev8 = xp_ref[:, ls].astype(F32)[8:] * keep_prev
                ext = jnp.concatenate([x_ref_[:, ls].astype(F32), xn_ref[:, ls].astype(F32)[:8]], axis=0)
                conv, s1, s2 = _conv3(prev8, ext, taps)
                return taps, conv + cb_ref[:, ls], (ext, s1, s2)

            tg, cg, gsh = half_fwd(g_ref, gp_ref, gn_ref, cwg_ref, cbg_ref)
            tv, cv, vsh = half_fwd(v_ref, vp_ref, vn_ref, cwv_ref, cbv_ref)
            gel, dgel = _gelu_parts(cg)
            d_gate = df * cv * dgel
            d_val = df * gel

            def half_bwd(d, taps, shifts, dcw_ref, dcb_ref, slab, w_ref):
                dup = (taps[2] * d + taps[1] * _shift_up(d, zeros8, 1) + taps[0] * _shift_up(d, zeros8, 2))[:tm]
                dupb = dup.astype(BF16)
                dup_ref[slab, :, ls] = dupb
                dm = d[:tm]
                ext, s1, s2 = shifts
                dcw_ref[2:3, pl.ds(off, sc)] += _col_sum(ext[:tm] * dm)
                dcw_ref[1:2, pl.ds(off, sc)] += _col_sum(s1[:tm] * dm)
                dcw_ref[0:1, pl.ds(off, sc)] += _col_sum(s2[:tm] * dm)
                dcb_ref[:, pl.ds(off, sc)] += _col_sum(dm)
                return lax.dot_general(dupb, w_ref[:, ls], _DN_NT, preferred_element_type=F32)

            d = (half_bwd(d_gate, tg, gsh, dcwg_ref, dcbg_ref, 0, wgate_ref)
                 + half_bwd(d_val, tv, vsh, dcwv_ref, dcbv_ref, 1, wval_ref))
            part = d if part is None else part + d
        acc[...] += part

        @pl.when(k == nk - 1)
        def _():
            dh = acc[...]
            xv = x_ref[...]
            r = lax.rsqrt(_rows_mean(xv * xv) + RMS_EPS)
            xn = xv * r
            dg_ref[...] += _col_sum(dh * xn)
            dyg = dh * ng_ref[...]
            dx = r_ref[...] + r * (dyg - xn * _rows_mean(dyg * xn))
            dx_ref[...] = dx
            dxb_ref[...] = dx.astype(BF16)

    row = lambda i, k: (i, 0)
    prev = lambda i, k: (jnp.maximum(i * hb - 1, 0), k)
    nxt = lambda i, k: (jnp.minimum((i + 1) * hb, S // 16 - 1), k)
    main = lambda s: pl.BlockSpec((None, tm, tc), lambda i, k: (s, i, k))
    halo = lambda s, f: pl.BlockSpec((None, 16, tc), lambda i, k: (s,) + f(i, k))
    full3 = pl.BlockSpec((3, Fd), lambda i, k: (0, 0))
    full1 = pl.BlockSpec((1, Fd), lambda i, k: (0, 0))
    once = pl.Buffered(1)
    return _pallas(
        body, 17, deps, name=name, grid=(ni, nk),
        in_specs=[main(0), main(1), halo(0, prev), halo(1, prev), halo(0, nxt), halo(1, nxt),
                  pl.BlockSpec((tm, tc), lambda i, k: (i, k)), pl.BlockSpec((16, tc), nxt),
                  pl.BlockSpec((3, tc), lambda i, k: (0, k)), pl.BlockSpec((3, tc), lambda i, k: (0, k + nk)),
                  pl.BlockSpec((1, tc), lambda i, k: (0, k)), pl.BlockSpec((1, tc), lambda i, k: (0, k + nk)),
                  pl.BlockSpec((None, D, tc), lambda i, k: (k // nsubw, 0, k % nsubw)),
                  pl.BlockSpec((None, D, tc), lambda i, k: (half + k // nsubw, 0, k % nsubw)),
                  pl.BlockSpec((tm, D), row, pipeline_mode=once), pl.BlockSpec((tm, D), row, pipeline_mode=once),
                  pl.BlockSpec((1, D), lambda i, k: (0, 0))],
        out_specs=[pl.BlockSpec((2, tm, tc), lambda i, k: (0, i, k)), pl.BlockSpec((tm, D), row),
                   pl.BlockSpec((tm, D), row), pl.BlockSpec((1, D), lambda i, k: (0, 0)), full3, full3, full1, full1],
        out_shape=[jax.ShapeDtypeStruct((2, S, Fd), BF16), jax.ShapeDtypeStruct((S, D), F32),
                   jax.ShapeDtypeStruct((S, D), BF16), jax.ShapeDtypeStruct((1, D), F32),
                   jax.ShapeDtypeStruct((3, Fd), F32), jax.ShapeDtypeStruct((3, Fd), F32),
                   jax.ShapeDtypeStruct((1, Fd), F32), jax.ShapeDtypeStruct((1, Fd), F32)],
        scratch_shapes=[pltpu.VMEM((tm, D), F32)],
        compiler_params=_cp(2, 60))(up3, up3, up3, up3, up3, up3, d_f, d_f, cw, cw, cb, cb, wg, wg, resid, xin, g, *deps)


def _gm_forward_tile(pv, vg, vb, ws_ref, bsb_ref, mbuf, H, nc):
    W = H * CHUNK
    z, dz = _gelu_parts(pv)
    u, v0 = z[:, :W], z[:, W:]
    xc = v0 - _rows_mean(v0)
    rs = lax.rsqrt(_rows_mean(xc * xc) + LN_EPS)
    vh = xc * rs
    vnb = (vh * vg + vb).astype(BF16)
    mask = lax.broadcasted_iota(jnp.int32, (CHUNK, CHUNK), 0) >= lax.broadcasted_iota(jnp.int32, (CHUNK, CHUNK), 1)
    for h in range(H):
        cs = slice(h * CHUNK, (h + 1) * CHUNK)
        wm = jnp.where(mask, ws_ref[h], 0.0).astype(BF16)
        vcat = jnp.concatenate([vnb[c * CHUNK:(c + 1) * CHUNK, cs] for c in range(nc)], axis=1)
        mix = jnp.dot(wm, vcat, preferred_element_type=F32)
        for c in range(nc):
            mbuf[c * CHUNK:(c + 1) * CHUNK, cs] = mix[:, c * CHUNK:(c + 1) * CHUNK] + bsb_ref[h]
    return dz, u, vh, rs, vnb, mask


def _gm_fwd_call(p, v_g, v_b, ws, bsb, out_g, name, deps=()):
    S = p.shape[0]
    H = ws.shape[0]
    W = H * CHUNK
    tm = _tile(S, 256, CHUNK)
    nc = tm // CHUNK

    def body(p_ref, vg_ref, vb_ref, ws_ref, bsb_ref, og_ref, y_ref, mbuf):
        _, u, _, _, _, _ = _gm_forward_tile(p_ref[...], vg_ref[...], vb_ref[...], ws_ref, bsb_ref, mbuf, H, nc)
        yg = u * mbuf[...]
        r = lax.rsqrt(_rows_mean(yg * yg) + RMS_EPS)
        y_ref[...] = (yg * r * og_ref[...]).astype(BF16)

    vec = pl.BlockSpec((1, W), lambda i: (0, 0))
    mat = pl.BlockSpec((H, CHUNK, CHUNK), lambda i: (0, 0, 0))
    return _pallas(
        body, 6, deps, name=name, grid=(S // tm,),
        in_specs=[pl.BlockSpec((tm, 2 * W), lambda i: (i, 0)), vec, vec, mat, mat, vec],
        out_specs=pl.BlockSpec((tm, W), lambda i: (i, 0)),
        out_shape=jax.ShapeDtypeStruct((S, 2 * W), BF16),
        scratch_shapes=[pltpu.VMEM((tm, W), F32)], compiler_params=_cp(1))(p, v_g, v_b, ws, bsb, out_g, *deps)


def _gm_bwd_call(p, d_y, v_g, v_b, ws, bsb, out_g, name, deps=()):
    S = p.shape[0]
    H = ws.shape[0]
    W = H * CHUNK
    tm = _tile(S, 256, CHUNK)
    nc = tm // CHUNK
    ni = S // tm

    def body(p_ref, dy_ref, vg_ref, vb_ref, ws_ref, bsb_ref, og_ref,
             dp_ref, dvg_ref, dvb_ref, dws_ref, dbs_ref, dog_ref, mbuf, dvbuf):
        i = pl.program_id(0)

        @pl.when(i == 0)
        def _():
            for r in (dvg_ref, dvb_ref, dws_ref, dbs_ref, dog_ref):
                r[...] = jnp.zeros_like(r)

        vg = vg_ref[...]
        dz, u, vh, rs, vnb, mask = _gm_forward_tile(p_ref[...], vg, vb_ref[...], ws_ref, bsb_ref, mbuf, H, nc)
        mixed = mbuf[...]
        yg = u * mixed
        r = lax.rsqrt(_rows_mean(yg * yg) + RMS_EPS)
        yn = yg * r
        dya = dy_ref[...]
        dog_ref[...] += _col_sum(dya * yn)
        dyg = dya * og_ref[...]
        dygm = r * (dyg - yn * _rows_mean(dyg * yn))
        du = dygm * mixed
        dmix = dygm * u
        dmb = dmix.astype(BF16)
        for h in range(H):
            cs = slice(h * CHUNK, (h + 1) * CHUNK)
            wm = jnp.where(mask, ws_ref[h], 0.0).astype(BF16)
            dcat = jnp.concatenate([dmb[c * CHUNK:(c + 1) * CHUNK, cs] for c in range(nc)], axis=1)
            vcat = jnp.concatenate([vnb[c * CHUNK:(c + 1) * CHUNK, cs] for c in range(nc)], axis=1)
            dvn = lax.dot_general(wm, dcat, _DN_TN, preferred_element_type=F32)
            dws_ref[h] += jnp.where(mask, lax.dot_general(dcat, vcat, _DN_NT, preferred_element_type=F32), 0.0)
            dbs = dmix[0:CHUNK, cs]
            for c in range(1, nc):
                dbs = dbs + dmix[c * CHUNK:(c + 1) * CHUNK, cs]
            dbs_ref[h] += dbs
            for c in range(nc):
                dvbuf[c * CHUNK:(c + 1) * CHUNK, cs] = dvn[:, c * CHUNK:(c + 1) * CHUNK]
        dvn_all = dvbuf[...]
        dvg_ref[...] += _col_sum(dvn_all * vh)
        dvb_ref[...] += _col_sum(dvn_all)
        dvh = dvn_all * vg
        dv0 = rs * (dvh - _rows_mean(dvh) - vh * _rows_mean(dvh * vh))
        dp_ref[...] = (jnp.concatenate([du, dv0], axis=1) * dz).astype(BF16)

        @pl.when(i == ni - 1)
        def _():
            for h in range(H):
                dbs_ref[h] = jnp.broadcast_to(jnp.sum(dbs_ref[h], axis=1, keepdims=True), (CHUNK, CHUNK))

    vec = pl.BlockSpec((1, W), lambda i: (0, 0))
    mat = pl.BlockSpec((H, CHUNK, CHUNK), lambda i: (0, 0, 0))
    vshape = jax.ShapeDtypeStruct((1, W), F32)
    mshape = jax.ShapeDtypeStruct((H, CHUNK, CHUNK), F32)
    return _pallas(
        body, 7, deps, name=name, grid=(ni,),
        in_specs=[pl.BlockSpec((tm, 2 * W), lambda i: (i, 0)), pl.BlockSpec((tm, W), lambda i: (i, 0)),
                  vec, vec, mat, mat, vec],
        out_specs=[pl.BlockSpec((tm, 2 * W), lambda i: (i, 0)), vec, vec, mat, mat, vec],
        out_shape=[jax.ShapeDtypeStruct((S, 4 * W), BF16), vshape, vshape, mshape, mshape, vshape],
        scratch_shapes=[pltpu.VMEM((tm, W), F32), pltpu.VMEM((tm, W), F32)],
        compiler_params=_cp(1))(p, d_y, v_g, v_b, ws, bsb, out_g, *deps)


def _lru_gates(prev8, xl, cw, cb, wa_ref, ba, wx_ref, bx, lam, H):
    sh = [_shift_down(prev8, xl, k) for k in range(4)]
    xr = cw[3] * sh[0] + cw[2] * sh[1] + cw[1] * sh[2] + cw[0] * sh[3] + cb
    xrb = xr.astype(BF16)
    rp, ip = [], []
    for h in range(H):
        cs = slice(h * CHUNK, (h + 1) * CHUNK)
        rp.append(jnp.dot(xrb[:, cs], wa_ref[h].astype(BF16), preferred_element_type=F32))
        ip.append(jnp.dot(xrb[:, cs], wx_ref[h].astype(BF16), preferred_element_type=F32))
    r = _sigmoid(jnp.concatenate(rp, axis=1) + ba)
    ig = _sigmoid(jnp.concatenate(ip, axis=1) + bx)
    sp = _softplus(-lam)
    t = jnp.tanh((-LRU_C) * r * sp)
    q = lax.rsqrt(1.0 - t)
    a = jnp.sqrt(1.0 + t) * q
    mult = jnp.sqrt(-2.0 * t) * q
    a2_over_mult = (1.0 + t) * q * lax.rsqrt(-2.0 * t)
    return xr, xrb, r, ig, sp, a, mult, a2_over_mult, sh


def _lru_fwd_call(p, cw, cb, wa, ba, wx, bx, lam, out_g, y_half, name):
    S = p.shape[0]
    H = wa.shape[0]
    W = H * CHUNK
    tm = _tile(S, 256, 16)
    ng = tm // 8

    def body(pg_ref, px_ref, cw_ref, cb_ref, wa_ref, ba_ref, wx_ref, bx_ref, lam_ref, og_ref, y_in_ref,
             y_ref, h_ref, xprev, hcar, abuf, bbuf):
        @pl.when(pl.program_id(0) == 0)
        def _():
            xprev[...] = jnp.zeros_like(xprev)
            hcar[...] = jnp.zeros_like(hcar)

        xl = px_ref[...]
        xr, _, _, ig, _, a, mult, _, _ = _lru_gates(xprev[...], xl, _taps(cw_ref), cb_ref[...], wa_ref, ba_ref[...],
                                                    wx_ref, bx_ref[...], lam_ref[...], H)
        xprev[...] = xl[tm - 8:]
        b = mult * (ig * xr)
        sub = lax.broadcasted_iota(jnp.int32, (tm, W), 0) & 7
        for d in (1, 2, 4):
            m = sub >= d
            a_s = jnp.where(m, pltpu.roll(a, d, 0), 1.0)
            b_s = jnp.where(m, pltpu.roll(b, d, 0), 0.0)
            b = a * b_s + b
            a = a * a_s
        abuf[...] = a
        bbuf[...] = b

        def step(g, carry):
            r0 = pl.multiple_of(g * 8, 8)
            h_ref[pl.ds(r0, 8), :] = abuf[pl.ds(r0, 8), :] * carry + bbuf[pl.ds(r0, 8), :]
            return jnp.broadcast_to(h_ref[pl.ds(r0 + 7, 1), :], (8, W))

        hcar[...] = lax.fori_loop(0, ng, step, hcar[...])
        yl = h_ref[...] * _gelu(pg_ref[...])
        r = lax.rsqrt(_rows_mean(yl * yl) + RMS_EPS)
        y_ref[...] = (yl * r * og_ref[...]).astype(BF16)

    vec = pl.BlockSpec((1, W), lambda i: (0, 0))
    mat = pl.BlockSpec((H, CHUNK, CHUNK), lambda i: (0, 0, 0))
    return pl.pallas_call(
        body, name=name, grid=(S // tm,),
        in_specs=[pl.BlockSpec((tm, W), lambda i: (i, 2)), pl.BlockSpec((tm, W), lambda i: (i, 3)),
                  pl.BlockSpec((4, W), lambda i: (0, 0)), vec, mat, vec, mat, vec, vec, vec,
                  pl.BlockSpec(memory_space=pl.ANY)],
        out_specs=[pl.BlockSpec((tm, W), lambda i: (i, 1)), pl.BlockSpec((tm, W), lambda i: (i, 0))],
        out_shape=[jax.ShapeDtypeStruct((S, 2 * W), BF16), jax.ShapeDtypeStruct((S, W), F32)],
        input_output_aliases={10: 0},
        scratch_shapes=[pltpu.VMEM((8, W), F32), pltpu.VMEM((8, W), F32), pltpu.VMEM((tm, W), F32),
                        pltpu.VMEM((tm, W), F32)],
        compiler_params=_cp(1))(p, p, cw, cb, wa, ba, wx, bx, lam, out_g, y_half)


def _lru_bwd_call(p, hs, d_y, cw, cb, wa, ba, wx, bx, lam, out_g, dp_half, name):
    S = p.shape[0]
    H = wa.shape[0]
    W = H * CHUNK
    tm = _tile(S, 256, 16)
    ng = tm // 8
    ni = S // tm
    hb = tm // 8

    def body(pg_ref, px_ref, pxp_ref, h_ref, hp_ref, dy_ref, cw_ref, cb_ref, wa_ref, ba_ref, wx_ref, bx_ref,
             lam_ref, og_ref, dp_in_ref,
             dp_ref, dcw_ref, dcb_ref, dwa_ref, dba_ref, dwx_ref, dbx_ref, dlam_ref, dog_ref,
             a_next, e_next, dxr_next, abuf, bbuf, ebuf):
        i = pl.program_id(0)
        ri = ni - 1 - i

        @pl.when(i == 0)
        def _():
            for r in (dcw_ref, dcb_ref, dwa_ref, dba_ref, dwx_ref, dbx_ref, dlam_ref, dog_ref,
                      a_next, e_next, dxr_next):
                r[...] = jnp.zeros_like(r)

        keep_prev = jnp.where(ri == 0, 0.0, 1.0)
        cw_ = _taps(cw_ref)
        lam_ = lam_ref[...]
        xl = px_ref[...]
        xr, xrb, r, ig, sp, a, mult, a2m, sh = _lru_gates(pxp_ref[...] * keep_prev, xl, cw_, cb_ref[...], wa_ref,
                                                          ba_ref[...], wx_ref, bx_ref[...], lam_, H)
        gg, dgg = _gelu_parts(pg_ref[...])
        hv = h_ref[...]
        yl = hv * gg
        rr = lax.rsqrt(_rows_mean(yl * yl) + RMS_EPS)
        yn = yl * rr
        dyb = dy_ref[...]
        dog_ref[...] += _col_sum(dyb * yn)
        dyg = dyb * og_ref[...]
        dyl = rr * (dyg - yn * _rows_mean(dyg * yn))
        dh = dyl * gg
        dgl = dyl * hv * dgg

        an = _shift_up(a, a_next[...], 1)
        eb = dh
        sub = lax.broadcasted_iota(jnp.int32, (tm, W), 0) & 7
        for d in (1, 2, 4):
            m = sub < 8 - d
            a_s = jnp.where(m, pltpu.roll(an, tm - d, 0), 1.0)
            e_s = jnp.where(m, pltpu.roll(eb, tm - d, 0), 0.0)
            eb = an * e_s + eb
            an = an * a_s
        abuf[...] = an
        bbuf[...] = eb

        def step(g, carry):
            r0 = pl.multiple_of((ng - 1 - g) * 8, 8)
            ebuf[pl.ds(r0, 8), :] = abuf[pl.ds(r0, 8), :] * carry + bbuf[pl.ds(r0, 8), :]
            return jnp.broadcast_to(ebuf[pl.ds(r0, 1), :], (8, W))

        lax.fori_loop(0, ng, step, jnp.broadcast_to(e_next[0:1, :], (8, W)))
        e = ebuf[...]
        a_next[...] = a[0:8]
        e_next[...] = e[0:8]

        hm1 = _shift_down(hp_ref[...] * keep_prev, hv, 1)
        da = e * hm1
        dmult = e * ig * xr
        di = e * mult * xr
        dxr = e * mult * ig
        dla = da * a - dmult * a2m
        dr = dla * ((-LRU_C) * sp)
        dlam_ref[...] += _col_sum(dla * ((-LRU_C) * r))
        dpr = dr * r * (1.0 - r)
        dpi = di * ig * (1.0 - ig)
        dba_ref[...] += _col_sum(dpr)
        dbx_ref[...] += _col_sum(dpi)
        dprb = dpr.astype(BF16)
        dpib = dpi.astype(BF16)
        back = []
        for h in range(H):
            cs = slice(h * CHUNK, (h + 1) * CHUNK)
            wab = wa_ref[h].astype(BF16)
            wxb = wx_ref[h].astype(BF16)
            back.append(lax.dot_general(dprb[:, cs], wab, _DN_NT, preferred_element_type=F32)
                        + lax.dot_general(dpib[:, cs], wxb, _DN_NT, preferred_element_type=F32))
            dwa_ref[h] += lax.dot_general(xrb[:, cs], dprb[:, cs], _DN_TN, preferred_element_type=F32)
            dwx_ref[h] += lax.dot_general(xrb[:, cs], dpib[:, cs], _DN_TN, preferred_element_type=F32)
        dxr = dxr + jnp.concatenate(back, axis=1)

        nxt = dxr_next[...]
        dxl = (cw_[3] * dxr + cw_[2] * _shift_up(dxr, nxt, 1) + cw_[1] * _shift_up(dxr, nxt, 2)
               + cw_[0] * _shift_up(dxr, nxt, 3))
        dxr_next[...] = dxr[0:8]
        for k in range(4):
            dcw_ref[k:k + 1, :] += _col_sum(sh[3 - k] * dxr)
        dcb_ref[...] += _col_sum(dxr)
        dp_ref[...] = jnp.concatenate([dgl, dxl], axis=1).astype(BF16)

        @pl.when(i == ni - 1)
        def _():
            dlam_ref[...] = -dlam_ref[...] * _sigmoid(-lam_)

    vec = pl.BlockSpec((1, W), lambda i: (0, 0))
    mat = pl.BlockSpec((H, CHUNK, CHUNK), lambda i: (0, 0, 0))
    rev = lambda i: ni - 1 - i
    prev = lambda i: jnp.maximum(rev(i) * hb - 1, 0)
    vshape = jax.ShapeDtypeStruct((1, W), F32)
    mshape = jax.ShapeDtypeStruct((H, CHUNK, CHUNK), F32)
    tile = lambda: pltpu.VMEM((tm, W), F32)
    car = lambda: pltpu.VMEM((8, W), F32)
    return pl.pallas_call(
        body, name=name, grid=(ni,),
        in_specs=[pl.BlockSpec((tm, W), lambda i: (rev(i), 2)), pl.BlockSpec((tm, W), lambda i: (rev(i), 3)),
                  pl.BlockSpec((8, W), lambda i: (prev(i), 3)),
                  pl.BlockSpec((tm, W), lambda i: (rev(i), 0)), pl.BlockSpec((8, W), lambda i: (prev(i), 0)),
                  pl.BlockSpec((tm, W), lambda i: (rev(i), 1)),
                  pl.BlockSpec((4, W), lambda i: (0, 0)), vec, mat, vec, mat, vec, vec, vec,
                  pl.BlockSpec(memory_space=pl.ANY)],
        out_specs=[pl.BlockSpec((tm, 2 * W), lambda i: (rev(i), 1)), pl.BlockSpec((4, W), lambda i: (0, 0)), vec,
                   mat, vec, mat, vec, vec, vec],
        out_shape=[jax.ShapeDtypeStruct((S, 4 * W), BF16), jax.ShapeDtypeStruct((4, W), F32), vshape,
                   mshape, vshape, mshape, vshape, vshape, vshape],
        input_output_aliases={14: 0},
        scratch_shapes=[car(), car(), car(), tile(), tile(), tile()],
        compiler_params=_cp(1, 56))(p, p, p, hs, hs, d_y, cw, cb, wa, ba, wx, bx, lam, out_g, dp_half)


def _rows128(a):
    return a.reshape(-1, LANES).astype(F32)


def _pack(arrays, pad_to=256):
    flat = jnp.concatenate([_rows128(a) for a in arrays], axis=0)
    pad = (-flat.shape[0]) % pad_to
    if pad:
        flat = jnp.concatenate([flat, jnp.zeros((pad, LANES), F32)], axis=0)
    return flat


def _unpack(flat, shapes):
    out, r = [], 0
    for s in shapes:
        n = 1
        for d in s:
            n *= d
        out.append(flat[r:r + n // LANES].reshape(s))
        r += n // LANES
    return out


def kernel(x, norm1_g, w_in, gm_v_g, gm_v_b, gm_ws, gm_bs, lru_conv_w, lru_conv_b, lru_wa, lru_ba, lru_wx, lru_bx, lru_lambda, gm_out_g, lru_out_g, w_out, norm2_g, ffn_w_up, ffn_conv_w, ffn_conv_b, ffn_w_down, final_g, loss_target, m_norm1_g, m_w_in, m_gm_v_g, m_gm_v_b, m_gm_ws, m_gm_bs, m_lru_conv_w, m_lru_conv_b, m_lru_wa, m_lru_ba, m_lru_wx, m_lru_bx, m_lru_lambda, m_gm_out_g, m_lru_out_g, m_w_out, m_norm2_g, m_ffn_w_up, m_ffn_conv_w, m_ffn_conv_b, m_ffn_w_down, m_final_g, v_norm1_g, v_w_in, v_gm_v_g, v_gm_v_b, v_gm_ws, v_gm_bs, v_lru_conv_w, v_lru_conv_b, v_lru_wa, v_lru_ba, v_lru_wx, v_lru_bx, v_lru_lambda, v_gm_out_g, v_lru_out_g, v_w_out, v_norm2_g, v_ffn_w_up, v_ffn_conv_w, v_ffn_conv_b, v_ffn_w_down, v_final_g):
    wts = dict(norm1_g=norm1_g, w_in=w_in, gm_v_g=gm_v_g, gm_v_b=gm_v_b, gm_ws=gm_ws, gm_bs=gm_bs,
               lru_conv_w=lru_conv_w, lru_conv_b=lru_conv_b, lru_wa=lru_wa, lru_ba=lru_ba, lru_wx=lru_wx,
               lru_bx=lru_bx, lru_lambda=lru_lambda, gm_out_g=gm_out_g, lru_out_g=lru_out_g, w_out=w_out,
               norm2_g=norm2_g, ffn_w_up=ffn_w_up, ffn_conv_w=ffn_conv_w, ffn_conv_b=ffn_conv_b,
               ffn_w_down=ffn_w_down, final_g=final_g)
    mom = dict(norm1_g=m_norm1_g, w_in=m_w_in, gm_v_g=m_gm_v_g, gm_v_b=m_gm_v_b, gm_ws=m_gm_ws, gm_bs=m_gm_bs,
               lru_conv_w=m_lru_conv_w, lru_conv_b=m_lru_conv_b, lru_wa=m_lru_wa, lru_ba=m_lru_ba, lru_wx=m_lru_wx,
               lru_bx=m_lru_bx, lru_lambda=m_lru_lambda, gm_out_g=m_gm_out_g, lru_out_g=m_lru_out_g, w_out=m_w_out,
               norm2_g=m_norm2_g, ffn_w_up=m_ffn_w_up, ffn_conv_w=m_ffn_conv_w, ffn_conv_b=m_ffn_conv_b,
               ffn_w_down=m_ffn_w_down, final_g=m_final_g)
    var = dict(norm1_g=v_norm1_g, w_in=v_w_in, gm_v_g=v_gm_v_g, gm_v_b=v_gm_v_b, gm_ws=v_gm_ws, gm_bs=v_gm_bs,
               lru_conv_w=v_lru_conv_w, lru_conv_b=v_lru_conv_b, lru_wa=v_lru_wa, lru_ba=v_lru_ba, lru_wx=v_lru_wx,
               lru_bx=v_lru_bx, lru_lambda=v_lru_lambda, gm_out_g=v_gm_out_g, lru_out_g=v_lru_out_g, w_out=v_w_out,
               norm2_g=v_norm2_g, ffn_w_up=v_ffn_w_up, ffn_conv_w=v_ffn_conv_w, ffn_conv_b=v_ffn_conv_b,
               ffn_w_down=v_ffn_w_down, final_g=v_final_g)

    xi, yi, ci = lax.axis_index("x"), lax.axis_index("y"), lax.axis_index("c")
    chip = 2 * xi + yi
    dev = 2 * chip + ci
    core_chip = jnp.stack([ci, chip]).astype(jnp.int32)

    xs = x[0]
    tgt = loss_target[0]
    S, D = xs.shape
    H = gm_ws.shape[1]
    W = H * CHUNK
    Fd = ffn_w_down.shape[1] * N_DEV
    lcw_cols = lru_conv_w.shape[2]
    fcw_cols = ffn_conv_w.shape[2]

    dev1 = jnp.reshape(dev, (1,)).astype(jnp.int32)

    def gather_start(shards, name, after=()):
        lands = [_place_own_call(s, dev1, "%s_own%d" % (name, k)) for k, s in enumerate(shards)]
        return _exchange_start(shards, lands, 4 * len(shards), _gather_stage1_copies(len(shards)), name + "_ici", after)

    def gather_forward(lands, name, after=()):
        return _exchange_start([], lands, 3 * len(lands), _gather_stage2_copies(len(lands)), name + "_d2d", after)

    def pair_start(g, name, after=()):
        return _exchange_start([g], [lax.empty((4,) + g.shape[1:], F32)], 4, _pair_copies(1), name, after)

    def chip_start(p16, name, after=()):
        return _exchange_start([p16], [lax.empty((3,) + p16.shape[1:], BF16)], 3, _chip_copies(1), name, after)

    vgm_g, vgm_b = gm_v_g, gm_v_b
    ws, wa, wx = gm_ws[0], lru_wa[0], lru_wx[0]
    bsb = jnp.broadcast_to(gm_bs[0][:, :, None], (H, CHUNK, CHUNK))
    ba, bx = lru_ba.reshape(1, W), lru_bx.reshape(1, W)
    fcb = ffn_conv_b
    fing = final_g.reshape(1, D)

    conv_pack = _pack([lru_conv_w[0], ffn_conv_w[0]], pad_to=8)
    ga1 = gather_start([w_in[0].astype(BF16), conv_pack], "gather_in")
    h1 = _rmsnorm_call(xs, norm1_g, "norm1", deps=(ga1.token,))
    _, la = _exchange_wait(ga1, after=(h1,))
    ga2 = gather_forward(la, "gather_in")
    gb1 = gather_start([w_out[0].astype(BF16)], "gather_out", after=(ga2.token,))
    gc1 = gather_start([ffn_w_up[0].astype(BF16)], "gather_up", after=(gb1.token,))
    _, (win_g, conv_g) = _exchange_wait(ga2, after=(gc1.token,))
    n_l = 4 * lcw_cols // LANES
    n_f = 3 * fcw_cols // LANES
    lcw = conv_g[:, :n_l].reshape(N_DEV, 4, lcw_cols).transpose(1, 0, 2).reshape(4, N_DEV * lcw_cols)
    fcw = conv_g[:, n_l:n_l + n_f].reshape(N_DEV, 3, fcw_cols).transpose(1, 0, 2).reshape(3, N_DEV * fcw_cols)

    p = _mm_blocked_call(h1, win_g, F32, False, "in_proj")
    win_rows = _unblock_call(win_g, "w_in_rows")
    _, lb = _exchange_wait(gb1, after=(p,))
    gb2 = gather_forward(lb, "gather_out")
    y_half = _gm_fwd_call(p, vgm_g, vgm_b, ws, bsb, gm_out_g, "gmlp_fwd", deps=(gb2.token,))
    y, hs = _lru_fwd_call(p, lcw, lru_conv_b, wa, ba, wx, bx, lru_lambda, lru_out_g, y_half, "lru_fwd")
    _, (wout_g,) = _exchange_wait(gb2, after=(y,))
    wout_full = wout_g.reshape(D, D)
    x2 = _mm_out_call(xs, y, wout_full, "out_proj")
    h2 = _rmsnorm_call(x2, norm2_g, "norm2")
    _, lc = _exchange_wait(gc1, after=(h2,))
    gc2 = gather_forward(lc, "gather_up")
    gd1 = gather_start([ffn_w_down[0].astype(BF16)], "gather_down", after=(gc2.token,))
    _, (wup_g,) = _exchange_wait(gc2, after=(gd1.token,))
    up3, upc3, f = _ffn_up_act_call(h2, wup_g, fcw, fcb, "ffn_up")
    _, ld = _exchange_wait(gd1, after=(f,))
    gd2 = gather_forward(ld, "gather_down")
    _, (wdown_g,) = _exchange_wait(gd2)
    wdown_full = wdown_g.reshape(Fd, D)
    dx3, dx3b, loss_acc, d_final = _mm_down_loss_call(x2, f, wdown_full, fing, tgt, "ffn_down_loss")

    g_wdown = _mm_tn_rows_call(f, dx3b, "ffn_down_dw").reshape((N_DEV,) + ffn_w_down.shape[1:])
    pd = pair_start(g_wdown, "pair_down")
    d_up3, dfcw_g, dfcw_v, dfcb_g, dfcb_v = _ffn_down_dx_act_bwd_call(dx3b, wdown_full, up3, upc3, fcw, "ffn_down_dx",
                                                                     deps=(pd.token,))
    (g_wdown,), (r1,) = _exchange_wait(pd, after=(d_up3,))
    own_down, p16 = _pair_add_call(g_wdown, r1, core_chip, "pair_add_down")
    cd = chip_start(p16, "chip_down")
    g_wup = _mm_tn_cols_call(h2, d_up3, N_DEV, ffn_w_up.shape[2], "ffn_up_dw", deps=(cd.token,))
    pu = pair_start(g_wup, "pair_up")
    dx2, dx2b, d_norm2 = _mm_dx_norm_call(d_up3, wup_g, dx3, x2, norm2_g, "ffn_up_dx", deps=(pu.token,))
    g_wout = _mm_tn_rows_call(y, dx2b, "out_proj_dw").reshape((N_DEV,) + w_out.shape[1:])
    po = pair_start(g_wout, "pair_out")
    d_y = _mm_nt_call(dx2b, wout_full, F32, "out_proj_dx", deps=(po.token,))
    (g_wup,), (r1,) = _exchange_wait(pu, after=(d_y,))
    own_up, p16 = _pair_add_call(g_wup, r1, core_chip, "pair_add_up")
    _, (r2_down,) = _exchange_wait(cd, after=(p16,))
    cu = chip_start(p16, "chip_up", after=(r2_down,))
    dp_half, d_vg, d_vb, d_ws, d_bs, d_gog = _gm_bwd_call(p, d_y, vgm_g, vgm_b, ws, bsb, gm_out_g, "gmlp_bwd",
                                                          deps=(cu.token,))
    d_p2, d_lcw, d_lcb, d_wa, d_ba, d_wx, d_bx, d_lam, d_log = _lru_bwd_call(
        p, hs, d_y, lcw, lru_conv_b, wa, ba, wx, bx, lru_lambda, lru_out_g, dp_half, "lru_bwd")
    d_p = d_p2[None]
    (g_wout,), (r1,) = _exchange_wait(po, after=(d_p,))
    own_out, p16_out = _pair_add_call(g_wout, r1, core_chip, "pair_add_out")
    g_win = _mm_tn_cols_call(h1, d_p, N_DEV, w_in.shape[2], "in_proj_dw")
    pi = pair_start(g_win, "pair_in")
    _, (r2_up,) = _exchange_wait(cu, after=(g_win,))
    co = chip_start(p16_out, "chip_out", after=(r2_up,))
    grad_x, d_norm1 = _mm_nt_norm_call(d_p[0], win_rows, dx2, xs, norm1_g, "in_proj_dx", deps=(co.token, pi.token))
    small_g = dict(norm1_g=d_norm1, gm_v_g=d_vg, gm_v_b=d_vb, gm_ws=d_ws, gm_bs=d_bs[:, :, 0], lru_conv_b=d_lcb,
                   lru_wa=d_wa, lru_ba=d_ba, lru_wx=d_wx, lru_bx=d_bx, lru_lambda=d_lam, gm_out_g=d_gog,
                   lru_out_g=d_log, norm2_g=d_norm2,
                   ffn_conv_b=jnp.concatenate([dfcb_g, dfcb_v], axis=1), final_g=d_final)
    rep = _pack([small_g[n] for n in SMALL])
    conv_part = _pack([d_lcw, jnp.concatenate([dfcw_g, dfcw_v], axis=1)], pad_to=8)
    n_rep, n_conv = rep.shape[0], conv_part.shape[0]
    gs1 = gather_start([jnp.concatenate([rep, conv_part], axis=0)], "gather_small")

    (g_win,), (r1,) = _exchange_wait(pi, after=(grad_x,))
    own_in, p16 = _pair_add_call(g_win, r1, core_chip, "pair_add_in")
    _, (r2_out,) = _exchange_wait(co, after=(p16,))
    ci_ = chip_start(p16, "chip_in", after=(r2_out, gs1.token))

    def adamw_big(n, own, r2, deps=()):
        return _adamw_call(wts[n][0], mom[n][0], var[n][0], [(own, None), (r2, 0), (r2, 1), (r2, 2)], "adamw_" + n, deps)

    res = {}
    res["ffn_w_down"] = adamw_big("ffn_w_down", own_down, r2_down, (ci_.token,))
    res["ffn_w_up"] = adamw_big("ffn_w_up", own_up, r2_up, (ci_.token,))
    res["w_out"] = adamw_big("w_out", own_out, r2_out, (ci_.token,))
    _, ls = _exchange_wait(gs1, after=(res["w_out"][0], res["ffn_w_up"][0], res["ffn_w_down"][0]))
    gs2 = gather_forward(ls, "gather_small")
    _, (r2_in,) = _exchange_wait(ci_, after=(gs2.token,))
    res["w_in"] = adamw_big("w_in", own_in, r2_in)
    _, (parts,) = _exchange_wait(gs2, after=(res["w_in"][0],))
    g_rep, d_rep, m_rep, v_rep = _adamw_call(
        _pack([wts[n] for n in SMALL]), _pack([mom[n] for n in SMALL]), _pack([var[n] for n in SMALL]),
        [(parts, k) for k in range(N_DEV)], "adamw_small")
    shapes = [wts[n].shape for n in SMALL]
    for n, g_, d_, m_, v_ in zip(SMALL, _unpack(g_rep, shapes), _unpack(d_rep, shapes), _unpack(m_rep, shapes),
                                 _unpack(v_rep, shapes)):
        res[n] = (g_, d_, m_, v_)
    conv_sum = _sum_call(parts, n_rep, n_conv, "sum_conv_grads")
    g_lcw = conv_sum[:4 * W // LANES].reshape(4, W)
    g_fcw = conv_sum[4 * W // LANES:4 * W // LANES + 6 * Fd // LANES].reshape(3, 2 * Fd)
    for n, full in (("lru_conv_w", g_lcw), ("ffn_conv_w", g_fcw)):
        cols = wts[n].shape[2]
        mine = lax.dynamic_slice_in_dim(full, dev * cols, cols, axis=1)
        res[n] = _adamw_call(wts[n][0], mom[n][0], var[n][0], [(mine, None)], "adamw_" + n)

    loss = lax.psum(loss_acc[0, 0], ("x", "y", "c"))
    outs = [[], [], [], []]
    for n in WEIGHTS:
        for k in range(4):
            outs[k].append(res[n][k].reshape(wts[n].shape))
    return (loss, grad_x[None], *outs[0], *outs[1], *outs[2], *outs[3])
```

```python
import functools
import math

import jax
import jax.numpy as jnp
from jax import lax
from jax.experimental import pallas as pl
from jax.experimental.pallas import tpu as pltpu

F32 = jnp.float32
BF16 = jnp.bfloat16

RMS_EPS = 1e-6
LN_EPS = 1e-5
LRU_C = 8.0
CHUNK = 128
ADAM_LR = 0.001
ADAM_B1 = 0.9
ADAM_B2 = 0.999
ADAM_EPS = 1e-08
ADAM_WD = 0.01
ADAM_STEP = 10
N_DEV = 8
LANES = 128
MIB = 1024 * 1024

WEIGHTS = ['norm1_g', 'w_in', 'gm_v_g', 'gm_v_b', 'gm_ws', 'gm_bs', 'lru_conv_w', 'lru_conv_b', 'lru_wa', 'lru_ba',
           'lru_wx', 'lru_bx', 'lru_lambda', 'gm_out_g', 'lru_out_g', 'w_out', 'norm2_g', 'ffn_w_up', 'ffn_conv_w',
           'ffn_conv_b', 'ffn_w_down', 'final_g']
BIG = ['w_in', 'w_out', 'ffn_w_up', 'ffn_w_down']
CONV = ['lru_conv_w', 'ffn_conv_w']
SMALL = [n for n in WEIGHTS if n not in BIG and n not in CONV]

_DN_NT = (((1,), (1,)), ((), ()))
_DN_TN = (((0,), (0,)), ((), ()))
_GELU_C = 0.7978845608028654


def _cp(n_axes, vmem_mib=48):
    return pltpu.CompilerParams(dimension_semantics=("arbitrary",) * n_axes, vmem_limit_bytes=vmem_mib * MIB)


def _tile(n, pref, mult=8):
    t = min(pref, n)
    t -= t % mult
    while t >= mult:
        if n % t == 0:
            return t
        t -= mult
    return n


def _gelu(z):
    return 0.5 * z * (1.0 + jnp.tanh(_GELU_C * z * (1.0 + 0.044715 * z * z)))


def _gelu_parts(z):
    z2 = z * z
    t = jnp.tanh(_GELU_C * z * (1.0 + 0.044715 * z2))
    g = 0.5 * z * (1.0 + t)
    dg = 0.5 * (1.0 + t) + 0.5 * z * (1.0 - t * t) * (_GELU_C * (1.0 + 0.134145 * z2))
    return g, dg


def _sigmoid(z):
    return 0.5 + 0.5 * jnp.tanh(0.5 * z)


def _softplus(z):
    t = jnp.exp(-jnp.abs(z))
    u = 1.0 + t
    log1p = jnp.where(u == 1.0, t, jnp.log(u) * t / (u - 1.0))
    return jnp.maximum(z, 0.0) + log1p


def _neg_expm1(y):
    e = jnp.exp(y)
    small = jnp.where(e == 1.0, y, (e - 1.0) * y / jnp.log(e))
    return -jnp.where(y < -0.5, e - 1.0, small)


def _rows_mean(v):
    return jnp.mean(v, axis=-1, keepdims=True)


def _col_sum(v):
    return jnp.sum(v, axis=0, keepdims=True)


def _shift_down(prev8, cur, k):
    if k == 0:
        return cur
    z = jnp.concatenate([prev8, cur], axis=0)
    return pltpu.roll(z, k, 0)[8:]


def _shift_up(cur, next8, k):
    if k == 0:
        return cur
    n = cur.shape[0]
    z = jnp.concatenate([cur, next8], axis=0)
    return pltpu.roll(z, n + 8 - k, 0)[:n]


def _mesh_pos():
    return lax.axis_index("x"), lax.axis_index("y"), lax.axis_index("c")


def _any_specs(n):
    return [pl.BlockSpec(memory_space=pl.ANY)] * n


def _pallas(body, n_in, deps, **kw):
    nd = len(deps)
    if not nd:
        return pl.pallas_call(body, **kw)

    def ordered(*refs):
        body(*refs[:n_in], *refs[n_in + nd:])

    kw["in_specs"] = list(kw["in_specs"]) + _any_specs(nd)
    return pl.pallas_call(ordered, **kw)


_HBM = pl.BlockSpec(memory_space=pltpu.HBM)
_SEM = pl.BlockSpec(memory_space=pltpu.SEMAPHORE)
_EFFECT = pltpu.SideEffectType.DATAFLOW_SIDE_EFFECTING


class _InFlight:
    def __init__(self, sems, bufs, token, n_src, n_copies, make_copies, name):
        self.sems, self.bufs, self.token = sems, bufs, token
        self.n_src, self.n_copies, self.make_copies, self.name = n_src, n_copies, make_copies, name


def _exchange_start(srcs, lands, n_copies, make_copies, name, after=()):
    bufs = list(srcs) + list(lands)
    nb, na = len(bufs), len(after)
    ns = len(srcs)

    def body(*refs):
        b_refs = refs[:nb]
        outs = refs[nb + na:]
        send, recv = outs[:n_copies], outs[n_copies:2 * n_copies]
        token = outs[-1]
        for cp in make_copies(b_refs[:ns], b_refs[ns:], send, recv):
            cp.start()
        token[...] = jnp.zeros_like(token)

    out = pl.pallas_call(
        body, name=name,
        out_shape=[pltpu.SemaphoreType.DMA(())] * (2 * n_copies) + [pltpu.HBM(b.shape, b.dtype) for b in bufs]
        + [jax.ShapeDtypeStruct((8, LANES), F32)],
        in_specs=[_HBM] * nb + _any_specs(na),
        out_specs=[_SEM] * (2 * n_copies) + [_HBM] * nb + [pl.BlockSpec(memory_space=pltpu.VMEM)],
        input_output_aliases={i: 2 * n_copies + i for i in range(nb)},
        compiler_params=pltpu.CompilerParams(has_side_effects=_EFFECT),
    )(*[pltpu.with_memory_space_constraint(b, pltpu.HBM) for b in bufs], *after)
    return _InFlight(out[:2 * n_copies], out[2 * n_copies:2 * n_copies + nb], out[-1], ns, n_copies, make_copies, name)


def _exchange_wait(fl, after=()):
    nb, na, nc, ns = len(fl.bufs), len(after), fl.n_copies, fl.n_src

    def body(*refs):
        b_refs = refs[:nb]
        sems = refs[nb:nb + 2 * nc]
        copies = fl.make_copies(b_refs[:ns], b_refs[ns:], sems[:nc], sems[nc:])
        for cp in copies:
            cp.wait_send()
        for cp in copies:
            cp.wait_recv()

    out = pl.pallas_call(
        body, name=fl.name + "_wait",
        out_shape=[pltpu.HBM(b.shape, b.dtype) for b in fl.bufs],
        in_specs=[_HBM] * nb + [_SEM] * (2 * nc) + _any_specs(na),
        out_specs=[_HBM] * nb,
        input_output_aliases={i: i for i in range(nb)},
        compiler_params=pltpu.CompilerParams(has_side_effects=_EFFECT),
    )(*fl.bufs, *fl.sems, *after)
    return list(out[:ns]), list(out[ns:])


def _remote(src, dst, send_sem, recv_sem, to):
    return pltpu.make_async_remote_copy(src_ref=src, dst_ref=dst, send_sem=send_sem, recv_sem=recv_sem,
                                        device_id=to, device_id_type=pl.DeviceIdType.MESH)


def _gather_stage1_copies(n):
    def make(s_refs, l_refs, send, recv):
        x, y, c = _mesh_pos()
        own = 4 * x + 2 * y + c
        targets = [(x, y, 1 - c), (1 - x, y, c), (x, 1 - y, c), (1 - x, 1 - y, c)]
        return [_remote(s_refs[a], l_refs[a].at[own], send[4 * a + k], recv[4 * a + k], to)
                for a in range(n) for k, to in enumerate(targets)]
    return make


def _gather_stage2_copies(n):
    def make(s_refs, l_refs, send, recv):
        x, y, c = _mesh_pos()
        blocks = [4 * (1 - x) + 2 * y + c, 4 * x + 2 * (1 - y) + c, 4 * (1 - x) + 2 * (1 - y) + c]
        return [_remote(l_refs[a].at[b], l_refs[a].at[b], send[3 * a + j], recv[3 * a + j], (x, y, 1 - c))
                for a in range(n) for j, b in enumerate(blocks)]
    return make


def _pair_copies(n):
    def make(s_refs, l_refs, send, recv):
        x, y, c = _mesh_pos()
        return [_remote(s_refs[a].at[2 * k + 1 - c], l_refs[a].at[k], send[4 * a + k], recv[4 * a + k], (x, y, 1 - c))
                for a in range(n) for k in range(4)]
    return make


def _chip_copies(n):
    def make(s_refs, l_refs, send, recv):
        x, y, c = _mesh_pos()
        chips = [(1 - x, y), (x, 1 - y), (1 - x, 1 - y)]
        return [_remote(s_refs[a].at[2 * ch[0] + ch[1]], l_refs[a].at[j], send[3 * a + j], recv[3 * a + j], (*ch, c))
                for a in range(n) for j, ch in enumerate(chips)]
    return make


def _place_own_call(shard, dev, name):
    R, C = shard.shape
    tr = _tile(R, max(16, MIB // (C * shard.dtype.itemsize)), 16)

    def body(d_ref, s_ref, o_ref):
        o_ref[...] = s_ref[...]

    grid_spec = pltpu.PrefetchScalarGridSpec(
        num_scalar_prefetch=1, grid=(R // tr,),
        in_specs=[pl.BlockSpec((tr, C), lambda r, d: (r, 0))],
        out_specs=pl.BlockSpec((None, tr, C), lambda r, d: (d[0], r, 0)))
    return pl.pallas_call(body, name=name, grid_spec=grid_spec,
                          out_shape=jax.ShapeDtypeStruct((N_DEV, R, C), shard.dtype), compiler_params=_cp(1))(dev, shard)


def _all_gather_call(shards, name):
    n = len(shards)

    def body(*refs):
        x_refs, o_refs = refs[:n], refs[n:2 * n]
        send_sems, recv_sems, local_sems = refs[2 * n:]
        x, y, c = _mesh_pos()
        me, sib = (x, y, c), (x, y, 1 - c)
        chips = [(1 - x, y), (x, 1 - y), (1 - x, 1 - y)]

        def copy(a, k, block, to, src=None):
            dst = o_refs[a].at[4 * block[0] + 2 * block[1] + block[2]]
            return pltpu.make_async_remote_copy(
                src_ref=dst if src is None else src, dst_ref=dst,
                send_sem=send_sems.at[a, k], recv_sem=recv_sems.at[a, k],
                device_id=to, device_id_type=pl.DeviceIdType.MESH)

        mine = [pltpu.make_async_copy(x_refs[a], o_refs[a].at[4 * x + 2 * y + c], local_sems.at[a]) for a in range(n)]
        for cp in mine:
            cp.start()
        first = []
        for a in range(n):
            first.append(copy(a, 0, me, sib, src=x_refs[a]))
            for j, chip in enumerate(chips):
                first.append(copy(a, 1 + j, me, (*chip, c), src=x_refs[a]))
        for cp in first:
            cp.start()
        passed = []
        for a in range(n):
            for j, chip in enumerate(chips):
                copy(a, 1 + j, (*chip, c), me).wait_recv()
                fwd = copy(a, 4 + j, (*chip, c), sib)
                fwd.start()
                passed.append(fwd)
        for a in range(n):
            copy(a, 0, sib, me).wait_recv()
            for j, chip in enumerate(chips):
                copy(a, 4 + j, (*chip, 1 - c), me).wait_recv()
        for cp in first + passed:
            cp.wait_send()
        for cp in mine:
            cp.wait()

    return pl.pallas_call(
        body, name=name,
        out_shape=[jax.ShapeDtypeStruct((N_DEV,) + s.shape, s.dtype) for s in shards],
        in_specs=_any_specs(n), out_specs=_any_specs(n),
        scratch_shapes=[pltpu.SemaphoreType.DMA((n, 7)), pltpu.SemaphoreType.DMA((n, 7)), pltpu.SemaphoreType.DMA((n,))],
    )(*shards)


def _pair_exchange_call(grads, name):
    n = len(grads)

    def body(*refs):
        g_refs, r_refs = refs[:n], refs[n:2 * n]
        send_sems, recv_sems = refs[2 * n:]
        x, y, c = _mesh_pos()
        copies = []
        for a in range(n):
            for k in range(4):
                copies.append(pltpu.make_async_remote_copy(
                    src_ref=g_refs[a].at[2 * k + 1 - c], dst_ref=r_refs[a].at[k],
                    send_sem=send_sems.at[a, k], recv_sem=recv_sems.at[a, k],
                    device_id=(x, y, 1 - c), device_id_type=pl.DeviceIdType.MESH))
        for cp in copies:
            cp.start()
        for cp in copies:
            cp.wait_recv()
        for cp in copies:
            cp.wait_send()

    return pl.pallas_call(
        body, name=name,
        out_shape=[jax.ShapeDtypeStruct((4,) + g.shape[1:], g.dtype) for g in grads],
        in_specs=_any_specs(n), out_specs=_any_specs(n),
        scratch_shapes=[pltpu.SemaphoreType.DMA((n, 4)), pltpu.SemaphoreType.DMA((n, 4))],
    )(*grads)


def _chip_exchange_call(parts, name):
    n = len(parts)

    def body(*refs):
        p_refs, r_refs = refs[:n], refs[n:2 * n]
        send_sems, recv_sems = refs[2 * n:]
        x, y, c = _mesh_pos()
        chips = [(1 - x, y), (x, 1 - y), (1 - x, 1 - y)]
        copies = []
        for a in range(n):
            for j, chip in enumerate(chips):
                copies.append(pltpu.make_async_remote_copy(
                    src_ref=p_refs[a].at[2 * chip[0] + chip[1]], dst_ref=r_refs[a].at[j],
                    send_sem=send_sems.at[a, j], recv_sem=recv_sems.at[a, j],
                    device_id=(*chip, c), device_id_type=pl.DeviceIdType.MESH))
        for cp in copies:
            cp.start()
        for cp in copies:
            cp.wait_recv()
        for cp in copies:
            cp.wait_send()

    return pl.pallas_call(
        body, name=name,
        out_shape=[jax.ShapeDtypeStruct((3,) + p.shape[1:], p.dtype) for p in parts],
        in_specs=_any_specs(n), out_specs=_any_specs(n),
        scratch_shapes=[pltpu.SemaphoreType.DMA((n, 3)), pltpu.SemaphoreType.DMA((n, 3))],
    )(*parts)


def _pair_add_call(g, r1, core_chip, name):
    _, R, C = g.shape
    tr = _tile(R, max(16, (2 * MIB) // (C * 4)), 16)

    def body(cc_ref, g_ref, r_ref, p32_ref, p16_ref):
        s = g_ref[...] + r_ref[...]
        p16_ref[...] = s.astype(BF16)

        @pl.when(pl.program_id(1) == cc_ref[1])
        def _():
            p32_ref[...] = s

    grid_spec = pltpu.PrefetchScalarGridSpec(
        num_scalar_prefetch=1, grid=(R // tr, 4),
        in_specs=[pl.BlockSpec((None, tr, C), lambda r, k, cc: (2 * k + cc[0], r, 0)),
                  pl.BlockSpec((None, tr, C), lambda r, k, cc: (k, r, 0))],
        out_specs=[pl.BlockSpec((tr, C), lambda r, k, cc: (r, 0)),
                   pl.BlockSpec((None, tr, C), lambda r, k, cc: (k, r, 0))])
    return pl.pallas_call(
        body, name=name, grid_spec=grid_spec,
        out_shape=[jax.ShapeDtypeStruct((R, C), F32), jax.ShapeDtypeStruct((4, R, C), BF16)],
        compiler_params=_cp(2))(core_chip, g, r1)


def _adamw_call(w, m, v, addends, name, deps=()):
    R, C = w.shape
    tr = _tile(R, max(8, (MIB // 2) // (C * 4)), 16)
    na = len(addends)
    c1 = 1.0 - ADAM_B1 ** ADAM_STEP
    c2 = 1.0 - ADAM_B2 ** ADAM_STEP

    def body(*refs):
        w_ref, m_ref, v_ref = refs[:3]
        a_refs = refs[3:3 + na]
        g_ref, d_ref, nm_ref, nv_ref = refs[3 + na:]
        g = a_refs[0][...].astype(F32)
        for a_ref in a_refs[1:]:
            g = g + a_ref[...].astype(F32)
        nm = ADAM_B1 * m_ref[...] + (1.0 - ADAM_B1) * g
        nv = ADAM_B2 * v_ref[...] + (1.0 - ADAM_B2) * (g * g)
        g_ref[...] = g
        nm_ref[...] = nm
        nv_ref[...] = nv
        d_ref[...] = -ADAM_LR * ((nm / c1) / (jnp.sqrt(nv / c2) + ADAM_EPS) + ADAM_WD * w_ref[...])

    flat = pl.BlockSpec((tr, C), lambda r: (r, 0))
    a_specs = [flat if k is None else pl.BlockSpec((None, tr, C), functools.partial(lambda r, kk: (kk, r, 0), kk=k))
               for _, k in addends]
    out = jax.ShapeDtypeStruct((R, C), F32)
    return _pallas(
        body, 3 + na, deps, name=name, grid=(R // tr,),
        in_specs=[flat, flat, flat] + a_specs, out_specs=[flat] * 4, out_shape=[out] * 4,
        compiler_params=_cp(1))(w, m, v, *[a for a, _ in addends], *deps)


def _sum_call(parts, row0, rows, name):
    n = parts.shape[0]
    tr = _tile(math.gcd(row0, rows), 256, 8)
    b0 = row0 // tr

    def body(p_ref, o_ref):
        s = p_ref[0]
        for k in range(1, n):
            s = s + p_ref[k]
        o_ref[...] = s

    return pl.pallas_call(
        body, name=name, grid=(rows // tr,),
        in_specs=[pl.BlockSpec((n, tr, LANES), lambda r: (0, r + b0, 0))],
        out_specs=pl.BlockSpec((tr, LANES), lambda r: (r, 0)),
        out_shape=jax.ShapeDtypeStruct((rows, LANES), F32), compiler_params=_cp(1))(parts)


def _rmsnorm_call(x, g, name, deps=()):
    S, D = x.shape
    tm = _tile(S, 512, 16)

    def body(x_ref, g_ref, o_ref):
        xv = x_ref[...]
        r = lax.rsqrt(_rows_mean(xv * xv) + RMS_EPS)
        o_ref[...] = (xv * r * g_ref[...]).astype(BF16)

    return _pallas(
        body, 2, deps, name=name, grid=(S // tm,),
        in_specs=[pl.BlockSpec((tm, D), lambda i: (i, 0)), pl.BlockSpec((1, D), lambda i: (0, 0))],
        out_specs=pl.BlockSpec((tm, D), lambda i: (i, 0)),
        out_shape=jax.ShapeDtypeStruct((S, D), BF16), compiler_params=_cp(1))(x, g, *deps)


def _mm_blocked_call(a, wg, out_dtype, halves, name, deps=()):
    S, K = a.shape
    nb, _, bn = wg.shape
    tm = _tile(S, 1024, 16)
    tn = _tile(bn, 768, LANES)
    nsub = bn // tn
    J = nb * nsub

    def body(a_ref, w_ref, o_ref):
        o_ref[...] = jnp.dot(a_ref[...], w_ref[...], preferred_element_type=F32).astype(out_dtype)

    if halves:
        nh = J // 2
        out_spec = pl.BlockSpec((None, tm, tn), lambda i, j: (j // nh, i, j % nh))
        out_shape = jax.ShapeDtypeStruct((2, S, nb * bn // 2), out_dtype)
    else:
        out_spec = pl.BlockSpec((tm, tn), lambda i, j: (i, j))
        out_shape = jax.ShapeDtypeStruct((S, nb * bn), out_dtype)
    return _pallas(
        body, 2, deps, name=name, grid=(S // tm, J),
        in_specs=[pl.BlockSpec((tm, K), lambda i, j: (i, 0)),
                  pl.BlockSpec((None, K, tn), lambda i, j: (j // nsub, 0, j % nsub))],
        out_specs=out_spec, out_shape=out_shape, compiler_params=_cp(2))(a, wg, *deps)


def _mm_out_call(x, y, w, name):
    S, D = x.shape
    tm = _tile(S, 512, 16)

    def body(x_ref, y_ref, w_ref, o_ref):
        o_ref[...] = x_ref[...] + jnp.dot(y_ref[...], w_ref[...], preferred_element_type=F32)

    return pl.pallas_call(
        body, name=name, grid=(S // tm,),
        in_specs=[pl.BlockSpec((tm, D), lambda i: (i, 0)), pl.BlockSpec((tm, D), lambda i: (i, 0)),
                  pl.BlockSpec((D, D), lambda i: (0, 0))],
        out_specs=pl.BlockSpec((tm, D), lambda i: (i, 0)),
        out_shape=jax.ShapeDtypeStruct((S, D), F32), compiler_params=_cp(1))(x, y, w)


def _mm_nt_call(a, w, out_dtype, name, deps=()):
    S, K = a.shape
    N = w.shape[0]
    tm = _tile(S, 1024, 16)
    tn = _tile(N, 768, LANES)

    def body(a_ref, w_ref, o_ref):
        o_ref[...] = lax.dot_general(a_ref[...], w_ref[...], _DN_NT, preferred_element_type=F32).astype(out_dtype)

    return _pallas(
        body, 2, deps, name=name, grid=(S // tm, N // tn),
        in_specs=[pl.BlockSpec((tm, K), lambda i, j: (i, 0)), pl.BlockSpec((tn, K), lambda i, j: (j, 0))],
        out_specs=pl.BlockSpec((tm, tn), lambda i, j: (i, j)),
        out_shape=jax.ShapeDtypeStruct((S, N), out_dtype), compiler_params=_cp(2))(a, w, *deps)


def _mm_down_loss_call(x2, f, w, final_g, target, name):
    S, D = x2.shape
    Fd = f.shape[1]
    tm = _tile(S, 512, 16)
    tk = _tile(Fd, 768, LANES)
    nk = Fd // tk

    def body(x_ref, f_ref, w_ref, g_ref, t_ref, dx_ref, dxb_ref, loss_ref, dg_ref, acc):
        i, k = pl.program_id(0), pl.program_id(1)

        @pl.when(jnp.logical_and(i == 0, k == 0))
        def _():
            loss_ref[...] = jnp.zeros_like(loss_ref)
            dg_ref[...] = jnp.zeros_like(dg_ref)

        @pl.when(k == 0)
        def _():
            acc[...] = jnp.zeros_like(acc)

        acc[...] += jnp.dot(f_ref[...], w_ref[...], preferred_element_type=F32)

        @pl.when(k == nk - 1)
        def _():
            x3 = x_ref[...] + acc[...]
            r = lax.rsqrt(_rows_mean(x3 * x3) + RMS_EPS)
            g = g_ref[...]
            xn = x3 * r
            diff = xn * g - t_ref[...]
            loss_ref[...] += 0.5 * jnp.sum(_rows_mean(diff * diff))
            dout = diff * (1.0 / D)
            dg_ref[...] += _col_sum(dout * xn)
            dyg = dout * g
            dx = r * (dyg - xn * _rows_mean(dyg * xn))
            dx_ref[...] = dx
            dxb_ref[...] = dx.astype(BF16)

    row = lambda i, k: (i, 0)
    return pl.pallas_call(
        body, name=name, grid=(S // tm, nk),
        in_specs=[pl.BlockSpec((tm, D), row), pl.BlockSpec((tm, tk), lambda i, k: (i, k)),
                  pl.BlockSpec((tk, D), lambda i, k: (k, 0)), pl.BlockSpec((1, D), lambda i, k: (0, 0)),
                  pl.BlockSpec((tm, D), row)],
        out_specs=[pl.BlockSpec((tm, D), row), pl.BlockSpec((tm, D), row),
                   pl.BlockSpec((8, LANES), lambda i, k: (0, 0)), pl.BlockSpec((1, D), lambda i, k: (0, 0))],
        out_shape=[jax.ShapeDtypeStruct((S, D), F32), jax.ShapeDtypeStruct((S, D), BF16),
                   jax.ShapeDtypeStruct((8, LANES), F32), jax.ShapeDtypeStruct((1, D), F32)],
        scratch_shapes=[pltpu.VMEM((tm, D), F32)], compiler_params=_cp(2, 56))(x2, f, w, final_g, target)


def _mm_dx_norm_call(a3, wg, resid, xin, g, name, deps=()):
    na, S, Fa = a3.shape
    nb, D, bn = wg.shape
    tm = _tile(S, 512, 16)
    tk = _tile(bn, 1536, LANES)
    nsub = bn // tk
    nka = Fa // tk
    nk = nb * nsub
    assert na * nka == nk

    def body(a_ref, w_ref, r_ref, x_ref, g_ref, dx_ref, dxb_ref, dg_ref, acc):
        i, k = pl.program_id(0), pl.program_id(1)

        @pl.when(jnp.logical_and(i == 0, k == 0))
        def _():
            dg_ref[...] = jnp.zeros_like(dg_ref)

        @pl.when(k == 0)
        def _():
            acc[...] = jnp.zeros_like(acc)

        acc[...] += lax.dot_general(a_ref[...], w_ref[...], _DN_NT, preferred_element_type=F32)

        @pl.when(k == nk - 1)
        def _():
            dh = acc[...]
            xv = x_ref[...]
            r = lax.rsqrt(_rows_mean(xv * xv) + RMS_EPS)
            xn = xv * r
            dg_ref[...] += _col_sum(dh * xn)
            dyg = dh * g_ref[...]
            dx = r_ref[...] + r * (dyg - xn * _rows_mean(dyg * xn))
            dx_ref[...] = dx
            dxb_ref[...] = dx.astype(BF16)

    row = lambda i, k: (i, 0)
    return _pallas(
        body, 5, deps, name=name, grid=(S // tm, nk),
        in_specs=[pl.BlockSpec((None, tm, tk), lambda i, k: (k // nka, i, k % nka)),
                  pl.BlockSpec((None, D, tk), lambda i, k: (k // nsub, 0, k % nsub)),
                  pl.BlockSpec((tm, D), row, pipeline_mode=pl.Buffered(1)),
                  pl.BlockSpec((tm, D), row, pipeline_mode=pl.Buffered(1)), pl.BlockSpec((1, D), lambda i, k: (0, 0))],
        out_specs=[pl.BlockSpec((tm, D), row), pl.BlockSpec((tm, D), row), pl.BlockSpec((1, D), lambda i, k: (0, 0))],
        out_shape=[jax.ShapeDtypeStruct((S, D), F32), jax.ShapeDtypeStruct((S, D), BF16),
                   jax.ShapeDtypeStruct((1, D), F32)],
        scratch_shapes=[pltpu.VMEM((tm, D), F32)], compiler_params=_cp(2, 56))(a3, wg, resid, xin, g, *deps)


def _unblock_call(wg, name):
    nb, K, bn = wg.shape

    def body(w_ref, o_ref):
        o_ref[...] = w_ref[...]

    return pl.pallas_call(
        body, name=name, grid=(nb,),
        in_specs=[pl.BlockSpec((None, K, bn), lambda o: (o, 0, 0))],
        out_specs=pl.BlockSpec((K, bn), lambda o: (0, o)),
        out_shape=jax.ShapeDtypeStruct((K, nb * bn), wg.dtype), compiler_params=_cp(1))(wg)


def _mm_nt_norm_call(a, w, resid, xin, g, name, deps=(), part=(0, 1), dx_so_far=None):
    S, K = a.shape
    D = w.shape[0]
    tm = _tile(S, 256, 16)
    tiles = (S // tm) // part[1]
    first = part[0] * tiles

    def body(a_ref, w_ref, r_ref, x_ref, g_ref, *rest):
        dx_ref, dg_ref = rest[-2:]

        @pl.when(pl.program_id(0) == 0)
        def _():
            dg_ref[...] = jnp.zeros_like(dg_ref)

        dh = lax.dot_general(a_ref[...], w_ref[...], _DN_NT, preferred_element_type=F32)
        xv = x_ref[...]
        r = lax.rsqrt(_rows_mean(xv * xv) + RMS_EPS)
        xn = xv * r
        dg_ref[...] += _col_sum(dh * xn)
        dyg = dh * g_ref[...]
        dx_ref[...] = r_ref[...] + r * (dyg - xn * _rows_mean(dyg * xn))

    row = lambda i: (i + first, 0)
    fixed = lambda i: (0, 0)
    in_specs = [pl.BlockSpec((tm, K), row), pl.BlockSpec((D, K), fixed, pipeline_mode=pl.Buffered(1)),
                pl.BlockSpec((tm, D), row), pl.BlockSpec((tm, D), row), pl.BlockSpec((1, D), fixed)]
    operands = [a, w, resid, xin, g]
    aliases = {}
    if dx_so_far is not None:
        in_specs.append(pl.BlockSpec(memory_space=pl.ANY))
        operands.append(dx_so_far)
        aliases = {5: 0}
    return _pallas(
        body, len(operands), deps, name=name, grid=(tiles,),
        in_specs=in_specs, out_specs=[pl.BlockSpec((tm, D), row), pl.BlockSpec((1, D), fixed)],
        out_shape=[jax.ShapeDtypeStruct((S, D), F32), jax.ShapeDtypeStruct((1, D), F32)],
        input_output_aliases=aliases, compiler_params=_cp(1, 56))(*operands, *deps)


def _mm_tn_cols_call(a, b3, nb, bn, name, deps=()):
    S, Ka = a.shape
    nh, _, Fb = b3.shape
    tm = _tile(S, 2048, 16)
    tn = _tile(bn, 768, LANES)
    nsub = bn // tn
    njb = Fb // tn
    J = nb * nsub
    assert nh * njb == J

    def body(a_ref, b_ref, o_ref):
        @pl.when(pl.program_id(1) == 0)
        def _():
            o_ref[...] = jnp.zeros_like(o_ref)

        o_ref[...] += lax.dot_general(a_ref[...], b_ref[...], _DN_TN, preferred_element_type=F32)

    return _pallas(
        body, 2, deps, name=name, grid=(J, S // tm),
        in_specs=[pl.BlockSpec((tm, Ka), lambda j, i: (i, 0)),
                  pl.BlockSpec((None, tm, tn), lambda j, i: (j // njb, i, j % njb))],
        out_specs=pl.BlockSpec((None, Ka, tn), lambda j, i: (j // nsub, 0, j % nsub)),
        out_shape=jax.ShapeDtypeStruct((nb, Ka, bn), F32), compiler_params=_cp(2, 56))(a, b3, *deps)


def _mm_tn_rows_call(a, b, name, deps=()):
    S, E = a.shape
    D = b.shape[1]
    tm = _tile(S, 2048, 16)
    te = _tile(E, 768, LANES)

    def body(a_ref, b_ref, o_ref):
        @pl.when(pl.program_id(1) == 0)
        def _():
            o_ref[...] = jnp.zeros_like(o_ref)

        o_ref[...] += lax.dot_general(a_ref[...], b_ref[...], _DN_TN, preferred_element_type=F32)

    return _pallas(
        body, 2, deps, name=name, grid=(E // te, S // tm),
        in_specs=[pl.BlockSpec((tm, te), lambda j, i: (i, j)), pl.BlockSpec((tm, D), lambda j, i: (i, 0))],
        out_specs=pl.BlockSpec((te, D), lambda j, i: (j, 0)),
        out_shape=jax.ShapeDtypeStruct((E, D), F32), compiler_params=_cp(2, 56))(a, b, *deps)


def _ffn_tiles(S, Fd):
    return _tile(S, 512, 16), _tile(Fd, 768, LANES)


def _taps(cw_ref):
    return [cw_ref[k:k + 1, :] for k in range(cw_ref.shape[0])]


def _conv3(prev8, cur, taps):
    s1 = _shift_down(prev8, cur, 1)
    s2 = _shift_down(prev8, cur, 2)
    return taps[2] * cur + taps[1] * s1 + taps[0] * s2, s1, s2


def _ffn_act_call(up3, cw, cb, name):
    _, S, Fd = up3.shape
    tm, tc = _ffn_tiles(S, Fd)
    nj = Fd // tc
    hb = tm // 16

    def body(g_ref, v_ref, gp_ref, vp_ref, cwg_ref, cwv_ref, cbg_ref, cbv_ref, o_ref, c_ref):
        first = pl.program_id(0) == 0
        keep = jnp.where(first, 0.0, 1.0)
        gp = gp_ref[...].astype(F32)[8:] * keep
        vp = vp_ref[...].astype(F32)[8:] * keep
        cg, _, _ = _conv3(gp, g_ref[...].astype(F32), _taps(cwg_ref))
        cv, _, _ = _conv3(vp, v_ref[...].astype(F32), _taps(cwv_ref))
        cg = cg + cbg_ref[...]
        cv = cv + cbv_ref[...]
        o_ref[...] = (_gelu(cg) * cv).astype(BF16)
        c_ref[0] = cg.astype(BF16)
        c_ref[1] = cv.astype(BF16)

    prev = lambda i, j: (jnp.maximum(i * hb - 1, 0), j)
    return pl.pallas_call(
        body, name=name, grid=(S // tm, nj),
        in_specs=[pl.BlockSpec((None, tm, tc), lambda i, j: (0, i, j)), pl.BlockSpec((None, tm, tc), lambda i, j: (1, i, j)),
                  pl.BlockSpec((None, 16, tc), lambda i, j: (0,) + prev(i, j)),
                  pl.BlockSpec((None, 16, tc), lambda i, j: (1,) + prev(i, j)),
                  pl.BlockSpec((3, tc), lambda i, j: (0, j)), pl.BlockSpec((3, tc), lambda i, j: (0, j + nj)),
                  pl.BlockSpec((1, tc), lambda i, j: (0, j)), pl.BlockSpec((1, tc), lambda i, j: (0, j + nj))],
        out_specs=[pl.BlockSpec((tm, tc), lambda i, j: (i, j)), pl.BlockSpec((2, tm, tc), lambda i, j: (0, i, j))],
        out_shape=[jax.ShapeDtypeStruct((S, Fd), BF16), jax.ShapeDtypeStruct((2, S, Fd), BF16)],
        compiler_params=_cp(2))(up3, up3, up3, up3, cw, cw, cb, cb)


def _ffn_act_bwd_call(up3, upc3, d_f, cw, name):
    _, S, Fd = up3.shape
    tm, tc = _ffn_tiles(S, Fd)
    nj = Fd // tc
    ni = S // tm
    hb = tm // 16

    def body(g_ref, v_ref, cg_ref, cv_ref, cgn_ref, cvn_ref, df_ref, dfn_ref, cwg_ref, cwv_ref,
             dup_ref, dcwg_ref, dcwv_ref, dcbg_ref, dcbv_ref):
        i = pl.program_id(1)
        keep_next = jnp.where(i == ni - 1, 0.0, 1.0)

        @pl.when(i == 0)
        def _():
            for r in (dcwg_ref, dcwv_ref, dcbg_ref, dcbv_ref):
                r[...] = jnp.zeros_like(r)

        df = jnp.concatenate([df_ref[...].astype(F32), dfn_ref[...].astype(F32)[:8] * keep_next], axis=0)
        cg = jnp.concatenate([cg_ref[...].astype(F32), cgn_ref[...].astype(F32)[:8]], axis=0)
        cv = jnp.concatenate([cv_ref[...].astype(F32), cvn_ref[...].astype(F32)[:8]], axis=0)
        gel, dgel = _gelu_parts(cg)

        def back(d, cw_ref, x_ref, dcw_ref, dcb_ref, slab):
            taps = _taps(cw_ref)
            d0 = d[:tm]
            d1 = pltpu.roll(d, tm + 8 - 1, 0)[:tm]
            d2 = pltpu.roll(d, tm + 8 - 2, 0)[:tm]
            dup_ref[slab] = (taps[2] * d0 + taps[1] * d1 + taps[0] * d2).astype(BF16)
            xv = x_ref[...].astype(F32)
            dcw_ref[2:3, :] += _col_sum(xv * d0)
            dcw_ref[1:2, :] += _col_sum(xv * d1)
            dcw_ref[0:1, :] += _col_sum(xv * d2)
            dcb_ref[...] += _col_sum(d0)

        back(df * cv * dgel, cwg_ref, g_ref, dcwg_ref, dcbg_ref, 0)
        back(df * gel, cwv_ref, v_ref, dcwv_ref, dcbv_ref, 1)

    nxt = lambda j, i: (jnp.minimum((i + 1) * hb, S // 16 - 1), j)
    main = lambda s: pl.BlockSpec((None, tm, tc), lambda j, i: (s, i, j))
    halo = lambda s: pl.BlockSpec((None, 16, tc), lambda j, i: (s,) + nxt(j, i))
    acc3 = pl.BlockSpec((3, tc), lambda j, i: (0, j))
    acc1 = pl.BlockSpec((1, tc), lambda j, i: (0, j))
    return pl.pallas_call(
        body, name=name, grid=(nj, ni),
        in_specs=[main(0), main(1), main(0), main(1), halo(0), halo(1),
                  pl.BlockSpec((tm, tc), lambda j, i: (i, j)), pl.BlockSpec((16, tc), nxt),
                  pl.BlockSpec((3, tc), lambda j, i: (0, j)), pl.BlockSpec((3, tc), lambda j, i: (0, j + nj))],
        out_specs=[pl.BlockSpec((2, tm, tc), lambda j, i: (0, i, j)), acc3, acc3, acc1, acc1],
        out_shape=[jax.ShapeDtypeStruct((2, S, Fd), BF16), jax.ShapeDtypeStruct((3, Fd), F32),
                   jax.ShapeDtypeStruct((3, Fd), F32), jax.ShapeDtypeStruct((1, Fd), F32),
                   jax.ShapeDtypeStruct((1, Fd), F32)],
        compiler_params=_cp(2, 56))(up3, up3, upc3, upc3, upc3, upc3, d_f, d_f, cw, cw)


SUB_LANES = 256


def _ffn_up_act_call(h2, wg, cw, cb, name, deps=()):
    S, D = h2.shape
    nb, _, bn = wg.shape
    Fd = nb * bn // 2
    tm, tc = _ffn_tiles(S, Fd)
    nk = Fd // tc
    hb = tm // 16
    nsubw = bn // tc
    half = nb // 2
    sc = _tile(tc, SUB_LANES, LANES)

    def body(a_ref, ap_ref, wgate_ref, wval_ref, cwg_ref, cwv_ref, cbg_ref, cbv_ref, up_ref, upc_ref, f_ref):
        keep = jnp.where(pl.program_id(0) == 0, 0.0, 1.0)
        a_ext = jnp.concatenate([ap_ref[...], a_ref[...]], axis=0)
        nsub = tc // sc
        lanes = [slice(s * sc, (s + 1) * sc) for s in range(nsub)]

        def products(s):
            return [jnp.dot(a_ext, w_ref[:, lanes[s]], preferred_element_type=F32).astype(BF16)
                    for w_ref in (wgate_ref, wval_ref)]

        ready = products(0)
        for s in range(nsub):
            ls = lanes[s]
            following = products(s + 1) if s + 1 < nsub else None

            def conv_half(ub, cw_ref, cb_ref, slab):
                up_ref[slab, :, ls] = ub[16:]
                u = ub.astype(F32)
                conv, _, _ = _conv3(u[8:16] * keep, u[16:], _lane_taps(cw_ref, ls))
                c = conv + cb_ref[:, ls]
                upc_ref[slab, :, ls] = c.astype(BF16)
                return c

            cg = conv_half(ready[0], cwg_ref, cbg_ref, 0)
            cv = conv_half(ready[1], cwv_ref, cbv_ref, 1)
            f_ref[:, ls] = (_gelu(cg) * cv).astype(BF16)
            ready = following

    return _pallas(
        body, 8, deps, name=name, grid=(S // tm, nk),
        in_specs=[pl.BlockSpec((tm, D), lambda i, k: (i, 0)),
                  pl.BlockSpec((16, D), lambda i, k: (jnp.maximum(i * hb - 1, 0), 0)),
                  pl.BlockSpec((None, D, tc), lambda i, k: (k // nsubw, 0, k % nsubw)),
                  pl.BlockSpec((None, D, tc), lambda i, k: (half + k // nsubw, 0, k % nsubw)),
                  pl.BlockSpec((3, tc), lambda i, k: (0, k)), pl.BlockSpec((3, tc), lambda i, k: (0, k + nk)),
                  pl.BlockSpec((1, tc), lambda i, k: (0, k)), pl.BlockSpec((1, tc), lambda i, k: (0, k + nk))],
        out_specs=[pl.BlockSpec((2, tm, tc), lambda i, k: (0, i, k)), pl.BlockSpec((2, tm, tc), lambda i, k: (0, i, k)),
                   pl.BlockSpec((tm, tc), lambda i, k: (i, k))],
        out_shape=[jax.ShapeDtypeStruct((2, S, Fd), BF16), jax.ShapeDtypeStruct((2, S, Fd), BF16),
                   jax.ShapeDtypeStruct((S, Fd), BF16)],
        compiler_params=_cp(2, 56))(h2, h2, wg, wg, cw, cw, cb, cb, *deps)


def _ffn_down_dx_act_bwd_call(dxb, w, up3, upc3, cw, name, deps=()):
    S, D = dxb.shape
    _, _, Fd = up3.shape
    tm, tc = _ffn_tiles(S, Fd)
    nj = Fd // tc
    ni = S // tm
    hb = tm // 16
    sc = _tile(tc, SUB_LANES, LANES)

    def body(a_ref, an_ref, w_ref, g_ref, v_ref, cg_ref, cv_ref, cgn_ref, cvn_ref, cwg_ref, cwv_ref,
             dup_ref, dcwg_ref, dcwv_ref, dcbg_ref, dcbv_ref):
        i = pl.program_id(1)
        keep_next = jnp.where(i == ni - 1, 0.0, 1.0)

        @pl.when(i == 0)
        def _():
            for r in (dcwg_ref, dcwv_ref, dcbg_ref, dcbv_ref):
                r[...] = jnp.zeros_like(r)

        a_ext = jnp.concatenate([a_ref[...], an_ref[...]], axis=0)
        for s in range(tc // sc):
            ls = slice(s * sc, (s + 1) * sc)
            df_ext = lax.dot_general(a_ext, w_ref[s * sc:(s + 1) * sc, :], _DN_NT, preferred_element_type=F32)
            df = jnp.concatenate([df_ext[:tm], df_ext[tm:tm + 8] * keep_next], axis=0)
            cg = jnp.concatenate([cg_ref[:, ls].astype(F32), cgn_ref[:, ls].astype(F32)[:8]], axis=0)
            cv = jnp.concatenate([cv_ref[:, ls].astype(F32), cvn_ref[:, ls].astype(F32)[:8]], axis=0)
            gel, dgel = _gelu_parts(cg)

            def back(d, cw_ref, x_ref, dcw_ref, dcb_ref, slab):
                taps = _lane_taps(cw_ref, ls)
                d0 = d[:tm]
                d1 = pltpu.roll(d, tm + 8 - 1, 0)[:tm]
                d2 = pltpu.roll(d, tm + 8 - 2, 0)[:tm]
                dup_ref[slab, :, ls] = (taps[2] * d0 + taps[1] * d1 + taps[0] * d2).astype(BF16)
                xv = x_ref[:, ls].astype(F32)
                dcw_ref[2:3, ls] += _col_sum(xv * d0)
                dcw_ref[1:2, ls] += _col_sum(xv * d1)
                dcw_ref[0:1, ls] += _col_sum(xv * d2)
                dcb_ref[:, ls] += _col_sum(d0)

            back(df * cv * dgel, cwg_ref, g_ref, dcwg_ref, dcbg_ref, 0)
            back(df * gel, cwv_ref, v_ref, dcwv_ref, dcbv_ref, 1)

    nxt = lambda j, i: jnp.minimum((i + 1) * hb, S // 16 - 1)
    main = lambda s: pl.BlockSpec((None, tm, tc), lambda j, i: (s, i, j))
    halo = lambda s: pl.BlockSpec((None, 16, tc), lambda j, i: (s, nxt(j, i), j))
    acc3 = pl.BlockSpec((3, tc), lambda j, i: (0, j))
    acc1 = pl.BlockSpec((1, tc), lambda j, i: (0, j))
    return _pallas(
        body, 11, deps, name=name, grid=(nj, ni),
        in_specs=[pl.BlockSpec((tm, D), lambda j, i: (i, 0)), pl.BlockSpec((16, D), lambda j, i: (nxt(j, i), 0)),
                  pl.BlockSpec((tc, D), lambda j, i: (j, 0)),
                  main(0), main(1), main(0), main(1), halo(0), halo(1),
                  pl.BlockSpec((3, tc), lambda j, i: (0, j)), pl.BlockSpec((3, tc), lambda j, i: (0, j + nj))],
        out_specs=[pl.BlockSpec((2, tm, tc), lambda j, i: (0, i, j)), acc3, acc3, acc1, acc1],
        out_shape=[jax.ShapeDtypeStruct((2, S, Fd), BF16), jax.ShapeDtypeStruct((3, Fd), F32),
                   jax.ShapeDtypeStruct((3, Fd), F32), jax.ShapeDtypeStruct((1, Fd), F32),
                   jax.ShapeDtypeStruct((1, Fd), F32)],
        compiler_params=_cp(2, 56))(dxb, dxb, w, up3, up3, upc3, upc3, upc3, upc3, cw, cw, *deps)


def _lane_taps(cw_ref, ls):
    return [cw_ref[k:k + 1, ls] for k in range(cw_ref.shape[0])]


def _ffn_down_fused_call(up3, cw, cb, w, x2, final_g, target, name):
    _, S, Fd = up3.shape
    D = x2.shape[1]
    tm, tc = _ffn_tiles(S, Fd)
    nk = Fd // tc
    hb = tm // 16
    sc = _tile(tc, SUB_LANES, LANES)

    def body(g_ref, v_ref, gp_ref, vp_ref, cwg_ref, cwv_ref, cbg_ref, cbv_ref, w_ref, x_ref, fg_ref, t_ref,
             f_ref, dx_ref, dxb_ref, loss_ref, dg_ref, acc):
        i, k = pl.program_id(0), pl.program_id(1)

        @pl.when(jnp.logical_and(i == 0, k == 0))
        def _():
            loss_ref[...] = jnp.zeros_like(loss_ref)
            dg_ref[...] = jnp.zeros_like(dg_ref)

        @pl.when(k == 0)
        def _():
            acc[...] = jnp.zeros_like(acc)

        keep = jnp.where(i == 0, 0.0, 1.0)
        part = None
        for s in range(tc // sc):
            ls = slice(s * sc, (s + 1) * sc)
            gp = gp_ref[:, ls].astype(F32)[8:] * keep
            vp = vp_ref[:, ls].astype(F32)[8:] * keep
            cg, _, _ = _conv3(gp, g_ref[:, ls].astype(F32), _lane_taps(cwg_ref, ls))
            cv, _, _ = _conv3(vp, v_ref[:, ls].astype(F32), _lane_taps(cwv_ref, ls))
            fs = (_gelu(cg + cbg_ref[:, ls]) * (cv + cbv_ref[:, ls])).astype(BF16)
            f_ref[:, ls] = fs
            d = jnp.dot(fs, w_ref[s * sc:(s + 1) * sc, :], preferred_element_type=F32)
            part = d if part is None else part + d
        acc[...] += part

        @pl.when(k == nk - 1)
        def _():
            x3 = x_ref[...] + acc[...]
            r = lax.rsqrt(_rows_mean(x3 * x3) + RMS_EPS)
            g = fg_ref[...]
            xn = x3 * r
            diff = xn * g - t_ref[...]
            loss_ref[...] += 0.5 * jnp.sum(_rows_mean(diff * diff))
            dout = diff * (1.0 / D)
            dg_ref[...] += _col_sum(dout * xn)
            dyg = dout * g
            dx = r * (dyg - xn * _rows_mean(dyg * xn))
            dx_ref[...] = dx
            dxb_ref[...] = dx.astype(BF16)

    row = lambda i, k: (i, 0)
    prev = lambda i, k: (jnp.maximum(i * hb - 1, 0), k)
    once = pl.Buffered(1)
    return pl.pallas_call(
        body, name=name, grid=(S // tm, nk),
        in_specs=[pl.BlockSpec((None, tm, tc), lambda i, k: (0, i, k)), pl.BlockSpec((None, tm, tc), lambda i, k: (1, i, k)),
                  pl.BlockSpec((None, 16, tc), lambda i, k: (0,) + prev(i, k)),
                  pl.BlockSpec((None, 16, tc), lambda i, k: (1,) + prev(i, k)),
                  pl.BlockSpec((3, tc), lambda i, k: (0, k)), pl.BlockSpec((3, tc), lambda i, k: (0, k + nk)),
                  pl.BlockSpec((1, tc), lambda i, k: (0, k)), pl.BlockSpec((1, tc), lambda i, k: (0, k + nk)),
                  pl.BlockSpec((tc, D), lambda i, k: (k, 0)),
                  pl.BlockSpec((tm, D), row, pipeline_mode=once), pl.BlockSpec((1, D), lambda i, k: (0, 0)),
                  pl.BlockSpec((tm, D), row, pipeline_mode=once)],
        out_specs=[pl.BlockSpec((tm, tc), lambda i, k: (i, k)), pl.BlockSpec((tm, D), row), pl.BlockSpec((tm, D), row),
                   pl.BlockSpec((8, LANES), lambda i, k: (0, 0)), pl.BlockSpec((1, D), lambda i, k: (0, 0))],
        out_shape=[jax.ShapeDtypeStruct((S, Fd), BF16), jax.ShapeDtypeStruct((S, D), F32),
                   jax.ShapeDtypeStruct((S, D), BF16), jax.ShapeDtypeStruct((8, LANES), F32),
                   jax.ShapeDtypeStruct((1, D), F32)],
        scratch_shapes=[pltpu.VMEM((tm, D), F32)],
        compiler_params=_cp(2, 56))(up3, up3, up3, up3, cw, cw, cb, cb, w, x2, final_g, target)


def _ffn_bwd_fused_call(up3, d_f, cw, cb, wg, resid, xin, g, name, deps=()):
    _, S, Fd = up3.shape
    nb, D, bn = wg.shape
    tm, tc = _tile(S, 256, 16), _ffn_tiles(S, Fd)[1]
    nk = Fd // tc
    ni = S // tm
    hb = tm // 16
    nsubw = bn // tc
    half = nb // 2
    sc = _tile(tc, SUB_LANES, LANES)

    def body(g_ref, v_ref, gp_ref, vp_ref, gn_ref, vn_ref, df_ref, dfn_ref, cwg_ref, cwv_ref, cbg_ref, cbv_ref,
             wgate_ref, wval_ref, r_ref, x_ref, ng_ref,
             dup_ref, dx_ref, dxb_ref, dg_ref, dcwg_ref, dcwv_ref, dcbg_ref, dcbv_ref, acc):
        i, k = pl.program_id(0), pl.program_id(1)

        @pl.when(jnp.logical_and(i == 0, k == 0))
        def _():
            for r in (dg_ref, dcwg_ref, dcwv_ref, dcbg_ref, dcbv_ref):
                r[...] = jnp.zeros_like(r)

        @pl.when(k == 0)
        def _():
            acc[...] = jnp.zeros_like(acc)

        keep_prev = jnp.where(i == 0, 0.0, 1.0)
        keep_next = jnp.where(i == ni - 1, 0.0, 1.0)
        zeros8 = jnp.zeros((8, sc), F32)
        part = None
        for s in range(tc // sc):
            ls = slice(s * sc, (s + 1) * sc)
            off = pl.multiple_of(k * tc + s * sc, LANES)
            df = jnp.concatenate([df_ref[:, ls].astype(F32), dfn_ref[:, ls].astype(F32)[:8] * keep_next], axis=0)

            def half_fwd(x_ref_, xp_ref, xn_ref, cw_ref, cb_ref):
                taps = _lane_taps(cw_ref, ls)
                prev8 = xp_ref[:, ls].astype(F32)[8:] * keep_prev
                ext = jnp.concatenate([x_ref_[:, ls].astype(F32), xn_ref[:, ls].astype(F32)[:8]], axis=0)
                conv, s1, s2 = _conv3(prev8, ext, taps)
                return taps, conv + cb_ref[:, ls], (ext, s1, s2)

            tg, cg, gsh = half_fwd(g_ref, gp_ref, gn_ref, cwg_ref, cbg_ref)
            tv, cv, vsh = half_fwd(v_ref, vp_ref, vn_ref, cwv_ref, cbv_ref)
            gel, dgel = _gelu_parts(cg)
            d_gate = df * cv * dgel
            d_val = df * gel

            def half_bwd(d, taps, shifts, dcw_ref, dcb_ref, slab, w_ref):
                dup = (taps[2] * d + taps[1] * _shift_up(d, zeros8, 1) + taps[0] * _shift_up(d, zeros8, 2))[:tm]
                dupb = dup.astype(BF16)
                dup_ref[slab, :, ls] = dupb
                dm = d[:tm]
                ext, s1, s2 = shifts
                dcw_ref[2:3, pl.ds(off, sc)] += _col_sum(ext[:tm] * dm)
                dcw_ref[1:2, pl.ds(off, sc)] += _col_sum(s1[:tm] * dm)
                dcw_ref[0:1, pl.ds(off, sc)] += _col_sum(s2[:tm] * dm)
                dcb_ref[:, pl.ds(off, sc)] += _col_sum(dm)
                return lax.dot_general(dupb, w_ref[:, ls], _DN_NT, preferred_element_type=F32)

            d = (half_bwd(d_gate, tg, gsh, dcwg_ref, dcbg_ref, 0, wgate_ref)
                 + half_bwd(d_val, tv, vsh, dcwv_ref, dcbv_ref, 1, wval_ref))
            part = d if part is None else part + d
        acc[...] += part

        @pl.when(k == nk - 1)
        def _():
            dh = acc[...]
            xv = x_ref[...]
            r = lax.rsqrt(_rows_mean(xv * xv) + RMS_EPS)
            xn = xv * r
            dg_ref[...] += _col_sum(dh * xn)
            dyg = dh * ng_ref[...]
            dx = r_ref[...] + r * (dyg - xn * _rows_mean(dyg * xn))
            dx_ref[...] = dx
            dxb_ref[...] = dx.astype(BF16)

    row = lambda i, k: (i, 0)
    prev = lambda i, k: (jnp.maximum(i * hb - 1, 0), k)
    nxt = lambda i, k: (jnp.minimum((i + 1) * hb, S // 16 - 1), k)
    main = lambda s: pl.BlockSpec((None, tm, tc), lambda i, k: (s, i, k))
    halo = lambda s, f: pl.BlockSpec((None, 16, tc), lambda i, k: (s,) + f(i, k))
    full3 = pl.BlockSpec((3, Fd), lambda i, k: (0, 0))
    full1 = pl.BlockSpec((1, Fd), lambda i, k: (0, 0))
    once = pl.Buffered(1)
    return _pallas(
        body, 17, deps, name=name, grid=(ni, nk),
        in_specs=[main(0), main(1), halo(0, prev), halo(1, prev), halo(0, nxt), halo(1, nxt),
                  pl.BlockSpec((tm, tc), lambda i, k: (i, k)), pl.BlockSpec((16, tc), nxt),
                  pl.BlockSpec((3, tc), lambda i, k: (0, k)), pl.BlockSpec((3, tc), lambda i, k: (0, k + nk)),
                  pl.BlockSpec((1, tc), lambda i, k: (0, k)), pl.BlockSpec((1, tc), lambda i, k: (0, k + nk)),
                  pl.BlockSpec((None, D, tc), lambda i, k: (k // nsubw, 0, k % nsubw)),
                  pl.BlockSpec((None, D, tc), lambda i, k: (half + k // nsubw, 0, k % nsubw)),
                  pl.BlockSpec((tm, D), row, pipeline_mode=once), pl.BlockSpec((tm, D), row, pipeline_mode=once),
                  pl.BlockSpec((1, D), lambda i, k: (0, 0))],
        out_specs=[pl.BlockSpec((2, tm, tc), lambda i, k: (0, i, k)), pl.BlockSpec((tm, D), row),
                   pl.BlockSpec((tm, D), row), pl.BlockSpec((1, D), lambda i, k: (0, 0)), full3, full3, full1, full1],
        out_shape=[jax.ShapeDtypeStruct((2, S, Fd), BF16), jax.ShapeDtypeStruct((S, D), F32),
                   jax.ShapeDtypeStruct((S, D), BF16), jax.ShapeDtypeStruct((1, D), F32),
                   jax.ShapeDtypeStruct((3, Fd), F32), jax.ShapeDtypeStruct((3, Fd), F32),
                   jax.ShapeDtypeStruct((1, Fd), F32), jax.ShapeDtypeStruct((1, Fd), F32)],
        scratch_shapes=[pltpu.VMEM((tm, D), F32)],
        compiler_params=_cp(2, 60))(up3, up3, up3, up3, up3, up3, d_f, d_f, cw, cw, cb, cb, wg, wg, resid, xin, g, *deps)


def _gm_forward_tile(pv, vg, vb, ws_ref, bsb_ref, mbuf, H, nc):
    W = H * CHUNK
    z, dz = _gelu_parts(pv)
    u, v0 = z[:, :W], z[:, W:]
    xc = v0 - _rows_mean(v0)
    rs = lax.rsqrt(_rows_mean(xc * xc) + LN_EPS)
    vh = xc * rs
    vnb = (vh * vg + vb).astype(BF16)
    mask = lax.broadcasted_iota(jnp.int32, (CHUNK, CHUNK), 0) >= lax.broadcasted_iota(jnp.int32, (CHUNK, CHUNK), 1)
    for h in range(H):
        cs = slice(h * CHUNK, (h + 1) * CHUNK)
        wm = jnp.where(mask, ws_ref[h], 0.0).astype(BF16)
        vcat = jnp.concatenate([vnb[c * CHUNK:(c + 1) * CHUNK, cs] for c in range(nc)], axis=1)
        mix = jnp.dot(wm, vcat, preferred_element_type=F32)
        for c in range(nc):
            mbuf[c * CHUNK:(c + 1) * CHUNK, cs] = mix[:, c * CHUNK:(c + 1) * CHUNK] + bsb_ref[h]
    return dz, u, vh, rs, vnb, mask


def _gm_fwd_call(p, v_g, v_b, ws, bsb, out_g, name, deps=()):
    S = p.shape[0]
    H = ws.shape[0]
    W = H * CHUNK
    tm = _tile(S, 256, CHUNK)
    nc = tm // CHUNK

    def body(p_ref, vg_ref, vb_ref, ws_ref, bsb_ref, og_ref, y_ref, mbuf):
        _, u, _, _, _, _ = _gm_forward_tile(p_ref[...], vg_ref[...], vb_ref[...], ws_ref, bsb_ref, mbuf, H, nc)
        yg = u * mbuf[...]
        r = lax.rsqrt(_rows_mean(yg * yg) + RMS_EPS)
        y_ref[...] = (yg * r * og_ref[...]).astype(BF16)

    vec = pl.BlockSpec((1, W), lambda i: (0, 0))
    mat = pl.BlockSpec((H, CHUNK, CHUNK), lambda i: (0, 0, 0))
    return _pallas(
        body, 6, deps, name=name, grid=(S // tm,),
        in_specs=[pl.BlockSpec((tm, 2 * W), lambda i: (i, 0)), vec, vec, mat, mat, vec],
        out_specs=pl.BlockSpec((tm, W), lambda i: (i, 0)),
        out_shape=jax.ShapeDtypeStruct((S, 2 * W), BF16),
        scratch_shapes=[pltpu.VMEM((tm, W), F32)], compiler_params=_cp(1))(p, v_g, v_b, ws, bsb, out_g, *deps)


def _gm_bwd_call(p, d_y, v_g, v_b, ws, bsb, out_g, name, deps=()):
    S = p.shape[0]
    H = ws.shape[0]
    W = H * CHUNK
    tm = _tile(S, 256, CHUNK)
    nc = tm // CHUNK
    ni = S // tm

    def body(p_ref, dy_ref, vg_ref, vb_ref, ws_ref, bsb_ref, og_ref,
             dp_ref, dvg_ref, dvb_ref, dws_ref, dbs_ref, dog_ref, mbuf, dvbuf):
        i = pl.program_id(0)

        @pl.when(i == 0)
        def _():
            for r in (dvg_ref, dvb_ref, dws_ref, dbs_ref, dog_ref):
                r[...] = jnp.zeros_like(r)

        vg = vg_ref[...]
        dz, u, vh, rs, vnb, mask = _gm_forward_tile(p_ref[...], vg, vb_ref[...], ws_ref, bsb_ref, mbuf, H, nc)
        mixed = mbuf[...]
        yg = u * mixed
        r = lax.rsqrt(_rows_mean(yg * yg) + RMS_EPS)
        yn = yg * r
        dya = dy_ref[...]
        dog_ref[...] += _col_sum(dya * yn)
        dyg = dya * og_ref[...]
        dygm = r * (dyg - yn * _rows_mean(dyg * yn))
        du = dygm * mixed
        dmix = dygm * u
        dmb = dmix.astype(BF16)
        for h in range(H):
            cs = slice(h * CHUNK, (h + 1) * CHUNK)
            wm = jnp.where(mask, ws_ref[h], 0.0).astype(BF16)
            dcat = jnp.concatenate([dmb[c * CHUNK:(c + 1) * CHUNK, cs] for c in range(nc)], axis=1)
            vcat = jnp.concatenate([vnb[c * CHUNK:(c + 1) * CHUNK, cs] for c in range(nc)], axis=1)
            dvn = lax.dot_general(wm, dcat, _DN_TN, preferred_element_type=F32)
            dws_ref[h] += jnp.where(mask, lax.dot_general(dcat, vcat, _DN_NT, preferred_element_type=F32), 0.0)
            dbs = dmix[0:CHUNK, cs]
            for c in range(1, nc):
                dbs = dbs + dmix[c * CHUNK:(c + 1) * CHUNK, cs]
            dbs_ref[h] += dbs
            for c in range(nc):
                dvbuf[c * CHUNK:(c + 1) * CHUNK, cs] = dvn[:, c * CHUNK:(c + 1) * CHUNK]
        dvn_all = dvbuf[...]
        dvg_ref[...] += _col_sum(dvn_all * vh)
        dvb_ref[...] += _col_sum(dvn_all)
        dvh = dvn_all * vg
        dv0 = rs * (dvh - _rows_mean(dvh) - vh * _rows_mean(dvh * vh))
        dp_ref[...] = (jnp.concatenate([du, dv0], axis=1) * dz).astype(BF16)

        @pl.when(i == ni - 1)
        def _():
            for h in range(H):
                dbs_ref[h] = jnp.broadcast_to(jnp.sum(dbs_ref[h], axis=1, keepdims=True), (CHUNK, CHUNK))

    vec = pl.BlockSpec((1, W), lambda i: (0, 0))
    mat = pl.BlockSpec((H, CHUNK, CHUNK), lambda i: (0, 0, 0))
    vshape = jax.ShapeDtypeStruct((1, W), F32)
    mshape = jax.ShapeDtypeStruct((H, CHUNK, CHUNK), F32)
    return _pallas(
        body, 7, deps, name=name, grid=(ni,),
        in_specs=[pl.BlockSpec((tm, 2 * W), lambda i: (i, 0)), pl.BlockSpec((tm, W), lambda i: (i, 0)),
                  vec, vec, mat, mat, vec],
        out_specs=[pl.BlockSpec((tm, 2 * W), lambda i: (i, 0)), vec, vec, mat, mat, vec],
        out_shape=[jax.ShapeDtypeStruct((S, 4 * W), BF16), vshape, vshape, mshape, mshape, vshape],
        scratch_shapes=[pltpu.VMEM((tm, W), F32), pltpu.VMEM((tm, W), F32)],
        compiler_params=_cp(1))(p, d_y, v_g, v_b, ws, bsb, out_g, *deps)


def _lru_gates(prev8, xl, cw, cb, wa_ref, ba, wx_ref, bx, lam, H):
    sh = [_shift_down(prev8, xl, k) for k in range(4)]
    xr = cw[3] * sh[0] + cw[2] * sh[1] + cw[1] * sh[2] + cw[0] * sh[3] + cb
    xrb = xr.astype(BF16)
    rp, ip = [], []
    for h in range(H):
        cs = slice(h * CHUNK, (h + 1) * CHUNK)
        rp.append(jnp.dot(xrb[:, cs], wa_ref[h].astype(BF16), preferred_element_type=F32))
        ip.append(jnp.dot(xrb[:, cs], wx_ref[h].astype(BF16), preferred_element_type=F32))
    r = _sigmoid(jnp.concatenate(rp, axis=1) + ba)
    ig = _sigmoid(jnp.concatenate(ip, axis=1) + bx)
    sp = _softplus(-lam)
    t = jnp.tanh((-LRU_C) * r * sp)
    q = lax.rsqrt(1.0 - t)
    a = jnp.sqrt(1.0 + t) * q
    mult = jnp.sqrt(-2.0 * t) * q
    a2_over_mult = (1.0 + t) * q * lax.rsqrt(-2.0 * t)
    return xr, xrb, r, ig, sp, a, mult, a2_over_mult, sh


def _lru_fwd_call(p, cw, cb, wa, ba, wx, bx, lam, out_g, y_half, name):
    S = p.shape[0]
    H = wa.shape[0]
    W = H * CHUNK
    tm = _tile(S, 256, 16)
    ng = tm // 8

    def body(pg_ref, px_ref, cw_ref, cb_ref, wa_ref, ba_ref, wx_ref, bx_ref, lam_ref, og_ref, y_in_ref,
             y_ref, h_ref, xprev, hcar, abuf, bbuf):
        @pl.when(pl.program_id(0) == 0)
        def _():
            xprev[...] = jnp.zeros_like(xprev)
            hcar[...] = jnp.zeros_like(hcar)

        xl = px_ref[...]
        xr, _, _, ig, _, a, mult, _, _ = _lru_gates(xprev[...], xl, _taps(cw_ref), cb_ref[...], wa_ref, ba_ref[...],
                                                    wx_ref, bx_ref[...], lam_ref[...], H)
        xprev[...] = xl[tm - 8:]
        b = mult * (ig * xr)
        sub = lax.broadcasted_iota(jnp.int32, (tm, W), 0) & 7
        for d in (1, 2, 4):
            m = sub >= d
            a_s = jnp.where(m, pltpu.roll(a, d, 0), 1.0)
            b_s = jnp.where(m, pltpu.roll(b, d, 0), 0.0)
            b = a * b_s + b
            a = a * a_s
        abuf[...] = a
        bbuf[...] = b

        def step(g, carry):
            r0 = pl.multiple_of(g * 8, 8)
            h_ref[pl.ds(r0, 8), :] = abuf[pl.ds(r0, 8), :] * carry + bbuf[pl.ds(r0, 8), :]
            return jnp.broadcast_to(h_ref[pl.ds(r0 + 7, 1), :], (8, W))

        hcar[...] = lax.fori_loop(0, ng, step, hcar[...])
        yl = h_ref[...] * _gelu(pg_ref[...])
        r = lax.rsqrt(_rows_mean(yl * yl) + RMS_EPS)
        y_ref[...] = (yl * r * og_ref[...]).astype(BF16)

    vec = pl.BlockSpec((1, W), lambda i: (0, 0))
    mat = pl.BlockSpec((H, CHUNK, CHUNK), lambda i: (0, 0, 0))
    return pl.pallas_call(
        body, name=name, grid=(S // tm,),
        in_specs=[pl.BlockSpec((tm, W), lambda i: (i, 2)), pl.BlockSpec((tm, W), lambda i: (i, 3)),
                  pl.BlockSpec((4, W), lambda i: (0, 0)), vec, mat, vec, mat, vec, vec, vec,
                  pl.BlockSpec(memory_space=pl.ANY)],
        out_specs=[pl.BlockSpec((tm, W), lambda i: (i, 1)), pl.BlockSpec((tm, W), lambda i: (i, 0))],
        out_shape=[jax.ShapeDtypeStruct((S, 2 * W), BF16), jax.ShapeDtypeStruct((S, W), F32)],
        input_output_aliases={10: 0},
        scratch_shapes=[pltpu.VMEM((8, W), F32), pltpu.VMEM((8, W), F32), pltpu.VMEM((tm, W), F32),
                        pltpu.VMEM((tm, W), F32)],
        compiler_params=_cp(1))(p, p, cw, cb, wa, ba, wx, bx, lam, out_g, y_half)


def _lru_bwd_call(p, hs, d_y, cw, cb, wa, ba, wx, bx, lam, out_g, dp_half, name):
    S = p.shape[0]
    H = wa.shape[0]
    W = H * CHUNK
    tm = _tile(S, 256, 16)
    ng = tm // 8
    ni = S // tm
    hb = tm // 8

    def body(pg_ref, px_ref, pxp_ref, h_ref, hp_ref, dy_ref, cw_ref, cb_ref, wa_ref, ba_ref, wx_ref, bx_ref,
             lam_ref, og_ref, dp_in_ref,
             dp_ref, dcw_ref, dcb_ref, dwa_ref, dba_ref, dwx_ref, dbx_ref, dlam_ref, dog_ref,
             a_next, e_next, dxr_next, abuf, bbuf, ebuf):
        i = pl.program_id(0)
        ri = ni - 1 - i

        @pl.when(i == 0)
        def _():
            for r in (dcw_ref, dcb_ref, dwa_ref, dba_ref, dwx_ref, dbx_ref, dlam_ref, dog_ref,
                      a_next, e_next, dxr_next):
                r[...] = jnp.zeros_like(r)

        keep_prev = jnp.where(ri == 0, 0.0, 1.0)
        cw_ = _taps(cw_ref)
        lam_ = lam_ref[...]
        xl = px_ref[...]
        xr, xrb, r, ig, sp, a, mult, a2m, sh = _lru_gates(pxp_ref[...] * keep_prev, xl, cw_, cb_ref[...], wa_ref,
                                                          ba_ref[...], wx_ref, bx_ref[...], lam_, H)
        gg, dgg = _gelu_parts(pg_ref[...])
        hv = h_ref[...]
        yl = hv * gg
        rr = lax.rsqrt(_rows_mean(yl * yl) + RMS_EPS)
        yn = yl * rr
        dyb = dy_ref[...]
        dog_ref[...] += _col_sum(dyb * yn)
        dyg = dyb * og_ref[...]
        dyl = rr * (dyg - yn * _rows_mean(dyg * yn))
        dh = dyl * gg
        dgl = dyl * hv * dgg

        an = _shift_up(a, a_next[...], 1)
        eb = dh
        sub = lax.broadcasted_iota(jnp.int32, (tm, W), 0) & 7
        for d in (1, 2, 4):
            m = sub < 8 - d
            a_s = jnp.where(m, pltpu.roll(an, tm - d, 0), 1.0)
            e_s = jnp.where(m, pltpu.roll(eb, tm - d, 0), 0.0)
            eb = an * e_s + eb
            an = an * a_s
        abuf[...] = an
        bbuf[...] = eb

        def step(g, carry):
            r0 = pl.multiple_of((ng - 1 - g) * 8, 8)
            ebuf[pl.ds(r0, 8), :] = abuf[pl.ds(r0, 8), :] * carry + bbuf[pl.ds(r0, 8), :]
            return jnp.broadcast_to(ebuf[pl.ds(r0, 1), :], (8, W))

        lax.fori_loop(0, ng, step, jnp.broadcast_to(e_next[0:1, :], (8, W)))
        e = ebuf[...]
        a_next[...] = a[0:8]
        e_next[...] = e[0:8]

        hm1 = _shift_down(hp_ref[...] * keep_prev, hv, 1)
        da = e * hm1
        dmult = e * ig * xr
        di = e * mult * xr
        dxr = e * mult * ig
        dla = da * a - dmult * a2m
        dr = dla * ((-LRU_C) * sp)
        dlam_ref[...] += _col_sum(dla * ((-LRU_C) * r))
        dpr = dr * r * (1.0 - r)
        dpi = di * ig * (1.0 - ig)
        dba_ref[...] += _col_sum(dpr)
        dbx_ref[...] += _col_sum(dpi)
        dprb = dpr.astype(BF16)
        dpib = dpi.astype(BF16)
        back = []
        for h in range(H):
            cs = slice(h * CHUNK, (h + 1) * CHUNK)
            wab = wa_ref[h].astype(BF16)
            wxb = wx_ref[h].astype(BF16)
            back.append(lax.dot_general(dprb[:, cs], wab, _DN_NT, preferred_element_type=F32)
                        + lax.dot_general(dpib[:, cs], wxb, _DN_NT, preferred_element_type=F32))
            dwa_ref[h] += lax.dot_general(xrb[:, cs], dprb[:, cs], _DN_TN, preferred_element_type=F32)
            dwx_ref[h] += lax.dot_general(xrb[:, cs], dpib[:, cs], _DN_TN, preferred_element_type=F32)
        dxr = dxr + jnp.concatenate(back, axis=1)

        nxt = dxr_next[...]
        dxl = (cw_[3] * dxr + cw_[2] * _shift_up(dxr, nxt, 1) + cw_[1] * _shift_up(dxr, nxt, 2)
               + cw_[0] * _shift_up(dxr, nxt, 3))
        dxr_next[...] = dxr[0:8]
        for k in range(4):
            dcw_ref[k:k + 1, :] += _col_sum(sh[3 - k] * dxr)
        dcb_ref[...] += _col_sum(dxr)
        dp_ref[...] = jnp.concatenate([dgl, dxl], axis=1).astype(BF16)

        @pl.when(i == ni - 1)
        def _():
            dlam_ref[...] = -dlam_ref[...] * _sigmoid(-lam_)

    vec = pl.BlockSpec((1, W), lambda i: (0, 0))
    mat = pl.BlockSpec((H, CHUNK, CHUNK), lambda i: (0, 0, 0))
    rev = lambda i: ni - 1 - i
    prev = lambda i: jnp.maximum(rev(i) * hb - 1, 0)
    vshape = jax.ShapeDtypeStruct((1, W), F32)
    mshape = jax.ShapeDtypeStruct((H, CHUNK, CHUNK), F32)
    tile = lambda: pltpu.VMEM((tm, W), F32)
    car = lambda: pltpu.VMEM((8, W), F32)
    return pl.pallas_call(
        body, name=name, grid=(ni,),
        in_specs=[pl.BlockSpec((tm, W), lambda i: (rev(i), 2)), pl.BlockSpec((tm, W), lambda i: (rev(i), 3)),
                  pl.BlockSpec((8, W), lambda i: (prev(i), 3)),
                  pl.BlockSpec((tm, W), lambda i: (rev(i), 0)), pl.BlockSpec((8, W), lambda i: (prev(i), 0)),
                  pl.BlockSpec((tm, W), lambda i: (rev(i), 1)),
                  pl.BlockSpec((4, W), lambda i: (0, 0)), vec, mat, vec, mat, vec, vec, vec,
                  pl.BlockSpec(memory_space=pl.ANY)],
        out_specs=[pl.BlockSpec((tm, 2 * W), lambda i: (rev(i), 1)), pl.BlockSpec((4, W), lambda i: (0, 0)), vec,
                   mat, vec, mat, vec, vec, vec],
        out_shape=[jax.ShapeDtypeStruct((S, 4 * W), BF16), jax.ShapeDtypeStruct((4, W), F32), vshape,
                   mshape, vshape, mshape, vshape, vshape, vshape],
        input_output_aliases={14: 0},
        scratch_shapes=[car(), car(), car(), tile(), tile(), tile()],
        compiler_params=_cp(1, 56))(p, p, p, hs, hs, d_y, cw, cb, wa, ba, wx, bx, lam, out_g, dp_half)


def _rows128(a):
    return a.reshape(-1, LANES).astype(F32)


def _pack(arrays, pad_to=256):
    flat = jnp.concatenate([_rows128(a) for a in arrays], axis=0)
    pad = (-flat.shape[0]) % pad_to
    if pad:
        flat = jnp.concatenate([flat, jnp.zeros((pad, LANES), F32)], axis=0)
    return flat


def _unpack(flat, shapes):
    out, r = [], 0
    for s in shapes:
        n = 1
        for d in s:
            n *= d
        out.append(flat[r:r + n // LANES].reshape(s))
        r += n // LANES
    return out


def kernel(x, norm1_g, w_in, gm_v_g, gm_v_b, gm_ws, gm_bs, lru_conv_w, lru_conv_b, lru_wa, lru_ba, lru_wx, lru_bx, lru_lambda, gm_out_g, lru_out_g, w_out, norm2_g, ffn_w_up, ffn_conv_w, ffn_conv_b, ffn_w_down, final_g, loss_target, m_norm1_g, m_w_in, m_gm_v_g, m_gm_v_b, m_gm_ws, m_gm_bs, m_lru_conv_w, m_lru_conv_b, m_lru_wa, m_lru_ba, m_lru_wx, m_lru_bx, m_lru_lambda, m_gm_out_g, m_lru_out_g, m_w_out, m_norm2_g, m_ffn_w_up, m_ffn_conv_w, m_ffn_conv_b, m_ffn_w_down, m_final_g, v_norm1_g, v_w_in, v_gm_v_g, v_gm_v_b, v_gm_ws, v_gm_bs, v_lru_conv_w, v_lru_conv_b, v_lru_wa, v_lru_ba, v_lru_wx, v_lru_bx, v_lru_lambda, v_gm_out_g, v_lru_out_g, v_w_out, v_norm2_g, v_ffn_w_up, v_ffn_conv_w, v_ffn_conv_b, v_ffn_w_down, v_final_g):
    wts = dict(norm1_g=norm1_g, w_in=w_in, gm_v_g=gm_v_g, gm_v_b=gm_v_b, gm_ws=gm_ws, gm_bs=gm_bs,
               lru_conv_w=lru_conv_w, lru_conv_b=lru_conv_b, lru_wa=lru_wa, lru_ba=lru_ba, lru_wx=lru_wx,
               lru_bx=lru_bx, lru_lambda=lru_lambda, gm_out_g=gm_out_g, lru_out_g=lru_out_g, w_out=w_out,
               norm2_g=norm2_g, ffn_w_up=ffn_w_up, ffn_conv_w=ffn_conv_w, ffn_conv_b=ffn_conv_b,
               ffn_w_down=ffn_w_down, final_g=final_g)
    mom = dict(norm1_g=m_norm1_g, w_in=m_w_in, gm_v_g=m_gm_v_g, gm_v_b=m_gm_v_b, gm_ws=m_gm_ws, gm_bs=m_gm_bs,
               lru_conv_w=m_lru_conv_w, lru_conv_b=m_lru_conv_b, lru_wa=m_lru_wa, lru_ba=m_lru_ba, lru_wx=m_lru_wx,
               lru_bx=m_lru_bx, lru_lambda=m_lru_lambda, gm_out_g=m_gm_out_g, lru_out_g=m_lru_out_g, w_out=m_w_out,
               norm2_g=m_norm2_g, ffn_w_up=m_ffn_w_up, ffn_conv_w=m_ffn_conv_w, ffn_conv_b=m_ffn_conv_b,
               ffn_w_down=m_ffn_w_down, final_g=m_final_g)
    var = dict(norm1_g=v_norm1_g, w_in=v_w_in, gm_v_g=v_gm_v_g, gm_v_b=v_gm_v_b, gm_ws=v_gm_ws, gm_bs=v_gm_bs,
               lru_conv_w=v_lru_conv_w, lru_conv_b=v_lru_conv_b, lru_wa=v_lru_wa, lru_ba=v_lru_ba, lru_wx=v_lru_wx,
               lru_bx=v_lru_bx, lru_lambda=v_lru_lambda, gm_out_g=v_gm_out_g, lru_out_g=v_lru_out_g, w_out=v_w_out,
               norm2_g=v_norm2_g, ffn_w_up=v_ffn_w_up, ffn_conv_w=v_ffn_conv_w, ffn_conv_b=v_ffn_conv_b,
               ffn_w_down=v_ffn_w_down, final_g=v_final_g)

    xi, yi, ci = lax.axis_index("x"), lax.axis_index("y"), lax.axis_index("c")
    chip = 2 * xi + yi
    dev = 2 * chip + ci
    core_chip = jnp.stack([ci, chip]).astype(jnp.int32)

    xs = x[0]
    tgt = loss_target[0]
    S, D = xs.shape
    H = gm_ws.shape[1]
    W = H * CHUNK
    Fd = ffn_w_down.shape[1] * N_DEV
    lcw_cols = lru_conv_w.shape[2]
    fcw_cols = ffn_conv_w.shape[2]

    dev1 = jnp.reshape(dev, (1,)).astype(jnp.int32)

    def gather_start(shards, name, after=()):
        lands = [_place_own_call(s, dev1, "%s_own%d" % (name, k)) for k, s in enumerate(shards)]
        return _exchange_start(shards, lands, 4 * len(shards), _gather_stage1_copies(len(shards)), name + "_ici", after)

    def gather_forward(lands, name, after=()):
        return _exchange_start([], lands, 3 * len(lands), _gather_stage2_copies(len(lands)), name + "_d2d", after)

    def pair_start(g, name, after=()):
        return _exchange_start([g], [lax.empty((4,) + g.shape[1:], F32)], 4, _pair_copies(1), name, after)

    def chip_start(p16, name, after=()):
        return _exchange_start([p16], [lax.empty((3,) + p16.shape[1:], BF16)], 3, _chip_copies(1), name, after)

    vgm_g, vgm_b = gm_v_g, gm_v_b
    ws, wa, wx = gm_ws[0], lru_wa[0], lru_wx[0]
    bsb = jnp.broadcast_to(gm_bs[0][:, :, None], (H, CHUNK, CHUNK))
    ba, bx = lru_ba.reshape(1, W), lru_bx.reshape(1, W)
    fcb = ffn_conv_b
    fing = final_g.reshape(1, D)

    conv_pack = _pack([lru_conv_w[0], ffn_conv_w[0]], pad_to=8)
    ga1 = gather_start([w_in[0].astype(BF16), conv_pack], "gather_in")
    h1 = _rmsnorm_call(xs, norm1_g, "norm1", deps=(ga1.token,))
    _, la = _exchange_wait(ga1, after=(h1,))
    ga2 = gather_forward(la, "gather_in")
    gb1 = gather_start([w_out[0].astype(BF16)], "gather_out", after=(ga2.token,))
    gc1 = gather_start([ffn_w_up[0].astype(BF16)], "gather_up", after=(gb1.token,))
    _, (win_g, conv_g) = _exchange_wait(ga2, after=(gc1.token,))
    n_l = 4 * lcw_cols // LANES
    n_f = 3 * fcw_cols // LANES
    lcw = conv_g[:, :n_l].reshape(N_DEV, 4, lcw_cols).transpose(1, 0, 2).reshape(4, N_DEV * lcw_cols)
    fcw = conv_g[:, n_l:n_l + n_f].reshape(N_DEV, 3, fcw_cols).transpose(1, 0, 2).reshape(3, N_DEV * fcw_cols)

    p = _mm_blocked_call(h1, win_g, F32, False, "in_proj")
    win_rows = _unblock_call(win_g, "w_in_rows")
    _, lb = _exchange_wait(gb1, after=(p,))
    gb2 = gather_forward(lb, "gather_out")
    y_half = _gm_fwd_call(p, vgm_g, vgm_b, ws, bsb, gm_out_g, "gmlp_fwd", deps=(gb2.token,))
    y, hs = _lru_fwd_call(p, lcw, lru_conv_b, wa, ba, wx, bx, lru_lambda, lru_out_g, y_half, "lru_fwd")
    _, (wout_g,) = _exchange_wait(gb2, after=(y,))
    wout_full = wout_g.reshape(D, D)
    x2 = _mm_out_call(xs, y, wout_full, "out_proj")
    h2 = _rmsnorm_call(x2, norm2_g, "norm2")
    _, lc = _exchange_wait(gc1, after=(h2,))
    gc2 = gather_forward(lc, "gather_up")
    gd1 = gather_start([ffn_w_down[0].astype(BF16)], "gather_down", after=(gc2.token,))
    _, (wup_g,) = _exchange_wait(gc2, after=(gd1.token,))
    up3, upc3, f = _ffn_up_act_call(h2, wup_g, fcw, fcb, "ffn_up")
    _, ld = _exchange_wait(gd1, after=(f,))
    gd2 = gather_forward(ld, "gather_down")
    _, (wdown_g,) = _exchange_wait(gd2)
    wdown_full = wdown_g.reshape(Fd, D)
    dx3, dx3b, loss_acc, d_final = _mm_down_loss_call(x2, f, wdown_full, fing, tgt, "ffn_down_loss")

    g_wdown = _mm_tn_rows_call(f, dx3b, "ffn_down_dw").reshape((N_DEV,) + ffn_w_down.shape[1:])
    pd = pair_start(g_wdown, "pair_down")
    d_up3, dfcw_g, dfcw_v, dfcb_g, dfcb_v = _ffn_down_dx_act_bwd_call(dx3b, wdown_full, up3, upc3, fcw, "ffn_down_dx",
                                                                     deps=(pd.token,))
    (g_wdown,), (r1,) = _exchange_wait(pd, after=(d_up3,))
    own_down, p16 = _pair_add_call(g_wdown, r1, core_chip, "pair_add_down")
    cd = chip_start(p16, "chip_down")
    g_wup = _mm_tn_cols_call(h2, d_up3, N_DEV, ffn_w_up.shape[2], "ffn_up_dw", deps=(cd.token,))
    pu = pair_start(g_wup, "pair_up")
    dx2, dx2b, d_norm2 = _mm_dx_norm_call(d_up3, wup_g, dx3, x2, norm2_g, "ffn_up_dx", deps=(pu.token,))
    g_wout = _mm_tn_rows_call(y, dx2b, "out_proj_dw").reshape((N_DEV,) + w_out.shape[1:])
    po = pair_start(g_wout, "pair_out")
    d_y = _mm_nt_call(dx2b, wout_full, F32, "out_proj_dx", deps=(po.token,))
    (g_wup,), (r1,) = _exchange_wait(pu, after=(d_y,))
    own_up, p16 = _pair_add_call(g_wup, r1, core_chip, "pair_add_up")
    _, (r2_down,) = _exchange_wait(cd, after=(p16,))
    cu = chip_start(p16, "chip_up", after=(r2_down,))
    dp_half, d_vg, d_vb, d_ws, d_bs, d_gog = _gm_bwd_call(p, d_y, vgm_g, vgm_b, ws, bsb, gm_out_g, "gmlp_bwd",
                                                          deps=(cu.token,))
    d_p2, d_lcw, d_lcb, d_wa, d_ba, d_wx, d_bx, d_lam, d_log = _lru_bwd_call(
        p, hs, d_y, lcw, lru_conv_b, wa, ba, wx, bx, lru_lambda, lru_out_g, dp_half, "lru_bwd")
    d_p = d_p2[None]
    (g_wout,), (r1,) = _exchange_wait(po, after=(d_p,))
    own_out, p16_out = _pair_add_call(g_wout, r1, core_chip, "pair_add_out")
    g_win = _mm_tn_cols_call(h1, d_p, N_DEV, w_in.shape[2], "in_proj_dw")
    pi = pair_start(g_win, "pair_in")
    _, (r2_up,) = _exchange_wait(cu, after=(g_win,))
    co = chip_start(p16_out, "chip_out", after=(r2_up,))
    gx_a, dn_a = _mm_nt_norm_call(d_p[0], win_rows, dx2, xs, norm1_g, "in_proj_dx_a", deps=(co.token, pi.token),
                                  part=(0, 2))
    (g_win,), (r1,) = _exchange_wait(pi, after=(gx_a,))
    own_in, p16 = _pair_add_call(g_win, r1, core_chip, "pair_add_in")
    _, (r2_out,) = _exchange_wait(co, after=(p16,))
    ci_ = chip_start(p16, "chip_in", after=(r2_out,))
    grad_x, dn_b = _mm_nt_norm_call(d_p[0], win_rows, dx2, xs, norm1_g, "in_proj_dx_b", deps=(ci_.token,),
                                    part=(1, 2), dx_so_far=gx_a)

    small_g = dict(norm1_g=dn_a + dn_b, gm_v_g=d_vg, gm_v_b=d_vb, gm_ws=d_ws, gm_bs=d_bs[:, :, 0], lru_conv_b=d_lcb,
                   lru_wa=d_wa, lru_ba=d_ba, lru_wx=d_wx, lru_bx=d_bx, lru_lambda=d_lam, gm_out_g=d_gog,
                   lru_out_g=d_log, norm2_g=d_norm2,
                   ffn_conv_b=jnp.concatenate([dfcb_g, dfcb_v], axis=1), final_g=d_final)
    rep = _pack([small_g[n] for n in SMALL])
    conv_part = _pack([d_lcw, jnp.concatenate([dfcw_g, dfcw_v], axis=1)], pad_to=8)
    n_rep, n_conv = rep.shape[0], conv_part.shape[0]
    gs1 = gather_start([jnp.concatenate([rep, conv_part], axis=0)], "gather_small")

    def adamw_big(n, own, r2, deps=()):
        return _adamw_call(wts[n][0], mom[n][0], var[n][0], [(own, None), (r2, 0), (r2, 1), (r2, 2)], "adamw_" + n, deps)

    res = {}
    res["ffn_w_down"] = adamw_big("ffn_w_down", own_down, r2_down, (gs1.token,))
    res["ffn_w_up"] = adamw_big("ffn_w_up", own_up, r2_up, (gs1.token,))
    res["w_out"] = adamw_big("w_out", own_out, r2_out, (gs1.token,))
    _, ls = _exchange_wait(gs1, after=(res["w_out"][0], res["ffn_w_up"][0], res["ffn_w_down"][0]))
    gs2 = gather_forward(ls, "gather_small")
    _, (r2_in,) = _exchange_wait(ci_, after=(gs2.token,))
    res["w_in"] = adamw_big("w_in", own_in, r2_in)
    _, (parts,) = _exchange_wait(gs2, after=(res["w_in"][0],))
    g_rep, d_rep, m_rep, v_rep = _adamw_call(
        _pack([wts[n] for n in SMALL]), _pack([mom[n] for n in SMALL]), _pack([var[n] for n in SMALL]),
        [(parts, k) for k in range(N_DEV)], "adamw_small")
    shapes = [wts[n].shape for n in SMALL]
    for n, g_, d_, m_, v_ in zip(SMALL, _unpack(g_rep, shapes), _unpack(d_rep, shapes), _unpack(m_rep, shapes),
                                 _unpack(v_rep, shapes)):
        res[n] = (g_, d_, m_, v_)
    conv_sum = _sum_call(parts, n_rep, n_conv, "sum_conv_grads")
    g_lcw = conv_sum[:4 * W // LANES].reshape(4, W)
    g_fcw = conv_sum[4 * W // LANES:4 * W // LANES + 6 * Fd // LANES].reshape(3, 2 * Fd)
    for n, full in (("lru_conv_w", g_lcw), ("ffn_conv_w", g_fcw)):
        cols = wts[n].shape[2]
        mine = lax.dynamic_slice_in_dim(full, dev * cols, cols, axis=1)
        res[n] = _adamw_call(wts[n][0], mom[n][0], var[n][0], [(mine, None)], "adamw_" + n)

    loss = lax.psum(loss_acc[0, 0], ("x", "y", "c"))
    outs = [[], [], [], []]
    for n in WEIGHTS:
        for k in range(4):
            outs[k].append(res[n][k].reshape(wts[n].shape))
    return (loss, grad_x[None], *outs[0], *outs[1], *outs[2], *outs[3])
```

```python
import functools
import math

import jax
import jax.numpy as jnp
from jax import lax
from jax.experimental import pallas as pl
from jax.experimental.pallas import tpu as pltpu

F32 = jnp.float32
BF16 = jnp.bfloat16

RMS_EPS = 1e-6
LN_EPS = 1e-5
LRU_C = 8.0
CHUNK = 128
ADAM_LR = 0.001
ADAM_B1 = 0.9
ADAM_B2 = 0.999
ADAM_EPS = 1e-08
ADAM_WD = 0.01
ADAM_STEP = 10
N_DEV = 8
LANES = 128
MIB = 1024 * 1024

WEIGHTS = ['norm1_g', 'w_in', 'gm_v_g', 'gm_v_b', 'gm_ws', 'gm_bs', 'lru_conv_w', 'lru_conv_b', 'lru_wa', 'lru_ba',
           'lru_wx', 'lru_bx', 'lru_lambda', 'gm_out_g', 'lru_out_g', 'w_out', 'norm2_g', 'ffn_w_up', 'ffn_conv_w',
           'ffn_conv_b', 'ffn_w_down', 'final_g']
BIG = ['w_in', 'w_out', 'ffn_w_up', 'ffn_w_down']
CONV = ['lru_conv_w', 'ffn_conv_w']
SMALL = [n for n in WEIGHTS if n not in BIG and n not in CONV]

_DN_NT = (((1,), (1,)), ((), ()))
_DN_TN = (((0,), (0,)), ((), ()))
_GELU_C = 0.7978845608028654


def _cp(n_axes, vmem_mib=48):
    return pltpu.CompilerParams(dimension_semantics=("arbitrary",) * n_axes, vmem_limit_bytes=vmem_mib * MIB)


def _tile(n, pref, mult=8):
    t = min(pref, n)
    t -= t % mult
    while t >= mult:
        if n % t == 0:
            return t
        t -= mult
    return n


def _gelu(z):
    return 0.5 * z * (1.0 + jnp.tanh(_GELU_C * z * (1.0 + 0.044715 * z * z)))


def _gelu_parts(z):
    z2 = z * z
    t = jnp.tanh(_GELU_C * z * (1.0 + 0.044715 * z2))
    g = 0.5 * z * (1.0 + t)
    dg = 0.5 * (1.0 + t) + 0.5 * z * (1.0 - t * t) * (_GELU_C * (1.0 + 0.134145 * z2))
    return g, dg


def _sigmoid(z):
    return 0.5 + 0.5 * jnp.tanh(0.5 * z)


def _softplus(z):
    t = jnp.exp(-jnp.abs(z))
    u = 1.0 + t
    log1p = jnp.where(u == 1.0, t, jnp.log(u) * t / (u - 1.0))
    return jnp.maximum(z, 0.0) + log1p


def _rows_mean(v):
    return jnp.mean(v, axis=-1, keepdims=True)


def _col_sum(v):
    return jnp.sum(v, axis=0, keepdims=True)


def _shift_down(prev8, cur, k):
    if k == 0:
        return cur
    z = jnp.concatenate([prev8, cur], axis=0)
    return pltpu.roll(z, k, 0)[8:]


def _shift_up(cur, next8, k):
    if k == 0:
        return cur
    n = cur.shape[0]
    z = jnp.concatenate([cur, next8], axis=0)
    return pltpu.roll(z, n + 8 - k, 0)[:n]


def _mesh_pos():
    return lax.axis_index("x"), lax.axis_index("y"), lax.axis_index("c")


def _any_specs(n):
    return [pl.BlockSpec(memory_space=pl.ANY)] * n


def _pallas(body, n_in, deps, **kw):
    nd = len(deps)
    if not nd:
        return pl.pallas_call(body, **kw)

    def ordered(*refs):
        body(*refs[:n_in], *refs[n_in + nd:])

    kw["in_specs"] = list(kw["in_specs"]) + _any_specs(nd)
    return pl.pallas_call(ordered, **kw)


_HBM = pl.BlockSpec(memory_space=pltpu.HBM)
_SEM = pl.BlockSpec(memory_space=pltpu.SEMAPHORE)
_EFFECT = pltpu.SideEffectType.DATAFLOW_SIDE_EFFECTING


class _InFlight:
    def __init__(self, sems, bufs, token, n_src, n_copies, make_copies, name):
        self.sems, self.bufs, self.token = sems, bufs, token
        self.n_src, self.n_copies, self.make_copies, self.name = n_src, n_copies, make_copies, name


def _exchange_start(srcs, lands, n_copies, make_copies, name, after=()):
    bufs = list(srcs) + list(lands)
    nb, na = len(bufs), len(after)
    ns = len(srcs)

    def body(*refs):
        b_refs = refs[:nb]
        outs = refs[nb + na:]
        send, recv = outs[:n_copies], outs[n_copies:2 * n_copies]
        token = outs[-1]
        for cp in make_copies(b_refs[:ns], b_refs[ns:], send, recv):
            cp.start()
        token[...] = jnp.zeros_like(token)

    out = pl.pallas_call(
        body, name=name,
        out_shape=[pltpu.SemaphoreType.DMA(())] * (2 * n_copies) + [pltpu.HBM(b.shape, b.dtype) for b in bufs]
        + [jax.ShapeDtypeStruct((8, LANES), F32)],
        in_specs=[_HBM] * nb + _any_specs(na),
        out_specs=[_SEM] * (2 * n_copies) + [_HBM] * nb + [pl.BlockSpec(memory_space=pltpu.VMEM)],
        input_output_aliases={i: 2 * n_copies + i for i in range(nb)},
        compiler_params=pltpu.CompilerParams(has_side_effects=_EFFECT),
    )(*[pltpu.with_memory_space_constraint(b, pltpu.HBM) for b in bufs], *after)
    return _InFlight(out[:2 * n_copies], out[2 * n_copies:2 * n_copies + nb], out[-1], ns, n_copies, make_copies, name)


def _exchange_wait(fl, after=()):
    nb, na, nc, ns = len(fl.bufs), len(after), fl.n_copies, fl.n_src

    def body(*refs):
        b_refs = refs[:nb]
        sems = refs[nb:nb + 2 * nc]
        copies = fl.make_copies(b_refs[:ns], b_refs[ns:], sems[:nc], sems[nc:])
        for cp in copies:
            cp.wait_send()
        for cp in copies:
            cp.wait_recv()

    out = pl.pallas_call(
        body, name=fl.name + "_wait",
        out_shape=[pltpu.HBM(b.shape, b.dtype) for b in fl.bufs],
        in_specs=[_HBM] * nb + [_SEM] * (2 * nc) + _any_specs(na),
        out_specs=[_HBM] * nb,
        input_output_aliases={i: i for i in range(nb)},
        compiler_params=pltpu.CompilerParams(has_side_effects=_EFFECT),
    )(*fl.bufs, *fl.sems, *after)
    return list(out[:ns]), list(out[ns:])


def _remote(src, dst, send_sem, recv_sem, to):
    return pltpu.make_async_remote_copy(src_ref=src, dst_ref=dst, send_sem=send_sem, recv_sem=recv_sem,
                                        device_id=to, device_id_type=pl.DeviceIdType.MESH)


def _gather_stage1_copies(n):
    def make(s_refs, l_refs, send, recv):
        x, y, c = _mesh_pos()
        own = 4 * x + 2 * y + c
        targets = [(x, y, 1 - c), (1 - x, y, c), (x, 1 - y, c), (1 - x, 1 - y, c)]
        return [_remote(s_refs[a], l_refs[a].at[own], send[4 * a + k], recv[4 * a + k], to)
                for a in range(n) for k, to in enumerate(targets)]
    return make


def _gather_stage2_copies(n):
    def make(s_refs, l_refs, send, recv):
        x, y, c = _mesh_pos()
        blocks = [4 * (1 - x) + 2 * y + c, 4 * x + 2 * (1 - y) + c, 4 * (1 - x) + 2 * (1 - y) + c]
        return [_remote(l_refs[a].at[b], l_refs[a].at[b], send[3 * a + j], recv[3 * a + j], (x, y, 1 - c))
                for a in range(n) for j, b in enumerate(blocks)]
    return make


def _pair_copies(n):
    def make(s_refs, l_refs, send, recv):
        x, y, c = _mesh_pos()
        return [_remote(s_refs[a].at[2 * k + 1 - c], l_refs[a].at[k], send[4 * a + k], recv[4 * a + k], (x, y, 1 - c))
                for a in range(n) for k in range(4)]
    return make


def _chip_copies(n):
    def make(s_refs, l_refs, send, recv):
        x, y, c = _mesh_pos()
        chips = [(1 - x, y), (x, 1 - y), (1 - x, 1 - y)]
        return [_remote(s_refs[a].at[2 * ch[0] + ch[1]], l_refs[a].at[j], send[3 * a + j], recv[3 * a + j], (*ch, c))
                for a in range(n) for j, ch in enumerate(chips)]
    return make


def _place_own_call(shard, dev, name):
    R, C = shard.shape
    tr = _tile(R, max(16, MIB // (C * shard.dtype.itemsize)), 16)

    def body(d_ref, s_ref, o_ref):
        o_ref[...] = s_ref[...]

    grid_spec = pltpu.PrefetchScalarGridSpec(
        num_scalar_prefetch=1, grid=(R // tr,),
        in_specs=[pl.BlockSpec((tr, C), lambda r, d: (r, 0))],
        out_specs=pl.BlockSpec((None, tr, C), lambda r, d: (d[0], r, 0)))
    return pl.pallas_call(body, name=name, grid_spec=grid_spec,
                          out_shape=jax.ShapeDtypeStruct((N_DEV, R, C), shard.dtype), compiler_params=_cp(1))(dev, shard)


def _pair_add_call(g, r1, core_chip, name):
    _, R, C = g.shape
    tr = _tile(R, max(16, (2 * MIB) // (C * 4)), 16)

    def body(cc_ref, g_ref, r_ref, p32_ref, p16_ref):
        s = g_ref[...] + r_ref[...]
        p16_ref[...] = s.astype(BF16)

        @pl.when(pl.program_id(1) == cc_ref[1])
        def _():
            p32_ref[...] = s

    grid_spec = pltpu.PrefetchScalarGridSpec(
        num_scalar_prefetch=1, grid=(R // tr, 4),
        in_specs=[pl.BlockSpec((None, tr, C), lambda r, k, cc: (2 * k + cc[0], r, 0)),
                  pl.BlockSpec((None, tr, C), lambda r, k, cc: (k, r, 0))],
        out_specs=[pl.BlockSpec((tr, C), lambda r, k, cc: (r, 0)),
                   pl.BlockSpec((None, tr, C), lambda r, k, cc: (k, r, 0))])
    return pl.pallas_call(
        body, name=name, grid_spec=grid_spec,
        out_shape=[jax.ShapeDtypeStruct((R, C), F32), jax.ShapeDtypeStruct((4, R, C), BF16)],
        compiler_params=_cp(2))(core_chip, g, r1)


def _adamw_call(w, m, v, addends, name, deps=()):
    R, C = w.shape
    tr = _tile(R, max(8, (MIB // 2) // (C * 4)), 16)
    na = len(addends)
    c1 = 1.0 - ADAM_B1 ** ADAM_STEP
    c2 = 1.0 - ADAM_B2 ** ADAM_STEP

    def body(*refs):
        w_ref, m_ref, v_ref = refs[:3]
        a_refs = refs[3:3 + na]
        g_ref, d_ref, nm_ref, nv_ref = refs[3 + na:]
        g = a_refs[0][...].astype(F32)
        for a_ref in a_refs[1:]:
            g = g + a_ref[...].astype(F32)
        nm = ADAM_B1 * m_ref[...] + (1.0 - ADAM_B1) * g
        nv = ADAM_B2 * v_ref[...] + (1.0 - ADAM_B2) * (g * g)
        g_ref[...] = g
        nm_ref[...] = nm
        nv_ref[...] = nv
        d_ref[...] = -ADAM_LR * ((nm / c1) / (jnp.sqrt(nv / c2) + ADAM_EPS) + ADAM_WD * w_ref[...])

    flat = pl.BlockSpec((tr, C), lambda r: (r, 0))
    a_specs = [flat if k is None else pl.BlockSpec((None, tr, C), functools.partial(lambda r, kk: (kk, r, 0), kk=k))
               for _, k in addends]
    out = jax.ShapeDtypeStruct((R, C), F32)
    return _pallas(
        body, 3 + na, deps, name=name, grid=(R // tr,),
        in_specs=[flat, flat, flat] + a_specs, out_specs=[flat] * 4, out_shape=[out] * 4,
        compiler_params=_cp(1))(w, m, v, *[a for a, _ in addends], *deps)


def _sum_call(parts, row0, rows, name):
    n = parts.shape[0]
    tr = _tile(math.gcd(row0, rows), 256, 8)
    b0 = row0 // tr

    def body(p_ref, o_ref):
        s = p_ref[0]
        for k in range(1, n):
            s = s + p_ref[k]
        o_ref[...] = s

    return pl.pallas_call(
        body, name=name, grid=(rows // tr,),
        in_specs=[pl.BlockSpec((n, tr, LANES), lambda r: (0, r + b0, 0))],
        out_specs=pl.BlockSpec((tr, LANES), lambda r: (r, 0)),
        out_shape=jax.ShapeDtypeStruct((rows, LANES), F32), compiler_params=_cp(1))(parts)


def _rmsnorm_call(x, g, name, deps=()):
    S, D = x.shape
    tm = _tile(S, 512, 16)

    def body(x_ref, g_ref, o_ref):
        xv = x_ref[...]
        r = lax.rsqrt(_rows_mean(xv * xv) + RMS_EPS)
        o_ref[...] = (xv * r * g_ref[...]).astype(BF16)

    return _pallas(
        body, 2, deps, name=name, grid=(S // tm,),
        in_specs=[pl.BlockSpec((tm, D), lambda i: (i, 0)), pl.BlockSpec((1, D), lambda i: (0, 0))],
        out_specs=pl.BlockSpec((tm, D), lambda i: (i, 0)),
        out_shape=jax.ShapeDtypeStruct((S, D), BF16), compiler_params=_cp(1))(x, g, *deps)


def _mm_blocked_call(a, wg, out_dtype, name, deps=()):
    S, K = a.shape
    nb, _, bn = wg.shape
    tm = _tile(S, 1024, 16)
    tn = _tile(bn, 768, LANES)
    nsub = bn // tn

    def body(a_ref, w_ref, o_ref):
        o_ref[...] = jnp.dot(a_ref[...], w_ref[...], preferred_element_type=F32).astype(out_dtype)

    return _pallas(
        body, 2, deps, name=name, grid=(S // tm, nb * nsub),
        in_specs=[pl.BlockSpec((tm, K), lambda i, j: (i, 0)),
                  pl.BlockSpec((None, K, tn), lambda i, j: (j // nsub, 0, j % nsub))],
        out_specs=pl.BlockSpec((tm, tn), lambda i, j: (i, j)),
        out_shape=jax.ShapeDtypeStruct((S, nb * bn), out_dtype), compiler_params=_cp(2))(a, wg, *deps)


def _mm_out_call(x, y, w, name):
    S, D = x.shape
    tm = _tile(S, 512, 16)

    def body(x_ref, y_ref, w_ref, o_ref):
        o_ref[...] = x_ref[...] + jnp.dot(y_ref[...], w_ref[...], preferred_element_type=F32)

    return pl.pallas_call(
        body, name=name, grid=(S // tm,),
        in_specs=[pl.BlockSpec((tm, D), lambda i: (i, 0)), pl.BlockSpec((tm, D), lambda i: (i, 0)),
                  pl.BlockSpec((D, D), lambda i: (0, 0))],
        out_specs=pl.BlockSpec((tm, D), lambda i: (i, 0)),
        out_shape=jax.ShapeDtypeStruct((S, D), F32), compiler_params=_cp(1))(x, y, w)


def _mm_nt_call(a, w, out_dtype, name, deps=()):
    S, K = a.shape
    N = w.shape[0]
    tm = _tile(S, 1024, 16)
    tn = _tile(N, 768, LANES)

    def body(a_ref, w_ref, o_ref):
        o_ref[...] = lax.dot_general(a_ref[...], w_ref[...], _DN_NT, preferred_element_type=F32).astype(out_dtype)

    return _pallas(
        body, 2, deps, name=name, grid=(S // tm, N // tn),
        in_specs=[pl.BlockSpec((tm, K), lambda i, j: (i, 0)), pl.BlockSpec((tn, K), lambda i, j: (j, 0))],
        out_specs=pl.BlockSpec((tm, tn), lambda i, j: (i, j)),
        out_shape=jax.ShapeDtypeStruct((S, N), out_dtype), compiler_params=_cp(2))(a, w, *deps)


def _mm_down_loss_call(x2, f, w, final_g, target, name):
    S, D = x2.shape
    Fd = f.shape[1]
    tm = _tile(S, 512, 16)
    tk = _tile(Fd, 768, LANES)
    nk = Fd // tk

    def body(x_ref, f_ref, w_ref, g_ref, t_ref, dx_ref, dxb_ref, loss_ref, dg_ref, acc):
        i, k = pl.program_id(0), pl.program_id(1)

        @pl.when(jnp.logical_and(i == 0, k == 0))
        def _():
            loss_ref[...] = jnp.zeros_like(loss_ref)
            dg_ref[...] = jnp.zeros_like(dg_ref)

        @pl.when(k == 0)
        def _():
            acc[...] = jnp.zeros_like(acc)

        acc[...] += jnp.dot(f_ref[...], w_ref[...], preferred_element_type=F32)

        @pl.when(k == nk - 1)
        def _():
            x3 = x_ref[...] + acc[...]
            r = lax.rsqrt(_rows_mean(x3 * x3) + RMS_EPS)
            g = g_ref[...]
            xn = x3 * r
            diff = xn * g - t_ref[...]
            loss_ref[...] += 0.5 * jnp.sum(_rows_mean(diff * diff))
            dout = diff * (1.0 / D)
            dg_ref[...] += _col_sum(dout * xn)
            dyg = dout * g
            dx = r * (dyg - xn * _rows_mean(dyg * xn))
            dx_ref[...] = dx
            dxb_ref[...] = dx.astype(BF16)

    row = lambda i, k: (i, 0)
    return pl.pallas_call(
        body, name=name, grid=(S // tm, nk),
        in_specs=[pl.BlockSpec((tm, D), row), pl.BlockSpec((tm, tk), lambda i, k: (i, k)),
                  pl.BlockSpec((tk, D), lambda i, k: (k, 0)), pl.BlockSpec((1, D), lambda i, k: (0, 0)),
                  pl.BlockSpec((tm, D), row)],
        out_specs=[pl.BlockSpec((tm, D), row), pl.BlockSpec((tm, D), row),
                   pl.BlockSpec((8, LANES), lambda i, k: (0, 0)), pl.BlockSpec((1, D), lambda i, k: (0, 0))],
        out_shape=[jax.ShapeDtypeStruct((S, D), F32), jax.ShapeDtypeStruct((S, D), BF16),
                   jax.ShapeDtypeStruct((8, LANES), F32), jax.ShapeDtypeStruct((1, D), F32)],
        scratch_shapes=[pltpu.VMEM((tm, D), F32)], compiler_params=_cp(2, 56))(x2, f, w, final_g, target)


def _mm_dx_norm_call(a3, wg, resid, xin, g, name, deps=()):
    na, S, Fa = a3.shape
    nb, D, bn = wg.shape
    tm = _tile(S, 512, 16)
    tk = _tile(bn, 1536, LANES)
    nsub = bn // tk
    nka = Fa // tk
    nk = nb * nsub
    assert na * nka == nk

    def body(a_ref, w_ref, r_ref, x_ref, g_ref, dx_ref, dxb_ref, dg_ref, acc):
        i, k = pl.program_id(0), pl.program_id(1)

        @pl.when(jnp.logical_and(i == 0, k == 0))
        def _():
            dg_ref[...] = jnp.zeros_like(dg_ref)

        @pl.when(k == 0)
        def _():
            acc[...] = jnp.zeros_like(acc)

        acc[...] += lax.dot_general(a_ref[...], w_ref[...], _DN_NT, preferred_element_type=F32)

        @pl.when(k == nk - 1)
        def _():
            dh = acc[...]
            xv = x_ref[...]
            r = lax.rsqrt(_rows_mean(xv * xv) + RMS_EPS)
            xn = xv * r
            dg_ref[...] += _col_sum(dh * xn)
            dyg = dh * g_ref[...]
            dx = r_ref[...] + r * (dyg - xn * _rows_mean(dyg * xn))
            dx_ref[...] = dx
            dxb_ref[...] = dx.astype(BF16)

    row = lambda i, k: (i, 0)
    return _pallas(
        body, 5, deps, name=name, grid=(S // tm, nk),
        in_specs=[pl.BlockSpec((None, tm, tk), lambda i, k: (k // nka, i, k % nka)),
                  pl.BlockSpec((None, D, tk), lambda i, k: (k // nsub, 0, k % nsub)),
                  pl.BlockSpec((tm, D), row, pipeline_mode=pl.Buffered(1)),
                  pl.BlockSpec((tm, D), row, pipeline_mode=pl.Buffered(1)), pl.BlockSpec((1, D), lambda i, k: (0, 0))],
        out_specs=[pl.BlockSpec((tm, D), row), pl.BlockSpec((tm, D), row), pl.BlockSpec((1, D), lambda i, k: (0, 0))],
        out_shape=[jax.ShapeDtypeStruct((S, D), F32), jax.ShapeDtypeStruct((S, D), BF16),
                   jax.ShapeDtypeStruct((1, D), F32)],
        scratch_shapes=[pltpu.VMEM((tm, D), F32)], compiler_params=_cp(2, 56))(a3, wg, resid, xin, g, *deps)


def _unblock_call(wg, name):
    nb, K, bn = wg.shape

    def body(w_ref, o_ref):
        o_ref[...] = w_ref[...]

    return pl.pallas_call(
        body, name=name, grid=(nb,),
        in_specs=[pl.BlockSpec((None, K, bn), lambda o: (o, 0, 0))],
        out_specs=pl.BlockSpec((K, bn), lambda o: (0, o)),
        out_shape=jax.ShapeDtypeStruct((K, nb * bn), wg.dtype), compiler_params=_cp(1))(wg)


def _mm_nt_norm_call(a, w, resid, xin, g, name, deps=(), part=(0, 1), dx_so_far=None):
    S, K = a.shape
    D = w.shape[0]
    tm = _tile(S, 256, 16)
    tiles = (S // tm) // part[1]
    first = part[0] * tiles

    def body(a_ref, w_ref, r_ref, x_ref, g_ref, *rest):
        dx_ref, dg_ref = rest[-2:]

        @pl.when(pl.program_id(0) == 0)
        def _():
            dg_ref[...] = jnp.zeros_like(dg_ref)

        dh = lax.dot_general(a_ref[...], w_ref[...], _DN_NT, preferred_element_type=F32)
        xv = x_ref[...]
        r = lax.rsqrt(_rows_mean(xv * xv) + RMS_EPS)
        xn = xv * r
        dg_ref[...] += _col_sum(dh * xn)
        dyg = dh * g_ref[...]
        dx_ref[...] = r_ref[...] + r * (dyg - xn * _rows_mean(dyg * xn))

    row = lambda i: (i + first, 0)
    fixed = lambda i: (0, 0)
    in_specs = [pl.BlockSpec((tm, K), row), pl.BlockSpec((D, K), fixed, pipeline_mode=pl.Buffered(1)),
                pl.BlockSpec((tm, D), row), pl.BlockSpec((tm, D), row), pl.BlockSpec((1, D), fixed)]
    operands = [a, w, resid, xin, g]
    aliases = {}
    if dx_so_far is not None:
        in_specs.append(pl.BlockSpec(memory_space=pl.ANY))
        operands.append(dx_so_far)
        aliases = {5: 0}
    return _pallas(
        body, len(operands), deps, name=name, grid=(tiles,),
        in_specs=in_specs, out_specs=[pl.BlockSpec((tm, D), row), pl.BlockSpec((1, D), fixed)],
        out_shape=[jax.ShapeDtypeStruct((S, D), F32), jax.ShapeDtypeStruct((1, D), F32)],
        input_output_aliases=aliases, compiler_params=_cp(1, 56))(*operands, *deps)


def _mm_tn_cols_call(a, b3, nb, bn, name, deps=()):
    S, Ka = a.shape
    nh, _, Fb = b3.shape
    tm = _tile(S, 2048, 16)
    tn = _tile(bn, 768, LANES)
    nsub = bn // tn
    njb = Fb // tn
    J = nb * nsub
    assert nh * njb == J

    def body(a_ref, b_ref, o_ref):
        @pl.when(pl.program_id(1) == 0)
        def _():
            o_ref[...] = jnp.zeros_like(o_ref)

        o_ref[...] += lax.dot_general(a_ref[...], b_ref[...], _DN_TN, preferred_element_type=F32)

    return _pallas(
        body, 2, deps, name=name, grid=(J, S // tm),
        in_specs=[pl.BlockSpec((tm, Ka), lambda j, i: (i, 0)),
                  pl.BlockSpec((None, tm, tn), lambda j, i: (j // njb, i, j % njb))],
        out_specs=pl.BlockSpec((None, Ka, tn), lambda j, i: (j // nsub, 0, j % nsub)),
        out_shape=jax.ShapeDtypeStruct((nb, Ka, bn), F32), compiler_params=_cp(2, 56))(a, b3, *deps)


def _mm_tn_rows_call(a, b, name, deps=()):
    S, E = a.shape
    D = b.shape[1]
    tm = _tile(S, 2048, 16)
    te = _tile(E, 768, LANES)

    def body(a_ref, b_ref, o_ref):
        @pl.when(pl.program_id(1) == 0)
        def _():
            o_ref[...] = jnp.zeros_like(o_ref)

        o_ref[...] += lax.dot_general(a_ref[...], b_ref[...], _DN_TN, preferred_element_type=F32)

    return _pallas(
        body, 2, deps, name=name, grid=(E // te, S // tm),
        in_specs=[pl.BlockSpec((tm, te), lambda j, i: (i, j)), pl.BlockSpec((tm, D), lambda j, i: (i, 0))],
        out_specs=pl.BlockSpec((te, D), lambda j, i: (j, 0)),
        out_shape=jax.ShapeDtypeStruct((E, D), F32), compiler_params=_cp(2, 56))(a, b, *deps)


def _ffn_tiles(S, Fd):
    return _tile(S, 512, 16), _tile(Fd // (N_DEV // 2), 1536, LANES)


def _taps(cw_ref):
    return [cw_ref[k:k + 1, :] for k in range(cw_ref.shape[0])]


def _conv3(prev8, cur, taps):
    s1 = _shift_down(prev8, cur, 1)
    s2 = _shift_down(prev8, cur, 2)
    return taps[2] * cur + taps[1] * s1 + taps[0] * s2, s1, s2


SUB_LANES = 256


def _lane_taps(cw_ref, ls):
    return [cw_ref[k:k + 1, ls] for k in range(cw_ref.shape[0])]


def _ffn_up_act_call(h2, wg, cw, cb, name, deps=()):
    S, D = h2.shape
    nb, _, bn = wg.shape
    Fd = nb * bn // 2
    tm, tc = _ffn_tiles(S, Fd)
    nk = Fd // tc
    hb = tm // 16
    nsubw = bn // tc
    half = nb // 2
    sc = _tile(tc, SUB_LANES, LANES)

    def body(a_ref, ap_ref, wgate_ref, wval_ref, cwg_ref, cwv_ref, cbg_ref, cbv_ref, up_ref, upc_ref, f_ref):
        keep = jnp.where(pl.program_id(0) == 0, 0.0, 1.0)
        a_ext = jnp.concatenate([ap_ref[...], a_ref[...]], axis=0)
        nsub = tc // sc
        lanes = [slice(s * sc, (s + 1) * sc) for s in range(nsub)]

        def products(s):
            return [jnp.dot(a_ext, w_ref[:, lanes[s]], preferred_element_type=F32).astype(BF16)
                    for w_ref in (wgate_ref, wval_ref)]

        ready = products(0)
        for s in range(nsub):
            ls = lanes[s]
            following = products(s + 1) if s + 1 < nsub else None

            def conv_half(ub, cw_ref, cb_ref, slab):
                up_ref[slab, :, ls] = ub[16:]
                u = ub.astype(F32)
                conv, _, _ = _conv3(u[8:16] * keep, u[16:], _lane_taps(cw_ref, ls))
                c = conv + cb_ref[:, ls]
                upc_ref[slab, :, ls] = c.astype(BF16)
                return c

            cg = conv_half(ready[0], cwg_ref, cbg_ref, 0)
            cv = conv_half(ready[1], cwv_ref, cbv_ref, 1)
            f_ref[:, ls] = (_gelu(cg) * cv).astype(BF16)
            ready = following

    return _pallas(
        body, 8, deps, name=name, grid=(S // tm, nk),
        in_specs=[pl.BlockSpec((tm, D), lambda i, k: (i, 0)),
                  pl.BlockSpec((16, D), lambda i, k: (jnp.maximum(i * hb - 1, 0), 0)),
                  pl.BlockSpec((None, D, tc), lambda i, k: (k // nsubw, 0, k % nsubw)),
                  pl.BlockSpec((None, D, tc), lambda i, k: (half + k // nsubw, 0, k % nsubw)),
                  pl.BlockSpec((3, tc), lambda i, k: (0, k)), pl.BlockSpec((3, tc), lambda i, k: (0, k + nk)),
                  pl.BlockSpec((1, tc), lambda i, k: (0, k)), pl.BlockSpec((1, tc), lambda i, k: (0, k + nk))],
        out_specs=[pl.BlockSpec((2, tm, tc), lambda i, k: (0, i, k)), pl.BlockSpec((2, tm, tc), lambda i, k: (0, i, k)),
                   pl.BlockSpec((tm, tc), lambda i, k: (i, k))],
        out_shape=[jax.ShapeDtypeStruct((2, S, Fd), BF16), jax.ShapeDtypeStruct((2, S, Fd), BF16),
                   jax.ShapeDtypeStruct((S, Fd), BF16)],
        compiler_params=_cp(2, 56))(h2, h2, wg, wg, cw, cw, cb, cb, *deps)


def _ffn_down_dx_act_bwd_call(dxb, w, up3, upc3, cw, name, deps=()):
    S, D = dxb.shape
    _, _, Fd = up3.shape
    tm, tc = _ffn_tiles(S, Fd)
    nj = Fd // tc
    ni = S // tm
    hb = tm // 16
    sc = _tile(tc, SUB_LANES, LANES)

    def body(a_ref, an_ref, w_ref, g_ref, v_ref, cg_ref, cv_ref, cgn_ref, cvn_ref, cwg_ref, cwv_ref,
             dup_ref, dcwg_ref, dcwv_ref, dcbg_ref, dcbv_ref):
        i = pl.program_id(1)
        keep_next = jnp.where(i == ni - 1, 0.0, 1.0)

        @pl.when(i == 0)
        def _():
            for r in (dcwg_ref, dcwv_ref, dcbg_ref, dcbv_ref):
                r[...] = jnp.zeros_like(r)

        a_ext = jnp.concatenate([a_ref[...], an_ref[...]], axis=0)
        for s in range(tc // sc):
            ls = slice(s * sc, (s + 1) * sc)
            df_ext = lax.dot_general(a_ext, w_ref[s * sc:(s + 1) * sc, :], _DN_NT, preferred_element_type=F32)
            df = jnp.concatenate([df_ext[:tm], df_ext[tm:tm + 8] * keep_next], axis=0)
            cg = jnp.concatenate([cg_ref[:, ls].astype(F32), cgn_ref[:, ls].astype(F32)[:8]], axis=0)
            cv = jnp.concatenate([cv_ref[:, ls].astype(F32), cvn_ref[:, ls].astype(F32)[:8]], axis=0)
            gel, dgel = _gelu_parts(cg)

            def back(d, cw_ref, x_ref, dcw_ref, dcb_ref, slab):
                taps = _lane_taps(cw_ref, ls)
                d0 = d[:tm]
                d1 = pltpu.roll(d, tm + 8 - 1, 0)[:tm]
                d2 = pltpu.roll(d, tm + 8 - 2, 0)[:tm]
                dup_ref[slab, :, ls] = (taps[2] * d0 + taps[1] * d1 + taps[0] * d2).astype(BF16)
                xv = x_ref[:, ls].astype(F32)
                dcw_ref[2:3, ls] += _col_sum(xv * d0)
                dcw_ref[1:2, ls] += _col_sum(xv * d1)
                dcw_ref[0:1, ls] += _col_sum(xv * d2)
                dcb_ref[:, ls] += _col_sum(d0)

            back(df * cv * dgel, cwg_ref, g_ref, dcwg_ref, dcbg_ref, 0)
            back(df * gel, cwv_ref, v_ref, dcwv_ref, dcbv_ref, 1)

    nxt = lambda j, i: jnp.minimum((i + 1) * hb, S // 16 - 1)
    main = lambda s: pl.BlockSpec((None, tm, tc), lambda j, i: (s, i, j))
    halo = lambda s: pl.BlockSpec((None, 16, tc), lambda j, i: (s, nxt(j, i), j))
    acc3 = pl.BlockSpec((3, tc), lambda j, i: (0, j))
    acc1 = pl.BlockSpec((1, tc), lambda j, i: (0, j))
    return _pallas(
        body, 11, deps, name=name, grid=(nj, ni),
        in_specs=[pl.BlockSpec((tm, D), lambda j, i: (i, 0)), pl.BlockSpec((16, D), lambda j, i: (nxt(j, i), 0)),
                  pl.BlockSpec((tc, D), lambda j, i: (j, 0)),
                  main(0), main(1), main(0), main(1), halo(0), halo(1),
                  pl.BlockSpec((3, tc), lambda j, i: (0, j)), pl.BlockSpec((3, tc), lambda j, i: (0, j + nj))],
        out_specs=[pl.BlockSpec((2, tm, tc), lambda j, i: (0, i, j)), acc3, acc3, acc1, acc1],
        out_shape=[jax.ShapeDtypeStruct((2, S, Fd), BF16), jax.ShapeDtypeStruct((3, Fd), F32),
                   jax.ShapeDtypeStruct((3, Fd), F32), jax.ShapeDtypeStruct((1, Fd), F32),
                   jax.ShapeDtypeStruct((1, Fd), F32)],
        compiler_params=_cp(2, 56))(dxb, dxb, w, up3, up3, upc3, upc3, upc3, upc3, cw, cw, *deps)


def _gm_forward_tile(pv, vg, vb, ws_ref, bsb_ref, mbuf, H, nc):
    W = H * CHUNK
    z, dz = _gelu_parts(pv)
    u, v0 = z[:, :W], z[:, W:]
    xc = v0 - _rows_mean(v0)
    rs = lax.rsqrt(_rows_mean(xc * xc) + LN_EPS)
    vh = xc * rs
    vnb = (vh * vg + vb).astype(BF16)
    mask = lax.broadcasted_iota(jnp.int32, (CHUNK, CHUNK), 0) >= lax.broadcasted_iota(jnp.int32, (CHUNK, CHUNK), 1)
    for h in range(H):
        cs = slice(h * CHUNK, (h + 1) * CHUNK)
        wm = jnp.where(mask, ws_ref[h], 0.0).astype(BF16)
        vcat = jnp.concatenate([vnb[c * CHUNK:(c + 1) * CHUNK, cs] for c in range(nc)], axis=1)
        mix = jnp.dot(wm, vcat, preferred_element_type=F32)
        for c in range(nc):
            mbuf[c * CHUNK:(c + 1) * CHUNK, cs] = mix[:, c * CHUNK:(c + 1) * CHUNK] + bsb_ref[h]
    return dz, u, vh, rs, vnb, mask


def _gm_fwd_call(p, v_g, v_b, ws, bsb, out_g, name, deps=()):
    S = p.shape[0]
    H = ws.shape[0]
    W = H * CHUNK
    tm = _tile(S, 256, CHUNK)
    nc = tm // CHUNK

    def body(p_ref, vg_ref, vb_ref, ws_ref, bsb_ref, og_ref, y_ref, mbuf):
        _, u, _, _, _, _ = _gm_forward_tile(p_ref[...], vg_ref[...], vb_ref[...], ws_ref, bsb_ref, mbuf, H, nc)
        yg = u * mbuf[...]
        r = lax.rsqrt(_rows_mean(yg * yg) + RMS_EPS)
        y_ref[...] = (yg * r * og_ref[...]).astype(BF16)

    vec = pl.BlockSpec((1, W), lambda i: (0, 0))
    mat = pl.BlockSpec((H, CHUNK, CHUNK), lambda i: (0, 0, 0))
    return _pallas(
        body, 6, deps, name=name, grid=(S // tm,),
        in_specs=[pl.BlockSpec((tm, 2 * W), lambda i: (i, 0)), vec, vec, mat, mat, vec],
        out_specs=pl.BlockSpec((tm, W), lambda i: (i, 0)),
        out_shape=jax.ShapeDtypeStruct((S, 2 * W), BF16),
        scratch_shapes=[pltpu.VMEM((tm, W), F32)], compiler_params=_cp(1))(p, v_g, v_b, ws, bsb, out_g, *deps)


def _gm_bwd_call(p, d_y, v_g, v_b, ws, bsb, out_g, name, deps=()):
    S = p.shape[0]
    H = ws.shape[0]
    W = H * CHUNK
    tm = _tile(S, 256, CHUNK)
    nc = tm // CHUNK
    ni = S // tm

    def body(p_ref, dy_ref, vg_ref, vb_ref, ws_ref, bsb_ref, og_ref,
             dp_ref, dvg_ref, dvb_ref, dws_ref, dbs_ref, dog_ref, mbuf, dvbuf):
        i = pl.program_id(0)

        @pl.when(i == 0)
        def _():
            for r in (dvg_ref, dvb_ref, dws_ref, dbs_ref, dog_ref):
                r[...] = jnp.zeros_like(r)

        vg = vg_ref[...]
        dz, u, vh, rs, vnb, mask = _gm_forward_tile(p_ref[...], vg, vb_ref[...], ws_ref, bsb_ref, mbuf, H, nc)
        mixed = mbuf[...]
        yg = u * mixed
        r = lax.rsqrt(_rows_mean(yg * yg) + RMS_EPS)
        yn = yg * r
        dya = dy_ref[...]
        dog_ref[...] += _col_sum(dya * yn)
        dyg = dya * og_ref[...]
        dygm = r * (dyg - yn * _rows_mean(dyg * yn))
        du = dygm * mixed
        dmix = dygm * u
        dmb = dmix.astype(BF16)
        for h in range(H):
            cs = slice(h * CHUNK, (h + 1) * CHUNK)
            wm = jnp.where(mask, ws_ref[h], 0.0).astype(BF16)
            dcat = jnp.concatenate([dmb[c * CHUNK:(c + 1) * CHUNK, cs] for c in range(nc)], axis=1)
            vcat = jnp.concatenate([vnb[c * CHUNK:(c + 1) * CHUNK, cs] for c in range(nc)], axis=1)
            dvn = lax.dot_general(wm, dcat, _DN_TN, preferred_element_type=F32)
            dws_ref[h] += jnp.where(mask, lax.dot_general(dcat, vcat, _DN_NT, preferred_element_type=F32), 0.0)
            dbs = dmix[0:CHUNK, cs]
            for c in range(1, nc):
                dbs = dbs + dmix[c * CHUNK:(c + 1) * CHUNK, cs]
            dbs_ref[h] += dbs
            for c in range(nc):
                dvbuf[c * CHUNK:(c + 1) * CHUNK, cs] = dvn[:, c * CHUNK:(c + 1) * CHUNK]
        dvn_all = dvbuf[...]
        dvg_ref[...] += _col_sum(dvn_all * vh)
        dvb_ref[...] += _col_sum(dvn_all)
        dvh = dvn_all * vg
        dv0 = rs * (dvh - _rows_mean(dvh) - vh * _rows_mean(dvh * vh))
        dp_ref[...] = (jnp.concatenate([du, dv0], axis=1) * dz).astype(BF16)

        @pl.when(i == ni - 1)
        def _():
            for h in range(H):
                dbs_ref[h] = jnp.broadcast_to(jnp.sum(dbs_ref[h], axis=1, keepdims=True), (CHUNK, CHUNK))

    vec = pl.BlockSpec((1, W), lambda i: (0, 0))
    mat = pl.BlockSpec((H, CHUNK, CHUNK), lambda i: (0, 0, 0))
    vshape = jax.ShapeDtypeStruct((1, W), F32)
    mshape = jax.ShapeDtypeStruct((H, CHUNK, CHUNK), F32)
    return _pallas(
        body, 7, deps, name=name, grid=(ni,),
        in_specs=[pl.BlockSpec((tm, 2 * W), lambda i: (i, 0)), pl.BlockSpec((tm, W), lambda i: (i, 0)),
                  vec, vec, mat, mat, vec],
        out_specs=[pl.BlockSpec((tm, 2 * W), lambda i: (i, 0)), vec, vec, mat, mat, vec],
        out_shape=[jax.ShapeDtypeStruct((S, 4 * W), BF16), vshape, vshape, mshape, mshape, vshape],
        scratch_shapes=[pltpu.VMEM((tm, W), F32), pltpu.VMEM((tm, W), F32)],
        compiler_params=_cp(1))(p, d_y, v_g, v_b, ws, bsb, out_g, *deps)


def _lru_gates(prev8, xl, cw, cb, wa_ref, ba, wx_ref, bx, lam, H):
    sh = [_shift_down(prev8, xl, k) for k in range(4)]
    xr = cw[3] * sh[0] + cw[2] * sh[1] + cw[1] * sh[2] + cw[0] * sh[3] + cb
    xrb = xr.astype(BF16)
    rp, ip = [], []
    for h in range(H):
        cs = slice(h * CHUNK, (h + 1) * CHUNK)
        rp.append(jnp.dot(xrb[:, cs], wa_ref[h].astype(BF16), preferred_element_type=F32))
        ip.append(jnp.dot(xrb[:, cs], wx_ref[h].astype(BF16), preferred_element_type=F32))
    r = _sigmoid(jnp.concatenate(rp, axis=1) + ba)
    ig = _sigmoid(jnp.concatenate(ip, axis=1) + bx)
    sp = _softplus(-lam)
    t = jnp.tanh((-LRU_C) * r * sp)
    q = lax.rsqrt(1.0 - t)
    a = jnp.sqrt(1.0 + t) * q
    mult = jnp.sqrt(-2.0 * t) * q
    a2_over_mult = (1.0 + t) * q * lax.rsqrt(-2.0 * t)
    return xr, xrb, r, ig, sp, a, mult, a2_over_mult, sh


def _lru_fwd_call(p, cw, cb, wa, ba, wx, bx, lam, out_g, y_half, name):
    S = p.shape[0]
    H = wa.shape[0]
    W = H * CHUNK
    tm = _tile(S, 256, 16)
    ng = tm // 8

    def body(pg_ref, px_ref, cw_ref, cb_ref, wa_ref, ba_ref, wx_ref, bx_ref, lam_ref, og_ref, y_in_ref,
             y_ref, h_ref, xprev, hcar, abuf, bbuf):
        @pl.when(pl.program_id(0) == 0)
        def _():
            xprev[...] = jnp.zeros_like(xprev)
            hcar[...] = jnp.zeros_like(hcar)

        xl = px_ref[...]
        xr, _, _, ig, _, a, mult, _, _ = _lru_gates(xprev[...], xl, _taps(cw_ref), cb_ref[...], wa_ref, ba_ref[...],
                                                    wx_ref, bx_ref[...], lam_ref[...], H)
        xprev[...] = xl[tm - 8:]
        b = mult * (ig * xr)
        sub = lax.broadcasted_iota(jnp.int32, (tm, W), 0) & 7
        for d in (1, 2, 4):
            m = sub >= d
            a_s = jnp.where(m, pltpu.roll(a, d, 0), 1.0)
            b_s = jnp.where(m, pltpu.roll(b, d, 0), 0.0)
            b = a * b_s + b
            a = a * a_s
        abuf[...] = a
        bbuf[...] = b

        def step(g, carry):
            r0 = pl.multiple_of(g * 8, 8)
            h_ref[pl.ds(r0, 8), :] = abuf[pl.ds(r0, 8), :] * carry + bbuf[pl.ds(r0, 8), :]
            return jnp.broadcast_to(h_ref[pl.ds(r0 + 7, 1), :], (8, W))

        hcar[...] = lax.fori_loop(0, ng, step, hcar[...])
        yl = h_ref[...] * _gelu(pg_ref[...])
        r = lax.rsqrt(_rows_mean(yl * yl) + RMS_EPS)
        y_ref[...] = (yl * r * og_ref[...]).astype(BF16)

    vec = pl.BlockSpec((1, W), lambda i: (0, 0))
    mat = pl.BlockSpec((H, CHUNK, CHUNK), lambda i: (0, 0, 0))
    return pl.pallas_call(
        body, name=name, grid=(S // tm,),
        in_specs=[pl.BlockSpec((tm, W), lambda i: (i, 2)), pl.BlockSpec((tm, W), lambda i: (i, 3)),
                  pl.BlockSpec((4, W), lambda i: (0, 0)), vec, mat, vec, mat, vec, vec, vec,
                  pl.BlockSpec(memory_space=pl.ANY)],
        out_specs=[pl.BlockSpec((tm, W), lambda i: (i, 1)), pl.BlockSpec((tm, W), lambda i: (i, 0))],
        out_shape=[jax.ShapeDtypeStruct((S, 2 * W), BF16), jax.ShapeDtypeStruct((S, W), F32)],
        input_output_aliases={10: 0},
        scratch_shapes=[pltpu.VMEM((8, W), F32), pltpu.VMEM((8, W), F32), pltpu.VMEM((tm, W), F32),
                        pltpu.VMEM((tm, W), F32)],
        compiler_params=_cp(1))(p, p, cw, cb, wa, ba, wx, bx, lam, out_g, y_half)


def _lru_bwd_call(p, hs, d_y, cw, cb, wa, ba, wx, bx, lam, out_g, dp_half, name):
    S = p.shape[0]
    H = wa.shape[0]
    W = H * CHUNK
    tm = _tile(S, 256, 16)
    ng = tm // 8
    ni = S // tm
    hb = tm // 8

    def body(pg_ref, px_ref, pxp_ref, h_ref, hp_ref, dy_ref, cw_ref, cb_ref, wa_ref, ba_ref, wx_ref, bx_ref,
             lam_ref, og_ref, dp_in_ref,
             dp_ref, dcw_ref, dcb_ref, dwa_ref, dba_ref, dwx_ref, dbx_ref, dlam_ref, dog_ref,
             a_next, e_next, dxr_next, abuf, bbuf, ebuf):
        i = pl.program_id(0)
        ri = ni - 1 - i

        @pl.when(i == 0)
        def _():
            for r in (dcw_ref, dcb_ref, dwa_ref, dba_ref, dwx_ref, dbx_ref, dlam_ref, dog_ref,
                      a_next, e_next, dxr_next):
                r[...] = jnp.zeros_like(r)

        keep_prev = jnp.where(ri == 0, 0.0, 1.0)
        cw_ = _taps(cw_ref)
        lam_ = lam_ref[...]
        xl = px_ref[...]
        xr, xrb, r, ig, sp, a, mult, a2m, sh = _lru_gates(pxp_ref[...] * keep_prev, xl, cw_, cb_ref[...], wa_ref,
                                                          ba_ref[...], wx_ref, bx_ref[...], lam_, H)
        gg, dgg = _gelu_parts(pg_ref[...])
        hv = h_ref[...]
        yl = hv * gg
        rr = lax.rsqrt(_rows_mean(yl * yl) + RMS_EPS)
        yn = yl * rr
        dyb = dy_ref[...]
        dog_ref[...] += _col_sum(dyb * yn)
        dyg = dyb * og_ref[...]
        dyl = rr * (dyg - yn * _rows_mean(dyg * yn))
        dh = dyl * gg
        dgl = dyl * hv * dgg

        an = _shift_up(a, a_next[...], 1)
        eb = dh
        sub = lax.broadcasted_iota(jnp.int32, (tm, W), 0) & 7
        for d in (1, 2, 4):
            m = sub < 8 - d
            a_s = jnp.where(m, pltpu.roll(an, tm - d, 0), 1.0)
            e_s = jnp.where(m, pltpu.roll(eb, tm - d, 0), 0.0)
            eb = an * e_s + eb
            an = an * a_s
        abuf[...] = an
        bbuf[...] = eb

        def step(g, carry):
            r0 = pl.multiple_of((ng - 1 - g) * 8, 8)
            ebuf[pl.ds(r0, 8), :] = abuf[pl.ds(r0, 8), :] * carry + bbuf[pl.ds(r0, 8), :]
            return jnp.broadcast_to(ebuf[pl.ds(r0, 1), :], (8, W))

        lax.fori_loop(0, ng, step, jnp.broadcast_to(e_next[0:1, :], (8, W)))
        e = ebuf[...]
        a_next[...] = a[0:8]
        e_next[...] = e[0:8]

        hm1 = _shift_down(hp_ref[...] * keep_prev, hv, 1)
        da = e * hm1
        dmult = e * ig * xr
        di = e * mult * xr
        dxr = e * mult * ig
        dla = da * a - dmult * a2m
        dr = dla * ((-LRU_C) * sp)
        dlam_ref[...] += _col_sum(dla * ((-LRU_C) * r))
        dpr = dr * r * (1.0 - r)
        dpi = di * ig * (1.0 - ig)
        dba_ref[...] += _col_sum(dpr)
        dbx_ref[...] += _col_sum(dpi)
        dprb = dpr.astype(BF16)
        dpib = dpi.astype(BF16)
        back = []
        for h in range(H):
            cs = slice(h * CHUNK, (h + 1) * CHUNK)
            wab = wa_ref[h].astype(BF16)
            wxb = wx_ref[h].astype(BF16)
            back.append(lax.dot_general(dprb[:, cs], wab, _DN_NT, preferred_element_type=F32)
                        + lax.dot_general(dpib[:, cs], wxb, _DN_NT, preferred_element_type=F32))
            dwa_ref[h] += lax.dot_general(xrb[:, cs], dprb[:, cs], _DN_TN, preferred_element_type=F32)
            dwx_ref[h] += lax.dot_general(xrb[:, cs], dpib[:, cs], _DN_TN, preferred_element_type=F32)
        dxr = dxr + jnp.concatenate(back, axis=1)

        nxt = dxr_next[...]
        dxl = (cw_[3] * dxr + cw_[2] * _shift_up(dxr, nxt, 1) + cw_[1] * _shift_up(dxr, nxt, 2)
               + cw_[0] * _shift_up(dxr, nxt, 3))
        dxr_next[...] = dxr[0:8]
        for k in range(4):
            dcw_ref[k:k + 1, :] += _col_sum(sh[3 - k] * dxr)
        dcb_ref[...] += _col_sum(dxr)
        dp_ref[...] = jnp.concatenate([dgl, dxl], axis=1).astype(BF16)

        @pl.when(i == ni - 1)
        def _():
            dlam_ref[...] = -dlam_ref[...] * _sigmoid(-lam_)

    vec = pl.BlockSpec((1, W), lambda i: (0, 0))
    mat = pl.BlockSpec((H, CHUNK, CHUNK), lambda i: (0, 0, 0))
    rev = lambda i: ni - 1 - i
    prev = lambda i: jnp.maximum(rev(i) * hb - 1, 0)
    vshape = jax.ShapeDtypeStruct((1, W), F32)
    mshape = jax.ShapeDtypeStruct((H, CHUNK, CHUNK), F32)
    tile = lambda: pltpu.VMEM((tm, W), F32)
    car = lambda: pltpu.VMEM((8, W), F32)
    return pl.pallas_call(
        body, name=name, grid=(ni,),
        in_specs=[pl.BlockSpec((tm, W), lambda i: (rev(i), 2)), pl.BlockSpec((tm, W), lambda i: (rev(i), 3)),
                  pl.BlockSpec((8, W), lambda i: (prev(i), 3)),
                  pl.BlockSpec((tm, W), lambda i: (rev(i), 0)), pl.BlockSpec((8, W), lambda i: (prev(i), 0)),
                  pl.BlockSpec((tm, W), lambda i: (rev(i), 1)),
                  pl.BlockSpec((4, W), lambda i: (0, 0)), vec, mat, vec, mat, vec, vec, vec,
                  pl.BlockSpec(memory_space=pl.ANY)],
        out_specs=[pl.BlockSpec((tm, 2 * W), lambda i: (rev(i), 1)), pl.BlockSpec((4, W), lambda i: (0, 0)), vec,
                   mat, vec, mat, vec, vec, vec],
        out_shape=[jax.ShapeDtypeStruct((S, 4 * W), BF16), jax.ShapeDtypeStruct((4, W), F32), vshape,
                   mshape, vshape, mshape, vshape, vshape, vshape],
        input_output_aliases={14: 0},
        scratch_shapes=[car(), car(), car(), tile(), tile(), tile()],
        compiler_params=_cp(1, 56))(p, p, p, hs, hs, d_y, cw, cb, wa, ba, wx, bx, lam, out_g, dp_half)


def _rows128(a):
    return a.reshape(-1, LANES).astype(F32)


def _pack(arrays, pad_to=256):
    flat = jnp.concatenate([_rows128(a) for a in arrays], axis=0)
    pad = (-flat.shape[0]) % pad_to
    if pad:
        flat = jnp.concatenate([flat, jnp.zeros((pad, LANES), F32)], axis=0)
    return flat


def _unpack(flat, shapes):
    out, r = [], 0
    for s in shapes:
        n = 1
        for d in s:
            n *= d
        out.append(flat[r:r + n // LANES].reshape(s))
        r += n // LANES
    return out


def kernel(x, norm1_g, w_in, gm_v_g, gm_v_b, gm_ws, gm_bs, lru_conv_w, lru_conv_b, lru_wa, lru_ba, lru_wx, lru_bx, lru_lambda, gm_out_g, lru_out_g, w_out, norm2_g, ffn_w_up, ffn_conv_w, ffn_conv_b, ffn_w_down, final_g, loss_target, m_norm1_g, m_w_in, m_gm_v_g, m_gm_v_b, m_gm_ws, m_gm_bs, m_lru_conv_w, m_lru_conv_b, m_lru_wa, m_lru_ba, m_lru_wx, m_lru_bx, m_lru_lambda, m_gm_out_g, m_lru_out_g, m_w_out, m_norm2_g, m_ffn_w_up, m_ffn_conv_w, m_ffn_conv_b, m_ffn_w_down, m_final_g, v_norm1_g, v_w_in, v_gm_v_g, v_gm_v_b, v_gm_ws, v_gm_bs, v_lru_conv_w, v_lru_conv_b, v_lru_wa, v_lru_ba, v_lru_wx, v_lru_bx, v_lru_lambda, v_gm_out_g, v_lru_out_g, v_w_out, v_norm2_g, v_ffn_w_up, v_ffn_conv_w, v_ffn_conv_b, v_ffn_w_down, v_final_g):
    wts = dict(norm1_g=norm1_g, w_in=w_in, gm_v_g=gm_v_g, gm_v_b=gm_v_b, gm_ws=gm_ws, gm_bs=gm_bs,
               lru_conv_w=lru_conv_w, lru_conv_b=lru_conv_b, lru_wa=lru_wa, lru_ba=lru_ba, lru_wx=lru_wx,
               lru_bx=lru_bx, lru_lambda=lru_lambda, gm_out_g=gm_out_g, lru_out_g=lru_out_g, w_out=w_out,
               norm2_g=norm2_g, ffn_w_up=ffn_w_up, ffn_conv_w=ffn_conv_w, ffn_conv_b=ffn_conv_b,
               ffn_w_down=ffn_w_down, final_g=final_g)
    mom = dict(norm1_g=m_norm1_g, w_in=m_w_in, gm_v_g=m_gm_v_g, gm_v_b=m_gm_v_b, gm_ws=m_gm_ws, gm_bs=m_gm_bs,
               lru_conv_w=m_lru_conv_w, lru_conv_b=m_lru_conv_b, lru_wa=m_lru_wa, lru_ba=m_lru_ba, lru_wx=m_lru_wx,
               lru_bx=m_lru_bx, lru_lambda=m_lru_lambda, gm_out_g=m_gm_out_g, lru_out_g=m_lru_out_g, w_out=m_w_out,
               norm2_g=m_norm2_g, ffn_w_up=m_ffn_w_up, ffn_conv_w=m_ffn_conv_w, ffn_conv_b=m_ffn_conv_b,
               ffn_w_down=m_ffn_w_down, final_g=m_final_g)
    var = dict(norm1_g=v_norm1_g, w_in=v_w_in, gm_v_g=v_gm_v_g, gm_v_b=v_gm_v_b, gm_ws=v_gm_ws, gm_bs=v_gm_bs,
               lru_conv_w=v_lru_conv_w, lru_conv_b=v_lru_conv_b, lru_wa=v_lru_wa, lru_ba=v_lru_ba, lru_wx=v_lru_wx,
               lru_bx=v_lru_bx, lru_lambda=v_lru_lambda, gm_out_g=v_gm_out_g, lru_out_g=v_lru_out_g, w_out=v_w_out,
               norm2_g=v_norm2_g, ffn_w_up=v_ffn_w_up, ffn_conv_w=v_ffn_conv_w, ffn_conv_b=v_ffn_conv_b,
               ffn_w_down=v_ffn_w_down, final_g=v_final_g)

    xi, yi, ci = lax.axis_index("x"), lax.axis_index("y"), lax.axis_index("c")
    chip = 2 * xi + yi
    dev = 2 * chip + ci
    core_chip = jnp.stack([ci, chip]).astype(jnp.int32)

    xs = x[0]
    tgt = loss_target[0]
    S, D = xs.shape
    H = gm_ws.shape[1]
    W = H * CHUNK
    Fd = ffn_w_down.shape[1] * N_DEV
    lcw_cols = lru_conv_w.shape[2]
    fcw_cols = ffn_conv_w.shape[2]

    dev1 = jnp.reshape(dev, (1,)).astype(jnp.int32)

    def gather_start(shards, name, after=()):
        lands = [_place_own_call(s, dev1, "%s_own%d" % (name, k)) for k, s in enumerate(shards)]
        return _exchange_start(shards, lands, 4 * len(shards), _gather_stage1_copies(len(shards)), name + "_ici", after)

    def gather_forward(lands, name, after=()):
        return _exchange_start([], lands, 3 * len(lands), _gather_stage2_copies(len(lands)), name + "_d2d", after)

    def pair_start(g, name, after=()):
        return _exchange_start([g], [lax.empty((4,) + g.shape[1:], F32)], 4, _pair_copies(1), name, after)

    def chip_start(p16, name, after=()):
        return _exchange_start([p16], [lax.empty((3,) + p16.shape[1:], BF16)], 3, _chip_copies(1), name, after)

    vgm_g, vgm_b = gm_v_g, gm_v_b
    ws, wa, wx = gm_ws[0], lru_wa[0], lru_wx[0]
    bsb = jnp.broadcast_to(gm_bs[0][:, :, None], (H, CHUNK, CHUNK))
    ba, bx = lru_ba.reshape(1, W), lru_bx.reshape(1, W)
    fcb = ffn_conv_b
    fing = final_g.reshape(1, D)

    conv_pack = _pack([lru_conv_w[0], ffn_conv_w[0]], pad_to=8)
    ga1 = gather_start([w_in[0].astype(BF16), conv_pack], "gather_in")
    h1 = _rmsnorm_call(xs, norm1_g, "norm1", deps=(ga1.token,))
    _, la = _exchange_wait(ga1, after=(h1,))
    ga2 = gather_forward(la, "gather_in")
    gb1 = gather_start([w_out[0].astype(BF16)], "gather_out", after=(ga2.token,))
    gc1 = gather_start([ffn_w_up[0].astype(BF16)], "gather_up", after=(gb1.token,))
    _, (win_g, conv_g) = _exchange_wait(ga2, after=(gc1.token,))
    n_l = 4 * lcw_cols // LANES
    n_f = 3 * fcw_cols // LANES
    lcw = conv_g[:, :n_l].reshape(N_DEV, 4, lcw_cols).transpose(1, 0, 2).reshape(4, N_DEV * lcw_cols)
    fcw = conv_g[:, n_l:n_l + n_f].reshape(N_DEV, 3, fcw_cols).transpose(1, 0, 2).reshape(3, N_DEV * fcw_cols)

    p = _mm_blocked_call(h1, win_g, F32, "in_proj")
    win_rows = _unblock_call(win_g, "w_in_rows")
    _, lb = _exchange_wait(gb1, after=(p,))
    gb2 = gather_forward(lb, "gather_out")
    y_half = _gm_fwd_call(p, vgm_g, vgm_b, ws, bsb, gm_out_g, "gmlp_fwd", deps=(gb2.token,))
    y, hs = _lru_fwd_call(p, lcw, lru_conv_b, wa, ba, wx, bx, lru_lambda, lru_out_g, y_half, "lru_fwd")
    _, (wout_g,) = _exchange_wait(gb2, after=(y,))
    wout_full = wout_g.reshape(D, D)
    x2 = _mm_out_call(xs, y, wout_full, "out_proj")
    h2 = _rmsnorm_call(x2, norm2_g, "norm2")
    _, lc = _exchange_wait(gc1, after=(h2,))
    gc2 = gather_forward(lc, "gather_up")
    gd1 = gather_start([ffn_w_down[0].astype(BF16)], "gather_down", after=(gc2.token,))
    _, (wup_g,) = _exchange_wait(gc2, after=(gd1.token,))
    up3, upc3, f = _ffn_up_act_call(h2, wup_g, fcw, fcb, "ffn_up")
    _, ld = _exchange_wait(gd1, after=(f,))
    gd2 = gather_forward(ld, "gather_down")
    _, (wdown_g,) = _exchange_wait(gd2)
    wdown_full = wdown_g.reshape(Fd, D)
    dx3, dx3b, loss_acc, d_final = _mm_down_loss_call(x2, f, wdown_full, fing, tgt, "ffn_down_loss")

    g_wdown = _mm_tn_rows_call(f, dx3b, "ffn_down_dw").reshape((N_DEV,) + ffn_w_down.shape[1:])
    pd = pair_start(g_wdown, "pair_down")
    d_up3, dfcw_g, dfcw_v, dfcb_g, dfcb_v = _ffn_down_dx_act_bwd_call(dx3b, wdown_full, up3, upc3, fcw, "ffn_down_dx",
                                                                     deps=(pd.token,))
    (g_wdown,), (r1,) = _exchange_wait(pd, after=(d_up3,))
    own_down, p16 = _pair_add_call(g_wdown, r1, core_chip, "pair_add_down")
    cd = chip_start(p16, "chip_down")
    g_wup = _mm_tn_cols_call(h2, d_up3, N_DEV, ffn_w_up.shape[2], "ffn_up_dw", deps=(cd.token,))
    pu = pair_start(g_wup, "pair_up")
    dx2, dx2b, d_norm2 = _mm_dx_norm_call(d_up3, wup_g, dx3, x2, norm2_g, "ffn_up_dx", deps=(pu.token,))
    g_wout = _mm_tn_rows_call(y, dx2b, "out_proj_dw").reshape((N_DEV,) + w_out.shape[1:])
    po = pair_start(g_wout, "pair_out")
    d_y = _mm_nt_call(dx2b, wout_full, F32, "out_proj_dx", deps=(po.token,))
    (g_wup,), (r1,) = _exchange_wait(pu, after=(d_y,))
    own_up, p16 = _pair_add_call(g_wup, r1, core_chip, "pair_add_up")
    _, (r2_down,) = _exchange_wait(cd, after=(p16,))
    cu = chip_start(p16, "chip_up", after=(r2_down,))
    dp_half, d_vg, d_vb, d_ws, d_bs, d_gog = _gm_bwd_call(p, d_y, vgm_g, vgm_b, ws, bsb, gm_out_g, "gmlp_bwd",
                                                          deps=(cu.token,))
    d_p2, d_lcw, d_lcb, d_wa, d_ba, d_wx, d_bx, d_lam, d_log = _lru_bwd_call(
        p, hs, d_y, lcw, lru_conv_b, wa, ba, wx, bx, lru_lambda, lru_out_g, dp_half, "lru_bwd")
    d_p = d_p2[None]
    (g_wout,), (r1,) = _exchange_wait(po, after=(d_p,))
    own_out, p16_out = _pair_add_call(g_wout, r1, core_chip, "pair_add_out")
    g_win = _mm_tn_cols_call(h1, d_p, N_DEV, w_in.shape[2], "in_proj_dw")
    pi = pair_start(g_win, "pair_in")
    _, (r2_up,) = _exchange_wait(cu, after=(g_win,))
    co = chip_start(p16_out, "chip_out", after=(r2_up,))
    gx_a, dn_a = _mm_nt_norm_call(d_p[0], win_rows, dx2, xs, norm1_g, "in_proj_dx_a", deps=(co.token, pi.token),
                                  part=(0, 2))
    (g_win,), (r1,) = _exchange_wait(pi, after=(gx_a,))
    own_in, p16 = _pair_add_call(g_win, r1, core_chip, "pair_add_in")
    _, (r2_out,) = _exchange_wait(co, after=(p16,))
    ci_ = chip_start(p16, "chip_in", after=(r2_out,))
    grad_x, dn_b = _mm_nt_norm_call(d_p[0], win_rows, dx2, xs, norm1_g, "in_proj_dx_b", deps=(ci_.token,),
                                    part=(1, 2), dx_so_far=gx_a)

    small_g = dict(norm1_g=dn_a + dn_b, gm_v_g=d_vg, gm_v_b=d_vb, gm_ws=d_ws, gm_bs=d_bs[:, :, 0], lru_conv_b=d_lcb,
                   lru_wa=d_wa, lru_ba=d_ba, lru_wx=d_wx, lru_bx=d_bx, lru_lambda=d_lam, gm_out_g=d_gog,
                   lru_out_g=d_log, norm2_g=d_norm2,
                   ffn_conv_b=jnp.concatenate([dfcb_g, dfcb_v], axis=1), final_g=d_final)
    rep = _pack([small_g[n] for n in SMALL])
    conv_part = _pack([d_lcw, jnp.concatenate([dfcw_g, dfcw_v], axis=1)], pad_to=8)
    n_rep, n_conv = rep.shape[0], conv_part.shape[0]
    gs1 = gather_start([jnp.concatenate([rep, conv_part], axis=0)], "gather_small")

    def adamw_big(n, own, r2, deps=()):
        return _adamw_call(wts[n][0], mom[n][0], var[n][0], [(own, None), (r2, 0), (r2, 1), (r2, 2)], "adamw_" + n, deps)

    res = {}
    res["ffn_w_down"] = adamw_big("ffn_w_down", own_down, r2_down, (gs1.token,))
    res["ffn_w_up"] = adamw_big("ffn_w_up", own_up, r2_up, (gs1.token,))
    res["w_out"] = adamw_big("w_out", own_out, r2_out, (gs1.token,))
    _, ls = _exchange_wait(gs1, after=(res["w_out"][0], res["ffn_w_up"][0], res["ffn_w_down"][0]))
    gs2 = gather_forward(ls, "gather_small")
    _, (r2_in,) = _exchange_wait(ci_, after=(gs2.token,))
    res["w_in"] = adamw_big("w_in", own_in, r2_in)
    _, (parts,) = _exchange_wait(gs2, after=(res["w_in"][0],))
    g_rep, d_rep, m_rep, v_rep = _adamw_call(
        _pack([wts[n] for n in SMALL]), _pack([mom[n] for n in SMALL]), _pack([var[n] for n in SMALL]),
        [(parts, k) for k in range(N_DEV)], "adamw_small")
    shapes = [wts[n].shape for n in SMALL]
    for n, g_, d_, m_, v_ in zip(SMALL, _unpack(g_rep, shapes), _unpack(d_rep, shapes), _unpack(m_rep, shapes),
                                 _unpack(v_rep, shapes)):
        res[n] = (g_, d_, m_, v_)
    conv_sum = _sum_call(parts, n_rep, n_conv, "sum_conv_grads")
    g_lcw = conv_sum[:4 * W // LANES].reshape(4, W)
    g_fcw = conv_sum[4 * W // LANES:4 * W // LANES + 6 * Fd // LANES].reshape(3, 2 * Fd)
    for n, full in (("lru_conv_w", g_lcw), ("ffn_conv_w", g_fcw)):
        cols = wts[n].shape[2]
        mine = lax.dynamic_slice_in_dim(full, dev * cols, cols, axis=1)
        res[n] = _adamw_call(wts[n][0], mom[n][0], var[n][0], [(mine, None)], "adamw_" + n)

    loss = lax.psum(loss_acc[0, 0], ("x", "y", "c"))
    outs = [[], [], [], []]
    for n in WEIGHTS:
        for k in range(4):
            outs[k].append(res[n][k].reshape(wts[n].shape))
    return (loss, grad_x[None], *outs[0], *outs[1], *outs[2], *outs[3])
```

```python
import functools
import math

import jax
import jax.numpy as jnp
from jax import lax
from jax.experimental import pallas as pl
from jax.experimental.pallas import tpu as pltpu

F32 = jnp.float32
BF16 = jnp.bfloat16

RMS_EPS = 1e-6
LN_EPS = 1e-5
LRU_C = 8.0
CHUNK = 128
ADAM_LR = 0.001
ADAM_B1 = 0.9
ADAM_B2 = 0.999
ADAM_EPS = 1e-08
ADAM_WD = 0.01
ADAM_STEP = 10
N_DEV = 8
LANES = 128
MIB = 1024 * 1024

WEIGHTS = ['norm1_g', 'w_in', 'gm_v_g', 'gm_v_b', 'gm_ws', 'gm_bs', 'lru_conv_w', 'lru_conv_b', 'lru_wa', 'lru_ba',
           'lru_wx', 'lru_bx', 'lru_lambda', 'gm_out_g', 'lru_out_g', 'w_out', 'norm2_g', 'ffn_w_up', 'ffn_conv_w',
           'ffn_conv_b', 'ffn_w_down', 'final_g']
BIG = ['w_in', 'w_out', 'ffn_w_up', 'ffn_w_down']
CONV = ['lru_conv_w', 'ffn_conv_w']
SMALL = [n for n in WEIGHTS if n not in BIG and n not in CONV]

_DN_NT = (((1,), (1,)), ((), ()))
_DN_TN = (((0,), (0,)), ((), ()))
_GELU_C = 0.7978845608028654


def _cp(n_axes, vmem_mib=48):
    return pltpu.CompilerParams(dimension_semantics=("arbitrary",) * n_axes, vmem_limit_bytes=vmem_mib * MIB)


def _tile(n, pref, mult=8):
    t = min(pref, n)
    t -= t % mult
    while t >= mult:
        if n % t == 0:
            return t
        t -= mult
    return n


def _gelu(z):
    return 0.5 * z * (1.0 + jnp.tanh(_GELU_C * z * (1.0 + 0.044715 * z * z)))


def _gelu_parts(z):
    z2 = z * z
    t = jnp.tanh(_GELU_C * z * (1.0 + 0.044715 * z2))
    g = 0.5 * z * (1.0 + t)
    dg = 0.5 * (1.0 + t) + 0.5 * z * (1.0 - t * t) * (_GELU_C * (1.0 + 0.134145 * z2))
    return g, dg


def _sigmoid(z):
    return 0.5 + 0.5 * jnp.tanh(0.5 * z)


def _softplus(z):
    t = jnp.exp(-jnp.abs(z))
    u = 1.0 + t
    log1p = jnp.where(u == 1.0, t, jnp.log(u) * t / (u - 1.0))
    return jnp.maximum(z, 0.0) + log1p


def _rows_mean(v):
    return jnp.mean(v, axis=-1, keepdims=True)


def _col_sum(v):
    return jnp.sum(v, axis=0, keepdims=True)


def _shift_down(prev8, cur, k):
    if k == 0:
        return cur
    z = jnp.concatenate([prev8, cur], axis=0)
    return pltpu.roll(z, k, 0)[8:]


def _shift_up(cur, next8, k):
    if k == 0:
        return cur
    n = cur.shape[0]
    z = jnp.concatenate([cur, next8], axis=0)
    return pltpu.roll(z, n + 8 - k, 0)[:n]


def _mesh_pos():
    return lax.axis_index("x"), lax.axis_index("y"), lax.axis_index("c")


def _any_specs(n):
    return [pl.BlockSpec(memory_space=pl.ANY)] * n


def _pallas(body, n_in, deps, **kw):
    nd = len(deps)
    if not nd:
        return pl.pallas_call(body, **kw)

    def ordered(*refs):
        body(*refs[:n_in], *refs[n_in + nd:])

    kw["in_specs"] = list(kw["in_specs"]) + _any_specs(nd)
    return pl.pallas_call(ordered, **kw)


_HBM = pl.BlockSpec(memory_space=pltpu.HBM)
_SEM = pl.BlockSpec(memory_space=pltpu.SEMAPHORE)
_EFFECT = pltpu.SideEffectType.DATAFLOW_SIDE_EFFECTING


class _InFlight:
    def __init__(self, sems, bufs, token, n_src, n_copies, make_copies, name):
        self.sems, self.bufs, self.token = sems, bufs, token
        self.n_src, self.n_copies, self.make_copies, self.name = n_src, n_copies, make_copies, name


def _exchange_start(srcs, lands, n_copies, make_copies, name, after=()):
    bufs = list(srcs) + list(lands)
    nb, na = len(bufs), len(after)
    ns = len(srcs)

    def body(*refs):
        b_refs = refs[:nb]
        outs = refs[nb + na:]
        send, recv = outs[:n_copies], outs[n_copies:2 * n_copies]
        token = outs[-1]
        for cp in make_copies(b_refs[:ns], b_refs[ns:], send, recv):
            cp.start()
        token[...] = jnp.zeros_like(token)

    out = pl.pallas_call(
        body, name=name,
        out_shape=[pltpu.SemaphoreType.DMA(())] * (2 * n_copies) + [pltpu.HBM(b.shape, b.dtype) for b in bufs]
        + [jax.ShapeDtypeStruct((8, LANES), F32)],
        in_specs=[_HBM] * nb + _any_specs(na),
        out_specs=[_SEM] * (2 * n_copies) + [_HBM] * nb + [pl.BlockSpec(memory_space=pltpu.VMEM)],
        input_output_aliases={i: 2 * n_copies + i for i in range(nb)},
        compiler_params=pltpu.CompilerParams(has_side_effects=_EFFECT),
    )(*[pltpu.with_memory_space_constraint(b, pltpu.HBM) for b in bufs], *after)
    return _InFlight(out[:2 * n_copies], out[2 * n_copies:2 * n_copies + nb], out[-1], ns, n_copies, make_copies, name)


def _exchange_wait(fl, after=()):
    nb, na, nc, ns = len(fl.bufs), len(after), fl.n_copies, fl.n_src

    def body(*refs):
        b_refs = refs[:nb]
        sems = refs[nb:nb + 2 * nc]
        copies = fl.make_copies(b_refs[:ns], b_refs[ns:], sems[:nc], sems[nc:])
        for cp in copies:
            cp.wait_send()
        for cp in copies:
            cp.wait_recv()

    out = pl.pallas_call(
        body, name=fl.name + "_wait",
        out_shape=[pltpu.HBM(b.shape, b.dtype) for b in fl.bufs],
        in_specs=[_HBM] * nb + [_SEM] * (2 * nc) + _any_specs(na),
        out_specs=[_HBM] * nb,
        input_output_aliases={i: i for i in range(nb)},
        compiler_params=pltpu.CompilerParams(has_side_effects=_EFFECT),
    )(*fl.bufs, *fl.sems, *after)
    return list(out[:ns]), list(out[ns:])


def _remote(src, dst, send_sem, recv_sem, to):
    return pltpu.make_async_remote_copy(src_ref=src, dst_ref=dst, send_sem=send_sem, recv_sem=recv_sem,
                                        device_id=to, device_id_type=pl.DeviceIdType.MESH)


def _gather_stage1_copies(n, to_sibling=True, to_chips=True):
    def make(s_refs, l_refs, send, recv):
        x, y, c = _mesh_pos()
        own = 4 * x + 2 * y + c
        targets = ([(x, y, 1 - c)] if to_sibling else []) + (
            [(1 - x, y, c), (x, 1 - y, c), (1 - x, 1 - y, c)] if to_chips else [])
        m = len(targets)
        return [_remote(l_refs[a].at[own], l_refs[a].at[own], send[m * a + k], recv[m * a + k], to)
                for a in range(n) for k, to in enumerate(targets)]
    return make


def _gather_stage2_copies(n):
    def make(s_refs, l_refs, send, recv):
        x, y, c = _mesh_pos()
        blocks = [4 * (1 - x) + 2 * y + c, 4 * x + 2 * (1 - y) + c, 4 * (1 - x) + 2 * (1 - y) + c]
        return [_remote(l_refs[a].at[b], l_refs[a].at[b], send[3 * a + j], recv[3 * a + j], (x, y, 1 - c))
                for a in range(n) for j, b in enumerate(blocks)]
    return make


def _pair_copies(n):
    def make(s_refs, l_refs, send, recv):
        x, y, c = _mesh_pos()
        return [_remote(s_refs[a].at[2 * k + 1 - c], l_refs[a].at[k], send[4 * a + k], recv[4 * a + k], (x, y, 1 - c))
                for a in range(n) for k in range(4)]
    return make


def _chip_copies(n):
    def make(s_refs, l_refs, send, recv):
        x, y, c = _mesh_pos()
        chips = [(1 - x, y), (x, 1 - y), (1 - x, 1 - y)]
        return [_remote(s_refs[a].at[2 * ch[0] + ch[1]], l_refs[a].at[j], send[3 * a + j], recv[3 * a + j], (*ch, c))
                for a in range(n) for j, ch in enumerate(chips)]
    return make


def _place_own_call(shard, dev, dtype, name):
    R, C = shard.shape
    tr = _tile(R, max(16, MIB // (C * 4)), 16)

    def body(d_ref, s_ref, o_ref):
        o_ref[...] = s_ref[...].astype(dtype)

    grid_spec = pltpu.PrefetchScalarGridSpec(
        num_scalar_prefetch=1, grid=(R // tr,),
        in_specs=[pl.BlockSpec((tr, C), lambda r, d: (r, 0))],
        out_specs=pl.BlockSpec((None, tr, C), lambda r, d: (d[0], r, 0)))
    return pl.pallas_call(body, name=name, grid_spec=grid_spec,
                          out_shape=jax.ShapeDtypeStruct((N_DEV, R, C), dtype), compiler_params=_cp(1))(dev, shard)


def _pair_add_call(g, r1, core_chip, name):
    _, R, C = g.shape
    tr = _tile(R, max(16, (2 * MIB) // (C * 4)), 16)

    def body(cc_ref, g_ref, r_ref, p32_ref, p16_ref):
        s = g_ref[...] + r_ref[...]
        p16_ref[...] = s.astype(BF16)

        @pl.when(pl.program_id(1) == cc_ref[1])
        def _():
            p32_ref[...] = s

    grid_spec = pltpu.PrefetchScalarGridSpec(
        num_scalar_prefetch=1, grid=(R // tr, 4),
        in_specs=[pl.BlockSpec((None, tr, C), lambda r, k, cc: (2 * k + cc[0], r, 0)),
                  pl.BlockSpec((None, tr, C), lambda r, k, cc: (k, r, 0))],
        out_specs=[pl.BlockSpec((tr, C), lambda r, k, cc: (r, 0)),
                   pl.BlockSpec((None, tr, C), lambda r, k, cc: (k, r, 0))])
    return pl.pallas_call(
        body, name=name, grid_spec=grid_spec,
        out_shape=[jax.ShapeDtypeStruct((R, C), F32), jax.ShapeDtypeStruct((4, R, C), BF16)],
        compiler_params=_cp(2))(core_chip, g, r1)


def _adamw_call(w, m, v, addends, name, deps=()):
    R, C = w.shape
    tr = _tile(R, max(8, (MIB // 2) // (C * 4)), 16)
    na = len(addends)
    c1 = 1.0 - ADAM_B1 ** ADAM_STEP
    c2 = 1.0 - ADAM_B2 ** ADAM_STEP

    def body(*refs):
        w_ref, m_ref, v_ref = refs[:3]
        a_refs = refs[3:3 + na]
        g_ref, d_ref, nm_ref, nv_ref = refs[3 + na:]
        g = a_refs[0][...].astype(F32)
        for a_ref in a_refs[1:]:
            g = g + a_ref[...].astype(F32)
        nm = ADAM_B1 * m_ref[...] + (1.0 - ADAM_B1) * g
        nv = ADAM_B2 * v_ref[...] + (1.0 - ADAM_B2) * (g * g)
        g_ref[...] = g
        nm_ref[...] = nm
        nv_ref[...] = nv
        d_ref[...] = -ADAM_LR * ((nm / c1) / (jnp.sqrt(nv / c2) + ADAM_EPS) + ADAM_WD * w_ref[...])

    flat = pl.BlockSpec((tr, C), lambda r: (r, 0))
    a_specs = [flat if k is None else pl.BlockSpec((None, tr, C), functools.partial(lambda r, kk: (kk, r, 0), kk=k))
               for _, k in addends]
    out = jax.ShapeDtypeStruct((R, C), F32)
    return _pallas(
        body, 3 + na, deps, name=name, grid=(R // tr,),
        in_specs=[flat, flat, flat] + a_specs, out_specs=[flat] * 4, out_shape=[out] * 4,
        compiler_params=_cp(1))(w, m, v, *[a for a, _ in addends], *deps)


def _sum_call(parts, row0, rows, name):
    n = parts.shape[0]
    tr = _tile(math.gcd(row0, rows), 256, 8)
    b0 = row0 // tr

    def body(p_ref, o_ref):
        s = p_ref[0]
        for k in range(1, n):
            s = s + p_ref[k]
        o_ref[...] = s

    return pl.pallas_call(
        body, name=name, grid=(rows // tr,),
        in_specs=[pl.BlockSpec((n, tr, LANES), lambda r: (0, r + b0, 0))],
        out_specs=pl.BlockSpec((tr, LANES), lambda r: (r, 0)),
        out_shape=jax.ShapeDtypeStruct((rows, LANES), F32), compiler_params=_cp(1))(parts)


def _rmsnorm_call(x, g, name, deps=()):
    S, D = x.shape
    tm = _tile(S, 512, 16)

    def body(x_ref, g_ref, o_ref):
        xv = x_ref[...]
        r = lax.rsqrt(_rows_mean(xv * xv) + RMS_EPS)
        o_ref[...] = (xv * r * g_ref[...]).astype(BF16)

    return _pallas(
        body, 2, deps, name=name, grid=(S // tm,),
        in_specs=[pl.BlockSpec((tm, D), lambda i: (i, 0)), pl.BlockSpec((1, D), lambda i: (0, 0))],
        out_specs=pl.BlockSpec((tm, D), lambda i: (i, 0)),
        out_shape=jax.ShapeDtypeStruct((S, D), BF16), compiler_params=_cp(1))(x, g, *deps)


def _mm_some_blocks_call(a, wg, blocks, out_dtype, name, out_so_far=None):
    S, K = a.shape
    nb, _, bn = wg.shape
    tm = _tile(S, 1024, 16)

    def body(b_ref, a_ref, w_ref, *rest):
        rest[-1][...] = jnp.dot(a_ref[...], w_ref[...], preferred_element_type=F32).astype(out_dtype)

    in_specs = [pl.BlockSpec((tm, K), lambda i, j, b: (i, 0)), pl.BlockSpec((None, K, bn), lambda i, j, b: (b[j], 0, 0))]
    operands = [a, wg]
    aliases = {}
    if out_so_far is not None:
        in_specs.append(pl.BlockSpec(memory_space=pl.ANY))
        operands.append(out_so_far)
        aliases = {3: 0}
    grid_spec = pltpu.PrefetchScalarGridSpec(
        num_scalar_prefetch=1, grid=(S // tm, blocks.shape[0]), in_specs=in_specs,
        out_specs=pl.BlockSpec((tm, bn), lambda i, j, b: (i, b[j])))
    return pl.pallas_call(body, name=name, grid_spec=grid_spec,
                          out_shape=jax.ShapeDtypeStruct((S, nb * bn), out_dtype), input_output_aliases=aliases,
                          compiler_params=_cp(2))(blocks, *operands)


def _mm_out_call(x, y, w, name):
    S, D = x.shape
    tm = _tile(S, 512, 16)

    def body(x_ref, y_ref, w_ref, o_ref):
        o_ref[...] = x_ref[...] + jnp.dot(y_ref[...], w_ref[...], preferred_element_type=F32)

    return pl.pallas_call(
        body, name=name, grid=(S // tm,),
        in_specs=[pl.BlockSpec((tm, D), lambda i: (i, 0)), pl.BlockSpec((tm, D), lambda i: (i, 0)),
                  pl.BlockSpec((D, D), lambda i: (0, 0))],
        out_specs=pl.BlockSpec((tm, D), lambda i: (i, 0)),
        out_shape=jax.ShapeDtypeStruct((S, D), F32), compiler_params=_cp(1))(x, y, w)


def _mm_nt_call(a, w, out_dtype, name, deps=()):
    S, K = a.shape
    N = w.shape[0]
    tm = _tile(S, 1024, 16)
    tn = _tile(N, 768, LANES)

    def body(a_ref, w_ref, o_ref):
        o_ref[...] = lax.dot_general(a_ref[...], w_ref[...], _DN_NT, preferred_element_type=F32).astype(out_dtype)

    return _pallas(
        body, 2, deps, name=name, grid=(S // tm, N // tn),
        in_specs=[pl.BlockSpec((tm, K), lambda i, j: (i, 0)), pl.BlockSpec((tn, K), lambda i, j: (j, 0))],
        out_specs=pl.BlockSpec((tm, tn), lambda i, j: (i, j)),
        out_shape=jax.ShapeDtypeStruct((S, N), out_dtype), compiler_params=_cp(2))(a, w, *deps)


def _mm_down_loss_call(x2, f, w, final_g, target, name):
    S, D = x2.shape
    Fd = f.shape[1]
    tm = _tile(S, 512, 16)
    tk = _tile(Fd, 768, LANES)
    nk = Fd // tk

    def body(x_ref, f_ref, w_ref, g_ref, t_ref, dx_ref, dxb_ref, loss_ref, dg_ref, acc):
        i, k = pl.program_id(0), pl.program_id(1)

        @pl.when(jnp.logical_and(i == 0, k == 0))
        def _():
            loss_ref[...] = jnp.zeros_like(loss_ref)
            dg_ref[...] = jnp.zeros_like(dg_ref)

        @pl.when(k == 0)
        def _():
            acc[...] = jnp.zeros_like(acc)

        acc[...] += jnp.dot(f_ref[...], w_ref[...], preferred_element_type=F32)

        @pl.when(k == nk - 1)
        def _():
            x3 = x_ref[...] + acc[...]
            r = lax.rsqrt(_rows_mean(x3 * x3) + RMS_EPS)
            g = g_ref[...]
            xn = x3 * r
            diff = xn * g - t_ref[...]
            loss_ref[...] += 0.5 * jnp.sum(_rows_mean(diff * diff))
            dout = diff * (1.0 / D)
            dg_ref[...] += _col_sum(dout * xn)
            dyg = dout * g
            dx = r * (dyg - xn * _rows_mean(dyg * xn))
            dx_ref[...] = dx
            dxb_ref[...] = dx.astype(BF16)

    row = lambda i, k: (i, 0)
    return pl.pallas_call(
        body, name=name, grid=(S // tm, nk),
        in_specs=[pl.BlockSpec((tm, D), row), pl.BlockSpec((tm, tk), lambda i, k: (i, k)),
                  pl.BlockSpec((tk, D), lambda i, k: (k, 0)), pl.BlockSpec((1, D), lambda i, k: (0, 0)),
                  pl.BlockSpec((tm, D), row)],
        out_specs=[pl.BlockSpec((tm, D), row), pl.BlockSpec((tm, D), row),
                   pl.BlockSpec((8, LANES), lambda i, k: (0, 0)), pl.BlockSpec((1, D), lambda i, k: (0, 0))],
        out_shape=[jax.ShapeDtypeStruct((S, D), F32), jax.ShapeDtypeStruct((S, D), BF16),
                   jax.ShapeDtypeStruct((8, LANES), F32), jax.ShapeDtypeStruct((1, D), F32)],
        scratch_shapes=[pltpu.VMEM((tm, D), F32)], compiler_params=_cp(2, 56))(x2, f, w, final_g, target)


def _mm_dx_norm_call(a3, wg, resid, xin, g, name, deps=()):
    na, S, Fa = a3.shape
    nb, D, bn = wg.shape
    tm = _tile(S, 512, 16)
    tk = _tile(bn, 1536, LANES)
    nsub = bn // tk
    nka = Fa // tk
    nk = nb * nsub
    assert na * nka == nk

    def body(a_ref, w_ref, r_ref, x_ref, g_ref, dx_ref, dxb_ref, dg_ref, acc):
        i, k = pl.program_id(0), pl.program_id(1)

        @pl.when(jnp.logical_and(i == 0, k == 0))
        def _():
            dg_ref[...] = jnp.zeros_like(dg_ref)

        @pl.when(k == 0)
        def _():
            acc[...] = jnp.zeros_like(acc)

        acc[...] += lax.dot_general(a_ref[...], w_ref[...], _DN_NT, preferred_element_type=F32)

        @pl.when(k == nk - 1)
        def _():
            dh = acc[...]
            xv = x_ref[...]
            r = lax.rsqrt(_rows_mean(xv * xv) + RMS_EPS)
            xn = xv * r
            dg_ref[...] += _col_sum(dh * xn)
            dyg = dh * g_ref[...]
            dx = r_ref[...] + r * (dyg - xn * _rows_mean(dyg * xn))
            dx_ref[...] = dx
            dxb_ref[...] = dx.astype(BF16)

    row = lambda i, k: (i, 0)
    return _pallas(
        body, 5, deps, name=name, grid=(S // tm, nk),
        in_specs=[pl.BlockSpec((None, tm, tk), lambda i, k: (k // nka, i, k % nka)),
                  pl.BlockSpec((None, D, tk), lambda i, k: (k // nsub, 0, k % nsub)),
                  pl.BlockSpec((tm, D), row, pipeline_mode=pl.Buffered(1)),
                  pl.BlockSpec((tm, D), row, pipeline_mode=pl.Buffered(1)), pl.BlockSpec((1, D), lambda i, k: (0, 0))],
        out_specs=[pl.BlockSpec((tm, D), row), pl.BlockSpec((tm, D), row), pl.BlockSpec((1, D), lambda i, k: (0, 0))],
        out_shape=[jax.ShapeDtypeStruct((S, D), F32), jax.ShapeDtypeStruct((S, D), BF16),
                   jax.ShapeDtypeStruct((1, D), F32)],
        scratch_shapes=[pltpu.VMEM((tm, D), F32)], compiler_params=_cp(2, 56))(a3, wg, resid, xin, g, *deps)


def _unblock_call(wg, name):
    nb, K, bn = wg.shape

    def body(w_ref, o_ref):
        o_ref[...] = w_ref[...]

    return pl.pallas_call(
        body, name=name, grid=(nb,),
        in_specs=[pl.BlockSpec((None, K, bn), lambda o: (o, 0, 0))],
        out_specs=pl.BlockSpec((K, bn), lambda o: (0, o)),
        out_shape=jax.ShapeDtypeStruct((K, nb * bn), wg.dtype), compiler_params=_cp(1))(wg)


def _mm_nt_norm_call(a, w, resid, xin, g, name, deps=(), part=(0, 1), dx_so_far=None):
    S, K = a.shape
    D = w.shape[0]
    tm = _tile(S, 256, 16)
    tiles = (S // tm) // part[1]
    first = part[0] * tiles

    def body(a_ref, w_ref, r_ref, x_ref, g_ref, *rest):
        dx_ref, dg_ref = rest[-2:]

        @pl.when(pl.program_id(0) == 0)
        def _():
            dg_ref[...] = jnp.zeros_like(dg_ref)

        dh = lax.dot_general(a_ref[...], w_ref[...], _DN_NT, preferred_element_type=F32)
        xv = x_ref[...]
        r = lax.rsqrt(_rows_mean(xv * xv) + RMS_EPS)
        xn = xv * r
        dg_ref[...] += _col_sum(dh * xn)
        dyg = dh * g_ref[...]
        dx_ref[...] = r_ref[...] + r * (dyg - xn * _rows_mean(dyg * xn))

    row = lambda i: (i + first, 0)
    fixed = lambda i: (0, 0)
    in_specs = [pl.BlockSpec((tm, K), row), pl.BlockSpec((D, K), fixed, pipeline_mode=pl.Buffered(1)),
                pl.BlockSpec((tm, D), row), pl.BlockSpec((tm, D), row), pl.BlockSpec((1, D), fixed)]
    operands = [a, w, resid, xin, g]
    aliases = {}
    if dx_so_far is not None:
        in_specs.append(pl.BlockSpec(memory_space=pl.ANY))
        operands.append(dx_so_far)
        aliases = {5: 0}
    return _pallas(
        body, len(operands), deps, name=name, grid=(tiles,),
        in_specs=in_specs, out_specs=[pl.BlockSpec((tm, D), row), pl.BlockSpec((1, D), fixed)],
        out_shape=[jax.ShapeDtypeStruct((S, D), F32), jax.ShapeDtypeStruct((1, D), F32)],
        input_output_aliases=aliases, compiler_params=_cp(1, 56))(*operands, *deps)


def _mm_tn_cols_call(a, b3, nb, bn, name, deps=()):
    S, Ka = a.shape
    nh, _, Fb = b3.shape
    tm = _tile(S, 2048, 16)
    tn = _tile(bn, 768, LANES)
    nsub = bn // tn
    njb = Fb // tn
    J = nb * nsub
    assert nh * njb == J

    def body(a_ref, b_ref, o_ref):
        @pl.when(pl.program_id(1) == 0)
        def _():
            o_ref[...] = jnp.zeros_like(o_ref)

        o_ref[...] += lax.dot_general(a_ref[...], b_ref[...], _DN_TN, preferred_element_type=F32)

    return _pallas(
        body, 2, deps, name=name, grid=(J, S // tm),
        in_specs=[pl.BlockSpec((tm, Ka), lambda j, i: (i, 0)),
                  pl.BlockSpec((None, tm, tn), lambda j, i: (j // njb, i, j % njb))],
        out_specs=pl.BlockSpec((None, Ka, tn), lambda j, i: (j // nsub, 0, j % nsub)),
        out_shape=jax.ShapeDtypeStruct((nb, Ka, bn), F32), compiler_params=_cp(2, 56))(a, b3, *deps)


def _mm_tn_rows_call(a, b, name, deps=()):
    S, E = a.shape
    D = b.shape[1]
    tm = _tile(S, 2048, 16)
    te = _tile(E, 768, LANES)

    def body(a_ref, b_ref, o_ref):
        @pl.when(pl.program_id(1) == 0)
        def _():
            o_ref[...] = jnp.zeros_like(o_ref)

        o_ref[...] += lax.dot_general(a_ref[...], b_ref[...], _DN_TN, preferred_element_type=F32)

    return _pallas(
        body, 2, deps, name=name, grid=(E // te, S // tm),
        in_specs=[pl.BlockSpec((tm, te), lambda j, i: (i, j)), pl.BlockSpec((tm, D), lambda j, i: (i, 0))],
        out_specs=pl.BlockSpec((te, D), lambda j, i: (j, 0)),
        out_shape=jax.ShapeDtypeStruct((E, D), F32), compiler_params=_cp(2, 56))(a, b, *deps)


def _ffn_tiles(S, Fd):
    return _tile(S, 512, 16), _tile(Fd // (N_DEV // 2), 1536, LANES)


def _taps(cw_ref):
    return [cw_ref[k:k + 1, :] for k in range(cw_ref.shape[0])]


def _conv3(prev8, cur, taps):
    s1 = _shift_down(prev8, cur, 1)
    s2 = _shift_down(prev8, cur, 2)
    return taps[2] * cur + taps[1] * s1 + taps[0] * s2, s1, s2


SUB_LANES = 256


def _lane_taps(cw_ref, ls):
    return [cw_ref[k:k + 1, ls] for k in range(cw_ref.shape[0])]


def _ffn_up_act_call(h2, wg, cw, cb, name, deps=()):
    S, D = h2.shape
    nb, _, bn = wg.shape
    Fd = nb * bn // 2
    tm, tc = _ffn_tiles(S, Fd)
    nk = Fd // tc
    hb = tm // 16
    nsubw = bn // tc
    half = nb // 2
    sc = _tile(tc, SUB_LANES, LANES)

    def body(a_ref, ap_ref, wgate_ref, wval_ref, cwg_ref, cwv_ref, cbg_ref, cbv_ref, up_ref, upc_ref, f_ref):
        keep = jnp.where(pl.program_id(0) == 0, 0.0, 1.0)
        a_ext = jnp.concatenate([ap_ref[...], a_ref[...]], axis=0)
        nsub = tc // sc
        lanes = [slice(s * sc, (s + 1) * sc) for s in range(nsub)]

        def products(s):
            return [jnp.dot(a_ext, w_ref[:, lanes[s]], preferred_element_type=F32).astype(BF16)
                    for w_ref in (wgate_ref, wval_ref)]

        ready = products(0)
        for s in range(nsub):
            ls = lanes[s]
            following = products(s + 1) if s + 1 < nsub else None

            def conv_half(ub, cw_ref, cb_ref, slab):
                up_ref[slab, :, ls] = ub[16:]
                u = ub.astype(F32)
                conv, _, _ = _conv3(u[8:16] * keep, u[16:], _lane_taps(cw_ref, ls))
                c = conv + cb_ref[:, ls]
                upc_ref[slab, :, ls] = c.astype(BF16)
                return c

            cg = conv_half(ready[0], cwg_ref, cbg_ref, 0)
            cv = conv_half(ready[1], cwv_ref, cbv_ref, 1)
            f_ref[:, ls] = (_gelu(cg) * cv).astype(BF16)
            ready = following

    return _pallas(
        body, 8, deps, name=name, grid=(S // tm, nk),
        in_specs=[pl.BlockSpec((tm, D), lambda i, k: (i, 0)),
                  pl.BlockSpec((16, D), lambda i, k: (jnp.maximum(i * hb - 1, 0), 0)),
                  pl.BlockSpec((None, D, tc), lambda i, k: (k // nsubw, 0, k % nsubw)),
                  pl.BlockSpec((None, D, tc), lambda i, k: (half + k // nsubw, 0, k % nsubw)),
                  pl.BlockSpec((3, tc), lambda i, k: (0, k)), pl.BlockSpec((3, tc), lambda i, k: (0, k + nk)),
                  pl.BlockSpec((1, tc), lambda i, k: (0, k)), pl.BlockSpec((1, tc), lambda i, k: (0, k + nk))],
        out_specs=[pl.BlockSpec((2, tm, tc), lambda i, k: (0, i, k)), pl.BlockSpec((2, tm, tc), lambda i, k: (0, i, k)),
                   pl.BlockSpec((tm, tc), lambda i, k: (i, k))],
        out_shape=[jax.ShapeDtypeStruct((2, S, Fd), BF16), jax.ShapeDtypeStruct((2, S, Fd), BF16),
                   jax.ShapeDtypeStruct((S, Fd), BF16)],
        compiler_params=_cp(2, 56))(h2, h2, wg, wg, cw, cw, cb, cb, *deps)


def _ffn_down_dx_act_bwd_call(dxb, w, up3, upc3, cw, name, deps=()):
    S, D = dxb.shape
    _, _, Fd = up3.shape
    tm, tc = _ffn_tiles(S, Fd)
    nj = Fd // tc
    ni = S // tm
    hb = tm // 16
    sc = _tile(tc, SUB_LANES, LANES)

    def body(a_ref, an_ref, w_ref, g_ref, v_ref, cg_ref, cv_ref, cgn_ref, cvn_ref, cwg_ref, cwv_ref,
             dup_ref, dcwg_ref, dcwv_ref, dcbg_ref, dcbv_ref):
        i = pl.program_id(1)
        keep_next = jnp.where(i == ni - 1, 0.0, 1.0)

        @pl.when(i == 0)
        def _():
            for r in (dcwg_ref, dcwv_ref, dcbg_ref, dcbv_ref):
                r[...] = jnp.zeros_like(r)

        a_ext = jnp.concatenate([a_ref[...], an_ref[...]], axis=0)
        for s in range(tc // sc):
            ls = slice(s * sc, (s + 1) * sc)
            df_ext = lax.dot_general(a_ext, w_ref[s * sc:(s + 1) * sc, :], _DN_NT, preferred_element_type=F32)
            df = jnp.concatenate([df_ext[:tm], df_ext[tm:tm + 8] * keep_next], axis=0)
            cg = jnp.concatenate([cg_ref[:, ls].astype(F32), cgn_ref[:, ls].astype(F32)[:8]], axis=0)
            cv = jnp.concatenate([cv_ref[:, ls].astype(F32), cvn_ref[:, ls].astype(F32)[:8]], axis=0)
            gel, dgel = _gelu_parts(cg)

            def back(d, cw_ref, x_ref, dcw_ref, dcb_ref, slab):
                taps = _lane_taps(cw_ref, ls)
                d0 = d[:tm]
                d1 = pltpu.roll(d, tm + 8 - 1, 0)[:tm]
                d2 = pltpu.roll(d, tm + 8 - 2, 0)[:tm]
                dup_ref[slab, :, ls] = (taps[2] * d0 + taps[1] * d1 + taps[0] * d2).astype(BF16)
                xv = x_ref[:, ls].astype(F32)
                dcw_ref[2:3, ls] += _col_sum(xv * d0)
                dcw_ref[1:2, ls] += _col_sum(xv * d1)
                dcw_ref[0:1, ls] += _col_sum(xv * d2)
                dcb_ref[:, ls] += _col_sum(d0)

            back(df * cv * dgel, cwg_ref, g_ref, dcwg_ref, dcbg_ref, 0)
            back(df * gel, cwv_ref, v_ref, dcwv_ref, dcbv_ref, 1)

    nxt = lambda j, i: jnp.minimum((i + 1) * hb, S // 16 - 1)
    main = lambda s: pl.BlockSpec((None, tm, tc), lambda j, i: (s, i, j))
    halo = lambda s: pl.BlockSpec((None, 16, tc), lambda j, i: (s, nxt(j, i), j))
    acc3 = pl.BlockSpec((3, tc), lambda j, i: (0, j))
    acc1 = pl.BlockSpec((1, tc), lambda j, i: (0, j))
    return _pallas(
        body, 11, deps, name=name, grid=(nj, ni),
        in_specs=[pl.BlockSpec((tm, D), lambda j, i: (i, 0)), pl.BlockSpec((16, D), lambda j, i: (nxt(j, i), 0)),
                  pl.BlockSpec((tc, D), lambda j, i: (j, 0)),
                  main(0), main(1), main(0), main(1), halo(0), halo(1),
                  pl.BlockSpec((3, tc), lambda j, i: (0, j)), pl.BlockSpec((3, tc), lambda j, i: (0, j + nj))],
        out_specs=[pl.BlockSpec((2, tm, tc), lambda j, i: (0, i, j)), acc3, acc3, acc1, acc1],
        out_shape=[jax.ShapeDtypeStruct((2, S, Fd), BF16), jax.ShapeDtypeStruct((3, Fd), F32),
                   jax.ShapeDtypeStruct((3, Fd), F32), jax.ShapeDtypeStruct((1, Fd), F32),
                   jax.ShapeDtypeStruct((1, Fd), F32)],
        compiler_params=_cp(2, 56))(dxb, dxb, w, up3, up3, upc3, upc3, upc3, upc3, cw, cw, *deps)


def _gm_forward_tile(pv, vg, vb, ws_ref, bsb_ref, mbuf, H, nc):
    W = H * CHUNK
    z, dz = _gelu_parts(pv)
    u, v0 = z[:, :W], z[:, W:]
    xc = v0 - _rows_mean(v0)
    rs = lax.rsqrt(_rows_mean(xc * xc) + LN_EPS)
    vh = xc * rs
    vnb = (vh * vg + vb).astype(BF16)
    mask = lax.broadcasted_iota(jnp.int32, (CHUNK, CHUNK), 0) >= lax.broadcasted_iota(jnp.int32, (CHUNK, CHUNK), 1)
    for h in range(H):
        cs = slice(h * CHUNK, (h + 1) * CHUNK)
        wm = jnp.where(mask, ws_ref[h], 0.0).astype(BF16)
        vcat = jnp.concatenate([vnb[c * CHUNK:(c + 1) * CHUNK, cs] for c in range(nc)], axis=1)
        mix = jnp.dot(wm, vcat, preferred_element_type=F32)
        for c in range(nc):
            mbuf[c * CHUNK:(c + 1) * CHUNK, cs] = mix[:, c * CHUNK:(c + 1) * CHUNK] + bsb_ref[h]
    return dz, u, vh, rs, vnb, mask


def _gm_fwd_call(p, v_g, v_b, ws, bsb, out_g, name, deps=()):
    S = p.shape[0]
    H = ws.shape[0]
    W = H * CHUNK
    tm = _tile(S, 256, CHUNK)
    nc = tm // CHUNK

    def body(p_ref, vg_ref, vb_ref, ws_ref, bsb_ref, og_ref, y_ref, mbuf):
        _, u, _, _, _, _ = _gm_forward_tile(p_ref[...], vg_ref[...], vb_ref[...], ws_ref, bsb_ref, mbuf, H, nc)
        yg = u * mbuf[...]
        r = lax.rsqrt(_rows_mean(yg * yg) + RMS_EPS)
        y_ref[...] = (yg * r * og_ref[...]).astype(BF16)

    vec = pl.BlockSpec((1, W), lambda i: (0, 0))
    mat = pl.BlockSpec((H, CHUNK, CHUNK), lambda i: (0, 0, 0))
    return _pallas(
        body, 6, deps, name=name, grid=(S // tm,),
        in_specs=[pl.BlockSpec((tm, 2 * W), lambda i: (i, 0)), vec, vec, mat, mat, vec],
        out_specs=pl.BlockSpec((tm, W), lambda i: (i, 0)),
        out_shape=jax.ShapeDtypeStruct((S, 2 * W), BF16),
        scratch_shapes=[pltpu.VMEM((tm, W), F32)], compiler_params=_cp(1))(p, v_g, v_b, ws, bsb, out_g, *deps)


def _gm_bwd_call(p, d_y, v_g, v_b, ws, bsb, out_g, name, deps=()):
    S = p.shape[0]
    H = ws.shape[0]
    W = H * CHUNK
    tm = _tile(S, 256, CHUNK)
    nc = tm // CHUNK
    ni = S // tm

    def body(p_ref, dy_ref, vg_ref, vb_ref, ws_ref, bsb_ref, og_ref,
             dp_ref, dvg_ref, dvb_ref, dws_ref, dbs_ref, dog_ref, mbuf, dvbuf):
        i = pl.program_id(0)

        @pl.when(i == 0)
        def _():
            for r in (dvg_ref, dvb_ref, dws_ref, dbs_ref, dog_ref):
                r[...] = jnp.zeros_like(r)

        vg = vg_ref[...]
        dz, u, vh, rs, vnb, mask = _gm_forward_tile(p_ref[...], vg, vb_ref[...], ws_ref, bsb_ref, mbuf, H, nc)
        mixed = mbuf[...]
        yg = u * mixed
        r = lax.rsqrt(_rows_mean(yg * yg) + RMS_EPS)
        yn = yg * r
        dya = dy_ref[...]
        dog_ref[...] += _col_sum(dya * yn)
        dyg = dya * og_ref[...]
        dygm = r * (dyg - yn * _rows_mean(dyg * yn))
        du = dygm * mixed
        dmix = dygm * u
        dmb = dmix.astype(BF16)
        for h in range(H):
            cs = slice(h * CHUNK, (h + 1) * CHUNK)
            wm = jnp.where(mask, ws_ref[h], 0.0).astype(BF16)
            dcat = jnp.concatenate([dmb[c * CHUNK:(c + 1) * CHUNK, cs] for c in range(nc)], axis=1)
            vcat = jnp.concatenate([vnb[c * CHUNK:(c + 1) * CHUNK, cs] for c in range(nc)], axis=1)
            dvn = lax.dot_general(wm, dcat, _DN_TN, preferred_element_type=F32)
            dws_ref[h] += jnp.where(mask, lax.dot_general(dcat, vcat, _DN_NT, preferred_element_type=F32), 0.0)
            dbs = dmix[0:CHUNK, cs]
            for c in range(1, nc):
                dbs = dbs + dmix[c * CHUNK:(c + 1) * CHUNK, cs]
            dbs_ref[h] += dbs
            for c in range(nc):
                dvbuf[c * CHUNK:(c + 1) * CHUNK, cs] = dvn[:, c * CHUNK:(c + 1) * CHUNK]
        dvn_all = dvbuf[...]
        dvg_ref[...] += _col_sum(dvn_all * vh)
        dvb_ref[...] += _col_sum(dvn_all)
        dvh = dvn_all * vg
        dv0 = rs * (dvh - _rows_mean(dvh) - vh * _rows_mean(dvh * vh))
        dp_ref[...] = (jnp.concatenate([du, dv0], axis=1) * dz).astype(BF16)

        @pl.when(i == ni - 1)
        def _():
            for h in range(H):
                dbs_ref[h] = jnp.broadcast_to(jnp.sum(dbs_ref[h], axis=1, keepdims=True), (CHUNK, CHUNK))

    vec = pl.BlockSpec((1, W), lambda i: (0, 0))
    mat = pl.BlockSpec((H, CHUNK, CHUNK), lambda i: (0, 0, 0))
    vshape = jax.ShapeDtypeStruct((1, W), F32)
    mshape = jax.ShapeDtypeStruct((H, CHUNK, CHUNK), F32)
    return _pallas(
        body, 7, deps, name=name, grid=(ni,),
        in_specs=[pl.BlockSpec((tm, 2 * W), lambda i: (i, 0)), pl.BlockSpec((tm, W), lambda i: (i, 0)),
                  vec, vec, mat, mat, vec],
        out_specs=[pl.BlockSpec((tm, 2 * W), lambda i: (i, 0)), vec, vec, mat, mat, vec],
        out_shape=[jax.ShapeDtypeStruct((S, 4 * W), BF16), vshape, vshape, mshape, mshape, vshape],
        scratch_shapes=[pltpu.VMEM((tm, W), F32), pltpu.VMEM((tm, W), F32)],
        compiler_params=_cp(1))(p, d_y, v_g, v_b, ws, bsb, out_g, *deps)


def _lru_gates(prev8, xl, cw, cb, wa_ref, ba, wx_ref, bx, lam, H):
    sh = [_shift_down(prev8, xl, k) for k in range(4)]
    xr = cw[3] * sh[0] + cw[2] * sh[1] + cw[1] * sh[2] + cw[0] * sh[3] + cb
    xrb = xr.astype(BF16)
    rp, ip = [], []
    for h in range(H):
        cs = slice(h * CHUNK, (h + 1) * CHUNK)
        rp.append(jnp.dot(xrb[:, cs], wa_ref[h].astype(BF16), preferred_element_type=F32))
        ip.append(jnp.dot(xrb[:, cs], wx_ref[h].astype(BF16), preferred_element_type=F32))
    r = _sigmoid(jnp.concatenate(rp, axis=1) + ba)
    ig = _sigmoid(jnp.concatenate(ip, axis=1) + bx)
    sp = _softplus(-lam)
    t = jnp.tanh((-LRU_C) * r * sp)
    q = lax.rsqrt(1.0 - t)
    a = jnp.sqrt(1.0 + t) * q
    mult = jnp.sqrt(-2.0 * t) * q
    a2_over_mult = (1.0 + t) * q * lax.rsqrt(-2.0 * t)
    return xr, xrb, r, ig, sp, a, mult, a2_over_mult, sh


def _lru_fwd_call(p, cw, cb, wa, ba, wx, bx, lam, out_g, y_half, name):
    S = p.shape[0]
    H = wa.shape[0]
    W = H * CHUNK
    tm = _tile(S, 256, 16)
    ng = tm // 8

    def body(pg_ref, px_ref, cw_ref, cb_ref, wa_ref, ba_ref, wx_ref, bx_ref, lam_ref, og_ref, y_in_ref,
             y_ref, h_ref, xprev, hcar, abuf, bbuf):
        @pl.when(pl.program_id(0) == 0)
        def _():
            xprev[...] = jnp.zeros_like(xprev)
            hcar[...] = jnp.zeros_like(hcar)

        xl = px_ref[...]
        xr, _, _, ig, _, a, mult, _, _ = _lru_gates(xprev[...], xl, _taps(cw_ref), cb_ref[...], wa_ref, ba_ref[...],
                                                    wx_ref, bx_ref[...], lam_ref[...], H)
        xprev[...] = xl[tm - 8:]
        b = mult * (ig * xr)
        sub = lax.broadcasted_iota(jnp.int32, (tm, W), 0) & 7
        for d in (1, 2, 4):
            m = sub >= d
            a_s = jnp.where(m, pltpu.roll(a, d, 0), 1.0)
            b_s = jnp.where(m, pltpu.roll(b, d, 0), 0.0)
            b = a * b_s + b
            a = a * a_s
        abuf[...] = a
        bbuf[...] = b

        def step(g, carry):
            r0 = pl.multiple_of(g * 8, 8)
            h_ref[pl.ds(r0, 8), :] = abuf[pl.ds(r0, 8), :] * carry + bbuf[pl.ds(r0, 8), :]
            return jnp.broadcast_to(h_ref[pl.ds(r0 + 7, 1), :], (8, W))

        hcar[...] = lax.fori_loop(0, ng, step, hcar[...])
        yl = h_ref[...] * _gelu(pg_ref[...])
        r = lax.rsqrt(_rows_mean(yl * yl) + RMS_EPS)
        y_ref[...] = (yl * r * og_ref[...]).astype(BF16)

    vec = pl.BlockSpec((1, W), lambda i: (0, 0))
    mat = pl.BlockSpec((H, CHUNK, CHUNK), lambda i: (0, 0, 0))
    return pl.pallas_call(
        body, name=name, grid=(S // tm,),
        in_specs=[pl.BlockSpec((tm, W), lambda i: (i, 2)), pl.BlockSpec((tm, W), lambda i: (i, 3)),
                  pl.BlockSpec((4, W), lambda i: (0, 0)), vec, mat, vec, mat, vec, vec, vec,
                  pl.BlockSpec(memory_space=pl.ANY)],
        out_specs=[pl.BlockSpec((tm, W), lambda i: (i, 1)), pl.BlockSpec((tm, W), lambda i: (i, 0))],
        out_shape=[jax.ShapeDtypeStruct((S, 2 * W), BF16), jax.ShapeDtypeStruct((S, W), F32)],
        input_output_aliases={10: 0},
        scratch_shapes=[pltpu.VMEM((8, W), F32), pltpu.VMEM((8, W), F32), pltpu.VMEM((tm, W), F32),
                        pltpu.VMEM((tm, W), F32)],
        compiler_params=_cp(1))(p, p, cw, cb, wa, ba, wx, bx, lam, out_g, y_half)


def _lru_bwd_call(p, hs, d_y, cw, cb, wa, ba, wx, bx, lam, out_g, dp_half, name):
    S = p.shape[0]
    H = wa.shape[0]
    W = H * CHUNK
    tm = _tile(S, 256, 16)
    ng = tm // 8
    ni = S // tm
    hb = tm // 8

    def body(pg_ref, px_ref, pxp_ref, h_ref, hp_ref, dy_ref, cw_ref, cb_ref, wa_ref, ba_ref, wx_ref, bx_ref,
             lam_ref, og_ref, dp_in_ref,
             dp_ref, dcw_ref, dcb_ref, dwa_ref, dba_ref, dwx_ref, dbx_ref, dlam_ref, dog_ref,
             a_next, e_next, dxr_next, abuf, bbuf, ebuf):
        i = pl.program_id(0)
        ri = ni - 1 - i

        @pl.when(i == 0)
        def _():
            for r in (dcw_ref, dcb_ref, dwa_ref, dba_ref, dwx_ref, dbx_ref, dlam_ref, dog_ref,
                      a_next, e_next, dxr_next):
                r[...] = jnp.zeros_like(r)

        keep_prev = jnp.where(ri == 0, 0.0, 1.0)
        cw_ = _taps(cw_ref)
        lam_ = lam_ref[...]
        xl = px_ref[...]
        xr, xrb, r, ig, sp, a, mult, a2m, sh = _lru_gates(pxp_ref[...] * keep_prev, xl, cw_, cb_ref[...], wa_ref,
                                                          ba_ref[...], wx_ref, bx_ref[...], lam_, H)
        gg, dgg = _gelu_parts(pg_ref[...])
        hv = h_ref[...]
        yl = hv * gg
        rr = lax.rsqrt(_rows_mean(yl * yl) + RMS_EPS)
        yn = yl * rr
        dyb = dy_ref[...]
        dog_ref[...] += _col_sum(dyb * yn)
        dyg = dyb * og_ref[...]
        dyl = rr * (dyg - yn * _rows_mean(dyg * yn))
        dh = dyl * gg
        dgl = dyl * hv * dgg

        an = _shift_up(a, a_next[...], 1)
        eb = dh
        sub = lax.broadcasted_iota(jnp.int32, (tm, W), 0) & 7
        for d in (1, 2, 4):
            m = sub < 8 - d
            a_s = jnp.where(m, pltpu.roll(an, tm - d, 0), 1.0)
            e_s = jnp.where(m, pltpu.roll(eb, tm - d, 0), 0.0)
            eb = an * e_s + eb
            an = an * a_s
        abuf[...] = an
        bbuf[...] = eb

        def step(g, carry):
            r0 = pl.multiple_of((ng - 1 - g) * 8, 8)
            ebuf[pl.ds(r0, 8), :] = abuf[pl.ds(r0, 8), :] * carry + bbuf[pl.ds(r0, 8), :]
            return jnp.broadcast_to(ebuf[pl.ds(r0, 1), :], (8, W))

        lax.fori_loop(0, ng, step, jnp.broadcast_to(e_next[0:1, :], (8, W)))
        e = ebuf[...]
        a_next[...] = a[0:8]
        e_next[...] = e[0:8]

        hm1 = _shift_down(hp_ref[...] * keep_prev, hv, 1)
        da = e * hm1
        dmult = e * ig * xr
        di = e * mult * xr
        dxr = e * mult * ig
        dla = da * a - dmult * a2m
        dr = dla * ((-LRU_C) * sp)
        dlam_ref[...] += _col_sum(dla * ((-LRU_C) * r))
        dpr = dr * r * (1.0 - r)
        dpi = di * ig * (1.0 - ig)
        dba_ref[...] += _col_sum(dpr)
        dbx_ref[...] += _col_sum(dpi)
        dprb = dpr.astype(BF16)
        dpib = dpi.astype(BF16)
        back = []
        for h in range(H):
            cs = slice(h * CHUNK, (h + 1) * CHUNK)
            wab = wa_ref[h].astype(BF16)
            wxb = wx_ref[h].astype(BF16)
            back.append(lax.dot_general(dprb[:, cs], wab, _DN_NT, preferred_element_type=F32)
                        + lax.dot_general(dpib[:, cs], wxb, _DN_NT, preferred_element_type=F32))
            dwa_ref[h] += lax.dot_general(xrb[:, cs], dprb[:, cs], _DN_TN, preferred_element_type=F32)
            dwx_ref[h] += lax.dot_general(xrb[:, cs], dpib[:, cs], _DN_TN, preferred_element_type=F32)
        dxr = dxr + jnp.concatenate(back, axis=1)

        nxt = dxr_next[...]
        dxl = (cw_[3] * dxr + cw_[2] * _shift_up(dxr, nxt, 1) + cw_[1] * _shift_up(dxr, nxt, 2)
               + cw_[0] * _shift_up(dxr, nxt, 3))
        dxr_next[...] = dxr[0:8]
        for k in range(4):
            dcw_ref[k:k + 1, :] += _col_sum(sh[3 - k] * dxr)
        dcb_ref[...] += _col_sum(dxr)
        dp_ref[...] = jnp.concatenate([dgl, dxl], axis=1).astype(BF16)

        @pl.when(i == ni - 1)
        def _():
            dlam_ref[...] = -dlam_ref[...] * _sigmoid(-lam_)

    vec = pl.BlockSpec((1, W), lambda i: (0, 0))
    mat = pl.BlockSpec((H, CHUNK, CHUNK), lambda i: (0, 0, 0))
    rev = lambda i: ni - 1 - i
    prev = lambda i: jnp.maximum(rev(i) * hb - 1, 0)
    vshape = jax.ShapeDtypeStruct((1, W), F32)
    mshape = jax.ShapeDtypeStruct((H, CHUNK, CHUNK), F32)
    tile = lambda: pltpu.VMEM((tm, W), F32)
    car = lambda: pltpu.VMEM((8, W), F32)
    return pl.pallas_call(
        body, name=name, grid=(ni,),
        in_specs=[pl.BlockSpec((tm, W), lambda i: (rev(i), 2)), pl.BlockSpec((tm, W), lambda i: (rev(i), 3)),
                  pl.BlockSpec((8, W), lambda i: (prev(i), 3)),
                  pl.BlockSpec((tm, W), lambda i: (rev(i), 0)), pl.BlockSpec((8, W), lambda i: (prev(i), 0)),
                  pl.BlockSpec((tm, W), lambda i: (rev(i), 1)),
                  pl.BlockSpec((4, W), lambda i: (0, 0)), vec, mat, vec, mat, vec, vec, vec,
                  pl.BlockSpec(memory_space=pl.ANY)],
        out_specs=[pl.BlockSpec((tm, 2 * W), lambda i: (rev(i), 1)), pl.BlockSpec((4, W), lambda i: (0, 0)), vec,
                   mat, vec, mat, vec, vec, vec],
        out_shape=[jax.ShapeDtypeStruct((S, 4 * W), BF16), jax.ShapeDtypeStruct((4, W), F32), vshape,
                   mshape, vshape, mshape, vshape, vshape, vshape],
        input_output_aliases={14: 0},
        scratch_shapes=[car(), car(), car(), tile(), tile(), tile()],
        compiler_params=_cp(1, 56))(p, p, p, hs, hs, d_y, cw, cb, wa, ba, wx, bx, lam, out_g, dp_half)


def _rows128(a):
    return a.reshape(-1, LANES).astype(F32)


def _pack(arrays, pad_to=256):
    flat = jnp.concatenate([_rows128(a) for a in arrays], axis=0)
    pad = (-flat.shape[0]) % pad_to
    if pad:
        flat = jnp.concatenate([flat, jnp.zeros((pad, LANES), F32)], axis=0)
    return flat


def _unpack(flat, shapes):
    out, r = [], 0
    for s in shapes:
        n = 1
        for d in s:
            n *= d
        out.append(flat[r:r + n // LANES].reshape(s))
        r += n // LANES
    return out


def kernel(x, norm1_g, w_in, gm_v_g, gm_v_b, gm_ws, gm_bs, lru_conv_w, lru_conv_b, lru_wa, lru_ba, lru_wx, lru_bx, lru_lambda, gm_out_g, lru_out_g, w_out, norm2_g, ffn_w_up, ffn_conv_w, ffn_conv_b, ffn_w_down, final_g, loss_target, m_norm1_g, m_w_in, m_gm_v_g, m_gm_v_b, m_gm_ws, m_gm_bs, m_lru_conv_w, m_lru_conv_b, m_lru_wa, m_lru_ba, m_lru_wx, m_lru_bx, m_lru_lambda, m_gm_out_g, m_lru_out_g, m_w_out, m_norm2_g, m_ffn_w_up, m_ffn_conv_w, m_ffn_conv_b, m_ffn_w_down, m_final_g, v_norm1_g, v_w_in, v_gm_v_g, v_gm_v_b, v_gm_ws, v_gm_bs, v_lru_conv_w, v_lru_conv_b, v_lru_wa, v_lru_ba, v_lru_wx, v_lru_bx, v_lru_lambda, v_gm_out_g, v_lru_out_g, v_w_out, v_norm2_g, v_ffn_w_up, v_ffn_conv_w, v_ffn_conv_b, v_ffn_w_down, v_final_g):
    wts = dict(norm1_g=norm1_g, w_in=w_in, gm_v_g=gm_v_g, gm_v_b=gm_v_b, gm_ws=gm_ws, gm_bs=gm_bs,
               lru_conv_w=lru_conv_w, lru_conv_b=lru_conv_b, lru_wa=lru_wa, lru_ba=lru_ba, lru_wx=lru_wx,
               lru_bx=lru_bx, lru_lambda=lru_lambda, gm_out_g=gm_out_g, lru_out_g=lru_out_g, w_out=w_out,
               norm2_g=norm2_g, ffn_w_up=ffn_w_up, ffn_conv_w=ffn_conv_w, ffn_conv_b=ffn_conv_b,
               ffn_w_down=ffn_w_down, final_g=final_g)
    mom = dict(norm1_g=m_norm1_g, w_in=m_w_in, gm_v_g=m_gm_v_g, gm_v_b=m_gm_v_b, gm_ws=m_gm_ws, gm_bs=m_gm_bs,
               lru_conv_w=m_lru_conv_w, lru_conv_b=m_lru_conv_b, lru_wa=m_lru_wa, lru_ba=m_lru_ba, lru_wx=m_lru_wx,
               lru_bx=m_lru_bx, lru_lambda=m_lru_lambda, gm_out_g=m_gm_out_g, lru_out_g=m_lru_out_g, w_out=m_w_out,
               norm2_g=m_norm2_g, ffn_w_up=m_ffn_w_up, ffn_conv_w=m_ffn_conv_w, ffn_conv_b=m_ffn_conv_b,
               ffn_w_down=m_ffn_w_down, final_g=m_final_g)
    var = dict(norm1_g=v_norm1_g, w_in=v_w_in, gm_v_g=v_gm_v_g, gm_v_b=v_gm_v_b, gm_ws=v_gm_ws, gm_bs=v_gm_bs,
               lru_conv_w=v_lru_conv_w, lru_conv_b=v_lru_conv_b, lru_wa=v_lru_wa, lru_ba=v_lru_ba, lru_wx=v_lru_wx,
               lru_bx=v_lru_bx, lru_lambda=v_lru_lambda, gm_out_g=v_gm_out_g, lru_out_g=v_lru_out_g, w_out=v_w_out,
               norm2_g=v_norm2_g, ffn_w_up=v_ffn_w_up, ffn_conv_w=v_ffn_conv_w, ffn_conv_b=v_ffn_conv_b,
               ffn_w_down=v_ffn_w_down, final_g=v_final_g)

    xi, yi, ci = lax.axis_index("x"), lax.axis_index("y"), lax.axis_index("c")
    chip = 2 * xi + yi
    dev = 2 * chip + ci
    core_chip = jnp.stack([ci, chip]).astype(jnp.int32)

    xs = x[0]
    tgt = loss_target[0]
    S, D = xs.shape
    H = gm_ws.shape[1]
    W = H * CHUNK
    Fd = ffn_w_down.shape[1] * N_DEV
    lcw_cols = lru_conv_w.shape[2]
    fcw_cols = ffn_conv_w.shape[2]

    dev1 = jnp.reshape(dev, (1,)).astype(jnp.int32)

    def place_own(shards, name):
        return [_place_own_call(s, dev1, dt, "%s_own%d" % (name, k)) for k, (s, dt) in enumerate(shards)]

    def gather_start(shards, name, after=()):
        lands = place_own(shards, name)
        return _exchange_start([], lands, 4 * len(lands), _gather_stage1_copies(len(lands)), name + "_ici", after)

    def gather_forward(lands, name, after=()):
        return _exchange_start([], lands, 3 * len(lands), _gather_stage2_copies(len(lands)), name + "_d2d", after)

    def pair_start(g, name, after=()):
        return _exchange_start([g], [lax.empty((4,) + g.shape[1:], F32)], 4, _pair_copies(1), name, after)

    def chip_start(p16, name, after=()):
        return _exchange_start([p16], [lax.empty((3,) + p16.shape[1:], BF16)], 3, _chip_copies(1), name, after)

    vgm_g, vgm_b = gm_v_g, gm_v_b
    ws, wa, wx = gm_ws[0], lru_wa[0], lru_wx[0]
    bsb = jnp.broadcast_to(gm_bs[0][:, :, None], (H, CHUNK, CHUNK))
    ba, bx = lru_ba.reshape(1, W), lru_bx.reshape(1, W)
    fcb = ffn_conv_b
    fing = final_g.reshape(1, D)

    conv_pack = _pack([lru_conv_w[0], ffn_conv_w[0]], pad_to=8)
    lands = place_own([(w_in[0], BF16), (conv_pack, F32)], "gather_in")
    ga_pair = _exchange_start([], lands, 2, _gather_stage1_copies(2, to_chips=False), "gather_in_pair")
    ga1 = _exchange_start([], ga_pair.bufs, 6, _gather_stage1_copies(2, to_sibling=False), "gather_in_ici")
    h1 = _rmsnorm_call(xs, norm1_g, "norm1", deps=(ga1.token,))
    ga_pair.bufs = ga1.bufs
    _, la = _exchange_wait(ga_pair, after=(h1,))
    own_blocks = jnp.stack([dev, dev + 1 - 2 * ci]).astype(jnp.int32)
    other_blocks = ((2 * chip + 2 + jnp.arange(N_DEV - 2)) % N_DEV).astype(jnp.int32)
    p_own = _mm_some_blocks_call(h1, la[0], own_blocks, F32, "in_proj_own")
    ga1.bufs = la
    _, la = _exchange_wait(ga1, after=(p_own,))
    ga2 = gather_forward(la, "gather_in")
    gb1 = gather_start([(w_out[0], BF16)], "gather_out", after=(ga2.token,))
    gc1 = gather_start([(ffn_w_up[0], BF16)], "gather_up", after=(gb1.token,))
    _, (win_g, conv_g) = _exchange_wait(ga2, after=(gc1.token,))
    n_l = 4 * lcw_cols // LANES
    n_f = 3 * fcw_cols // LANES
    lcw = conv_g[:, :n_l].reshape(N_DEV, 4, lcw_cols).transpose(1, 0, 2).reshape(4, N_DEV * lcw_cols)
    fcw = conv_g[:, n_l:n_l + n_f].reshape(N_DEV, 3, fcw_cols).transpose(1, 0, 2).reshape(3, N_DEV * fcw_cols)

    p = _mm_some_blocks_call(h1, win_g, other_blocks, F32, "in_proj_rest", out_so_far=p_own)
    win_rows = _unblock_call(win_g, "w_in_rows")
    _, lb = _exchange_wait(gb1, after=(p,))
    gb2 = gather_forward(lb, "gather_out")
    y_half = _gm_fwd_call(p, vgm_g, vgm_b, ws, bsb, gm_out_g, "gmlp_fwd", deps=(gb2.token,))
    y, hs = _lru_fwd_call(p, lcw, lru_conv_b, wa, ba, wx, bx, lru_lambda, lru_out_g, y_half, "lru_fwd")
    _, (wout_g,) = _exchange_wait(gb2, after=(y,))
    wout_full = wout_g.reshape(D, D)
    x2 = _mm_out_call(xs, y, wout_full, "out_proj")
    h2 = _rmsnorm_call(x2, norm2_g, "norm2")
    _, lc = _exchange_wait(gc1, after=(h2,))
    gc2 = gather_forward(lc, "gather_up")
    gd1 = gather_start([(ffn_w_down[0], BF16)], "gather_down", after=(gc2.token,))
    _, (wup_g,) = _exchange_wait(gc2, after=(gd1.token,))
    up3, upc3, f = _ffn_up_act_call(h2, wup_g, fcw, fcb, "ffn_up")
    _, ld = _exchange_wait(gd1, after=(f,))
    gd2 = gather_forward(ld, "gather_down")
    _, (wdown_g,) = _exchange_wait(gd2)
    wdown_full = wdown_g.reshape(Fd, D)
    dx3, dx3b, loss_acc, d_final = _mm_down_loss_call(x2, f, wdown_full, fing, tgt, "ffn_down_loss")

    g_wdown = _mm_tn_rows_call(f, dx3b, "ffn_down_dw").reshape((N_DEV,) + ffn_w_down.shape[1:])
    pd = pair_start(g_wdown, "pair_down")
    d_up3, dfcw_g, dfcw_v, dfcb_g, dfcb_v = _ffn_down_dx_act_bwd_call(dx3b, wdown_full, up3, upc3, fcw, "ffn_down_dx",
                                                                     deps=(pd.token,))
    (g_wdown,), (r1,) = _exchange_wait(pd, after=(d_up3,))
    own_down, p16 = _pair_add_call(g_wdown, r1, core_chip, "pair_add_down")
    cd = chip_start(p16, "chip_down")
    g_wup = _mm_tn_cols_call(h2, d_up3, N_DEV, ffn_w_up.shape[2], "ffn_up_dw", deps=(cd.token,))
    pu = pair_start(g_wup, "pair_up")
    dx2, dx2b, d_norm2 = _mm_dx_norm_call(d_up3, wup_g, dx3, x2, norm2_g, "ffn_up_dx", deps=(pu.token,))
    g_wout = _mm_tn_rows_call(y, dx2b, "out_proj_dw").reshape((N_DEV,) + w_out.shape[1:])
    po = pair_start(g_wout, "pair_out")
    d_y = _mm_nt_call(dx2b, wout_full, F32, "out_proj_dx", deps=(po.token,))
    (g_wup,), (r1,) = _exchange_wait(pu, after=(d_y,))
    own_up, p16 = _pair_add_call(g_wup, r1, core_chip, "pair_add_up")
    _, (r2_down,) = _exchange_wait(cd, after=(p16,))
    cu = chip_start(p16, "chip_up", after=(r2_down,))
    dp_half, d_vg, d_vb, d_ws, d_bs, d_gog = _gm_bwd_call(p, d_y, vgm_g, vgm_b, ws, bsb, gm_out_g, "gmlp_bwd",
                                                          deps=(cu.token,))
    d_p2, d_lcw, d_lcb, d_wa, d_ba, d_wx, d_bx, d_lam, d_log = _lru_bwd_call(
        p, hs, d_y, lcw, lru_conv_b, wa, ba, wx, bx, lru_lambda, lru_out_g, dp_half, "lru_bwd")
    d_p = d_p2[None]
    (g_wout,), (r1,) = _exchange_wait(po, after=(d_p,))
    own_out, p16_out = _pair_add_call(g_wout, r1, core_chip, "pair_add_out")
    g_win = _mm_tn_cols_call(h1, d_p, N_DEV, w_in.shape[2], "in_proj_dw")
    pi = pair_start(g_win, "pair_in")
    _, (r2_up,) = _exchange_wait(cu, after=(g_win,))
    co = chip_start(p16_out, "chip_out", after=(r2_up,))
    gx_a, dn_a = _mm_nt_norm_call(d_p[0], win_rows, dx2, xs, norm1_g, "in_proj_dx_a", deps=(co.token, pi.token),
                                  part=(0, 2))
    (g_win,), (r1,) = _exchange_wait(pi, after=(gx_a,))
    own_in, p16 = _pair_add_call(g_win, r1, core_chip, "pair_add_in")
    _, (r2_out,) = _exchange_wait(co, after=(p16,))
    ci_ = chip_start(p16, "chip_in", after=(r2_out,))
    grad_x, dn_b = _mm_nt_norm_call(d_p[0], win_rows, dx2, xs, norm1_g, "in_proj_dx_b", deps=(ci_.token,),
                                    part=(1, 2), dx_so_far=gx_a)

    small_g = dict(norm1_g=dn_a + dn_b, gm_v_g=d_vg, gm_v_b=d_vb, gm_ws=d_ws, gm_bs=d_bs[:, :, 0], lru_conv_b=d_lcb,
                   lru_wa=d_wa, lru_ba=d_ba, lru_wx=d_wx, lru_bx=d_bx, lru_lambda=d_lam, gm_out_g=d_gog,
                   lru_out_g=d_log, norm2_g=d_norm2,
                   ffn_conv_b=jnp.concatenate([dfcb_g, dfcb_v], axis=1), final_g=d_final)
    rep = _pack([small_g[n] for n in SMALL])
    conv_part = _pack([d_lcw, jnp.concatenate([dfcw_g, dfcw_v], axis=1)], pad_to=8)
    n_rep, n_conv = rep.shape[0], conv_part.shape[0]
    gs1 = gather_start([(jnp.concatenate([rep, conv_part], axis=0), F32)], "gather_small")

    def adamw_big(n, own, r2, deps=()):
        return _adamw_call(wts[n][0], mom[n][0], var[n][0], [(own, None), (r2, 0), (r2, 1), (r2, 2)], "adamw_" + n, deps)

    res = {}
    res["ffn_w_down"] = adamw_big("ffn_w_down", own_down, r2_down, (gs1.token,))
    res["ffn_w_up"] = adamw_big("ffn_w_up", own_up, r2_up, (gs1.token,))
    res["w_out"] = adamw_big("w_out", own_out, r2_out, (gs1.token,))
    _, ls = _exchange_wait(gs1, after=(res["w_out"][0], res["ffn_w_up"][0], res["ffn_w_down"][0]))
    gs2 = gather_forward(ls, "gather_small")
    _, (r2_in,) = _exchange_wait(ci_, after=(gs2.token,))
    res["w_in"] = adamw_big("w_in", own_in, r2_in)
    _, (parts,) = _exchange_wait(gs2, after=(res["w_in"][0],))
    g_rep, d_rep, m_rep, v_rep = _adamw_call(
        _pack([wts[n] for n in SMALL]), _pack([mom[n] for n in SMALL]), _pack([var[n] for n in SMALL]),
        [(parts, k) for k in range(N_DEV)], "adamw_small")
    shapes = [wts[n].shape for n in SMALL]
    for n, g_, d_, m_, v_ in zip(SMALL, _unpack(g_rep, shapes), _unpack(d_rep, shapes), _unpack(m_rep, shapes),
                                 _unpack(v_rep, shapes)):
        res[n] = (g_, d_, m_, v_)
    conv_sum = _sum_call(parts, n_rep, n_conv, "sum_conv_grads")
    g_lcw = conv_sum[:4 * W // LANES].reshape(4, W)
    g_fcw = conv_sum[4 * W // LANES:4 * W // LANES + 6 * Fd // LANES].reshape(3, 2 * Fd)
    for n, full in (("lru_conv_w", g_lcw), ("ffn_conv_w", g_fcw)):
        cols = wts[n].shape[2]
        mine = lax.dynamic_slice_in_dim(full, dev * cols, cols, axis=1)
        res[n] = _adamw_call(wts[n][0], mom[n][0], var[n][0], [(mine, None)], "adamw_" + n)

    loss = lax.psum(loss_acc[0, 0], ("x", "y", "c"))
    outs = [[], [], [], []]
    for n in WEIGHTS:
        for k in range(4):
            outs[k].append(res[n][k].reshape(wts[n].shape))
    return (loss, grad_x[None], *outs[0], *outs[1], *outs[2], *outs[3])
```

```python
import functools
import math

import jax
import jax.numpy as jnp
from jax import lax
from jax.experimental import pallas as pl
from jax.experimental.pallas import tpu as pltpu

F32 = jnp.float32
BF16 = jnp.bfloat16

RMS_EPS = 1e-6
LN_EPS = 1e-5
LRU_C = 8.0
CHUNK = 128
ADAM_LR = 0.001
ADAM_B1 = 0.9
ADAM_B2 = 0.999
ADAM_EPS = 1e-08
ADAM_WD = 0.01
ADAM_STEP = 10
N_DEV = 8
LANES = 128
MIB = 1024 * 1024

WEIGHTS = ['norm1_g', 'w_in', 'gm_v_g', 'gm_v_b', 'gm_ws', 'gm_bs', 'lru_conv_w', 'lru_conv_b', 'lru_wa', 'lru_ba',
           'lru_wx', 'lru_bx', 'lru_lambda', 'gm_out_g', 'lru_out_g', 'w_out', 'norm2_g', 'ffn_w_up', 'ffn_conv_w',
           'ffn_conv_b', 'ffn_w_down', 'final_g']
BIG = ['w_in', 'w_out', 'ffn_w_up', 'ffn_w_down']
CONV = ['lru_conv_w', 'ffn_conv_w']
SMALL = [n for n in WEIGHTS if n not in BIG and n not in CONV]

_DN_NT = (((1,), (1,)), ((), ()))
_DN_TN = (((0,), (0,)), ((), ()))
_GELU_C = 0.7978845608028654


def _cp(n_axes, vmem_mib=48):
    return pltpu.CompilerParams(dimension_semantics=("arbitrary",) * n_axes, vmem_limit_bytes=vmem_mib * MIB)


def _tile(n, pref, mult=8):
    t = min(pref, n)
    t -= t % mult
    while t >= mult:
        if n % t == 0:
            return t
        t -= mult
    return n


def _gelu_gate(z, z2):
    return 0.5 * jnp.tanh(z * ((_GELU_C * 0.044715) * z2 + _GELU_C)) + 0.5


def _gelu(z):
    return z * _gelu_gate(z, z * z)


def _gelu_parts(z):
    z2 = z * z
    s = _gelu_gate(z, z2)
    g = z * s
    dg = s + g * (1.0 - s) * ((6.0 * _GELU_C * 0.044715) * z2 + 2.0 * _GELU_C)
    return g, dg


def _sigmoid(z):
    return 0.5 + 0.5 * jnp.tanh(0.5 * z)


def _softplus(z):
    t = jnp.exp(-jnp.abs(z))
    u = 1.0 + t
    log1p = jnp.where(u == 1.0, t, jnp.log(u) * t / (u - 1.0))
    return jnp.maximum(z, 0.0) + log1p


def _rows_mean(v):
    return jnp.mean(v, axis=-1, keepdims=True)


def _col_sum(v):
    return jnp.sum(v, axis=0, keepdims=True)


def _shift_down(prev8, cur, k):
    if k == 0:
        return cur
    z = jnp.concatenate([prev8, cur], axis=0)
    return pltpu.roll(z, k, 0)[8:]


def _shift_up(cur, next8, k):
    if k == 0:
        return cur
    n = cur.shape[0]
    z = jnp.concatenate([cur, next8], axis=0)
    return pltpu.roll(z, n + 8 - k, 0)[:n]


def _mesh_pos():
    return lax.axis_index("x"), lax.axis_index("y"), lax.axis_index("c")


def _any_specs(n):
    return [pl.BlockSpec(memory_space=pl.ANY)] * n


def _pallas(body, n_in, deps, **kw):
    nd = len(deps)
    if not nd:
        return pl.pallas_call(body, **kw)

    def ordered(*refs):
        body(*refs[:n_in], *refs[n_in + nd:])

    kw["in_specs"] = list(kw["in_specs"]) + _any_specs(nd)
    return pl.pallas_call(ordered, **kw)


_HBM = pl.BlockSpec(memory_space=pltpu.HBM)
_SEM = pl.BlockSpec(memory_space=pltpu.SEMAPHORE)
_EFFECT = pltpu.SideEffectType.DATAFLOW_SIDE_EFFECTING


class _InFlight:
    def __init__(self, sems, bufs, token, n_src, n_copies, make_copies, name):
        self.sems, self.bufs, self.token = sems, bufs, token
        self.n_src, self.n_copies, self.make_copies, self.name = n_src, n_copies, make_copies, name


def _exchange_start(srcs, lands, n_copies, make_copies, name, after=()):
    bufs = list(srcs) + list(lands)
    nb, na = len(bufs), len(after)
    ns = len(srcs)

    def body(*refs):
        b_refs = refs[:nb]
        outs = refs[nb + na:]
        send, recv = outs[:n_copies], outs[n_copies:2 * n_copies]
        token = outs[-1]
        for cp in make_copies(b_refs[:ns], b_refs[ns:], send, recv):
            cp.start()
        token[...] = jnp.zeros_like(token)

    out = pl.pallas_call(
        body, name=name,
        out_shape=[pltpu.SemaphoreType.DMA(())] * (2 * n_copies) + [pltpu.HBM(b.shape, b.dtype) for b in bufs]
        + [jax.ShapeDtypeStruct((8, LANES), F32)],
        in_specs=[_HBM] * nb + _any_specs(na),
        out_specs=[_SEM] * (2 * n_copies) + [_HBM] * nb + [pl.BlockSpec(memory_space=pltpu.VMEM)],
        input_output_aliases={i: 2 * n_copies + i for i in range(nb)},
        compiler_params=pltpu.CompilerParams(has_side_effects=_EFFECT),
    )(*[pltpu.with_memory_space_constraint(b, pltpu.HBM) for b in bufs], *after)
    return _InFlight(out[:2 * n_copies], out[2 * n_copies:2 * n_copies + nb], out[-1], ns, n_copies, make_copies, name)


def _exchange_wait(fl, after=()):
    nb, na, nc, ns = len(fl.bufs), len(after), fl.n_copies, fl.n_src

    def body(*refs):
        b_refs = refs[:nb]
        sems = refs[nb:nb + 2 * nc]
        copies = fl.make_copies(b_refs[:ns], b_refs[ns:], sems[:nc], sems[nc:])
        for cp in copies:
            cp.wait_send()
        for cp in copies:
            cp.wait_recv()

    out = pl.pallas_call(
        body, name=fl.name + "_wait",
        out_shape=[pltpu.HBM(b.shape, b.dtype) for b in fl.bufs],
        in_specs=[_HBM] * nb + [_SEM] * (2 * nc) + _any_specs(na),
        out_specs=[_HBM] * nb,
        input_output_aliases={i: i for i in range(nb)},
        compiler_params=pltpu.CompilerParams(has_side_effects=_EFFECT),
    )(*fl.bufs, *fl.sems, *after)
    return list(out[:ns]), list(out[ns:])


def _remote(src, dst, send_sem, recv_sem, to):
    return pltpu.make_async_remote_copy(src_ref=src, dst_ref=dst, send_sem=send_sem, recv_sem=recv_sem,
                                        device_id=to, device_id_type=pl.DeviceIdType.MESH)


def _gather_stage1_copies(n, to_sibling=True, to_chips=True):
    def make(s_refs, l_refs, send, recv):
        x, y, c = _mesh_pos()
        own = 4 * x + 2 * y + c
        targets = ([(x, y, 1 - c)] if to_sibling else []) + (
            [(1 - x, y, c), (x, 1 - y, c), (1 - x, 1 - y, c)] if to_chips else [])
        m = len(targets)
        return [_remote(l_refs[a].at[own], l_refs[a].at[own], send[m * a + k], recv[m * a + k], to)
                for a in range(n) for k, to in enumerate(targets)]
    return make


def _gather_stage2_copies(n):
    def make(s_refs, l_refs, send, recv):
        x, y, c = _mesh_pos()
        blocks = [4 * (1 - x) + 2 * y + c, 4 * x + 2 * (1 - y) + c, 4 * (1 - x) + 2 * (1 - y) + c]
        return [_remote(l_refs[a].at[b], l_refs[a].at[b], send[3 * a + j], recv[3 * a + j], (x, y, 1 - c))
                for a in range(n) for j, b in enumerate(blocks)]
    return make


def _pair_copies(n):
    def make(s_refs, l_refs, send, recv):
        x, y, c = _mesh_pos()
        return [_remote(s_refs[a].at[2 * k + 1 - c], l_refs[a].at[k], send[4 * a + k], recv[4 * a + k], (x, y, 1 - c))
                for a in range(n) for k in range(4)]
    return make


def _chip_copies(n):
    def make(s_refs, l_refs, send, recv):
        x, y, c = _mesh_pos()
        chips = [(1 - x, y), (x, 1 - y), (1 - x, 1 - y)]
        return [_remote(s_refs[a].at[2 * ch[0] + ch[1]], l_refs[a].at[j], send[3 * a + j], recv[3 * a + j], (*ch, c))
                for a in range(n) for j, ch in enumerate(chips)]
    return make


def _place_own_call(shard, dev, dtype, name):
    R, C = shard.shape
    tr = _tile(R, max(16, MIB // (C * 4)), 16)

    def body(d_ref, s_ref, o_ref):
        o_ref[...] = s_ref[...].astype(dtype)

    grid_spec = pltpu.PrefetchScalarGridSpec(
        num_scalar_prefetch=1, grid=(R // tr,),
        in_specs=[pl.BlockSpec((tr, C), lambda r, d: (r, 0))],
        out_specs=pl.BlockSpec((None, tr, C), lambda r, d: (d[0], r, 0)))
    return pl.pallas_call(body, name=name, grid_spec=grid_spec,
                          out_shape=jax.ShapeDtypeStruct((N_DEV, R, C), dtype), compiler_params=_cp(1))(dev, shard)


def _pair_add_call(g, r1, core_chip, name):
    _, R, C = g.shape
    tr = _tile(R, max(16, (2 * MIB) // (C * 4)), 16)

    def body(cc_ref, g_ref, r_ref, p32_ref, p16_ref):
        s = g_ref[...] + r_ref[...]
        p16_ref[...] = s.astype(BF16)

        @pl.when(pl.program_id(1) == cc_ref[1])
        def _():
            p32_ref[...] = s

    grid_spec = pltpu.PrefetchScalarGridSpec(
        num_scalar_prefetch=1, grid=(R // tr, 4),
        in_specs=[pl.BlockSpec((None, tr, C), lambda r, k, cc: (2 * k + cc[0], r, 0)),
                  pl.BlockSpec((None, tr, C), lambda r, k, cc: (k, r, 0))],
        out_specs=[pl.BlockSpec((tr, C), lambda r, k, cc: (r, 0)),
                   pl.BlockSpec((None, tr, C), lambda r, k, cc: (k, r, 0))])
    return pl.pallas_call(
        body, name=name, grid_spec=grid_spec,
        out_shape=[jax.ShapeDtypeStruct((R, C), F32), jax.ShapeDtypeStruct((4, R, C), BF16)],
        compiler_params=_cp(2))(core_chip, g, r1)


def _adamw_call(w, m, v, addends, name, deps=()):
    R, C = w.shape
    tr = _tile(R, max(8, (MIB // 2) // (C * 4)), 16)
    na = len(addends)
    c1 = 1.0 - ADAM_B1 ** ADAM_STEP
    c2 = 1.0 - ADAM_B2 ** ADAM_STEP

    def body(*refs):
        w_ref, m_ref, v_ref = refs[:3]
        a_refs = refs[3:3 + na]
        g_ref, d_ref, nm_ref, nv_ref = refs[3 + na:]
        g = a_refs[0][...].astype(F32)
        for a_ref in a_refs[1:]:
            g = g + a_ref[...].astype(F32)
        nm = ADAM_B1 * m_ref[...] + (1.0 - ADAM_B1) * g
        nv = ADAM_B2 * v_ref[...] + (1.0 - ADAM_B2) * (g * g)
        g_ref[...] = g
        nm_ref[...] = nm
        nv_ref[...] = nv
        d_ref[...] = -ADAM_LR * ((nm / c1) / (jnp.sqrt(nv / c2) + ADAM_EPS) + ADAM_WD * w_ref[...])

    flat = pl.BlockSpec((tr, C), lambda r: (r, 0))
    a_specs = [flat if k is None else pl.BlockSpec((None, tr, C), functools.partial(lambda r, kk: (kk, r, 0), kk=k))
               for _, k in addends]
    out = jax.ShapeDtypeStruct((R, C), F32)
    return _pallas(
        body, 3 + na, deps, name=name, grid=(R // tr,),
        in_specs=[flat, flat, flat] + a_specs, out_specs=[flat] * 4, out_shape=[out] * 4,
        compiler_params=_cp(1))(w, m, v, *[a for a, _ in addends], *deps)


def _sum_call(parts, row0, rows, name):
    n = parts.shape[0]
    tr = _tile(math.gcd(row0, rows), 256, 8)
    b0 = row0 // tr

    def body(p_ref, o_ref):
        s = p_ref[0]
        for k in range(1, n):
            s = s + p_ref[k]
        o_ref[...] = s

    return pl.pallas_call(
        body, name=name, grid=(rows // tr,),
        in_specs=[pl.BlockSpec((n, tr, LANES), lambda r: (0, r + b0, 0))],
        out_specs=pl.BlockSpec((tr, LANES), lambda r: (r, 0)),
        out_shape=jax.ShapeDtypeStruct((rows, LANES), F32), compiler_params=_cp(1))(parts)


def _rmsnorm_call(x, g, name, deps=()):
    S, D = x.shape
    tm = _tile(S, 512, 16)

    def body(x_ref, g_ref, o_ref):
        xv = x_ref[...]
        r = lax.rsqrt(_rows_mean(xv * xv) + RMS_EPS)
        o_ref[...] = (xv * r * g_ref[...]).astype(BF16)

    return _pallas(
        body, 2, deps, name=name, grid=(S // tm,),
        in_specs=[pl.BlockSpec((tm, D), lambda i: (i, 0)), pl.BlockSpec((1, D), lambda i: (0, 0))],
        out_specs=pl.BlockSpec((tm, D), lambda i: (i, 0)),
        out_shape=jax.ShapeDtypeStruct((S, D), BF16), compiler_params=_cp(1))(x, g, *deps)


def _mm_some_blocks_call(a, wg, blocks, out_dtype, name, out_so_far=None):
    S, K = a.shape
    nb, _, bn = wg.shape
    tm = _tile(S, 1024, 16)

    def body(b_ref, a_ref, w_ref, *rest):
        rest[-1][...] = jnp.dot(a_ref[...], w_ref[...], preferred_element_type=F32).astype(out_dtype)

    in_specs = [pl.BlockSpec((tm, K), lambda i, j, b: (i, 0)), pl.BlockSpec((None, K, bn), lambda i, j, b: (b[j], 0, 0))]
    operands = [a, wg]
    aliases = {}
    if out_so_far is not None:
        in_specs.append(pl.BlockSpec(memory_space=pl.ANY))
        operands.append(out_so_far)
        aliases = {3: 0}
    grid_spec = pltpu.PrefetchScalarGridSpec(
        num_scalar_prefetch=1, grid=(S // tm, blocks.shape[0]), in_specs=in_specs,
        out_specs=pl.BlockSpec((tm, bn), lambda i, j, b: (i, b[j])))
    return pl.pallas_call(body, name=name, grid_spec=grid_spec,
                          out_shape=jax.ShapeDtypeStruct((S, nb * bn), out_dtype), input_output_aliases=aliases,
                          compiler_params=_cp(2))(blocks, *operands)


def _mm_out_call(x, y, w, name):
    S, D = x.shape
    tm = _tile(S, 512, 16)

    def body(x_ref, y_ref, w_ref, o_ref):
        o_ref[...] = x_ref[...] + jnp.dot(y_ref[...], w_ref[...], preferred_element_type=F32)

    return pl.pallas_call(
        body, name=name, grid=(S // tm,),
        in_specs=[pl.BlockSpec((tm, D), lambda i: (i, 0)), pl.BlockSpec((tm, D), lambda i: (i, 0)),
                  pl.BlockSpec((D, D), lambda i: (0, 0))],
        out_specs=pl.BlockSpec((tm, D), lambda i: (i, 0)),
        out_shape=jax.ShapeDtypeStruct((S, D), F32), compiler_params=_cp(1))(x, y, w)


def _mm_nt_call(a, w, out_dtype, name, deps=()):
    S, K = a.shape
    N = w.shape[0]
    tm = _tile(S, 1024, 16)
    tn = _tile(N, 768, LANES)

    def body(a_ref, w_ref, o_ref):
        o_ref[...] = lax.dot_general(a_ref[...], w_ref[...], _DN_NT, preferred_element_type=F32).astype(out_dtype)

    return _pallas(
        body, 2, deps, name=name, grid=(S // tm, N // tn),
        in_specs=[pl.BlockSpec((tm, K), lambda i, j: (i, 0)), pl.BlockSpec((tn, K), lambda i, j: (j, 0))],
        out_specs=pl.BlockSpec((tm, tn), lambda i, j: (i, j)),
        out_shape=jax.ShapeDtypeStruct((S, N), out_dtype), compiler_params=_cp(2))(a, w, *deps)


def _mm_down_loss_call(x2, f, w, final_g, target, name):
    S, D = x2.shape
    Fd = f.shape[1]
    tm = _tile(S, 512, 16)
    tk = _tile(Fd, 768, LANES)
    nk = Fd // tk

    def body(x_ref, f_ref, w_ref, g_ref, t_ref, dx_ref, dxb_ref, loss_ref, dg_ref, acc):
        i, k = pl.program_id(0), pl.program_id(1)

        @pl.when(jnp.logical_and(i == 0, k == 0))
        def _():
            loss_ref[...] = jnp.zeros_like(loss_ref)
            dg_ref[...] = jnp.zeros_like(dg_ref)

        @pl.when(k == 0)
        def _():
            acc[...] = jnp.zeros_like(acc)

        acc[...] += jnp.dot(f_ref[...], w_ref[...], preferred_element_type=F32)

        @pl.when(k == nk - 1)
        def _():
            x3 = x_ref[...] + acc[...]
            r = lax.rsqrt(_rows_mean(x3 * x3) + RMS_EPS)
            g = g_ref[...]
            xn = x3 * r
            diff = xn * g - t_ref[...]
            loss_ref[...] += 0.5 * jnp.sum(_rows_mean(diff * diff))
            dout = diff * (1.0 / D)
            dg_ref[...] += _col_sum(dout * xn)
            dyg = dout * g
            dx = r * (dyg - xn * _rows_mean(dyg * xn))
            dx_ref[...] = dx
            dxb_ref[...] = dx.astype(BF16)

    row = lambda i, k: (i, 0)
    return pl.pallas_call(
        body, name=name, grid=(S // tm, nk),
        in_specs=[pl.BlockSpec((tm, D), row), pl.BlockSpec((tm, tk), lambda i, k: (i, k)),
                  pl.BlockSpec((tk, D), lambda i, k: (k, 0)), pl.BlockSpec((1, D), lambda i, k: (0, 0)),
                  pl.BlockSpec((tm, D), row)],
        out_specs=[pl.BlockSpec((tm, D), row), pl.BlockSpec((tm, D), row),
                   pl.BlockSpec((8, LANES), lambda i, k: (0, 0)), pl.BlockSpec((1, D), lambda i, k: (0, 0))],
        out_shape=[jax.ShapeDtypeStruct((S, D), F32), jax.ShapeDtypeStruct((S, D), BF16),
                   jax.ShapeDtypeStruct((8, LANES), F32), jax.ShapeDtypeStruct((1, D), F32)],
        scratch_shapes=[pltpu.VMEM((tm, D), F32)], compiler_params=_cp(2, 56))(x2, f, w, final_g, target)


def _mm_dx_norm_call(a3, wg, resid, xin, g, name, deps=()):
    na, S, Fa = a3.shape
    nb, D, bn = wg.shape
    tm = _tile(S, 512, 16)
    tk = _tile(bn, 1536, LANES)
    nsub = bn // tk
    nka = Fa // tk
    nk = nb * nsub
    assert na * nka == nk

    def body(a_ref, w_ref, r_ref, x_ref, g_ref, dx_ref, dxb_ref, dg_ref, acc):
        i, k = pl.program_id(0), pl.program_id(1)

        @pl.when(jnp.logical_and(i == 0, k == 0))
        def _():
            dg_ref[...] = jnp.zeros_like(dg_ref)

        @pl.when(k == 0)
        def _():
            acc[...] = jnp.zeros_like(acc)

        acc[...] += lax.dot_general(a_ref[...], w_ref[...], _DN_NT, preferred_element_type=F32)

        @pl.when(k == nk - 1)
        def _():
            dh = acc[...]
            xv = x_ref[...]
            r = lax.rsqrt(_rows_mean(xv * xv) + RMS_EPS)
            xn = xv * r
            dg_ref[...] += _col_sum(dh * xn)
            dyg = dh * g_ref[...]
            dx = r_ref[...] + r * (dyg - xn * _rows_mean(dyg * xn))
            dx_ref[...] = dx
            dxb_ref[...] = dx.astype(BF16)

    row = lambda i, k: (i, 0)
    return _pallas(
        body, 5, deps, name=name, grid=(S // tm, nk),
        in_specs=[pl.BlockSpec((None, tm, tk), lambda i, k: (k // nka, i, k % nka)),
                  pl.BlockSpec((None, D, tk), lambda i, k: (k // nsub, 0, k % nsub)),
                  pl.BlockSpec((tm, D), row, pipeline_mode=pl.Buffered(1)),
                  pl.BlockSpec((tm, D), row, pipeline_mode=pl.Buffered(1)), pl.BlockSpec((1, D), lambda i, k: (0, 0))],
        out_specs=[pl.BlockSpec((tm, D), row), pl.BlockSpec((tm, D), row), pl.BlockSpec((1, D), lambda i, k: (0, 0))],
        out_shape=[jax.ShapeDtypeStruct((S, D), F32), jax.ShapeDtypeStruct((S, D), BF16),
                   jax.ShapeDtypeStruct((1, D), F32)],
        scratch_shapes=[pltpu.VMEM((tm, D), F32)], compiler_params=_cp(2, 56))(a3, wg, resid, xin, g, *deps)


def _unblock_call(wg, name):
    nb, K, bn = wg.shape

    def body(w_ref, o_ref):
        o_ref[...] = w_ref[...]

    return pl.pallas_call(
        body, name=name, grid=(nb,),
        in_specs=[pl.BlockSpec((None, K, bn), lambda o: (o, 0, 0))],
        out_specs=pl.BlockSpec((K, bn), lambda o: (0, o)),
        out_shape=jax.ShapeDtypeStruct((K, nb * bn), wg.dtype), compiler_params=_cp(1))(wg)


def _mm_nt_norm_call(a, w, resid, xin, g, name, deps=(), part=(0, 1), dx_so_far=None):
    S, K = a.shape
    D = w.shape[0]
    tm = _tile(S, 256, 16)
    tiles = (S // tm) // part[1]
    first = part[0] * tiles

    def body(a_ref, w_ref, r_ref, x_ref, g_ref, *rest):
        dx_ref, dg_ref = rest[-2:]

        @pl.when(pl.program_id(0) == 0)
        def _():
            dg_ref[...] = jnp.zeros_like(dg_ref)

        dh = lax.dot_general(a_ref[...], w_ref[...], _DN_NT, preferred_element_type=F32)
        xv = x_ref[...]
        r = lax.rsqrt(_rows_mean(xv * xv) + RMS_EPS)
        xn = xv * r
        dg_ref[...] += _col_sum(dh * xn)
        dyg = dh * g_ref[...]
        dx_ref[...] = r_ref[...] + r * (dyg - xn * _rows_mean(dyg * xn))

    row = lambda i: (i + first, 0)
    fixed = lambda i: (0, 0)
    in_specs = [pl.BlockSpec((tm, K), row), pl.BlockSpec((D, K), fixed, pipeline_mode=pl.Buffered(1)),
                pl.BlockSpec((tm, D), row), pl.BlockSpec((tm, D), row), pl.BlockSpec((1, D), fixed)]
    operands = [a, w, resid, xin, g]
    aliases = {}
    if dx_so_far is not None:
        in_specs.append(pl.BlockSpec(memory_space=pl.ANY))
        operands.append(dx_so_far)
        aliases = {5: 0}
    return _pallas(
        body, len(operands), deps, name=name, grid=(tiles,),
        in_specs=in_specs, out_specs=[pl.BlockSpec((tm, D), row), pl.BlockSpec((1, D), fixed)],
        out_shape=[jax.ShapeDtypeStruct((S, D), F32), jax.ShapeDtypeStruct((1, D), F32)],
        input_output_aliases=aliases, compiler_params=_cp(1, 56))(*operands, *deps)


def _mm_tn_cols_call(a, b3, nb, bn, name, deps=()):
    S, Ka = a.shape
    nh, _, Fb = b3.shape
    tm = _tile(S, 2048, 16)
    tn = _tile(bn, 768, LANES)
    nsub = bn // tn
    njb = Fb // tn
    J = nb * nsub
    assert nh * njb == J

    def body(a_ref, b_ref, o_ref):
        @pl.when(pl.program_id(1) == 0)
        def _():
            o_ref[...] = jnp.zeros_like(o_ref)

        o_ref[...] += lax.dot_general(a_ref[...], b_ref[...], _DN_TN, preferred_element_type=F32)

    return _pallas(
        body, 2, deps, name=name, grid=(J, S // tm),
        in_specs=[pl.BlockSpec((tm, Ka), lambda j, i: (i, 0)),
                  pl.BlockSpec((None, tm, tn), lambda j, i: (j // njb, i, j % njb))],
        out_specs=pl.BlockSpec((None, Ka, tn), lambda j, i: (j // nsub, 0, j % nsub)),
        out_shape=jax.ShapeDtypeStruct((nb, Ka, bn), F32), compiler_params=_cp(2, 56))(a, b3, *deps)


def _mm_tn_rows_call(a, b, name, deps=()):
    S, E = a.shape
    D = b.shape[1]
    tm = _tile(S, 2048, 16)
    te = _tile(E, 768, LANES)

    def body(a_ref, b_ref, o_ref):
        @pl.when(pl.program_id(1) == 0)
        def _():
            o_ref[...] = jnp.zeros_like(o_ref)

        o_ref[...] += lax.dot_general(a_ref[...], b_ref[...], _DN_TN, preferred_element_type=F32)

    return _pallas(
        body, 2, deps, name=name, grid=(E // te, S // tm),
        in_specs=[pl.BlockSpec((tm, te), lambda j, i: (i, j)), pl.BlockSpec((tm, D), lambda j, i: (i, 0))],
        out_specs=pl.BlockSpec((te, D), lambda j, i: (j, 0)),
        out_shape=jax.ShapeDtypeStruct((E, D), F32), compiler_params=_cp(2, 56))(a, b, *deps)


def _ffn_tiles(S, Fd):
    return _tile(S, 512, 16), _tile(Fd // (N_DEV // 2), 1536, LANES)


def _taps(cw_ref):
    return [cw_ref[k:k + 1, :] for k in range(cw_ref.shape[0])]


def _conv3(prev8, cur, taps):
    s1 = _shift_down(prev8, cur, 1)
    s2 = _shift_down(prev8, cur, 2)
    return taps[2] * cur + taps[1] * s1 + taps[0] * s2, s1, s2


SUB_LANES = 256


def _lane_taps(cw_ref, ls):
    return [cw_ref[k:k + 1, ls] for k in range(cw_ref.shape[0])]


def _ffn_up_act_call(h2, wg, cw, cb, name, deps=()):
    S, D = h2.shape
    nb, _, bn = wg.shape
    Fd = nb * bn // 2
    tm, tc = _ffn_tiles(S, Fd)
    nk = Fd // tc
    hb = tm // 16
    nsubw = bn // tc
    half = nb // 2
    sc = _tile(tc, SUB_LANES, LANES)

    def body(a_ref, ap_ref, wgate_ref, wval_ref, cwg_ref, cwv_ref, cbg_ref, cbv_ref, up_ref, upc_ref, f_ref):
        keep = jnp.where(pl.program_id(0) == 0, 0.0, 1.0)
        a_ext = jnp.concatenate([ap_ref[...], a_ref[...]], axis=0)
        nsub = tc // sc
        lanes = [slice(s * sc, (s + 1) * sc) for s in range(nsub)]

        def products(s):
            return [jnp.dot(a_ext, w_ref[:, lanes[s]], preferred_element_type=F32).astype(BF16)
                    for w_ref in (wgate_ref, wval_ref)]

        ready = products(0)
        for s in range(nsub):
            ls = lanes[s]
            following = products(s + 1) if s + 1 < nsub else None

            def conv_half(ub, cw_ref, cb_ref, slab):
                up_ref[slab, :, ls] = ub[16:]
                u = ub.astype(F32)
                conv, _, _ = _conv3(u[8:16] * keep, u[16:], _lane_taps(cw_ref, ls))
                c = conv + cb_ref[:, ls]
                upc_ref[slab, :, ls] = c.astype(BF16)
                return c

            cg = conv_half(ready[0], cwg_ref, cbg_ref, 0)
            cv = conv_half(ready[1], cwv_ref, cbv_ref, 1)
            f_ref[:, ls] = (_gelu(cg) * cv).astype(BF16)
            ready = following

    return _pallas(
        body, 8, deps, name=name, grid=(S // tm, nk),
        in_specs=[pl.BlockSpec((tm, D), lambda i, k: (i, 0)),
                  pl.BlockSpec((16, D), lambda i, k: (jnp.maximum(i * hb - 1, 0), 0)),
                  pl.BlockSpec((None, D, tc), lambda i, k: (k // nsubw, 0, k % nsubw)),
                  pl.BlockSpec((None, D, tc), lambda i, k: (half + k // nsubw, 0, k % nsubw)),
                  pl.BlockSpec((3, tc), lambda i, k: (0, k)), pl.BlockSpec((3, tc), lambda i, k: (0, k + nk)),
                  pl.BlockSpec((1, tc), lambda i, k: (0, k)), pl.BlockSpec((1, tc), lambda i, k: (0, k + nk))],
        out_specs=[pl.BlockSpec((2, tm, tc), lambda i, k: (0, i, k)), pl.BlockSpec((2, tm, tc), lambda i, k: (0, i, k)),
                   pl.BlockSpec((tm, tc), lambda i, k: (i, k))],
        out_shape=[jax.ShapeDtypeStruct((2, S, Fd), BF16), jax.ShapeDtypeStruct((2, S, Fd), BF16),
                   jax.ShapeDtypeStruct((S, Fd), BF16)],
        compiler_params=_cp(2, 56))(h2, h2, wg, wg, cw, cw, cb, cb, *deps)


def _ffn_down_dx_act_bwd_call(dxb, w, up3, upc3, cw, name, deps=()):
    S, D = dxb.shape
    _, _, Fd = up3.shape
    tm, tc = _ffn_tiles(S, Fd)
    nj = Fd // tc
    ni = S // tm
    hb = tm // 16
    sc = _tile(tc, SUB_LANES, LANES)

    def body(a_ref, an_ref, w_ref, g_ref, v_ref, cg_ref, cv_ref, cgn_ref, cvn_ref, cwg_ref, cwv_ref,
             dup_ref, dcwg_ref, dcwv_ref, dcbg_ref, dcbv_ref):
        i = pl.program_id(1)
        keep_next = jnp.where(i == ni - 1, 0.0, 1.0)

        @pl.when(i == 0)
        def _():
            for r in (dcwg_ref, dcwv_ref, dcbg_ref, dcbv_ref):
                r[...] = jnp.zeros_like(r)

        a_ext = jnp.concatenate([a_ref[...], an_ref[...]], axis=0)
        for s in range(tc // sc):
            ls = slice(s * sc, (s + 1) * sc)
            df_ext = lax.dot_general(a_ext, w_ref[s * sc:(s + 1) * sc, :], _DN_NT, preferred_element_type=F32)
            df = jnp.concatenate([df_ext[:tm], df_ext[tm:tm + 8] * keep_next], axis=0)
            cg = jnp.concatenate([cg_ref[:, ls].astype(F32), cgn_ref[:, ls].astype(F32)[:8]], axis=0)
            cv = jnp.concatenate([cv_ref[:, ls].astype(F32), cvn_ref[:, ls].astype(F32)[:8]], axis=0)
            gel, dgel = _gelu_parts(cg)

            def back(d, cw_ref, x_ref, dcw_ref, dcb_ref, slab):
                taps = _lane_taps(cw_ref, ls)
                d0 = d[:tm]
                d1 = pltpu.roll(d, tm + 8 - 1, 0)[:tm]
                d2 = pltpu.roll(d, tm + 8 - 2, 0)[:tm]
                dup_ref[slab, :, ls] = (taps[2] * d0 + taps[1] * d1 + taps[0] * d2).astype(BF16)
                xv = x_ref[:, ls].astype(F32)
                dcw_ref[2:3, ls] += _col_sum(xv * d0)
                dcw_ref[1:2, ls] += _col_sum(xv * d1)
                dcw_ref[0:1, ls] += _col_sum(xv * d2)
                dcb_ref[:, ls] += _col_sum(d0)

            back(df * cv * dgel, cwg_ref, g_ref, dcwg_ref, dcbg_ref, 0)
            back(df * gel, cwv_ref, v_ref, dcwv_ref, dcbv_ref, 1)

    nxt = lambda j, i: jnp.minimum((i + 1) * hb, S // 16 - 1)
    main = lambda s: pl.BlockSpec((None, tm, tc), lambda j, i: (s, i, j))
    halo = lambda s: pl.BlockSpec((None, 16, tc), lambda j, i: (s, nxt(j, i), j))
    acc3 = pl.BlockSpec((3, tc), lambda j, i: (0, j))
    acc1 = pl.BlockSpec((1, tc), lambda j, i: (0, j))
    return _pallas(
        body, 11, deps, name=name, grid=(nj, ni),
        in_specs=[pl.BlockSpec((tm, D), lambda j, i: (i, 0)), pl.BlockSpec((16, D), lambda j, i: (nxt(j, i), 0)),
                  pl.BlockSpec((tc, D), lambda j, i: (j, 0)),
                  main(0), main(1), main(0), main(1), halo(0), halo(1),
                  pl.BlockSpec((3, tc), lambda j, i: (0, j)), pl.BlockSpec((3, tc), lambda j, i: (0, j + nj))],
        out_specs=[pl.BlockSpec((2, tm, tc), lambda j, i: (0, i, j)), acc3, acc3, acc1, acc1],
        out_shape=[jax.ShapeDtypeStruct((2, S, Fd), BF16), jax.ShapeDtypeStruct((3, Fd), F32),
                   jax.ShapeDtypeStruct((3, Fd), F32), jax.ShapeDtypeStruct((1, Fd), F32),
                   jax.ShapeDtypeStruct((1, Fd), F32)],
        compiler_params=_cp(2, 56))(dxb, dxb, w, up3, up3, upc3, upc3, upc3, upc3, cw, cw, *deps)


def _gm_forward_tile(pv, vg, vb, ws_ref, bsb_ref, mbuf, H, nc):
    W = H * CHUNK
    z, dz = _gelu_parts(pv)
    u, v0 = z[:, :W], z[:, W:]
    xc = v0 - _rows_mean(v0)
    rs = lax.rsqrt(_rows_mean(xc * xc) + LN_EPS)
    vh = xc * rs
    vnb = (vh * vg + vb).astype(BF16)
    mask = lax.broadcasted_iota(jnp.int32, (CHUNK, CHUNK), 0) >= lax.broadcasted_iota(jnp.int32, (CHUNK, CHUNK), 1)
    for h in range(H):
        cs = slice(h * CHUNK, (h + 1) * CHUNK)
        wm = jnp.where(mask, ws_ref[h], 0.0).astype(BF16)
        vcat = jnp.concatenate([vnb[c * CHUNK:(c + 1) * CHUNK, cs] for c in range(nc)], axis=1)
        mix = jnp.dot(wm, vcat, preferred_element_type=F32)
        for c in range(nc):
            mbuf[c * CHUNK:(c + 1) * CHUNK, cs] = mix[:, c * CHUNK:(c + 1) * CHUNK] + bsb_ref[h]
    return dz, u, vh, rs, vnb, mask


def _gm_fwd_call(p, v_g, v_b, ws, bsb, out_g, name, deps=()):
    S = p.shape[0]
    H = ws.shape[0]
    W = H * CHUNK
    tm = _tile(S, 256, CHUNK)
    nc = tm // CHUNK

    def body(p_ref, vg_ref, vb_ref, ws_ref, bsb_ref, og_ref, y_ref, mbuf):
        _, u, _, _, _, _ = _gm_forward_tile(p_ref[...], vg_ref[...], vb_ref[...], ws_ref, bsb_ref, mbuf, H, nc)
        yg = u * mbuf[...]
        r = lax.rsqrt(_rows_mean(yg * yg) + RMS_EPS)
        y_ref[...] = (yg * r * og_ref[...]).astype(BF16)

    vec = pl.BlockSpec((1, W), lambda i: (0, 0))
    mat = pl.BlockSpec((H, CHUNK, CHUNK), lambda i: (0, 0, 0))
    return _pallas(
        body, 6, deps, name=name, grid=(S // tm,),
        in_specs=[pl.BlockSpec((tm, 2 * W), lambda i: (i, 0)), vec, vec, mat, mat, vec],
        out_specs=pl.BlockSpec((tm, W), lambda i: (i, 0)),
        out_shape=jax.ShapeDtypeStruct((S, 2 * W), BF16),
        scratch_shapes=[pltpu.VMEM((tm, W), F32)], compiler_params=_cp(1))(p, v_g, v_b, ws, bsb, out_g, *deps)


def _gm_bwd_call(p, d_y, v_g, v_b, ws, bsb, out_g, name, deps=()):
    S = p.shape[0]
    H = ws.shape[0]
    W = H * CHUNK
    tm = _tile(S, 256, CHUNK)
    nc = tm // CHUNK
    ni = S // tm

    def body(p_ref, dy_ref, vg_ref, vb_ref, ws_ref, bsb_ref, og_ref,
             dp_ref, dvg_ref, dvb_ref, dws_ref, dbs_ref, dog_ref, mbuf, dvbuf):
        i = pl.program_id(0)

        @pl.when(i == 0)
        def _():
            for r in (dvg_ref, dvb_ref, dws_ref, dbs_ref, dog_ref):
                r[...] = jnp.zeros_like(r)

        vg = vg_ref[...]
        dz, u, vh, rs, vnb, mask = _gm_forward_tile(p_ref[...], vg, vb_ref[...], ws_ref, bsb_ref, mbuf, H, nc)
        mixed = mbuf[...]
        yg = u * mixed
        r = lax.rsqrt(_rows_mean(yg * yg) + RMS_EPS)
        yn = yg * r
        dya = dy_ref[...]
        dog_ref[...] += _col_sum(dya * yn)
        dyg = dya * og_ref[...]
        dygm = r * (dyg - yn * _rows_mean(dyg * yn))
        du = dygm * mixed
        dmix = dygm * u
        dmb = dmix.astype(BF16)
        for h in range(H):
            cs = slice(h * CHUNK, (h + 1) * CHUNK)
            wm = jnp.where(mask, ws_ref[h], 0.0).astype(BF16)
            dcat = jnp.concatenate([dmb[c * CHUNK:(c + 1) * CHUNK, cs] for c in range(nc)], axis=1)
            vcat = jnp.concatenate([vnb[c * CHUNK:(c + 1) * CHUNK, cs] for c in range(nc)], axis=1)
            dvn = lax.dot_general(wm, dcat, _DN_TN, preferred_element_type=F32)
            dws_ref[h] += jnp.where(mask, lax.dot_general(dcat, vcat, _DN_NT, preferred_element_type=F32), 0.0)
            dbs = dmix[0:CHUNK, cs]
            for c in range(1, nc):
                dbs = dbs + dmix[c * CHUNK:(c + 1) * CHUNK, cs]
            dbs_ref[h] += dbs
            for c in range(nc):
                dvbuf[c * CHUNK:(c + 1) * CHUNK, cs] = dvn[:, c * CHUNK:(c + 1) * CHUNK]
        dvn_all = dvbuf[...]
        dvg_ref[...] += _col_sum(dvn_all * vh)
        dvb_ref[...] += _col_sum(dvn_all)
        dvh = dvn_all * vg
        dv0 = rs * (dvh - _rows_mean(dvh) - vh * _rows_mean(dvh * vh))
        dp_ref[...] = (jnp.concatenate([du, dv0], axis=1) * dz).astype(BF16)

        @pl.when(i == ni - 1)
        def _():
            for h in range(H):
                dbs_ref[h] = jnp.broadcast_to(jnp.sum(dbs_ref[h], axis=1, keepdims=True), (CHUNK, CHUNK))

    vec = pl.BlockSpec((1, W), lambda i: (0, 0))
    mat = pl.BlockSpec((H, CHUNK, CHUNK), lambda i: (0, 0, 0))
    vshape = jax.ShapeDtypeStruct((1, W), F32)
    mshape = jax.ShapeDtypeStruct((H, CHUNK, CHUNK), F32)
    return _pallas(
        body, 7, deps, name=name, grid=(ni,),
        in_specs=[pl.BlockSpec((tm, 2 * W), lambda i: (i, 0)), pl.BlockSpec((tm, W), lambda i: (i, 0)),
                  vec, vec, mat, mat, vec],
        out_specs=[pl.BlockSpec((tm, 2 * W), lambda i: (i, 0)), vec, vec, mat, mat, vec],
        out_shape=[jax.ShapeDtypeStruct((S, 4 * W), BF16), vshape, vshape, mshape, mshape, vshape],
        scratch_shapes=[pltpu.VMEM((tm, W), F32), pltpu.VMEM((tm, W), F32)],
        compiler_params=_cp(1))(p, d_y, v_g, v_b, ws, bsb, out_g, *deps)


def _lru_gates(prev8, xl, cw, cb, wa_ref, ba, wx_ref, bx, lam, H):
    sh = [_shift_down(prev8, xl, k) for k in range(4)]
    xr = cw[3] * sh[0] + cw[2] * sh[1] + cw[1] * sh[2] + cw[0] * sh[3] + cb
    xrb = xr.astype(BF16)
    rp, ip = [], []
    for h in range(H):
        cs = slice(h * CHUNK, (h + 1) * CHUNK)
        rp.append(jnp.dot(xrb[:, cs], wa_ref[h].astype(BF16), preferred_element_type=F32))
        ip.append(jnp.dot(xrb[:, cs], wx_ref[h].astype(BF16), preferred_element_type=F32))
    r = _sigmoid(jnp.concatenate(rp, axis=1) + ba)
    ig = _sigmoid(jnp.concatenate(ip, axis=1) + bx)
    sp = _softplus(-lam)
    t = jnp.tanh((-LRU_C) * r * sp)
    q = lax.rsqrt(1.0 - t)
    a = jnp.sqrt(1.0 + t) * q
    mult = jnp.sqrt(-2.0 * t) * q
    a2_over_mult = (1.0 + t) * q * lax.rsqrt(-2.0 * t)
    return xr, xrb, r, ig, sp, a, mult, a2_over_mult, sh


def _lru_fwd_call(p, cw, cb, wa, ba, wx, bx, lam, out_g, y_half, name):
    S = p.shape[0]
    H = wa.shape[0]
    W = H * CHUNK
    tm = _tile(S, 256, 16)
    ng = tm // 8

    def body(pg_ref, px_ref, cw_ref, cb_ref, wa_ref, ba_ref, wx_ref, bx_ref, lam_ref, og_ref, y_in_ref,
             y_ref, h_ref, xprev, hcar, abuf, bbuf):
        @pl.when(pl.program_id(0) == 0)
        def _():
            xprev[...] = jnp.zeros_like(xprev)
            hcar[...] = jnp.zeros_like(hcar)

        xl = px_ref[...]
        xr, _, _, ig, _, a, mult, _, _ = _lru_gates(xprev[...], xl, _taps(cw_ref), cb_ref[...], wa_ref, ba_ref[...],
                                                    wx_ref, bx_ref[...], lam_ref[...], H)
        xprev[...] = xl[tm - 8:]
        b = mult * (ig * xr)
        sub = lax.broadcasted_iota(jnp.int32, (tm, W), 0) & 7
        for d in (1, 2, 4):
            m = sub >= d
            a_s = jnp.where(m, pltpu.roll(a, d, 0), 1.0)
            b_s = jnp.where(m, pltpu.roll(b, d, 0), 0.0)
            b = a * b_s + b
            a = a * a_s
        abuf[...] = a
        bbuf[...] = b

        def step(g, carry):
            r0 = pl.multiple_of(g * 8, 8)
            h_ref[pl.ds(r0, 8), :] = abuf[pl.ds(r0, 8), :] * carry + bbuf[pl.ds(r0, 8), :]
            return jnp.broadcast_to(h_ref[pl.ds(r0 + 7, 1), :], (8, W))

        hcar[...] = lax.fori_loop(0, ng, step, hcar[...])
        yl = h_ref[...] * _gelu(pg_ref[...])
        r = lax.rsqrt(_rows_mean(yl * yl) + RMS_EPS)
        y_ref[...] = (yl * r * og_ref[...]).astype(BF16)

    vec = pl.BlockSpec((1, W), lambda i: (0, 0))
    mat = pl.BlockSpec((H, CHUNK, CHUNK), lambda i: (0, 0, 0))
    return pl.pallas_call(
        body, name=name, grid=(S // tm,),
        in_specs=[pl.BlockSpec((tm, W), lambda i: (i, 2)), pl.BlockSpec((tm, W), lambda i: (i, 3)),
                  pl.BlockSpec((4, W), lambda i: (0, 0)), vec, mat, vec, mat, vec, vec, vec,
                  pl.BlockSpec(memory_space=pl.ANY)],
        out_specs=[pl.BlockSpec((tm, W), lambda i: (i, 1)), pl.BlockSpec((tm, W), lambda i: (i, 0))],
        out_shape=[jax.ShapeDtypeStruct((S, 2 * W), BF16), jax.ShapeDtypeStruct((S, W), F32)],
        input_output_aliases={10: 0},
        scratch_shapes=[pltpu.VMEM((8, W), F32), pltpu.VMEM((8, W), F32), pltpu.VMEM((tm, W), F32),
                        pltpu.VMEM((tm, W), F32)],
        compiler_params=_cp(1))(p, p, cw, cb, wa, ba, wx, bx, lam, out_g, y_half)


def _lru_bwd_call(p, hs, d_y, cw, cb, wa, ba, wx, bx, lam, out_g, dp_half, name):
    S = p.shape[0]
    H = wa.shape[0]
    W = H * CHUNK
    tm = _tile(S, 256, 16)
    ng = tm // 8
    ni = S // tm
    hb = tm // 8

    def body(pg_ref, px_ref, pxp_ref, h_ref, hp_ref, dy_ref, cw_ref, cb_ref, wa_ref, ba_ref, wx_ref, bx_ref,
             lam_ref, og_ref, dp_in_ref,
             dp_ref, dcw_ref, dcb_ref, dwa_ref, dba_ref, dwx_ref, dbx_ref, dlam_ref, dog_ref,
             a_next, e_next, dxr_next, abuf, bbuf, ebuf):
        i = pl.program_id(0)
        ri = ni - 1 - i

        @pl.when(i == 0)
        def _():
            for r in (dcw_ref, dcb_ref, dwa_ref, dba_ref, dwx_ref, dbx_ref, dlam_ref, dog_ref,
                      a_next, e_next, dxr_next):
                r[...] = jnp.zeros_like(r)

        keep_prev = jnp.where(ri == 0, 0.0, 1.0)
        cw_ = _taps(cw_ref)
        lam_ = lam_ref[...]
        xl = px_ref[...]
        xr, xrb, r, ig, sp, a, mult, a2m, sh = _lru_gates(pxp_ref[...] * keep_prev, xl, cw_, cb_ref[...], wa_ref,
                                                          ba_ref[...], wx_ref, bx_ref[...], lam_, H)
        gg, dgg = _gelu_parts(pg_ref[...])
        hv = h_ref[...]
        yl = hv * gg
        rr = lax.rsqrt(_rows_mean(yl * yl) + RMS_EPS)
        yn = yl * rr
        dyb = dy_ref[...]
        dog_ref[...] += _col_sum(dyb * yn)
        dyg = dyb * og_ref[...]
        dyl = rr * (dyg - yn * _rows_mean(dyg * yn))
        dh = dyl * gg
        dgl = dyl * hv * dgg

        an = _shift_up(a, a_next[...], 1)
        eb = dh
        sub = lax.broadcasted_iota(jnp.int32, (tm, W), 0) & 7
        for d in (1, 2, 4):
            m = sub < 8 - d
            a_s = jnp.where(m, pltpu.roll(an, tm - d, 0), 1.0)
            e_s = jnp.where(m, pltpu.roll(eb, tm - d, 0), 0.0)
            eb = an * e_s + eb
            an = an * a_s
        abuf[...] = an
        bbuf[...] = eb

        def step(g, carry):
            r0 = pl.multiple_of((ng - 1 - g) * 8, 8)
            ebuf[pl.ds(r0, 8), :] = abuf[pl.ds(r0, 8), :] * carry + bbuf[pl.ds(r0, 8), :]
            return jnp.broadcast_to(ebuf[pl.ds(r0, 1), :], (8, W))

        lax.fori_loop(0, ng, step, jnp.broadcast_to(e_next[0:1, :], (8, W)))
        e = ebuf[...]
        a_next[...] = a[0:8]
        e_next[...] = e[0:8]

        hm1 = _shift_down(hp_ref[...] * keep_prev, hv, 1)
        da = e * hm1
        dmult = e * ig * xr
        di = e * mult * xr
        dxr = e * mult * ig
        dla = da * a - dmult * a2m
        dr = dla * ((-LRU_C) * sp)
        dlam_ref[...] += _col_sum(dla * ((-LRU_C) * r))
        dpr = dr * r * (1.0 - r)
        dpi = di * ig * (1.0 - ig)
        dba_ref[...] += _col_sum(dpr)
        dbx_ref[...] += _col_sum(dpi)
        dprb = dpr.astype(BF16)
        dpib = dpi.astype(BF16)
        back = []
        for h in range(H):
            cs = slice(h * CHUNK, (h + 1) * CHUNK)
            wab = wa_ref[h].astype(BF16)
            wxb = wx_ref[h].astype(BF16)
            back.append(lax.dot_general(dprb[:, cs], wab, _DN_NT, preferred_element_type=F32)
                        + lax.dot_general(dpib[:, cs], wxb, _DN_NT, preferred_element_type=F32))
            dwa_ref[h] += lax.dot_general(xrb[:, cs], dprb[:, cs], _DN_TN, preferred_element_type=F32)
            dwx_ref[h] += lax.dot_general(xrb[:, cs], dpib[:, cs], _DN_TN, preferred_element_type=F32)
        dxr = dxr + jnp.concatenate(back, axis=1)

        nxt = dxr_next[...]
        dxl = (cw_[3] * dxr + cw_[2] * _shift_up(dxr, nxt, 1) + cw_[1] * _shift_up(dxr, nxt, 2)
               + cw_[0] * _shift_up(dxr, nxt, 3))
        dxr_next[...] = dxr[0:8]
        for k in range(4):
            dcw_ref[k:k + 1, :] += _col_sum(sh[3 - k] * dxr)
        dcb_ref[...] += _col_sum(dxr)
        dp_ref[...] = jnp.concatenate([dgl, dxl], axis=1).astype(BF16)

        @pl.when(i == ni - 1)
        def _():
            dlam_ref[...] = -dlam_ref[...] * _sigmoid(-lam_)

    vec = pl.BlockSpec((1, W), lambda i: (0, 0))
    mat = pl.BlockSpec((H, CHUNK, CHUNK), lambda i: (0, 0, 0))
    rev = lambda i: ni - 1 - i
    prev = lambda i: jnp.maximum(rev(i) * hb - 1, 0)
    vshape = jax.ShapeDtypeStruct((1, W), F32)
    mshape = jax.ShapeDtypeStruct((H, CHUNK, CHUNK), F32)
    tile = lambda: pltpu.VMEM((tm, W), F32)
    car = lambda: pltpu.VMEM((8, W), F32)
    return pl.pallas_call(
        body, name=name, grid=(ni,),
        in_specs=[pl.BlockSpec((tm, W), lambda i: (rev(i), 2)), pl.BlockSpec((tm, W), lambda i: (rev(i), 3)),
                  pl.BlockSpec((8, W), lambda i: (prev(i), 3)),
                  pl.BlockSpec((tm, W), lambda i: (rev(i), 0)), pl.BlockSpec((8, W), lambda i: (prev(i), 0)),
                  pl.BlockSpec((tm, W), lambda i: (rev(i), 1)),
                  pl.BlockSpec((4, W), lambda i: (0, 0)), vec, mat, vec, mat, vec, vec, vec,
                  pl.BlockSpec(memory_space=pl.ANY)],
        out_specs=[pl.BlockSpec((tm, 2 * W), lambda i: (rev(i), 1)), pl.BlockSpec((4, W), lambda i: (0, 0)), vec,
                   mat, vec, mat, vec, vec, vec],
        out_shape=[jax.ShapeDtypeStruct((S, 4 * W), BF16), jax.ShapeDtypeStruct((4, W), F32), vshape,
                   mshape, vshape, mshape, vshape, vshape, vshape],
        input_output_aliases={14: 0},
        scratch_shapes=[car(), car(), car(), tile(), tile(), tile()],
        compiler_params=_cp(1, 56))(p, p, p, hs, hs, d_y, cw, cb, wa, ba, wx, bx, lam, out_g, dp_half)


def _rows128(a):
    return a.reshape(-1, LANES).astype(F32)


def _pack(arrays, pad_to=256):
    flat = jnp.concatenate([_rows128(a) for a in arrays], axis=0)
    pad = (-flat.shape[0]) % pad_to
    if pad:
        flat = jnp.concatenate([flat, jnp.zeros((pad, LANES), F32)], axis=0)
    return flat


def _unpack(flat, shapes):
    out, r = [], 0
    for s in shapes:
        n = 1
        for d in s:
            n *= d
        out.append(flat[r:r + n // LANES].reshape(s))
        r += n // LANES
    return out


def kernel(x, norm1_g, w_in, gm_v_g, gm_v_b, gm_ws, gm_bs, lru_conv_w, lru_conv_b, lru_wa, lru_ba, lru_wx, lru_bx, lru_lambda, gm_out_g, lru_out_g, w_out, norm2_g, ffn_w_up, ffn_conv_w, ffn_conv_b, ffn_w_down, final_g, loss_target, m_norm1_g, m_w_in, m_gm_v_g, m_gm_v_b, m_gm_ws, m_gm_bs, m_lru_conv_w, m_lru_conv_b, m_lru_wa, m_lru_ba, m_lru_wx, m_lru_bx, m_lru_lambda, m_gm_out_g, m_lru_out_g, m_w_out, m_norm2_g, m_ffn_w_up, m_ffn_conv_w, m_ffn_conv_b, m_ffn_w_down, m_final_g, v_norm1_g, v_w_in, v_gm_v_g, v_gm_v_b, v_gm_ws, v_gm_bs, v_lru_conv_w, v_lru_conv_b, v_lru_wa, v_lru_ba, v_lru_wx, v_lru_bx, v_lru_lambda, v_gm_out_g, v_lru_out_g, v_w_out, v_norm2_g, v_ffn_w_up, v_ffn_conv_w, v_ffn_conv_b, v_ffn_w_down, v_final_g):
    wts = dict(norm1_g=norm1_g, w_in=w_in, gm_v_g=gm_v_g, gm_v_b=gm_v_b, gm_ws=gm_ws, gm_bs=gm_bs,
               lru_conv_w=lru_conv_w, lru_conv_b=lru_conv_b, lru_wa=lru_wa, lru_ba=lru_ba, lru_wx=lru_wx,
               lru_bx=lru_bx, lru_lambda=lru_lambda, gm_out_g=gm_out_g, lru_out_g=lru_out_g, w_out=w_out,
               norm2_g=norm2_g, ffn_w_up=ffn_w_up, ffn_conv_w=ffn_conv_w, ffn_conv_b=ffn_conv_b,
               ffn_w_down=ffn_w_down, final_g=final_g)
    mom = dict(norm1_g=m_norm1_g, w_in=m_w_in, gm_v_g=m_gm_v_g, gm_v_b=m_gm_v_b, gm_ws=m_gm_ws, gm_bs=m_gm_bs,
               lru_conv_w=m_lru_conv_w, lru_conv_b=m_lru_conv_b, lru_wa=m_lru_wa, lru_ba=m_lru_ba, lru_wx=m_lru_wx,
               lru_bx=m_lru_bx, lru_lambda=m_lru_lambda, gm_out_g=m_gm_out_g, lru_out_g=m_lru_out_g, w_out=m_w_out,
               norm2_g=m_norm2_g, ffn_w_up=m_ffn_w_up, ffn_conv_w=m_ffn_conv_w, ffn_conv_b=m_ffn_conv_b,
               ffn_w_down=m_ffn_w_down, final_g=m_final_g)
    var = dict(norm1_g=v_norm1_g, w_in=v_w_in, gm_v_g=v_gm_v_g, gm_v_b=v_gm_v_b, gm_ws=v_gm_ws, gm_bs=v_gm_bs,
               lru_conv_w=v_lru_conv_w, lru_conv_b=v_lru_conv_b, lru_wa=v_lru_wa, lru_ba=v_lru_ba, lru_wx=v_lru_wx,
               lru_bx=v_lru_bx, lru_lambda=v_lru_lambda, gm_out_g=v_gm_out_g, lru_out_g=v_lru_out_g, w_out=v_w_out,
               norm2_g=v_norm2_g, ffn_w_up=v_ffn_w_up, ffn_conv_w=v_ffn_conv_w, ffn_conv_b=v_ffn_conv_b,
               ffn_w_down=v_ffn_w_down, final_g=v_final_g)

    xi, yi, ci = lax.axis_index("x"), lax.axis_index("y"), lax.axis_index("c")
    chip = 2 * xi + yi
    dev = 2 * chip + ci
    core_chip = jnp.stack([ci, chip]).astype(jnp.int32)

    xs = x[0]
    tgt = loss_target[0]
    S, D = xs.shape
    H = gm_ws.shape[1]
    W = H * CHUNK
    Fd = ffn_w_down.shape[1] * N_DEV
    lcw_cols = lru_conv_w.shape[2]
    fcw_cols = ffn_conv_w.shape[2]

    dev1 = jnp.reshape(dev, (1,)).astype(jnp.int32)

    def place_own(shards, name):
        return [_place_own_call(s, dev1, dt, "%s_own%d" % (name, k)) for k, (s, dt) in enumerate(shards)]

    def gather_start(shards, name, after=()):
        lands = place_own(shards, name)
        return _exchange_start([], lands, 4 * len(lands), _gather_stage1_copies(len(lands)), name + "_ici", after)

    def gather_forward(lands, name, after=()):
        return _exchange_start([], lands, 3 * len(lands), _gather_stage2_copies(len(lands)), name + "_d2d", after)

    def pair_start(g, name, after=()):
        return _exchange_start([g], [lax.empty((4,) + g.shape[1:], F32)], 4, _pair_copies(1), name, after)

    def chip_start(p16, name, after=()):
        return _exchange_start([p16], [lax.empty((3,) + p16.shape[1:], BF16)], 3, _chip_copies(1), name, after)

    vgm_g, vgm_b = gm_v_g, gm_v_b
    ws, wa, wx = gm_ws[0], lru_wa[0], lru_wx[0]
    bsb = jnp.broadcast_to(gm_bs[0][:, :, None], (H, CHUNK, CHUNK))
    ba, bx = lru_ba.reshape(1, W), lru_bx.reshape(1, W)
    fcb = ffn_conv_b
    fing = final_g.reshape(1, D)

    conv_pack = _pack([lru_conv_w[0], ffn_conv_w[0]], pad_to=8)
    lands = place_own([(w_in[0], BF16), (conv_pack, F32)], "gather_in")
    ga_pair = _exchange_start([], lands, 2, _gather_stage1_copies(2, to_chips=False), "gather_in_pair")
    ga1 = _exchange_start([], ga_pair.bufs, 6, _gather_stage1_copies(2, to_sibling=False), "gather_in_ici")
    h1 = _rmsnorm_call(xs, norm1_g, "norm1", deps=(ga1.token,))
    ga_pair.bufs = ga1.bufs
    _, la = _exchange_wait(ga_pair, after=(h1,))
    own_blocks = jnp.stack([dev, dev + 1 - 2 * ci]).astype(jnp.int32)
    other_blocks = ((2 * chip + 2 + jnp.arange(N_DEV - 2)) % N_DEV).astype(jnp.int32)
    p_own = _mm_some_blocks_call(h1, la[0], own_blocks, F32, "in_proj_own")
    ga1.bufs = la
    _, la = _exchange_wait(ga1, after=(p_own,))
    ga2 = gather_forward(la, "gather_in")
    gb1 = gather_start([(w_out[0], BF16)], "gather_out", after=(ga2.token,))
    gc1 = gather_start([(ffn_w_up[0], BF16)], "gather_up", after=(gb1.token,))
    _, (win_g, conv_g) = _exchange_wait(ga2, after=(gc1.token,))
    n_l = 4 * lcw_cols // LANES
    n_f = 3 * fcw_cols // LANES
    lcw = conv_g[:, :n_l].reshape(N_DEV, 4, lcw_cols).transpose(1, 0, 2).reshape(4, N_DEV * lcw_cols)
    fcw = conv_g[:, n_l:n_l + n_f].reshape(N_DEV, 3, fcw_cols).transpose(1, 0, 2).reshape(3, N_DEV * fcw_cols)

    p = _mm_some_blocks_call(h1, win_g, other_blocks, F32, "in_proj_rest", out_so_far=p_own)
    win_rows = _unblock_call(win_g, "w_in_rows")
    _, lb = _exchange_wait(gb1, after=(p,))
    gb2 = gather_forward(lb, "gather_out")
    y_half = _gm_fwd_call(p, vgm_g, vgm_b, ws, bsb, gm_out_g, "gmlp_fwd", deps=(gb2.token,))
    y, hs = _lru_fwd_call(p, lcw, lru_conv_b, wa, ba, wx, bx, lru_lambda, lru_out_g, y_half, "lru_fwd")
    _, (wout_g,) = _exchange_wait(gb2, after=(y,))
    wout_full = wout_g.reshape(D, D)
    x2 = _mm_out_call(xs, y, wout_full, "out_proj")
    h2 = _rmsnorm_call(x2, norm2_g, "norm2")
    _, lc = _exchange_wait(gc1, after=(h2,))
    gc2 = gather_forward(lc, "gather_up")
    gd1 = gather_start([(ffn_w_down[0], BF16)], "gather_down", after=(gc2.token,))
    _, (wup_g,) = _exchange_wait(gc2, after=(gd1.token,))
    up3, upc3, f = _ffn_up_act_call(h2, wup_g, fcw, fcb, "ffn_up")
    _, ld = _exchange_wait(gd1, after=(f,))
    gd2 = gather_forward(ld, "gather_down")
    _, (wdown_g,) = _exchange_wait(gd2)
    wdown_full = wdown_g.reshape(Fd, D)
    dx3, dx3b, loss_acc, d_final = _mm_down_loss_call(x2, f, wdown_full, fing, tgt, "ffn_down_loss")

    g_wdown = _mm_tn_rows_call(f, dx3b, "ffn_down_dw").reshape((N_DEV,) + ffn_w_down.shape[1:])
    pd = pair_start(g_wdown, "pair_down")
    d_up3, dfcw_g, dfcw_v, dfcb_g, dfcb_v = _ffn_down_dx_act_bwd_call(dx3b, wdown_full, up3, upc3, fcw, "ffn_down_dx",
                                                                     deps=(pd.token,))
    (g_wdown,), (r1,) = _exchange_wait(pd, after=(d_up3,))
    own_down, p16 = _pair_add_call(g_wdown, r1, core_chip, "pair_add_down")
    cd = chip_start(p16, "chip_down")
    g_wup = _mm_tn_cols_call(h2, d_up3, N_DEV, ffn_w_up.shape[2], "ffn_up_dw", deps=(cd.token,))
    pu = pair_start(g_wup, "pair_up")
    dx2, dx2b, d_norm2 = _mm_dx_norm_call(d_up3, wup_g, dx3, x2, norm2_g, "ffn_up_dx", deps=(pu.token,))
    g_wout = _mm_tn_rows_call(y, dx2b, "out_proj_dw").reshape((N_DEV,) + w_out.shape[1:])
    po = pair_start(g_wout, "pair_out")
    d_y = _mm_nt_call(dx2b, wout_full, F32, "out_proj_dx", deps=(po.token,))
    (g_wup,), (r1,) = _exchange_wait(pu, after=(d_y,))
    own_up, p16 = _pair_add_call(g_wup, r1, core_chip, "pair_add_up")
    _, (r2_down,) = _exchange_wait(cd, after=(p16,))
    cu = chip_start(p16, "chip_up", after=(r2_down,))
    dp_half, d_vg, d_vb, d_ws, d_bs, d_gog = _gm_bwd_call(p, d_y, vgm_g, vgm_b, ws, bsb, gm_out_g, "gmlp_bwd",
                                                          deps=(cu.token,))
    d_p2, d_lcw, d_lcb, d_wa, d_ba, d_wx, d_bx, d_lam, d_log = _lru_bwd_call(
        p, hs, d_y, lcw, lru_conv_b, wa, ba, wx, bx, lru_lambda, lru_out_g, dp_half, "lru_bwd")
    d_p = d_p2[None]
    (g_wout,), (r1,) = _exchange_wait(po, after=(d_p,))
    own_out, p16_out = _pair_add_call(g_wout, r1, core_chip, "pair_add_out")
    g_win = _mm_tn_cols_call(h1, d_p, N_DEV, w_in.shape[2], "in_proj_dw")
    pi = pair_start(g_win, "pair_in")
    _, (r2_up,) = _exchange_wait(cu, after=(g_win,))
    co = chip_start(p16_out, "chip_out", after=(r2_up,))
    gx_a, dn_a = _mm_nt_norm_call(d_p[0], win_rows, dx2, xs, norm1_g, "in_proj_dx_a", deps=(co.token, pi.token),
                                  part=(0, 2))
    (g_win,), (r1,) = _exchange_wait(pi, after=(gx_a,))
    own_in, p16 = _pair_add_call(g_win, r1, core_chip, "pair_add_in")
    _, (r2_out,) = _exchange_wait(co, after=(p16,))
    ci_ = chip_start(p16, "chip_in", after=(r2_out,))
    grad_x, dn_b = _mm_nt_norm_call(d_p[0], win_rows, dx2, xs, norm1_g, "in_proj_dx_b", deps=(ci_.token,),
                                    part=(1, 2), dx_so_far=gx_a)

    small_g = dict(norm1_g=dn_a + dn_b, gm_v_g=d_vg, gm_v_b=d_vb, gm_ws=d_ws, gm_bs=d_bs[:, :, 0], lru_conv_b=d_lcb,
                   lru_wa=d_wa, lru_ba=d_ba, lru_wx=d_wx, lru_bx=d_bx, lru_lambda=d_lam, gm_out_g=d_gog,
                   lru_out_g=d_log, norm2_g=d_norm2,
                   ffn_conv_b=jnp.concatenate([dfcb_g, dfcb_v], axis=1), final_g=d_final)
    rep = _pack([small_g[n] for n in SMALL])
    conv_part = _pack([d_lcw, jnp.concatenate([dfcw_g, dfcw_v], axis=1)], pad_to=8)
    n_rep, n_conv = rep.shape[0], conv_part.shape[0]
    gs1 = gather_start([(jnp.concatenate([rep, conv_part], axis=0), F32)], "gather_small")

    def adamw_big(n, own, r2, deps=()):
        return _adamw_call(wts[n][0], mom[n][0], var[n][0], [(own, None), (r2, 0), (r2, 1), (r2, 2)], "adamw_" + n, deps)

    res = {}
    res["ffn_w_down"] = adamw_big("ffn_w_down", own_down, r2_down, (gs1.token,))
    res["ffn_w_up"] = adamw_big("ffn_w_up", own_up, r2_up, (gs1.token,))
    res["w_out"] = adamw_big("w_out", own_out, r2_out, (gs1.token,))
    _, ls = _exchange_wait(gs1, after=(res["w_out"][0], res["ffn_w_up"][0], res["ffn_w_down"][0]))
    gs2 = gather_forward(ls, "gather_small")
    _, (r2_in,) = _exchange_wait(ci_, after=(gs2.token,))
    res["w_in"] = adamw_big("w_in", own_in, r2_in)
    _, (parts,) = _exchange_wait(gs2, after=(res["w_in"][0],))
    g_rep, d_rep, m_rep, v_rep = _adamw_call(
        _pack([wts[n] for n in SMALL]), _pack([mom[n] for n in SMALL]), _pack([var[n] for n in SMALL]),
        [(parts, k) for k in range(N_DEV)], "adamw_small")
    shapes = [wts[n].shape for n in SMALL]
    for n, g_, d_, m_, v_ in zip(SMALL, _unpack(g_rep, shapes), _unpack(d_rep, shapes), _unpack(m_rep, shapes),
                                 _unpack(v_rep, shapes)):
        res[n] = (g_, d_, m_, v_)
    conv_sum = _sum_call(parts, n_rep, n_conv, "sum_conv_grads")
    g_lcw = conv_sum[:4 * W // LANES].reshape(4, W)
    g_fcw = conv_sum[4 * W // LANES:4 * W // LANES + 6 * Fd // LANES].reshape(3, 2 * Fd)
    for n, full in (("lru_conv_w", g_lcw), ("ffn_conv_w", g_fcw)):
        cols = wts[n].shape[2]
        mine = lax.dynamic_slice_in_dim(full, dev * cols, cols, axis=1)
        res[n] = _adamw_call(wts[n][0], mom[n][0], var[n][0], [(mine, None)], "adamw_" + n)

    loss = lax.psum(loss_acc[0, 0], ("x", "y", "c"))
    outs = [[], [], [], []]
    for n in WEIGHTS:
        for k in range(4):
            outs[k].append(res[n][k].reshape(wts[n].shape))
    return (loss, grad_x[None], *outs[0], *outs[1], *outs[2], *outs[3])
```

```python
import functools
import math

import jax
import jax.numpy as jnp
from jax import lax
from jax.experimental import pallas as pl
from jax.experimental.pallas import tpu as pltpu

F32 = jnp.float32
BF16 = jnp.bfloat16

RMS_EPS = 1e-6
LN_EPS = 1e-5
LRU_C = 8.0
CHUNK = 128
ADAM_LR = 0.001
ADAM_B1 = 0.9
ADAM_B2 = 0.999
ADAM_EPS = 1e-08
ADAM_WD = 0.01
ADAM_STEP = 10
N_DEV = 8
LANES = 128
MIB = 1024 * 1024

WEIGHTS = ['norm1_g', 'w_in', 'gm_v_g', 'gm_v_b', 'gm_ws', 'gm_bs', 'lru_conv_w', 'lru_conv_b', 'lru_wa', 'lru_ba',
           'lru_wx', 'lru_bx', 'lru_lambda', 'gm_out_g', 'lru_out_g', 'w_out', 'norm2_g', 'ffn_w_up', 'ffn_conv_w',
           'ffn_conv_b', 'ffn_w_down', 'final_g']
BIG = ['w_in', 'w_out', 'ffn_w_up', 'ffn_w_down']
CONV = ['lru_conv_w', 'ffn_conv_w']
SMALL = [n for n in WEIGHTS if n not in BIG and n not in CONV]

_DN_NT = (((1,), (1,)), ((), ()))
_DN_TN = (((0,), (0,)), ((), ()))
_GELU_C = 0.7978845608028654


def _cp(n_axes, vmem_mib=48):
    return pltpu.CompilerParams(dimension_semantics=("arbitrary",) * n_axes, vmem_limit_bytes=vmem_mib * MIB)


def _tile(n, pref, mult=8):
    t = min(pref, n)
    t -= t % mult
    while t >= mult:
        if n % t == 0:
            return t
        t -= mult
    return n


def _gelu_gate(z, z2):
    return 0.5 * jnp.tanh(z * ((_GELU_C * 0.044715) * z2 + _GELU_C)) + 0.5


def _gelu(z):
    return z * _gelu_gate(z, z * z)


def _gelu_parts(z):
    z2 = z * z
    s = _gelu_gate(z, z2)
    g = z * s
    dg = s + g * (1.0 - s) * ((6.0 * _GELU_C * 0.044715) * z2 + 2.0 * _GELU_C)
    return g, dg


def _sigmoid(z):
    return 0.5 + 0.5 * jnp.tanh(0.5 * z)


def _softplus(z):
    t = jnp.exp(-jnp.abs(z))
    u = 1.0 + t
    log1p = jnp.where(u == 1.0, t, jnp.log(u) * t / (u - 1.0))
    return jnp.maximum(z, 0.0) + log1p


def _rows_mean(v):
    return jnp.mean(v, axis=-1, keepdims=True)


def _col_sum(v):
    return jnp.sum(v, axis=0, keepdims=True)


def _shift_down(prev8, cur, k):
    if k == 0:
        return cur
    z = jnp.concatenate([prev8, cur], axis=0)
    return pltpu.roll(z, k, 0)[8:]


def _shift_up(cur, next8, k):
    if k == 0:
        return cur
    n = cur.shape[0]
    z = jnp.concatenate([cur, next8], axis=0)
    return pltpu.roll(z, n + 8 - k, 0)[:n]


def _mesh_pos():
    return lax.axis_index("x"), lax.axis_index("y"), lax.axis_index("c")


def _any_specs(n):
    return [pl.BlockSpec(memory_space=pl.ANY)] * n


def _pallas(body, n_in, deps, **kw):
    nd = len(deps)
    if not nd:
        return pl.pallas_call(body, **kw)

    def ordered(*refs):
        body(*refs[:n_in], *refs[n_in + nd:])

    kw["in_specs"] = list(kw["in_specs"]) + _any_specs(nd)
    return pl.pallas_call(ordered, **kw)


_HBM = pl.BlockSpec(memory_space=pltpu.HBM)
_SEM = pl.BlockSpec(memory_space=pltpu.SEMAPHORE)
_EFFECT = pltpu.SideEffectType.DATAFLOW_SIDE_EFFECTING


class _InFlight:
    def __init__(self, sems, bufs, token, n_src, n_copies, make_copies, name):
        self.sems, self.bufs, self.token = sems, bufs, token
        self.n_src, self.n_copies, self.make_copies, self.name = n_src, n_copies, make_copies, name


def _exchange_start(srcs, lands, n_copies, make_copies, name, after=()):
    bufs = list(srcs) + list(lands)
    nb, na = len(bufs), len(after)
    ns = len(srcs)

    def body(*refs):
        b_refs = refs[:nb]
        outs = refs[nb + na:]
        send, recv = outs[:n_copies], outs[n_copies:2 * n_copies]
        token = outs[-1]
        for cp in make_copies(b_refs[:ns], b_refs[ns:], send, recv):
            cp.start()
        token[...] = jnp.zeros_like(token)

    out = pl.pallas_call(
        body, name=name,
        out_shape=[pltpu.SemaphoreType.DMA(())] * (2 * n_copies) + [pltpu.HBM(b.shape, b.dtype) for b in bufs]
        + [jax.ShapeDtypeStruct((8, LANES), F32)],
        in_specs=[_HBM] * nb + _any_specs(na),
        out_specs=[_SEM] * (2 * n_copies) + [_HBM] * nb + [pl.BlockSpec(memory_space=pltpu.VMEM)],
        input_output_aliases={i: 2 * n_copies + i for i in range(nb)},
        compiler_params=pltpu.CompilerParams(has_side_effects=_EFFECT),
    )(*[pltpu.with_memory_space_constraint(b, pltpu.HBM) for b in bufs], *after)
    return _InFlight(out[:2 * n_copies], out[2 * n_copies:2 * n_copies + nb], out[-1], ns, n_copies, make_copies, name)


def _exchange_wait(fl, after=()):
    nb, na, nc, ns = len(fl.bufs), len(after), fl.n_copies, fl.n_src

    def body(*refs):
        b_refs = refs[:nb]
        sems = refs[nb:nb + 2 * nc]
        copies = fl.make_copies(b_refs[:ns], b_refs[ns:], sems[:nc], sems[nc:])
        for cp in copies:
            cp.wait_send()
        for cp in copies:
            cp.wait_recv()

    out = pl.pallas_call(
        body, name=fl.name + "_wait",
        out_shape=[pltpu.HBM(b.shape, b.dtype) for b in fl.bufs],
        in_specs=[_HBM] * nb + [_SEM] * (2 * nc) + _any_specs(na),
        out_specs=[_HBM] * nb,
        input_output_aliases={i: i for i in range(nb)},
        compiler_params=pltpu.CompilerParams(has_side_effects=_EFFECT),
    )(*fl.bufs, *fl.sems, *after)
    return list(out[:ns]), list(out[ns:])


def _remote(src, dst, send_sem, recv_sem, to):
    return pltpu.make_async_remote_copy(src_ref=src, dst_ref=dst, send_sem=send_sem, recv_sem=recv_sem,
                                        device_id=to, device_id_type=pl.DeviceIdType.MESH)


def _gather_stage1_copies(n, to_sibling=True, to_chips=True):
    def make(s_refs, l_refs, send, recv):
        x, y, c = _mesh_pos()
        own = 4 * x + 2 * y + c
        targets = ([(x, y, 1 - c)] if to_sibling else []) + (
            [(1 - x, y, c), (x, 1 - y, c), (1 - x, 1 - y, c)] if to_chips else [])
        m = len(targets)
        return [_remote(l_refs[a].at[own], l_refs[a].at[own], send[m * a + k], recv[m * a + k], to)
                for a in range(n) for k, to in enumerate(targets)]
    return make


def _gather_stage2_copies(n):
    def make(s_refs, l_refs, send, recv):
        x, y, c = _mesh_pos()
        blocks = [4 * (1 - x) + 2 * y + c, 4 * x + 2 * (1 - y) + c, 4 * (1 - x) + 2 * (1 - y) + c]
        return [_remote(l_refs[a].at[b], l_refs[a].at[b], send[3 * a + j], recv[3 * a + j], (x, y, 1 - c))
                for a in range(n) for j, b in enumerate(blocks)]
    return make


def _pair_copies(n):
    def make(s_refs, l_refs, send, recv):
        x, y, c = _mesh_pos()
        return [_remote(s_refs[a].at[2 * k + 1 - c], l_refs[a].at[k], send[4 * a + k], recv[4 * a + k], (x, y, 1 - c))
                for a in range(n) for k in range(4)]
    return make


def _chip_copies(n):
    def make(s_refs, l_refs, send, recv):
        x, y, c = _mesh_pos()
        chips = [(1 - x, y), (x, 1 - y), (1 - x, 1 - y)]
        return [_remote(s_refs[a].at[2 * ch[0] + ch[1]], l_refs[a].at[j], send[3 * a + j], recv[3 * a + j], (*ch, c))
                for a in range(n) for j, ch in enumerate(chips)]
    return make


def _place_own_call(shard, dev, dtype, name):
    R, C = shard.shape
    tr = _tile(R, max(16, MIB // (C * 4)), 16)

    def body(d_ref, s_ref, o_ref):
        o_ref[...] = s_ref[...].astype(dtype)

    grid_spec = pltpu.PrefetchScalarGridSpec(
        num_scalar_prefetch=1, grid=(R // tr,),
        in_specs=[pl.BlockSpec((tr, C), lambda r, d: (r, 0))],
        out_specs=pl.BlockSpec((None, tr, C), lambda r, d: (d[0], r, 0)))
    return pl.pallas_call(body, name=name, grid_spec=grid_spec,
                          out_shape=jax.ShapeDtypeStruct((N_DEV, R, C), dtype), compiler_params=_cp(1))(dev, shard)


def _pair_add_call(g, r1, core_chip, name):
    _, R, C = g.shape
    tr = _tile(R, max(16, (2 * MIB) // (C * 4)), 16)

    def body(cc_ref, g_ref, r_ref, p32_ref, p16_ref):
        s = g_ref[...] + r_ref[...]
        p16_ref[...] = s.astype(BF16)

        @pl.when(pl.program_id(1) == cc_ref[1])
        def _():
            p32_ref[...] = s

    grid_spec = pltpu.PrefetchScalarGridSpec(
        num_scalar_prefetch=1, grid=(R // tr, 4),
        in_specs=[pl.BlockSpec((None, tr, C), lambda r, k, cc: (2 * k + cc[0], r, 0)),
                  pl.BlockSpec((None, tr, C), lambda r, k, cc: (k, r, 0))],
        out_specs=[pl.BlockSpec((tr, C), lambda r, k, cc: (r, 0)),
                   pl.BlockSpec((None, tr, C), lambda r, k, cc: (k, r, 0))])
    return pl.pallas_call(
        body, name=name, grid_spec=grid_spec,
        out_shape=[jax.ShapeDtypeStruct((R, C), F32), jax.ShapeDtypeStruct((4, R, C), BF16)],
        compiler_params=_cp(2))(core_chip, g, r1)


def _adamw_call(w, m, v, addends, name, deps=()):
    R, C = w.shape
    tr = _tile(R, max(8, (MIB // 2) // (C * 4)), 16)
    na = len(addends)
    c1 = 1.0 - ADAM_B1 ** ADAM_STEP
    c2 = 1.0 - ADAM_B2 ** ADAM_STEP

    def body(*refs):
        w_ref, m_ref, v_ref = refs[:3]
        a_refs = refs[3:3 + na]
        g_ref, d_ref, nm_ref, nv_ref = refs[3 + na:]
        g = a_refs[0][...].astype(F32)
        for a_ref in a_refs[1:]:
            g = g + a_ref[...].astype(F32)
        nm = ADAM_B1 * m_ref[...] + (1.0 - ADAM_B1) * g
        nv = ADAM_B2 * v_ref[...] + (1.0 - ADAM_B2) * (g * g)
        g_ref[...] = g
        nm_ref[...] = nm
        nv_ref[...] = nv
        d_ref[...] = -ADAM_LR * ((nm / c1) / (jnp.sqrt(nv / c2) + ADAM_EPS) + ADAM_WD * w_ref[...])

    flat = pl.BlockSpec((tr, C), lambda r: (r, 0))
    a_specs = [flat if k is None else pl.BlockSpec((None, tr, C), functools.partial(lambda r, kk: (kk, r, 0), kk=k))
               for _, k in addends]
    out = jax.ShapeDtypeStruct((R, C), F32)
    return _pallas(
        body, 3 + na, deps, name=name, grid=(R // tr,),
        in_specs=[flat, flat, flat] + a_specs, out_specs=[flat] * 4, out_shape=[out] * 4,
        compiler_params=_cp(1))(w, m, v, *[a for a, _ in addends], *deps)


def _sum_call(parts, row0, rows, name):
    n = parts.shape[0]
    tr = _tile(math.gcd(row0, rows), 256, 8)
    b0 = row0 // tr

    def body(p_ref, o_ref):
        s = p_ref[0]
        for k in range(1, n):
            s = s + p_ref[k]
        o_ref[...] = s

    return pl.pallas_call(
        body, name=name, grid=(rows // tr,),
        in_specs=[pl.BlockSpec((n, tr, LANES), lambda r: (0, r + b0, 0))],
        out_specs=pl.BlockSpec((tr, LANES), lambda r: (r, 0)),
        out_shape=jax.ShapeDtypeStruct((rows, LANES), F32), compiler_params=_cp(1))(parts)


def _rmsnorm_call(x, g, name, deps=()):
    S, D = x.shape
    tm = _tile(S, 512, 16)

    def body(x_ref, g_ref, o_ref):
        xv = x_ref[...]
        r = lax.rsqrt(_rows_mean(xv * xv) + RMS_EPS)
        o_ref[...] = (xv * r * g_ref[...]).astype(BF16)

    return _pallas(
        body, 2, deps, name=name, grid=(S // tm,),
        in_specs=[pl.BlockSpec((tm, D), lambda i: (i, 0)), pl.BlockSpec((1, D), lambda i: (0, 0))],
        out_specs=pl.BlockSpec((tm, D), lambda i: (i, 0)),
        out_shape=jax.ShapeDtypeStruct((S, D), BF16), compiler_params=_cp(1))(x, g, *deps)


def _mm_some_blocks_call(a, wg, blocks, out_dtype, name, out_so_far=None):
    S, K = a.shape
    nb, _, bn = wg.shape
    tm = _tile(S, 1024, 16)

    def body(b_ref, a_ref, w_ref, *rest):
        rest[-1][...] = jnp.dot(a_ref[...], w_ref[...], preferred_element_type=F32).astype(out_dtype)

    in_specs = [pl.BlockSpec((tm, K), lambda i, j, b: (i, 0)), pl.BlockSpec((None, K, bn), lambda i, j, b: (b[j], 0, 0))]
    operands = [a, wg]
    aliases = {}
    if out_so_far is not None:
        in_specs.append(pl.BlockSpec(memory_space=pl.ANY))
        operands.append(out_so_far)
        aliases = {3: 0}
    grid_spec = pltpu.PrefetchScalarGridSpec(
        num_scalar_prefetch=1, grid=(S // tm, blocks.shape[0]), in_specs=in_specs,
        out_specs=pl.BlockSpec((tm, bn), lambda i, j, b: (i, b[j])))
    return pl.pallas_call(body, name=name, grid_spec=grid_spec,
                          out_shape=jax.ShapeDtypeStruct((S, nb * bn), out_dtype), input_output_aliases=aliases,
                          compiler_params=_cp(2))(blocks, *operands)


def _mm_out_call(x, y, w, name):
    S, D = x.shape
    tm = _tile(S, 512, 16)

    def body(x_ref, y_ref, w_ref, o_ref):
        o_ref[...] = x_ref[...] + jnp.dot(y_ref[...], w_ref[...], preferred_element_type=F32)

    return pl.pallas_call(
        body, name=name, grid=(S // tm,),
        in_specs=[pl.BlockSpec((tm, D), lambda i: (i, 0)), pl.BlockSpec((tm, D), lambda i: (i, 0)),
                  pl.BlockSpec((D, D), lambda i: (0, 0))],
        out_specs=pl.BlockSpec((tm, D), lambda i: (i, 0)),
        out_shape=jax.ShapeDtypeStruct((S, D), F32), compiler_params=_cp(1))(x, y, w)


def _mm_nt_call(a, w, out_dtype, name, deps=()):
    S, K = a.shape
    N = w.shape[0]
    tm = _tile(S, 1024, 16)
    tn = _tile(N, 768, LANES)

    def body(a_ref, w_ref, o_ref):
        o_ref[...] = lax.dot_general(a_ref[...], w_ref[...], _DN_NT, preferred_element_type=F32).astype(out_dtype)

    return _pallas(
        body, 2, deps, name=name, grid=(S // tm, N // tn),
        in_specs=[pl.BlockSpec((tm, K), lambda i, j: (i, 0)), pl.BlockSpec((tn, K), lambda i, j: (j, 0))],
        out_specs=pl.BlockSpec((tm, tn), lambda i, j: (i, j)),
        out_shape=jax.ShapeDtypeStruct((S, N), out_dtype), compiler_params=_cp(2))(a, w, *deps)


def _mm_down_loss_call(x2, f, w, final_g, target, name):
    S, D = x2.shape
    Fd = f.shape[1]
    tm = _tile(S, 512, 16)
    tk = _tile(Fd, 768, LANES)
    nk = Fd // tk

    def body(x_ref, f_ref, w_ref, g_ref, t_ref, dx_ref, dxb_ref, loss_ref, dg_ref, acc):
        i, k = pl.program_id(0), pl.program_id(1)

        @pl.when(jnp.logical_and(i == 0, k == 0))
        def _():
            loss_ref[...] = jnp.zeros_like(loss_ref)
            dg_ref[...] = jnp.zeros_like(dg_ref)

        @pl.when(k == 0)
        def _():
            acc[...] = jnp.zeros_like(acc)

        acc[...] += jnp.dot(f_ref[...], w_ref[...], preferred_element_type=F32)

        @pl.when(k == nk - 1)
        def _():
            x3 = x_ref[...] + acc[...]
            r = lax.rsqrt(_rows_mean(x3 * x3) + RMS_EPS)
            g = g_ref[...]
            xn = x3 * r
            diff = xn * g - t_ref[...]
            loss_ref[...] += 0.5 * jnp.sum(_rows_mean(diff * diff))
            dout = diff * (1.0 / D)
            dg_ref[...] += _col_sum(dout * xn)
            dyg = dout * g
            dx = r * (dyg - xn * _rows_mean(dyg * xn))
            dx_ref[...] = dx
            dxb_ref[...] = dx.astype(BF16)

    row = lambda i, k: (i, 0)
    return pl.pallas_call(
        body, name=name, grid=(S // tm, nk),
        in_specs=[pl.BlockSpec((tm, D), row), pl.BlockSpec((tm, tk), lambda i, k: (i, k)),
                  pl.BlockSpec((tk, D), lambda i, k: (k, 0)), pl.BlockSpec((1, D), lambda i, k: (0, 0)),
                  pl.BlockSpec((tm, D), row)],
        out_specs=[pl.BlockSpec((tm, D), row), pl.BlockSpec((tm, D), row),
                   pl.BlockSpec((8, LANES), lambda i, k: (0, 0)), pl.BlockSpec((1, D), lambda i, k: (0, 0))],
        out_shape=[jax.ShapeDtypeStruct((S, D), F32), jax.ShapeDtypeStruct((S, D), BF16),
                   jax.ShapeDtypeStruct((8, LANES), F32), jax.ShapeDtypeStruct((1, D), F32)],
        scratch_shapes=[pltpu.VMEM((tm, D), F32)], compiler_params=_cp(2, 56))(x2, f, w, final_g, target)


def _mm_dx_norm_call(a3, wg, resid, xin, g, name, deps=()):
    na, S, Fa = a3.shape
    nb, D, bn = wg.shape
    tm = _tile(S, 512, 16)
    tk = _tile(bn, 1536, LANES)
    nsub = bn // tk
    nka = Fa // tk
    nk = nb * nsub
    assert na * nka == nk

    def body(a_ref, w_ref, r_ref, x_ref, g_ref, dx_ref, dxb_ref, dg_ref, acc):
        i, k = pl.program_id(0), pl.program_id(1)

        @pl.when(jnp.logical_and(i == 0, k == 0))
        def _():
            dg_ref[...] = jnp.zeros_like(dg_ref)

        @pl.when(k == 0)
        def _():
            acc[...] = jnp.zeros_like(acc)

        acc[...] += lax.dot_general(a_ref[...], w_ref[...], _DN_NT, preferred_element_type=F32)

        @pl.when(k == nk - 1)
        def _():
            dh = acc[...]
            xv = x_ref[...]
            r = lax.rsqrt(_rows_mean(xv * xv) + RMS_EPS)
            xn = xv * r
            dg_ref[...] += _col_sum(dh * xn)
            dyg = dh * g_ref[...]
            dx = r_ref[...] + r * (dyg - xn * _rows_mean(dyg * xn))
            dx_ref[...] = dx
            dxb_ref[...] = dx.astype(BF16)

    row = lambda i, k: (i, 0)
    return _pallas(
        body, 5, deps, name=name, grid=(S // tm, nk),
        in_specs=[pl.BlockSpec((None, tm, tk), lambda i, k: (k // nka, i, k % nka)),
                  pl.BlockSpec((None, D, tk), lambda i, k: (k // nsub, 0, k % nsub)),
                  pl.BlockSpec((tm, D), row, pipeline_mode=pl.Buffered(1)),
                  pl.BlockSpec((tm, D), row, pipeline_mode=pl.Buffered(1)), pl.BlockSpec((1, D), lambda i, k: (0, 0))],
        out_specs=[pl.BlockSpec((tm, D), row), pl.BlockSpec((tm, D), row), pl.BlockSpec((1, D), lambda i, k: (0, 0))],
        out_shape=[jax.ShapeDtypeStruct((S, D), F32), jax.ShapeDtypeStruct((S, D), BF16),
                   jax.ShapeDtypeStruct((1, D), F32)],
        scratch_shapes=[pltpu.VMEM((tm, D), F32)], compiler_params=_cp(2, 56))(a3, wg, resid, xin, g, *deps)


def _unblock_call(wg, name):
    nb, K, bn = wg.shape

    def body(w_ref, o_ref):
        o_ref[...] = w_ref[...]

    return pl.pallas_call(
        body, name=name, grid=(nb,),
        in_specs=[pl.BlockSpec((None, K, bn), lambda o: (o, 0, 0))],
        out_specs=pl.BlockSpec((K, bn), lambda o: (0, o)),
        out_shape=jax.ShapeDtypeStruct((K, nb * bn), wg.dtype), compiler_params=_cp(1))(wg)


def _mm_nt_norm_call(a, w, resid, xin, g, name, deps=(), part=(0, 1), dx_so_far=None):
    S, K = a.shape
    D = w.shape[0]
    tm = _tile(S, 256, 16)
    tiles = (S // tm) // part[1]
    first = part[0] * tiles

    def body(a_ref, w_ref, r_ref, x_ref, g_ref, *rest):
        dx_ref, dg_ref = rest[-2:]

        @pl.when(pl.program_id(0) == 0)
        def _():
            dg_ref[...] = jnp.zeros_like(dg_ref)

        dh = lax.dot_general(a_ref[...], w_ref[...], _DN_NT, preferred_element_type=F32)
        xv = x_ref[...]
        r = lax.rsqrt(_rows_mean(xv * xv) + RMS_EPS)
        xn = xv * r
        dg_ref[...] += _col_sum(dh * xn)
        dyg = dh * g_ref[...]
        dx_ref[...] = r_ref[...] + r * (dyg - xn * _rows_mean(dyg * xn))

    row = lambda i: (i + first, 0)
    fixed = lambda i: (0, 0)
    in_specs = [pl.BlockSpec((tm, K), row), pl.BlockSpec((D, K), fixed, pipeline_mode=pl.Buffered(1)),
                pl.BlockSpec((tm, D), row), pl.BlockSpec((tm, D), row), pl.BlockSpec((1, D), fixed)]
    operands = [a, w, resid, xin, g]
    aliases = {}
    if dx_so_far is not None:
        in_specs.append(pl.BlockSpec(memory_space=pl.ANY))
        operands.append(dx_so_far)
        aliases = {5: 0}
    return _pallas(
        body, len(operands), deps, name=name, grid=(tiles,),
        in_specs=in_specs, out_specs=[pl.BlockSpec((tm, D), row), pl.BlockSpec((1, D), fixed)],
        out_shape=[jax.ShapeDtypeStruct((S, D), F32), jax.ShapeDtypeStruct((1, D), F32)],
        input_output_aliases=aliases, compiler_params=_cp(1, 56))(*operands, *deps)


def _mm_tn_cols_call(a, b3, nb, bn, name, deps=()):
    S, Ka = a.shape
    nh, _, Fb = b3.shape
    tm = _tile(S, 2048, 16)
    tn = _tile(bn, 768, LANES)
    nsub = bn // tn
    njb = Fb // tn
    J = nb * nsub
    assert nh * njb == J

    def body(a_ref, b_ref, o_ref):
        @pl.when(pl.program_id(1) == 0)
        def _():
            o_ref[...] = jnp.zeros_like(o_ref)

        o_ref[...] += lax.dot_general(a_ref[...], b_ref[...], _DN_TN, preferred_element_type=F32)

    return _pallas(
        body, 2, deps, name=name, grid=(J, S // tm),
        in_specs=[pl.BlockSpec((tm, Ka), lambda j, i: (i, 0)),
                  pl.BlockSpec((None, tm, tn), lambda j, i: (j // njb, i, j % njb))],
        out_specs=pl.BlockSpec((None, Ka, tn), lambda j, i: (j // nsub, 0, j % nsub)),
        out_shape=jax.ShapeDtypeStruct((nb, Ka, bn), F32), compiler_params=_cp(2, 56))(a, b3, *deps)


def _mm_tn_rows_call(a, b, name, deps=()):
    S, E = a.shape
    D = b.shape[1]
    tm = _tile(S, 2048, 16)
    te = _tile(E, 768, LANES)

    def body(a_ref, b_ref, o_ref):
        @pl.when(pl.program_id(1) == 0)
        def _():
            o_ref[...] = jnp.zeros_like(o_ref)

        o_ref[...] += lax.dot_general(a_ref[...], b_ref[...], _DN_TN, preferred_element_type=F32)

    return _pallas(
        body, 2, deps, name=name, grid=(E // te, S // tm),
        in_specs=[pl.BlockSpec((tm, te), lambda j, i: (i, j)), pl.BlockSpec((tm, D), lambda j, i: (i, 0))],
        out_specs=pl.BlockSpec((te, D), lambda j, i: (j, 0)),
        out_shape=jax.ShapeDtypeStruct((E, D), F32), compiler_params=_cp(2, 56))(a, b, *deps)


def _ffn_tiles(S, Fd):
    return _tile(S, 512, 16), _tile(Fd // (N_DEV // 2), 1536, LANES)


def _taps(cw_ref):
    return [cw_ref[k:k + 1, :] for k in range(cw_ref.shape[0])]


def _conv3(prev8, cur, taps):
    s1 = _shift_down(prev8, cur, 1)
    s2 = _shift_down(prev8, cur, 2)
    return taps[2] * cur + taps[1] * s1 + taps[0] * s2, s1, s2


GATE_LANES = 256
SUB_LANES = 256


def _lane_taps(cw_ref, ls):
    return [cw_ref[k:k + 1, ls] for k in range(cw_ref.shape[0])]


def _ffn_up_act_call(h2, wg, cw, cb, name, deps=()):
    S, D = h2.shape
    nb, _, bn = wg.shape
    Fd = nb * bn // 2
    tm, tc = _ffn_tiles(S, Fd)
    nk = Fd // tc
    hb = tm // 16
    nsubw = bn // tc
    half = nb // 2
    sc = _tile(tc, SUB_LANES, LANES)

    def body(a_ref, ap_ref, wgate_ref, wval_ref, cwg_ref, cwv_ref, cbg_ref, cbv_ref, up_ref, upc_ref, f_ref):
        keep = jnp.where(pl.program_id(0) == 0, 0.0, 1.0)
        a_ext = jnp.concatenate([ap_ref[...], a_ref[...]], axis=0)
        nsub = tc // sc
        lanes = [slice(s * sc, (s + 1) * sc) for s in range(nsub)]

        def products(s):
            return [jnp.dot(a_ext, w_ref[:, lanes[s]], preferred_element_type=F32).astype(BF16)
                    for w_ref in (wgate_ref, wval_ref)]

        ready = products(0)
        for s in range(nsub):
            ls = lanes[s]
            following = products(s + 1) if s + 1 < nsub else None

            def conv_half(ub, cw_ref, cb_ref, slab):
                up_ref[slab, :, ls] = ub[16:]
                u = ub.astype(F32)
                conv, _, _ = _conv3(u[8:16] * keep, u[16:], _lane_taps(cw_ref, ls))
                c = conv + cb_ref[:, ls]
                upc_ref[slab, :, ls] = c.astype(BF16)
                return c

            cg = conv_half(ready[0], cwg_ref, cbg_ref, 0)
            cv = conv_half(ready[1], cwv_ref, cbv_ref, 1)
            f_ref[:, ls] = (_gelu(cg) * cv).astype(BF16)
            ready = following

    return _pallas(
        body, 8, deps, name=name, grid=(S // tm, nk),
        in_specs=[pl.BlockSpec((tm, D), lambda i, k: (i, 0)),
                  pl.BlockSpec((16, D), lambda i, k: (jnp.maximum(i * hb - 1, 0), 0)),
                  pl.BlockSpec((None, D, tc), lambda i, k: (k // nsubw, 0, k % nsubw)),
                  pl.BlockSpec((None, D, tc), lambda i, k: (half + k // nsubw, 0, k % nsubw)),
                  pl.BlockSpec((3, tc), lambda i, k: (0, k)), pl.BlockSpec((3, tc), lambda i, k: (0, k + nk)),
                  pl.BlockSpec((1, tc), lambda i, k: (0, k)), pl.BlockSpec((1, tc), lambda i, k: (0, k + nk))],
        out_specs=[pl.BlockSpec((2, tm, tc), lambda i, k: (0, i, k)), pl.BlockSpec((2, tm, tc), lambda i, k: (0, i, k)),
                   pl.BlockSpec((tm, tc), lambda i, k: (i, k))],
        out_shape=[jax.ShapeDtypeStruct((2, S, Fd), BF16), jax.ShapeDtypeStruct((2, S, Fd), BF16),
                   jax.ShapeDtypeStruct((S, Fd), BF16)],
        compiler_params=_cp(2, 56))(h2, h2, wg, wg, cw, cw, cb, cb, *deps)


def _ffn_down_dx_act_bwd_call(dxb, w, up3, upc3, cw, name, deps=()):
    S, D = dxb.shape
    _, _, Fd = up3.shape
    tm, tc = _ffn_tiles(S, Fd)
    nj = Fd // tc
    ni = S // tm
    hb = tm // 16
    sc = _tile(tc, SUB_LANES, LANES)

    def body(a_ref, an_ref, w_ref, g_ref, v_ref, cg_ref, cv_ref, cgn_ref, cvn_ref, cwg_ref, cwv_ref,
             dup_ref, dcwg_ref, dcwv_ref, dcbg_ref, dcbv_ref):
        i = pl.program_id(1)
        keep_next = jnp.where(i == ni - 1, 0.0, 1.0)

        @pl.when(i == 0)
        def _():
            for r in (dcwg_ref, dcwv_ref, dcbg_ref, dcbv_ref):
                r[...] = jnp.zeros_like(r)

        a_ext = jnp.concatenate([a_ref[...], an_ref[...]], axis=0)
        for s in range(tc // sc):
            ls = slice(s * sc, (s + 1) * sc)
            df_ext = lax.dot_general(a_ext, w_ref[s * sc:(s + 1) * sc, :], _DN_NT, preferred_element_type=F32)
            df = jnp.concatenate([df_ext[:tm], df_ext[tm:tm + 8] * keep_next], axis=0)
            cg = jnp.concatenate([cg_ref[:, ls].astype(F32), cgn_ref[:, ls].astype(F32)[:8]], axis=0)
            cv = jnp.concatenate([cv_ref[:, ls].astype(F32), cvn_ref[:, ls].astype(F32)[:8]], axis=0)
            gel, dgel = _gelu_parts(cg)

            def back(d, cw_ref, x_ref, dcw_ref, dcb_ref, slab):
                taps = _lane_taps(cw_ref, ls)
                d0 = d[:tm]
                d1 = pltpu.roll(d, tm + 8 - 1, 0)[:tm]
                d2 = pltpu.roll(d, tm + 8 - 2, 0)[:tm]
                dup_ref[slab, :, ls] = (taps[2] * d0 + taps[1] * d1 + taps[0] * d2).astype(BF16)
                xv = x_ref[:, ls].astype(F32)
                dcw_ref[2:3, ls] += _col_sum(xv * d0)
                dcw_ref[1:2, ls] += _col_sum(xv * d1)
                dcw_ref[0:1, ls] += _col_sum(xv * d2)
                dcb_ref[:, ls] += _col_sum(d0)

            back(df * cv * dgel, cwg_ref, g_ref, dcwg_ref, dcbg_ref, 0)
            back(df * gel, cwv_ref, v_ref, dcwv_ref, dcbv_ref, 1)

    nxt = lambda j, i: jnp.minimum((i + 1) * hb, S // 16 - 1)
    main = lambda s: pl.BlockSpec((None, tm, tc), lambda j, i: (s, i, j))
    halo = lambda s: pl.BlockSpec((None, 16, tc), lambda j, i: (s, nxt(j, i), j))
    acc3 = pl.BlockSpec((3, tc), lambda j, i: (0, j))
    acc1 = pl.BlockSpec((1, tc), lambda j, i: (0, j))
    return _pallas(
        body, 11, deps, name=name, grid=(nj, ni),
        in_specs=[pl.BlockSpec((tm, D), lambda j, i: (i, 0)), pl.BlockSpec((16, D), lambda j, i: (nxt(j, i), 0)),
                  pl.BlockSpec((tc, D), lambda j, i: (j, 0)),
                  main(0), main(1), main(0), main(1), halo(0), halo(1),
                  pl.BlockSpec((3, tc), lambda j, i: (0, j)), pl.BlockSpec((3, tc), lambda j, i: (0, j + nj))],
        out_specs=[pl.BlockSpec((2, tm, tc), lambda j, i: (0, i, j)), acc3, acc3, acc1, acc1],
        out_shape=[jax.ShapeDtypeStruct((2, S, Fd), BF16), jax.ShapeDtypeStruct((3, Fd), F32),
                   jax.ShapeDtypeStruct((3, Fd), F32), jax.ShapeDtypeStruct((1, Fd), F32),
                   jax.ShapeDtypeStruct((1, Fd), F32)],
        compiler_params=_cp(2, 56))(dxb, dxb, w, up3, up3, upc3, upc3, upc3, upc3, cw, cw, *deps)


def _gm_forward_tile(pv, vg, vb, ws_ref, bsb_ref, mbuf, H, nc):
    W = H * CHUNK
    z, dz = _gelu_parts(pv)
    u, v0 = z[:, :W], z[:, W:]
    xc = v0 - _rows_mean(v0)
    rs = lax.rsqrt(_rows_mean(xc * xc) + LN_EPS)
    vh = xc * rs
    vnb = (vh * vg + vb).astype(BF16)
    mask = lax.broadcasted_iota(jnp.int32, (CHUNK, CHUNK), 0) >= lax.broadcasted_iota(jnp.int32, (CHUNK, CHUNK), 1)
    for h in range(H):
        cs = slice(h * CHUNK, (h + 1) * CHUNK)
        wm = jnp.where(mask, ws_ref[h], 0.0).astype(BF16)
        vcat = jnp.concatenate([vnb[c * CHUNK:(c + 1) * CHUNK, cs] for c in range(nc)], axis=1)
        mix = jnp.dot(wm, vcat, preferred_element_type=F32)
        for c in range(nc):
            mbuf[c * CHUNK:(c + 1) * CHUNK, cs] = mix[:, c * CHUNK:(c + 1) * CHUNK] + bsb_ref[h]
    return dz, u, vh, rs, vnb, mask


def _gm_fwd_call(p, v_g, v_b, ws, bsb, out_g, name, deps=()):
    S = p.shape[0]
    H = ws.shape[0]
    W = H * CHUNK
    tm = _tile(S, 256, CHUNK)
    nc = tm // CHUNK

    def body(p_ref, vg_ref, vb_ref, ws_ref, bsb_ref, og_ref, y_ref, mbuf):
        _, u, _, _, _, _ = _gm_forward_tile(p_ref[...], vg_ref[...], vb_ref[...], ws_ref, bsb_ref, mbuf, H, nc)
        yg = u * mbuf[...]
        r = lax.rsqrt(_rows_mean(yg * yg) + RMS_EPS)
        y_ref[...] = (yg * r * og_ref[...]).astype(BF16)

    vec = pl.BlockSpec((1, W), lambda i: (0, 0))
    mat = pl.BlockSpec((H, CHUNK, CHUNK), lambda i: (0, 0, 0))
    return _pallas(
        body, 6, deps, name=name, grid=(S // tm,),
        in_specs=[pl.BlockSpec((tm, 2 * W), lambda i: (i, 0)), vec, vec, mat, mat, vec],
        out_specs=pl.BlockSpec((tm, W), lambda i: (i, 0)),
        out_shape=jax.ShapeDtypeStruct((S, 2 * W), BF16),
        scratch_shapes=[pltpu.VMEM((tm, W), F32)], compiler_params=_cp(1))(p, v_g, v_b, ws, bsb, out_g, *deps)


def _gm_bwd_call(p, d_y, v_g, v_b, ws, bsb, out_g, name, deps=()):
    S = p.shape[0]
    H = ws.shape[0]
    W = H * CHUNK
    tm = _tile(S, 256, CHUNK)
    nc = tm // CHUNK
    ni = S // tm

    def body(p_ref, dy_ref, vg_ref, vb_ref, ws_ref, bsb_ref, og_ref,
             dp_ref, dvg_ref, dvb_ref, dws_ref, dbs_ref, dog_ref, mbuf, dvbuf):
        i = pl.program_id(0)

        @pl.when(i == 0)
        def _():
            for r in (dvg_ref, dvb_ref, dws_ref, dbs_ref, dog_ref):
                r[...] = jnp.zeros_like(r)

        vg = vg_ref[...]
        dz, u, vh, rs, vnb, mask = _gm_forward_tile(p_ref[...], vg, vb_ref[...], ws_ref, bsb_ref, mbuf, H, nc)
        mixed = mbuf[...]
        yg = u * mixed
        r = lax.rsqrt(_rows_mean(yg * yg) + RMS_EPS)
        yn = yg * r
        dya = dy_ref[...]
        dog_ref[...] += _col_sum(dya * yn)
        dyg = dya * og_ref[...]
        dygm = r * (dyg - yn * _rows_mean(dyg * yn))
        du = dygm * mixed
        dmix = dygm * u
        dmb = dmix.astype(BF16)
        for h in range(H):
            cs = slice(h * CHUNK, (h + 1) * CHUNK)
            wm = jnp.where(mask, ws_ref[h], 0.0).astype(BF16)
            dcat = jnp.concatenate([dmb[c * CHUNK:(c + 1) * CHUNK, cs] for c in range(nc)], axis=1)
            vcat = jnp.concatenate([vnb[c * CHUNK:(c + 1) * CHUNK, cs] for c in range(nc)], axis=1)
            dvn = lax.dot_general(wm, dcat, _DN_TN, preferred_element_type=F32)
            dws_ref[h] += jnp.where(mask, lax.dot_general(dcat, vcat, _DN_NT, preferred_element_type=F32), 0.0)
            dbs = dmix[0:CHUNK, cs]
            for c in range(1, nc):
                dbs = dbs + dmix[c * CHUNK:(c + 1) * CHUNK, cs]
            dbs_ref[h] += dbs
            for c in range(nc):
                dvbuf[c * CHUNK:(c + 1) * CHUNK, cs] = dvn[:, c * CHUNK:(c + 1) * CHUNK]
        dvn_all = dvbuf[...]
        dvg_ref[...] += _col_sum(dvn_all * vh)
        dvb_ref[...] += _col_sum(dvn_all)
        dvh = dvn_all * vg
        dv0 = rs * (dvh - _rows_mean(dvh) - vh * _rows_mean(dvh * vh))
        dp_ref[...] = (jnp.concatenate([du, dv0], axis=1) * dz).astype(BF16)

        @pl.when(i == ni - 1)
        def _():
            for h in range(H):
                dbs_ref[h] = jnp.broadcast_to(jnp.sum(dbs_ref[h], axis=1, keepdims=True), (CHUNK, CHUNK))

    vec = pl.BlockSpec((1, W), lambda i: (0, 0))
    mat = pl.BlockSpec((H, CHUNK, CHUNK), lambda i: (0, 0, 0))
    vshape = jax.ShapeDtypeStruct((1, W), F32)
    mshape = jax.ShapeDtypeStruct((H, CHUNK, CHUNK), F32)
    return _pallas(
        body, 7, deps, name=name, grid=(ni,),
        in_specs=[pl.BlockSpec((tm, 2 * W), lambda i: (i, 0)), pl.BlockSpec((tm, W), lambda i: (i, 0)),
                  vec, vec, mat, mat, vec],
        out_specs=[pl.BlockSpec((tm, 2 * W), lambda i: (i, 0)), vec, vec, mat, mat, vec],
        out_shape=[jax.ShapeDtypeStruct((S, 4 * W), BF16), vshape, vshape, mshape, mshape, vshape],
        scratch_shapes=[pltpu.VMEM((tm, W), F32), pltpu.VMEM((tm, W), F32)],
        compiler_params=_cp(1))(p, d_y, v_g, v_b, ws, bsb, out_g, *deps)


def _lru_gates(prev8, xl, cw, cb, wa_ref, ba, wx_ref, bx, lam, H):
    sh = [_shift_down(prev8, xl, k) for k in range(4)]
    xr = cw[3] * sh[0] + cw[2] * sh[1] + cw[1] * sh[2] + cw[0] * sh[3] + cb
    xrb = xr.astype(BF16)
    rp, ip = [], []
    for h in range(H):
        cs = slice(h * CHUNK, (h + 1) * CHUNK)
        rp.append(jnp.dot(xrb[:, cs], wa_ref[h].astype(BF16), preferred_element_type=F32))
        ip.append(jnp.dot(xrb[:, cs], wx_ref[h].astype(BF16), preferred_element_type=F32))
    r = _sigmoid(jnp.concatenate(rp, axis=1) + ba)
    ig = _sigmoid(jnp.concatenate(ip, axis=1) + bx)
    sp = _softplus(-lam)
    t = jnp.tanh((-LRU_C) * r * sp)
    q = lax.rsqrt(1.0 - t)
    a = jnp.sqrt(1.0 + t) * q
    mult = jnp.sqrt(-2.0 * t) * q
    a2_over_mult = (1.0 + t) * q * lax.rsqrt(-2.0 * t)
    return xr, xrb, r, ig, sp, a, mult, a2_over_mult, sh


def _lru_fwd_call(p, cw, cb, wa, ba, wx, bx, lam, out_g, y_half, name):
    S = p.shape[0]
    H = wa.shape[0]
    W = H * CHUNK
    tm = _tile(S, 256, 16)
    ng = tm // 8

    def body(pg_ref, px_ref, cw_ref, cb_ref, wa_ref, ba_ref, wx_ref, bx_ref, lam_ref, og_ref, y_in_ref,
             y_ref, h_ref, saved_ref, xprev, hcar, abuf, bbuf):
        @pl.when(pl.program_id(0) == 0)
        def _():
            xprev[...] = jnp.zeros_like(xprev)
            hcar[...] = jnp.zeros_like(hcar)

        xl = px_ref[...]
        xr, _, r_gate, ig, _, a, mult, a2m, _ = _lru_gates(xprev[...], xl, _taps(cw_ref), cb_ref[...], wa_ref,
                                                           ba_ref[...], wx_ref, bx_ref[...], lam_ref[...], H)
        for k, val in enumerate((xr, r_gate, ig, a, mult, a2m)):
            saved_ref[k] = val
        xprev[...] = xl[tm - 8:]
        b = mult * (ig * xr)
        sub = lax.broadcasted_iota(jnp.int32, (tm, W), 0) & 7
        for d in (1, 2, 4):
            m = sub >= d
            a_s = jnp.where(m, pltpu.roll(a, d, 0), 1.0)
            b_s = jnp.where(m, pltpu.roll(b, d, 0), 0.0)
            b = a * b_s + b
            a = a * a_s
        abuf[...] = a
        bbuf[...] = b

        def step(g, carry):
            r0 = pl.multiple_of(g * 8, 8)
            h_ref[pl.ds(r0, 8), :] = abuf[pl.ds(r0, 8), :] * carry + bbuf[pl.ds(r0, 8), :]
            return jnp.broadcast_to(h_ref[pl.ds(r0 + 7, 1), :], (8, W))

        hcar[...] = lax.fori_loop(0, ng, step, hcar[...])
        yl = h_ref[...] * _gelu(pg_ref[...])
        r = lax.rsqrt(_rows_mean(yl * yl) + RMS_EPS)
        y_ref[...] = (yl * r * og_ref[...]).astype(BF16)

    vec = pl.BlockSpec((1, W), lambda i: (0, 0))
    mat = pl.BlockSpec((H, CHUNK, CHUNK), lambda i: (0, 0, 0))
    return pl.pallas_call(
        body, name=name, grid=(S // tm,),
        in_specs=[pl.BlockSpec((tm, W), lambda i: (i, 2)), pl.BlockSpec((tm, W), lambda i: (i, 3)),
                  pl.BlockSpec((4, W), lambda i: (0, 0)), vec, mat, vec, mat, vec, vec, vec,
                  pl.BlockSpec(memory_space=pl.ANY)],
        out_specs=[pl.BlockSpec((tm, W), lambda i: (i, 1)), pl.BlockSpec((tm, W), lambda i: (i, 0)),
                   pl.BlockSpec((6, tm, W), lambda i: (0, i, 0))],
        out_shape=[jax.ShapeDtypeStruct((S, 2 * W), BF16), jax.ShapeDtypeStruct((S, W), F32),
                   jax.ShapeDtypeStruct((6, S, W), F32)],
        input_output_aliases={10: 0},
        scratch_shapes=[pltpu.VMEM((8, W), F32), pltpu.VMEM((8, W), F32), pltpu.VMEM((tm, W), F32),
                        pltpu.VMEM((tm, W), F32)],
        compiler_params=_cp(1))(p, p, cw, cb, wa, ba, wx, bx, lam, out_g, y_half)


def _lru_bwd_call(p, hs, saved, d_y, cw, wa, wx, lam, out_g, dp_half, name):
    S = p.shape[0]
    H = wa.shape[0]
    W = H * CHUNK
    tm = _tile(S, 256, 16)
    ng = tm // 8
    ni = S // tm
    hb = tm // 8

    def body(pg_ref, px_ref, saved_ref, h_ref, hp_ref, dy_ref, cw_ref, wa_ref, wx_ref, lam_ref, og_ref, dp_in_ref,
             dp_ref, dcw_ref, dcb_ref, dwa_ref, dba_ref, dwx_ref, dbx_ref, dlam_ref, dog_ref,
             a_next, e_next, dxr_next, abuf, bbuf, ebuf, dxr0, dprb_buf, dpib_buf, sums):
        i = pl.program_id(0)
        ri = ni - 1 - i

        @pl.when(i == 0)
        def _():
            for r in (dcw_ref, dcb_ref, dwa_ref, dba_ref, dwx_ref, dbx_ref, dlam_ref, dog_ref,
                      a_next, e_next, dxr_next):
                r[...] = jnp.zeros_like(r)

        keep_prev = jnp.where(ri == 0, 0.0, 1.0)
        cw_ = _taps(cw_ref)
        lam_ = lam_ref[...]
        xl = px_ref[...]
        a = saved_ref[3]
        sp = _softplus(-lam_)
        gg, dgg = _gelu_parts(pg_ref[...])
        hv = h_ref[...]
        yl = hv * gg
        rr = lax.rsqrt(_rows_mean(yl * yl) + RMS_EPS)
        yn = yl * rr
        dyb = dy_ref[...]
        dog_ref[...] += _col_sum(dyb * yn)
        dyg = dyb * og_ref[...]
        dyl = rr * (dyg - yn * _rows_mean(dyg * yn))
        dh = dyl * gg
        dgl = dyl * hv * dgg

        an = _shift_up(a, a_next[...], 1)
        eb = dh
        sub = lax.broadcasted_iota(jnp.int32, (tm, W), 0) & 7
        for d in (1, 2, 4):
            m = sub < 8 - d
            a_s = jnp.where(m, pltpu.roll(an, tm - d, 0), 1.0)
            e_s = jnp.where(m, pltpu.roll(eb, tm - d, 0), 0.0)
            eb = an * e_s + eb
            an = an * a_s
        abuf[...] = an
        bbuf[...] = eb

        def step(g, carry):
            r0 = pl.multiple_of((ng - 1 - g) * 8, 8)
            ebuf[pl.ds(r0, 8), :] = abuf[pl.ds(r0, 8), :] * carry + bbuf[pl.ds(r0, 8), :]
            return jnp.broadcast_to(ebuf[pl.ds(r0, 1), :], (8, W))

        lax.fori_loop(0, ng, step, jnp.broadcast_to(e_next[0:1, :], (8, W)))
        a_next[...] = a[0:8]
        e_next[...] = ebuf[0:8, :]

        sums[...] = jnp.zeros_like(sums)
        row16 = lax.broadcasted_iota(jnp.int32, (16, GATE_LANES), 0)
        h_before = hp_ref[7:8, :] * keep_prev
        r_scale = (-LRU_C) * sp

        def gate_grads(g, carry):
            r0 = pl.multiple_of(g * 16, 16)
            above = jnp.maximum(r0 - 1, 0)
            for q in range(W // GATE_LANES):
                ls = slice(q * GATE_LANES, (q + 1) * GATE_LANES)
                e = ebuf[pl.ds(r0, 16), ls]
                h_prev_row = jnp.where(g == 0, h_before[:, ls], h_ref[pl.ds(above, 1), ls])
                hm1 = jnp.where(row16 == 0, h_prev_row, pltpu.roll(h_ref[pl.ds(r0, 16), ls], 1, 0))
                xr_, r_, ig_, a_, mult_, a2m_ = [saved_ref[k, pl.ds(r0, 16), ls] for k in range(6)]
                em = e * mult_
                dxr0[pl.ds(r0, 16), ls] = em * ig_
                dla = e * hm1 * a_ - e * ig_ * xr_ * a2m_
                dpr = dla * r_scale[:, ls] * r_ * (1.0 - r_)
                dpi = em * xr_ * ig_ * (1.0 - ig_)
                sums[0, :, ls] += dla * r_
                sums[1, :, ls] += dpr
                sums[2, :, ls] += dpi
                dprb_buf[pl.ds(r0, 16), ls] = dpr.astype(BF16)
                dpib_buf[pl.ds(r0, 16), ls] = dpi.astype(BF16)
            return carry

        lax.fori_loop(0, tm // 16, gate_grads, 0)
        dlam_ref[...] += (-LRU_C) * _col_sum(sums[0])
        dba_ref[...] += _col_sum(sums[1])
        dbx_ref[...] += _col_sum(sums[2])
        dprb = dprb_buf[...]
        dpib = dpib_buf[...]
        dxr = dxr0[...]
        xrb = saved_ref[0].astype(BF16)
        back = []
        for h in range(H):
            cs = slice(h * CHUNK, (h + 1) * CHUNK)
            wab = wa_ref[h].astype(BF16)
            wxb = wx_ref[h].astype(BF16)
            back.append(lax.dot_general(dprb[:, cs], wab, _DN_NT, preferred_element_type=F32)
                        + lax.dot_general(dpib[:, cs], wxb, _DN_NT, preferred_element_type=F32))
            dwa_ref[h] += lax.dot_general(xrb[:, cs], dprb[:, cs], _DN_TN, preferred_element_type=F32)
            dwx_ref[h] += lax.dot_general(xrb[:, cs], dpib[:, cs], _DN_TN, preferred_element_type=F32)
        dxr = dxr + jnp.concatenate(back, axis=1)

        nxt = dxr_next[...]
        ahead = [_shift_up(dxr, nxt, j) for j in range(4)]
        dxl = cw_[3] * ahead[0] + cw_[2] * ahead[1] + cw_[1] * ahead[2] + cw_[0] * ahead[3]
        dxr_next[...] = dxr[0:8]
        for k in range(4):
            dcw_ref[k:k + 1, :] += _col_sum(xl * ahead[3 - k])
        dcb_ref[...] += _col_sum(dxr)
        dp_ref[...] = jnp.concatenate([dgl, dxl], axis=1).astype(BF16)

        @pl.when(i == ni - 1)
        def _():
            dlam_ref[...] = -dlam_ref[...] * _sigmoid(-lam_)

    vec = pl.BlockSpec((1, W), lambda i: (0, 0))
    mat = pl.BlockSpec((H, CHUNK, CHUNK), lambda i: (0, 0, 0))
    rev = lambda i: ni - 1 - i
    prev = lambda i: jnp.maximum(rev(i) * hb - 1, 0)
    vshape = jax.ShapeDtypeStruct((1, W), F32)
    mshape = jax.ShapeDtypeStruct((H, CHUNK, CHUNK), F32)
    tile = lambda: pltpu.VMEM((tm, W), F32)
    car = lambda: pltpu.VMEM((8, W), F32)
    return pl.pallas_call(
        body, name=name, grid=(ni,),
        in_specs=[pl.BlockSpec((tm, W), lambda i: (rev(i), 2)), pl.BlockSpec((tm, W), lambda i: (rev(i), 3)),
                  pl.BlockSpec((6, tm, W), lambda i: (0, rev(i), 0)),
                  pl.BlockSpec((tm, W), lambda i: (rev(i), 0)), pl.BlockSpec((8, W), lambda i: (prev(i), 0)),
                  pl.BlockSpec((tm, W), lambda i: (rev(i), 1)),
                  pl.BlockSpec((4, W), lambda i: (0, 0)), mat, mat, vec, vec,
                  pl.BlockSpec(memory_space=pl.ANY)],
        out_specs=[pl.BlockSpec((tm, 2 * W), lambda i: (rev(i), 1)), pl.BlockSpec((4, W), lambda i: (0, 0)), vec,
                   mat, vec, mat, vec, vec, vec],
        out_shape=[jax.ShapeDtypeStruct((S, 4 * W), BF16), jax.ShapeDtypeStruct((4, W), F32), vshape,
                   mshape, vshape, mshape, vshape, vshape, vshape],
        input_output_aliases={11: 0},
        scratch_shapes=[car(), car(), car(), tile(), tile(), tile(), tile(), pltpu.VMEM((tm, W), BF16),
                        pltpu.VMEM((tm, W), BF16), pltpu.VMEM((3, 16, W), F32)],
        compiler_params=_cp(1, 56))(p, p, saved, hs, hs, d_y, cw, wa, wx, lam, out_g, dp_half)


def _rows128(a):
    return a.reshape(-1, LANES).astype(F32)


def _pack(arrays, pad_to=256):
    flat = jnp.concatenate([_rows128(a) for a in arrays], axis=0)
    pad = (-flat.shape[0]) % pad_to
    if pad:
        flat = jnp.concatenate([flat, jnp.zeros((pad, LANES), F32)], axis=0)
    return flat


def _unpack(flat, shapes):
    out, r = [], 0
    for s in shapes:
        n = 1
        for d in s:
            n *= d
        out.append(flat[r:r + n // LANES].reshape(s))
        r += n // LANES
    return out


def kernel(x, norm1_g, w_in, gm_v_g, gm_v_b, gm_ws, gm_bs, lru_conv_w, lru_conv_b, lru_wa, lru_ba, lru_wx, lru_bx, lru_lambda, gm_out_g, lru_out_g, w_out, norm2_g, ffn_w_up, ffn_conv_w, ffn_conv_b, ffn_w_down, final_g, loss_target, m_norm1_g, m_w_in, m_gm_v_g, m_gm_v_b, m_gm_ws, m_gm_bs, m_lru_conv_w, m_lru_conv_b, m_lru_wa, m_lru_ba, m_lru_wx, m_lru_bx, m_lru_lambda, m_gm_out_g, m_lru_out_g, m_w_out, m_norm2_g, m_ffn_w_up, m_ffn_conv_w, m_ffn_conv_b, m_ffn_w_down, m_final_g, v_norm1_g, v_w_in, v_gm_v_g, v_gm_v_b, v_gm_ws, v_gm_bs, v_lru_conv_w, v_lru_conv_b, v_lru_wa, v_lru_ba, v_lru_wx, v_lru_bx, v_lru_lambda, v_gm_out_g, v_lru_out_g, v_w_out, v_norm2_g, v_ffn_w_up, v_ffn_conv_w, v_ffn_conv_b, v_ffn_w_down, v_final_g):
    wts = dict(norm1_g=norm1_g, w_in=w_in, gm_v_g=gm_v_g, gm_v_b=gm_v_b, gm_ws=gm_ws, gm_bs=gm_bs,
               lru_conv_w=lru_conv_w, lru_conv_b=lru_conv_b, lru_wa=lru_wa, lru_ba=lru_ba, lru_wx=lru_wx,
               lru_bx=lru_bx, lru_lambda=lru_lambda, gm_out_g=gm_out_g, lru_out_g=lru_out_g, w_out=w_out,
               norm2_g=norm2_g, ffn_w_up=ffn_w_up, ffn_conv_w=ffn_conv_w, ffn_conv_b=ffn_conv_b,
               ffn_w_down=ffn_w_down, final_g=final_g)
    mom = dict(norm1_g=m_norm1_g, w_in=m_w_in, gm_v_g=m_gm_v_g, gm_v_b=m_gm_v_b, gm_ws=m_gm_ws, gm_bs=m_gm_bs,
               lru_conv_w=m_lru_conv_w, lru_conv_b=m_lru_conv_b, lru_wa=m_lru_wa, lru_ba=m_lru_ba, lru_wx=m_lru_wx,
               lru_bx=m_lru_bx, lru_lambda=m_lru_lambda, gm_out_g=m_gm_out_g, lru_out_g=m_lru_out_g, w_out=m_w_out,
               norm2_g=m_norm2_g, ffn_w_up=m_ffn_w_up, ffn_conv_w=m_ffn_conv_w, ffn_conv_b=m_ffn_conv_b,
               ffn_w_down=m_ffn_w_down, final_g=m_final_g)
    var = dict(norm1_g=v_norm1_g, w_in=v_w_in, gm_v_g=v_gm_v_g, gm_v_b=v_gm_v_b, gm_ws=v_gm_ws, gm_bs=v_gm_bs,
               lru_conv_w=v_lru_conv_w, lru_conv_b=v_lru_conv_b, lru_wa=v_lru_wa, lru_ba=v_lru_ba, lru_wx=v_lru_wx,
               lru_bx=v_lru_bx, lru_lambda=v_lru_lambda, gm_out_g=v_gm_out_g, lru_out_g=v_lru_out_g, w_out=v_w_out,
               norm2_g=v_norm2_g, ffn_w_up=v_ffn_w_up, ffn_conv_w=v_ffn_conv_w, ffn_conv_b=v_ffn_conv_b,
               ffn_w_down=v_ffn_w_down, final_g=v_final_g)

    xi, yi, ci = lax.axis_index("x"), lax.axis_index("y"), lax.axis_index("c")
    chip = 2 * xi + yi
    dev = 2 * chip + ci
    core_chip = jnp.stack([ci, chip]).astype(jnp.int32)

    xs = x[0]
    tgt = loss_target[0]
    S, D = xs.shape
    H = gm_ws.shape[1]
    W = H * CHUNK
    Fd = ffn_w_down.shape[1] * N_DEV
    lcw_cols = lru_conv_w.shape[2]
    fcw_cols = ffn_conv_w.shape[2]

    dev1 = jnp.reshape(dev, (1,)).astype(jnp.int32)

    def place_own(shards, name):
        return [_place_own_call(s, dev1, dt, "%s_own%d" % (name, k)) for k, (s, dt) in enumerate(shards)]

    def gather_start(shards, name, after=()):
        lands = place_own(shards, name)
        return _exchange_start([], lands, 4 * len(lands), _gather_stage1_copies(len(lands)), name + "_ici", after)

    def gather_forward(lands, name, after=()):
        return _exchange_start([], lands, 3 * len(lands), _gather_stage2_copies(len(lands)), name + "_d2d", after)

    def pair_start(g, name, after=()):
        return _exchange_start([g], [lax.empty((4,) + g.shape[1:], F32)], 4, _pair_copies(1), name, after)

    def chip_start(p16, name, after=()):
        return _exchange_start([p16], [lax.empty((3,) + p16.shape[1:], BF16)], 3, _chip_copies(1), name, after)

    vgm_g, vgm_b = gm_v_g, gm_v_b
    ws, wa, wx = gm_ws[0], lru_wa[0], lru_wx[0]
    bsb = jnp.broadcast_to(gm_bs[0][:, :, None], (H, CHUNK, CHUNK))
    ba, bx = lru_ba.reshape(1, W), lru_bx.reshape(1, W)
    fcb = ffn_conv_b
    fing = final_g.reshape(1, D)

    conv_pack = _pack([lru_conv_w[0], ffn_conv_w[0]], pad_to=8)
    lands = place_own([(w_in[0], BF16), (conv_pack, F32)], "gather_in")
    ga_pair = _exchange_start([], lands, 2, _gather_stage1_copies(2, to_chips=False), "gather_in_pair")
    ga1 = _exchange_start([], ga_pair.bufs, 6, _gather_stage1_copies(2, to_sibling=False), "gather_in_ici")
    h1 = _rmsnorm_call(xs, norm1_g, "norm1", deps=(ga1.token,))
    ga_pair.bufs = ga1.bufs
    _, la = _exchange_wait(ga_pair, after=(h1,))
    own_blocks = jnp.stack([dev, dev + 1 - 2 * ci]).astype(jnp.int32)
    other_blocks = ((2 * chip + 2 + jnp.arange(N_DEV - 2)) % N_DEV).astype(jnp.int32)
    p_own = _mm_some_blocks_call(h1, la[0], own_blocks, F32, "in_proj_own")
    ga1.bufs = la
    _, la = _exchange_wait(ga1, after=(p_own,))
    ga2 = gather_forward(la, "gather_in")
    gb1 = gather_start([(w_out[0], BF16)], "gather_out", after=(ga2.token,))
    gc1 = gather_start([(ffn_w_up[0], BF16)], "gather_up", after=(gb1.token,))
    gd1 = gather_start([(ffn_w_down[0], BF16)], "gather_down", after=(gc1.token,))
    _, (win_g, conv_g) = _exchange_wait(ga2, after=(gd1.token,))
    n_l = 4 * lcw_cols // LANES
    n_f = 3 * fcw_cols // LANES
    lcw = conv_g[:, :n_l].reshape(N_DEV, 4, lcw_cols).transpose(1, 0, 2).reshape(4, N_DEV * lcw_cols)
    fcw = conv_g[:, n_l:n_l + n_f].reshape(N_DEV, 3, fcw_cols).transpose(1, 0, 2).reshape(3, N_DEV * fcw_cols)

    p = _mm_some_blocks_call(h1, win_g, other_blocks, F32, "in_proj_rest", out_so_far=p_own)
    win_rows = _unblock_call(win_g, "w_in_rows")
    _, lb = _exchange_wait(gb1, after=(p,))
    gb2 = gather_forward(lb, "gather_out")
    y_half = _gm_fwd_call(p, vgm_g, vgm_b, ws, bsb, gm_out_g, "gmlp_fwd", deps=(gb2.token,))
    y, hs, lru_saved = _lru_fwd_call(p, lcw, lru_conv_b, wa, ba, wx, bx, lru_lambda, lru_out_g, y_half, "lru_fwd")
    _, lc = _exchange_wait(gc1, after=(y,))
    gc2 = gather_forward(lc, "gather_up")
    _, (wout_g,) = _exchange_wait(gb2, after=(y, gc2.token))
    wout_full = wout_g.reshape(D, D)
    x2 = _mm_out_call(xs, y, wout_full, "out_proj")
    h2 = _rmsnorm_call(x2, norm2_g, "norm2")
    _, (wup_g,) = _exchange_wait(gc2, after=(h2,))
    _, ld = _exchange_wait(gd1, after=(h2,))
    gd2 = gather_forward(ld, "gather_down")
    up3, upc3, f = _ffn_up_act_call(h2, wup_g, fcw, fcb, "ffn_up", deps=(gd2.token,))
    _, (wdown_g,) = _exchange_wait(gd2, after=(f,))
    wdown_full = wdown_g.reshape(Fd, D)
    dx3, dx3b, loss_acc, d_final = _mm_down_loss_call(x2, f, wdown_full, fing, tgt, "ffn_down_loss")

    g_wdown = _mm_tn_rows_call(f, dx3b, "ffn_down_dw").reshape((N_DEV,) + ffn_w_down.shape[1:])
    pd = pair_start(g_wdown, "pair_down")
    d_up3, dfcw_g, dfcw_v, dfcb_g, dfcb_v = _ffn_down_dx_act_bwd_call(dx3b, wdown_full, up3, upc3, fcw, "ffn_down_dx",
                                                                     deps=(pd.token,))
    (g_wdown,), (r1,) = _exchange_wait(pd, after=(d_up3,))
    own_down, p16 = _pair_add_call(g_wdown, r1, core_chip, "pair_add_down")
    cd = chip_start(p16, "chip_down")
    g_wup = _mm_tn_cols_call(h2, d_up3, N_DEV, ffn_w_up.shape[2], "ffn_up_dw", deps=(cd.token,))
    pu = pair_start(g_wup, "pair_up")
    dx2, dx2b, d_norm2 = _mm_dx_norm_call(d_up3, wup_g, dx3, x2, norm2_g, "ffn_up_dx", deps=(pu.token,))
    g_wout = _mm_tn_rows_call(y, dx2b, "out_proj_dw").reshape((N_DEV,) + w_out.shape[1:])
    po = pair_start(g_wout, "pair_out")
    d_y = _mm_nt_call(dx2b, wout_full, F32, "out_proj_dx", deps=(po.token,))
    (g_wup,), (r1,) = _exchange_wait(pu, after=(d_y,))
    own_up, p16 = _pair_add_call(g_wup, r1, core_chip, "pair_add_up")
    _, (r2_down,) = _exchange_wait(cd, after=(p16,))
    cu = chip_start(p16, "chip_up", after=(r2_down,))
    dp_half, d_vg, d_vb, d_ws, d_bs, d_gog = _gm_bwd_call(p, d_y, vgm_g, vgm_b, ws, bsb, gm_out_g, "gmlp_bwd",
                                                          deps=(cu.token,))
    d_p2, d_lcw, d_lcb, d_wa, d_ba, d_wx, d_bx, d_lam, d_log = _lru_bwd_call(
        p, hs, lru_saved, d_y, lcw, wa, wx, lru_lambda, lru_out_g, dp_half, "lru_bwd")
    d_p = d_p2[None]
    (g_wout,), (r1,) = _exchange_wait(po, after=(d_p,))
    own_out, p16_out = _pair_add_call(g_wout, r1, core_chip, "pair_add_out")
    g_win = _mm_tn_cols_call(h1, d_p, N_DEV, w_in.shape[2], "in_proj_dw")
    pi = pair_start(g_win, "pair_in")
    _, (r2_up,) = _exchange_wait(cu, after=(g_win,))
    co = chip_start(p16_out, "chip_out", after=(r2_up,))
    gx_a, dn_a = _mm_nt_norm_call(d_p[0], win_rows, dx2, xs, norm1_g, "in_proj_dx_a", deps=(co.token, pi.token),
                                  part=(0, 2))
    (g_win,), (r1,) = _exchange_wait(pi, after=(gx_a,))
    own_in, p16 = _pair_add_call(g_win, r1, core_chip, "pair_add_in")
    _, (r2_out,) = _exchange_wait(co, after=(p16,))
    ci_ = chip_start(p16, "chip_in", after=(r2_out,))
    grad_x, dn_b = _mm_nt_norm_call(d_p[0], win_rows, dx2, xs, norm1_g, "in_proj_dx_b", deps=(ci_.token,),
                                    part=(1, 2), dx_so_far=gx_a)

    small_g = dict(norm1_g=dn_a + dn_b, gm_v_g=d_vg, gm_v_b=d_vb, gm_ws=d_ws, gm_bs=d_bs[:, :, 0], lru_conv_b=d_lcb,
                   lru_wa=d_wa, lru_ba=d_ba, lru_wx=d_wx, lru_bx=d_bx, lru_lambda=d_lam, gm_out_g=d_gog,
                   lru_out_g=d_log, norm2_g=d_norm2,
                   ffn_conv_b=jnp.concatenate([dfcb_g, dfcb_v], axis=1), final_g=d_final)
    rep = _pack([small_g[n] for n in SMALL])
    conv_part = _pack([d_lcw, jnp.concatenate([dfcw_g, dfcw_v], axis=1)], pad_to=8)
    n_rep, n_conv = rep.shape[0], conv_part.shape[0]
    gs1 = gather_start([(jnp.concatenate([rep, conv_part], axis=0), F32)], "gather_small")

    def adamw_big(n, own, r2, deps=()):
        return _adamw_call(wts[n][0], mom[n][0], var[n][0], [(own, None), (r2, 0), (r2, 1), (r2, 2)], "adamw_" + n, deps)

    res = {}
    res["ffn_w_down"] = adamw_big("ffn_w_down", own_down, r2_down, (gs1.token,))
    res["ffn_w_up"] = adamw_big("ffn_w_up", own_up, r2_up, (gs1.token,))
    res["w_out"] = adamw_big("w_out", own_out, r2_out, (gs1.token,))
    _, ls = _exchange_wait(gs1, after=(res["w_out"][0], res["ffn_w_up"][0], res["ffn_w_down"][0]))
    gs2 = gather_forward(ls, "gather_small")
    _, (r2_in,) = _exchange_wait(ci_, after=(gs2.token,))
    res["w_in"] = adamw_big("w_in", own_in, r2_in)
    _, (parts,) = _exchange_wait(gs2, after=(res["w_in"][0],))
    g_rep, d_rep, m_rep, v_rep = _adamw_call(
        _pack([wts[n] for n in SMALL]), _pack([mom[n] for n in SMALL]), _pack([var[n] for n in SMALL]),
        [(parts, k) for k in range(N_DEV)], "adamw_small")
    shapes = [wts[n].shape for n in SMALL]
    for n, g_, d_, m_, v_ in zip(SMALL, _unpack(g_rep, shapes), _unpack(d_rep, shapes), _unpack(m_rep, shapes),
                                 _unpack(v_rep, shapes)):
        res[n] = (g_, d_, m_, v_)
    conv_sum = _sum_call(parts, n_rep, n_conv, "sum_conv_grads")
    g_lcw = conv_sum[:4 * W // LANES].reshape(4, W)
    g_fcw = conv_sum[4 * W // LANES:4 * W // LANES + 6 * Fd // LANES].reshape(3, 2 * Fd)
    for n, full in (("lru_conv_w", g_lcw), ("ffn_conv_w", g_fcw)):
        cols = wts[n].shape[2]
        mine = lax.dynamic_slice_in_dim(full, dev * cols, cols, axis=1)
        res[n] = _adamw_call(wts[n][0], mom[n][0], var[n][0], [(mine, None)], "adamw_" + n)

    loss = lax.psum(loss_acc[0, 0], ("x", "y", "c"))
    outs = [[], [], [], []]
    for n in WEIGHTS:
        for k in range(4):
            outs[k].append(res[n][k].reshape(wts[n].shape))
    return (loss, grad_x[None], *outs[0], *outs[1], *outs[2], *outs[3])
```

```python
import functools
import math

import jax
import jax.numpy as jnp
from jax import lax
from jax.experimental import pallas as pl
from jax.experimental.pallas import tpu as pltpu

F32 = jnp.float32
BF16 = jnp.bfloat16

RMS_EPS = 1e-6
LN_EPS = 1e-5
LRU_C = 8.0
CHUNK = 128
ADAM_LR = 0.001
ADAM_B1 = 0.9
ADAM_B2 = 0.999
ADAM_EPS = 1e-08
ADAM_WD = 0.01
ADAM_STEP = 10
N_DEV = 8
LANES = 128
MIB = 1024 * 1024

WEIGHTS = ['norm1_g', 'w_in', 'gm_v_g', 'gm_v_b', 'gm_ws', 'gm_bs', 'lru_conv_w', 'lru_conv_b', 'lru_wa', 'lru_ba',
           'lru_wx', 'lru_bx', 'lru_lambda', 'gm_out_g', 'lru_out_g', 'w_out', 'norm2_g', 'ffn_w_up', 'ffn_conv_w',
           'ffn_conv_b', 'ffn_w_down', 'final_g']
BIG = ['w_in', 'w_out', 'ffn_w_up', 'ffn_w_down']
CONV = ['lru_conv_w', 'ffn_conv_w']
SMALL = [n for n in WEIGHTS if n not in BIG and n not in CONV]

_DN_NT = (((1,), (1,)), ((), ()))
_DN_TN = (((0,), (0,)), ((), ()))
_GELU_C = 0.7978845608028654


def _cp(n_axes, vmem_mib=48):
    return pltpu.CompilerParams(dimension_semantics=("arbitrary",) * n_axes, vmem_limit_bytes=vmem_mib * MIB)


def _tile(n, pref, mult=8):
    t = min(pref, n)
    t -= t % mult
    while t >= mult:
        if n % t == 0:
            return t
        t -= mult
    return n


def _gelu_gate(z, z2):
    return 0.5 * jnp.tanh(z * ((_GELU_C * 0.044715) * z2 + _GELU_C)) + 0.5


def _gelu(z):
    return z * _gelu_gate(z, z * z)


def _gelu_parts(z):
    z2 = z * z
    s = _gelu_gate(z, z2)
    g = z * s
    dg = s + g * (1.0 - s) * ((6.0 * _GELU_C * 0.044715) * z2 + 2.0 * _GELU_C)
    return g, dg


def _sigmoid(z):
    return 0.5 + 0.5 * jnp.tanh(0.5 * z)


def _softplus(z):
    t = jnp.exp(-jnp.abs(z))
    u = 1.0 + t
    log1p = jnp.where(u == 1.0, t, jnp.log(u) * t / (u - 1.0))
    return jnp.maximum(z, 0.0) + log1p


def _rows_mean(v):
    return jnp.mean(v, axis=-1, keepdims=True)


def _col_sum(v):
    return jnp.sum(v, axis=0, keepdims=True)


def _shift_down(prev8, cur, k):
    if k == 0:
        return cur
    z = jnp.concatenate([prev8, cur], axis=0)
    return pltpu.roll(z, k, 0)[8:]


def _shift_up(cur, next8, k):
    if k == 0:
        return cur
    n = cur.shape[0]
    z = jnp.concatenate([cur, next8], axis=0)
    return pltpu.roll(z, n + 8 - k, 0)[:n]


def _mesh_pos():
    return lax.axis_index("x"), lax.axis_index("y"), lax.axis_index("c")


def _any_specs(n):
    return [pl.BlockSpec(memory_space=pl.ANY)] * n


def _pallas(body, n_in, deps, **kw):
    nd = len(deps)
    if not nd:
        return pl.pallas_call(body, **kw)

    def ordered(*refs):
        body(*refs[:n_in], *refs[n_in + nd:])

    kw["in_specs"] = list(kw["in_specs"]) + _any_specs(nd)
    return pl.pallas_call(ordered, **kw)


_HBM = pl.BlockSpec(memory_space=pltpu.HBM)
_SEM = pl.BlockSpec(memory_space=pltpu.SEMAPHORE)
_EFFECT = pltpu.SideEffectType.DATAFLOW_SIDE_EFFECTING


class _InFlight:
    def __init__(self, sems, bufs, token, n_src, n_copies, make_copies, name):
        self.sems, self.bufs, self.token = sems, bufs, token
        self.n_src, self.n_copies, self.make_copies, self.name = n_src, n_copies, make_copies, name


def _exchange_start(srcs, lands, n_copies, make_copies, name, after=()):
    bufs = list(srcs) + list(lands)
    nb, na = len(bufs), len(after)
    ns = len(srcs)

    def body(*refs):
        b_refs = refs[:nb]
        outs = refs[nb + na:]
        send, recv = outs[:n_copies], outs[n_copies:2 * n_copies]
        token = outs[-1]
        for cp in make_copies(b_refs[:ns], b_refs[ns:], send, recv):
            cp.start()
        token[...] = jnp.zeros_like(token)

    out = pl.pallas_call(
        body, name=name,
        out_shape=[pltpu.SemaphoreType.DMA(())] * (2 * n_copies) + [pltpu.HBM(b.shape, b.dtype) for b in bufs]
        + [jax.ShapeDtypeStruct((8, LANES), F32)],
        in_specs=[_HBM] * nb + _any_specs(na),
        out_specs=[_SEM] * (2 * n_copies) + [_HBM] * nb + [pl.BlockSpec(memory_space=pltpu.VMEM)],
        input_output_aliases={i: 2 * n_copies + i for i in range(nb)},
        compiler_params=pltpu.CompilerParams(has_side_effects=_EFFECT),
    )(*[pltpu.with_memory_space_constraint(b, pltpu.HBM) for b in bufs], *after)
    return _InFlight(out[:2 * n_copies], out[2 * n_copies:2 * n_copies + nb], out[-1], ns, n_copies, make_copies, name)


def _exchange_wait(fl, after=()):
    nb, na, nc, ns = len(fl.bufs), len(after), fl.n_copies, fl.n_src

    def body(*refs):
        b_refs = refs[:nb]
        sems = refs[nb:nb + 2 * nc]
        copies = fl.make_copies(b_refs[:ns], b_refs[ns:], sems[:nc], sems[nc:])
        for cp in copies:
            cp.wait_send()
        for cp in copies:
            cp.wait_recv()

    out = pl.pallas_call(
        body, name=fl.name + "_wait",
        out_shape=[pltpu.HBM(b.shape, b.dtype) for b in fl.bufs],
        in_specs=[_HBM] * nb + [_SEM] * (2 * nc) + _any_specs(na),
        out_specs=[_HBM] * nb,
        input_output_aliases={i: i for i in range(nb)},
        compiler_params=pltpu.CompilerParams(has_side_effects=_EFFECT),
    )(*fl.bufs, *fl.sems, *after)
    return list(out[:ns]), list(out[ns:])


def _remote(src, dst, send_sem, recv_sem, to):
    return pltpu.make_async_remote_copy(src_ref=src, dst_ref=dst, send_sem=send_sem, recv_sem=recv_sem,
                                        device_id=to, device_id_type=pl.DeviceIdType.MESH)


def _gather_stage1_copies(n, to_sibling=True, to_chips=True):
    def make(s_refs, l_refs, send, recv):
        x, y, c = _mesh_pos()
        own = 4 * x + 2 * y + c
        targets = ([(x, y, 1 - c)] if to_sibling else []) + (
            [(1 - x, y, c), (x, 1 - y, c), (1 - x, 1 - y, c)] if to_chips else [])
        m = len(targets)
        return [_remote(l_refs[a].at[own], l_refs[a].at[own], send[m * a + k], recv[m * a + k], to)
                for a in range(n) for k, to in enumerate(targets)]
    return make


def _gather_stage2_copies(n):
    def make(s_refs, l_refs, send, recv):
        x, y, c = _mesh_pos()
        blocks = [4 * (1 - x) + 2 * y + c, 4 * x + 2 * (1 - y) + c, 4 * (1 - x) + 2 * (1 - y) + c]
        return [_remote(l_refs[a].at[b], l_refs[a].at[b], send[3 * a + j], recv[3 * a + j], (x, y, 1 - c))
                for a in range(n) for j, b in enumerate(blocks)]
    return make


def _pair_copies(n):
    def make(s_refs, l_refs, send, recv):
        x, y, c = _mesh_pos()
        return [_remote(s_refs[a].at[2 * k + 1 - c], l_refs[a].at[k], send[4 * a + k], recv[4 * a + k], (x, y, 1 - c))
                for a in range(n) for k in range(4)]
    return make


def _chip_copies(n):
    def make(s_refs, l_refs, send, recv):
        x, y, c = _mesh_pos()
        chips = [(1 - x, y), (x, 1 - y), (1 - x, 1 - y)]
        return [_remote(s_refs[a].at[2 * ch[0] + ch[1]], l_refs[a].at[j], send[3 * a + j], recv[3 * a + j], (*ch, c))
                for a in range(n) for j, ch in enumerate(chips)]
    return make


def _place_own_call(shard, dev, dtype, name):
    R, C = shard.shape
    tr = _tile(R, max(16, MIB // (C * 4)), 16)

    def body(d_ref, s_ref, o_ref):
        o_ref[...] = s_ref[...].astype(dtype)

    grid_spec = pltpu.PrefetchScalarGridSpec(
        num_scalar_prefetch=1, grid=(R // tr,),
        in_specs=[pl.BlockSpec((tr, C), lambda r, d: (r, 0))],
        out_specs=pl.BlockSpec((None, tr, C), lambda r, d: (d[0], r, 0)))
    return pl.pallas_call(body, name=name, grid_spec=grid_spec,
                          out_shape=jax.ShapeDtypeStruct((N_DEV, R, C), dtype), compiler_params=_cp(1))(dev, shard)


def _pair_add_call(g, r1, core_chip, name):
    _, R, C = g.shape
    tr = _tile(R, max(16, (2 * MIB) // (C * 4)), 16)

    def body(cc_ref, g_ref, r_ref, p32_ref, p16_ref):
        s = g_ref[...] + r_ref[...]
        p16_ref[...] = s.astype(BF16)

        @pl.when(pl.program_id(1) == cc_ref[1])
        def _():
            p32_ref[...] = s

    grid_spec = pltpu.PrefetchScalarGridSpec(
        num_scalar_prefetch=1, grid=(R // tr, 4),
        in_specs=[pl.BlockSpec((None, tr, C), lambda r, k, cc: (2 * k + cc[0], r, 0)),
                  pl.BlockSpec((None, tr, C), lambda r, k, cc: (k, r, 0))],
        out_specs=[pl.BlockSpec((tr, C), lambda r, k, cc: (r, 0)),
                   pl.BlockSpec((None, tr, C), lambda r, k, cc: (k, r, 0))])
    return pl.pallas_call(
        body, name=name, grid_spec=grid_spec,
        out_shape=[jax.ShapeDtypeStruct((R, C), F32), jax.ShapeDtypeStruct((4, R, C), BF16)],
        compiler_params=_cp(2))(core_chip, g, r1)


def _adamw_call(w, m, v, addends, name, deps=()):
    R, C = w.shape
    tr = _tile(R, max(8, (MIB // 2) // (C * 4)), 16)
    na = len(addends)
    c1 = 1.0 - ADAM_B1 ** ADAM_STEP
    c2 = 1.0 - ADAM_B2 ** ADAM_STEP

    def body(*refs):
        w_ref, m_ref, v_ref = refs[:3]
        a_refs = refs[3:3 + na]
        g_ref, d_ref, nm_ref, nv_ref = refs[3 + na:]
        g = a_refs[0][...].astype(F32)
        for a_ref in a_refs[1:]:
            g = g + a_ref[...].astype(F32)
        nm = ADAM_B1 * m_ref[...] + (1.0 - ADAM_B1) * g
        nv = ADAM_B2 * v_ref[...] + (1.0 - ADAM_B2) * (g * g)
        g_ref[...] = g
        nm_ref[...] = nm
        nv_ref[...] = nv
        d_ref[...] = -ADAM_LR * ((nm / c1) / (jnp.sqrt(nv / c2) + ADAM_EPS) + ADAM_WD * w_ref[...])

    flat = pl.BlockSpec((tr, C), lambda r: (r, 0))
    a_specs = [flat if k is None else pl.BlockSpec((None, tr, C), functools.partial(lambda r, kk: (kk, r, 0), kk=k))
               for _, k in addends]
    out = jax.ShapeDtypeStruct((R, C), F32)
    return _pallas(
        body, 3 + na, deps, name=name, grid=(R // tr,),
        in_specs=[flat, flat, flat] + a_specs, out_specs=[flat] * 4, out_shape=[out] * 4,
        compiler_params=_cp(1))(w, m, v, *[a for a, _ in addends], *deps)


def _sum_call(parts, row0, rows, name):
    n = parts.shape[0]
    tr = _tile(math.gcd(row0, rows), 256, 8)
    b0 = row0 // tr

    def body(p_ref, o_ref):
        s = p_ref[0]
        for k in range(1, n):
            s = s + p_ref[k]
        o_ref[...] = s

    return pl.pallas_call(
        body, name=name, grid=(rows // tr,),
        in_specs=[pl.BlockSpec((n, tr, LANES), lambda r: (0, r + b0, 0))],
        out_specs=pl.BlockSpec((tr, LANES), lambda r: (r, 0)),
        out_shape=jax.ShapeDtypeStruct((rows, LANES), F32), compiler_params=_cp(1))(parts)


def _rmsnorm_call(x, g, name, deps=()):
    S, D = x.shape
    tm = _tile(S, 512, 16)

    def body(x_ref, g_ref, o_ref):
        xv = x_ref[...]
        r = lax.rsqrt(_rows_mean(xv * xv) + RMS_EPS)
        o_ref[...] = (xv * r * g_ref[...]).astype(BF16)

    return _pallas(
        body, 2, deps, name=name, grid=(S // tm,),
        in_specs=[pl.BlockSpec((tm, D), lambda i: (i, 0)), pl.BlockSpec((1, D), lambda i: (0, 0))],
        out_specs=pl.BlockSpec((tm, D), lambda i: (i, 0)),
        out_shape=jax.ShapeDtypeStruct((S, D), BF16), compiler_params=_cp(1))(x, g, *deps)


def _mm_some_blocks_call(a, wg, blocks, out_dtype, name, out_so_far=None):
    S, K = a.shape
    nb, _, bn = wg.shape
    tm = _tile(S, 1024, 16)

    def body(b_ref, a_ref, w_ref, *rest):
        rest[-1][...] = jnp.dot(a_ref[...], w_ref[...], preferred_element_type=F32).astype(out_dtype)

    in_specs = [pl.BlockSpec((tm, K), lambda i, j, b: (i, 0)), pl.BlockSpec((None, K, bn), lambda i, j, b: (b[j], 0, 0))]
    operands = [a, wg]
    aliases = {}
    if out_so_far is not None:
        in_specs.append(pl.BlockSpec(memory_space=pl.ANY))
        operands.append(out_so_far)
        aliases = {3: 0}
    grid_spec = pltpu.PrefetchScalarGridSpec(
        num_scalar_prefetch=1, grid=(S // tm, blocks.shape[0]), in_specs=in_specs,
        out_specs=pl.BlockSpec((tm, bn), lambda i, j, b: (i, b[j])))
    return pl.pallas_call(body, name=name, grid_spec=grid_spec,
                          out_shape=jax.ShapeDtypeStruct((S, nb * bn), out_dtype), input_output_aliases=aliases,
                          compiler_params=_cp(2))(blocks, *operands)


def _mm_out_call(x, y, w, name):
    S, D = x.shape
    tm = _tile(S, 512, 16)

    def body(x_ref, y_ref, w_ref, o_ref):
        o_ref[...] = x_ref[...] + jnp.dot(y_ref[...], w_ref[...], preferred_element_type=F32)

    return pl.pallas_call(
        body, name=name, grid=(S // tm,),
        in_specs=[pl.BlockSpec((tm, D), lambda i: (i, 0)), pl.BlockSpec((tm, D), lambda i: (i, 0)),
                  pl.BlockSpec((D, D), lambda i: (0, 0))],
        out_specs=pl.BlockSpec((tm, D), lambda i: (i, 0)),
        out_shape=jax.ShapeDtypeStruct((S, D), F32), compiler_params=_cp(1))(x, y, w)


def _mm_nt_call(a, w, out_dtype, name, deps=()):
    S, K = a.shape
    N = w.shape[0]
    tm = _tile(S, 1024, 16)
    tn = _tile(N, 768, LANES)

    def body(a_ref, w_ref, o_ref):
        o_ref[...] = lax.dot_general(a_ref[...], w_ref[...], _DN_NT, preferred_element_type=F32).astype(out_dtype)

    return _pallas(
        body, 2, deps, name=name, grid=(S // tm, N // tn),
        in_specs=[pl.BlockSpec((tm, K), lambda i, j: (i, 0)), pl.BlockSpec((tn, K), lambda i, j: (j, 0))],
        out_specs=pl.BlockSpec((tm, tn), lambda i, j: (i, j)),
        out_shape=jax.ShapeDtypeStruct((S, N), out_dtype), compiler_params=_cp(2))(a, w, *deps)


def _mm_down_loss_call(x2, f, w, final_g, target, name):
    S, D = x2.shape
    Fd = f.shape[1]
    tm = _tile(S, 512, 16)
    tk = _tile(Fd, 768, LANES)
    nk = Fd // tk

    def body(x_ref, f_ref, w_ref, g_ref, t_ref, dx_ref, dxb_ref, loss_ref, dg_ref, acc):
        i, k = pl.program_id(0), pl.program_id(1)

        @pl.when(jnp.logical_and(i == 0, k == 0))
        def _():
            loss_ref[...] = jnp.zeros_like(loss_ref)
            dg_ref[...] = jnp.zeros_like(dg_ref)

        @pl.when(k == 0)
        def _():
            acc[...] = jnp.zeros_like(acc)

        acc[...] += jnp.dot(f_ref[...], w_ref[...], preferred_element_type=F32)

        @pl.when(k == nk - 1)
        def _():
            x3 = x_ref[...] + acc[...]
            r = lax.rsqrt(_rows_mean(x3 * x3) + RMS_EPS)
            g = g_ref[...]
            xn = x3 * r
            diff = xn * g - t_ref[...]
            loss_ref[...] += 0.5 * jnp.sum(_rows_mean(diff * diff))
            dout = diff * (1.0 / D)
            dg_ref[...] += _col_sum(dout * xn)
            dyg = dout * g
            dx = r * (dyg - xn * _rows_mean(dyg * xn))
            dx_ref[...] = dx
            dxb_ref[...] = dx.astype(BF16)

    row = lambda i, k: (i, 0)
    return pl.pallas_call(
        body, name=name, grid=(S // tm, nk),
        in_specs=[pl.BlockSpec((tm, D), row), pl.BlockSpec((tm, tk), lambda i, k: (i, k)),
                  pl.BlockSpec((tk, D), lambda i, k: (k, 0)), pl.BlockSpec((1, D), lambda i, k: (0, 0)),
                  pl.BlockSpec((tm, D), row)],
        out_specs=[pl.BlockSpec((tm, D), row), pl.BlockSpec((tm, D), row),
                   pl.BlockSpec((8, LANES), lambda i, k: (0, 0)), pl.BlockSpec((1, D), lambda i, k: (0, 0))],
        out_shape=[jax.ShapeDtypeStruct((S, D), F32), jax.ShapeDtypeStruct((S, D), BF16),
                   jax.ShapeDtypeStruct((8, LANES), F32), jax.ShapeDtypeStruct((1, D), F32)],
        scratch_shapes=[pltpu.VMEM((tm, D), F32)], compiler_params=_cp(2, 56))(x2, f, w, final_g, target)


def _mm_dx_norm_call(a3, wg, resid, xin, g, name, deps=()):
    na, S, Fa = a3.shape
    nb, D, bn = wg.shape
    tm = _tile(S, 512, 16)
    tk = _tile(bn, 1536, LANES)
    nsub = bn // tk
    nka = Fa // tk
    nk = nb * nsub
    assert na * nka == nk

    def body(a_ref, w_ref, r_ref, x_ref, g_ref, dx_ref, dxb_ref, dg_ref, acc):
        i, k = pl.program_id(0), pl.program_id(1)

        @pl.when(jnp.logical_and(i == 0, k == 0))
        def _():
            dg_ref[...] = jnp.zeros_like(dg_ref)

        @pl.when(k == 0)
        def _():
            acc[...] = jnp.zeros_like(acc)

        acc[...] += lax.dot_general(a_ref[...], w_ref[...], _DN_NT, preferred_element_type=F32)

        @pl.when(k == nk - 1)
        def _():
            dh = acc[...]
            xv = x_ref[...]
            r = lax.rsqrt(_rows_mean(xv * xv) + RMS_EPS)
            xn = xv * r
            dg_ref[...] += _col_sum(dh * xn)
            dyg = dh * g_ref[...]
            dx = r_ref[...] + r * (dyg - xn * _rows_mean(dyg * xn))
            dx_ref[...] = dx
            dxb_ref[...] = dx.astype(BF16)

    row = lambda i, k: (i, 0)
    return _pallas(
        body, 5, deps, name=name, grid=(S // tm, nk),
        in_specs=[pl.BlockSpec((None, tm, tk), lambda i, k: (k // nka, i, k % nka)),
                  pl.BlockSpec((None, D, tk), lambda i, k: (k // nsub, 0, k % nsub)),
                  pl.BlockSpec((tm, D), row, pipeline_mode=pl.Buffered(1)),
                  pl.BlockSpec((tm, D), row, pipeline_mode=pl.Buffered(1)), pl.BlockSpec((1, D), lambda i, k: (0, 0))],
        out_specs=[pl.BlockSpec((tm, D), row), pl.BlockSpec((tm, D), row), pl.BlockSpec((1, D), lambda i, k: (0, 0))],
        out_shape=[jax.ShapeDtypeStruct((S, D), F32), jax.ShapeDtypeStruct((S, D), BF16),
                   jax.ShapeDtypeStruct((1, D), F32)],
        scratch_shapes=[pltpu.VMEM((tm, D), F32)], compiler_params=_cp(2, 56))(a3, wg, resid, xin, g, *deps)


def _unblock_call(wg, name):
    nb, K, bn = wg.shape

    def body(w_ref, o_ref):
        o_ref[...] = w_ref[...]

    return pl.pallas_call(
        body, name=name, grid=(nb,),
        in_specs=[pl.BlockSpec((None, K, bn), lambda o: (o, 0, 0))],
        out_specs=pl.BlockSpec((K, bn), lambda o: (0, o)),
        out_shape=jax.ShapeDtypeStruct((K, nb * bn), wg.dtype), compiler_params=_cp(1))(wg)


def _mm_nt_norm_call(a, w, resid, xin, g, name, deps=(), part=(0, 1), dx_so_far=None):
    S, K = a.shape
    D = w.shape[0]
    tm = _tile(S, 256, 16)
    tiles = (S // tm) // part[1]
    first = part[0] * tiles

    def body(a_ref, w_ref, r_ref, x_ref, g_ref, *rest):
        dx_ref, dg_ref = rest[-2:]

        @pl.when(pl.program_id(0) == 0)
        def _():
            dg_ref[...] = jnp.zeros_like(dg_ref)

        dh = lax.dot_general(a_ref[...], w_ref[...], _DN_NT, preferred_element_type=F32)
        xv = x_ref[...]
        r = lax.rsqrt(_rows_mean(xv * xv) + RMS_EPS)
        xn = xv * r
        dg_ref[...] += _col_sum(dh * xn)
        dyg = dh * g_ref[...]
        dx_ref[...] = r_ref[...] + r * (dyg - xn * _rows_mean(dyg * xn))

    row = lambda i: (i + first, 0)
    fixed = lambda i: (0, 0)
    in_specs = [pl.BlockSpec((tm, K), row), pl.BlockSpec((D, K), fixed, pipeline_mode=pl.Buffered(1)),
                pl.BlockSpec((tm, D), row), pl.BlockSpec((tm, D), row), pl.BlockSpec((1, D), fixed)]
    operands = [a, w, resid, xin, g]
    aliases = {}
    if dx_so_far is not None:
        in_specs.append(pl.BlockSpec(memory_space=pl.ANY))
        operands.append(dx_so_far)
        aliases = {5: 0}
    return _pallas(
        body, len(operands), deps, name=name, grid=(tiles,),
        in_specs=in_specs, out_specs=[pl.BlockSpec((tm, D), row), pl.BlockSpec((1, D), fixed)],
        out_shape=[jax.ShapeDtypeStruct((S, D), F32), jax.ShapeDtypeStruct((1, D), F32)],
        input_output_aliases=aliases, compiler_params=_cp(1, 56))(*operands, *deps)


def _mm_tn_cols_call(a, b3, nb, bn, name, deps=()):
    S, Ka = a.shape
    nh, _, Fb = b3.shape
    tm = _tile(S, 2048, 16)
    tn = _tile(bn, 768, LANES)
    nsub = bn // tn
    njb = Fb // tn
    J = nb * nsub
    assert nh * njb == J

    def body(a_ref, b_ref, o_ref):
        @pl.when(pl.program_id(1) == 0)
        def _():
            o_ref[...] = jnp.zeros_like(o_ref)

        o_ref[...] += lax.dot_general(a_ref[...], b_ref[...], _DN_TN, preferred_element_type=F32)

    return _pallas(
        body, 2, deps, name=name, grid=(J, S // tm),
        in_specs=[pl.BlockSpec((tm, Ka), lambda j, i: (i, 0)),
                  pl.BlockSpec((None, tm, tn), lambda j, i: (j // njb, i, j % njb))],
        out_specs=pl.BlockSpec((None, Ka, tn), lambda j, i: (j // nsub, 0, j % nsub)),
        out_shape=jax.ShapeDtypeStruct((nb, Ka, bn), F32), compiler_params=_cp(2, 56))(a, b3, *deps)


def _mm_tn_rows_call(a, b, name, deps=()):
    S, E = a.shape
    D = b.shape[1]
    tm = _tile(S, 2048, 16)
    te = _tile(E, 768, LANES)

    def body(a_ref, b_ref, o_ref):
        @pl.when(pl.program_id(1) == 0)
        def _():
            o_ref[...] = jnp.zeros_like(o_ref)

        o_ref[...] += lax.dot_general(a_ref[...], b_ref[...], _DN_TN, preferred_element_type=F32)

    return _pallas(
        body, 2, deps, name=name, grid=(E // te, S // tm),
        in_specs=[pl.BlockSpec((tm, te), lambda j, i: (i, j)), pl.BlockSpec((tm, D), lambda j, i: (i, 0))],
        out_specs=pl.BlockSpec((te, D), lambda j, i: (j, 0)),
        out_shape=jax.ShapeDtypeStruct((E, D), F32), compiler_params=_cp(2, 56))(a, b, *deps)


def _ffn_tiles(S, Fd):
    return _tile(S, 512, 16), _tile(Fd // (N_DEV // 2), 1536, LANES)


def _taps(cw_ref):
    return [cw_ref[k:k + 1, :] for k in range(cw_ref.shape[0])]


def _conv3(prev8, cur, taps):
    s1 = _shift_down(prev8, cur, 1)
    s2 = _shift_down(prev8, cur, 2)
    return taps[2] * cur + taps[1] * s1 + taps[0] * s2, s1, s2


GATE_LANES = 256
SUB_LANES = 256


def _lane_taps(cw_ref, ls):
    return [cw_ref[k:k + 1, ls] for k in range(cw_ref.shape[0])]


def _ffn_up_act_call(h2, wg, cw, cb, name, deps=()):
    S, D = h2.shape
    nb, _, bn = wg.shape
    Fd = nb * bn // 2
    tm, tc = _ffn_tiles(S, Fd)
    nk = Fd // tc
    hb = tm // 16
    nsubw = bn // tc
    half = nb // 2
    sc = _tile(tc, SUB_LANES, LANES)

    def body(a_ref, ap_ref, wgate_ref, wval_ref, cwg_ref, cwv_ref, cbg_ref, cbv_ref, up_ref, upc_ref, f_ref):
        keep = jnp.where(pl.program_id(0) == 0, 0.0, 1.0)
        a_ext = jnp.concatenate([ap_ref[...], a_ref[...]], axis=0)
        nsub = tc // sc
        lanes = [slice(s * sc, (s + 1) * sc) for s in range(nsub)]

        def products(s):
            return [jnp.dot(a_ext, w_ref[:, lanes[s]], preferred_element_type=F32).astype(BF16)
                    for w_ref in (wgate_ref, wval_ref)]

        ready = products(0)
        for s in range(nsub):
            ls = lanes[s]
            following = products(s + 1) if s + 1 < nsub else None

            def conv_half(ub, cw_ref, cb_ref, slab):
                up_ref[slab, :, ls] = ub[16:]
                u = ub.astype(F32)
                conv, _, _ = _conv3(u[8:16] * keep, u[16:], _lane_taps(cw_ref, ls))
                c = conv + cb_ref[:, ls]
                upc_ref[slab, :, ls] = c.astype(BF16)
                return c

            cg = conv_half(ready[0], cwg_ref, cbg_ref, 0)
            cv = conv_half(ready[1], cwv_ref, cbv_ref, 1)
            f_ref[:, ls] = (_gelu(cg) * cv).astype(BF16)
            ready = following

    return _pallas(
        body, 8, deps, name=name, grid=(S // tm, nk),
        in_specs=[pl.BlockSpec((tm, D), lambda i, k: (i, 0)),
                  pl.BlockSpec((16, D), lambda i, k: (jnp.maximum(i * hb - 1, 0), 0)),
                  pl.BlockSpec((None, D, tc), lambda i, k: (k // nsubw, 0, k % nsubw)),
                  pl.BlockSpec((None, D, tc), lambda i, k: (half + k // nsubw, 0, k % nsubw)),
                  pl.BlockSpec((3, tc), lambda i, k: (0, k)), pl.BlockSpec((3, tc), lambda i, k: (0, k + nk)),
                  pl.BlockSpec((1, tc), lambda i, k: (0, k)), pl.BlockSpec((1, tc), lambda i, k: (0, k + nk))],
        out_specs=[pl.BlockSpec((2, tm, tc), lambda i, k: (0, i, k)), pl.BlockSpec((2, tm, tc), lambda i, k: (0, i, k)),
                   pl.BlockSpec((tm, tc), lambda i, k: (i, k))],
        out_shape=[jax.ShapeDtypeStruct((2, S, Fd), BF16), jax.ShapeDtypeStruct((2, S, Fd), BF16),
                   jax.ShapeDtypeStruct((S, Fd), BF16)],
        compiler_params=_cp(2, 56))(h2, h2, wg, wg, cw, cw, cb, cb, *deps)


def _ffn_down_dx_act_bwd_call(dxb, w, up3, upc3, cw, name, deps=()):
    S, D = dxb.shape
    _, _, Fd = up3.shape
    tm, tc = _ffn_tiles(S, Fd)
    nj = Fd // tc
    ni = S // tm
    hb = tm // 16
    sc = _tile(tc, SUB_LANES, LANES)

    def body(a_ref, an_ref, w_ref, g_ref, v_ref, cg_ref, cv_ref, cgn_ref, cvn_ref, cwg_ref, cwv_ref,
             dup_ref, dcwg_ref, dcwv_ref, dcbg_ref, dcbv_ref):
        i = pl.program_id(1)
        keep_next = jnp.where(i == ni - 1, 0.0, 1.0)

        @pl.when(i == 0)
        def _():
            for r in (dcwg_ref, dcwv_ref, dcbg_ref, dcbv_ref):
                r[...] = jnp.zeros_like(r)

        a_ext = jnp.concatenate([a_ref[...], an_ref[...]], axis=0)
        for s in range(tc // sc):
            ls = slice(s * sc, (s + 1) * sc)
            df_ext = lax.dot_general(a_ext, w_ref[s * sc:(s + 1) * sc, :], _DN_NT, preferred_element_type=F32)
            df = jnp.concatenate([df_ext[:tm], df_ext[tm:tm + 8] * keep_next], axis=0)
            cg = jnp.concatenate([cg_ref[:, ls].astype(F32), cgn_ref[:, ls].astype(F32)[:8]], axis=0)
            cv = jnp.concatenate([cv_ref[:, ls].astype(F32), cvn_ref[:, ls].astype(F32)[:8]], axis=0)
            gel, dgel = _gelu_parts(cg)

            def back(d, cw_ref, x_ref, dcw_ref, dcb_ref, slab):
                taps = _lane_taps(cw_ref, ls)
                d0 = d[:tm]
                d1 = pltpu.roll(d, tm + 8 - 1, 0)[:tm]
                d2 = pltpu.roll(d, tm + 8 - 2, 0)[:tm]
                dup_ref[slab, :, ls] = (taps[2] * d0 + taps[1] * d1 + taps[0] * d2).astype(BF16)
                xv = x_ref[:, ls].astype(F32)
                dcw_ref[2:3, ls] += _col_sum(xv * d0)
                dcw_ref[1:2, ls] += _col_sum(xv * d1)
                dcw_ref[0:1, ls] += _col_sum(xv * d2)
                dcb_ref[:, ls] += _col_sum(d0)

            back(df * cv * dgel, cwg_ref, g_ref, dcwg_ref, dcbg_ref, 0)
            back(df * gel, cwv_ref, v_ref, dcwv_ref, dcbv_ref, 1)

    nxt = lambda j, i: jnp.minimum((i + 1) * hb, S // 16 - 1)
    main = lambda s: pl.BlockSpec((None, tm, tc), lambda j, i: (s, i, j))
    halo = lambda s: pl.BlockSpec((None, 16, tc), lambda j, i: (s, nxt(j, i), j))
    acc3 = pl.BlockSpec((3, tc), lambda j, i: (0, j))
    acc1 = pl.BlockSpec((1, tc), lambda j, i: (0, j))
    return _pallas(
        body, 11, deps, name=name, grid=(nj, ni),
        in_specs=[pl.BlockSpec((tm, D), lambda j, i: (i, 0)), pl.BlockSpec((16, D), lambda j, i: (nxt(j, i), 0)),
                  pl.BlockSpec((tc, D), lambda j, i: (j, 0)),
                  main(0), main(1), main(0), main(1), halo(0), halo(1),
                  pl.BlockSpec((3, tc), lambda j, i: (0, j)), pl.BlockSpec((3, tc), lambda j, i: (0, j + nj))],
        out_specs=[pl.BlockSpec((2, tm, tc), lambda j, i: (0, i, j)), acc3, acc3, acc1, acc1],
        out_shape=[jax.ShapeDtypeStruct((2, S, Fd), BF16), jax.ShapeDtypeStruct((3, Fd), F32),
                   jax.ShapeDtypeStruct((3, Fd), F32), jax.ShapeDtypeStruct((1, Fd), F32),
                   jax.ShapeDtypeStruct((1, Fd), F32)],
        compiler_params=_cp(2, 56))(dxb, dxb, w, up3, up3, upc3, upc3, upc3, upc3, cw, cw, *deps)


def _gm_forward_tile(pv, vg, vb, ws_ref, bsb_ref, mbuf, H, nc):
    W = H * CHUNK
    z, dz = _gelu_parts(pv)
    u, v0 = z[:, :W], z[:, W:]
    xc = v0 - _rows_mean(v0)
    rs = lax.rsqrt(_rows_mean(xc * xc) + LN_EPS)
    vh = xc * rs
    vnb = (vh * vg + vb).astype(BF16)
    mask = lax.broadcasted_iota(jnp.int32, (CHUNK, CHUNK), 0) >= lax.broadcasted_iota(jnp.int32, (CHUNK, CHUNK), 1)
    for h in range(H):
        cs = slice(h * CHUNK, (h + 1) * CHUNK)
        wm = jnp.where(mask, ws_ref[h], 0.0).astype(BF16)
        vcat = jnp.concatenate([vnb[c * CHUNK:(c + 1) * CHUNK, cs] for c in range(nc)], axis=1)
        mix = jnp.dot(wm, vcat, preferred_element_type=F32)
        for c in range(nc):
            mbuf[c * CHUNK:(c + 1) * CHUNK, cs] = mix[:, c * CHUNK:(c + 1) * CHUNK] + bsb_ref[h]
    return dz, u, vh, rs, vnb, mask


def _gm_fwd_call(p, v_g, v_b, ws, bsb, out_g, name, deps=()):
    S = p.shape[0]
    H = ws.shape[0]
    W = H * CHUNK
    tm = _tile(S, 256, CHUNK)
    nc = tm // CHUNK

    def body(p_ref, vg_ref, vb_ref, ws_ref, bsb_ref, og_ref, y_ref, mbuf):
        _, u, _, _, _, _ = _gm_forward_tile(p_ref[...], vg_ref[...], vb_ref[...], ws_ref, bsb_ref, mbuf, H, nc)
        yg = u * mbuf[...]
        r = lax.rsqrt(_rows_mean(yg * yg) + RMS_EPS)
        y_ref[...] = (yg * r * og_ref[...]).astype(BF16)

    vec = pl.BlockSpec((1, W), lambda i: (0, 0))
    mat = pl.BlockSpec((H, CHUNK, CHUNK), lambda i: (0, 0, 0))
    return _pallas(
        body, 6, deps, name=name, grid=(S // tm,),
        in_specs=[pl.BlockSpec((tm, 2 * W), lambda i: (i, 0)), vec, vec, mat, mat, vec],
        out_specs=pl.BlockSpec((tm, W), lambda i: (i, 0)),
        out_shape=jax.ShapeDtypeStruct((S, 2 * W), BF16),
        scratch_shapes=[pltpu.VMEM((tm, W), F32)], compiler_params=_cp(1))(p, v_g, v_b, ws, bsb, out_g, *deps)


def _gm_bwd_call(p, d_y, v_g, v_b, ws, bsb, out_g, name, deps=()):
    S = p.shape[0]
    H = ws.shape[0]
    W = H * CHUNK
    tm = _tile(S, 256, CHUNK)
    nc = tm // CHUNK
    ni = S // tm

    def body(p_ref, dy_ref, vg_ref, vb_ref, ws_ref, bsb_ref, og_ref,
             dp_ref, dvg_ref, dvb_ref, dws_ref, dbs_ref, dog_ref, mbuf, dvbuf):
        i = pl.program_id(0)

        @pl.when(i == 0)
        def _():
            for r in (dvg_ref, dvb_ref, dws_ref, dbs_ref, dog_ref):
                r[...] = jnp.zeros_like(r)

        vg = vg_ref[...]
        dz, u, vh, rs, vnb, mask = _gm_forward_tile(p_ref[...], vg, vb_ref[...], ws_ref, bsb_ref, mbuf, H, nc)
        mixed = mbuf[...]
        yg = u * mixed
        r = lax.rsqrt(_rows_mean(yg * yg) + RMS_EPS)
        yn = yg * r
        dya = dy_ref[...]
        dog_ref[...] += _col_sum(dya * yn)
        dyg = dya * og_ref[...]
        dygm = r * (dyg - yn * _rows_mean(dyg * yn))
        du = dygm * mixed
        dmix = dygm * u
        dmb = dmix.astype(BF16)
        for h in range(H):
            cs = slice(h * CHUNK, (h + 1) * CHUNK)
            wm = jnp.where(mask, ws_ref[h], 0.0).astype(BF16)
            dcat = jnp.concatenate([dmb[c * CHUNK:(c + 1) * CHUNK, cs] for c in range(nc)], axis=1)
            vcat = jnp.concatenate([vnb[c * CHUNK:(c + 1) * CHUNK, cs] for c in range(nc)], axis=1)
            dvn = lax.dot_general(wm, dcat, _DN_TN, preferred_element_type=F32)
            dws_ref[h] += jnp.where(mask, lax.dot_general(dcat, vcat, _DN_NT, preferred_element_type=F32), 0.0)
            dbs = dmix[0:CHUNK, cs]
            for c in range(1, nc):
                dbs = dbs + dmix[c * CHUNK:(c + 1) * CHUNK, cs]
            dbs_ref[h] += dbs
            for c in range(nc):
                dvbuf[c * CHUNK:(c + 1) * CHUNK, cs] = dvn[:, c * CHUNK:(c + 1) * CHUNK]
        dvn_all = dvbuf[...]
        dvg_ref[...] += _col_sum(dvn_all * vh)
        dvb_ref[...] += _col_sum(dvn_all)
        dvh = dvn_all * vg
        dv0 = rs * (dvh - _rows_mean(dvh) - vh * _rows_mean(dvh * vh))
        dp_ref[...] = (jnp.concatenate([du, dv0], axis=1) * dz).astype(BF16)

        @pl.when(i == ni - 1)
        def _():
            for h in range(H):
                dbs_ref[h] = jnp.broadcast_to(jnp.sum(dbs_ref[h], axis=1, keepdims=True), (CHUNK, CHUNK))

    vec = pl.BlockSpec((1, W), lambda i: (0, 0))
    mat = pl.BlockSpec((H, CHUNK, CHUNK), lambda i: (0, 0, 0))
    vshape = jax.ShapeDtypeStruct((1, W), F32)
    mshape = jax.ShapeDtypeStruct((H, CHUNK, CHUNK), F32)
    return _pallas(
        body, 7, deps, name=name, grid=(ni,),
        in_specs=[pl.BlockSpec((tm, 2 * W), lambda i: (i, 0)), pl.BlockSpec((tm, W), lambda i: (i, 0)),
                  vec, vec, mat, mat, vec],
        out_specs=[pl.BlockSpec((tm, 2 * W), lambda i: (i, 0)), vec, vec, mat, mat, vec],
        out_shape=[jax.ShapeDtypeStruct((S, 4 * W), BF16), vshape, vshape, mshape, mshape, vshape],
        scratch_shapes=[pltpu.VMEM((tm, W), F32), pltpu.VMEM((tm, W), F32)],
        compiler_params=_cp(1))(p, d_y, v_g, v_b, ws, bsb, out_g, *deps)


def _lru_gates(prev8, xl, cw, cb, wa_ref, ba, wx_ref, bx, lam, H):
    sh = [_shift_down(prev8, xl, k) for k in range(4)]
    xr = cw[3] * sh[0] + cw[2] * sh[1] + cw[1] * sh[2] + cw[0] * sh[3] + cb
    xrb = xr.astype(BF16)
    rp, ip = [], []
    for h in range(H):
        cs = slice(h * CHUNK, (h + 1) * CHUNK)
        rp.append(jnp.dot(xrb[:, cs], wa_ref[h].astype(BF16), preferred_element_type=F32))
        ip.append(jnp.dot(xrb[:, cs], wx_ref[h].astype(BF16), preferred_element_type=F32))
    r = _sigmoid(jnp.concatenate(rp, axis=1) + ba)
    ig = _sigmoid(jnp.concatenate(ip, axis=1) + bx)
    sp = _softplus(-lam)
    t = jnp.tanh((-LRU_C) * r * sp)
    q = lax.rsqrt(1.0 - t)
    a = jnp.sqrt(1.0 + t) * q
    mult = jnp.sqrt(-2.0 * t) * q
    a2_over_mult = (1.0 + t) * q * lax.rsqrt(-2.0 * t)
    return xr, xrb, r, ig, sp, a, mult, a2_over_mult, sh


def _lru_fwd_call(p, cw, cb, wa, ba, wx, bx, lam, out_g, y_half, name):
    S = p.shape[0]
    H = wa.shape[0]
    W = H * CHUNK
    tm = _tile(S, 256, 16)
    ng = tm // 8

    def body(pg_ref, px_ref, cw_ref, cb_ref, wa_ref, ba_ref, wx_ref, bx_ref, lam_ref, og_ref, y_in_ref,
             y_ref, h_ref, saved_ref, xprev, hcar, abuf, bbuf):
        @pl.when(pl.program_id(0) == 0)
        def _():
            xprev[...] = jnp.zeros_like(xprev)
            hcar[...] = jnp.zeros_like(hcar)

        xl = px_ref[...]
        xr, _, r_gate, ig, _, a, mult, a2m, _ = _lru_gates(xprev[...], xl, _taps(cw_ref), cb_ref[...], wa_ref,
                                                           ba_ref[...], wx_ref, bx_ref[...], lam_ref[...], H)
        for k, val in enumerate((xr, r_gate, ig, a, mult, a2m)):
            saved_ref[k] = val
        xprev[...] = xl[tm - 8:]
        b = mult * (ig * xr)
        sub = lax.broadcasted_iota(jnp.int32, (tm, W), 0) & 7
        for d in (1, 2, 4):
            m = sub >= d
            a_s = jnp.where(m, pltpu.roll(a, d, 0), 1.0)
            b_s = jnp.where(m, pltpu.roll(b, d, 0), 0.0)
            b = a * b_s + b
            a = a * a_s
        abuf[...] = a
        bbuf[...] = b

        def step(g, carry):
            r0 = pl.multiple_of(g * 8, 8)
            h_ref[pl.ds(r0, 8), :] = abuf[pl.ds(r0, 8), :] * carry + bbuf[pl.ds(r0, 8), :]
            return jnp.broadcast_to(h_ref[pl.ds(r0 + 7, 1), :], (8, W))

        hcar[...] = lax.fori_loop(0, ng, step, hcar[...])
        yl = h_ref[...] * _gelu(pg_ref[...])
        r = lax.rsqrt(_rows_mean(yl * yl) + RMS_EPS)
        y_ref[...] = (yl * r * og_ref[...]).astype(BF16)

    vec = pl.BlockSpec((1, W), lambda i: (0, 0))
    mat = pl.BlockSpec((H, CHUNK, CHUNK), lambda i: (0, 0, 0))
    return pl.pallas_call(
        body, name=name, grid=(S // tm,),
        in_specs=[pl.BlockSpec((tm, W), lambda i: (i, 2)), pl.BlockSpec((tm, W), lambda i: (i, 3)),
                  pl.BlockSpec((4, W), lambda i: (0, 0)), vec, mat, vec, mat, vec, vec, vec,
                  pl.BlockSpec(memory_space=pl.ANY)],
        out_specs=[pl.BlockSpec((tm, W), lambda i: (i, 1)), pl.BlockSpec((tm, W), lambda i: (i, 0)),
                   pl.BlockSpec((6, tm, W), lambda i: (0, i, 0))],
        out_shape=[jax.ShapeDtypeStruct((S, 2 * W), BF16), jax.ShapeDtypeStruct((S, W), F32),
                   jax.ShapeDtypeStruct((6, S, W), F32)],
        input_output_aliases={10: 0},
        scratch_shapes=[pltpu.VMEM((8, W), F32), pltpu.VMEM((8, W), F32), pltpu.VMEM((tm, W), F32),
                        pltpu.VMEM((tm, W), F32)],
        compiler_params=_cp(1))(p, p, cw, cb, wa, ba, wx, bx, lam, out_g, y_half)


def _lru_bwd_call(p, hs, saved, d_y, cw, wa, wx, lam, out_g, dp_half, name):
    S = p.shape[0]
    H = wa.shape[0]
    W = H * CHUNK
    tm = _tile(S, 256, 16)
    ng = tm // 8
    ni = S // tm
    hb = tm // 8

    def body(pg_ref, px_ref, saved_ref, h_ref, hp_ref, dy_ref, cw_ref, wa_ref, wx_ref, lam_ref, og_ref, dp_in_ref,
             dp_ref, dcw_ref, dcb_ref, dwa_ref, dba_ref, dwx_ref, dbx_ref, dlam_ref, dog_ref,
             a_next, e_next, dxr_next, abuf, bbuf, ebuf, dxr0, dprb_buf, dpib_buf, sums):
        i = pl.program_id(0)
        ri = ni - 1 - i

        @pl.when(i == 0)
        def _():
            for r in (dcw_ref, dcb_ref, dwa_ref, dba_ref, dwx_ref, dbx_ref, dlam_ref, dog_ref,
                      a_next, e_next, dxr_next):
                r[...] = jnp.zeros_like(r)

        keep_prev = jnp.where(ri == 0, 0.0, 1.0)
        cw_ = _taps(cw_ref)
        lam_ = lam_ref[...]
        xl = px_ref[...]
        a = saved_ref[3]
        sp = _softplus(-lam_)
        sums[...] = jnp.zeros_like(sums)
        og = og_ref[...]
        lane_groups = [slice(q * GATE_LANES, (q + 1) * GATE_LANES) for q in range(W // GATE_LANES)]

        def fold(v):
            return sum(v[:, k * LANES:(k + 1) * LANES] for k in range(GATE_LANES // LANES))

        def out_grads(g, carry):
            rows = pl.ds(pl.multiple_of(g * 16, 16), 16)
            s_yy = jnp.zeros((16, LANES), F32)
            s_dy = jnp.zeros((16, LANES), F32)
            for ls in lane_groups:
                yl = h_ref[rows, ls] * _gelu(pg_ref[rows, ls])
                s_yy = s_yy + fold(yl * yl)
                s_dy = s_dy + fold(dy_ref[rows, ls] * og[:, ls] * yl)
            rr = lax.rsqrt(jnp.sum(s_yy, axis=1, keepdims=True) * (1.0 / W) + RMS_EPS)
            c = rr * rr * jnp.sum(s_dy, axis=1, keepdims=True) * (1.0 / W)
            for ls in lane_groups:
                hq = h_ref[rows, ls]
                gg, dgg = _gelu_parts(pg_ref[rows, ls])
                yl = hq * gg
                dyb = dy_ref[rows, ls]
                sums[3, :, ls] += dyb * (yl * rr)
                dyl = rr * (dyb * og[:, ls] - yl * c)
                bbuf[rows, ls] = dyl * gg
                dp_ref[rows, ls] = (dyl * hq * dgg).astype(BF16)
            return carry

        lax.fori_loop(0, tm // 16, out_grads, 0, unroll=2)
        dog_ref[...] += _col_sum(sums[3])

        an = _shift_up(a, a_next[...], 1)
        eb = bbuf[...]
        sub = lax.broadcasted_iota(jnp.int32, (tm, W), 0) & 7
        for d in (1, 2, 4):
            m = sub < 8 - d
            a_s = jnp.where(m, pltpu.roll(an, tm - d, 0), 1.0)
            e_s = jnp.where(m, pltpu.roll(eb, tm - d, 0), 0.0)
            eb = an * e_s + eb
            an = an * a_s
        abuf[...] = an
        bbuf[...] = eb

        def step(g, carry):
            r0 = pl.multiple_of((ng - 1 - g) * 8, 8)
            ebuf[pl.ds(r0, 8), :] = abuf[pl.ds(r0, 8), :] * carry + bbuf[pl.ds(r0, 8), :]
            return jnp.broadcast_to(ebuf[pl.ds(r0, 1), :], (8, W))

        lax.fori_loop(0, ng, step, jnp.broadcast_to(e_next[0:1, :], (8, W)))
        a_next[...] = a[0:8]
        e_next[...] = ebuf[0:8, :]

        row16 = lax.broadcasted_iota(jnp.int32, (16, GATE_LANES), 0)
        h_before = hp_ref[7:8, :] * keep_prev
        r_scale = (-LRU_C) * sp

        def gate_grads(g, carry):
            r0 = pl.multiple_of(g * 16, 16)
            above = jnp.maximum(r0 - 1, 0)
            for q in range(W // GATE_LANES):
                ls = slice(q * GATE_LANES, (q + 1) * GATE_LANES)
                e = ebuf[pl.ds(r0, 16), ls]
                h_prev_row = jnp.where(g == 0, h_before[:, ls], h_ref[pl.ds(above, 1), ls])
                hm1 = jnp.where(row16 == 0, h_prev_row, pltpu.roll(h_ref[pl.ds(r0, 16), ls], 1, 0))
                xr_, r_, ig_, a_, mult_, a2m_ = [saved_ref[k, pl.ds(r0, 16), ls] for k in range(6)]
                em = e * mult_
                dxr0[pl.ds(r0, 16), ls] = em * ig_
                dla = e * hm1 * a_ - e * ig_ * xr_ * a2m_
                dpr = dla * r_scale[:, ls] * r_ * (1.0 - r_)
                dpi = em * xr_ * ig_ * (1.0 - ig_)
                sums[0, :, ls] += dla * r_
                sums[1, :, ls] += dpr
                sums[2, :, ls] += dpi
                dprb_buf[pl.ds(r0, 16), ls] = dpr.astype(BF16)
                dpib_buf[pl.ds(r0, 16), ls] = dpi.astype(BF16)
            return carry

        lax.fori_loop(0, tm // 16, gate_grads, 0)
        dlam_ref[...] += (-LRU_C) * _col_sum(sums[0])
        dba_ref[...] += _col_sum(sums[1])
        dbx_ref[...] += _col_sum(sums[2])
        dprb = dprb_buf[...]
        dpib = dpib_buf[...]
        dxr = dxr0[...]
        xrb = saved_ref[0].astype(BF16)
        back = []
        for h in range(H):
            cs = slice(h * CHUNK, (h + 1) * CHUNK)
            wab = wa_ref[h].astype(BF16)
            wxb = wx_ref[h].astype(BF16)
            back.append(lax.dot_general(dprb[:, cs], wab, _DN_NT, preferred_element_type=F32)
                        + lax.dot_general(dpib[:, cs], wxb, _DN_NT, preferred_element_type=F32))
            dwa_ref[h] += lax.dot_general(xrb[:, cs], dprb[:, cs], _DN_TN, preferred_element_type=F32)
            dwx_ref[h] += lax.dot_general(xrb[:, cs], dpib[:, cs], _DN_TN, preferred_element_type=F32)
        dxr = dxr + jnp.concatenate(back, axis=1)

        nxt = dxr_next[...]
        ahead = [_shift_up(dxr, nxt, j) for j in range(4)]
        dxl = cw_[3] * ahead[0] + cw_[2] * ahead[1] + cw_[1] * ahead[2] + cw_[0] * ahead[3]
        dxr_next[...] = dxr[0:8]
        for k in range(4):
            dcw_ref[k:k + 1, :] += _col_sum(xl * ahead[3 - k])
        dcb_ref[...] += _col_sum(dxr)
        dp_ref[:, W:] = dxl.astype(BF16)

        @pl.when(i == ni - 1)
        def _():
            dlam_ref[...] = -dlam_ref[...] * _sigmoid(-lam_)

    vec = pl.BlockSpec((1, W), lambda i: (0, 0))
    mat = pl.BlockSpec((H, CHUNK, CHUNK), lambda i: (0, 0, 0))
    rev = lambda i: ni - 1 - i
    prev = lambda i: jnp.maximum(rev(i) * hb - 1, 0)
    vshape = jax.ShapeDtypeStruct((1, W), F32)
    mshape = jax.ShapeDtypeStruct((H, CHUNK, CHUNK), F32)
    tile = lambda: pltpu.VMEM((tm, W), F32)
    car = lambda: pltpu.VMEM((8, W), F32)
    return pl.pallas_call(
        body, name=name, grid=(ni,),
        in_specs=[pl.BlockSpec((tm, W), lambda i: (rev(i), 2)), pl.BlockSpec((tm, W), lambda i: (rev(i), 3)),
                  pl.BlockSpec((6, tm, W), lambda i: (0, rev(i), 0)),
                  pl.BlockSpec((tm, W), lambda i: (rev(i), 0)), pl.BlockSpec((8, W), lambda i: (prev(i), 0)),
                  pl.BlockSpec((tm, W), lambda i: (rev(i), 1)),
                  pl.BlockSpec((4, W), lambda i: (0, 0)), mat, mat, vec, vec,
                  pl.BlockSpec(memory_space=pl.ANY)],
        out_specs=[pl.BlockSpec((tm, 2 * W), lambda i: (rev(i), 1)), pl.BlockSpec((4, W), lambda i: (0, 0)), vec,
                   mat, vec, mat, vec, vec, vec],
        out_shape=[jax.ShapeDtypeStruct((S, 4 * W), BF16), jax.ShapeDtypeStruct((4, W), F32), vshape,
                   mshape, vshape, mshape, vshape, vshape, vshape],
        input_output_aliases={11: 0},
        scratch_shapes=[car(), car(), car(), tile(), tile(), tile(), tile(), pltpu.VMEM((tm, W), BF16),
                        pltpu.VMEM((tm, W), BF16), pltpu.VMEM((4, 16, W), F32)],
        compiler_params=_cp(1, 56))(p, p, saved, hs, hs, d_y, cw, wa, wx, lam, out_g, dp_half)


def _rows128(a):
    return a.reshape(-1, LANES).astype(F32)


def _pack(arrays, pad_to=256):
    flat = jnp.concatenate([_rows128(a) for a in arrays], axis=0)
    pad = (-flat.shape[0]) % pad_to
    if pad:
        flat = jnp.concatenate([flat, jnp.zeros((pad, LANES), F32)], axis=0)
    return flat


def _unpack(flat, shapes):
    out, r = [], 0
    for s in shapes:
        n = 1
        for d in s:
            n *= d
        out.append(flat[r:r + n // LANES].reshape(s))
        r += n // LANES
    return out


def kernel(x, norm1_g, w_in, gm_v_g, gm_v_b, gm_ws, gm_bs, lru_conv_w, lru_conv_b, lru_wa, lru_ba, lru_wx, lru_bx, lru_lambda, gm_out_g, lru_out_g, w_out, norm2_g, ffn_w_up, ffn_conv_w, ffn_conv_b, ffn_w_down, final_g, loss_target, m_norm1_g, m_w_in, m_gm_v_g, m_gm_v_b, m_gm_ws, m_gm_bs, m_lru_conv_w, m_lru_conv_b, m_lru_wa, m_lru_ba, m_lru_wx, m_lru_bx, m_lru_lambda, m_gm_out_g, m_lru_out_g, m_w_out, m_norm2_g, m_ffn_w_up, m_ffn_conv_w, m_ffn_conv_b, m_ffn_w_down, m_final_g, v_norm1_g, v_w_in, v_gm_v_g, v_gm_v_b, v_gm_ws, v_gm_bs, v_lru_conv_w, v_lru_conv_b, v_lru_wa, v_lru_ba, v_lru_wx, v_lru_bx, v_lru_lambda, v_gm_out_g, v_lru_out_g, v_w_out, v_norm2_g, v_ffn_w_up, v_ffn_conv_w, v_ffn_conv_b, v_ffn_w_down, v_final_g):
    wts = dict(norm1_g=norm1_g, w_in=w_in, gm_v_g=gm_v_g, gm_v_b=gm_v_b, gm_ws=gm_ws, gm_bs=gm_bs,
               lru_conv_w=lru_conv_w, lru_conv_b=lru_conv_b, lru_wa=lru_wa, lru_ba=lru_ba, lru_wx=lru_wx,
               lru_bx=lru_bx, lru_lambda=lru_lambda, gm_out_g=gm_out_g, lru_out_g=lru_out_g, w_out=w_out,
               norm2_g=norm2_g, ffn_w_up=ffn_w_up, ffn_conv_w=ffn_conv_w, ffn_conv_b=ffn_conv_b,
               ffn_w_down=ffn_w_down, final_g=final_g)
    mom = dict(norm1_g=m_norm1_g, w_in=m_w_in, gm_v_g=m_gm_v_g, gm_v_b=m_gm_v_b, gm_ws=m_gm_ws, gm_bs=m_gm_bs,
               lru_conv_w=m_lru_conv_w, lru_conv_b=m_lru_conv_b, lru_wa=m_lru_wa, lru_ba=m_lru_ba, lru_wx=m_lru_wx,
               lru_bx=m_lru_bx, lru_lambda=m_lru_lambda, gm_out_g=m_gm_out_g, lru_out_g=m_lru_out_g, w_out=m_w_out,
               norm2_g=m_norm2_g, ffn_w_up=m_ffn_w_up, ffn_conv_w=m_ffn_conv_w, ffn_conv_b=m_ffn_conv_b,
               ffn_w_down=m_ffn_w_down, final_g=m_final_g)
    var = dict(norm1_g=v_norm1_g, w_in=v_w_in, gm_v_g=v_gm_v_g, gm_v_b=v_gm_v_b, gm_ws=v_gm_ws, gm_bs=v_gm_bs,
               lru_conv_w=v_lru_conv_w, lru_conv_b=v_lru_conv_b, lru_wa=v_lru_wa, lru_ba=v_lru_ba, lru_wx=v_lru_wx,
               lru_bx=v_lru_bx, lru_lambda=v_lru_lambda, gm_out_g=v_gm_out_g, lru_out_g=v_lru_out_g, w_out=v_w_out,
               norm2_g=v_norm2_g, ffn_w_up=v_ffn_w_up, ffn_conv_w=v_ffn_conv_w, ffn_conv_b=v_ffn_conv_b,
               ffn_w_down=v_ffn_w_down, final_g=v_final_g)

    xi, yi, ci = lax.axis_index("x"), lax.axis_index("y"), lax.axis_index("c")
    chip = 2 * xi + yi
    dev = 2 * chip + ci
    core_chip = jnp.stack([ci, chip]).astype(jnp.int32)

    xs = x[0]
    tgt = loss_target[0]
    S, D = xs.shape
    H = gm_ws.shape[1]
    W = H * CHUNK
    Fd = ffn_w_down.shape[1] * N_DEV
    lcw_cols = lru_conv_w.shape[2]
    fcw_cols = ffn_conv_w.shape[2]

    dev1 = jnp.reshape(dev, (1,)).astype(jnp.int32)

    def place_own(shards, name):
        return [_place_own_call(s, dev1, dt, "%s_own%d" % (name, k)) for k, (s, dt) in enumerate(shards)]

    def gather_start(shards, name, after=()):
        lands = place_own(shards, name)
        return _exchange_start([], lands, 4 * len(lands), _gather_stage1_copies(len(lands)), name + "_ici", after)

    def gather_forward(lands, name, after=()):
        return _exchange_start([], lands, 3 * len(lands), _gather_stage2_copies(len(lands)), name + "_d2d", after)

    def pair_start(g, name, after=()):
        return _exchange_start([g], [lax.empty((4,) + g.shape[1:], F32)], 4, _pair_copies(1), name, after)

    def chip_start(p16, name, after=()):
        return _exchange_start([p16], [lax.empty((3,) + p16.shape[1:], BF16)], 3, _chip_copies(1), name, after)

    vgm_g, vgm_b = gm_v_g, gm_v_b
    ws, wa, wx = gm_ws[0], lru_wa[0], lru_wx[0]
    bsb = jnp.broadcast_to(gm_bs[0][:, :, None], (H, CHUNK, CHUNK))
    ba, bx = lru_ba.reshape(1, W), lru_bx.reshape(1, W)
    fcb = ffn_conv_b
    fing = final_g.reshape(1, D)

    conv_pack = _pack([lru_conv_w[0], ffn_conv_w[0]], pad_to=8)
    lands = place_own([(w_in[0], BF16), (conv_pack, F32)], "gather_in")
    ga_pair = _exchange_start([], lands, 2, _gather_stage1_copies(2, to_chips=False), "gather_in_pair")
    ga1 = _exchange_start([], ga_pair.bufs, 6, _gather_stage1_copies(2, to_sibling=False), "gather_in_ici")
    h1 = _rmsnorm_call(xs, norm1_g, "norm1", deps=(ga1.token,))
    ga_pair.bufs = ga1.bufs
    _, la = _exchange_wait(ga_pair, after=(h1,))
    own_blocks = jnp.stack([dev, dev + 1 - 2 * ci]).astype(jnp.int32)
    other_blocks = ((2 * chip + 2 + jnp.arange(N_DEV - 2)) % N_DEV).astype(jnp.int32)
    p_own = _mm_some_blocks_call(h1, la[0], own_blocks, F32, "in_proj_own")
    ga1.bufs = la
    _, la = _exchange_wait(ga1, after=(p_own,))
    ga2 = gather_forward(la, "gather_in")
    gb1 = gather_start([(w_out[0], BF16)], "gather_out", after=(ga2.token,))
    gc1 = gather_start([(ffn_w_up[0], BF16)], "gather_up", after=(gb1.token,))
    gd1 = gather_start([(ffn_w_down[0], BF16)], "gather_down", after=(gc1.token,))
    _, (win_g, conv_g) = _exchange_wait(ga2, after=(gd1.token,))
    n_l = 4 * lcw_cols // LANES
    n_f = 3 * fcw_cols // LANES
    lcw = conv_g[:, :n_l].reshape(N_DEV, 4, lcw_cols).transpose(1, 0, 2).reshape(4, N_DEV * lcw_cols)
    fcw = conv_g[:, n_l:n_l + n_f].reshape(N_DEV, 3, fcw_cols).transpose(1, 0, 2).reshape(3, N_DEV * fcw_cols)

    p = _mm_some_blocks_call(h1, win_g, other_blocks, F32, "in_proj_rest", out_so_far=p_own)
    win_rows = _unblock_call(win_g, "w_in_rows")
    _, lb = _exchange_wait(gb1, after=(p,))
    gb2 = gather_forward(lb, "gather_out")
    y_half = _gm_fwd_call(p, vgm_g, vgm_b, ws, bsb, gm_out_g, "gmlp_fwd", deps=(gb2.token,))
    y, hs, lru_saved = _lru_fwd_call(p, lcw, lru_conv_b, wa, ba, wx, bx, lru_lambda, lru_out_g, y_half, "lru_fwd")
    _, lc = _exchange_wait(gc1, after=(y,))
    gc2 = gather_forward(lc, "gather_up")
    _, (wout_g,) = _exchange_wait(gb2, after=(y, gc2.token))
    wout_full = wout_g.reshape(D, D)
    x2 = _mm_out_call(xs, y, wout_full, "out_proj")
    h2 = _rmsnorm_call(x2, norm2_g, "norm2")
    _, (wup_g,) = _exchange_wait(gc2, after=(h2,))
    _, ld = _exchange_wait(gd1, after=(h2,))
    gd2 = gather_forward(ld, "gather_down")
    up3, upc3, f = _ffn_up_act_call(h2, wup_g, fcw, fcb, "ffn_up", deps=(gd2.token,))
    _, (wdown_g,) = _exchange_wait(gd2, after=(f,))
    wdown_full = wdown_g.reshape(Fd, D)
    dx3, dx3b, loss_acc, d_final = _mm_down_loss_call(x2, f, wdown_full, fing, tgt, "ffn_down_loss")

    g_wdown = _mm_tn_rows_call(f, dx3b, "ffn_down_dw").reshape((N_DEV,) + ffn_w_down.shape[1:])
    pd = pair_start(g_wdown, "pair_down")
    d_up3, dfcw_g, dfcw_v, dfcb_g, dfcb_v = _ffn_down_dx_act_bwd_call(dx3b, wdown_full, up3, upc3, fcw, "ffn_down_dx",
                                                                     deps=(pd.token,))
    (g_wdown,), (r1,) = _exchange_wait(pd, after=(d_up3,))
    own_down, p16 = _pair_add_call(g_wdown, r1, core_chip, "pair_add_down")
    cd = chip_start(p16, "chip_down")
    g_wup = _mm_tn_cols_call(h2, d_up3, N_DEV, ffn_w_up.shape[2], "ffn_up_dw", deps=(cd.token,))
    pu = pair_start(g_wup, "pair_up")
    dx2, dx2b, d_norm2 = _mm_dx_norm_call(d_up3, wup_g, dx3, x2, norm2_g, "ffn_up_dx", deps=(pu.token,))
    g_wout = _mm_tn_rows_call(y, dx2b, "out_proj_dw").reshape((N_DEV,) + w_out.shape[1:])
    po = pair_start(g_wout, "pair_out")
    d_y = _mm_nt_call(dx2b, wout_full, F32, "out_proj_dx", deps=(po.token,))
    (g_wup,), (r1,) = _exchange_wait(pu, after=(d_y,))
    own_up, p16 = _pair_add_call(g_wup, r1, core_chip, "pair_add_up")
    _, (r2_down,) = _exchange_wait(cd, after=(p16,))
    cu = chip_start(p16, "chip_up", after=(r2_down,))
    dp_half, d_vg, d_vb, d_ws, d_bs, d_gog = _gm_bwd_call(p, d_y, vgm_g, vgm_b, ws, bsb, gm_out_g, "gmlp_bwd",
                                                          deps=(cu.token,))
    d_p2, d_lcw, d_lcb, d_wa, d_ba, d_wx, d_bx, d_lam, d_log = _lru_bwd_call(
        p, hs, lru_saved, d_y, lcw, wa, wx, lru_lambda, lru_out_g, dp_half, "lru_bwd")
    d_p = d_p2[None]
    (g_wout,), (r1,) = _exchange_wait(po, after=(d_p,))
    own_out, p16_out = _pair_add_call(g_wout, r1, core_chip, "pair_add_out")
    g_win = _mm_tn_cols_call(h1, d_p, N_DEV, w_in.shape[2], "in_proj_dw")
    pi = pair_start(g_win, "pair_in")
    _, (r2_up,) = _exchange_wait(cu, after=(g_win,))
    co = chip_start(p16_out, "chip_out", after=(r2_up,))
    gx_a, dn_a = _mm_nt_norm_call(d_p[0], win_rows, dx2, xs, norm1_g, "in_proj_dx_a", deps=(co.token, pi.token),
                                  part=(0, 2))
    (g_win,), (r1,) = _exchange_wait(pi, after=(gx_a,))
    own_in, p16 = _pair_add_call(g_win, r1, core_chip, "pair_add_in")
    _, (r2_out,) = _exchange_wait(co, after=(p16,))
    ci_ = chip_start(p16, "chip_in", after=(r2_out,))
    grad_x, dn_b = _mm_nt_norm_call(d_p[0], win_rows, dx2, xs, norm1_g, "in_proj_dx_b", deps=(ci_.token,),
                                    part=(1, 2), dx_so_far=gx_a)

    small_g = dict(norm1_g=dn_a + dn_b, gm_v_g=d_vg, gm_v_b=d_vb, gm_ws=d_ws, gm_bs=d_bs[:, :, 0], lru_conv_b=d_lcb,
                   lru_wa=d_wa, lru_ba=d_ba, lru_wx=d_wx, lru_bx=d_bx, lru_lambda=d_lam, gm_out_g=d_gog,
                   lru_out_g=d_log, norm2_g=d_norm2,
                   ffn_conv_b=jnp.concatenate([dfcb_g, dfcb_v], axis=1), final_g=d_final)
    rep = _pack([small_g[n] for n in SMALL])
    conv_part = _pack([d_lcw, jnp.concatenate([dfcw_g, dfcw_v], axis=1)], pad_to=8)
    n_rep, n_conv = rep.shape[0], conv_part.shape[0]
    gs1 = gather_start([(jnp.concatenate([rep, conv_part], axis=0), F32)], "gather_small")

    def adamw_big(n, own, r2, deps=()):
        return _adamw_call(wts[n][0], mom[n][0], var[n][0], [(own, None), (r2, 0), (r2, 1), (r2, 2)], "adamw_" + n, deps)

    res = {}
    res["ffn_w_down"] = adamw_big("ffn_w_down", own_down, r2_down, (gs1.token,))
    res["ffn_w_up"] = adamw_big("ffn_w_up", own_up, r2_up, (gs1.token,))
    res["w_out"] = adamw_big("w_out", own_out, r2_out, (gs1.token,))
    _, ls = _exchange_wait(gs1, after=(res["w_out"][0], res["ffn_w_up"][0], res["ffn_w_down"][0]))
    gs2 = gather_forward(ls, "gather_small")
    _, (r2_in,) = _exchange_wait(ci_, after=(gs2.token,))
    res["w_in"] = adamw_big("w_in", own_in, r2_in)
    _, (parts,) = _exchange_wait(gs2, after=(res["w_in"][0],))
    g_rep, d_rep, m_rep, v_rep = _adamw_call(
        _pack([wts[n] for n in SMALL]), _pack([mom[n] for n in SMALL]), _pack([var[n] for n in SMALL]),
        [(parts, k) for k in range(N_DEV)], "adamw_small")
    shapes = [wts[n].shape for n in SMALL]
    for n, g_, d_, m_, v_ in zip(SMALL, _unpack(g_rep, shapes), _unpack(d_rep, shapes), _unpack(m_rep, shapes),
                                 _unpack(v_rep, shapes)):
        res[n] = (g_, d_, m_, v_)
    conv_sum = _sum_call(parts, n_rep, n_conv, "sum_conv_grads")
    g_lcw = conv_sum[:4 * W // LANES].reshape(4, W)
    g_fcw = conv_sum[4 * W // LANES:4 * W // LANES + 6 * Fd // LANES].reshape(3, 2 * Fd)
    for n, full in (("lru_conv_w", g_lcw), ("ffn_conv_w", g_fcw)):
        cols = wts[n].shape[2]
        mine = lax.dynamic_slice_in_dim(full, dev * cols, cols, axis=1)
        res[n] = _adamw_call(wts[n][0], mom[n][0], var[n][0], [(mine, None)], "adamw_" + n)

    loss = lax.psum(loss_acc[0, 0], ("x", "y", "c"))
    outs = [[], [], [], []]
    for n in WEIGHTS:
        for k in range(4):
            outs[k].append(res[n][k].reshape(wts[n].shape))
    return (loss, grad_x[None], *outs[0], *outs[1], *outs[2], *outs[3])
```

```python
import functools
import math

import jax
import jax.numpy as jnp
from jax import lax
from jax.experimental import pallas as pl
from jax.experimental.pallas import tpu as pltpu

F32 = jnp.float32
BF16 = jnp.bfloat16

RMS_EPS = 1e-6
LN_EPS = 1e-5
LRU_C = 8.0
CHUNK = 128
ADAM_LR = 0.001
ADAM_B1 = 0.9
ADAM_B2 = 0.999
ADAM_EPS = 1e-08
ADAM_WD = 0.01
ADAM_STEP = 10
N_DEV = 8
LANES = 128
MIB = 1024 * 1024

WEIGHTS = ['norm1_g', 'w_in', 'gm_v_g', 'gm_v_b', 'gm_ws', 'gm_bs', 'lru_conv_w', 'lru_conv_b', 'lru_wa', 'lru_ba',
           'lru_wx', 'lru_bx', 'lru_lambda', 'gm_out_g', 'lru_out_g', 'w_out', 'norm2_g', 'ffn_w_up', 'ffn_conv_w',
           'ffn_conv_b', 'ffn_w_down', 'final_g']
BIG = ['w_in', 'w_out', 'ffn_w_up', 'ffn_w_down']
CONV = ['lru_conv_w', 'ffn_conv_w']
SMALL = [n for n in WEIGHTS if n not in BIG and n not in CONV]

_DN_NT = (((1,), (1,)), ((), ()))
_DN_TN = (((0,), (0,)), ((), ()))
_GELU_C = 0.7978845608028654


def _cp(n_axes, vmem_mib=48):
    return pltpu.CompilerParams(dimension_semantics=("arbitrary",) * n_axes, vmem_limit_bytes=vmem_mib * MIB)


def _tile(n, pref, mult=8):
    t = min(pref, n)
    t -= t % mult
    while t >= mult:
        if n % t == 0:
            return t
        t -= mult
    return n


def _gelu_gate(z, z2):
    return 0.5 * jnp.tanh(z * ((_GELU_C * 0.044715) * z2 + _GELU_C)) + 0.5


def _gelu(z):
    return z * _gelu_gate(z, z * z)


def _gelu_parts(z):
    z2 = z * z
    s = _gelu_gate(z, z2)
    g = z * s
    dg = s + g * (1.0 - s) * ((6.0 * _GELU_C * 0.044715) * z2 + 2.0 * _GELU_C)
    return g, dg


def _sigmoid(z):
    return 0.5 + 0.5 * jnp.tanh(0.5 * z)


def _softplus(z):
    t = jnp.exp(-jnp.abs(z))
    u = 1.0 + t
    log1p = jnp.where(u == 1.0, t, jnp.log(u) * t / (u - 1.0))
    return jnp.maximum(z, 0.0) + log1p


def _rows_mean(v):
    return jnp.mean(v, axis=-1, keepdims=True)


def _col_sum(v):
    return jnp.sum(v, axis=0, keepdims=True)


def _shift_down(prev8, cur, k):
    if k == 0:
        return cur
    z = jnp.concatenate([prev8, cur], axis=0)
    return pltpu.roll(z, k, 0)[8:]


def _shift_up(cur, next8, k):
    if k == 0:
        return cur
    n = cur.shape[0]
    z = jnp.concatenate([cur, next8], axis=0)
    return pltpu.roll(z, n + 8 - k, 0)[:n]


def _mesh_pos():
    return lax.axis_index("x"), lax.axis_index("y"), lax.axis_index("c")


def _any_specs(n):
    return [pl.BlockSpec(memory_space=pl.ANY)] * n


def _pallas(body, n_in, deps, **kw):
    nd = len(deps)
    if not nd:
        return pl.pallas_call(body, **kw)

    def ordered(*refs):
        body(*refs[:n_in], *refs[n_in + nd:])

    kw["in_specs"] = list(kw["in_specs"]) + _any_specs(nd)
    return pl.pallas_call(ordered, **kw)


_HBM = pl.BlockSpec(memory_space=pltpu.HBM)
_SEM = pl.BlockSpec(memory_space=pltpu.SEMAPHORE)
_EFFECT = pltpu.SideEffectType.DATAFLOW_SIDE_EFFECTING


class _InFlight:
    def __init__(self, sems, bufs, token, n_src, n_copies, make_copies, name):
        self.sems, self.bufs, self.token = sems, bufs, token
        self.n_src, self.n_copies, self.make_copies, self.name = n_src, n_copies, make_copies, name


def _exchange_start(srcs, lands, n_copies, make_copies, name, after=()):
    bufs = list(srcs) + list(lands)
    nb, na = len(bufs), len(after)
    ns = len(srcs)

    def body(*refs):
        b_refs = refs[:nb]
        outs = refs[nb + na:]
        send, recv = outs[:n_copies], outs[n_copies:2 * n_copies]
        token = outs[-1]
        for cp in make_copies(b_refs[:ns], b_refs[ns:], send, recv):
            cp.start()
        token[...] = jnp.zeros_like(token)

    out = pl.pallas_call(
        body, name=name,
        out_shape=[pltpu.SemaphoreType.DMA(())] * (2 * n_copies) + [pltpu.HBM(b.shape, b.dtype) for b in bufs]
        + [jax.ShapeDtypeStruct((8, LANES), F32)],
        in_specs=[_HBM] * nb + _any_specs(na),
        out_specs=[_SEM] * (2 * n_copies) + [_HBM] * nb + [pl.BlockSpec(memory_space=pltpu.VMEM)],
        input_output_aliases={i: 2 * n_copies + i for i in range(nb)},
        compiler_params=pltpu.CompilerParams(has_side_effects=_EFFECT),
    )(*[pltpu.with_memory_space_constraint(b, pltpu.HBM) for b in bufs], *after)
    return _InFlight(out[:2 * n_copies], out[2 * n_copies:2 * n_copies + nb], out[-1], ns, n_copies, make_copies, name)


def _exchange_wait(fl, after=()):
    nb, na, nc, ns = len(fl.bufs), len(after), fl.n_copies, fl.n_src

    def body(*refs):
        b_refs = refs[:nb]
        sems = refs[nb:nb + 2 * nc]
        copies = fl.make_copies(b_refs[:ns], b_refs[ns:], sems[:nc], sems[nc:])
        for cp in copies:
            cp.wait_send()
        for cp in copies:
            cp.wait_recv()

    out = pl.pallas_call(
        body, name=fl.name + "_wait",
        out_shape=[pltpu.HBM(b.shape, b.dtype) for b in fl.bufs],
        in_specs=[_HBM] * nb + [_SEM] * (2 * nc) + _any_specs(na),
        out_specs=[_HBM] * nb,
        input_output_aliases={i: i for i in range(nb)},
        compiler_params=pltpu.CompilerParams(has_side_effects=_EFFECT),
    )(*fl.bufs, *fl.sems, *after)
    return list(out[:ns]), list(out[ns:])


def _remote(src, dst, send_sem, recv_sem, to):
    return pltpu.make_async_remote_copy(src_ref=src, dst_ref=dst, send_sem=send_sem, recv_sem=recv_sem,
                                        device_id=to, device_id_type=pl.DeviceIdType.MESH)


def _gather_stage1_copies(n, to_sibling=True, to_chips=True):
    def make(s_refs, l_refs, send, recv):
        x, y, c = _mesh_pos()
        own = 4 * x + 2 * y + c
        targets = ([(x, y, 1 - c)] if to_sibling else []) + (
            [(1 - x, y, c), (x, 1 - y, c), (1 - x, 1 - y, c)] if to_chips else [])
        m = len(targets)
        return [_remote(l_refs[a].at[own], l_refs[a].at[own], send[m * a + k], recv[m * a + k], to)
                for a in range(n) for k, to in enumerate(targets)]
    return make


def _gather_stage2_copies(n):
    def make(s_refs, l_refs, send, recv):
        x, y, c = _mesh_pos()
        blocks = [4 * (1 - x) + 2 * y + c, 4 * x + 2 * (1 - y) + c, 4 * (1 - x) + 2 * (1 - y) + c]
        return [_remote(l_refs[a].at[b], l_refs[a].at[b], send[3 * a + j], recv[3 * a + j], (x, y, 1 - c))
                for a in range(n) for j, b in enumerate(blocks)]
    return make


def _pair_copies(n):
    def make(s_refs, l_refs, send, recv):
        x, y, c = _mesh_pos()
        return [_remote(s_refs[a].at[2 * k + 1 - c], l_refs[a].at[k], send[4 * a + k], recv[4 * a + k], (x, y, 1 - c))
                for a in range(n) for k in range(4)]
    return make


def _chip_copies(n):
    def make(s_refs, l_refs, send, recv):
        x, y, c = _mesh_pos()
        chips = [(1 - x, y), (x, 1 - y), (1 - x, 1 - y)]
        return [_remote(s_refs[a].at[2 * ch[0] + ch[1]], l_refs[a].at[j], send[3 * a + j], recv[3 * a + j], (*ch, c))
                for a in range(n) for j, ch in enumerate(chips)]
    return make


def _place_own_call(shard, dev, dtype, name):
    R, C = shard.shape
    tr = _tile(R, max(16, MIB // (C * 4)), 16)

    def body(d_ref, s_ref, o_ref):
        o_ref[...] = s_ref[...].astype(dtype)

    grid_spec = pltpu.PrefetchScalarGridSpec(
        num_scalar_prefetch=1, grid=(R // tr,),
        in_specs=[pl.BlockSpec((tr, C), lambda r, d: (r, 0))],
        out_specs=pl.BlockSpec((None, tr, C), lambda r, d: (d[0], r, 0)))
    return pl.pallas_call(body, name=name, grid_spec=grid_spec,
                          out_shape=jax.ShapeDtypeStruct((N_DEV, R, C), dtype), compiler_params=_cp(1))(dev, shard)


def _pair_add_call(g, r1, core_chip, name):
    _, R, C = g.shape
    tr = _tile(R, max(16, (2 * MIB) // (C * 4)), 16)

    def body(cc_ref, g_ref, r_ref, p32_ref, p16_ref):
        s = g_ref[...] + r_ref[...]
        p16_ref[...] = s.astype(BF16)

        @pl.when(pl.program_id(1) == cc_ref[1])
        def _():
            p32_ref[...] = s

    grid_spec = pltpu.PrefetchScalarGridSpec(
        num_scalar_prefetch=1, grid=(R // tr, 4),
        in_specs=[pl.BlockSpec((None, tr, C), lambda r, k, cc: (2 * k + cc[0], r, 0)),
                  pl.BlockSpec((None, tr, C), lambda r, k, cc: (k, r, 0))],
        out_specs=[pl.BlockSpec((tr, C), lambda r, k, cc: (r, 0)),
                   pl.BlockSpec((None, tr, C), lambda r, k, cc: (k, r, 0))])
    return pl.pallas_call(
        body, name=name, grid_spec=grid_spec,
        out_shape=[jax.ShapeDtypeStruct((R, C), F32), jax.ShapeDtypeStruct((4, R, C), BF16)],
        compiler_params=_cp(2))(core_chip, g, r1)


def _adamw_call(w, m, v, addends, name, deps=()):
    R, C = w.shape
    tr = _tile(R, max(8, (MIB // 2) // (C * 4)), 16)
    na = len(addends)
    c1 = 1.0 - ADAM_B1 ** ADAM_STEP
    c2 = 1.0 - ADAM_B2 ** ADAM_STEP

    def body(*refs):
        w_ref, m_ref, v_ref = refs[:3]
        a_refs = refs[3:3 + na]
        g_ref, d_ref, nm_ref, nv_ref = refs[3 + na:]
        g = a_refs[0][...].astype(F32)
        for a_ref in a_refs[1:]:
            g = g + a_ref[...].astype(F32)
        nm = ADAM_B1 * m_ref[...] + (1.0 - ADAM_B1) * g
        nv = ADAM_B2 * v_ref[...] + (1.0 - ADAM_B2) * (g * g)
        g_ref[...] = g
        nm_ref[...] = nm
        nv_ref[...] = nv
        d_ref[...] = -ADAM_LR * ((nm / c1) / (jnp.sqrt(nv / c2) + ADAM_EPS) + ADAM_WD * w_ref[...])

    flat = pl.BlockSpec((tr, C), lambda r: (r, 0))
    a_specs = [flat if k is None else pl.BlockSpec((None, tr, C), functools.partial(lambda r, kk: (kk, r, 0), kk=k))
               for _, k in addends]
    out = jax.ShapeDtypeStruct((R, C), F32)
    return _pallas(
        body, 3 + na, deps, name=name, grid=(R // tr,),
        in_specs=[flat, flat, flat] + a_specs, out_specs=[flat] * 4, out_shape=[out] * 4,
        compiler_params=_cp(1))(w, m, v, *[a for a, _ in addends], *deps)


def _sum_call(parts, row0, rows, name):
    n = parts.shape[0]
    tr = _tile(math.gcd(row0, rows), 256, 8)
    b0 = row0 // tr

    def body(p_ref, o_ref):
        s = p_ref[0]
        for k in range(1, n):
            s = s + p_ref[k]
        o_ref[...] = s

    return pl.pallas_call(
        body, name=name, grid=(rows // tr,),
        in_specs=[pl.BlockSpec((n, tr, LANES), lambda r: (0, r + b0, 0))],
        out_specs=pl.BlockSpec((tr, LANES), lambda r: (r, 0)),
        out_shape=jax.ShapeDtypeStruct((rows, LANES), F32), compiler_params=_cp(1))(parts)


def _rmsnorm_call(x, g, name, deps=()):
    S, D = x.shape
    tm = _tile(S, 512, 16)

    def body(x_ref, g_ref, o_ref):
        xv = x_ref[...]
        r = lax.rsqrt(_rows_mean(xv * xv) + RMS_EPS)
        o_ref[...] = (xv * r * g_ref[...]).astype(BF16)

    return _pallas(
        body, 2, deps, name=name, grid=(S // tm,),
        in_specs=[pl.BlockSpec((tm, D), lambda i: (i, 0)), pl.BlockSpec((1, D), lambda i: (0, 0))],
        out_specs=pl.BlockSpec((tm, D), lambda i: (i, 0)),
        out_shape=jax.ShapeDtypeStruct((S, D), BF16), compiler_params=_cp(1))(x, g, *deps)


def _mm_some_blocks_call(a, wg, blocks, out_dtype, name, out_so_far=None):
    S, K = a.shape
    nb, _, bn = wg.shape
    tm = _tile(S, 1024, 16)

    def body(b_ref, a_ref, w_ref, *rest):
        rest[-1][...] = jnp.dot(a_ref[...], w_ref[...], preferred_element_type=F32).astype(out_dtype)

    in_specs = [pl.BlockSpec((tm, K), lambda i, j, b: (i, 0)), pl.BlockSpec((None, K, bn), lambda i, j, b: (b[j], 0, 0))]
    operands = [a, wg]
    aliases = {}
    if out_so_far is not None:
        in_specs.append(pl.BlockSpec(memory_space=pl.ANY))
        operands.append(out_so_far)
        aliases = {3: 0}
    grid_spec = pltpu.PrefetchScalarGridSpec(
        num_scalar_prefetch=1, grid=(S // tm, blocks.shape[0]), in_specs=in_specs,
        out_specs=pl.BlockSpec((tm, bn), lambda i, j, b: (i, b[j])))
    return pl.pallas_call(body, name=name, grid_spec=grid_spec,
                          out_shape=jax.ShapeDtypeStruct((S, nb * bn), out_dtype), input_output_aliases=aliases,
                          compiler_params=_cp(2))(blocks, *operands)


def _mm_out_call(x, y, w, name):
    S, D = x.shape
    tm = _tile(S, 512, 16)

    def body(x_ref, y_ref, w_ref, o_ref):
        o_ref[...] = x_ref[...] + jnp.dot(y_ref[...], w_ref[...], preferred_element_type=F32)

    return pl.pallas_call(
        body, name=name, grid=(S // tm,),
        in_specs=[pl.BlockSpec((tm, D), lambda i: (i, 0)), pl.BlockSpec((tm, D), lambda i: (i, 0)),
                  pl.BlockSpec((D, D), lambda i: (0, 0))],
        out_specs=pl.BlockSpec((tm, D), lambda i: (i, 0)),
        out_shape=jax.ShapeDtypeStruct((S, D), F32), compiler_params=_cp(1))(x, y, w)


def _mm_nt_call(a, w, out_dtype, name, deps=()):
    S, K = a.shape
    N = w.shape[0]
    tm = _tile(S, 1024, 16)
    tn = _tile(N, 768, LANES)

    def body(a_ref, w_ref, o_ref):
        o_ref[...] = lax.dot_general(a_ref[...], w_ref[...], _DN_NT, preferred_element_type=F32).astype(out_dtype)

    return _pallas(
        body, 2, deps, name=name, grid=(S // tm, N // tn),
        in_specs=[pl.BlockSpec((tm, K), lambda i, j: (i, 0)), pl.BlockSpec((tn, K), lambda i, j: (j, 0))],
        out_specs=pl.BlockSpec((tm, tn), lambda i, j: (i, j)),
        out_shape=jax.ShapeDtypeStruct((S, N), out_dtype), compiler_params=_cp(2))(a, w, *deps)


ROW_LANES = 256


def _fold_lanes(v):
    return sum(v[:, k * LANES:(k + 1) * LANES] for k in range(v.shape[1] // LANES))


def _lane_groups(width):
    return [slice(q * ROW_LANES, (q + 1) * ROW_LANES) for q in range(width // ROW_LANES)]


def _norm_bwd_rows(dh_ref, x_ref, resid_ref, g_ref, dx_ref, dxb_ref, dg_rows, n_rows):
    D = x_ref.shape[1]
    groups = _lane_groups(D)

    def rows16(t, carry):
        rows = pl.ds(pl.multiple_of(t * 16, 16), 16)
        s_xx = jnp.zeros((16, LANES), F32)
        for ls in groups:
            xv = x_ref[rows, ls]
            s_xx = s_xx + _fold_lanes(xv * xv)
        r = lax.rsqrt(jnp.sum(s_xx, axis=1, keepdims=True) * (1.0 / D) + RMS_EPS)
        s_dx = jnp.zeros((16, LANES), F32)
        for ls in groups:
            xn = x_ref[rows, ls] * r
            dh = dh_ref[rows, ls]
            dg_rows[:, ls] += dh * xn
            s_dx = s_dx + _fold_lanes(dh * g_ref[:, ls] * xn)
        c = jnp.sum(s_dx, axis=1, keepdims=True) * (1.0 / D)
        for ls in groups:
            xn = x_ref[rows, ls] * r
            dx = resid_ref[rows, ls] + r * (dh_ref[rows, ls] * g_ref[:, ls] - xn * c)
            dx_ref[rows, ls] = dx
            if dxb_ref is not None:
                dxb_ref[rows, ls] = dx.astype(BF16)
        return carry

    lax.fori_loop(0, n_rows // 16, rows16, 0, unroll=2)


def _mm_down_loss_call(x2, f, w, final_g, target, name):
    S, D = x2.shape
    Fd = f.shape[1]
    tm = _tile(S, 512, 16)
    tk = _tile(Fd, 768, LANES)
    nk = Fd // tk

    groups = _lane_groups(D)

    def body(x_ref, f_ref, w_ref, g_ref, t_ref, dx_ref, dxb_ref, loss_ref, dg_ref, acc, dg_rows, loss_rows):
        i, k = pl.program_id(0), pl.program_id(1)

        @pl.when(jnp.logical_and(i == 0, k == 0))
        def _():
            dg_rows[...] = jnp.zeros_like(dg_rows)
            loss_rows[...] = jnp.zeros_like(loss_rows)

        @pl.when(k == 0)
        def _():
            acc[...] = jnp.zeros_like(acc)

        acc[...] += jnp.dot(f_ref[...], w_ref[...], preferred_element_type=F32)

        def rows16(t, carry):
            rows = pl.ds(pl.multiple_of(t * 16, 16), 16)
            s_xx = jnp.zeros((16, LANES), F32)
            for ls in groups:
                x3 = x_ref[rows, ls] + acc[rows, ls]
                s_xx = s_xx + _fold_lanes(x3 * x3)
            r = lax.rsqrt(jnp.sum(s_xx, axis=1, keepdims=True) * (1.0 / D) + RMS_EPS)
            s_dd = jnp.zeros((16, LANES), F32)
            s_dx = jnp.zeros((16, LANES), F32)
            for ls in groups:
                xn = (x_ref[rows, ls] + acc[rows, ls]) * r
                g = g_ref[:, ls]
                dout = (xn * g - t_ref[rows, ls]) * (1.0 / D)
                dg_rows[:, ls] += dout * xn
                s_dd = s_dd + _fold_lanes(dout * dout)
                s_dx = s_dx + _fold_lanes(dout * g * xn)
            loss_rows[...] += s_dd
            c = jnp.sum(s_dx, axis=1, keepdims=True) * (1.0 / D)
            for ls in groups:
                xn = (x_ref[rows, ls] + acc[rows, ls]) * r
                g = g_ref[:, ls]
                dx = r * ((xn * g - t_ref[rows, ls]) * (g * (1.0 / D)) - xn * c)
                dx_ref[rows, ls] = dx
                dxb_ref[rows, ls] = dx.astype(BF16)
            return carry

        @pl.when(k == nk - 1)
        def _():
            lax.fori_loop(0, tm // 16, rows16, 0, unroll=2)

        @pl.when(jnp.logical_and(i == S // tm - 1, k == nk - 1))
        def _():
            loss_ref[...] = jnp.zeros_like(loss_ref) + (0.5 * D) * jnp.sum(loss_rows[...])
            dg_ref[...] = _col_sum(dg_rows[...])

    row = lambda i, k: (i, 0)
    return pl.pallas_call(
        body, name=name, grid=(S // tm, nk),
        in_specs=[pl.BlockSpec((tm, D), row), pl.BlockSpec((tm, tk), lambda i, k: (i, k)),
                  pl.BlockSpec((tk, D), lambda i, k: (k, 0)), pl.BlockSpec((1, D), lambda i, k: (0, 0)),
                  pl.BlockSpec((tm, D), row)],
        out_specs=[pl.BlockSpec((tm, D), row), pl.BlockSpec((tm, D), row),
                   pl.BlockSpec((8, LANES), lambda i, k: (0, 0)), pl.BlockSpec((1, D), lambda i, k: (0, 0))],
        out_shape=[jax.ShapeDtypeStruct((S, D), F32), jax.ShapeDtypeStruct((S, D), BF16),
                   jax.ShapeDtypeStruct((8, LANES), F32), jax.ShapeDtypeStruct((1, D), F32)],
        scratch_shapes=[pltpu.VMEM((tm, D), F32), pltpu.VMEM((16, D), F32), pltpu.VMEM((16, LANES), F32)],
        compiler_params=_cp(2, 56))(x2, f, w, final_g, target)


def _mm_dx_norm_call(a3, wg, resid, xin, g, name, deps=()):
    na, S, Fa = a3.shape
    nb, D, bn = wg.shape
    tm = _tile(S, 512, 16)
    tk = _tile(bn, 1536, LANES)
    nsub = bn // tk
    nka = Fa // tk
    nk = nb * nsub
    assert na * nka == nk

    def body(a_ref, w_ref, r_ref, x_ref, g_ref, dx_ref, dxb_ref, dg_ref, acc, dg_rows):
        i, k = pl.program_id(0), pl.program_id(1)

        @pl.when(jnp.logical_and(i == 0, k == 0))
        def _():
            dg_rows[...] = jnp.zeros_like(dg_rows)

        @pl.when(k == 0)
        def _():
            acc[...] = jnp.zeros_like(acc)

        acc[...] += lax.dot_general(a_ref[...], w_ref[...], _DN_NT, preferred_element_type=F32)

        @pl.when(k == nk - 1)
        def _():
            _norm_bwd_rows(acc, x_ref, r_ref, g_ref, dx_ref, dxb_ref, dg_rows, tm)

        @pl.when(jnp.logical_and(i == S // tm - 1, k == nk - 1))
        def _():
            dg_ref[...] = _col_sum(dg_rows[...])

    row = lambda i, k: (i, 0)
    return _pallas(
        body, 5, deps, name=name, grid=(S // tm, nk),
        in_specs=[pl.BlockSpec((None, tm, tk), lambda i, k: (k // nka, i, k % nka)),
                  pl.BlockSpec((None, D, tk), lambda i, k: (k // nsub, 0, k % nsub)),
                  pl.BlockSpec((tm, D), row, pipeline_mode=pl.Buffered(1)),
                  pl.BlockSpec((tm, D), row, pipeline_mode=pl.Buffered(1)), pl.BlockSpec((1, D), lambda i, k: (0, 0))],
        out_specs=[pl.BlockSpec((tm, D), row), pl.BlockSpec((tm, D), row), pl.BlockSpec((1, D), lambda i, k: (0, 0))],
        out_shape=[jax.ShapeDtypeStruct((S, D), F32), jax.ShapeDtypeStruct((S, D), BF16),
                   jax.ShapeDtypeStruct((1, D), F32)],
        scratch_shapes=[pltpu.VMEM((tm, D), F32), pltpu.VMEM((16, D), F32)],
        compiler_params=_cp(2, 56))(a3, wg, resid, xin, g, *deps)


def _unblock_call(wg, name):
    nb, K, bn = wg.shape

    def body(w_ref, o_ref):
        o_ref[...] = w_ref[...]

    return pl.pallas_call(
        body, name=name, grid=(nb,),
        in_specs=[pl.BlockSpec((None, K, bn), lambda o: (o, 0, 0))],
        out_specs=pl.BlockSpec((K, bn), lambda o: (0, o)),
        out_shape=jax.ShapeDtypeStruct((K, nb * bn), wg.dtype), compiler_params=_cp(1))(wg)


def _mm_nt_norm_call(a, w, resid, xin, g, name, deps=(), part=(0, 1), dx_so_far=None):
    S, K = a.shape
    D = w.shape[0]
    tm = _tile(S, 256, 16)
    tiles = (S // tm) // part[1]
    first = part[0] * tiles

    def body(a_ref, w_ref, r_ref, x_ref, g_ref, *rest):
        dx_ref, dg_ref, dh_buf, dg_rows = rest[-4:]

        @pl.when(pl.program_id(0) == 0)
        def _():
            dg_rows[...] = jnp.zeros_like(dg_rows)

        dh_buf[...] = lax.dot_general(a_ref[...], w_ref[...], _DN_NT, preferred_element_type=F32)
        _norm_bwd_rows(dh_buf, x_ref, r_ref, g_ref, dx_ref, None, dg_rows, tm)

        @pl.when(pl.program_id(0) == tiles - 1)
        def _():
            dg_ref[...] = _col_sum(dg_rows[...])

    row = lambda i: (i + first, 0)
    fixed = lambda i: (0, 0)
    in_specs = [pl.BlockSpec((tm, K), row), pl.BlockSpec((D, K), fixed, pipeline_mode=pl.Buffered(1)),
                pl.BlockSpec((tm, D), row), pl.BlockSpec((tm, D), row), pl.BlockSpec((1, D), fixed)]
    operands = [a, w, resid, xin, g]
    aliases = {}
    if dx_so_far is not None:
        in_specs.append(pl.BlockSpec(memory_space=pl.ANY))
        operands.append(dx_so_far)
        aliases = {5: 0}
    return _pallas(
        body, len(operands), deps, name=name, grid=(tiles,),
        in_specs=in_specs, out_specs=[pl.BlockSpec((tm, D), row), pl.BlockSpec((1, D), fixed)],
        out_shape=[jax.ShapeDtypeStruct((S, D), F32), jax.ShapeDtypeStruct((1, D), F32)],
        scratch_shapes=[pltpu.VMEM((tm, D), F32), pltpu.VMEM((16, D), F32)],
        input_output_aliases=aliases, compiler_params=_cp(1, 56))(*operands, *deps)


def _mm_tn_cols_call(a, b3, nb, bn, name, deps=()):
    S, Ka = a.shape
    nh, _, Fb = b3.shape
    tm = _tile(S, 2048, 16)
    tn = _tile(bn, 768, LANES)
    nsub = bn // tn
    njb = Fb // tn
    J = nb * nsub
    assert nh * njb == J

    def body(a_ref, b_ref, o_ref):
        @pl.when(pl.program_id(1) == 0)
        def _():
            o_ref[...] = jnp.zeros_like(o_ref)

        o_ref[...] += lax.dot_general(a_ref[...], b_ref[...], _DN_TN, preferred_element_type=F32)

    return _pallas(
        body, 2, deps, name=name, grid=(J, S // tm),
        in_specs=[pl.BlockSpec((tm, Ka), lambda j, i: (i, 0)),
                  pl.BlockSpec((None, tm, tn), lambda j, i: (j // njb, i, j % njb))],
        out_specs=pl.BlockSpec((None, Ka, tn), lambda j, i: (j // nsub, 0, j % nsub)),
        out_shape=jax.ShapeDtypeStruct((nb, Ka, bn), F32), compiler_params=_cp(2, 56))(a, b3, *deps)


def _mm_tn_rows_call(a, b, name, deps=()):
    S, E = a.shape
    D = b.shape[1]
    tm = _tile(S, 2048, 16)
    te = _tile(E, 768, LANES)

    def body(a_ref, b_ref, o_ref):
        @pl.when(pl.program_id(1) == 0)
        def _():
            o_ref[...] = jnp.zeros_like(o_ref)

        o_ref[...] += lax.dot_general(a_ref[...], b_ref[...], _DN_TN, preferred_element_type=F32)

    return _pallas(
        body, 2, deps, name=name, grid=(E // te, S // tm),
        in_specs=[pl.BlockSpec((tm, te), lambda j, i: (i, j)), pl.BlockSpec((tm, D), lambda j, i: (i, 0))],
        out_specs=pl.BlockSpec((te, D), lambda j, i: (j, 0)),
        out_shape=jax.ShapeDtypeStruct((E, D), F32), compiler_params=_cp(2, 56))(a, b, *deps)


def _ffn_tiles(S, Fd):
    return _tile(S, 512, 16), _tile(Fd // (N_DEV // 2), 1536, LANES)


def _taps(cw_ref):
    return [cw_ref[k:k + 1, :] for k in range(cw_ref.shape[0])]


def _conv3(prev8, cur, taps):
    s1 = _shift_down(prev8, cur, 1)
    s2 = _shift_down(prev8, cur, 2)
    return taps[2] * cur + taps[1] * s1 + taps[0] * s2, s1, s2


GATE_LANES = 256
SUB_LANES = 256


def _lane_taps(cw_ref, ls):
    return [cw_ref[k:k + 1, ls] for k in range(cw_ref.shape[0])]


def _ffn_up_act_call(h2, wg, cw, cb, name, deps=()):
    S, D = h2.shape
    nb, _, bn = wg.shape
    Fd = nb * bn // 2
    tm, tc = _ffn_tiles(S, Fd)
    nk = Fd // tc
    hb = tm // 16
    nsubw = bn // tc
    half = nb // 2
    sc = _tile(tc, SUB_LANES, LANES)

    def body(a_ref, ap_ref, wgate_ref, wval_ref, cwg_ref, cwv_ref, cbg_ref, cbv_ref, up_ref, upc_ref, f_ref):
        keep = jnp.where(pl.program_id(0) == 0, 0.0, 1.0)
        a_ext = jnp.concatenate([ap_ref[...], a_ref[...]], axis=0)
        nsub = tc // sc
        lanes = [slice(s * sc, (s + 1) * sc) for s in range(nsub)]

        def products(s):
            return [jnp.dot(a_ext, w_ref[:, lanes[s]], preferred_element_type=F32).astype(BF16)
                    for w_ref in (wgate_ref, wval_ref)]

        ready = products(0)
        for s in range(nsub):
            ls = lanes[s]
            following = products(s + 1) if s + 1 < nsub else None

            def conv_half(ub, cw_ref, cb_ref, slab):
                up_ref[slab, :, ls] = ub[16:]
                u = ub.astype(F32)
                conv, _, _ = _conv3(u[8:16] * keep, u[16:], _lane_taps(cw_ref, ls))
                c = conv + cb_ref[:, ls]
                upc_ref[slab, :, ls] = c.astype(BF16)
                return c

            cg = conv_half(ready[0], cwg_ref, cbg_ref, 0)
            cv = conv_half(ready[1], cwv_ref, cbv_ref, 1)
            f_ref[:, ls] = (_gelu(cg) * cv).astype(BF16)
            ready = following

    return _pallas(
        body, 8, deps, name=name, grid=(S // tm, nk),
        in_specs=[pl.BlockSpec((tm, D), lambda i, k: (i, 0)),
                  pl.BlockSpec((16, D), lambda i, k: (jnp.maximum(i * hb - 1, 0), 0)),
                  pl.BlockSpec((None, D, tc), lambda i, k: (k // nsubw, 0, k % nsubw)),
                  pl.BlockSpec((None, D, tc), lambda i, k: (half + k // nsubw, 0, k % nsubw)),
                  pl.BlockSpec((3, tc), lambda i, k: (0, k)), pl.BlockSpec((3, tc), lambda i, k: (0, k + nk)),
                  pl.BlockSpec((1, tc), lambda i, k: (0, k)), pl.BlockSpec((1, tc), lambda i, k: (0, k + nk))],
        out_specs=[pl.BlockSpec((2, tm, tc), lambda i, k: (0, i, k)), pl.BlockSpec((2, tm, tc), lambda i, k: (0, i, k)),
                   pl.BlockSpec((tm, tc), lambda i, k: (i, k))],
        out_shape=[jax.ShapeDtypeStruct((2, S, Fd), BF16), jax.ShapeDtypeStruct((2, S, Fd), BF16),
                   jax.ShapeDtypeStruct((S, Fd), BF16)],
        compiler_params=_cp(2, 56))(h2, h2, wg, wg, cw, cw, cb, cb, *deps)


def _ffn_down_dx_act_bwd_call(dxb, w, up3, upc3, cw, name, deps=()):
    S, D = dxb.shape
    _, _, Fd = up3.shape
    tm, tc = _ffn_tiles(S, Fd)
    nj = Fd // tc
    ni = S // tm
    hb = tm // 16
    sc = _tile(tc, SUB_LANES, LANES)

    def body(a_ref, an_ref, w_ref, g_ref, v_ref, cg_ref, cv_ref, cgn_ref, cvn_ref, cwg_ref, cwv_ref,
             dup_ref, dcwg_ref, dcwv_ref, dcbg_ref, dcbv_ref):
        i = pl.program_id(1)
        keep_next = jnp.where(i == ni - 1, 0.0, 1.0)

        @pl.when(i == 0)
        def _():
            for r in (dcwg_ref, dcwv_ref, dcbg_ref, dcbv_ref):
                r[...] = jnp.zeros_like(r)

        a_ext = jnp.concatenate([a_ref[...], an_ref[...]], axis=0)
        for s in range(tc // sc):
            ls = slice(s * sc, (s + 1) * sc)
            df_ext = lax.dot_general(a_ext, w_ref[s * sc:(s + 1) * sc, :], _DN_NT, preferred_element_type=F32)
            df = jnp.concatenate([df_ext[:tm], df_ext[tm:tm + 8] * keep_next], axis=0)
            cg = jnp.concatenate([cg_ref[:, ls].astype(F32), cgn_ref[:, ls].astype(F32)[:8]], axis=0)
            cv = jnp.concatenate([cv_ref[:, ls].astype(F32), cvn_ref[:, ls].astype(F32)[:8]], axis=0)
            gel, dgel = _gelu_parts(cg)

            def back(d, cw_ref, x_ref, dcw_ref, dcb_ref, slab):
                taps = _lane_taps(cw_ref, ls)
                d0 = d[:tm]
                d1 = pltpu.roll(d, tm + 8 - 1, 0)[:tm]
                d2 = pltpu.roll(d, tm + 8 - 2, 0)[:tm]
                dup_ref[slab, :, ls] = (taps[2] * d0 + taps[1] * d1 + taps[0] * d2).astype(BF16)
                xv = x_ref[:, ls].astype(F32)
                dcw_ref[2:3, ls] += _col_sum(xv * d0)
                dcw_ref[1:2, ls] += _col_sum(xv * d1)
                dcw_ref[0:1, ls] += _col_sum(xv * d2)
                dcb_ref[:, ls] += _col_sum(d0)

            back(df * cv * dgel, cwg_ref, g_ref, dcwg_ref, dcbg_ref, 0)
            back(df * gel, cwv_ref, v_ref, dcwv_ref, dcbv_ref, 1)

    nxt = lambda j, i: jnp.minimum((i + 1) * hb, S // 16 - 1)
    main = lambda s: pl.BlockSpec((None, tm, tc), lambda j, i: (s, i, j))
    halo = lambda s: pl.BlockSpec((None, 16, tc), lambda j, i: (s, nxt(j, i), j))
    acc3 = pl.BlockSpec((3, tc), lambda j, i: (0, j))
    acc1 = pl.BlockSpec((1, tc), lambda j, i: (0, j))
    return _pallas(
        body, 11, deps, name=name, grid=(nj, ni),
        in_specs=[pl.BlockSpec((tm, D), lambda j, i: (i, 0)), pl.BlockSpec((16, D), lambda j, i: (nxt(j, i), 0)),
                  pl.BlockSpec((tc, D), lambda j, i: (j, 0)),
                  main(0), main(1), main(0), main(1), halo(0), halo(1),
                  pl.BlockSpec((3, tc), lambda j, i: (0, j)), pl.BlockSpec((3, tc), lambda j, i: (0, j + nj))],
        out_specs=[pl.BlockSpec((2, tm, tc), lambda j, i: (0, i, j)), acc3, acc3, acc1, acc1],
        out_shape=[jax.ShapeDtypeStruct((2, S, Fd), BF16), jax.ShapeDtypeStruct((3, Fd), F32),
                   jax.ShapeDtypeStruct((3, Fd), F32), jax.ShapeDtypeStruct((1, Fd), F32),
                   jax.ShapeDtypeStruct((1, Fd), F32)],
        compiler_params=_cp(2, 56))(dxb, dxb, w, up3, up3, upc3, upc3, upc3, upc3, cw, cw, *deps)


def _gm_forward_tile(pv, vg, vb, ws_ref, bsb_ref, mbuf, H, nc):
    W = H * CHUNK
    z, dz = _gelu_parts(pv)
    u, v0 = z[:, :W], z[:, W:]
    xc = v0 - _rows_mean(v0)
    rs = lax.rsqrt(_rows_mean(xc * xc) + LN_EPS)
    vh = xc * rs
    vnb = (vh * vg + vb).astype(BF16)
    mask = lax.broadcasted_iota(jnp.int32, (CHUNK, CHUNK), 0) >= lax.broadcasted_iota(jnp.int32, (CHUNK, CHUNK), 1)
    for h in range(H):
        cs = slice(h * CHUNK, (h + 1) * CHUNK)
        wm = jnp.where(mask, ws_ref[h], 0.0).astype(BF16)
        vcat = jnp.concatenate([vnb[c * CHUNK:(c + 1) * CHUNK, cs] for c in range(nc)], axis=1)
        mix = jnp.dot(wm, vcat, preferred_element_type=F32)
        for c in range(nc):
            mbuf[c * CHUNK:(c + 1) * CHUNK, cs] = mix[:, c * CHUNK:(c + 1) * CHUNK] + bsb_ref[h]
    return dz, u, vh, rs, vnb, mask


def _gm_fwd_call(p, v_g, v_b, ws, bsb, out_g, name, deps=()):
    S = p.shape[0]
    H = ws.shape[0]
    W = H * CHUNK
    tm = _tile(S, 256, CHUNK)
    nc = tm // CHUNK

    def body(p_ref, vg_ref, vb_ref, ws_ref, bsb_ref, og_ref, y_ref, mbuf):
        _, u, _, _, _, _ = _gm_forward_tile(p_ref[...], vg_ref[...], vb_ref[...], ws_ref, bsb_ref, mbuf, H, nc)
        yg = u * mbuf[...]
        r = lax.rsqrt(_rows_mean(yg * yg) + RMS_EPS)
        y_ref[...] = (yg * r * og_ref[...]).astype(BF16)

    vec = pl.BlockSpec((1, W), lambda i: (0, 0))
    mat = pl.BlockSpec((H, CHUNK, CHUNK), lambda i: (0, 0, 0))
    return _pallas(
        body, 6, deps, name=name, grid=(S // tm,),
        in_specs=[pl.BlockSpec((tm, 2 * W), lambda i: (i, 0)), vec, vec, mat, mat, vec],
        out_specs=pl.BlockSpec((tm, W), lambda i: (i, 0)),
        out_shape=jax.ShapeDtypeStruct((S, 2 * W), BF16),
        scratch_shapes=[pltpu.VMEM((tm, W), F32)], compiler_params=_cp(1))(p, v_g, v_b, ws, bsb, out_g, *deps)


def _gm_bwd_call(p, d_y, v_g, v_b, ws, bsb, out_g, name, deps=()):
    S = p.shape[0]
    H = ws.shape[0]
    W = H * CHUNK
    tm = _tile(S, 256, CHUNK)
    nc = tm // CHUNK
    ni = S // tm

    def body(p_ref, dy_ref, vg_ref, vb_ref, ws_ref, bsb_ref, og_ref,
             dp_ref, dvg_ref, dvb_ref, dws_ref, dbs_ref, dog_ref, mbuf, dvbuf):
        i = pl.program_id(0)

        @pl.when(i == 0)
        def _():
            for r in (dvg_ref, dvb_ref, dws_ref, dbs_ref, dog_ref):
                r[...] = jnp.zeros_like(r)

        vg = vg_ref[...]
        dz, u, vh, rs, vnb, mask = _gm_forward_tile(p_ref[...], vg, vb_ref[...], ws_ref, bsb_ref, mbuf, H, nc)
        mixed = mbuf[...]
        yg = u * mixed
        r = lax.rsqrt(_rows_mean(yg * yg) + RMS_EPS)
        yn = yg * r
        dya = dy_ref[...]
        dog_ref[...] += _col_sum(dya * yn)
        dyg = dya * og_ref[...]
        dygm = r * (dyg - yn * _rows_mean(dyg * yn))
        du = dygm * mixed
        dmix = dygm * u
        dmb = dmix.astype(BF16)
        for h in range(H):
            cs = slice(h * CHUNK, (h + 1) * CHUNK)
            wm = jnp.where(mask, ws_ref[h], 0.0).astype(BF16)
            dcat = jnp.concatenate([dmb[c * CHUNK:(c + 1) * CHUNK, cs] for c in range(nc)], axis=1)
            vcat = jnp.concatenate([vnb[c * CHUNK:(c + 1) * CHUNK, cs] for c in range(nc)], axis=1)
            dvn = lax.dot_general(wm, dcat, _DN_TN, preferred_element_type=F32)
            dws_ref[h] += jnp.where(mask, lax.dot_general(dcat, vcat, _DN_NT, preferred_element_type=F32), 0.0)
            dbs = dmix[0:CHUNK, cs]
            for c in range(1, nc):
                dbs = dbs + dmix[c * CHUNK:(c + 1) * CHUNK, cs]
            dbs_ref[h] += dbs
            for c in range(nc):
                dvbuf[c * CHUNK:(c + 1) * CHUNK, cs] = dvn[:, c * CHUNK:(c + 1) * CHUNK]
        dvn_all = dvbuf[...]
        dvg_ref[...] += _col_sum(dvn_all * vh)
        dvb_ref[...] += _col_sum(dvn_all)
        dvh = dvn_all * vg
        dv0 = rs * (dvh - _rows_mean(dvh) - vh * _rows_mean(dvh * vh))
        dp_ref[...] = (jnp.concatenate([du, dv0], axis=1) * dz).astype(BF16)

        @pl.when(i == ni - 1)
        def _():
            for h in range(H):
                dbs_ref[h] = jnp.broadcast_to(jnp.sum(dbs_ref[h], axis=1, keepdims=True), (CHUNK, CHUNK))

    vec = pl.BlockSpec((1, W), lambda i: (0, 0))
    mat = pl.BlockSpec((H, CHUNK, CHUNK), lambda i: (0, 0, 0))
    vshape = jax.ShapeDtypeStruct((1, W), F32)
    mshape = jax.ShapeDtypeStruct((H, CHUNK, CHUNK), F32)
    return _pallas(
        body, 7, deps, name=name, grid=(ni,),
        in_specs=[pl.BlockSpec((tm, 2 * W), lambda i: (i, 0)), pl.BlockSpec((tm, W), lambda i: (i, 0)),
                  vec, vec, mat, mat, vec],
        out_specs=[pl.BlockSpec((tm, 2 * W), lambda i: (i, 0)), vec, vec, mat, mat, vec],
        out_shape=[jax.ShapeDtypeStruct((S, 4 * W), BF16), vshape, vshape, mshape, mshape, vshape],
        scratch_shapes=[pltpu.VMEM((tm, W), F32), pltpu.VMEM((tm, W), F32)],
        compiler_params=_cp(1))(p, d_y, v_g, v_b, ws, bsb, out_g, *deps)


def _lru_gates(prev8, xl, cw, cb, wa_ref, ba, wx_ref, bx, lam, H):
    sh = [_shift_down(prev8, xl, k) for k in range(4)]
    xr = cw[3] * sh[0] + cw[2] * sh[1] + cw[1] * sh[2] + cw[0] * sh[3] + cb
    xrb = xr.astype(BF16)
    rp, ip = [], []
    for h in range(H):
        cs = slice(h * CHUNK, (h + 1) * CHUNK)
        rp.append(jnp.dot(xrb[:, cs], wa_ref[h].astype(BF16), preferred_element_type=F32))
        ip.append(jnp.dot(xrb[:, cs], wx_ref[h].astype(BF16), preferred_element_type=F32))
    r = _sigmoid(jnp.concatenate(rp, axis=1) + ba)
    ig = _sigmoid(jnp.concatenate(ip, axis=1) + bx)
    sp = _softplus(-lam)
    t = jnp.tanh((-LRU_C) * r * sp)
    q = lax.rsqrt(1.0 - t)
    a = jnp.sqrt(1.0 + t) * q
    mult = jnp.sqrt(-2.0 * t) * q
    a2_over_mult = (1.0 + t) * q * lax.rsqrt(-2.0 * t)
    return xr, xrb, r, ig, sp, a, mult, a2_over_mult, sh


def _lru_fwd_call(p, cw, cb, wa, ba, wx, bx, lam, out_g, y_half, name):
    S = p.shape[0]
    H = wa.shape[0]
    W = H * CHUNK
    tm = _tile(S, 256, 16)
    ng = tm // 8

    def body(pg_ref, px_ref, cw_ref, cb_ref, wa_ref, ba_ref, wx_ref, bx_ref, lam_ref, og_ref, y_in_ref,
             y_ref, h_ref, saved_ref, xprev, hcar, abuf, bbuf):
        @pl.when(pl.program_id(0) == 0)
        def _():
            xprev[...] = jnp.zeros_like(xprev)
            hcar[...] = jnp.zeros_like(hcar)

        xl = px_ref[...]
        xr, _, r_gate, ig, _, a, mult, a2m, _ = _lru_gates(xprev[...], xl, _taps(cw_ref), cb_ref[...], wa_ref,
                                                           ba_ref[...], wx_ref, bx_ref[...], lam_ref[...], H)
        for k, val in enumerate((xr, r_gate, ig, a, mult, a2m)):
            saved_ref[k] = val
        xprev[...] = xl[tm - 8:]
        b = mult * (ig * xr)
        sub = lax.broadcasted_iota(jnp.int32, (tm, W), 0) & 7
        for d in (1, 2, 4):
            m = sub >= d
            a_s = jnp.where(m, pltpu.roll(a, d, 0), 1.0)
            b_s = jnp.where(m, pltpu.roll(b, d, 0), 0.0)
            b = a * b_s + b
            a = a * a_s
        abuf[...] = a
        bbuf[...] = b

        def step(g, carry):
            r0 = pl.multiple_of(g * 8, 8)
            h_ref[pl.ds(r0, 8), :] = abuf[pl.ds(r0, 8), :] * carry + bbuf[pl.ds(r0, 8), :]
            return jnp.broadcast_to(h_ref[pl.ds(r0 + 7, 1), :], (8, W))

        hcar[...] = lax.fori_loop(0, ng, step, hcar[...])
        yl = h_ref[...] * _gelu(pg_ref[...])
        r = lax.rsqrt(_rows_mean(yl * yl) + RMS_EPS)
        y_ref[...] = (yl * r * og_ref[...]).astype(BF16)

    vec = pl.BlockSpec((1, W), lambda i: (0, 0))
    mat = pl.BlockSpec((H, CHUNK, CHUNK), lambda i: (0, 0, 0))
    return pl.pallas_call(
        body, name=name, grid=(S // tm,),
        in_specs=[pl.BlockSpec((tm, W), lambda i: (i, 2)), pl.BlockSpec((tm, W), lambda i: (i, 3)),
                  pl.BlockSpec((4, W), lambda i: (0, 0)), vec, mat, vec, mat, vec, vec, vec,
                  pl.BlockSpec(memory_space=pl.ANY)],
        out_specs=[pl.BlockSpec((tm, W), lambda i: (i, 1)), pl.BlockSpec((tm, W), lambda i: (i, 0)),
                   pl.BlockSpec((6, tm, W), lambda i: (0, i, 0))],
        out_shape=[jax.ShapeDtypeStruct((S, 2 * W), BF16), jax.ShapeDtypeStruct((S, W), F32),
                   jax.ShapeDtypeStruct((6, S, W), F32)],
        input_output_aliases={10: 0},
        scratch_shapes=[pltpu.VMEM((8, W), F32), pltpu.VMEM((8, W), F32), pltpu.VMEM((tm, W), F32),
                        pltpu.VMEM((tm, W), F32)],
        compiler_params=_cp(1))(p, p, cw, cb, wa, ba, wx, bx, lam, out_g, y_half)


def _lru_bwd_call(p, hs, saved, d_y, cw, wa, wx, lam, out_g, dp_half, name):
    S = p.shape[0]
    H = wa.shape[0]
    W = H * CHUNK
    tm = _tile(S, 256, 16)
    ng = tm // 8
    ni = S // tm
    hb = tm // 8

    def body(pg_ref, px_ref, saved_ref, h_ref, hp_ref, dy_ref, cw_ref, wa_ref, wx_ref, lam_ref, og_ref, dp_in_ref,
             dp_ref, dcw_ref, dcb_ref, dwa_ref, dba_ref, dwx_ref, dbx_ref, dlam_ref, dog_ref,
             a_next, e_next, dxr_next, abuf, bbuf, ebuf, dxr0, dprb_buf, dpib_buf, sums):
        i = pl.program_id(0)
        ri = ni - 1 - i

        @pl.when(i == 0)
        def _():
            for r in (dcw_ref, dcb_ref, dwa_ref, dba_ref, dwx_ref, dbx_ref, dlam_ref, dog_ref,
                      a_next, e_next, dxr_next):
                r[...] = jnp.zeros_like(r)

        keep_prev = jnp.where(ri == 0, 0.0, 1.0)
        cw_ = _taps(cw_ref)
        lam_ = lam_ref[...]
        xl = px_ref[...]
        a = saved_ref[3]
        sp = _softplus(-lam_)
        sums[...] = jnp.zeros_like(sums)
        og = og_ref[...]
        lane_groups = [slice(q * GATE_LANES, (q + 1) * GATE_LANES) for q in range(W // GATE_LANES)]

        def fold(v):
            return sum(v[:, k * LANES:(k + 1) * LANES] for k in range(GATE_LANES // LANES))

        def out_grads(g, carry):
            rows = pl.ds(pl.multiple_of(g * 16, 16), 16)
            s_yy = jnp.zeros((16, LANES), F32)
            s_dy = jnp.zeros((16, LANES), F32)
            for ls in lane_groups:
                yl = h_ref[rows, ls] * _gelu(pg_ref[rows, ls])
                s_yy = s_yy + fold(yl * yl)
                s_dy = s_dy + fold(dy_ref[rows, ls] * og[:, ls] * yl)
            rr = lax.rsqrt(jnp.sum(s_yy, axis=1, keepdims=True) * (1.0 / W) + RMS_EPS)
            c = rr * rr * jnp.sum(s_dy, axis=1, keepdims=True) * (1.0 / W)
            for ls in lane_groups:
                hq = h_ref[rows, ls]
                gg, dgg = _gelu_parts(pg_ref[rows, ls])
                yl = hq * gg
                dyb = dy_ref[rows, ls]
                sums[3, :, ls] += dyb * (yl * rr)
                dyl = rr * (dyb * og[:, ls] - yl * c)
                bbuf[rows, ls] = dyl * gg
                dp_ref[rows, ls] = (dyl * hq * dgg).astype(BF16)
            return carry

        lax.fori_loop(0, tm // 16, out_grads, 0, unroll=2)
        dog_ref[...] += _col_sum(sums[3])

        an = _shift_up(a, a_next[...], 1)
        eb = bbuf[...]
        sub = lax.broadcasted_iota(jnp.int32, (tm, W), 0) & 7
        for d in (1, 2, 4):
            m = sub < 8 - d
            a_s = jnp.where(m, pltpu.roll(an, tm - d, 0), 1.0)
            e_s = jnp.where(m, pltpu.roll(eb, tm - d, 0), 0.0)
            eb = an * e_s + eb
            an = an * a_s
        abuf[...] = an
        bbuf[...] = eb

        def step(g, carry):
            r0 = pl.multiple_of((ng - 1 - g) * 8, 8)
            ebuf[pl.ds(r0, 8), :] = abuf[pl.ds(r0, 8), :] * carry + bbuf[pl.ds(r0, 8), :]
            return jnp.broadcast_to(ebuf[pl.ds(r0, 1), :], (8, W))

        lax.fori_loop(0, ng, step, jnp.broadcast_to(e_next[0:1, :], (8, W)))
        a_next[...] = a[0:8]
        e_next[...] = ebuf[0:8, :]

        row16 = lax.broadcasted_iota(jnp.int32, (16, GATE_LANES), 0)
        h_before = hp_ref[7:8, :] * keep_prev
        r_scale = (-LRU_C) * sp

        def gate_grads(g, carry):
            r0 = pl.multiple_of(g * 16, 16)
            above = jnp.maximum(r0 - 1, 0)
            for q in range(W // GATE_LANES):
                ls = slice(q * GATE_LANES, (q + 1) * GATE_LANES)
                e = ebuf[pl.ds(r0, 16), ls]
                h_prev_row = jnp.where(g == 0, h_before[:, ls], h_ref[pl.ds(above, 1), ls])
                hm1 = jnp.where(row16 == 0, h_prev_row, pltpu.roll(h_ref[pl.ds(r0, 16), ls], 1, 0))
                xr_, r_, ig_, a_, mult_, a2m_ = [saved_ref[k, pl.ds(r0, 16), ls] for k in range(6)]
                em = e * mult_
                dxr0[pl.ds(r0, 16), ls] = em * ig_
                dla = e * hm1 * a_ - e * ig_ * xr_ * a2m_
                dpr = dla * r_scale[:, ls] * r_ * (1.0 - r_)
                dpi = em * xr_ * ig_ * (1.0 - ig_)
                sums[0, :, ls] += dla * r_
                sums[1, :, ls] += dpr
                sums[2, :, ls] += dpi
                dprb_buf[pl.ds(r0, 16), ls] = dpr.astype(BF16)
                dpib_buf[pl.ds(r0, 16), ls] = dpi.astype(BF16)
            return carry

        lax.fori_loop(0, tm // 16, gate_grads, 0)
        dlam_ref[...] += (-LRU_C) * _col_sum(sums[0])
        dba_ref[...] += _col_sum(sums[1])
        dbx_ref[...] += _col_sum(sums[2])
        dprb = dprb_buf[...]
        dpib = dpib_buf[...]
        dxr = dxr0[...]
        xrb = saved_ref[0].astype(BF16)
        back = []
        for h in range(H):
            cs = slice(h * CHUNK, (h + 1) * CHUNK)
            wab = wa_ref[h].astype(BF16)
            wxb = wx_ref[h].astype(BF16)
            back.append(lax.dot_general(dprb[:, cs], wab, _DN_NT, preferred_element_type=F32)
                        + lax.dot_general(dpib[:, cs], wxb, _DN_NT, preferred_element_type=F32))
            dwa_ref[h] += lax.dot_general(xrb[:, cs], dprb[:, cs], _DN_TN, preferred_element_type=F32)
            dwx_ref[h] += lax.dot_general(xrb[:, cs], dpib[:, cs], _DN_TN, preferred_element_type=F32)
        dxr = dxr + jnp.concatenate(back, axis=1)

        nxt = dxr_next[...]
        ahead = [_shift_up(dxr, nxt, j) for j in range(4)]
        dxl = cw_[3] * ahead[0] + cw_[2] * ahead[1] + cw_[1] * ahead[2] + cw_[0] * ahead[3]
        dxr_next[...] = dxr[0:8]
        for k in range(4):
            dcw_ref[k:k + 1, :] += _col_sum(xl * ahead[3 - k])
        dcb_ref[...] += _col_sum(dxr)
        dp_ref[:, W:] = dxl.astype(BF16)

        @pl.when(i == ni - 1)
        def _():
            dlam_ref[...] = -dlam_ref[...] * _sigmoid(-lam_)

    vec = pl.BlockSpec((1, W), lambda i: (0, 0))
    mat = pl.BlockSpec((H, CHUNK, CHUNK), lambda i: (0, 0, 0))
    rev = lambda i: ni - 1 - i
    prev = lambda i: jnp.maximum(rev(i) * hb - 1, 0)
    vshape = jax.ShapeDtypeStruct((1, W), F32)
    mshape = jax.ShapeDtypeStruct((H, CHUNK, CHUNK), F32)
    tile = lambda: pltpu.VMEM((tm, W), F32)
    car = lambda: pltpu.VMEM((8, W), F32)
    return pl.pallas_call(
        body, name=name, grid=(ni,),
        in_specs=[pl.BlockSpec((tm, W), lambda i: (rev(i), 2)), pl.BlockSpec((tm, W), lambda i: (rev(i), 3)),
                  pl.BlockSpec((6, tm, W), lambda i: (0, rev(i), 0)),
                  pl.BlockSpec((tm, W), lambda i: (rev(i), 0)), pl.BlockSpec((8, W), lambda i: (prev(i), 0)),
                  pl.BlockSpec((tm, W), lambda i: (rev(i), 1)),
                  pl.BlockSpec((4, W), lambda i: (0, 0)), mat, mat, vec, vec,
                  pl.BlockSpec(memory_space=pl.ANY)],
        out_specs=[pl.BlockSpec((tm, 2 * W), lambda i: (rev(i), 1)), pl.BlockSpec((4, W), lambda i: (0, 0)), vec,
                   mat, vec, mat, vec, vec, vec],
        out_shape=[jax.ShapeDtypeStruct((S, 4 * W), BF16), jax.ShapeDtypeStruct((4, W), F32), vshape,
                   mshape, vshape, mshape, vshape, vshape, vshape],
        input_output_aliases={11: 0},
        scratch_shapes=[car(), car(), car(), tile(), tile(), tile(), tile(), pltpu.VMEM((tm, W), BF16),
                        pltpu.VMEM((tm, W), BF16), pltpu.VMEM((4, 16, W), F32)],
        compiler_params=_cp(1, 56))(p, p, saved, hs, hs, d_y, cw, wa, wx, lam, out_g, dp_half)


def _rows128(a):
    return a.reshape(-1, LANES).astype(F32)


def _pack(arrays, pad_to=256):
    flat = jnp.concatenate([_rows128(a) for a in arrays], axis=0)
    pad = (-flat.shape[0]) % pad_to
    if pad:
        flat = jnp.concatenate([flat, jnp.zeros((pad, LANES), F32)], axis=0)
    return flat


def _unpack(flat, shapes):
    out, r = [], 0
    for s in shapes:
        n = 1
        for d in s:
            n *= d
        out.append(flat[r:r + n // LANES].reshape(s))
        r += n // LANES
    return out


def kernel(x, norm1_g, w_in, gm_v_g, gm_v_b, gm_ws, gm_bs, lru_conv_w, lru_conv_b, lru_wa, lru_ba, lru_wx, lru_bx, lru_lambda, gm_out_g, lru_out_g, w_out, norm2_g, ffn_w_up, ffn_conv_w, ffn_conv_b, ffn_w_down, final_g, loss_target, m_norm1_g, m_w_in, m_gm_v_g, m_gm_v_b, m_gm_ws, m_gm_bs, m_lru_conv_w, m_lru_conv_b, m_lru_wa, m_lru_ba, m_lru_wx, m_lru_bx, m_lru_lambda, m_gm_out_g, m_lru_out_g, m_w_out, m_norm2_g, m_ffn_w_up, m_ffn_conv_w, m_ffn_conv_b, m_ffn_w_down, m_final_g, v_norm1_g, v_w_in, v_gm_v_g, v_gm_v_b, v_gm_ws, v_gm_bs, v_lru_conv_w, v_lru_conv_b, v_lru_wa, v_lru_ba, v_lru_wx, v_lru_bx, v_lru_lambda, v_gm_out_g, v_lru_out_g, v_w_out, v_norm2_g, v_ffn_w_up, v_ffn_conv_w, v_ffn_conv_b, v_ffn_w_down, v_final_g):
    wts = dict(norm1_g=norm1_g, w_in=w_in, gm_v_g=gm_v_g, gm_v_b=gm_v_b, gm_ws=gm_ws, gm_bs=gm_bs,
               lru_conv_w=lru_conv_w, lru_conv_b=lru_conv_b, lru_wa=lru_wa, lru_ba=lru_ba, lru_wx=lru_wx,
               lru_bx=lru_bx, lru_lambda=lru_lambda, gm_out_g=gm_out_g, lru_out_g=lru_out_g, w_out=w_out,
               norm2_g=norm2_g, ffn_w_up=ffn_w_up, ffn_conv_w=ffn_conv_w, ffn_conv_b=ffn_conv_b,
               ffn_w_down=ffn_w_down, final_g=final_g)
    mom = dict(norm1_g=m_norm1_g, w_in=m_w_in, gm_v_g=m_gm_v_g, gm_v_b=m_gm_v_b, gm_ws=m_gm_ws, gm_bs=m_gm_bs,
               lru_conv_w=m_lru_conv_w, lru_conv_b=m_lru_conv_b, lru_wa=m_lru_wa, lru_ba=m_lru_ba, lru_wx=m_lru_wx,
               lru_bx=m_lru_bx, lru_lambda=m_lru_lambda, gm_out_g=m_gm_out_g, lru_out_g=m_lru_out_g, w_out=m_w_out,
               norm2_g=m_norm2_g, ffn_w_up=m_ffn_w_up, ffn_conv_w=m_ffn_conv_w, ffn_conv_b=m_ffn_conv_b,
               ffn_w_down=m_ffn_w_down, final_g=m_final_g)
    var = dict(norm1_g=v_norm1_g, w_in=v_w_in, gm_v_g=v_gm_v_g, gm_v_b=v_gm_v_b, gm_ws=v_gm_ws, gm_bs=v_gm_bs,
               lru_conv_w=v_lru_conv_w, lru_conv_b=v_lru_conv_b, lru_wa=v_lru_wa, lru_ba=v_lru_ba, lru_wx=v_lru_wx,
               lru_bx=v_lru_bx, lru_lambda=v_lru_lambda, gm_out_g=v_gm_out_g, lru_out_g=v_lru_out_g, w_out=v_w_out,
               norm2_g=v_norm2_g, ffn_w_up=v_ffn_w_up, ffn_conv_w=v_ffn_conv_w, ffn_conv_b=v_ffn_conv_b,
               ffn_w_down=v_ffn_w_down, final_g=v_final_g)

    xi, yi, ci = lax.axis_index("x"), lax.axis_index("y"), lax.axis_index("c")
    chip = 2 * xi + yi
    dev = 2 * chip + ci
    core_chip = jnp.stack([ci, chip]).astype(jnp.int32)

    xs = x[0]
    tgt = loss_target[0]
    S, D = xs.shape
    H = gm_ws.shape[1]
    W = H * CHUNK
    Fd = ffn_w_down.shape[1] * N_DEV
    lcw_cols = lru_conv_w.shape[2]
    fcw_cols = ffn_conv_w.shape[2]

    dev1 = jnp.reshape(dev, (1,)).astype(jnp.int32)

    def place_own(shards, name):
        return [_place_own_call(s, dev1, dt, "%s_own%d" % (name, k)) for k, (s, dt) in enumerate(shards)]

    def gather_start(shards, name, after=()):
        lands = place_own(shards, name)
        return _exchange_start([], lands, 4 * len(lands), _gather_stage1_copies(len(lands)), name + "_ici", after)

    def gather_forward(lands, name, after=()):
        return _exchange_start([], lands, 3 * len(lands), _gather_stage2_copies(len(lands)), name + "_d2d", after)

    def pair_start(g, name, after=()):
        return _exchange_start([g], [lax.empty((4,) + g.shape[1:], F32)], 4, _pair_copies(1), name, after)

    def chip_start(p16, name, after=()):
        return _exchange_start([p16], [lax.empty((3,) + p16.shape[1:], BF16)], 3, _chip_copies(1), name, after)

    vgm_g, vgm_b = gm_v_g, gm_v_b
    ws, wa, wx = gm_ws[0], lru_wa[0], lru_wx[0]
    bsb = jnp.broadcast_to(gm_bs[0][:, :, None], (H, CHUNK, CHUNK))
    ba, bx = lru_ba.reshape(1, W), lru_bx.reshape(1, W)
    fcb = ffn_conv_b
    fing = final_g.reshape(1, D)

    conv_pack = _pack([lru_conv_w[0], ffn_conv_w[0]], pad_to=8)
    lands = place_own([(w_in[0], BF16), (conv_pack, F32)], "gather_in")
    ga_pair = _exchange_start([], lands, 2, _gather_stage1_copies(2, to_chips=False), "gather_in_pair")
    ga1 = _exchange_start([], ga_pair.bufs, 6, _gather_stage1_copies(2, to_sibling=False), "gather_in_ici")
    h1 = _rmsnorm_call(xs, norm1_g, "norm1", deps=(ga1.token,))
    ga_pair.bufs = ga1.bufs
    _, la = _exchange_wait(ga_pair, after=(h1,))
    own_blocks = jnp.stack([dev, dev + 1 - 2 * ci]).astype(jnp.int32)
    other_blocks = ((2 * chip + 2 + jnp.arange(N_DEV - 2)) % N_DEV).astype(jnp.int32)
    p_own = _mm_some_blocks_call(h1, la[0], own_blocks, F32, "in_proj_own")
    ga1.bufs = la
    _, la = _exchange_wait(ga1, after=(p_own,))
    ga2 = gather_forward(la, "gather_in")
    gb1 = gather_start([(w_out[0], BF16)], "gather_out", after=(ga2.token,))
    gc1 = gather_start([(ffn_w_up[0], BF16)], "gather_up", after=(gb1.token,))
    gd1 = gather_start([(ffn_w_down[0], BF16)], "gather_down", after=(gc1.token,))
    _, (win_g, conv_g) = _exchange_wait(ga2, after=(gd1.token,))
    n_l = 4 * lcw_cols // LANES
    n_f = 3 * fcw_cols // LANES
    lcw = conv_g[:, :n_l].reshape(N_DEV, 4, lcw_cols).transpose(1, 0, 2).reshape(4, N_DEV * lcw_cols)
    fcw = conv_g[:, n_l:n_l + n_f].reshape(N_DEV, 3, fcw_cols).transpose(1, 0, 2).reshape(3, N_DEV * fcw_cols)

    p = _mm_some_blocks_call(h1, win_g, other_blocks, F32, "in_proj_rest", out_so_far=p_own)
    win_rows = _unblock_call(win_g, "w_in_rows")
    _, lb = _exchange_wait(gb1, after=(p,))
    gb2 = gather_forward(lb, "gather_out")
    y_half = _gm_fwd_call(p, vgm_g, vgm_b, ws, bsb, gm_out_g, "gmlp_fwd", deps=(gb2.token,))
    y, hs, lru_saved = _lru_fwd_call(p, lcw, lru_conv_b, wa, ba, wx, bx, lru_lambda, lru_out_g, y_half, "lru_fwd")
    _, lc = _exchange_wait(gc1, after=(y,))
    gc2 = gather_forward(lc, "gather_up")
    _, (wout_g,) = _exchange_wait(gb2, after=(y, gc2.token))
    wout_full = wout_g.reshape(D, D)
    x2 = _mm_out_call(xs, y, wout_full, "out_proj")
    h2 = _rmsnorm_call(x2, norm2_g, "norm2")
    _, (wup_g,) = _exchange_wait(gc2, after=(h2,))
    _, ld = _exchange_wait(gd1, after=(h2,))
    gd2 = gather_forward(ld, "gather_down")
    up3, upc3, f = _ffn_up_act_call(h2, wup_g, fcw, fcb, "ffn_up", deps=(gd2.token,))
    _, (wdown_g,) = _exchange_wait(gd2, after=(f,))
    wdown_full = wdown_g.reshape(Fd, D)
    dx3, dx3b, loss_acc, d_final = _mm_down_loss_call(x2, f, wdown_full, fing, tgt, "ffn_down_loss")

    g_wdown = _mm_tn_rows_call(f, dx3b, "ffn_down_dw").reshape((N_DEV,) + ffn_w_down.shape[1:])
    pd = pair_start(g_wdown, "pair_down")
    d_up3, dfcw_g, dfcw_v, dfcb_g, dfcb_v = _ffn_down_dx_act_bwd_call(dx3b, wdown_full, up3, upc3, fcw, "ffn_down_dx",
                                                                     deps=(pd.token,))
    (g_wdown,), (r1,) = _exchange_wait(pd, after=(d_up3,))
    own_down, p16 = _pair_add_call(g_wdown, r1, core_chip, "pair_add_down")
    cd = chip_start(p16, "chip_down")
    g_wup = _mm_tn_cols_call(h2, d_up3, N_DEV, ffn_w_up.shape[2], "ffn_up_dw", deps=(cd.token,))
    pu = pair_start(g_wup, "pair_up")
    dx2, dx2b, d_norm2 = _mm_dx_norm_call(d_up3, wup_g, dx3, x2, norm2_g, "ffn_up_dx", deps=(pu.token,))
    g_wout = _mm_tn_rows_call(y, dx2b, "out_proj_dw").reshape((N_DEV,) + w_out.shape[1:])
    po = pair_start(g_wout, "pair_out")
    d_y = _mm_nt_call(dx2b, wout_full, F32, "out_proj_dx", deps=(po.token,))
    (g_wup,), (r1,) = _exchange_wait(pu, after=(d_y,))
    own_up, p16 = _pair_add_call(g_wup, r1, core_chip, "pair_add_up")
    _, (r2_down,) = _exchange_wait(cd, after=(p16,))
    cu = chip_start(p16, "chip_up", after=(r2_down,))
    dp_half, d_vg, d_vb, d_ws, d_bs, d_gog = _gm_bwd_call(p, d_y, vgm_g, vgm_b, ws, bsb, gm_out_g, "gmlp_bwd",
                                                          deps=(cu.token,))
    d_p2, d_lcw, d_lcb, d_wa, d_ba, d_wx, d_bx, d_lam, d_log = _lru_bwd_call(
        p, hs, lru_saved, d_y, lcw, wa, wx, lru_lambda, lru_out_g, dp_half, "lru_bwd")
    d_p = d_p2[None]
    (g_wout,), (r1,) = _exchange_wait(po, after=(d_p,))
    own_out, p16_out = _pair_add_call(g_wout, r1, core_chip, "pair_add_out")
    g_win = _mm_tn_cols_call(h1, d_p, N_DEV, w_in.shape[2], "in_proj_dw")
    pi = pair_start(g_win, "pair_in")
    _, (r2_up,) = _exchange_wait(cu, after=(g_win,))
    co = chip_start(p16_out, "chip_out", after=(r2_up,))
    gx_a, dn_a = _mm_nt_norm_call(d_p[0], win_rows, dx2, xs, norm1_g, "in_proj_dx_a", deps=(co.token, pi.token),
                                  part=(0, 2))
    (g_win,), (r1,) = _exchange_wait(pi, after=(gx_a,))
    own_in, p16 = _pair_add_call(g_win, r1, core_chip, "pair_add_in")
    _, (r2_out,) = _exchange_wait(co, after=(p16,))
    ci_ = chip_start(p16, "chip_in", after=(r2_out,))
    grad_x, dn_b = _mm_nt_norm_call(d_p[0], win_rows, dx2, xs, norm1_g, "in_proj_dx_b", deps=(ci_.token,),
                                    part=(1, 2), dx_so_far=gx_a)

    small_g = dict(norm1_g=dn_a + dn_b, gm_v_g=d_vg, gm_v_b=d_vb, gm_ws=d_ws, gm_bs=d_bs[:, :, 0], lru_conv_b=d_lcb,
                   lru_wa=d_wa, lru_ba=d_ba, lru_wx=d_wx, lru_bx=d_bx, lru_lambda=d_lam, gm_out_g=d_gog,
                   lru_out_g=d_log, norm2_g=d_norm2,
                   ffn_conv_b=jnp.concatenate([dfcb_g, dfcb_v], axis=1), final_g=d_final)
    rep = _pack([small_g[n] for n in SMALL])
    conv_part = _pack([d_lcw, jnp.concatenate([dfcw_g, dfcw_v], axis=1)], pad_to=8)
    n_rep, n_conv = rep.shape[0], conv_part.shape[0]
    gs1 = gather_start([(jnp.concatenate([rep, conv_part], axis=0), F32)], "gather_small")

    def adamw_big(n, own, r2, deps=()):
        return _adamw_call(wts[n][0], mom[n][0], var[n][0], [(own, None), (r2, 0), (r2, 1), (r2, 2)], "adamw_" + n, deps)

    res = {}
    res["ffn_w_down"] = adamw_big("ffn_w_down", own_down, r2_down, (gs1.token,))
    res["ffn_w_up"] = adamw_big("ffn_w_up", own_up, r2_up, (gs1.token,))
    res["w_out"] = adamw_big("w_out", own_out, r2_out, (gs1.token,))
    _, ls = _exchange_wait(gs1, after=(res["w_out"][0], res["ffn_w_up"][0], res["ffn_w_down"][0]))
    gs2 = gather_forward(ls, "gather_small")
    _, (r2_in,) = _exchange_wait(ci_, after=(gs2.token,))
    res["w_in"] = adamw_big("w_in", own_in, r2_in)
    _, (parts,) = _exchange_wait(gs2, after=(res["w_in"][0],))
    g_rep, d_rep, m_rep, v_rep = _adamw_call(
        _pack([wts[n] for n in SMALL]), _pack([mom[n] for n in SMALL]), _pack([var[n] for n in SMALL]),
        [(parts, k) for k in range(N_DEV)], "adamw_small")
    shapes = [wts[n].shape for n in SMALL]
    for n, g_, d_, m_, v_ in zip(SMALL, _unpack(g_rep, shapes), _unpack(d_rep, shapes), _unpack(m_rep, shapes),
                                 _unpack(v_rep, shapes)):
        res[n] = (g_, d_, m_, v_)
    conv_sum = _sum_call(parts, n_rep, n_conv, "sum_conv_grads")
    g_lcw = conv_sum[:4 * W // LANES].reshape(4, W)
    g_fcw = conv_sum[4 * W // LANES:4 * W // LANES + 6 * Fd // LANES].reshape(3, 2 * Fd)
    for n, full in (("lru_conv_w", g_lcw), ("ffn_conv_w", g_fcw)):
        cols = wts[n].shape[2]
        mine = lax.dynamic_slice_in_dim(full, dev * cols, cols, axis=1)
        res[n] = _adamw_call(wts[n][0], mom[n][0], var[n][0], [(mine, None)], "adamw_" + n)

    loss = lax.psum(loss_acc[0, 0], ("x", "y", "c"))
    outs = [[], [], [], []]
    for n in WEIGHTS:
        for k in range(4):
            outs[k].append(res[n][k].reshape(wts[n].shape))
    return (loss, grad_x[None], *outs[0], *outs[1], *outs[2], *outs[3])
```

```python
import functools
import math

import jax
import jax.numpy as jnp
from jax import lax
from jax.experimental import pallas as pl
from jax.experimental.pallas import tpu as pltpu

F32 = jnp.float32
BF16 = jnp.bfloat16

RMS_EPS = 1e-6
LN_EPS = 1e-5
LRU_C = 8.0
CHUNK = 128
ADAM_LR = 0.001
ADAM_B1 = 0.9
ADAM_B2 = 0.999
ADAM_EPS = 1e-08
ADAM_WD = 0.01
ADAM_STEP = 10
N_DEV = 8
LANES = 128
MIB = 1024 * 1024

WEIGHTS = ['norm1_g', 'w_in', 'gm_v_g', 'gm_v_b', 'gm_ws', 'gm_bs', 'lru_conv_w', 'lru_conv_b', 'lru_wa', 'lru_ba',
           'lru_wx', 'lru_bx', 'lru_lambda', 'gm_out_g', 'lru_out_g', 'w_out', 'norm2_g', 'ffn_w_up', 'ffn_conv_w',
           'ffn_conv_b', 'ffn_w_down', 'final_g']
BIG = ['w_in', 'w_out', 'ffn_w_up', 'ffn_w_down']
CONV = ['lru_conv_w', 'ffn_conv_w']
SMALL = [n for n in WEIGHTS if n not in BIG and n not in CONV]

_DN_NT = (((1,), (1,)), ((), ()))
_DN_TN = (((0,), (0,)), ((), ()))
_GELU_C = 0.7978845608028654


def _cp(n_axes, vmem_mib=48):
    return pltpu.CompilerParams(dimension_semantics=("arbitrary",) * n_axes, vmem_limit_bytes=vmem_mib * MIB)


def _tile(n, pref, mult=8):
    t = min(pref, n)
    t -= t % mult
    while t >= mult:
        if n % t == 0:
            return t
        t -= mult
    return n


def _gelu_gate(z, z2):
    return 0.5 * jnp.tanh(z * ((_GELU_C * 0.044715) * z2 + _GELU_C)) + 0.5


def _gelu(z):
    return z * _gelu_gate(z, z * z)


def _gelu_parts(z):
    z2 = z * z
    s = _gelu_gate(z, z2)
    g = z * s
    dg = s + g * (1.0 - s) * ((6.0 * _GELU_C * 0.044715) * z2 + 2.0 * _GELU_C)
    return g, dg


def _sigmoid(z):
    return 0.5 + 0.5 * jnp.tanh(0.5 * z)


def _softplus(z):
    t = jnp.exp(-jnp.abs(z))
    u = 1.0 + t
    log1p = jnp.where(u == 1.0, t, jnp.log(u) * t / (u - 1.0))
    return jnp.maximum(z, 0.0) + log1p


def _rows_mean(v):
    return jnp.mean(v, axis=-1, keepdims=True)


def _col_sum(v):
    return jnp.sum(v, axis=0, keepdims=True)


def _shift_down(prev8, cur, k):
    if k == 0:
        return cur
    z = jnp.concatenate([prev8, cur], axis=0)
    return pltpu.roll(z, k, 0)[8:]


def _shift_up(cur, next8, k):
    if k == 0:
        return cur
    n = cur.shape[0]
    z = jnp.concatenate([cur, next8], axis=0)
    return pltpu.roll(z, n + 8 - k, 0)[:n]


def _mesh_pos():
    return lax.axis_index("x"), lax.axis_index("y"), lax.axis_index("c")


def _any_specs(n):
    return [pl.BlockSpec(memory_space=pl.ANY)] * n


def _pallas(body, n_in, deps, **kw):
    nd = len(deps)
    if not nd:
        return pl.pallas_call(body, **kw)

    def ordered(*refs):
        body(*refs[:n_in], *refs[n_in + nd:])

    kw["in_specs"] = list(kw["in_specs"]) + _any_specs(nd)
    return pl.pallas_call(ordered, **kw)


_HBM = pl.BlockSpec(memory_space=pltpu.HBM)
_SEM = pl.BlockSpec(memory_space=pltpu.SEMAPHORE)
_EFFECT = pltpu.SideEffectType.DATAFLOW_SIDE_EFFECTING


class _InFlight:
    def __init__(self, sems, bufs, token, n_src, n_copies, make_copies, name):
        self.sems, self.bufs, self.token = sems, bufs, token
        self.n_src, self.n_copies, self.make_copies, self.name = n_src, n_copies, make_copies, name


def _exchange_start(srcs, lands, n_copies, make_copies, name, after=()):
    bufs = list(srcs) + list(lands)
    nb, na = len(bufs), len(after)
    ns = len(srcs)

    def body(*refs):
        b_refs = refs[:nb]
        outs = refs[nb + na:]
        send, recv = outs[:n_copies], outs[n_copies:2 * n_copies]
        token = outs[-1]
        for cp in make_copies(b_refs[:ns], b_refs[ns:], send, recv):
            cp.start()
        token[...] = jnp.zeros_like(token)

    out = pl.pallas_call(
        body, name=name,
        out_shape=[pltpu.SemaphoreType.DMA(())] * (2 * n_copies) + [pltpu.HBM(b.shape, b.dtype) for b in bufs]
        + [jax.ShapeDtypeStruct((8, LANES), F32)],
        in_specs=[_HBM] * nb + _any_specs(na),
        out_specs=[_SEM] * (2 * n_copies) + [_HBM] * nb + [pl.BlockSpec(memory_space=pltpu.VMEM)],
        input_output_aliases={i: 2 * n_copies + i for i in range(nb)},
        compiler_params=pltpu.CompilerParams(has_side_effects=_EFFECT),
    )(*[pltpu.with_memory_space_constraint(b, pltpu.HBM) for b in bufs], *after)
    return _InFlight(out[:2 * n_copies], out[2 * n_copies:2 * n_copies + nb], out[-1], ns, n_copies, make_copies, name)


def _exchange_wait(fl, after=()):
    nb, na, nc, ns = len(fl.bufs), len(after), fl.n_copies, fl.n_src

    def body(*refs):
        b_refs = refs[:nb]
        sems = refs[nb:nb + 2 * nc]
        copies = fl.make_copies(b_refs[:ns], b_refs[ns:], sems[:nc], sems[nc:])
        for cp in copies:
            cp.wait_send()
        for cp in copies:
            cp.wait_recv()

    out = pl.pallas_call(
        body, name=fl.name + "_wait",
        out_shape=[pltpu.HBM(b.shape, b.dtype) for b in fl.bufs],
        in_specs=[_HBM] * nb + [_SEM] * (2 * nc) + _any_specs(na),
        out_specs=[_HBM] * nb,
        input_output_aliases={i: i for i in range(nb)},
        compiler_params=pltpu.CompilerParams(has_side_effects=_EFFECT),
    )(*fl.bufs, *fl.sems, *after)
    return list(out[:ns]), list(out[ns:])


def _remote(src, dst, send_sem, recv_sem, to):
    return pltpu.make_async_remote_copy(src_ref=src, dst_ref=dst, send_sem=send_sem, recv_sem=recv_sem,
                                        device_id=to, device_id_type=pl.DeviceIdType.MESH)


def _gather_stage1_copies(n, to_sibling=True, to_chips=True):
    def make(s_refs, l_refs, send, recv):
        x, y, c = _mesh_pos()
        own = 4 * x + 2 * y + c
        targets = ([(x, y, 1 - c)] if to_sibling else []) + (
            [(1 - x, y, c), (x, 1 - y, c), (1 - x, 1 - y, c)] if to_chips else [])
        m = len(targets)
        return [_remote(l_refs[a].at[own], l_refs[a].at[own], send[m * a + k], recv[m * a + k], to)
                for a in range(n) for k, to in enumerate(targets)]
    return make


def _gather_stage2_copies(n):
    def make(s_refs, l_refs, send, recv):
        x, y, c = _mesh_pos()
        blocks = [4 * (1 - x) + 2 * y + c, 4 * x + 2 * (1 - y) + c, 4 * (1 - x) + 2 * (1 - y) + c]
        return [_remote(l_refs[a].at[b], l_refs[a].at[b], send[3 * a + j], recv[3 * a + j], (x, y, 1 - c))
                for a in range(n) for j, b in enumerate(blocks)]
    return make


def _pair_copies(n):
    def make(s_refs, l_refs, send, recv):
        x, y, c = _mesh_pos()
        return [_remote(s_refs[a].at[2 * k + 1 - c], l_refs[a].at[k], send[4 * a + k], recv[4 * a + k], (x, y, 1 - c))
                for a in range(n) for k in range(4)]
    return make


def _chip_copies(n):
    def make(s_refs, l_refs, send, recv):
        x, y, c = _mesh_pos()
        chips = [(1 - x, y), (x, 1 - y), (1 - x, 1 - y)]
        return [_remote(s_refs[a].at[2 * ch[0] + ch[1]], l_refs[a].at[j], send[3 * a + j], recv[3 * a + j], (*ch, c))
                for a in range(n) for j, ch in enumerate(chips)]
    return make


def _place_own_call(shard, dev, dtype, name):
    R, C = shard.shape
    tr = _tile(R, max(16, MIB // (C * 4)), 16)

    def body(d_ref, s_ref, o_ref):
        o_ref[...] = s_ref[...].astype(dtype)

    grid_spec = pltpu.PrefetchScalarGridSpec(
        num_scalar_prefetch=1, grid=(R // tr,),
        in_specs=[pl.BlockSpec((tr, C), lambda r, d: (r, 0))],
        out_specs=pl.BlockSpec((None, tr, C), lambda r, d: (d[0], r, 0)))
    return pl.pallas_call(body, name=name, grid_spec=grid_spec,
                          out_shape=jax.ShapeDtypeStruct((N_DEV, R, C), dtype), compiler_params=_cp(1))(dev, shard)


def _pair_add_call(g, r1, core_chip, name):
    _, R, C = g.shape
    tr = _tile(R, max(16, (2 * MIB) // (C * 4)), 16)

    def body(cc_ref, g_ref, r_ref, p32_ref, p16_ref):
        s = g_ref[...] + r_ref[...]
        p16_ref[...] = s.astype(BF16)

        @pl.when(pl.program_id(1) == cc_ref[1])
        def _():
            p32_ref[...] = s

    grid_spec = pltpu.PrefetchScalarGridSpec(
        num_scalar_prefetch=1, grid=(R // tr, 4),
        in_specs=[pl.BlockSpec((None, tr, C), lambda r, k, cc: (2 * k + cc[0], r, 0)),
                  pl.BlockSpec((None, tr, C), lambda r, k, cc: (k, r, 0))],
        out_specs=[pl.BlockSpec((tr, C), lambda r, k, cc: (r, 0)),
                   pl.BlockSpec((None, tr, C), lambda r, k, cc: (k, r, 0))])
    return pl.pallas_call(
        body, name=name, grid_spec=grid_spec,
        out_shape=[jax.ShapeDtypeStruct((R, C), F32), jax.ShapeDtypeStruct((4, R, C), BF16)],
        compiler_params=_cp(2))(core_chip, g, r1)


def _adamw_call(w, m, v, addends, name, deps=()):
    R, C = w.shape
    tr = _tile(R, max(8, (MIB // 2) // (C * 4)), 16)
    na = len(addends)
    c1 = 1.0 - ADAM_B1 ** ADAM_STEP
    c2 = 1.0 - ADAM_B2 ** ADAM_STEP

    def body(*refs):
        w_ref, m_ref, v_ref = refs[:3]
        a_refs = refs[3:3 + na]
        g_ref, d_ref, nm_ref, nv_ref = refs[3 + na:]
        g = a_refs[0][...].astype(F32)
        for a_ref in a_refs[1:]:
            g = g + a_ref[...].astype(F32)
        nm = ADAM_B1 * m_ref[...] + (1.0 - ADAM_B1) * g
        nv = ADAM_B2 * v_ref[...] + (1.0 - ADAM_B2) * (g * g)
        g_ref[...] = g
        nm_ref[...] = nm
        nv_ref[...] = nv
        d_ref[...] = -ADAM_LR * ((nm / c1) / (jnp.sqrt(nv / c2) + ADAM_EPS) + ADAM_WD * w_ref[...])

    flat = pl.BlockSpec((tr, C), lambda r: (r, 0))
    a_specs = [flat if k is None else pl.BlockSpec((None, tr, C), functools.partial(lambda r, kk: (kk, r, 0), kk=k))
               for _, k in addends]
    out = jax.ShapeDtypeStruct((R, C), F32)
    return _pallas(
        body, 3 + na, deps, name=name, grid=(R // tr,),
        in_specs=[flat, flat, flat] + a_specs, out_specs=[flat] * 4, out_shape=[out] * 4,
        compiler_params=_cp(1))(w, m, v, *[a for a, _ in addends], *deps)


def _sum_call(parts, row0, rows, name):
    n = parts.shape[0]
    tr = _tile(math.gcd(row0, rows), 256, 8)
    b0 = row0 // tr

    def body(p_ref, o_ref):
        s = p_ref[0]
        for k in range(1, n):
            s = s + p_ref[k]
        o_ref[...] = s

    return pl.pallas_call(
        body, name=name, grid=(rows // tr,),
        in_specs=[pl.BlockSpec((n, tr, LANES), lambda r: (0, r + b0, 0))],
        out_specs=pl.BlockSpec((tr, LANES), lambda r: (r, 0)),
        out_shape=jax.ShapeDtypeStruct((rows, LANES), F32), compiler_params=_cp(1))(parts)


def _rmsnorm_call(x, g, name, deps=()):
    S, D = x.shape
    tm = _tile(S, 512, 16)

    def body(x_ref, g_ref, o_ref):
        xv = x_ref[...]
        r = lax.rsqrt(_rows_mean(xv * xv) + RMS_EPS)
        o_ref[...] = (xv * r * g_ref[...]).astype(BF16)

    return _pallas(
        body, 2, deps, name=name, grid=(S // tm,),
        in_specs=[pl.BlockSpec((tm, D), lambda i: (i, 0)), pl.BlockSpec((1, D), lambda i: (0, 0))],
        out_specs=pl.BlockSpec((tm, D), lambda i: (i, 0)),
        out_shape=jax.ShapeDtypeStruct((S, D), BF16), compiler_params=_cp(1))(x, g, *deps)


def _mm_some_blocks_call(a, wg, blocks, out_dtype, name, out_so_far=None):
    S, K = a.shape
    nb, _, bn = wg.shape
    tm = _tile(S, 1024, 16)

    def body(b_ref, a_ref, w_ref, *rest):
        rest[-1][...] = jnp.dot(a_ref[...], w_ref[...], preferred_element_type=F32).astype(out_dtype)

    in_specs = [pl.BlockSpec((tm, K), lambda i, j, b: (i, 0)), pl.BlockSpec((None, K, bn), lambda i, j, b: (b[j], 0, 0))]
    operands = [a, wg]
    aliases = {}
    if out_so_far is not None:
        in_specs.append(pl.BlockSpec(memory_space=pl.ANY))
        operands.append(out_so_far)
        aliases = {3: 0}
    grid_spec = pltpu.PrefetchScalarGridSpec(
        num_scalar_prefetch=1, grid=(S // tm, blocks.shape[0]), in_specs=in_specs,
        out_specs=pl.BlockSpec((tm, bn), lambda i, j, b: (i, b[j])))
    return pl.pallas_call(body, name=name, grid_spec=grid_spec,
                          out_shape=jax.ShapeDtypeStruct((S, nb * bn), out_dtype), input_output_aliases=aliases,
                          compiler_params=_cp(2))(blocks, *operands)


def _mm_out_call(x, y, w, name):
    S, D = x.shape
    tm = _tile(S, 512, 16)

    def body(x_ref, y_ref, w_ref, o_ref):
        o_ref[...] = x_ref[...] + jnp.dot(y_ref[...], w_ref[...], preferred_element_type=F32)

    return pl.pallas_call(
        body, name=name, grid=(S // tm,),
        in_specs=[pl.BlockSpec((tm, D), lambda i: (i, 0)), pl.BlockSpec((tm, D), lambda i: (i, 0)),
                  pl.BlockSpec((D, D), lambda i: (0, 0))],
        out_specs=pl.BlockSpec((tm, D), lambda i: (i, 0)),
        out_shape=jax.ShapeDtypeStruct((S, D), F32), compiler_params=_cp(1))(x, y, w)


def _mm_nt_call(a, w, out_dtype, name, deps=()):
    S, K = a.shape
    N = w.shape[0]
    tm = _tile(S, 1024, 16)
    tn = _tile(N, 768, LANES)

    def body(a_ref, w_ref, o_ref):
        o_ref[...] = lax.dot_general(a_ref[...], w_ref[...], _DN_NT, preferred_element_type=F32).astype(out_dtype)

    return _pallas(
        body, 2, deps, name=name, grid=(S // tm, N // tn),
        in_specs=[pl.BlockSpec((tm, K), lambda i, j: (i, 0)), pl.BlockSpec((tn, K), lambda i, j: (j, 0))],
        out_specs=pl.BlockSpec((tm, tn), lambda i, j: (i, j)),
        out_shape=jax.ShapeDtypeStruct((S, N), out_dtype), compiler_params=_cp(2))(a, w, *deps)


def _mm_down_loss_call(x2, f, w, final_g, target, name):
    S, D = x2.shape
    Fd = f.shape[1]
    tm = _tile(S, 512, 16)
    tk = _tile(Fd, 768, LANES)
    nk = Fd // tk

    def body(x_ref, f_ref, w_ref, g_ref, t_ref, dx_ref, dxb_ref, loss_ref, dg_ref, acc):
        i, k = pl.program_id(0), pl.program_id(1)

        @pl.when(jnp.logical_and(i == 0, k == 0))
        def _():
            loss_ref[...] = jnp.zeros_like(loss_ref)
            dg_ref[...] = jnp.zeros_like(dg_ref)

        @pl.when(k == 0)
        def _():
            acc[...] = jnp.zeros_like(acc)

        acc[...] += jnp.dot(f_ref[...], w_ref[...], preferred_element_type=F32)

        @pl.when(k == nk - 1)
        def _():
            x3 = x_ref[...] + acc[...]
            r = lax.rsqrt(_rows_mean(x3 * x3) + RMS_EPS)
            g = g_ref[...]
            xn = x3 * r
            diff = xn * g - t_ref[...]
            loss_ref[...] += 0.5 * jnp.sum(_rows_mean(diff * diff))
            dout = diff * (1.0 / D)
            dg_ref[...] += _col_sum(dout * xn)
            dyg = dout * g
            dx = r * (dyg - xn * _rows_mean(dyg * xn))
            dx_ref[...] = dx
            dxb_ref[...] = dx.astype(BF16)

    row = lambda i, k: (i, 0)
    return pl.pallas_call(
        body, name=name, grid=(S // tm, nk),
        in_specs=[pl.BlockSpec((tm, D), row), pl.BlockSpec((tm, tk), lambda i, k: (i, k)),
                  pl.BlockSpec((tk, D), lambda i, k: (k, 0)), pl.BlockSpec((1, D), lambda i, k: (0, 0)),
                  pl.BlockSpec((tm, D), row)],
        out_specs=[pl.BlockSpec((tm, D), row), pl.BlockSpec((tm, D), row),
                   pl.BlockSpec((8, LANES), lambda i, k: (0, 0)), pl.BlockSpec((1, D), lambda i, k: (0, 0))],
        out_shape=[jax.ShapeDtypeStruct((S, D), F32), jax.ShapeDtypeStruct((S, D), BF16),
                   jax.ShapeDtypeStruct((8, LANES), F32), jax.ShapeDtypeStruct((1, D), F32)],
        scratch_shapes=[pltpu.VMEM((tm, D), F32)], compiler_params=_cp(2, 56))(x2, f, w, final_g, target)


def _mm_dx_norm_call(a3, wg, resid, xin, g, name, deps=()):
    na, S, Fa = a3.shape
    nb, D, bn = wg.shape
    tm = _tile(S, 512, 16)
    tk = _tile(bn, 1536, LANES)
    nsub = bn // tk
    nka = Fa // tk
    nk = nb * nsub
    assert na * nka == nk

    def body(a_ref, w_ref, r_ref, x_ref, g_ref, dx_ref, dxb_ref, dg_ref, acc):
        i, k = pl.program_id(0), pl.program_id(1)

        @pl.when(jnp.logical_and(i == 0, k == 0))
        def _():
            dg_ref[...] = jnp.zeros_like(dg_ref)

        @pl.when(k == 0)
        def _():
            acc[...] = jnp.zeros_like(acc)

        acc[...] += lax.dot_general(a_ref[...], w_ref[...], _DN_NT, preferred_element_type=F32)

        @pl.when(k == nk - 1)
        def _():
            dh = acc[...]
            xv = x_ref[...]
            r = lax.rsqrt(_rows_mean(xv * xv) + RMS_EPS)
            xn = xv * r
            dg_ref[...] += _col_sum(dh * xn)
            dyg = dh * g_ref[...]
            dx = r_ref[...] + r * (dyg - xn * _rows_mean(dyg * xn))
            dx_ref[...] = dx
            dxb_ref[...] = dx.astype(BF16)

    row = lambda i, k: (i, 0)
    return _pallas(
        body, 5, deps, name=name, grid=(S // tm, nk),
        in_specs=[pl.BlockSpec((None, tm, tk), lambda i, k: (k // nka, i, k % nka)),
                  pl.BlockSpec((None, D, tk), lambda i, k: (k // nsub, 0, k % nsub)),
                  pl.BlockSpec((tm, D), row, pipeline_mode=pl.Buffered(1)),
                  pl.BlockSpec((tm, D), row, pipeline_mode=pl.Buffered(1)), pl.BlockSpec((1, D), lambda i, k: (0, 0))],
        out_specs=[pl.BlockSpec((tm, D), row), pl.BlockSpec((tm, D), row), pl.BlockSpec((1, D), lambda i, k: (0, 0))],
        out_shape=[jax.ShapeDtypeStruct((S, D), F32), jax.ShapeDtypeStruct((S, D), BF16),
                   jax.ShapeDtypeStruct((1, D), F32)],
        scratch_shapes=[pltpu.VMEM((tm, D), F32)], compiler_params=_cp(2, 56))(a3, wg, resid, xin, g, *deps)


def _unblock_call(wg, name):
    nb, K, bn = wg.shape

    def body(w_ref, o_ref):
        o_ref[...] = w_ref[...]

    return pl.pallas_call(
        body, name=name, grid=(nb,),
        in_specs=[pl.BlockSpec((None, K, bn), lambda o: (o, 0, 0))],
        out_specs=pl.BlockSpec((K, bn), lambda o: (0, o)),
        out_shape=jax.ShapeDtypeStruct((K, nb * bn), wg.dtype), compiler_params=_cp(1))(wg)


def _mm_nt_norm_call(a, w, resid, xin, g, name, deps=(), part=(0, 1), dx_so_far=None):
    S, K = a.shape
    D = w.shape[0]
    tm = _tile(S, 256, 16)
    tiles = (S // tm) // part[1]
    first = part[0] * tiles

    def body(a_ref, w_ref, r_ref, x_ref, g_ref, *rest):
        dx_ref, dg_ref = rest[-2:]

        @pl.when(pl.program_id(0) == 0)
        def _():
            dg_ref[...] = jnp.zeros_like(dg_ref)

        dh = lax.dot_general(a_ref[...], w_ref[...], _DN_NT, preferred_element_type=F32)
        xv = x_ref[...]
        r = lax.rsqrt(_rows_mean(xv * xv) + RMS_EPS)
        xn = xv * r
        dg_ref[...] += _col_sum(dh * xn)
        dyg = dh * g_ref[...]
        dx_ref[...] = r_ref[...] + r * (dyg - xn * _rows_mean(dyg * xn))

    row = lambda i: (i + first, 0)
    fixed = lambda i: (0, 0)
    in_specs = [pl.BlockSpec((tm, K), row), pl.BlockSpec((D, K), fixed, pipeline_mode=pl.Buffered(1)),
                pl.BlockSpec((tm, D), row), pl.BlockSpec((tm, D), row), pl.BlockSpec((1, D), fixed)]
    operands = [a, w, resid, xin, g]
    aliases = {}
    if dx_so_far is not None:
        in_specs.append(pl.BlockSpec(memory_space=pl.ANY))
        operands.append(dx_so_far)
        aliases = {5: 0}
    return _pallas(
        body, len(operands), deps, name=name, grid=(tiles,),
        in_specs=in_specs, out_specs=[pl.BlockSpec((tm, D), row), pl.BlockSpec((1, D), fixed)],
        out_shape=[jax.ShapeDtypeStruct((S, D), F32), jax.ShapeDtypeStruct((1, D), F32)],
        input_output_aliases=aliases, compiler_params=_cp(1, 56))(*operands, *deps)


def _mm_tn_cols_call(a, b3, nb, bn, name, deps=()):
    S, Ka = a.shape
    nh, _, Fb = b3.shape
    tm = _tile(S, 2048, 16)
    tn = _tile(bn, 768, LANES)
    nsub = bn // tn
    njb = Fb // tn
    J = nb * nsub
    assert nh * njb == J

    def body(a_ref, b_ref, o_ref):
        @pl.when(pl.program_id(1) == 0)
        def _():
            o_ref[...] = jnp.zeros_like(o_ref)

        o_ref[...] += lax.dot_general(a_ref[...], b_ref[...], _DN_TN, preferred_element_type=F32)

    return _pallas(
        body, 2, deps, name=name, grid=(J, S // tm),
        in_specs=[pl.BlockSpec((tm, Ka), lambda j, i: (i, 0)),
                  pl.BlockSpec((None, tm, tn), lambda j, i: (j // njb, i, j % njb))],
        out_specs=pl.BlockSpec((None, Ka, tn), lambda j, i: (j // nsub, 0, j % nsub)),
        out_shape=jax.ShapeDtypeStruct((nb, Ka, bn), F32), compiler_params=_cp(2, 56))(a, b3, *deps)


def _mm_tn_rows_call(a, b, name, deps=()):
    S, E = a.shape
    D = b.shape[1]
    tm = _tile(S, 2048, 16)
    te = _tile(E, 768, LANES)

    def body(a_ref, b_ref, o_ref):
        @pl.when(pl.program_id(1) == 0)
        def _():
            o_ref[...] = jnp.zeros_like(o_ref)

        o_ref[...] += lax.dot_general(a_ref[...], b_ref[...], _DN_TN, preferred_element_type=F32)

    return _pallas(
        body, 2, deps, name=name, grid=(E // te, S // tm),
        in_specs=[pl.BlockSpec((tm, te), lambda j, i: (i, j)), pl.BlockSpec((tm, D), lambda j, i: (i, 0))],
        out_specs=pl.BlockSpec((te, D), lambda j, i: (j, 0)),
        out_shape=jax.ShapeDtypeStruct((E, D), F32), compiler_params=_cp(2, 56))(a, b, *deps)


def _ffn_tiles(S, Fd):
    return _tile(S, 512, 16), _tile(Fd // (N_DEV // 2), 1536, LANES)


def _taps(cw_ref):
    return [cw_ref[k:k + 1, :] for k in range(cw_ref.shape[0])]


def _conv3(prev8, cur, taps):
    s1 = _shift_down(prev8, cur, 1)
    s2 = _shift_down(prev8, cur, 2)
    return taps[2] * cur + taps[1] * s1 + taps[0] * s2, s1, s2


GATE_LANES = 256
SUB_LANES = 1536


def _lane_taps(cw_ref, ls):
    return [cw_ref[k:k + 1, ls] for k in range(cw_ref.shape[0])]


def _ffn_up_act_call(h2, wg, cw, cb, name, deps=()):
    S, D = h2.shape
    nb, _, bn = wg.shape
    Fd = nb * bn // 2
    tm, tc = _ffn_tiles(S, Fd)
    nk = Fd // tc
    hb = tm // 16
    nsubw = bn // tc
    half = nb // 2
    sc = _tile(tc, SUB_LANES, LANES)

    def body(a_ref, ap_ref, wgate_ref, wval_ref, cwg_ref, cwv_ref, cbg_ref, cbv_ref, up_ref, upc_ref, f_ref):
        keep = jnp.where(pl.program_id(0) == 0, 0.0, 1.0)
        a_ext = jnp.concatenate([ap_ref[...], a_ref[...]], axis=0)
        nsub = tc // sc
        lanes = [slice(s * sc, (s + 1) * sc) for s in range(nsub)]

        def products(s):
            return [jnp.dot(a_ext, w_ref[:, lanes[s]], preferred_element_type=F32).astype(BF16)
                    for w_ref in (wgate_ref, wval_ref)]

        ready = products(0)
        for s in range(nsub):
            ls = lanes[s]
            following = products(s + 1) if s + 1 < nsub else None

            def conv_half(ub, cw_ref, cb_ref, slab):
                up_ref[slab, :, ls] = ub[16:]
                u = ub.astype(F32)
                conv, _, _ = _conv3(u[8:16] * keep, u[16:], _lane_taps(cw_ref, ls))
                c = conv + cb_ref[:, ls]
                upc_ref[slab, :, ls] = c.astype(BF16)
                return c

            cg = conv_half(ready[0], cwg_ref, cbg_ref, 0)
            cv = conv_half(ready[1], cwv_ref, cbv_ref, 1)
            f_ref[:, ls] = (_gelu(cg) * cv).astype(BF16)
            ready = following

    return _pallas(
        body, 8, deps, name=name, grid=(S // tm, nk),
        in_specs=[pl.BlockSpec((tm, D), lambda i, k: (i, 0)),
                  pl.BlockSpec((16, D), lambda i, k: (jnp.maximum(i * hb - 1, 0), 0)),
                  pl.BlockSpec((None, D, tc), lambda i, k: (k // nsubw, 0, k % nsubw)),
                  pl.BlockSpec((None, D, tc), lambda i, k: (half + k // nsubw, 0, k % nsubw)),
                  pl.BlockSpec((3, tc), lambda i, k: (0, k)), pl.BlockSpec((3, tc), lambda i, k: (0, k + nk)),
                  pl.BlockSpec((1, tc), lambda i, k: (0, k)), pl.BlockSpec((1, tc), lambda i, k: (0, k + nk))],
        out_specs=[pl.BlockSpec((2, tm, tc), lambda i, k: (0, i, k)), pl.BlockSpec((2, tm, tc), lambda i, k: (0, i, k)),
                   pl.BlockSpec((tm, tc), lambda i, k: (i, k))],
        out_shape=[jax.ShapeDtypeStruct((2, S, Fd), BF16), jax.ShapeDtypeStruct((2, S, Fd), BF16),
                   jax.ShapeDtypeStruct((S, Fd), BF16)],
        compiler_params=_cp(2, 56))(h2, h2, wg, wg, cw, cw, cb, cb, *deps)


def _ffn_down_dx_act_bwd_call(dxb, w, up3, upc3, cw, name, deps=()):
    S, D = dxb.shape
    _, _, Fd = up3.shape
    tm, tc = _ffn_tiles(S, Fd)
    nj = Fd // tc
    ni = S // tm
    hb = tm // 16
    sc = _tile(tc, SUB_LANES, LANES)

    def body(a_ref, an_ref, w_ref, g_ref, v_ref, cg_ref, cv_ref, cgn_ref, cvn_ref, cwg_ref, cwv_ref,
             dup_ref, dcwg_ref, dcwv_ref, dcbg_ref, dcbv_ref):
        i = pl.program_id(1)
        keep_next = jnp.where(i == ni - 1, 0.0, 1.0)

        @pl.when(i == 0)
        def _():
            for r in (dcwg_ref, dcwv_ref, dcbg_ref, dcbv_ref):
                r[...] = jnp.zeros_like(r)

        a_ext = jnp.concatenate([a_ref[...], an_ref[...]], axis=0)
        for s in range(tc // sc):
            ls = slice(s * sc, (s + 1) * sc)
            df_ext = lax.dot_general(a_ext, w_ref[s * sc:(s + 1) * sc, :], _DN_NT, preferred_element_type=F32)
            df = jnp.concatenate([df_ext[:tm], df_ext[tm:tm + 8] * keep_next], axis=0)
            cg = jnp.concatenate([cg_ref[:, ls].astype(F32), cgn_ref[:, ls].astype(F32)[:8]], axis=0)
            cv = jnp.concatenate([cv_ref[:, ls].astype(F32), cvn_ref[:, ls].astype(F32)[:8]], axis=0)
            gel, dgel = _gelu_parts(cg)

            def back(d, cw_ref, x_ref, dcw_ref, dcb_ref, slab):
                taps = _lane_taps(cw_ref, ls)
                d0 = d[:tm]
                d1 = pltpu.roll(d, tm + 8 - 1, 0)[:tm]
                d2 = pltpu.roll(d, tm + 8 - 2, 0)[:tm]
                dup_ref[slab, :, ls] = (taps[2] * d0 + taps[1] * d1 + taps[0] * d2).astype(BF16)
                xv = x_ref[:, ls].astype(F32)
                dcw_ref[2:3, ls] += _col_sum(xv * d0)
                dcw_ref[1:2, ls] += _col_sum(xv * d1)
                dcw_ref[0:1, ls] += _col_sum(xv * d2)
                dcb_ref[:, ls] += _col_sum(d0)

            back(df * cv * dgel, cwg_ref, g_ref, dcwg_ref, dcbg_ref, 0)
            back(df * gel, cwv_ref, v_ref, dcwv_ref, dcbv_ref, 1)

    nxt = lambda j, i: jnp.minimum((i + 1) * hb, S // 16 - 1)
    main = lambda s: pl.BlockSpec((None, tm, tc), lambda j, i: (s, i, j))
    halo = lambda s: pl.BlockSpec((None, 16, tc), lambda j, i: (s, nxt(j, i), j))
    acc3 = pl.BlockSpec((3, tc), lambda j, i: (0, j))
    acc1 = pl.BlockSpec((1, tc), lambda j, i: (0, j))
    return _pallas(
        body, 11, deps, name=name, grid=(nj, ni),
        in_specs=[pl.BlockSpec((tm, D), lambda j, i: (i, 0)), pl.BlockSpec((16, D), lambda j, i: (nxt(j, i), 0)),
                  pl.BlockSpec((tc, D), lambda j, i: (j, 0)),
                  main(0), main(1), main(0), main(1), halo(0), halo(1),
                  pl.BlockSpec((3, tc), lambda j, i: (0, j)), pl.BlockSpec((3, tc), lambda j, i: (0, j + nj))],
        out_specs=[pl.BlockSpec((2, tm, tc), lambda j, i: (0, i, j)), acc3, acc3, acc1, acc1],
        out_shape=[jax.ShapeDtypeStruct((2, S, Fd), BF16), jax.ShapeDtypeStruct((3, Fd), F32),
                   jax.ShapeDtypeStruct((3, Fd), F32), jax.ShapeDtypeStruct((1, Fd), F32),
                   jax.ShapeDtypeStruct((1, Fd), F32)],
        compiler_params=_cp(2, 56))(dxb, dxb, w, up3, up3, upc3, upc3, upc3, upc3, cw, cw, *deps)


def _gm_forward_tile(pv, vg, vb, ws_ref, bsb_ref, mbuf, H, nc):
    W = H * CHUNK
    z, dz = _gelu_parts(pv)
    u, v0 = z[:, :W], z[:, W:]
    xc = v0 - _rows_mean(v0)
    rs = lax.rsqrt(_rows_mean(xc * xc) + LN_EPS)
    vh = xc * rs
    vnb = (vh * vg + vb).astype(BF16)
    mask = lax.broadcasted_iota(jnp.int32, (CHUNK, CHUNK), 0) >= lax.broadcasted_iota(jnp.int32, (CHUNK, CHUNK), 1)
    for h in range(H):
        cs = slice(h * CHUNK, (h + 1) * CHUNK)
        wm = jnp.where(mask, ws_ref[h], 0.0).astype(BF16)
        vcat = jnp.concatenate([vnb[c * CHUNK:(c + 1) * CHUNK, cs] for c in range(nc)], axis=1)
        mix = jnp.dot(wm, vcat, preferred_element_type=F32)
        for c in range(nc):
            mbuf[c * CHUNK:(c + 1) * CHUNK, cs] = mix[:, c * CHUNK:(c + 1) * CHUNK] + bsb_ref[h]
    return dz, u, vh, rs, vnb, mask


def _gm_fwd_call(p, v_g, v_b, ws, bsb, out_g, name, deps=()):
    S = p.shape[0]
    H = ws.shape[0]
    W = H * CHUNK
    tm = _tile(S, 256, CHUNK)
    nc = tm // CHUNK

    def body(p_ref, vg_ref, vb_ref, ws_ref, bsb_ref, og_ref, y_ref, mbuf):
        _, u, _, _, _, _ = _gm_forward_tile(p_ref[...], vg_ref[...], vb_ref[...], ws_ref, bsb_ref, mbuf, H, nc)
        yg = u * mbuf[...]
        r = lax.rsqrt(_rows_mean(yg * yg) + RMS_EPS)
        y_ref[...] = (yg * r * og_ref[...]).astype(BF16)

    vec = pl.BlockSpec((1, W), lambda i: (0, 0))
    mat = pl.BlockSpec((H, CHUNK, CHUNK), lambda i: (0, 0, 0))
    return _pallas(
        body, 6, deps, name=name, grid=(S // tm,),
        in_specs=[pl.BlockSpec((tm, 2 * W), lambda i: (i, 0)), vec, vec, mat, mat, vec],
        out_specs=pl.BlockSpec((tm, W), lambda i: (i, 0)),
        out_shape=jax.ShapeDtypeStruct((S, 2 * W), BF16),
        scratch_shapes=[pltpu.VMEM((tm, W), F32)], compiler_params=_cp(1))(p, v_g, v_b, ws, bsb, out_g, *deps)


def _gm_bwd_call(p, d_y, v_g, v_b, ws, bsb, out_g, name, deps=()):
    S = p.shape[0]
    H = ws.shape[0]
    W = H * CHUNK
    tm = _tile(S, 256, CHUNK)
    nc = tm // CHUNK
    ni = S // tm

    def body(p_ref, dy_ref, vg_ref, vb_ref, ws_ref, bsb_ref, og_ref,
             dp_ref, dvg_ref, dvb_ref, dws_ref, dbs_ref, dog_ref, mbuf, dvbuf):
        i = pl.program_id(0)

        @pl.when(i == 0)
        def _():
            for r in (dvg_ref, dvb_ref, dws_ref, dbs_ref, dog_ref):
                r[...] = jnp.zeros_like(r)

        vg = vg_ref[...]
        dz, u, vh, rs, vnb, mask = _gm_forward_tile(p_ref[...], vg, vb_ref[...], ws_ref, bsb_ref, mbuf, H, nc)
        mixed = mbuf[...]
        yg = u * mixed
        r = lax.rsqrt(_rows_mean(yg * yg) + RMS_EPS)
        yn = yg * r
        dya = dy_ref[...]
        dog_ref[...] += _col_sum(dya * yn)
        dyg = dya * og_ref[...]
        dygm = r * (dyg - yn * _rows_mean(dyg * yn))
        du = dygm * mixed
        dmix = dygm * u
        dmb = dmix.astype(BF16)
        for h in range(H):
            cs = slice(h * CHUNK, (h + 1) * CHUNK)
            wm = jnp.where(mask, ws_ref[h], 0.0).astype(BF16)
            dcat = jnp.concatenate([dmb[c * CHUNK:(c + 1) * CHUNK, cs] for c in range(nc)], axis=1)
            vcat = jnp.concatenate([vnb[c * CHUNK:(c + 1) * CHUNK, cs] for c in range(nc)], axis=1)
            dvn = lax.dot_general(wm, dcat, _DN_TN, preferred_element_type=F32)
            dws_ref[h] += jnp.where(mask, lax.dot_general(dcat, vcat, _DN_NT, preferred_element_type=F32), 0.0)
            dbs = dmix[0:CHUNK, cs]
            for c in range(1, nc):
                dbs = dbs + dmix[c * CHUNK:(c + 1) * CHUNK, cs]
            dbs_ref[h] += dbs
            for c in range(nc):
                dvbuf[c * CHUNK:(c + 1) * CHUNK, cs] = dvn[:, c * CHUNK:(c + 1) * CHUNK]
        dvn_all = dvbuf[...]
        dvg_ref[...] += _col_sum(dvn_all * vh)
        dvb_ref[...] += _col_sum(dvn_all)
        dvh = dvn_all * vg
        dv0 = rs * (dvh - _rows_mean(dvh) - vh * _rows_mean(dvh * vh))
        dp_ref[...] = (jnp.concatenate([du, dv0], axis=1) * dz).astype(BF16)

        @pl.when(i == ni - 1)
        def _():
            for h in range(H):
                dbs_ref[h] = jnp.broadcast_to(jnp.sum(dbs_ref[h], axis=1, keepdims=True), (CHUNK, CHUNK))

    vec = pl.BlockSpec((1, W), lambda i: (0, 0))
    mat = pl.BlockSpec((H, CHUNK, CHUNK), lambda i: (0, 0, 0))
    vshape = jax.ShapeDtypeStruct((1, W), F32)
    mshape = jax.ShapeDtypeStruct((H, CHUNK, CHUNK), F32)
    return _pallas(
        body, 7, deps, name=name, grid=(ni,),
        in_specs=[pl.BlockSpec((tm, 2 * W), lambda i: (i, 0)), pl.BlockSpec((tm, W), lambda i: (i, 0)),
                  vec, vec, mat, mat, vec],
        out_specs=[pl.BlockSpec((tm, 2 * W), lambda i: (i, 0)), vec, vec, mat, mat, vec],
        out_shape=[jax.ShapeDtypeStruct((S, 4 * W), BF16), vshape, vshape, mshape, mshape, vshape],
        scratch_shapes=[pltpu.VMEM((tm, W), F32), pltpu.VMEM((tm, W), F32)],
        compiler_params=_cp(1))(p, d_y, v_g, v_b, ws, bsb, out_g, *deps)


def _lru_gates(prev8, xl, cw, cb, wa_ref, ba, wx_ref, bx, lam, H):
    sh = [_shift_down(prev8, xl, k) for k in range(4)]
    xr = cw[3] * sh[0] + cw[2] * sh[1] + cw[1] * sh[2] + cw[0] * sh[3] + cb
    xrb = xr.astype(BF16)
    rp, ip = [], []
    for h in range(H):
        cs = slice(h * CHUNK, (h + 1) * CHUNK)
        rp.append(jnp.dot(xrb[:, cs], wa_ref[h].astype(BF16), preferred_element_type=F32))
        ip.append(jnp.dot(xrb[:, cs], wx_ref[h].astype(BF16), preferred_element_type=F32))
    r = _sigmoid(jnp.concatenate(rp, axis=1) + ba)
    ig = _sigmoid(jnp.concatenate(ip, axis=1) + bx)
    sp = _softplus(-lam)
    t = jnp.tanh((-LRU_C) * r * sp)
    q = lax.rsqrt(1.0 - t)
    a = jnp.sqrt(1.0 + t) * q
    mult = jnp.sqrt(-2.0 * t) * q
    a2_over_mult = (1.0 + t) * q * lax.rsqrt(-2.0 * t)
    return xr, xrb, r, ig, sp, a, mult, a2_over_mult, sh


def _lru_fwd_call(p, cw, cb, wa, ba, wx, bx, lam, out_g, y_half, name):
    S = p.shape[0]
    H = wa.shape[0]
    W = H * CHUNK
    tm = _tile(S, 256, 16)
    ng = tm // 8

    def body(pg_ref, px_ref, cw_ref, cb_ref, wa_ref, ba_ref, wx_ref, bx_ref, lam_ref, og_ref, y_in_ref,
             y_ref, h_ref, saved_ref, xprev, hcar, abuf, bbuf):
        @pl.when(pl.program_id(0) == 0)
        def _():
            xprev[...] = jnp.zeros_like(xprev)
            hcar[...] = jnp.zeros_like(hcar)

        xl = px_ref[...]
        xr, _, r_gate, ig, _, a, mult, a2m, _ = _lru_gates(xprev[...], xl, _taps(cw_ref), cb_ref[...], wa_ref,
                                                           ba_ref[...], wx_ref, bx_ref[...], lam_ref[...], H)
        for k, val in enumerate((xr, r_gate, ig, a, mult, a2m)):
            saved_ref[k] = val
        xprev[...] = xl[tm - 8:]
        b = mult * (ig * xr)
        sub = lax.broadcasted_iota(jnp.int32, (tm, W), 0) & 7
        for d in (1, 2, 4):
            m = sub >= d
            a_s = jnp.where(m, pltpu.roll(a, d, 0), 1.0)
            b_s = jnp.where(m, pltpu.roll(b, d, 0), 0.0)
            b = a * b_s + b
            a = a * a_s
        abuf[...] = a
        bbuf[...] = b

        def step(g, carry):
            r0 = pl.multiple_of(g * 8, 8)
            h_ref[pl.ds(r0, 8), :] = abuf[pl.ds(r0, 8), :] * carry + bbuf[pl.ds(r0, 8), :]
            return jnp.broadcast_to(h_ref[pl.ds(r0 + 7, 1), :], (8, W))

        hcar[...] = lax.fori_loop(0, ng, step, hcar[...])
        yl = h_ref[...] * _gelu(pg_ref[...])
        r = lax.rsqrt(_rows_mean(yl * yl) + RMS_EPS)
        y_ref[...] = (yl * r * og_ref[...]).astype(BF16)

    vec = pl.BlockSpec((1, W), lambda i: (0, 0))
    mat = pl.BlockSpec((H, CHUNK, CHUNK), lambda i: (0, 0, 0))
    return pl.pallas_call(
        body, name=name, grid=(S // tm,),
        in_specs=[pl.BlockSpec((tm, W), lambda i: (i, 2)), pl.BlockSpec((tm, W), lambda i: (i, 3)),
                  pl.BlockSpec((4, W), lambda i: (0, 0)), vec, mat, vec, mat, vec, vec, vec,
                  pl.BlockSpec(memory_space=pl.ANY)],
        out_specs=[pl.BlockSpec((tm, W), lambda i: (i, 1)), pl.BlockSpec((tm, W), lambda i: (i, 0)),
                   pl.BlockSpec((6, tm, W), lambda i: (0, i, 0))],
        out_shape=[jax.ShapeDtypeStruct((S, 2 * W), BF16), jax.ShapeDtypeStruct((S, W), F32),
                   jax.ShapeDtypeStruct((6, S, W), F32)],
        input_output_aliases={10: 0},
        scratch_shapes=[pltpu.VMEM((8, W), F32), pltpu.VMEM((8, W), F32), pltpu.VMEM((tm, W), F32),
                        pltpu.VMEM((tm, W), F32)],
        compiler_params=_cp(1))(p, p, cw, cb, wa, ba, wx, bx, lam, out_g, y_half)


def _lru_bwd_call(p, hs, saved, d_y, cw, wa, wx, lam, out_g, dp_half, name):
    S = p.shape[0]
    H = wa.shape[0]
    W = H * CHUNK
    tm = _tile(S, 256, 16)
    ng = tm // 8
    ni = S // tm
    hb = tm // 8

    def body(pg_ref, px_ref, saved_ref, h_ref, hp_ref, dy_ref, cw_ref, wa_ref, wx_ref, lam_ref, og_ref, dp_in_ref,
             dp_ref, dcw_ref, dcb_ref, dwa_ref, dba_ref, dwx_ref, dbx_ref, dlam_ref, dog_ref,
             a_next, e_next, dxr_next, abuf, bbuf, ebuf, dxr0, dprb_buf, dpib_buf, sums):
        i = pl.program_id(0)
        ri = ni - 1 - i

        @pl.when(i == 0)
        def _():
            for r in (dcw_ref, dcb_ref, dwa_ref, dba_ref, dwx_ref, dbx_ref, dlam_ref, dog_ref,
                      a_next, e_next, dxr_next):
                r[...] = jnp.zeros_like(r)

        keep_prev = jnp.where(ri == 0, 0.0, 1.0)
        cw_ = _taps(cw_ref)
        lam_ = lam_ref[...]
        xl = px_ref[...]
        a = saved_ref[3]
        sp = _softplus(-lam_)
        sums[...] = jnp.zeros_like(sums)
        og = og_ref[...]
        lane_groups = [slice(q * GATE_LANES, (q + 1) * GATE_LANES) for q in range(W // GATE_LANES)]

        def fold(v):
            return sum(v[:, k * LANES:(k + 1) * LANES] for k in range(GATE_LANES // LANES))

        def out_grads(g, carry):
            rows = pl.ds(pl.multiple_of(g * 16, 16), 16)
            s_yy = jnp.zeros((16, LANES), F32)
            s_dy = jnp.zeros((16, LANES), F32)
            for ls in lane_groups:
                yl = h_ref[rows, ls] * _gelu(pg_ref[rows, ls])
                s_yy = s_yy + fold(yl * yl)
                s_dy = s_dy + fold(dy_ref[rows, ls] * og[:, ls] * yl)
            rr = lax.rsqrt(jnp.sum(s_yy, axis=1, keepdims=True) * (1.0 / W) + RMS_EPS)
            c = rr * rr * jnp.sum(s_dy, axis=1, keepdims=True) * (1.0 / W)
            for ls in lane_groups:
                hq = h_ref[rows, ls]
                gg, dgg = _gelu_parts(pg_ref[rows, ls])
                yl = hq * gg
                dyb = dy_ref[rows, ls]
                sums[3, :, ls] += dyb * (yl * rr)
                dyl = rr * (dyb * og[:, ls] - yl * c)
                bbuf[rows, ls] = dyl * gg
                dp_ref[rows, ls] = (dyl * hq * dgg).astype(BF16)
            return carry

        lax.fori_loop(0, tm // 16, out_grads, 0, unroll=2)
        dog_ref[...] += _col_sum(sums[3])

        an = _shift_up(a, a_next[...], 1)
        eb = bbuf[...]
        sub = lax.broadcasted_iota(jnp.int32, (tm, W), 0) & 7
        for d in (1, 2, 4):
            m = sub < 8 - d
            a_s = jnp.where(m, pltpu.roll(an, tm - d, 0), 1.0)
            e_s = jnp.where(m, pltpu.roll(eb, tm - d, 0), 0.0)
            eb = an * e_s + eb
            an = an * a_s
        abuf[...] = an
        bbuf[...] = eb

        def step(g, carry):
            r0 = pl.multiple_of((ng - 1 - g) * 8, 8)
            ebuf[pl.ds(r0, 8), :] = abuf[pl.ds(r0, 8), :] * carry + bbuf[pl.ds(r0, 8), :]
            return jnp.broadcast_to(ebuf[pl.ds(r0, 1), :], (8, W))

        lax.fori_loop(0, ng, step, jnp.broadcast_to(e_next[0:1, :], (8, W)))
        a_next[...] = a[0:8]
        e_next[...] = ebuf[0:8, :]

        row16 = lax.broadcasted_iota(jnp.int32, (16, GATE_LANES), 0)
        h_before = hp_ref[7:8, :] * keep_prev
        r_scale = (-LRU_C) * sp

        def gate_grads(g, carry):
            r0 = pl.multiple_of(g * 16, 16)
            above = jnp.maximum(r0 - 1, 0)
            for q in range(W // GATE_LANES):
                ls = slice(q * GATE_LANES, (q + 1) * GATE_LANES)
                e = ebuf[pl.ds(r0, 16), ls]
                h_prev_row = jnp.where(g == 0, h_before[:, ls], h_ref[pl.ds(above, 1), ls])
                hm1 = jnp.where(row16 == 0, h_prev_row, pltpu.roll(h_ref[pl.ds(r0, 16), ls], 1, 0))
                xr_, r_, ig_, a_, mult_, a2m_ = [saved_ref[k, pl.ds(r0, 16), ls] for k in range(6)]
                em = e * mult_
                dxr0[pl.ds(r0, 16), ls] = em * ig_
                dla = e * hm1 * a_ - e * ig_ * xr_ * a2m_
                dpr = dla * r_scale[:, ls] * r_ * (1.0 - r_)
                dpi = em * xr_ * ig_ * (1.0 - ig_)
                sums[0, :, ls] += dla * r_
                sums[1, :, ls] += dpr
                sums[2, :, ls] += dpi
                dprb_buf[pl.ds(r0, 16), ls] = dpr.astype(BF16)
                dpib_buf[pl.ds(r0, 16), ls] = dpi.astype(BF16)
            return carry

        lax.fori_loop(0, tm // 16, gate_grads, 0)
        dlam_ref[...] += (-LRU_C) * _col_sum(sums[0])
        dba_ref[...] += _col_sum(sums[1])
        dbx_ref[...] += _col_sum(sums[2])
        dprb = dprb_buf[...]
        dpib = dpib_buf[...]
        dxr = dxr0[...]
        xrb = saved_ref[0].astype(BF16)
        back = []
        for h in range(H):
            cs = slice(h * CHUNK, (h + 1) * CHUNK)
            wab = wa_ref[h].astype(BF16)
            wxb = wx_ref[h].astype(BF16)
            back.append(lax.dot_general(dprb[:, cs], wab, _DN_NT, preferred_element_type=F32)
                        + lax.dot_general(dpib[:, cs], wxb, _DN_NT, preferred_element_type=F32))
            dwa_ref[h] += lax.dot_general(xrb[:, cs], dprb[:, cs], _DN_TN, preferred_element_type=F32)
            dwx_ref[h] += lax.dot_general(xrb[:, cs], dpib[:, cs], _DN_TN, preferred_element_type=F32)
        dxr = dxr + jnp.concatenate(back, axis=1)

        nxt = dxr_next[...]
        ahead = [_shift_up(dxr, nxt, j) for j in range(4)]
        dxl = cw_[3] * ahead[0] + cw_[2] * ahead[1] + cw_[1] * ahead[2] + cw_[0] * ahead[3]
        dxr_next[...] = dxr[0:8]
        for k in range(4):
            dcw_ref[k:k + 1, :] += _col_sum(xl * ahead[3 - k])
        dcb_ref[...] += _col_sum(dxr)
        dp_ref[:, W:] = dxl.astype(BF16)

        @pl.when(i == ni - 1)
        def _():
            dlam_ref[...] = -dlam_ref[...] * _sigmoid(-lam_)

    vec = pl.BlockSpec((1, W), lambda i: (0, 0))
    mat = pl.BlockSpec((H, CHUNK, CHUNK), lambda i: (0, 0, 0))
    rev = lambda i: ni - 1 - i
    prev = lambda i: jnp.maximum(rev(i) * hb - 1, 0)
    vshape = jax.ShapeDtypeStruct((1, W), F32)
    mshape = jax.ShapeDtypeStruct((H, CHUNK, CHUNK), F32)
    tile = lambda: pltpu.VMEM((tm, W), F32)
    car = lambda: pltpu.VMEM((8, W), F32)
    return pl.pallas_call(
        body, name=name, grid=(ni,),
        in_specs=[pl.BlockSpec((tm, W), lambda i: (rev(i), 2)), pl.BlockSpec((tm, W), lambda i: (rev(i), 3)),
                  pl.BlockSpec((6, tm, W), lambda i: (0, rev(i), 0)),
                  pl.BlockSpec((tm, W), lambda i: (rev(i), 0)), pl.BlockSpec((8, W), lambda i: (prev(i), 0)),
                  pl.BlockSpec((tm, W), lambda i: (rev(i), 1)),
                  pl.BlockSpec((4, W), lambda i: (0, 0)), mat, mat, vec, vec,
                  pl.BlockSpec(memory_space=pl.ANY)],
        out_specs=[pl.BlockSpec((tm, 2 * W), lambda i: (rev(i), 1)), pl.BlockSpec((4, W), lambda i: (0, 0)), vec,
                   mat, vec, mat, vec, vec, vec],
        out_shape=[jax.ShapeDtypeStruct((S, 4 * W), BF16), jax.ShapeDtypeStruct((4, W), F32), vshape,
                   mshape, vshape, mshape, vshape, vshape, vshape],
        input_output_aliases={11: 0},
        scratch_shapes=[car(), car(), car(), tile(), tile(), tile(), tile(), pltpu.VMEM((tm, W), BF16),
                        pltpu.VMEM((tm, W), BF16), pltpu.VMEM((4, 16, W), F32)],
        compiler_params=_cp(1, 56))(p, p, saved, hs, hs, d_y, cw, wa, wx, lam, out_g, dp_half)


def _rows128(a):
    return a.reshape(-1, LANES).astype(F32)


def _pack(arrays, pad_to=256):
    flat = jnp.concatenate([_rows128(a) for a in arrays], axis=0)
    pad = (-flat.shape[0]) % pad_to
    if pad:
        flat = jnp.concatenate([flat, jnp.zeros((pad, LANES), F32)], axis=0)
    return flat


def _unpack(flat, shapes):
    out, r = [], 0
    for s in shapes:
        n = 1
        for d in s:
            n *= d
        out.append(flat[r:r + n // LANES].reshape(s))
        r += n // LANES
    return out


def kernel(x, norm1_g, w_in, gm_v_g, gm_v_b, gm_ws, gm_bs, lru_conv_w, lru_conv_b, lru_wa, lru_ba, lru_wx, lru_bx, lru_lambda, gm_out_g, lru_out_g, w_out, norm2_g, ffn_w_up, ffn_conv_w, ffn_conv_b, ffn_w_down, final_g, loss_target, m_norm1_g, m_w_in, m_gm_v_g, m_gm_v_b, m_gm_ws, m_gm_bs, m_lru_conv_w, m_lru_conv_b, m_lru_wa, m_lru_ba, m_lru_wx, m_lru_bx, m_lru_lambda, m_gm_out_g, m_lru_out_g, m_w_out, m_norm2_g, m_ffn_w_up, m_ffn_conv_w, m_ffn_conv_b, m_ffn_w_down, m_final_g, v_norm1_g, v_w_in, v_gm_v_g, v_gm_v_b, v_gm_ws, v_gm_bs, v_lru_conv_w, v_lru_conv_b, v_lru_wa, v_lru_ba, v_lru_wx, v_lru_bx, v_lru_lambda, v_gm_out_g, v_lru_out_g, v_w_out, v_norm2_g, v_ffn_w_up, v_ffn_conv_w, v_ffn_conv_b, v_ffn_w_down, v_final_g):
    wts = dict(norm1_g=norm1_g, w_in=w_in, gm_v_g=gm_v_g, gm_v_b=gm_v_b, gm_ws=gm_ws, gm_bs=gm_bs,
               lru_conv_w=lru_conv_w, lru_conv_b=lru_conv_b, lru_wa=lru_wa, lru_ba=lru_ba, lru_wx=lru_wx,
               lru_bx=lru_bx, lru_lambda=lru_lambda, gm_out_g=gm_out_g, lru_out_g=lru_out_g, w_out=w_out,
               norm2_g=norm2_g, ffn_w_up=ffn_w_up, ffn_conv_w=ffn_conv_w, ffn_conv_b=ffn_conv_b,
               ffn_w_down=ffn_w_down, final_g=final_g)
    mom = dict(norm1_g=m_norm1_g, w_in=m_w_in, gm_v_g=m_gm_v_g, gm_v_b=m_gm_v_b, gm_ws=m_gm_ws, gm_bs=m_gm_bs,
               lru_conv_w=m_lru_conv_w, lru_conv_b=m_lru_conv_b, lru_wa=m_lru_wa, lru_ba=m_lru_ba, lru_wx=m_lru_wx,
               lru_bx=m_lru_bx, lru_lambda=m_lru_lambda, gm_out_g=m_gm_out_g, lru_out_g=m_lru_out_g, w_out=m_w_out,
               norm2_g=m_norm2_g, ffn_w_up=m_ffn_w_up, ffn_conv_w=m_ffn_conv_w, ffn_conv_b=m_ffn_conv_b,
               ffn_w_down=m_ffn_w_down, final_g=m_final_g)
    var = dict(norm1_g=v_norm1_g, w_in=v_w_in, gm_v_g=v_gm_v_g, gm_v_b=v_gm_v_b, gm_ws=v_gm_ws, gm_bs=v_gm_bs,
               lru_conv_w=v_lru_conv_w, lru_conv_b=v_lru_conv_b, lru_wa=v_lru_wa, lru_ba=v_lru_ba, lru_wx=v_lru_wx,
               lru_bx=v_lru_bx, lru_lambda=v_lru_lambda, gm_out_g=v_gm_out_g, lru_out_g=v_lru_out_g, w_out=v_w_out,
               norm2_g=v_norm2_g, ffn_w_up=v_ffn_w_up, ffn_conv_w=v_ffn_conv_w, ffn_conv_b=v_ffn_conv_b,
               ffn_w_down=v_ffn_w_down, final_g=v_final_g)

    xi, yi, ci = lax.axis_index("x"), lax.axis_index("y"), lax.axis_index("c")
    chip = 2 * xi + yi
    dev = 2 * chip + ci
    core_chip = jnp.stack([ci, chip]).astype(jnp.int32)

    xs = x[0]
    tgt = loss_target[0]
    S, D = xs.shape
    H = gm_ws.shape[1]
    W = H * CHUNK
    Fd = ffn_w_down.shape[1] * N_DEV
    lcw_cols = lru_conv_w.shape[2]
    fcw_cols = ffn_conv_w.shape[2]

    dev1 = jnp.reshape(dev, (1,)).astype(jnp.int32)

    def place_own(shards, name):
        return [_place_own_call(s, dev1, dt, "%s_own%d" % (name, k)) for k, (s, dt) in enumerate(shards)]

    def gather_start(shards, name, after=()):
        lands = place_own(shards, name)
        return _exchange_start([], lands, 4 * len(lands), _gather_stage1_copies(len(lands)), name + "_ici", after)

    def gather_forward(lands, name, after=()):
        return _exchange_start([], lands, 3 * len(lands), _gather_stage2_copies(len(lands)), name + "_d2d", after)

    def pair_start(g, name, after=()):
        return _exchange_start([g], [lax.empty((4,) + g.shape[1:], F32)], 4, _pair_copies(1), name, after)

    def chip_start(p16, name, after=()):
        return _exchange_start([p16], [lax.empty((3,) + p16.shape[1:], BF16)], 3, _chip_copies(1), name, after)

    vgm_g, vgm_b = gm_v_g, gm_v_b
    ws, wa, wx = gm_ws[0], lru_wa[0], lru_wx[0]
    bsb = jnp.broadcast_to(gm_bs[0][:, :, None], (H, CHUNK, CHUNK))
    ba, bx = lru_ba.reshape(1, W), lru_bx.reshape(1, W)
    fcb = ffn_conv_b
    fing = final_g.reshape(1, D)

    conv_pack = _pack([lru_conv_w[0], ffn_conv_w[0]], pad_to=8)
    lands = place_own([(w_in[0], BF16), (conv_pack, F32)], "gather_in")
    ga_pair = _exchange_start([], lands, 2, _gather_stage1_copies(2, to_chips=False), "gather_in_pair")
    ga1 = _exchange_start([], ga_pair.bufs, 6, _gather_stage1_copies(2, to_sibling=False), "gather_in_ici")
    h1 = _rmsnorm_call(xs, norm1_g, "norm1", deps=(ga1.token,))
    ga_pair.bufs = ga1.bufs
    _, la = _exchange_wait(ga_pair, after=(h1,))
    own_blocks = jnp.stack([dev, dev + 1 - 2 * ci]).astype(jnp.int32)
    other_blocks = ((2 * chip + 2 + jnp.arange(N_DEV - 2)) % N_DEV).astype(jnp.int32)
    p_own = _mm_some_blocks_call(h1, la[0], own_blocks, F32, "in_proj_own")
    ga1.bufs = la
    _, la = _exchange_wait(ga1, after=(p_own,))
    ga2 = gather_forward(la, "gather_in")
    gb1 = gather_start([(w_out[0], BF16)], "gather_out", after=(ga2.token,))
    gc1 = gather_start([(ffn_w_up[0], BF16)], "gather_up", after=(gb1.token,))
    gd1 = gather_start([(ffn_w_down[0], BF16)], "gather_down", after=(gc1.token,))
    _, (win_g, conv_g) = _exchange_wait(ga2, after=(gd1.token,))
    n_l = 4 * lcw_cols // LANES
    n_f = 3 * fcw_cols // LANES
    lcw = conv_g[:, :n_l].reshape(N_DEV, 4, lcw_cols).transpose(1, 0, 2).reshape(4, N_DEV * lcw_cols)
    fcw = conv_g[:, n_l:n_l + n_f].reshape(N_DEV, 3, fcw_cols).transpose(1, 0, 2).reshape(3, N_DEV * fcw_cols)

    p = _mm_some_blocks_call(h1, win_g, other_blocks, F32, "in_proj_rest", out_so_far=p_own)
    win_rows = _unblock_call(win_g, "w_in_rows")
    _, lb = _exchange_wait(gb1, after=(p,))
    gb2 = gather_forward(lb, "gather_out")
    y_half = _gm_fwd_call(p, vgm_g, vgm_b, ws, bsb, gm_out_g, "gmlp_fwd", deps=(gb2.token,))
    y, hs, lru_saved = _lru_fwd_call(p, lcw, lru_conv_b, wa, ba, wx, bx, lru_lambda, lru_out_g, y_half, "lru_fwd")
    _, lc = _exchange_wait(gc1, after=(y,))
    gc2 = gather_forward(lc, "gather_up")
    _, (wout_g,) = _exchange_wait(gb2, after=(y, gc2.token))
    wout_full = wout_g.reshape(D, D)
    x2 = _mm_out_call(xs, y, wout_full, "out_proj")
    h2 = _rmsnorm_call(x2, norm2_g, "norm2")
    _, (wup_g,) = _exchange_wait(gc2, after=(h2,))
    _, ld = _exchange_wait(gd1, after=(h2,))
    gd2 = gather_forward(ld, "gather_down")
    up3, upc3, f = _ffn_up_act_call(h2, wup_g, fcw, fcb, "ffn_up", deps=(gd2.token,))
    _, (wdown_g,) = _exchange_wait(gd2, after=(f,))
    wdown_full = wdown_g.reshape(Fd, D)
    dx3, dx3b, loss_acc, d_final = _mm_down_loss_call(x2, f, wdown_full, fing, tgt, "ffn_down_loss")

    g_wdown = _mm_tn_rows_call(f, dx3b, "ffn_down_dw").reshape((N_DEV,) + ffn_w_down.shape[1:])
    pd = pair_start(g_wdown, "pair_down")
    d_up3, dfcw_g, dfcw_v, dfcb_g, dfcb_v = _ffn_down_dx_act_bwd_call(dx3b, wdown_full, up3, upc3, fcw, "ffn_down_dx",
                                                                     deps=(pd.token,))
    (g_wdown,), (r1,) = _exchange_wait(pd, after=(d_up3,))
    own_down, p16 = _pair_add_call(g_wdown, r1, core_chip, "pair_add_down")
    cd = chip_start(p16, "chip_down")
    g_wup = _mm_tn_cols_call(h2, d_up3, N_DEV, ffn_w_up.shape[2], "ffn_up_dw", deps=(cd.token,))
    pu = pair_start(g_wup, "pair_up")
    dx2, dx2b, d_norm2 = _mm_dx_norm_call(d_up3, wup_g, dx3, x2, norm2_g, "ffn_up_dx", deps=(pu.token,))
    g_wout = _mm_tn_rows_call(y, dx2b, "out_proj_dw").reshape((N_DEV,) + w_out.shape[1:])
    po = pair_start(g_wout, "pair_out")
    d_y = _mm_nt_call(dx2b, wout_full, F32, "out_proj_dx", deps=(po.token,))
    (g_wup,), (r1,) = _exchange_wait(pu, after=(d_y,))
    own_up, p16 = _pair_add_call(g_wup, r1, core_chip, "pair_add_up")
    _, (r2_down,) = _exchange_wait(cd, after=(p16,))
    cu = chip_start(p16, "chip_up", after=(r2_down,))
    dp_half, d_vg, d_vb, d_ws, d_bs, d_gog = _gm_bwd_call(p, d_y, vgm_g, vgm_b, ws, bsb, gm_out_g, "gmlp_bwd",
                                                          deps=(cu.token,))
    d_p2, d_lcw, d_lcb, d_wa, d_ba, d_wx, d_bx, d_lam, d_log = _lru_bwd_call(
        p, hs, lru_saved, d_y, lcw, wa, wx, lru_lambda, lru_out_g, dp_half, "lru_bwd")
    d_p = d_p2[None]
    (g_wout,), (r1,) = _exchange_wait(po, after=(d_p,))
    own_out, p16_out = _pair_add_call(g_wout, r1, core_chip, "pair_add_out")
    g_win = _mm_tn_cols_call(h1, d_p, N_DEV, w_in.shape[2], "in_proj_dw")
    pi = pair_start(g_win, "pair_in")
    _, (r2_up,) = _exchange_wait(cu, after=(g_win,))
    co = chip_start(p16_out, "chip_out", after=(r2_up,))
    gx_a, dn_a = _mm_nt_norm_call(d_p[0], win_rows, dx2, xs, norm1_g, "in_proj_dx_a", deps=(co.token, pi.token),
                                  part=(0, 2))
    (g_win,), (r1,) = _exchange_wait(pi, after=(gx_a,))
    own_in, p16 = _pair_add_call(g_win, r1, core_chip, "pair_add_in")
    _, (r2_out,) = _exchange_wait(co, after=(p16,))
    ci_ = chip_start(p16, "chip_in", after=(r2_out,))
    grad_x, dn_b = _mm_nt_norm_call(d_p[0], win_rows, dx2, xs, norm1_g, "in_proj_dx_b", deps=(ci_.token,),
                                    part=(1, 2), dx_so_far=gx_a)

    small_g = dict(norm1_g=dn_a + dn_b, gm_v_g=d_vg, gm_v_b=d_vb, gm_ws=d_ws, gm_bs=d_bs[:, :, 0], lru_conv_b=d_lcb,
                   lru_wa=d_wa, lru_ba=d_ba, lru_wx=d_wx, lru_bx=d_bx, lru_lambda=d_lam, gm_out_g=d_gog,
                   lru_out_g=d_log, norm2_g=d_norm2,
                   ffn_conv_b=jnp.concatenate([dfcb_g, dfcb_v], axis=1), final_g=d_final)
    rep = _pack([small_g[n] for n in SMALL])
    conv_part = _pack([d_lcw, jnp.concatenate([dfcw_g, dfcw_v], axis=1)], pad_to=8)
    n_rep, n_conv = rep.shape[0], conv_part.shape[0]
    gs1 = gather_start([(jnp.concatenate([rep, conv_part], axis=0), F32)], "gather_small")

    def adamw_big(n, own, r2, deps=()):
        return _adamw_call(wts[n][0], mom[n][0], var[n][0], [(own, None), (r2, 0), (r2, 1), (r2, 2)], "adamw_" + n, deps)

    res = {}
    res["ffn_w_down"] = adamw_big("ffn_w_down", own_down, r2_down, (gs1.token,))
    res["ffn_w_up"] = adamw_big("ffn_w_up", own_up, r2_up, (gs1.token,))
    res["w_out"] = adamw_big("w_out", own_out, r2_out, (gs1.token,))
    _, ls = _exchange_wait(gs1, after=(res["w_out"][0], res["ffn_w_up"][0], res["ffn_w_down"][0]))
    gs2 = gather_forward(ls, "gather_small")
    _, (r2_in,) = _exchange_wait(ci_, after=(gs2.token,))
    res["w_in"] = adamw_big("w_in", own_in, r2_in)
    _, (parts,) = _exchange_wait(gs2, after=(res["w_in"][0],))
    g_rep, d_rep, m_rep, v_rep = _adamw_call(
        _pack([wts[n] for n in SMALL]), _pack([mom[n] for n in SMALL]), _pack([var[n] for n in SMALL]),
        [(parts, k) for k in range(N_DEV)], "adamw_small")
    shapes = [wts[n].shape for n in SMALL]
    for n, g_, d_, m_, v_ in zip(SMALL, _unpack(g_rep, shapes), _unpack(d_rep, shapes), _unpack(m_rep, shapes),
                                 _unpack(v_rep, shapes)):
        res[n] = (g_, d_, m_, v_)
    conv_sum = _sum_call(parts, n_rep, n_conv, "sum_conv_grads")
    g_lcw = conv_sum[:4 * W // LANES].reshape(4, W)
    g_fcw = conv_sum[4 * W // LANES:4 * W // LANES + 6 * Fd // LANES].reshape(3, 2 * Fd)
    for n, full in (("lru_conv_w", g_lcw), ("ffn_conv_w", g_fcw)):
        cols = wts[n].shape[2]
        mine = lax.dynamic_slice_in_dim(full, dev * cols, cols, axis=1)
        res[n] = _adamw_call(wts[n][0], mom[n][0], var[n][0], [(mine, None)], "adamw_" + n)

    loss = lax.psum(loss_acc[0, 0], ("x", "y", "c"))
    outs = [[], [], [], []]
    for n in WEIGHTS:
        for k in range(4):
            outs[k].append(res[n][k].reshape(wts[n].shape))
    return (loss, grad_x[None], *outs[0], *outs[1], *outs[2], *outs[3])
```

```python
import functools
import math

import jax
import jax.numpy as jnp
from jax import lax
from jax.experimental import pallas as pl
from jax.experimental.pallas import tpu as pltpu

F32 = jnp.float32
BF16 = jnp.bfloat16

RMS_EPS = 1e-6
LN_EPS = 1e-5
LRU_C = 8.0
CHUNK = 128
ADAM_LR = 0.001
ADAM_B1 = 0.9
ADAM_B2 = 0.999
ADAM_EPS = 1e-08
ADAM_WD = 0.01
ADAM_STEP = 10
N_DEV = 8
LANES = 128
MIB = 1024 * 1024

WEIGHTS = ['norm1_g', 'w_in', 'gm_v_g', 'gm_v_b', 'gm_ws', 'gm_bs', 'lru_conv_w', 'lru_conv_b', 'lru_wa', 'lru_ba',
           'lru_wx', 'lru_bx', 'lru_lambda', 'gm_out_g', 'lru_out_g', 'w_out', 'norm2_g', 'ffn_w_up', 'ffn_conv_w',
           'ffn_conv_b', 'ffn_w_down', 'final_g']
BIG = ['w_in', 'w_out', 'ffn_w_up', 'ffn_w_down']
CONV = ['lru_conv_w', 'ffn_conv_w']
SMALL = [n for n in WEIGHTS if n not in BIG and n not in CONV]

_DN_NT = (((1,), (1,)), ((), ()))
_DN_TN = (((0,), (0,)), ((), ()))
_GELU_C = 0.7978845608028654


def _cp(n_axes, vmem_mib=48):
    return pltpu.CompilerParams(dimension_semantics=("arbitrary",) * n_axes, vmem_limit_bytes=vmem_mib * MIB)


def _tile(n, pref, mult=8):
    t = min(pref, n)
    t -= t % mult
    while t >= mult:
        if n % t == 0:
            return t
        t -= mult
    return n


def _gelu_gate(z, z2):
    return 0.5 * jnp.tanh(z * ((_GELU_C * 0.044715) * z2 + _GELU_C)) + 0.5


def _gelu(z):
    return z * _gelu_gate(z, z * z)


def _gelu_parts(z):
    z2 = z * z
    s = _gelu_gate(z, z2)
    g = z * s
    dg = s + g * (1.0 - s) * ((6.0 * _GELU_C * 0.044715) * z2 + 2.0 * _GELU_C)
    return g, dg


def _sigmoid(z):
    return 0.5 + 0.5 * jnp.tanh(0.5 * z)


def _softplus(z):
    t = jnp.exp(-jnp.abs(z))
    u = 1.0 + t
    log1p = jnp.where(u == 1.0, t, jnp.log(u) * t / (u - 1.0))
    return jnp.maximum(z, 0.0) + log1p


def _rows_mean(v):
    return jnp.mean(v, axis=-1, keepdims=True)


def _col_sum(v):
    return jnp.sum(v, axis=0, keepdims=True)


def _shift_down(prev8, cur, k):
    if k == 0:
        return cur
    z = jnp.concatenate([prev8, cur], axis=0)
    return pltpu.roll(z, k, 0)[8:]


def _shift_up(cur, next8, k):
    if k == 0:
        return cur
    n = cur.shape[0]
    z = jnp.concatenate([cur, next8], axis=0)
    return pltpu.roll(z, n + 8 - k, 0)[:n]


def _mesh_pos():
    return lax.axis_index("x"), lax.axis_index("y"), lax.axis_index("c")


def _any_specs(n):
    return [pl.BlockSpec(memory_space=pl.ANY)] * n


def _pallas(body, n_in, deps, **kw):
    nd = len(deps)
    if not nd:
        return pl.pallas_call(body, **kw)

    def ordered(*refs):
        body(*refs[:n_in], *refs[n_in + nd:])

    kw["in_specs"] = list(kw["in_specs"]) + _any_specs(nd)
    return pl.pallas_call(ordered, **kw)


_HBM = pl.BlockSpec(memory_space=pltpu.HBM)
_SEM = pl.BlockSpec(memory_space=pltpu.SEMAPHORE)
_EFFECT = pltpu.SideEffectType.DATAFLOW_SIDE_EFFECTING


class _InFlight:
    def __init__(self, sems, bufs, token, n_src, n_copies, make_copies, name):
        self.sems, self.bufs, self.token = sems, bufs, token
        self.n_src, self.n_copies, self.make_copies, self.name = n_src, n_copies, make_copies, name


def _exchange_start(srcs, lands, n_copies, make_copies, name, after=()):
    bufs = list(srcs) + list(lands)
    nb, na = len(bufs), len(after)
    ns = len(srcs)

    def body(*refs):
        b_refs = refs[:nb]
        outs = refs[nb + na:]
        send, recv = outs[:n_copies], outs[n_copies:2 * n_copies]
        token = outs[-1]
        for cp in make_copies(b_refs[:ns], b_refs[ns:], send, recv):
            cp.start()
        token[...] = jnp.zeros_like(token)

    out = pl.pallas_call(
        body, name=name,
        out_shape=[pltpu.SemaphoreType.DMA(())] * (2 * n_copies) + [pltpu.HBM(b.shape, b.dtype) for b in bufs]
        + [jax.ShapeDtypeStruct((8, LANES), F32)],
        in_specs=[_HBM] * nb + _any_specs(na),
        out_specs=[_SEM] * (2 * n_copies) + [_HBM] * nb + [pl.BlockSpec(memory_space=pltpu.VMEM)],
        input_output_aliases={i: 2 * n_copies + i for i in range(nb)},
        compiler_params=pltpu.CompilerParams(has_side_effects=_EFFECT),
    )(*[pltpu.with_memory_space_constraint(b, pltpu.HBM) for b in bufs], *after)
    return _InFlight(out[:2 * n_copies], out[2 * n_copies:2 * n_copies + nb], out[-1], ns, n_copies, make_copies, name)


def _exchange_wait(fl, after=()):
    nb, na, nc, ns = len(fl.bufs), len(after), fl.n_copies, fl.n_src

    def body(*refs):
        b_refs = refs[:nb]
        sems = refs[nb:nb + 2 * nc]
        copies = fl.make_copies(b_refs[:ns], b_refs[ns:], sems[:nc], sems[nc:])
        for cp in copies:
            cp.wait_send()
        for cp in copies:
            cp.wait_recv()

    out = pl.pallas_call(
        body, name=fl.name + "_wait",
        out_shape=[pltpu.HBM(b.shape, b.dtype) for b in fl.bufs],
        in_specs=[_HBM] * nb + [_SEM] * (2 * nc) + _any_specs(na),
        out_specs=[_HBM] * nb,
        input_output_aliases={i: i for i in range(nb)},
        compiler_params=pltpu.CompilerParams(has_side_effects=_EFFECT),
    )(*fl.bufs, *fl.sems, *after)
    return list(out[:ns]), list(out[ns:])


def _remote(src, dst, send_sem, recv_sem, to):
    return pltpu.make_async_remote_copy(src_ref=src, dst_ref=dst, send_sem=send_sem, recv_sem=recv_sem,
                                        device_id=to, device_id_type=pl.DeviceIdType.MESH)


def _gather_stage1_copies(n, to_sibling=True, to_chips=True):
    def make(s_refs, l_refs, send, recv):
        x, y, c = _mesh_pos()
        own = 4 * x + 2 * y + c
        targets = ([(x, y, 1 - c)] if to_sibling else []) + (
            [(1 - x, y, c), (x, 1 - y, c), (1 - x, 1 - y, c)] if to_chips else [])
        m = len(targets)
        return [_remote(l_refs[a].at[own], l_refs[a].at[own], send[m * a + k], recv[m * a + k], to)
                for a in range(n) for k, to in enumerate(targets)]
    return make


def _gather_stage2_copies(n):
    def make(s_refs, l_refs, send, recv):
        x, y, c = _mesh_pos()
        blocks = [4 * (1 - x) + 2 * y + c, 4 * x + 2 * (1 - y) + c, 4 * (1 - x) + 2 * (1 - y) + c]
        return [_remote(l_refs[a].at[b], l_refs[a].at[b], send[3 * a + j], recv[3 * a + j], (x, y, 1 - c))
                for a in range(n) for j, b in enumerate(blocks)]
    return make


def _pair_copies(n):
    def make(s_refs, l_refs, send, recv):
        x, y, c = _mesh_pos()
        return [_remote(s_refs[a].at[2 * k + 1 - c], l_refs[a].at[k], send[4 * a + k], recv[4 * a + k], (x, y, 1 - c))
                for a in range(n) for k in range(4)]
    return make


def _chip_copies(n):
    def make(s_refs, l_refs, send, recv):
        x, y, c = _mesh_pos()
        chips = [(1 - x, y), (x, 1 - y), (1 - x, 1 - y)]
        return [_remote(s_refs[a].at[2 * ch[0] + ch[1]], l_refs[a].at[j], send[3 * a + j], recv[3 * a + j], (*ch, c))
                for a in range(n) for j, ch in enumerate(chips)]
    return make


def _place_own_call(shard, dev, dtype, name):
    R, C = shard.shape
    tr = _tile(R, max(16, MIB // (C * 4)), 16)

    def body(d_ref, s_ref, o_ref):
        o_ref[...] = s_ref[...].astype(dtype)

    grid_spec = pltpu.PrefetchScalarGridSpec(
        num_scalar_prefetch=1, grid=(R // tr,),
        in_specs=[pl.BlockSpec((tr, C), lambda r, d: (r, 0))],
        out_specs=pl.BlockSpec((None, tr, C), lambda r, d: (d[0], r, 0)))
    return pl.pallas_call(body, name=name, grid_spec=grid_spec,
                          out_shape=jax.ShapeDtypeStruct((N_DEV, R, C), dtype), compiler_params=_cp(1))(dev, shard)


def _pair_add_call(g, r1, core_chip, name):
    _, R, C = g.shape
    tr = _tile(R, max(16, (2 * MIB) // (C * 4)), 16)

    def body(cc_ref, g_ref, r_ref, p32_ref, p16_ref):
        s = g_ref[...] + r_ref[...]
        p16_ref[...] = s.astype(BF16)

        @pl.when(pl.program_id(1) == cc_ref[1])
        def _():
            p32_ref[...] = s

    grid_spec = pltpu.PrefetchScalarGridSpec(
        num_scalar_prefetch=1, grid=(R // tr, 4),
        in_specs=[pl.BlockSpec((None, tr, C), lambda r, k, cc: (2 * k + cc[0], r, 0)),
                  pl.BlockSpec((None, tr, C), lambda r, k, cc: (k, r, 0))],
        out_specs=[pl.BlockSpec((tr, C), lambda r, k, cc: (r, 0)),
                   pl.BlockSpec((None, tr, C), lambda r, k, cc: (k, r, 0))])
    return pl.pallas_call(
        body, name=name, grid_spec=grid_spec,
        out_shape=[jax.ShapeDtypeStruct((R, C), F32), jax.ShapeDtypeStruct((4, R, C), BF16)],
        compiler_params=_cp(2))(core_chip, g, r1)


def _adamw_call(w, m, v, addends, name, deps=()):
    R, C = w.shape
    tr = _tile(R, max(8, (MIB // 2) // (C * 4)), 16)
    na = len(addends)
    c1 = 1.0 - ADAM_B1 ** ADAM_STEP
    c2 = 1.0 - ADAM_B2 ** ADAM_STEP

    def body(*refs):
        w_ref, m_ref, v_ref = refs[:3]
        a_refs = refs[3:3 + na]
        g_ref, d_ref, nm_ref, nv_ref = refs[3 + na:]
        g = a_refs[0][...].astype(F32)
        for a_ref in a_refs[1:]:
            g = g + a_ref[...].astype(F32)
        nm = ADAM_B1 * m_ref[...] + (1.0 - ADAM_B1) * g
        nv = ADAM_B2 * v_ref[...] + (1.0 - ADAM_B2) * (g * g)
        g_ref[...] = g
        nm_ref[...] = nm
        nv_ref[...] = nv
        d_ref[...] = -ADAM_LR * ((nm / c1) / (jnp.sqrt(nv / c2) + ADAM_EPS) + ADAM_WD * w_ref[...])

    flat = pl.BlockSpec((tr, C), lambda r: (r, 0))
    a_specs = [flat if k is None else pl.BlockSpec((None, tr, C), functools.partial(lambda r, kk: (kk, r, 0), kk=k))
               for _, k in addends]
    out = jax.ShapeDtypeStruct((R, C), F32)
    return _pallas(
        body, 3 + na, deps, name=name, grid=(R // tr,),
        in_specs=[flat, flat, flat] + a_specs, out_specs=[flat] * 4, out_shape=[out] * 4,
        compiler_params=_cp(1))(w, m, v, *[a for a, _ in addends], *deps)


def _sum_call(parts, row0, rows, name):
    n = parts.shape[0]
    tr = _tile(math.gcd(row0, rows), 256, 8)
    b0 = row0 // tr

    def body(p_ref, o_ref):
        s = p_ref[0]
        for k in range(1, n):
            s = s + p_ref[k]
        o_ref[...] = s

    return pl.pallas_call(
        body, name=name, grid=(rows // tr,),
        in_specs=[pl.BlockSpec((n, tr, LANES), lambda r: (0, r + b0, 0))],
        out_specs=pl.BlockSpec((tr, LANES), lambda r: (r, 0)),
        out_shape=jax.ShapeDtypeStruct((rows, LANES), F32), compiler_params=_cp(1))(parts)


def _rmsnorm_call(x, g, name, deps=()):
    S, D = x.shape
    tm = _tile(S, 512, 16)

    def body(x_ref, g_ref, o_ref):
        xv = x_ref[...]
        r = lax.rsqrt(_rows_mean(xv * xv) + RMS_EPS)
        o_ref[...] = (xv * r * g_ref[...]).astype(BF16)

    return _pallas(
        body, 2, deps, name=name, grid=(S // tm,),
        in_specs=[pl.BlockSpec((tm, D), lambda i: (i, 0)), pl.BlockSpec((1, D), lambda i: (0, 0))],
        out_specs=pl.BlockSpec((tm, D), lambda i: (i, 0)),
        out_shape=jax.ShapeDtypeStruct((S, D), BF16), compiler_params=_cp(1))(x, g, *deps)


def _mm_some_blocks_call(a, wg, blocks, out_dtype, name, out_so_far=None):
    S, K = a.shape
    nb, _, bn = wg.shape
    tm = _tile(S, 1024, 16)

    def body(b_ref, a_ref, w_ref, *rest):
        rest[-1][...] = jnp.dot(a_ref[...], w_ref[...], preferred_element_type=F32).astype(out_dtype)

    in_specs = [pl.BlockSpec((tm, K), lambda i, j, b: (i, 0)), pl.BlockSpec((None, K, bn), lambda i, j, b: (b[j], 0, 0))]
    operands = [a, wg]
    aliases = {}
    if out_so_far is not None:
        in_specs.append(pl.BlockSpec(memory_space=pl.ANY))
        operands.append(out_so_far)
        aliases = {3: 0}
    grid_spec = pltpu.PrefetchScalarGridSpec(
        num_scalar_prefetch=1, grid=(S // tm, blocks.shape[0]), in_specs=in_specs,
        out_specs=pl.BlockSpec((tm, bn), lambda i, j, b: (i, b[j])))
    return pl.pallas_call(body, name=name, grid_spec=grid_spec,
                          out_shape=jax.ShapeDtypeStruct((S, nb * bn), out_dtype), input_output_aliases=aliases,
                          compiler_params=_cp(2))(blocks, *operands)


def _mm_out_call(x, y, w, name):
    S, D = x.shape
    tm = _tile(S, 512, 16)

    def body(x_ref, y_ref, w_ref, o_ref):
        o_ref[...] = x_ref[...] + jnp.dot(y_ref[...], w_ref[...], preferred_element_type=F32)

    return pl.pallas_call(
        body, name=name, grid=(S // tm,),
        in_specs=[pl.BlockSpec((tm, D), lambda i: (i, 0)), pl.BlockSpec((tm, D), lambda i: (i, 0)),
                  pl.BlockSpec((D, D), lambda i: (0, 0))],
        out_specs=pl.BlockSpec((tm, D), lambda i: (i, 0)),
        out_shape=jax.ShapeDtypeStruct((S, D), F32), compiler_params=_cp(1))(x, y, w)


def _mm_nt_call(a, w, out_dtype, name, deps=()):
    S, K = a.shape
    N = w.shape[0]
    tm = _tile(S, 1024, 16)
    tn = _tile(N, 768, LANES)

    def body(a_ref, w_ref, o_ref):
        o_ref[...] = lax.dot_general(a_ref[...], w_ref[...], _DN_NT, preferred_element_type=F32).astype(out_dtype)

    return _pallas(
        body, 2, deps, name=name, grid=(S // tm, N // tn),
        in_specs=[pl.BlockSpec((tm, K), lambda i, j: (i, 0)), pl.BlockSpec((tn, K), lambda i, j: (j, 0))],
        out_specs=pl.BlockSpec((tm, tn), lambda i, j: (i, j)),
        out_shape=jax.ShapeDtypeStruct((S, N), out_dtype), compiler_params=_cp(2))(a, w, *deps)


def _mm_down_loss_call(x2, f, w, final_g, target, name):
    S, D = x2.shape
    Fd = f.shape[1]
    tm = _tile(S, 512, 16)
    tk = _tile(Fd, 768, LANES)
    nk = Fd // tk

    def body(x_ref, f_ref, w_ref, g_ref, t_ref, dx_ref, dxb_ref, loss_ref, dg_ref, acc):
        i, k = pl.program_id(0), pl.program_id(1)

        @pl.when(jnp.logical_and(i == 0, k == 0))
        def _():
            loss_ref[...] = jnp.zeros_like(loss_ref)
            dg_ref[...] = jnp.zeros_like(dg_ref)

        @pl.when(k == 0)
        def _():
            acc[...] = jnp.zeros_like(acc)

        acc[...] += jnp.dot(f_ref[...], w_ref[...], preferred_element_type=F32)

        @pl.when(k == nk - 1)
        def _():
            x3 = x_ref[...] + acc[...]
            r = lax.rsqrt(_rows_mean(x3 * x3) + RMS_EPS)
            g = g_ref[...]
            xn = x3 * r
            diff = xn * g - t_ref[...]
            loss_ref[...] += 0.5 * jnp.sum(_rows_mean(diff * diff))
            dout = diff * (1.0 / D)
            dg_ref[...] += _col_sum(dout * xn)
            dyg = dout * g
            dx = r * (dyg - xn * _rows_mean(dyg * xn))
            dx_ref[...] = dx
            dxb_ref[...] = dx.astype(BF16)

    row = lambda i, k: (i, 0)
    return pl.pallas_call(
        body, name=name, grid=(S // tm, nk),
        in_specs=[pl.BlockSpec((tm, D), row), pl.BlockSpec((tm, tk), lambda i, k: (i, k)),
                  pl.BlockSpec((tk, D), lambda i, k: (k, 0)), pl.BlockSpec((1, D), lambda i, k: (0, 0)),
                  pl.BlockSpec((tm, D), row)],
        out_specs=[pl.BlockSpec((tm, D), row), pl.BlockSpec((tm, D), row),
                   pl.BlockSpec((8, LANES), lambda i, k: (0, 0)), pl.BlockSpec((1, D), lambda i, k: (0, 0))],
        out_shape=[jax.ShapeDtypeStruct((S, D), F32), jax.ShapeDtypeStruct((S, D), BF16),
                   jax.ShapeDtypeStruct((8, LANES), F32), jax.ShapeDtypeStruct((1, D), F32)],
        scratch_shapes=[pltpu.VMEM((tm, D), F32)], compiler_params=_cp(2, 56))(x2, f, w, final_g, target)


def _mm_dx_norm_call(a3, wg, resid, xin, g, name, deps=()):
    na, S, Fa = a3.shape
    nb, D, bn = wg.shape
    tm = _tile(S, 512, 16)
    tk = _tile(bn, 1536, LANES)
    nsub = bn // tk
    nka = Fa // tk
    nk = nb * nsub
    assert na * nka == nk

    def body(a_ref, w_ref, r_ref, x_ref, g_ref, dx_ref, dxb_ref, dg_ref, acc):
        i, k = pl.program_id(0), pl.program_id(1)

        @pl.when(jnp.logical_and(i == 0, k == 0))
        def _():
            dg_ref[...] = jnp.zeros_like(dg_ref)

        @pl.when(k == 0)
        def _():
            acc[...] = jnp.zeros_like(acc)

        acc[...] += lax.dot_general(a_ref[...], w_ref[...], _DN_NT, preferred_element_type=F32)

        @pl.when(k == nk - 1)
        def _():
            dh = acc[...]
            xv = x_ref[...]
            r = lax.rsqrt(_rows_mean(xv * xv) + RMS_EPS)
            xn = xv * r
            dg_ref[...] += _col_sum(dh * xn)
            dyg = dh * g_ref[...]
            dx = r_ref[...] + r * (dyg - xn * _rows_mean(dyg * xn))
            dx_ref[...] = dx
            dxb_ref[...] = dx.astype(BF16)

    row = lambda i, k: (i, 0)
    return _pallas(
        body, 5, deps, name=name, grid=(S // tm, nk),
        in_specs=[pl.BlockSpec((None, tm, tk), lambda i, k: (k // nka, i, k % nka)),
                  pl.BlockSpec((None, D, tk), lambda i, k: (k // nsub, 0, k % nsub)),
                  pl.BlockSpec((tm, D), row, pipeline_mode=pl.Buffered(1)),
                  pl.BlockSpec((tm, D), row, pipeline_mode=pl.Buffered(1)), pl.BlockSpec((1, D), lambda i, k: (0, 0))],
        out_specs=[pl.BlockSpec((tm, D), row), pl.BlockSpec((tm, D), row), pl.BlockSpec((1, D), lambda i, k: (0, 0))],
        out_shape=[jax.ShapeDtypeStruct((S, D), F32), jax.ShapeDtypeStruct((S, D), BF16),
                   jax.ShapeDtypeStruct((1, D), F32)],
        scratch_shapes=[pltpu.VMEM((tm, D), F32)], compiler_params=_cp(2, 56))(a3, wg, resid, xin, g, *deps)


def _unblock_call(wg, name):
    nb, K, bn = wg.shape

    def body(w_ref, o_ref):
        o_ref[...] = w_ref[...]

    return pl.pallas_call(
        body, name=name, grid=(nb,),
        in_specs=[pl.BlockSpec((None, K, bn), lambda o: (o, 0, 0))],
        out_specs=pl.BlockSpec((K, bn), lambda o: (0, o)),
        out_shape=jax.ShapeDtypeStruct((K, nb * bn), wg.dtype), compiler_params=_cp(1))(wg)


def _mm_nt_norm_call(a, w, resid, xin, g, name, deps=(), part=(0, 1), dx_so_far=None):
    S, K = a.shape
    D = w.shape[0]
    tm = _tile(S, 256, 16)
    tiles = (S // tm) // part[1]
    first = part[0] * tiles

    def body(a_ref, w_ref, r_ref, x_ref, g_ref, *rest):
        dx_ref, dg_ref = rest[-2:]

        @pl.when(pl.program_id(0) == 0)
        def _():
            dg_ref[...] = jnp.zeros_like(dg_ref)

        dh = lax.dot_general(a_ref[...], w_ref[...], _DN_NT, preferred_element_type=F32)
        xv = x_ref[...]
        r = lax.rsqrt(_rows_mean(xv * xv) + RMS_EPS)
        xn = xv * r
        dg_ref[...] += _col_sum(dh * xn)
        dyg = dh * g_ref[...]
        dx_ref[...] = r_ref[...] + r * (dyg - xn * _rows_mean(dyg * xn))

    row = lambda i: (i + first, 0)
    fixed = lambda i: (0, 0)
    in_specs = [pl.BlockSpec((tm, K), row), pl.BlockSpec((D, K), fixed, pipeline_mode=pl.Buffered(1)),
                pl.BlockSpec((tm, D), row), pl.BlockSpec((tm, D), row), pl.BlockSpec((1, D), fixed)]
    operands = [a, w, resid, xin, g]
    aliases = {}
    if dx_so_far is not None:
        in_specs.append(pl.BlockSpec(memory_space=pl.ANY))
        operands.append(dx_so_far)
        aliases = {5: 0}
    return _pallas(
        body, len(operands), deps, name=name, grid=(tiles,),
        in_specs=in_specs, out_specs=[pl.BlockSpec((tm, D), row), pl.BlockSpec((1, D), fixed)],
        out_shape=[jax.ShapeDtypeStruct((S, D), F32), jax.ShapeDtypeStruct((1, D), F32)],
        input_output_aliases=aliases, compiler_params=_cp(1, 56))(*operands, *deps)


def _mm_tn_cols_call(a, b3, nb, bn, name, deps=()):
    S, Ka = a.shape
    nh, _, Fb = b3.shape
    tm = _tile(S, 2048, 16)
    tn = _tile(bn, 768, LANES)
    nsub = bn // tn
    njb = Fb // tn
    J = nb * nsub
    assert nh * njb == J

    def body(a_ref, b_ref, o_ref):
        @pl.when(pl.program_id(1) == 0)
        def _():
            o_ref[...] = jnp.zeros_like(o_ref)

        o_ref[...] += lax.dot_general(a_ref[...], b_ref[...], _DN_TN, preferred_element_type=F32)

    return _pallas(
        body, 2, deps, name=name, grid=(J, S // tm),
        in_specs=[pl.BlockSpec((tm, Ka), lambda j, i: (i, 0)),
                  pl.BlockSpec((None, tm, tn), lambda j, i: (j // njb, i, j % njb))],
        out_specs=pl.BlockSpec((None, Ka, tn), lambda j, i: (j // nsub, 0, j % nsub)),
        out_shape=jax.ShapeDtypeStruct((nb, Ka, bn), F32), compiler_params=_cp(2, 56))(a, b3, *deps)


def _mm_tn_rows_call(a, b, name, deps=()):
    S, E = a.shape
    D = b.shape[1]
    tm = _tile(S, 2048, 16)
    te = _tile(E, 768, LANES)

    def body(a_ref, b_ref, o_ref):
        @pl.when(pl.program_id(1) == 0)
        def _():
            o_ref[...] = jnp.zeros_like(o_ref)

        o_ref[...] += lax.dot_general(a_ref[...], b_ref[...], _DN_TN, preferred_element_type=F32)

    return _pallas(
        body, 2, deps, name=name, grid=(E // te, S // tm),
        in_specs=[pl.BlockSpec((tm, te), lambda j, i: (i, j)), pl.BlockSpec((tm, D), lambda j, i: (i, 0))],
        out_specs=pl.BlockSpec((te, D), lambda j, i: (j, 0)),
        out_shape=jax.ShapeDtypeStruct((E, D), F32), compiler_params=_cp(2, 56))(a, b, *deps)


def _ffn_tiles(S, Fd):
    return _tile(S, 512, 16), _tile(Fd // (N_DEV // 2), 1536, LANES)


def _taps(cw_ref):
    return [cw_ref[k:k + 1, :] for k in range(cw_ref.shape[0])]


def _conv3(prev8, cur, taps):
    s1 = _shift_down(prev8, cur, 1)
    s2 = _shift_down(prev8, cur, 2)
    return taps[2] * cur + taps[1] * s1 + taps[0] * s2, s1, s2


GATE_LANES = 256
SUB_LANES = 1536


def _lane_taps(cw_ref, ls):
    return [cw_ref[k:k + 1, ls] for k in range(cw_ref.shape[0])]


def _ffn_up_act_call(h2, wg, cw, cb, name, deps=()):
    S, D = h2.shape
    nb, _, bn = wg.shape
    Fd = nb * bn // 2
    tm, tc = _ffn_tiles(S, Fd)
    nk = Fd // tc
    hb = tm // 16
    nsubw = bn // tc
    half = nb // 2
    sc = _tile(tc, SUB_LANES, LANES)

    def body(a_ref, ap_ref, wgate_ref, wval_ref, cwg_ref, cwv_ref, cbg_ref, cbv_ref, up_ref, upc_ref, f_ref):
        keep = jnp.where(pl.program_id(0) == 0, 0.0, 1.0)
        a_ext = jnp.concatenate([ap_ref[...], a_ref[...]], axis=0)
        nsub = tc // sc
        lanes = [slice(s * sc, (s + 1) * sc) for s in range(nsub)]

        def products(s):
            return [jnp.dot(a_ext, w_ref[:, lanes[s]], preferred_element_type=F32)
                    for w_ref in (wgate_ref, wval_ref)]

        ready = products(0)
        for s in range(nsub):
            ls = lanes[s]
            following = products(s + 1) if s + 1 < nsub else None

            def conv_half(u, cw_ref, cb_ref, slab):
                up_ref[slab, :, ls] = u[16:].astype(BF16)
                conv, _, _ = _conv3(u[8:16] * keep, u[16:], _lane_taps(cw_ref, ls))
                c = conv + cb_ref[:, ls]
                upc_ref[slab, :, ls] = c.astype(BF16)
                return c

            cg = conv_half(ready[0], cwg_ref, cbg_ref, 0)
            cv = conv_half(ready[1], cwv_ref, cbv_ref, 1)
            f_ref[:, ls] = (_gelu(cg) * cv).astype(BF16)
            ready = following

    return _pallas(
        body, 8, deps, name=name, grid=(S // tm, nk),
        in_specs=[pl.BlockSpec((tm, D), lambda i, k: (i, 0)),
                  pl.BlockSpec((16, D), lambda i, k: (jnp.maximum(i * hb - 1, 0), 0)),
                  pl.BlockSpec((None, D, tc), lambda i, k: (k // nsubw, 0, k % nsubw)),
                  pl.BlockSpec((None, D, tc), lambda i, k: (half + k // nsubw, 0, k % nsubw)),
                  pl.BlockSpec((3, tc), lambda i, k: (0, k)), pl.BlockSpec((3, tc), lambda i, k: (0, k + nk)),
                  pl.BlockSpec((1, tc), lambda i, k: (0, k)), pl.BlockSpec((1, tc), lambda i, k: (0, k + nk))],
        out_specs=[pl.BlockSpec((2, tm, tc), lambda i, k: (0, i, k)), pl.BlockSpec((2, tm, tc), lambda i, k: (0, i, k)),
                   pl.BlockSpec((tm, tc), lambda i, k: (i, k))],
        out_shape=[jax.ShapeDtypeStruct((2, S, Fd), BF16), jax.ShapeDtypeStruct((2, S, Fd), BF16),
                   jax.ShapeDtypeStruct((S, Fd), BF16)],
        compiler_params=_cp(2, 56))(h2, h2, wg, wg, cw, cw, cb, cb, *deps)


def _ffn_down_dx_act_bwd_call(dxb, w, up3, upc3, cw, name, deps=()):
    S, D = dxb.shape
    _, _, Fd = up3.shape
    tm, tc = _ffn_tiles(S, Fd)
    nj = Fd // tc
    ni = S // tm
    hb = tm // 16
    sc = _tile(tc, SUB_LANES, LANES)

    def body(a_ref, an_ref, w_ref, g_ref, v_ref, cg_ref, cv_ref, cgn_ref, cvn_ref, cwg_ref, cwv_ref,
             dup_ref, dcwg_ref, dcwv_ref, dcbg_ref, dcbv_ref):
        i = pl.program_id(1)
        keep_next = jnp.where(i == ni - 1, 0.0, 1.0)

        @pl.when(i == 0)
        def _():
            for r in (dcwg_ref, dcwv_ref, dcbg_ref, dcbv_ref):
                r[...] = jnp.zeros_like(r)

        after = (an_ref[...].astype(F32) * keep_next).astype(BF16)
        a_ext = jnp.concatenate([a_ref[...], after], axis=0)
        for s in range(tc // sc):
            ls = slice(s * sc, (s + 1) * sc)
            df_ext = lax.dot_general(a_ext, w_ref[s * sc:(s + 1) * sc, :], _DN_NT, preferred_element_type=F32)
            df = df_ext[:tm + 8]
            cg = jnp.concatenate([cg_ref[:, ls].astype(F32), cgn_ref[:, ls].astype(F32)[:8]], axis=0)
            cv = jnp.concatenate([cv_ref[:, ls].astype(F32), cvn_ref[:, ls].astype(F32)[:8]], axis=0)
            gel, dgel = _gelu_parts(cg)

            def back(d, cw_ref, x_ref, dcw_ref, dcb_ref, slab):
                taps = _lane_taps(cw_ref, ls)
                d0 = d[:tm]
                d1 = pltpu.roll(d, tm + 8 - 1, 0)[:tm]
                d2 = pltpu.roll(d, tm + 8 - 2, 0)[:tm]
                dup_ref[slab, :, ls] = (taps[2] * d0 + taps[1] * d1 + taps[0] * d2).astype(BF16)
                xv = x_ref[:, ls].astype(F32)
                dcw_ref[2:3, ls] += _col_sum(xv * d0)
                dcw_ref[1:2, ls] += _col_sum(xv * d1)
                dcw_ref[0:1, ls] += _col_sum(xv * d2)
                dcb_ref[:, ls] += _col_sum(d0)

            back(df * cv * dgel, cwg_ref, g_ref, dcwg_ref, dcbg_ref, 0)
            back(df * gel, cwv_ref, v_ref, dcwv_ref, dcbv_ref, 1)

    nxt = lambda j, i: jnp.minimum((i + 1) * hb, S // 16 - 1)
    main = lambda s: pl.BlockSpec((None, tm, tc), lambda j, i: (s, i, j))
    halo = lambda s: pl.BlockSpec((None, 16, tc), lambda j, i: (s, nxt(j, i), j))
    acc3 = pl.BlockSpec((3, tc), lambda j, i: (0, j))
    acc1 = pl.BlockSpec((1, tc), lambda j, i: (0, j))
    return _pallas(
        body, 11, deps, name=name, grid=(nj, ni),
        in_specs=[pl.BlockSpec((tm, D), lambda j, i: (i, 0)), pl.BlockSpec((16, D), lambda j, i: (nxt(j, i), 0)),
                  pl.BlockSpec((tc, D), lambda j, i: (j, 0)),
                  main(0), main(1), main(0), main(1), halo(0), halo(1),
                  pl.BlockSpec((3, tc), lambda j, i: (0, j)), pl.BlockSpec((3, tc), lambda j, i: (0, j + nj))],
        out_specs=[pl.BlockSpec((2, tm, tc), lambda j, i: (0, i, j)), acc3, acc3, acc1, acc1],
        out_shape=[jax.ShapeDtypeStruct((2, S, Fd), BF16), jax.ShapeDtypeStruct((3, Fd), F32),
                   jax.ShapeDtypeStruct((3, Fd), F32), jax.ShapeDtypeStruct((1, Fd), F32),
                   jax.ShapeDtypeStruct((1, Fd), F32)],
        compiler_params=_cp(2, 56))(dxb, dxb, w, up3, up3, upc3, upc3, upc3, upc3, cw, cw, *deps)


def _gm_forward_tile(pv, vg, vb, ws_ref, bsb_ref, mbuf, H, nc):
    W = H * CHUNK
    z, dz = _gelu_parts(pv)
    u, v0 = z[:, :W], z[:, W:]
    xc = v0 - _rows_mean(v0)
    rs = lax.rsqrt(_rows_mean(xc * xc) + LN_EPS)
    vh = xc * rs
    vnb = (vh * vg + vb).astype(BF16)
    mask = lax.broadcasted_iota(jnp.int32, (CHUNK, CHUNK), 0) >= lax.broadcasted_iota(jnp.int32, (CHUNK, CHUNK), 1)
    for h in range(H):
        cs = slice(h * CHUNK, (h + 1) * CHUNK)
        wm = jnp.where(mask, ws_ref[h], 0.0).astype(BF16)
        vcat = jnp.concatenate([vnb[c * CHUNK:(c + 1) * CHUNK, cs] for c in range(nc)], axis=1)
        mix = jnp.dot(wm, vcat, preferred_element_type=F32)
        for c in range(nc):
            mbuf[c * CHUNK:(c + 1) * CHUNK, cs] = mix[:, c * CHUNK:(c + 1) * CHUNK] + bsb_ref[h]
    return dz, u, vh, rs, vnb, mask


def _gm_fwd_call(p, v_g, v_b, ws, bsb, out_g, name, deps=()):
    S = p.shape[0]
    H = ws.shape[0]
    W = H * CHUNK
    tm = _tile(S, 256, CHUNK)
    nc = tm // CHUNK

    def body(p_ref, vg_ref, vb_ref, ws_ref, bsb_ref, og_ref, y_ref, mbuf):
        _, u, _, _, _, _ = _gm_forward_tile(p_ref[...], vg_ref[...], vb_ref[...], ws_ref, bsb_ref, mbuf, H, nc)
        yg = u * mbuf[...]
        r = lax.rsqrt(_rows_mean(yg * yg) + RMS_EPS)
        y_ref[...] = (yg * r * og_ref[...]).astype(BF16)

    vec = pl.BlockSpec((1, W), lambda i: (0, 0))
    mat = pl.BlockSpec((H, CHUNK, CHUNK), lambda i: (0, 0, 0))
    return _pallas(
        body, 6, deps, name=name, grid=(S // tm,),
        in_specs=[pl.BlockSpec((tm, 2 * W), lambda i: (i, 0)), vec, vec, mat, mat, vec],
        out_specs=pl.BlockSpec((tm, W), lambda i: (i, 0)),
        out_shape=jax.ShapeDtypeStruct((S, 2 * W), BF16),
        scratch_shapes=[pltpu.VMEM((tm, W), F32)], compiler_params=_cp(1))(p, v_g, v_b, ws, bsb, out_g, *deps)


def _gm_bwd_call(p, d_y, v_g, v_b, ws, bsb, out_g, name, deps=()):
    S = p.shape[0]
    H = ws.shape[0]
    W = H * CHUNK
    tm = _tile(S, 256, CHUNK)
    nc = tm // CHUNK
    ni = S // tm

    def body(p_ref, dy_ref, vg_ref, vb_ref, ws_ref, bsb_ref, og_ref,
             dp_ref, dvg_ref, dvb_ref, dws_ref, dbs_ref, dog_ref, mbuf, dvbuf):
        i = pl.program_id(0)

        @pl.when(i == 0)
        def _():
            for r in (dvg_ref, dvb_ref, dws_ref, dbs_ref, dog_ref):
                r[...] = jnp.zeros_like(r)

        vg = vg_ref[...]
        dz, u, vh, rs, vnb, mask = _gm_forward_tile(p_ref[...], vg, vb_ref[...], ws_ref, bsb_ref, mbuf, H, nc)
        mixed = mbuf[...]
        yg = u * mixed
        r = lax.rsqrt(_rows_mean(yg * yg) + RMS_EPS)
        yn = yg * r
        dya = dy_ref[...]
        dog_ref[...] += _col_sum(dya * yn)
        dyg = dya * og_ref[...]
        dygm = r * (dyg - yn * _rows_mean(dyg * yn))
        du = dygm * mixed
        dmix = dygm * u
        dmb = dmix.astype(BF16)
        for h in range(H):
            cs = slice(h * CHUNK, (h + 1) * CHUNK)
            wm = jnp.where(mask, ws_ref[h], 0.0).astype(BF16)
            dcat = jnp.concatenate([dmb[c * CHUNK:(c + 1) * CHUNK, cs] for c in range(nc)], axis=1)
            vcat = jnp.concatenate([vnb[c * CHUNK:(c + 1) * CHUNK, cs] for c in range(nc)], axis=1)
            dvn = lax.dot_general(wm, dcat, _DN_TN, preferred_element_type=F32)
            dws_ref[h] += jnp.where(mask, lax.dot_general(dcat, vcat, _DN_NT, preferred_element_type=F32), 0.0)
            dbs = dmix[0:CHUNK, cs]
            for c in range(1, nc):
                dbs = dbs + dmix[c * CHUNK:(c + 1) * CHUNK, cs]
            dbs_ref[h] += dbs
            for c in range(nc):
                dvbuf[c * CHUNK:(c + 1) * CHUNK, cs] = dvn[:, c * CHUNK:(c + 1) * CHUNK]
        dvn_all = dvbuf[...]
        dvg_ref[...] += _col_sum(dvn_all * vh)
        dvb_ref[...] += _col_sum(dvn_all)
        dvh = dvn_all * vg
        dv0 = rs * (dvh - _rows_mean(dvh) - vh * _rows_mean(dvh * vh))
        dp_ref[...] = (jnp.concatenate([du, dv0], axis=1) * dz).astype(BF16)

        @pl.when(i == ni - 1)
        def _():
            for h in range(H):
                dbs_ref[h] = jnp.broadcast_to(jnp.sum(dbs_ref[h], axis=1, keepdims=True), (CHUNK, CHUNK))

    vec = pl.BlockSpec((1, W), lambda i: (0, 0))
    mat = pl.BlockSpec((H, CHUNK, CHUNK), lambda i: (0, 0, 0))
    vshape = jax.ShapeDtypeStruct((1, W), F32)
    mshape = jax.ShapeDtypeStruct((H, CHUNK, CHUNK), F32)
    return _pallas(
        body, 7, deps, name=name, grid=(ni,),
        in_specs=[pl.BlockSpec((tm, 2 * W), lambda i: (i, 0)), pl.BlockSpec((tm, W), lambda i: (i, 0)),
                  vec, vec, mat, mat, vec],
        out_specs=[pl.BlockSpec((tm, 2 * W), lambda i: (i, 0)), vec, vec, mat, mat, vec],
        out_shape=[jax.ShapeDtypeStruct((S, 4 * W), BF16), vshape, vshape, mshape, mshape, vshape],
        scratch_shapes=[pltpu.VMEM((tm, W), F32), pltpu.VMEM((tm, W), F32)],
        compiler_params=_cp(1))(p, d_y, v_g, v_b, ws, bsb, out_g, *deps)


def _lru_gates(prev8, xl, cw, cb, wa_ref, ba, wx_ref, bx, lam, H):
    sh = [_shift_down(prev8, xl, k) for k in range(4)]
    xr = cw[3] * sh[0] + cw[2] * sh[1] + cw[1] * sh[2] + cw[0] * sh[3] + cb
    xrb = xr.astype(BF16)
    rp, ip = [], []
    for h in range(H):
        cs = slice(h * CHUNK, (h + 1) * CHUNK)
        rp.append(jnp.dot(xrb[:, cs], wa_ref[h].astype(BF16), preferred_element_type=F32))
        ip.append(jnp.dot(xrb[:, cs], wx_ref[h].astype(BF16), preferred_element_type=F32))
    r = _sigmoid(jnp.concatenate(rp, axis=1) + ba)
    ig = _sigmoid(jnp.concatenate(ip, axis=1) + bx)
    sp = _softplus(-lam)
    t = jnp.tanh((-LRU_C) * r * sp)
    q = lax.rsqrt(1.0 - t)
    a = jnp.sqrt(1.0 + t) * q
    mult = jnp.sqrt(-2.0 * t) * q
    a2_over_mult = (1.0 + t) * q * lax.rsqrt(-2.0 * t)
    return xr, xrb, r, ig, sp, a, mult, a2_over_mult, sh


def _lru_fwd_call(p, cw, cb, wa, ba, wx, bx, lam, out_g, y_half, name):
    S = p.shape[0]
    H = wa.shape[0]
    W = H * CHUNK
    tm = _tile(S, 256, 16)
    ng = tm // 8

    def body(pg_ref, px_ref, cw_ref, cb_ref, wa_ref, ba_ref, wx_ref, bx_ref, lam_ref, og_ref, y_in_ref,
             y_ref, h_ref, saved_ref, xprev, hcar, abuf, bbuf):
        @pl.when(pl.program_id(0) == 0)
        def _():
            xprev[...] = jnp.zeros_like(xprev)
            hcar[...] = jnp.zeros_like(hcar)

        xl = px_ref[...]
        xr, _, r_gate, ig, _, a, mult, a2m, _ = _lru_gates(xprev[...], xl, _taps(cw_ref), cb_ref[...], wa_ref,
                                                           ba_ref[...], wx_ref, bx_ref[...], lam_ref[...], H)
        for k, val in enumerate((xr, r_gate, ig, a, mult, a2m)):
            saved_ref[k] = val
        xprev[...] = xl[tm - 8:]
        b = mult * (ig * xr)
        sub = lax.broadcasted_iota(jnp.int32, (tm, W), 0) & 7
        for d in (1, 2, 4):
            m = sub >= d
            a_s = jnp.where(m, pltpu.roll(a, d, 0), 1.0)
            b_s = jnp.where(m, pltpu.roll(b, d, 0), 0.0)
            b = a * b_s + b
            a = a * a_s
        abuf[...] = a
        bbuf[...] = b

        def step(g, carry):
            r0 = pl.multiple_of(g * 8, 8)
            h_ref[pl.ds(r0, 8), :] = abuf[pl.ds(r0, 8), :] * carry + bbuf[pl.ds(r0, 8), :]
            return jnp.broadcast_to(h_ref[pl.ds(r0 + 7, 1), :], (8, W))

        hcar[...] = lax.fori_loop(0, ng, step, hcar[...])
        yl = h_ref[...] * _gelu(pg_ref[...])
        r = lax.rsqrt(_rows_mean(yl * yl) + RMS_EPS)
        y_ref[...] = (yl * r * og_ref[...]).astype(BF16)

    vec = pl.BlockSpec((1, W), lambda i: (0, 0))
    mat = pl.BlockSpec((H, CHUNK, CHUNK), lambda i: (0, 0, 0))
    return pl.pallas_call(
        body, name=name, grid=(S // tm,),
        in_specs=[pl.BlockSpec((tm, W), lambda i: (i, 2)), pl.BlockSpec((tm, W), lambda i: (i, 3)),
                  pl.BlockSpec((4, W), lambda i: (0, 0)), vec, mat, vec, mat, vec, vec, vec,
                  pl.BlockSpec(memory_space=pl.ANY)],
        out_specs=[pl.BlockSpec((tm, W), lambda i: (i, 1)), pl.BlockSpec((tm, W), lambda i: (i, 0)),
                   pl.BlockSpec((6, tm, W), lambda i: (0, i, 0))],
        out_shape=[jax.ShapeDtypeStruct((S, 2 * W), BF16), jax.ShapeDtypeStruct((S, W), F32),
                   jax.ShapeDtypeStruct((6, S, W), F32)],
        input_output_aliases={10: 0},
        scratch_shapes=[pltpu.VMEM((8, W), F32), pltpu.VMEM((8, W), F32), pltpu.VMEM((tm, W), F32),
                        pltpu.VMEM((tm, W), F32)],
        compiler_params=_cp(1))(p, p, cw, cb, wa, ba, wx, bx, lam, out_g, y_half)


def _lru_bwd_call(p, hs, saved, d_y, cw, wa, wx, lam, out_g, dp_half, name):
    S = p.shape[0]
    H = wa.shape[0]
    W = H * CHUNK
    tm = _tile(S, 256, 16)
    ng = tm // 8
    ni = S // tm
    hb = tm // 8

    def body(pg_ref, px_ref, saved_ref, h_ref, hp_ref, dy_ref, cw_ref, wa_ref, wx_ref, lam_ref, og_ref, dp_in_ref,
             dp_ref, dcw_ref, dcb_ref, dwa_ref, dba_ref, dwx_ref, dbx_ref, dlam_ref, dog_ref,
             a_next, e_next, dxr_next, abuf, bbuf, ebuf, dxr0, dprb_buf, dpib_buf, sums):
        i = pl.program_id(0)
        ri = ni - 1 - i

        @pl.when(i == 0)
        def _():
            for r in (dcw_ref, dcb_ref, dwa_ref, dba_ref, dwx_ref, dbx_ref, dlam_ref, dog_ref,
                      a_next, e_next, dxr_next):
                r[...] = jnp.zeros_like(r)

        keep_prev = jnp.where(ri == 0, 0.0, 1.0)
        cw_ = _taps(cw_ref)
        lam_ = lam_ref[...]
        xl = px_ref[...]
        a = saved_ref[3]
        sp = _softplus(-lam_)
        sums[...] = jnp.zeros_like(sums)
        og = og_ref[...]
        lane_groups = [slice(q * GATE_LANES, (q + 1) * GATE_LANES) for q in range(W // GATE_LANES)]

        def fold(v):
            return sum(v[:, k * LANES:(k + 1) * LANES] for k in range(GATE_LANES // LANES))

        def out_grads(g, carry):
            rows = pl.ds(pl.multiple_of(g * 16, 16), 16)
            s_yy = jnp.zeros((16, LANES), F32)
            s_dy = jnp.zeros((16, LANES), F32)
            for ls in lane_groups:
                yl = h_ref[rows, ls] * _gelu(pg_ref[rows, ls])
                s_yy = s_yy + fold(yl * yl)
                s_dy = s_dy + fold(dy_ref[rows, ls] * og[:, ls] * yl)
            rr = lax.rsqrt(jnp.sum(s_yy, axis=1, keepdims=True) * (1.0 / W) + RMS_EPS)
            c = rr * rr * jnp.sum(s_dy, axis=1, keepdims=True) * (1.0 / W)
            for ls in lane_groups:
                hq = h_ref[rows, ls]
                gg, dgg = _gelu_parts(pg_ref[rows, ls])
                yl = hq * gg
                dyb = dy_ref[rows, ls]
                sums[3, :, ls] += dyb * (yl * rr)
                dyl = rr * (dyb * og[:, ls] - yl * c)
                bbuf[rows, ls] = dyl * gg
                dp_ref[rows, ls] = (dyl * hq * dgg).astype(BF16)
            return carry

        lax.fori_loop(0, tm // 16, out_grads, 0, unroll=2)
        dog_ref[...] += _col_sum(sums[3])

        an = _shift_up(a, a_next[...], 1)
        eb = bbuf[...]
        sub = lax.broadcasted_iota(jnp.int32, (tm, W), 0) & 7
        for d in (1, 2, 4):
            m = sub < 8 - d
            a_s = jnp.where(m, pltpu.roll(an, tm - d, 0), 1.0)
            e_s = jnp.where(m, pltpu.roll(eb, tm - d, 0), 0.0)
            eb = an * e_s + eb
            an = an * a_s
        abuf[...] = an
        bbuf[...] = eb

        def step(g, carry):
            r0 = pl.multiple_of((ng - 1 - g) * 8, 8)
            ebuf[pl.ds(r0, 8), :] = abuf[pl.ds(r0, 8), :] * carry + bbuf[pl.ds(r0, 8), :]
            return jnp.broadcast_to(ebuf[pl.ds(r0, 1), :], (8, W))

        lax.fori_loop(0, ng, step, jnp.broadcast_to(e_next[0:1, :], (8, W)))
        a_next[...] = a[0:8]
        e_next[...] = ebuf[0:8, :]

        row16 = lax.broadcasted_iota(jnp.int32, (16, GATE_LANES), 0)
        h_before = hp_ref[7:8, :] * keep_prev
        r_scale = (-LRU_C) * sp

        def gate_grads(g, carry):
            r0 = pl.multiple_of(g * 16, 16)
            above = jnp.maximum(r0 - 1, 0)
            for q in range(W // GATE_LANES):
                ls = slice(q * GATE_LANES, (q + 1) * GATE_LANES)
                e = ebuf[pl.ds(r0, 16), ls]
                h_prev_row = jnp.where(g == 0, h_before[:, ls], h_ref[pl.ds(above, 1), ls])
                hm1 = jnp.where(row16 == 0, h_prev_row, pltpu.roll(h_ref[pl.ds(r0, 16), ls], 1, 0))
                xr_, r_, ig_, a_, mult_, a2m_ = [saved_ref[k, pl.ds(r0, 16), ls] for k in range(6)]
                em = e * mult_
                dxr0[pl.ds(r0, 16), ls] = em * ig_
                dla = e * hm1 * a_ - e * ig_ * xr_ * a2m_
                dpr = dla * r_scale[:, ls] * r_ * (1.0 - r_)
                dpi = em * xr_ * ig_ * (1.0 - ig_)
                sums[0, :, ls] += dla * r_
                sums[1, :, ls] += dpr
                sums[2, :, ls] += dpi
                dprb_buf[pl.ds(r0, 16), ls] = dpr.astype(BF16)
                dpib_buf[pl.ds(r0, 16), ls] = dpi.astype(BF16)
            return carry

        lax.fori_loop(0, tm // 16, gate_grads, 0)
        dlam_ref[...] += (-LRU_C) * _col_sum(sums[0])
        dba_ref[...] += _col_sum(sums[1])
        dbx_ref[...] += _col_sum(sums[2])
        dprb = dprb_buf[...]
        dpib = dpib_buf[...]
        dxr = dxr0[...]
        xrb = saved_ref[0].astype(BF16)
        back = []
        for h in range(H):
            cs = slice(h * CHUNK, (h + 1) * CHUNK)
            wab = wa_ref[h].astype(BF16)
            wxb = wx_ref[h].astype(BF16)
            back.append(lax.dot_general(dprb[:, cs], wab, _DN_NT, preferred_element_type=F32)
                        + lax.dot_general(dpib[:, cs], wxb, _DN_NT, preferred_element_type=F32))
            dwa_ref[h] += lax.dot_general(xrb[:, cs], dprb[:, cs], _DN_TN, preferred_element_type=F32)
            dwx_ref[h] += lax.dot_general(xrb[:, cs], dpib[:, cs], _DN_TN, preferred_element_type=F32)
        dxr = dxr + jnp.concatenate(back, axis=1)

        nxt = dxr_next[...]
        ahead = [_shift_up(dxr, nxt, j) for j in range(4)]
        dxl = cw_[3] * ahead[0] + cw_[2] * ahead[1] + cw_[1] * ahead[2] + cw_[0] * ahead[3]
        dxr_next[...] = dxr[0:8]
        for k in range(4):
            dcw_ref[k:k + 1, :] += _col_sum(xl * ahead[3 - k])
        dcb_ref[...] += _col_sum(dxr)
        dp_ref[:, W:] = dxl.astype(BF16)

        @pl.when(i == ni - 1)
        def _():
            dlam_ref[...] = -dlam_ref[...] * _sigmoid(-lam_)

    vec = pl.BlockSpec((1, W), lambda i: (0, 0))
    mat = pl.BlockSpec((H, CHUNK, CHUNK), lambda i: (0, 0, 0))
    rev = lambda i: ni - 1 - i
    prev = lambda i: jnp.maximum(rev(i) * hb - 1, 0)
    vshape = jax.ShapeDtypeStruct((1, W), F32)
    mshape = jax.ShapeDtypeStruct((H, CHUNK, CHUNK), F32)
    tile = lambda: pltpu.VMEM((tm, W), F32)
    car = lambda: pltpu.VMEM((8, W), F32)
    return pl.pallas_call(
        body, name=name, grid=(ni,),
        in_specs=[pl.BlockSpec((tm, W), lambda i: (rev(i), 2)), pl.BlockSpec((tm, W), lambda i: (rev(i), 3)),
                  pl.BlockSpec((6, tm, W), lambda i: (0, rev(i), 0)),
                  pl.BlockSpec((tm, W), lambda i: (rev(i), 0)), pl.BlockSpec((8, W), lambda i: (prev(i), 0)),
                  pl.BlockSpec((tm, W), lambda i: (rev(i), 1)),
                  pl.BlockSpec((4, W), lambda i: (0, 0)), mat, mat, vec, vec,
                  pl.BlockSpec(memory_space=pl.ANY)],
        out_specs=[pl.BlockSpec((tm, 2 * W), lambda i: (rev(i), 1)), pl.BlockSpec((4, W), lambda i: (0, 0)), vec,
                   mat, vec, mat, vec, vec, vec],
        out_shape=[jax.ShapeDtypeStruct((S, 4 * W), BF16), jax.ShapeDtypeStruct((4, W), F32), vshape,
                   mshape, vshape, mshape, vshape, vshape, vshape],
        input_output_aliases={11: 0},
        scratch_shapes=[car(), car(), car(), tile(), tile(), tile(), tile(), pltpu.VMEM((tm, W), BF16),
                        pltpu.VMEM((tm, W), BF16), pltpu.VMEM((4, 16, W), F32)],
        compiler_params=_cp(1, 56))(p, p, saved, hs, hs, d_y, cw, wa, wx, lam, out_g, dp_half)


def _rows128(a):
    return a.reshape(-1, LANES).astype(F32)


def _pack(arrays, pad_to=256):
    flat = jnp.concatenate([_rows128(a) for a in arrays], axis=0)
    pad = (-flat.shape[0]) % pad_to
    if pad:
        flat = jnp.concatenate([flat, jnp.zeros((pad, LANES), F32)], axis=0)
    return flat


def _unpack(flat, shapes):
    out, r = [], 0
    for s in shapes:
        n = 1
        for d in s:
            n *= d
        out.append(flat[r:r + n // LANES].reshape(s))
        r += n // LANES
    return out


def kernel(x, norm1_g, w_in, gm_v_g, gm_v_b, gm_ws, gm_bs, lru_conv_w, lru_conv_b, lru_wa, lru_ba, lru_wx, lru_bx, lru_lambda, gm_out_g, lru_out_g, w_out, norm2_g, ffn_w_up, ffn_conv_w, ffn_conv_b, ffn_w_down, final_g, loss_target, m_norm1_g, m_w_in, m_gm_v_g, m_gm_v_b, m_gm_ws, m_gm_bs, m_lru_conv_w, m_lru_conv_b, m_lru_wa, m_lru_ba, m_lru_wx, m_lru_bx, m_lru_lambda, m_gm_out_g, m_lru_out_g, m_w_out, m_norm2_g, m_ffn_w_up, m_ffn_conv_w, m_ffn_conv_b, m_ffn_w_down, m_final_g, v_norm1_g, v_w_in, v_gm_v_g, v_gm_v_b, v_gm_ws, v_gm_bs, v_lru_conv_w, v_lru_conv_b, v_lru_wa, v_lru_ba, v_lru_wx, v_lru_bx, v_lru_lambda, v_gm_out_g, v_lru_out_g, v_w_out, v_norm2_g, v_ffn_w_up, v_ffn_conv_w, v_ffn_conv_b, v_ffn_w_down, v_final_g):
    wts = dict(norm1_g=norm1_g, w_in=w_in, gm_v_g=gm_v_g, gm_v_b=gm_v_b, gm_ws=gm_ws, gm_bs=gm_bs,
               lru_conv_w=lru_conv_w, lru_conv_b=lru_conv_b, lru_wa=lru_wa, lru_ba=lru_ba, lru_wx=lru_wx,
               lru_bx=lru_bx, lru_lambda=lru_lambda, gm_out_g=gm_out_g, lru_out_g=lru_out_g, w_out=w_out,
               norm2_g=norm2_g, ffn_w_up=ffn_w_up, ffn_conv_w=ffn_conv_w, ffn_conv_b=ffn_conv_b,
               ffn_w_down=ffn_w_down, final_g=final_g)
    mom = dict(norm1_g=m_norm1_g, w_in=m_w_in, gm_v_g=m_gm_v_g, gm_v_b=m_gm_v_b, gm_ws=m_gm_ws, gm_bs=m_gm_bs,
               lru_conv_w=m_lru_conv_w, lru_conv_b=m_lru_conv_b, lru_wa=m_lru_wa, lru_ba=m_lru_ba, lru_wx=m_lru_wx,
               lru_bx=m_lru_bx, lru_lambda=m_lru_lambda, gm_out_g=m_gm_out_g, lru_out_g=m_lru_out_g, w_out=m_w_out,
               norm2_g=m_norm2_g, ffn_w_up=m_ffn_w_up, ffn_conv_w=m_ffn_conv_w, ffn_conv_b=m_ffn_conv_b,
               ffn_w_down=m_ffn_w_down, final_g=m_final_g)
    var = dict(norm1_g=v_norm1_g, w_in=v_w_in, gm_v_g=v_gm_v_g, gm_v_b=v_gm_v_b, gm_ws=v_gm_ws, gm_bs=v_gm_bs,
               lru_conv_w=v_lru_conv_w, lru_conv_b=v_lru_conv_b, lru_wa=v_lru_wa, lru_ba=v_lru_ba, lru_wx=v_lru_wx,
               lru_bx=v_lru_bx, lru_lambda=v_lru_lambda, gm_out_g=v_gm_out_g, lru_out_g=v_lru_out_g, w_out=v_w_out,
               norm2_g=v_norm2_g, ffn_w_up=v_ffn_w_up, ffn_conv_w=v_ffn_conv_w, ffn_conv_b=v_ffn_conv_b,
               ffn_w_down=v_ffn_w_down, final_g=v_final_g)

    xi, yi, ci = lax.axis_index("x"), lax.axis_index("y"), lax.axis_index("c")
    chip = 2 * xi + yi
    dev = 2 * chip + ci
    core_chip = jnp.stack([ci, chip]).astype(jnp.int32)

    xs = x[0]
    tgt = loss_target[0]
    S, D = xs.shape
    H = gm_ws.shape[1]
    W = H * CHUNK
    Fd = ffn_w_down.shape[1] * N_DEV
    lcw_cols = lru_conv_w.shape[2]
    fcw_cols = ffn_conv_w.shape[2]

    dev1 = jnp.reshape(dev, (1,)).astype(jnp.int32)

    def place_own(shards, name):
        return [_place_own_call(s, dev1, dt, "%s_own%d" % (name, k)) for k, (s, dt) in enumerate(shards)]

    def gather_start(shards, name, after=()):
        lands = place_own(shards, name)
        return _exchange_start([], lands, 4 * len(lands), _gather_stage1_copies(len(lands)), name + "_ici", after)

    def gather_forward(lands, name, after=()):
        return _exchange_start([], lands, 3 * len(lands), _gather_stage2_copies(len(lands)), name + "_d2d", after)

    def pair_start(g, name, after=()):
        return _exchange_start([g], [lax.empty((4,) + g.shape[1:], F32)], 4, _pair_copies(1), name, after)

    def chip_start(p16, name, after=()):
        return _exchange_start([p16], [lax.empty((3,) + p16.shape[1:], BF16)], 3, _chip_copies(1), name, after)

    vgm_g, vgm_b = gm_v_g, gm_v_b
    ws, wa, wx = gm_ws[0], lru_wa[0], lru_wx[0]
    bsb = jnp.broadcast_to(gm_bs[0][:, :, None], (H, CHUNK, CHUNK))
    ba, bx = lru_ba.reshape(1, W), lru_bx.reshape(1, W)
    fcb = ffn_conv_b
    fing = final_g.reshape(1, D)

    conv_pack = _pack([lru_conv_w[0], ffn_conv_w[0]], pad_to=8)
    lands = place_own([(w_in[0], BF16), (conv_pack, F32)], "gather_in")
    ga_pair = _exchange_start([], lands, 2, _gather_stage1_copies(2, to_chips=False), "gather_in_pair")
    ga1 = _exchange_start([], ga_pair.bufs, 6, _gather_stage1_copies(2, to_sibling=False), "gather_in_ici")
    h1 = _rmsnorm_call(xs, norm1_g, "norm1", deps=(ga1.token,))
    ga_pair.bufs = ga1.bufs
    _, la = _exchange_wait(ga_pair, after=(h1,))
    own_blocks = jnp.stack([dev, dev + 1 - 2 * ci]).astype(jnp.int32)
    other_blocks = ((2 * chip + 2 + jnp.arange(N_DEV - 2)) % N_DEV).astype(jnp.int32)
    p_own = _mm_some_blocks_call(h1, la[0], own_blocks, F32, "in_proj_own")
    ga1.bufs = la
    _, la = _exchange_wait(ga1, after=(p_own,))
    ga2 = gather_forward(la, "gather_in")
    gb1 = gather_start([(w_out[0], BF16)], "gather_out", after=(ga2.token,))
    gc1 = gather_start([(ffn_w_up[0], BF16)], "gather_up", after=(gb1.token,))
    gd1 = gather_start([(ffn_w_down[0], BF16)], "gather_down", after=(gc1.token,))
    _, (win_g, conv_g) = _exchange_wait(ga2, after=(gd1.token,))
    n_l = 4 * lcw_cols // LANES
    n_f = 3 * fcw_cols // LANES
    lcw = conv_g[:, :n_l].reshape(N_DEV, 4, lcw_cols).transpose(1, 0, 2).reshape(4, N_DEV * lcw_cols)
    fcw = conv_g[:, n_l:n_l + n_f].reshape(N_DEV, 3, fcw_cols).transpose(1, 0, 2).reshape(3, N_DEV * fcw_cols)

    p = _mm_some_blocks_call(h1, win_g, other_blocks, F32, "in_proj_rest", out_so_far=p_own)
    win_rows = _unblock_call(win_g, "w_in_rows")
    _, lb = _exchange_wait(gb1, after=(p,))
    gb2 = gather_forward(lb, "gather_out")
    y_half = _gm_fwd_call(p, vgm_g, vgm_b, ws, bsb, gm_out_g, "gmlp_fwd", deps=(gb2.token,))
    y, hs, lru_saved = _lru_fwd_call(p, lcw, lru_conv_b, wa, ba, wx, bx, lru_lambda, lru_out_g, y_half, "lru_fwd")
    _, lc = _exchange_wait(gc1, after=(y,))
    gc2 = gather_forward(lc, "gather_up")
    _, (wout_g,) = _exchange_wait(gb2, after=(y, gc2.token))
    wout_full = wout_g.reshape(D, D)
    x2 = _mm_out_call(xs, y, wout_full, "out_proj")
    h2 = _rmsnorm_call(x2, norm2_g, "norm2")
    _, (wup_g,) = _exchange_wait(gc2, after=(h2,))
    _, ld = _exchange_wait(gd1, after=(h2,))
    gd2 = gather_forward(ld, "gather_down")
    up3, upc3, f = _ffn_up_act_call(h2, wup_g, fcw, fcb, "ffn_up", deps=(gd2.token,))
    _, (wdown_g,) = _exchange_wait(gd2, after=(f,))
    wdown_full = wdown_g.reshape(Fd, D)
    dx3, dx3b, loss_acc, d_final = _mm_down_loss_call(x2, f, wdown_full, fing, tgt, "ffn_down_loss")

    g_wdown = _mm_tn_rows_call(f, dx3b, "ffn_down_dw").reshape((N_DEV,) + ffn_w_down.shape[1:])
    pd = pair_start(g_wdown, "pair_down")
    d_up3, dfcw_g, dfcw_v, dfcb_g, dfcb_v = _ffn_down_dx_act_bwd_call(dx3b, wdown_full, up3, upc3, fcw, "ffn_down_dx",
                                                                     deps=(pd.token,))
    (g_wdown,), (r1,) = _exchange_wait(pd, after=(d_up3,))
    own_down, p16 = _pair_add_call(g_wdown, r1, core_chip, "pair_add_down")
    cd = chip_start(p16, "chip_down")
    g_wup = _mm_tn_cols_call(h2, d_up3, N_DEV, ffn_w_up.shape[2], "ffn_up_dw", deps=(cd.token,))
    pu = pair_start(g_wup, "pair_up")
    dx2, dx2b, d_norm2 = _mm_dx_norm_call(d_up3, wup_g, dx3, x2, norm2_g, "ffn_up_dx", deps=(pu.token,))
    g_wout = _mm_tn_rows_call(y, dx2b, "out_proj_dw").reshape((N_DEV,) + w_out.shape[1:])
    po = pair_start(g_wout, "pair_out")
    d_y = _mm_nt_call(dx2b, wout_full, F32, "out_proj_dx", deps=(po.token,))
    (g_wup,), (r1,) = _exchange_wait(pu, after=(d_y,))
    own_up, p16 = _pair_add_call(g_wup, r1, core_chip, "pair_add_up")
    _, (r2_down,) = _exchange_wait(cd, after=(p16,))
    cu = chip_start(p16, "chip_up", after=(r2_down,))
    dp_half, d_vg, d_vb, d_ws, d_bs, d_gog = _gm_bwd_call(p, d_y, vgm_g, vgm_b, ws, bsb, gm_out_g, "gmlp_bwd",
                                                          deps=(cu.token,))
    d_p2, d_lcw, d_lcb, d_wa, d_ba, d_wx, d_bx, d_lam, d_log = _lru_bwd_call(
        p, hs, lru_saved, d_y, lcw, wa, wx, lru_lambda, lru_out_g, dp_half, "lru_bwd")
    d_p = d_p2[None]
    (g_wout,), (r1,) = _exchange_wait(po, after=(d_p,))
    own_out, p16_out = _pair_add_call(g_wout, r1, core_chip, "pair_add_out")
    g_win = _mm_tn_cols_call(h1, d_p, N_DEV, w_in.shape[2], "in_proj_dw")
    pi = pair_start(g_win, "pair_in")
    _, (r2_up,) = _exchange_wait(cu, after=(g_win,))
    co = chip_start(p16_out, "chip_out", after=(r2_up,))
    gx_a, dn_a = _mm_nt_norm_call(d_p[0], win_rows, dx2, xs, norm1_g, "in_proj_dx_a", deps=(co.token, pi.token),
                                  part=(0, 2))
    (g_win,), (r1,) = _exchange_wait(pi, after=(gx_a,))
    own_in, p16 = _pair_add_call(g_win, r1, core_chip, "pair_add_in")
    _, (r2_out,) = _exchange_wait(co, after=(p16,))
    ci_ = chip_start(p16, "chip_in", after=(r2_out,))
    grad_x, dn_b = _mm_nt_norm_call(d_p[0], win_rows, dx2, xs, norm1_g, "in_proj_dx_b", deps=(ci_.token,),
                                    part=(1, 2), dx_so_far=gx_a)

    small_g = dict(norm1_g=dn_a + dn_b, gm_v_g=d_vg, gm_v_b=d_vb, gm_ws=d_ws, gm_bs=d_bs[:, :, 0], lru_conv_b=d_lcb,
                   lru_wa=d_wa, lru_ba=d_ba, lru_wx=d_wx, lru_bx=d_bx, lru_lambda=d_lam, gm_out_g=d_gog,
                   lru_out_g=d_log, norm2_g=d_norm2,
                   ffn_conv_b=jnp.concatenate([dfcb_g, dfcb_v], axis=1), final_g=d_final)
    rep = _pack([small_g[n] for n in SMALL])
    conv_part = _pack([d_lcw, jnp.concatenate([dfcw_g, dfcw_v], axis=1)], pad_to=8)
    n_rep, n_conv = rep.shape[0], conv_part.shape[0]
    gs1 = gather_start([(jnp.concatenate([rep, conv_part], axis=0), F32)], "gather_small")

    def adamw_big(n, own, r2, deps=()):
        return _adamw_call(wts[n][0], mom[n][0], var[n][0], [(own, None), (r2, 0), (r2, 1), (r2, 2)], "adamw_" + n, deps)

    res = {}
    res["ffn_w_down"] = adamw_big("ffn_w_down", own_down, r2_down, (gs1.token,))
    res["ffn_w_up"] = adamw_big("ffn_w_up", own_up, r2_up, (gs1.token,))
    res["w_out"] = adamw_big("w_out", own_out, r2_out, (gs1.token,))
    _, ls = _exchange_wait(gs1, after=(res["w_out"][0], res["ffn_w_up"][0], res["ffn_w_down"][0]))
    gs2 = gather_forward(ls, "gather_small")
    _, (r2_in,) = _exchange_wait(ci_, after=(gs2.token,))
    res["w_in"] = adamw_big("w_in", own_in, r2_in)
    _, (parts,) = _exchange_wait(gs2, after=(res["w_in"][0],))
    g_rep, d_rep, m_rep, v_rep = _adamw_call(
        _pack([wts[n] for n in SMALL]), _pack([mom[n] for n in SMALL]), _pack([var[n] for n in SMALL]),
        [(parts, k) for k in range(N_DEV)], "adamw_small")
    shapes = [wts[n].shape for n in SMALL]
    for n, g_, d_, m_, v_ in zip(SMALL, _unpack(g_rep, shapes), _unpack(d_rep, shapes), _unpack(m_rep, shapes),
                                 _unpack(v_rep, shapes)):
        res[n] = (g_, d_, m_, v_)
    conv_sum = _sum_call(parts, n_rep, n_conv, "sum_conv_grads")
    g_lcw = conv_sum[:4 * W // LANES].reshape(4, W)
    g_fcw = conv_sum[4 * W // LANES:4 * W // LANES + 6 * Fd // LANES].reshape(3, 2 * Fd)
    for n, full in (("lru_conv_w", g_lcw), ("ffn_conv_w", g_fcw)):
        cols = wts[n].shape[2]
        mine = lax.dynamic_slice_in_dim(full, dev * cols, cols, axis=1)
        res[n] = _adamw_call(wts[n][0], mom[n][0], var[n][0], [(mine, None)], "adamw_" + n)

    loss = lax.psum(loss_acc[0, 0], ("x", "y", "c"))
    outs = [[], [], [], []]
    for n in WEIGHTS:
        for k in range(4):
            outs[k].append(res[n][k].reshape(wts[n].shape))
    return (loss, grad_x[None], *outs[0], *outs[1], *outs[2], *outs[3])
```

```python
import functools
import math

import jax
import jax.numpy as jnp
from jax import lax
from jax.experimental import pallas as pl
from jax.experimental.pallas import tpu as pltpu

F32 = jnp.float32
BF16 = jnp.bfloat16

RMS_EPS = 1e-6
LN_EPS = 1e-5
LRU_C = 8.0
CHUNK = 128
ADAM_LR = 0.001
ADAM_B1 = 0.9
ADAM_B2 = 0.999
ADAM_EPS = 1e-08
ADAM_WD = 0.01
ADAM_STEP = 10
N_DEV = 8
LANES = 128
MIB = 1024 * 1024

WEIGHTS = ['norm1_g', 'w_in', 'gm_v_g', 'gm_v_b', 'gm_ws', 'gm_bs', 'lru_conv_w', 'lru_conv_b', 'lru_wa', 'lru_ba',
           'lru_wx', 'lru_bx', 'lru_lambda', 'gm_out_g', 'lru_out_g', 'w_out', 'norm2_g', 'ffn_w_up', 'ffn_conv_w',
           'ffn_conv_b', 'ffn_w_down', 'final_g']
BIG = ['w_in', 'w_out', 'ffn_w_up', 'ffn_w_down']
CONV = ['lru_conv_w', 'ffn_conv_w']
SMALL = [n for n in WEIGHTS if n not in BIG and n not in CONV]

_DN_NT = (((1,), (1,)), ((), ()))
_DN_TN = (((0,), (0,)), ((), ()))
_GELU_C = 0.7978845608028654


def _cp(n_axes, vmem_mib=48):
    return pltpu.CompilerParams(dimension_semantics=("arbitrary",) * n_axes, vmem_limit_bytes=vmem_mib * MIB)


def _tile(n, pref, mult=8):
    t = min(pref, n)
    t -= t % mult
    while t >= mult:
        if n % t == 0:
            return t
        t -= mult
    return n


def _gelu_gate(z, z2):
    return 0.5 * jnp.tanh(z * ((_GELU_C * 0.044715) * z2 + _GELU_C)) + 0.5


def _gelu(z):
    return z * _gelu_gate(z, z * z)


def _gelu_parts(z):
    z2 = z * z
    s = _gelu_gate(z, z2)
    g = z * s
    dg = s + g * (1.0 - s) * ((6.0 * _GELU_C * 0.044715) * z2 + 2.0 * _GELU_C)
    return g, dg


def _sigmoid(z):
    return 0.5 + 0.5 * jnp.tanh(0.5 * z)


def _softplus(z):
    t = jnp.exp(-jnp.abs(z))
    u = 1.0 + t
    log1p = jnp.where(u == 1.0, t, jnp.log(u) * t / (u - 1.0))
    return jnp.maximum(z, 0.0) + log1p


def _rows_mean(v):
    return jnp.mean(v, axis=-1, keepdims=True)


def _col_sum(v):
    return jnp.sum(v, axis=0, keepdims=True)


def _shift_down(prev8, cur, k):
    if k == 0:
        return cur
    z = jnp.concatenate([prev8, cur], axis=0)
    return pltpu.roll(z, k, 0)[8:]


def _shift_up(cur, next8, k):
    if k == 0:
        return cur
    n = cur.shape[0]
    z = jnp.concatenate([cur, next8], axis=0)
    return pltpu.roll(z, n + 8 - k, 0)[:n]


def _mesh_pos():
    return lax.axis_index("x"), lax.axis_index("y"), lax.axis_index("c")


def _any_specs(n):
    return [pl.BlockSpec(memory_space=pl.ANY)] * n


def _pallas(body, n_in, deps, **kw):
    nd = len(deps)
    if not nd:
        return pl.pallas_call(body, **kw)

    def ordered(*refs):
        body(*refs[:n_in], *refs[n_in + nd:])

    kw["in_specs"] = list(kw["in_specs"]) + _any_specs(nd)
    return pl.pallas_call(ordered, **kw)


_HBM = pl.BlockSpec(memory_space=pltpu.HBM)
_SEM = pl.BlockSpec(memory_space=pltpu.SEMAPHORE)
_EFFECT = pltpu.SideEffectType.DATAFLOW_SIDE_EFFECTING


class _InFlight:
    def __init__(self, sems, bufs, token, n_src, n_copies, make_copies, name):
        self.sems, self.bufs, self.token = sems, bufs, token
        self.n_src, self.n_copies, self.make_copies, self.name = n_src, n_copies, make_copies, name


def _exchange_start(srcs, lands, n_copies, make_copies, name, after=()):
    bufs = list(srcs) + list(lands)
    nb, na = len(bufs), len(after)
    ns = len(srcs)

    def body(*refs):
        b_refs = refs[:nb]
        outs = refs[nb + na:]
        send, recv = outs[:n_copies], outs[n_copies:2 * n_copies]
        token = outs[-1]
        for cp in make_copies(b_refs[:ns], b_refs[ns:], send, recv):
            cp.start()
        token[...] = jnp.zeros_like(token)

    out = pl.pallas_call(
        body, name=name,
        out_shape=[pltpu.SemaphoreType.DMA(())] * (2 * n_copies) + [pltpu.HBM(b.shape, b.dtype) for b in bufs]
        + [jax.ShapeDtypeStruct((8, LANES), F32)],
        in_specs=[_HBM] * nb + _any_specs(na),
        out_specs=[_SEM] * (2 * n_copies) + [_HBM] * nb + [pl.BlockSpec(memory_space=pltpu.VMEM)],
        input_output_aliases={i: 2 * n_copies + i for i in range(nb)},
        compiler_params=pltpu.CompilerParams(has_side_effects=_EFFECT),
    )(*[pltpu.with_memory_space_constraint(b, pltpu.HBM) for b in bufs], *after)
    return _InFlight(out[:2 * n_copies], out[2 * n_copies:2 * n_copies + nb], out[-1], ns, n_copies, make_copies, name)


def _exchange_wait(fl, after=()):
    nb, na, nc, ns = len(fl.bufs), len(after), fl.n_copies, fl.n_src

    def body(*refs):
        b_refs = refs[:nb]
        sems = refs[nb:nb + 2 * nc]
        copies = fl.make_copies(b_refs[:ns], b_refs[ns:], sems[:nc], sems[nc:])
        for cp in copies:
            cp.wait_send()
        for cp in copies:
            cp.wait_recv()

    out = pl.pallas_call(
        body, name=fl.name + "_wait",
        out_shape=[pltpu.HBM(b.shape, b.dtype) for b in fl.bufs],
        in_specs=[_HBM] * nb + [_SEM] * (2 * nc) + _any_specs(na),
        out_specs=[_HBM] * nb,
        input_output_aliases={i: i for i in range(nb)},
        compiler_params=pltpu.CompilerParams(has_side_effects=_EFFECT),
    )(*fl.bufs, *fl.sems, *after)
    return list(out[:ns]), list(out[ns:])


def _remote(src, dst, send_sem, recv_sem, to):
    return pltpu.make_async_remote_copy(src_ref=src, dst_ref=dst, send_sem=send_sem, recv_sem=recv_sem,
                                        device_id=to, device_id_type=pl.DeviceIdType.MESH)


def _gather_stage1_copies(n, to_sibling=True, to_chips=True):
    def make(s_refs, l_refs, send, recv):
        x, y, c = _mesh_pos()
        own = 4 * x + 2 * y + c
        targets = ([(x, y, 1 - c)] if to_sibling else []) + (
            [(1 - x, y, c), (x, 1 - y, c), (1 - x, 1 - y, c)] if to_chips else [])
        m = len(targets)
        return [_remote(l_refs[a].at[own], l_refs[a].at[own], send[m * a + k], recv[m * a + k], to)
                for a in range(n) for k, to in enumerate(targets)]
    return make


def _gather_stage2_copies(n):
    def make(s_refs, l_refs, send, recv):
        x, y, c = _mesh_pos()
        blocks = [4 * (1 - x) + 2 * y + c, 4 * x + 2 * (1 - y) + c, 4 * (1 - x) + 2 * (1 - y) + c]
        return [_remote(l_refs[a].at[b], l_refs[a].at[b], send[3 * a + j], recv[3 * a + j], (x, y, 1 - c))
                for a in range(n) for j, b in enumerate(blocks)]
    return make


def _pair_copies(n):
    def make(s_refs, l_refs, send, recv):
        x, y, c = _mesh_pos()
        return [_remote(s_refs[a].at[2 * k + 1 - c], l_refs[a].at[k], send[4 * a + k], recv[4 * a + k], (x, y, 1 - c))
                for a in range(n) for k in range(4)]
    return make


def _chip_copies(n):
    def make(s_refs, l_refs, send, recv):
        x, y, c = _mesh_pos()
        chips = [(1 - x, y), (x, 1 - y), (1 - x, 1 - y)]
        return [_remote(s_refs[a].at[2 * ch[0] + ch[1]], l_refs[a].at[j], send[3 * a + j], recv[3 * a + j], (*ch, c))
                for a in range(n) for j, ch in enumerate(chips)]
    return make


def _place_own_call(shard, dev, dtype, name):
    R, C = shard.shape
    tr = _tile(R, max(16, MIB // (C * 4)), 16)

    def body(d_ref, s_ref, o_ref):
        o_ref[...] = s_ref[...].astype(dtype)

    grid_spec = pltpu.PrefetchScalarGridSpec(
        num_scalar_prefetch=1, grid=(R // tr,),
        in_specs=[pl.BlockSpec((tr, C), lambda r, d: (r, 0))],
        out_specs=pl.BlockSpec((None, tr, C), lambda r, d: (d[0], r, 0)))
    return pl.pallas_call(body, name=name, grid_spec=grid_spec,
                          out_shape=jax.ShapeDtypeStruct((N_DEV, R, C), dtype), compiler_params=_cp(1))(dev, shard)


def _pair_add_call(g, r1, core_chip, name):
    _, R, C = g.shape
    tr = _tile(R, max(16, (2 * MIB) // (C * 4)), 16)

    def body(cc_ref, g_ref, r_ref, p32_ref, p16_ref):
        s = g_ref[...] + r_ref[...]
        p16_ref[...] = s.astype(BF16)

        @pl.when(pl.program_id(1) == cc_ref[1])
        def _():
            p32_ref[...] = s

    grid_spec = pltpu.PrefetchScalarGridSpec(
        num_scalar_prefetch=1, grid=(R // tr, 4),
        in_specs=[pl.BlockSpec((None, tr, C), lambda r, k, cc: (2 * k + cc[0], r, 0)),
                  pl.BlockSpec((None, tr, C), lambda r, k, cc: (k, r, 0))],
        out_specs=[pl.BlockSpec((tr, C), lambda r, k, cc: (r, 0)),
                   pl.BlockSpec((None, tr, C), lambda r, k, cc: (k, r, 0))])
    return pl.pallas_call(
        body, name=name, grid_spec=grid_spec,
        out_shape=[jax.ShapeDtypeStruct((R, C), F32), jax.ShapeDtypeStruct((4, R, C), BF16)],
        compiler_params=_cp(2))(core_chip, g, r1)


def _adamw_call(w, m, v, addends, name, deps=()):
    R, C = w.shape
    tr = _tile(R, max(8, MIB // (C * 4)), 16)
    na = len(addends)
    c1 = 1.0 - ADAM_B1 ** ADAM_STEP
    c2 = 1.0 - ADAM_B2 ** ADAM_STEP

    def body(*refs):
        w_ref, m_ref, v_ref = refs[:3]
        a_refs = refs[3:3 + na]
        g_ref, d_ref, nm_ref, nv_ref = refs[3 + na:]
        g = a_refs[0][...].astype(F32)
        for a_ref in a_refs[1:]:
            g = g + a_ref[...].astype(F32)
        nm = ADAM_B1 * m_ref[...] + (1.0 - ADAM_B1) * g
        nv = ADAM_B2 * v_ref[...] + (1.0 - ADAM_B2) * (g * g)
        g_ref[...] = g
        nm_ref[...] = nm
        nv_ref[...] = nv
        d_ref[...] = -ADAM_LR * ((nm / c1) / (jnp.sqrt(nv / c2) + ADAM_EPS) + ADAM_WD * w_ref[...])

    flat = pl.BlockSpec((tr, C), lambda r: (r, 0))
    a_specs = [flat if k is None else pl.BlockSpec((None, tr, C), functools.partial(lambda r, kk: (kk, r, 0), kk=k))
               for _, k in addends]
    out = jax.ShapeDtypeStruct((R, C), F32)
    return _pallas(
        body, 3 + na, deps, name=name, grid=(R // tr,),
        in_specs=[flat, flat, flat] + a_specs, out_specs=[flat] * 4, out_shape=[out] * 4,
        compiler_params=_cp(1))(w, m, v, *[a for a, _ in addends], *deps)


def _sum_call(parts, row0, rows, name):
    n = parts.shape[0]
    tr = _tile(math.gcd(row0, rows), 256, 8)
    b0 = row0 // tr

    def body(p_ref, o_ref):
        s = p_ref[0]
        for k in range(1, n):
            s = s + p_ref[k]
        o_ref[...] = s

    return pl.pallas_call(
        body, name=name, grid=(rows // tr,),
        in_specs=[pl.BlockSpec((n, tr, LANES), lambda r: (0, r + b0, 0))],
        out_specs=pl.BlockSpec((tr, LANES), lambda r: (r, 0)),
        out_shape=jax.ShapeDtypeStruct((rows, LANES), F32), compiler_params=_cp(1))(parts)


def _rmsnorm_call(x, g, name, deps=()):
    S, D = x.shape
    tm = _tile(S, 512, 16)

    def body(x_ref, g_ref, o_ref):
        xv = x_ref[...]
        r = lax.rsqrt(_rows_mean(xv * xv) + RMS_EPS)
        o_ref[...] = (xv * r * g_ref[...]).astype(BF16)

    return _pallas(
        body, 2, deps, name=name, grid=(S // tm,),
        in_specs=[pl.BlockSpec((tm, D), lambda i: (i, 0)), pl.BlockSpec((1, D), lambda i: (0, 0))],
        out_specs=pl.BlockSpec((tm, D), lambda i: (i, 0)),
        out_shape=jax.ShapeDtypeStruct((S, D), BF16), compiler_params=_cp(1))(x, g, *deps)


def _mm_some_blocks_call(a, wg, blocks, out_dtype, name, out_so_far=None):
    S, K = a.shape
    nb, _, bn = wg.shape
    tm = _tile(S, 1024, 16)

    def body(b_ref, a_ref, w_ref, *rest):
        rest[-1][...] = jnp.dot(a_ref[...], w_ref[...], preferred_element_type=F32).astype(out_dtype)

    in_specs = [pl.BlockSpec((tm, K), lambda i, j, b: (i, 0)), pl.BlockSpec((None, K, bn), lambda i, j, b: (b[j], 0, 0))]
    operands = [a, wg]
    aliases = {}
    if out_so_far is not None:
        in_specs.append(pl.BlockSpec(memory_space=pl.ANY))
        operands.append(out_so_far)
        aliases = {3: 0}
    grid_spec = pltpu.PrefetchScalarGridSpec(
        num_scalar_prefetch=1, grid=(S // tm, blocks.shape[0]), in_specs=in_specs,
        out_specs=pl.BlockSpec((tm, bn), lambda i, j, b: (i, b[j])))
    return pl.pallas_call(body, name=name, grid_spec=grid_spec,
                          out_shape=jax.ShapeDtypeStruct((S, nb * bn), out_dtype), input_output_aliases=aliases,
                          compiler_params=_cp(2))(blocks, *operands)


def _mm_out_call(x, y, w, name):
    S, D = x.shape
    tm = _tile(S, 512, 16)

    def body(x_ref, y_ref, w_ref, o_ref):
        o_ref[...] = x_ref[...] + jnp.dot(y_ref[...], w_ref[...], preferred_element_type=F32)

    return pl.pallas_call(
        body, name=name, grid=(S // tm,),
        in_specs=[pl.BlockSpec((tm, D), lambda i: (i, 0)), pl.BlockSpec((tm, D), lambda i: (i, 0)),
                  pl.BlockSpec((D, D), lambda i: (0, 0))],
        out_specs=pl.BlockSpec((tm, D), lambda i: (i, 0)),
        out_shape=jax.ShapeDtypeStruct((S, D), F32), compiler_params=_cp(1))(x, y, w)


def _mm_nt_call(a, w, out_dtype, name, deps=()):
    S, K = a.shape
    N = w.shape[0]
    tm = _tile(S, 1024, 16)
    tn = _tile(N, 768, LANES)

    def body(a_ref, w_ref, o_ref):
        o_ref[...] = lax.dot_general(a_ref[...], w_ref[...], _DN_NT, preferred_element_type=F32).astype(out_dtype)

    return _pallas(
        body, 2, deps, name=name, grid=(S // tm, N // tn),
        in_specs=[pl.BlockSpec((tm, K), lambda i, j: (i, 0)), pl.BlockSpec((tn, K), lambda i, j: (j, 0))],
        out_specs=pl.BlockSpec((tm, tn), lambda i, j: (i, j)),
        out_shape=jax.ShapeDtypeStruct((S, N), out_dtype), compiler_params=_cp(2))(a, w, *deps)


def _mm_down_loss_call(x2, f, w, final_g, target, name):
    S, D = x2.shape
    Fd = f.shape[1]
    tm = _tile(S, 512, 16)
    tk = _tile(Fd, 768, LANES)
    nk = Fd // tk

    def body(x_ref, f_ref, w_ref, g_ref, t_ref, dx_ref, dxb_ref, loss_ref, dg_ref, acc):
        i, k = pl.program_id(0), pl.program_id(1)

        @pl.when(jnp.logical_and(i == 0, k == 0))
        def _():
            loss_ref[...] = jnp.zeros_like(loss_ref)
            dg_ref[...] = jnp.zeros_like(dg_ref)

        @pl.when(k == 0)
        def _():
            acc[...] = jnp.zeros_like(acc)

        acc[...] += jnp.dot(f_ref[...], w_ref[...], preferred_element_type=F32)

        @pl.when(k == nk - 1)
        def _():
            x3 = x_ref[...] + acc[...]
            r = lax.rsqrt(_rows_mean(x3 * x3) + RMS_EPS)
            g = g_ref[...]
            xn = x3 * r
            diff = xn * g - t_ref[...]
            loss_ref[...] += 0.5 * jnp.sum(_rows_mean(diff * diff))
            dout = diff * (1.0 / D)
            dg_ref[...] += _col_sum(dout * xn)
            dyg = dout * g
            dx = r * (dyg - xn * _rows_mean(dyg * xn))
            dx_ref[...] = dx
            dxb_ref[...] = dx.astype(BF16)

    row = lambda i, k: (i, 0)
    return pl.pallas_call(
        body, name=name, grid=(S // tm, nk),
        in_specs=[pl.BlockSpec((tm, D), row), pl.BlockSpec((tm, tk), lambda i, k: (i, k)),
                  pl.BlockSpec((tk, D), lambda i, k: (k, 0)), pl.BlockSpec((1, D), lambda i, k: (0, 0)),
                  pl.BlockSpec((tm, D), row)],
        out_specs=[pl.BlockSpec((tm, D), row), pl.BlockSpec((tm, D), row),
                   pl.BlockSpec((8, LANES), lambda i, k: (0, 0)), pl.BlockSpec((1, D), lambda i, k: (0, 0))],
        out_shape=[jax.ShapeDtypeStruct((S, D), F32), jax.ShapeDtypeStruct((S, D), BF16),
                   jax.ShapeDtypeStruct((8, LANES), F32), jax.ShapeDtypeStruct((1, D), F32)],
        scratch_shapes=[pltpu.VMEM((tm, D), F32)], compiler_params=_cp(2, 56))(x2, f, w, final_g, target)


def _mm_dx_norm_call(a3, wg, resid, xin, g, name, deps=()):
    na, S, Fa = a3.shape
    nb, D, bn = wg.shape
    tm = _tile(S, 512, 16)
    tk = _tile(bn, 1536, LANES)
    nsub = bn // tk
    nka = Fa // tk
    nk = nb * nsub
    assert na * nka == nk

    def body(a_ref, w_ref, r_ref, x_ref, g_ref, dx_ref, dxb_ref, dg_ref, acc):
        i, k = pl.program_id(0), pl.program_id(1)

        @pl.when(jnp.logical_and(i == 0, k == 0))
        def _():
            dg_ref[...] = jnp.zeros_like(dg_ref)

        @pl.when(k == 0)
        def _():
            acc[...] = jnp.zeros_like(acc)

        acc[...] += lax.dot_general(a_ref[...], w_ref[...], _DN_NT, preferred_element_type=F32)

        @pl.when(k == nk - 1)
        def _():
            dh = acc[...]
            xv = x_ref[...]
            r = lax.rsqrt(_rows_mean(xv * xv) + RMS_EPS)
            xn = xv * r
            dg_ref[...] += _col_sum(dh * xn)
            dyg = dh * g_ref[...]
            dx = r_ref[...] + r * (dyg - xn * _rows_mean(dyg * xn))
            dx_ref[...] = dx
            dxb_ref[...] = dx.astype(BF16)

    row = lambda i, k: (i, 0)
    return _pallas(
        body, 5, deps, name=name, grid=(S // tm, nk),
        in_specs=[pl.BlockSpec((None, tm, tk), lambda i, k: (k // nka, i, k % nka)),
                  pl.BlockSpec((None, D, tk), lambda i, k: (k // nsub, 0, k % nsub)),
                  pl.BlockSpec((tm, D), row, pipeline_mode=pl.Buffered(1)),
                  pl.BlockSpec((tm, D), row, pipeline_mode=pl.Buffered(1)), pl.BlockSpec((1, D), lambda i, k: (0, 0))],
        out_specs=[pl.BlockSpec((tm, D), row), pl.BlockSpec((tm, D), row), pl.BlockSpec((1, D), lambda i, k: (0, 0))],
        out_shape=[jax.ShapeDtypeStruct((S, D), F32), jax.ShapeDtypeStruct((S, D), BF16),
                   jax.ShapeDtypeStruct((1, D), F32)],
        scratch_shapes=[pltpu.VMEM((tm, D), F32)], compiler_params=_cp(2, 56))(a3, wg, resid, xin, g, *deps)


def _unblock_call(wg, name):
    nb, K, bn = wg.shape

    def body(w_ref, o_ref):
        o_ref[...] = w_ref[...]

    return pl.pallas_call(
        body, name=name, grid=(nb,),
        in_specs=[pl.BlockSpec((None, K, bn), lambda o: (o, 0, 0))],
        out_specs=pl.BlockSpec((K, bn), lambda o: (0, o)),
        out_shape=jax.ShapeDtypeStruct((K, nb * bn), wg.dtype), compiler_params=_cp(1))(wg)


def _mm_nt_norm_call(a, w, resid, xin, g, name, deps=(), part=(0, 1), dx_so_far=None):
    S, K = a.shape
    D = w.shape[0]
    tm = _tile(S, 256, 16)
    tiles = (S // tm) // part[1]
    first = part[0] * tiles

    def body(a_ref, w_ref, r_ref, x_ref, g_ref, *rest):
        dx_ref, dg_ref = rest[-2:]

        @pl.when(pl.program_id(0) == 0)
        def _():
            dg_ref[...] = jnp.zeros_like(dg_ref)

        dh = lax.dot_general(a_ref[...], w_ref[...], _DN_NT, preferred_element_type=F32)
        xv = x_ref[...]
        r = lax.rsqrt(_rows_mean(xv * xv) + RMS_EPS)
        xn = xv * r
        dg_ref[...] += _col_sum(dh * xn)
        dyg = dh * g_ref[...]
        dx_ref[...] = r_ref[...] + r * (dyg - xn * _rows_mean(dyg * xn))

    row = lambda i: (i + first, 0)
    fixed = lambda i: (0, 0)
    in_specs = [pl.BlockSpec((tm, K), row), pl.BlockSpec((D, K), fixed, pipeline_mode=pl.Buffered(1)),
                pl.BlockSpec((tm, D), row), pl.BlockSpec((tm, D), row), pl.BlockSpec((1, D), fixed)]
    operands = [a, w, resid, xin, g]
    aliases = {}
    if dx_so_far is not None:
        in_specs.append(pl.BlockSpec(memory_space=pl.ANY))
        operands.append(dx_so_far)
        aliases = {5: 0}
    return _pallas(
        body, len(operands), deps, name=name, grid=(tiles,),
        in_specs=in_specs, out_specs=[pl.BlockSpec((tm, D), row), pl.BlockSpec((1, D), fixed)],
        out_shape=[jax.ShapeDtypeStruct((S, D), F32), jax.ShapeDtypeStruct((1, D), F32)],
        input_output_aliases=aliases, compiler_params=_cp(1, 56))(*operands, *deps)


def _mm_tn_cols_call(a, b3, nb, bn, name, deps=()):
    S, Ka = a.shape
    nh, _, Fb = b3.shape
    tm = _tile(S, 2048, 16)
    tn = _tile(bn, 768, LANES)
    nsub = bn // tn
    njb = Fb // tn
    J = nb * nsub
    assert nh * njb == J

    def body(a_ref, b_ref, o_ref):
        @pl.when(pl.program_id(1) == 0)
        def _():
            o_ref[...] = jnp.zeros_like(o_ref)

        o_ref[...] += lax.dot_general(a_ref[...], b_ref[...], _DN_TN, preferred_element_type=F32)

    return _pallas(
        body, 2, deps, name=name, grid=(J, S // tm),
        in_specs=[pl.BlockSpec((tm, Ka), lambda j, i: (i, 0)),
                  pl.BlockSpec((None, tm, tn), lambda j, i: (j // njb, i, j % njb))],
        out_specs=pl.BlockSpec((None, Ka, tn), lambda j, i: (j // nsub, 0, j % nsub)),
        out_shape=jax.ShapeDtypeStruct((nb, Ka, bn), F32), compiler_params=_cp(2, 56))(a, b3, *deps)


def _mm_tn_rows_call(a, b, name, deps=()):
    S, E = a.shape
    D = b.shape[1]
    tm = _tile(S, 2048, 16)
    te = _tile(E, 768, LANES)

    def body(a_ref, b_ref, o_ref):
        @pl.when(pl.program_id(1) == 0)
        def _():
            o_ref[...] = jnp.zeros_like(o_ref)

        o_ref[...] += lax.dot_general(a_ref[...], b_ref[...], _DN_TN, preferred_element_type=F32)

    return _pallas(
        body, 2, deps, name=name, grid=(E // te, S // tm),
        in_specs=[pl.BlockSpec((tm, te), lambda j, i: (i, j)), pl.BlockSpec((tm, D), lambda j, i: (i, 0))],
        out_specs=pl.BlockSpec((te, D), lambda j, i: (j, 0)),
        out_shape=jax.ShapeDtypeStruct((E, D), F32), compiler_params=_cp(2, 56))(a, b, *deps)


def _ffn_tiles(S, Fd):
    return _tile(S, 512, 16), _tile(Fd // (N_DEV // 2), 1536, LANES)


def _taps(cw_ref):
    return [cw_ref[k:k + 1, :] for k in range(cw_ref.shape[0])]


def _conv3(prev8, cur, taps):
    s1 = _shift_down(prev8, cur, 1)
    s2 = _shift_down(prev8, cur, 2)
    return taps[2] * cur + taps[1] * s1 + taps[0] * s2, s1, s2


GATE_LANES = 256
SUB_LANES = 1536


def _lane_taps(cw_ref, ls):
    return [cw_ref[k:k + 1, ls] for k in range(cw_ref.shape[0])]


def _ffn_up_act_call(h2, wg, cw, cb, name, deps=()):
    S, D = h2.shape
    nb, _, bn = wg.shape
    Fd = nb * bn // 2
    tm, tc = _ffn_tiles(S, Fd)
    nk = Fd // tc
    hb = tm // 16
    nsubw = bn // tc
    half = nb // 2
    sc = _tile(tc, SUB_LANES, LANES)

    def body(a_ref, ap_ref, wgate_ref, wval_ref, cwg_ref, cwv_ref, cbg_ref, cbv_ref, up_ref, upc_ref, f_ref):
        keep = jnp.where(pl.program_id(0) == 0, 0.0, 1.0)
        a_ext = jnp.concatenate([ap_ref[...], a_ref[...]], axis=0)
        nsub = tc // sc
        lanes = [slice(s * sc, (s + 1) * sc) for s in range(nsub)]

        def products(s):
            return [jnp.dot(a_ext, w_ref[:, lanes[s]], preferred_element_type=F32)
                    for w_ref in (wgate_ref, wval_ref)]

        ready = products(0)
        for s in range(nsub):
            ls = lanes[s]
            following = products(s + 1) if s + 1 < nsub else None

            def conv_half(u, cw_ref, cb_ref, slab):
                up_ref[slab, :, ls] = u[16:].astype(BF16)
                conv, _, _ = _conv3(u[8:16] * keep, u[16:], _lane_taps(cw_ref, ls))
                c = conv + cb_ref[:, ls]
                upc_ref[slab, :, ls] = c.astype(BF16)
                return c

            cg = conv_half(ready[0], cwg_ref, cbg_ref, 0)
            cv = conv_half(ready[1], cwv_ref, cbv_ref, 1)
            f_ref[:, ls] = (_gelu(cg) * cv).astype(BF16)
            ready = following

    return _pallas(
        body, 8, deps, name=name, grid=(S // tm, nk),
        in_specs=[pl.BlockSpec((tm, D), lambda i, k: (i, 0)),
                  pl.BlockSpec((16, D), lambda i, k: (jnp.maximum(i * hb - 1, 0), 0)),
                  pl.BlockSpec((None, D, tc), lambda i, k: (k // nsubw, 0, k % nsubw)),
                  pl.BlockSpec((None, D, tc), lambda i, k: (half + k // nsubw, 0, k % nsubw)),
                  pl.BlockSpec((3, tc), lambda i, k: (0, k)), pl.BlockSpec((3, tc), lambda i, k: (0, k + nk)),
                  pl.BlockSpec((1, tc), lambda i, k: (0, k)), pl.BlockSpec((1, tc), lambda i, k: (0, k + nk))],
        out_specs=[pl.BlockSpec((2, tm, tc), lambda i, k: (0, i, k)), pl.BlockSpec((2, tm, tc), lambda i, k: (0, i, k)),
                   pl.BlockSpec((tm, tc), lambda i, k: (i, k))],
        out_shape=[jax.ShapeDtypeStruct((2, S, Fd), BF16), jax.ShapeDtypeStruct((2, S, Fd), BF16),
                   jax.ShapeDtypeStruct((S, Fd), BF16)],
        compiler_params=_cp(2, 56))(h2, h2, wg, wg, cw, cw, cb, cb, *deps)


def _ffn_down_dx_act_bwd_call(dxb, w, up3, upc3, cw, name, deps=()):
    S, D = dxb.shape
    _, _, Fd = up3.shape
    tm, tc = _ffn_tiles(S, Fd)
    nj = Fd // tc
    ni = S // tm
    hb = tm // 16
    sc = _tile(tc, SUB_LANES, LANES)

    def body(a_ref, an_ref, w_ref, g_ref, v_ref, cg_ref, cv_ref, cgn_ref, cvn_ref, cwg_ref, cwv_ref,
             dup_ref, dcwg_ref, dcwv_ref, dcbg_ref, dcbv_ref):
        i = pl.program_id(1)
        keep_next = jnp.where(i == ni - 1, 0.0, 1.0)

        @pl.when(i == 0)
        def _():
            for r in (dcwg_ref, dcwv_ref, dcbg_ref, dcbv_ref):
                r[...] = jnp.zeros_like(r)

        after = (an_ref[...].astype(F32) * keep_next).astype(BF16)
        a_ext = jnp.concatenate([a_ref[...], after], axis=0)
        for s in range(tc // sc):
            ls = slice(s * sc, (s + 1) * sc)
            df_ext = lax.dot_general(a_ext, w_ref[s * sc:(s + 1) * sc, :], _DN_NT, preferred_element_type=F32)
            df = df_ext[:tm + 8]
            cg = jnp.concatenate([cg_ref[:, ls].astype(F32), cgn_ref[:, ls].astype(F32)[:8]], axis=0)
            cv = jnp.concatenate([cv_ref[:, ls].astype(F32), cvn_ref[:, ls].astype(F32)[:8]], axis=0)
            gel, dgel = _gelu_parts(cg)

            def back(d, cw_ref, x_ref, dcw_ref, dcb_ref, slab):
                taps = _lane_taps(cw_ref, ls)
                d0 = d[:tm]
                d1 = pltpu.roll(d, tm + 8 - 1, 0)[:tm]
                d2 = pltpu.roll(d, tm + 8 - 2, 0)[:tm]
                dup_ref[slab, :, ls] = (taps[2] * d0 + taps[1] * d1 + taps[0] * d2).astype(BF16)
                xv = x_ref[:, ls].astype(F32)
                dcw_ref[2:3, ls] += _col_sum(xv * d0)
                dcw_ref[1:2, ls] += _col_sum(xv * d1)
                dcw_ref[0:1, ls] += _col_sum(xv * d2)
                dcb_ref[:, ls] += _col_sum(d0)

            back(df * cv * dgel, cwg_ref, g_ref, dcwg_ref, dcbg_ref, 0)
            back(df * gel, cwv_ref, v_ref, dcwv_ref, dcbv_ref, 1)

    nxt = lambda j, i: jnp.minimum((i + 1) * hb, S // 16 - 1)
    main = lambda s: pl.BlockSpec((None, tm, tc), lambda j, i: (s, i, j))
    halo = lambda s: pl.BlockSpec((None, 16, tc), lambda j, i: (s, nxt(j, i), j))
    acc3 = pl.BlockSpec((3, tc), lambda j, i: (0, j))
    acc1 = pl.BlockSpec((1, tc), lambda j, i: (0, j))
    return _pallas(
        body, 11, deps, name=name, grid=(nj, ni),
        in_specs=[pl.BlockSpec((tm, D), lambda j, i: (i, 0)), pl.BlockSpec((16, D), lambda j, i: (nxt(j, i), 0)),
                  pl.BlockSpec((tc, D), lambda j, i: (j, 0)),
                  main(0), main(1), main(0), main(1), halo(0), halo(1),
                  pl.BlockSpec((3, tc), lambda j, i: (0, j)), pl.BlockSpec((3, tc), lambda j, i: (0, j + nj))],
        out_specs=[pl.BlockSpec((2, tm, tc), lambda j, i: (0, i, j)), acc3, acc3, acc1, acc1],
        out_shape=[jax.ShapeDtypeStruct((2, S, Fd), BF16), jax.ShapeDtypeStruct((3, Fd), F32),
                   jax.ShapeDtypeStruct((3, Fd), F32), jax.ShapeDtypeStruct((1, Fd), F32),
                   jax.ShapeDtypeStruct((1, Fd), F32)],
        compiler_params=_cp(2, 56))(dxb, dxb, w, up3, up3, upc3, upc3, upc3, upc3, cw, cw, *deps)


def _gm_forward_tile(pv, vg, vb, ws_ref, bsb_ref, mbuf, H, nc):
    W = H * CHUNK
    z, dz = _gelu_parts(pv)
    u, v0 = z[:, :W], z[:, W:]
    xc = v0 - _rows_mean(v0)
    rs = lax.rsqrt(_rows_mean(xc * xc) + LN_EPS)
    vh = xc * rs
    vnb = (vh * vg + vb).astype(BF16)
    mask = lax.broadcasted_iota(jnp.int32, (CHUNK, CHUNK), 0) >= lax.broadcasted_iota(jnp.int32, (CHUNK, CHUNK), 1)
    for h in range(H):
        cs = slice(h * CHUNK, (h + 1) * CHUNK)
        wm = jnp.where(mask, ws_ref[h], 0.0).astype(BF16)
        vcat = jnp.concatenate([vnb[c * CHUNK:(c + 1) * CHUNK, cs] for c in range(nc)], axis=1)
        mix = jnp.dot(wm, vcat, preferred_element_type=F32)
        for c in range(nc):
            mbuf[c * CHUNK:(c + 1) * CHUNK, cs] = mix[:, c * CHUNK:(c + 1) * CHUNK] + bsb_ref[h]
    return dz, u, vh, rs, vnb, mask


def _gm_fwd_call(p, v_g, v_b, ws, bsb, out_g, name, deps=()):
    S = p.shape[0]
    H = ws.shape[0]
    W = H * CHUNK
    tm = _tile(S, 256, CHUNK)
    nc = tm // CHUNK

    def body(p_ref, vg_ref, vb_ref, ws_ref, bsb_ref, og_ref, y_ref, mbuf):
        _, u, _, _, _, _ = _gm_forward_tile(p_ref[...], vg_ref[...], vb_ref[...], ws_ref, bsb_ref, mbuf, H, nc)
        yg = u * mbuf[...]
        r = lax.rsqrt(_rows_mean(yg * yg) + RMS_EPS)
        y_ref[...] = (yg * r * og_ref[...]).astype(BF16)

    vec = pl.BlockSpec((1, W), lambda i: (0, 0))
    mat = pl.BlockSpec((H, CHUNK, CHUNK), lambda i: (0, 0, 0))
    return _pallas(
        body, 6, deps, name=name, grid=(S // tm,),
        in_specs=[pl.BlockSpec((tm, 2 * W), lambda i: (i, 0)), vec, vec, mat, mat, vec],
        out_specs=pl.BlockSpec((tm, W), lambda i: (i, 0)),
        out_shape=jax.ShapeDtypeStruct((S, 2 * W), BF16),
        scratch_shapes=[pltpu.VMEM((tm, W), F32)], compiler_params=_cp(1))(p, v_g, v_b, ws, bsb, out_g, *deps)


def _gm_bwd_call(p, d_y, v_g, v_b, ws, bsb, out_g, name, deps=()):
    S = p.shape[0]
    H = ws.shape[0]
    W = H * CHUNK
    tm = _tile(S, 256, CHUNK)
    nc = tm // CHUNK
    ni = S // tm

    def body(p_ref, dy_ref, vg_ref, vb_ref, ws_ref, bsb_ref, og_ref,
             dp_ref, dvg_ref, dvb_ref, dws_ref, dbs_ref, dog_ref, mbuf, dvbuf):
        i = pl.program_id(0)

        @pl.when(i == 0)
        def _():
            for r in (dvg_ref, dvb_ref, dws_ref, dbs_ref, dog_ref):
                r[...] = jnp.zeros_like(r)

        vg = vg_ref[...]
        dz, u, vh, rs, vnb, mask = _gm_forward_tile(p_ref[...], vg, vb_ref[...], ws_ref, bsb_ref, mbuf, H, nc)
        mixed = mbuf[...]
        yg = u * mixed
        r = lax.rsqrt(_rows_mean(yg * yg) + RMS_EPS)
        yn = yg * r
        dya = dy_ref[...]
        dog_ref[...] += _col_sum(dya * yn)
        dyg = dya * og_ref[...]
        dygm = r * (dyg - yn * _rows_mean(dyg * yn))
        du = dygm * mixed
        dmix = dygm * u
        dmb = dmix.astype(BF16)
        for h in range(H):
            cs = slice(h * CHUNK, (h + 1) * CHUNK)
            wm = jnp.where(mask, ws_ref[h], 0.0).astype(BF16)
            dcat = jnp.concatenate([dmb[c * CHUNK:(c + 1) * CHUNK, cs] for c in range(nc)], axis=1)
            vcat = jnp.concatenate([vnb[c * CHUNK:(c + 1) * CHUNK, cs] for c in range(nc)], axis=1)
            dvn = lax.dot_general(wm, dcat, _DN_TN, preferred_element_type=F32)
            dws_ref[h] += jnp.where(mask, lax.dot_general(dcat, vcat, _DN_NT, preferred_element_type=F32), 0.0)
            dbs = dmix[0:CHUNK, cs]
            for c in range(1, nc):
                dbs = dbs + dmix[c * CHUNK:(c + 1) * CHUNK, cs]
            dbs_ref[h] += dbs
            for c in range(nc):
                dvbuf[c * CHUNK:(c + 1) * CHUNK, cs] = dvn[:, c * CHUNK:(c + 1) * CHUNK]
        dvn_all = dvbuf[...]
        dvg_ref[...] += _col_sum(dvn_all * vh)
        dvb_ref[...] += _col_sum(dvn_all)
        dvh = dvn_all * vg
        dv0 = rs * (dvh - _rows_mean(dvh) - vh * _rows_mean(dvh * vh))
        dp_ref[...] = (jnp.concatenate([du, dv0], axis=1) * dz).astype(BF16)

        @pl.when(i == ni - 1)
        def _():
            for h in range(H):
                dbs_ref[h] = jnp.broadcast_to(jnp.sum(dbs_ref[h], axis=1, keepdims=True), (CHUNK, CHUNK))

    vec = pl.BlockSpec((1, W), lambda i: (0, 0))
    mat = pl.BlockSpec((H, CHUNK, CHUNK), lambda i: (0, 0, 0))
    vshape = jax.ShapeDtypeStruct((1, W), F32)
    mshape = jax.ShapeDtypeStruct((H, CHUNK, CHUNK), F32)
    return _pallas(
        body, 7, deps, name=name, grid=(ni,),
        in_specs=[pl.BlockSpec((tm, 2 * W), lambda i: (i, 0)), pl.BlockSpec((tm, W), lambda i: (i, 0)),
                  vec, vec, mat, mat, vec],
        out_specs=[pl.BlockSpec((tm, 2 * W), lambda i: (i, 0)), vec, vec, mat, mat, vec],
        out_shape=[jax.ShapeDtypeStruct((S, 4 * W), BF16), vshape, vshape, mshape, mshape, vshape],
        scratch_shapes=[pltpu.VMEM((tm, W), F32), pltpu.VMEM((tm, W), F32)],
        compiler_params=_cp(1))(p, d_y, v_g, v_b, ws, bsb, out_g, *deps)


def _lru_gates(prev8, xl, cw, cb, wa_ref, ba, wx_ref, bx, lam, H):
    sh = [_shift_down(prev8, xl, k) for k in range(4)]
    xr = cw[3] * sh[0] + cw[2] * sh[1] + cw[1] * sh[2] + cw[0] * sh[3] + cb
    xrb = xr.astype(BF16)
    rp, ip = [], []
    for h in range(H):
        cs = slice(h * CHUNK, (h + 1) * CHUNK)
        rp.append(jnp.dot(xrb[:, cs], wa_ref[h].astype(BF16), preferred_element_type=F32))
        ip.append(jnp.dot(xrb[:, cs], wx_ref[h].astype(BF16), preferred_element_type=F32))
    r = _sigmoid(jnp.concatenate(rp, axis=1) + ba)
    ig = _sigmoid(jnp.concatenate(ip, axis=1) + bx)
    sp = _softplus(-lam)
    t = jnp.tanh((-LRU_C) * r * sp)
    q = lax.rsqrt(1.0 - t)
    a = jnp.sqrt(1.0 + t) * q
    mult = jnp.sqrt(-2.0 * t) * q
    a2_over_mult = (1.0 + t) * q * lax.rsqrt(-2.0 * t)
    return xr, xrb, r, ig, sp, a, mult, a2_over_mult, sh


def _lru_fwd_call(p, cw, cb, wa, ba, wx, bx, lam, out_g, y_half, name):
    S = p.shape[0]
    H = wa.shape[0]
    W = H * CHUNK
    tm = _tile(S, 256, 16)
    ng = tm // 8

    def body(pg_ref, px_ref, cw_ref, cb_ref, wa_ref, ba_ref, wx_ref, bx_ref, lam_ref, og_ref, y_in_ref,
             y_ref, h_ref, saved_ref, xprev, hcar, abuf, bbuf):
        @pl.when(pl.program_id(0) == 0)
        def _():
            xprev[...] = jnp.zeros_like(xprev)
            hcar[...] = jnp.zeros_like(hcar)

        xl = px_ref[...]
        xr, _, r_gate, ig, _, a, mult, a2m, _ = _lru_gates(xprev[...], xl, _taps(cw_ref), cb_ref[...], wa_ref,
                                                           ba_ref[...], wx_ref, bx_ref[...], lam_ref[...], H)
        for k, val in enumerate((xr, r_gate, ig, a, mult, a2m)):
            saved_ref[k] = val
        xprev[...] = xl[tm - 8:]
        b = mult * (ig * xr)
        sub = lax.broadcasted_iota(jnp.int32, (tm, W), 0) & 7
        for d in (1, 2, 4):
            m = sub >= d
            a_s = jnp.where(m, pltpu.roll(a, d, 0), 1.0)
            b_s = jnp.where(m, pltpu.roll(b, d, 0), 0.0)
            b = a * b_s + b
            a = a * a_s
        abuf[...] = a
        bbuf[...] = b

        def step(g, carry):
            r0 = pl.multiple_of(g * 8, 8)
            h_ref[pl.ds(r0, 8), :] = abuf[pl.ds(r0, 8), :] * carry + bbuf[pl.ds(r0, 8), :]
            return jnp.broadcast_to(h_ref[pl.ds(r0 + 7, 1), :], (8, W))

        hcar[...] = lax.fori_loop(0, ng, step, hcar[...])
        yl = h_ref[...] * _gelu(pg_ref[...])
        r = lax.rsqrt(_rows_mean(yl * yl) + RMS_EPS)
        y_ref[...] = (yl * r * og_ref[...]).astype(BF16)

    vec = pl.BlockSpec((1, W), lambda i: (0, 0))
    mat = pl.BlockSpec((H, CHUNK, CHUNK), lambda i: (0, 0, 0))
    return pl.pallas_call(
        body, name=name, grid=(S // tm,),
        in_specs=[pl.BlockSpec((tm, W), lambda i: (i, 2)), pl.BlockSpec((tm, W), lambda i: (i, 3)),
                  pl.BlockSpec((4, W), lambda i: (0, 0)), vec, mat, vec, mat, vec, vec, vec,
                  pl.BlockSpec(memory_space=pl.ANY)],
        out_specs=[pl.BlockSpec((tm, W), lambda i: (i, 1)), pl.BlockSpec((tm, W), lambda i: (i, 0)),
                   pl.BlockSpec((6, tm, W), lambda i: (0, i, 0))],
        out_shape=[jax.ShapeDtypeStruct((S, 2 * W), BF16), jax.ShapeDtypeStruct((S, W), F32),
                   jax.ShapeDtypeStruct((6, S, W), F32)],
        input_output_aliases={10: 0},
        scratch_shapes=[pltpu.VMEM((8, W), F32), pltpu.VMEM((8, W), F32), pltpu.VMEM((tm, W), F32),
                        pltpu.VMEM((tm, W), F32)],
        compiler_params=_cp(1))(p, p, cw, cb, wa, ba, wx, bx, lam, out_g, y_half)


def _lru_bwd_call(p, hs, saved, d_y, cw, wa, wx, lam, out_g, dp_half, name):
    S = p.shape[0]
    H = wa.shape[0]
    W = H * CHUNK
    tm = _tile(S, 256, 16)
    ng = tm // 8
    ni = S // tm
    hb = tm // 8

    def body(pg_ref, px_ref, saved_ref, h_ref, hp_ref, dy_ref, cw_ref, wa_ref, wx_ref, lam_ref, og_ref, dp_in_ref,
             dp_ref, dcw_ref, dcb_ref, dwa_ref, dba_ref, dwx_ref, dbx_ref, dlam_ref, dog_ref,
             a_next, e_next, dxr_next, abuf, bbuf, ebuf, dxr0, dprb_buf, dpib_buf, sums):
        i = pl.program_id(0)
        ri = ni - 1 - i

        @pl.when(i == 0)
        def _():
            for r in (dcw_ref, dcb_ref, dwa_ref, dba_ref, dwx_ref, dbx_ref, dlam_ref, dog_ref,
                      a_next, e_next, dxr_next):
                r[...] = jnp.zeros_like(r)

        keep_prev = jnp.where(ri == 0, 0.0, 1.0)
        cw_ = _taps(cw_ref)
        lam_ = lam_ref[...]
        xl = px_ref[...]
        a = saved_ref[3]
        sp = _softplus(-lam_)
        sums[...] = jnp.zeros_like(sums)
        og = og_ref[...]
        lane_groups = [slice(q * GATE_LANES, (q + 1) * GATE_LANES) for q in range(W // GATE_LANES)]

        def fold(v):
            return sum(v[:, k * LANES:(k + 1) * LANES] for k in range(GATE_LANES // LANES))

        def out_grads(g, carry):
            rows = pl.ds(pl.multiple_of(g * 16, 16), 16)
            s_yy = jnp.zeros((16, LANES), F32)
            s_dy = jnp.zeros((16, LANES), F32)
            for ls in lane_groups:
                yl = h_ref[rows, ls] * _gelu(pg_ref[rows, ls])
                s_yy = s_yy + fold(yl * yl)
                s_dy = s_dy + fold(dy_ref[rows, ls] * og[:, ls] * yl)
            rr = lax.rsqrt(jnp.sum(s_yy, axis=1, keepdims=True) * (1.0 / W) + RMS_EPS)
            c = rr * rr * jnp.sum(s_dy, axis=1, keepdims=True) * (1.0 / W)
            for ls in lane_groups:
                hq = h_ref[rows, ls]
                gg, dgg = _gelu_parts(pg_ref[rows, ls])
                yl = hq * gg
                dyb = dy_ref[rows, ls]
                sums[3, :, ls] += dyb * (yl * rr)
                dyl = rr * (dyb * og[:, ls] - yl * c)
                bbuf[rows, ls] = dyl * gg
                dp_ref[rows, ls] = (dyl * hq * dgg).astype(BF16)
            return carry

        lax.fori_loop(0, tm // 16, out_grads, 0, unroll=2)
        dog_ref[...] += _col_sum(sums[3])

        an = _shift_up(a, a_next[...], 1)
        eb = bbuf[...]
        sub = lax.broadcasted_iota(jnp.int32, (tm, W), 0) & 7
        for d in (1, 2, 4):
            m = sub < 8 - d
            a_s = jnp.where(m, pltpu.roll(an, tm - d, 0), 1.0)
            e_s = jnp.where(m, pltpu.roll(eb, tm - d, 0), 0.0)
            eb = an * e_s + eb
            an = an * a_s
        abuf[...] = an
        bbuf[...] = eb

        def step(g, carry):
            r0 = pl.multiple_of((ng - 1 - g) * 8, 8)
            ebuf[pl.ds(r0, 8), :] = abuf[pl.ds(r0, 8), :] * carry + bbuf[pl.ds(r0, 8), :]
            return jnp.broadcast_to(ebuf[pl.ds(r0, 1), :], (8, W))

        lax.fori_loop(0, ng, step, jnp.broadcast_to(e_next[0:1, :], (8, W)))
        a_next[...] = a[0:8]
        e_next[...] = ebuf[0:8, :]

        row16 = lax.broadcasted_iota(jnp.int32, (16, GATE_LANES), 0)
        h_before = hp_ref[7:8, :] * keep_prev
        r_scale = (-LRU_C) * sp

        def gate_grads(g, carry):
            r0 = pl.multiple_of(g * 16, 16)
            above = jnp.maximum(r0 - 1, 0)
            for q in range(W // GATE_LANES):
                ls = slice(q * GATE_LANES, (q + 1) * GATE_LANES)
                e = ebuf[pl.ds(r0, 16), ls]
                h_prev_row = jnp.where(g == 0, h_before[:, ls], h_ref[pl.ds(above, 1), ls])
                hm1 = jnp.where(row16 == 0, h_prev_row, pltpu.roll(h_ref[pl.ds(r0, 16), ls], 1, 0))
                xr_, r_, ig_, a_, mult_, a2m_ = [saved_ref[k, pl.ds(r0, 16), ls] for k in range(6)]
                em = e * mult_
                dxr0[pl.ds(r0, 16), ls] = em * ig_
                dla = e * hm1 * a_ - e * ig_ * xr_ * a2m_
                dpr = dla * r_scale[:, ls] * r_ * (1.0 - r_)
                dpi = em * xr_ * ig_ * (1.0 - ig_)
                sums[0, :, ls] += dla * r_
                sums[1, :, ls] += dpr
                sums[2, :, ls] += dpi
                dprb_buf[pl.ds(r0, 16), ls] = dpr.astype(BF16)
                dpib_buf[pl.ds(r0, 16), ls] = dpi.astype(BF16)
            return carry

        lax.fori_loop(0, tm // 16, gate_grads, 0)
        dlam_ref[...] += (-LRU_C) * _col_sum(sums[0])
        dba_ref[...] += _col_sum(sums[1])
        dbx_ref[...] += _col_sum(sums[2])
        dprb = dprb_buf[...]
        dpib = dpib_buf[...]
        dxr = dxr0[...]
        xrb = saved_ref[0].astype(BF16)
        back = []
        for h in range(H):
            cs = slice(h * CHUNK, (h + 1) * CHUNK)
            wab = wa_ref[h].astype(BF16)
            wxb = wx_ref[h].astype(BF16)
            back.append(lax.dot_general(dprb[:, cs], wab, _DN_NT, preferred_element_type=F32)
                        + lax.dot_general(dpib[:, cs], wxb, _DN_NT, preferred_element_type=F32))
            dwa_ref[h] += lax.dot_general(xrb[:, cs], dprb[:, cs], _DN_TN, preferred_element_type=F32)
            dwx_ref[h] += lax.dot_general(xrb[:, cs], dpib[:, cs], _DN_TN, preferred_element_type=F32)
        dxr = dxr + jnp.concatenate(back, axis=1)

        nxt = dxr_next[...]
        ahead = [_shift_up(dxr, nxt, j) for j in range(4)]
        dxl = cw_[3] * ahead[0] + cw_[2] * ahead[1] + cw_[1] * ahead[2] + cw_[0] * ahead[3]
        dxr_next[...] = dxr[0:8]
        for k in range(4):
            dcw_ref[k:k + 1, :] += _col_sum(xl * ahead[3 - k])
        dcb_ref[...] += _col_sum(dxr)
        dp_ref[:, W:] = dxl.astype(BF16)

        @pl.when(i == ni - 1)
        def _():
            dlam_ref[...] = -dlam_ref[...] * _sigmoid(-lam_)

    vec = pl.BlockSpec((1, W), lambda i: (0, 0))
    mat = pl.BlockSpec((H, CHUNK, CHUNK), lambda i: (0, 0, 0))
    rev = lambda i: ni - 1 - i
    prev = lambda i: jnp.maximum(rev(i) * hb - 1, 0)
    vshape = jax.ShapeDtypeStruct((1, W), F32)
    mshape = jax.ShapeDtypeStruct((H, CHUNK, CHUNK), F32)
    tile = lambda: pltpu.VMEM((tm, W), F32)
    car = lambda: pltpu.VMEM((8, W), F32)
    return pl.pallas_call(
        body, name=name, grid=(ni,),
        in_specs=[pl.BlockSpec((tm, W), lambda i: (rev(i), 2)), pl.BlockSpec((tm, W), lambda i: (rev(i), 3)),
                  pl.BlockSpec((6, tm, W), lambda i: (0, rev(i), 0)),
                  pl.BlockSpec((tm, W), lambda i: (rev(i), 0)), pl.BlockSpec((8, W), lambda i: (prev(i), 0)),
                  pl.BlockSpec((tm, W), lambda i: (rev(i), 1)),
                  pl.BlockSpec((4, W), lambda i: (0, 0)), mat, mat, vec, vec,
                  pl.BlockSpec(memory_space=pl.ANY)],
        out_specs=[pl.BlockSpec((tm, 2 * W), lambda i: (rev(i), 1)), pl.BlockSpec((4, W), lambda i: (0, 0)), vec,
                   mat, vec, mat, vec, vec, vec],
        out_shape=[jax.ShapeDtypeStruct((S, 4 * W), BF16), jax.ShapeDtypeStruct((4, W), F32), vshape,
                   mshape, vshape, mshape, vshape, vshape, vshape],
        input_output_aliases={11: 0},
        scratch_shapes=[car(), car(), car(), tile(), tile(), tile(), tile(), pltpu.VMEM((tm, W), BF16),
                        pltpu.VMEM((tm, W), BF16), pltpu.VMEM((4, 16, W), F32)],
        compiler_params=_cp(1, 56))(p, p, saved, hs, hs, d_y, cw, wa, wx, lam, out_g, dp_half)


def _rows128(a):
    return a.reshape(-1, LANES).astype(F32)


def _pack(arrays, pad_to=256):
    flat = jnp.concatenate([_rows128(a) for a in arrays], axis=0)
    pad = (-flat.shape[0]) % pad_to
    if pad:
        flat = jnp.concatenate([flat, jnp.zeros((pad, LANES), F32)], axis=0)
    return flat


def _unpack(flat, shapes):
    out, r = [], 0
    for s in shapes:
        n = 1
        for d in s:
            n *= d
        out.append(flat[r:r + n // LANES].reshape(s))
        r += n // LANES
    return out


def kernel(x, norm1_g, w_in, gm_v_g, gm_v_b, gm_ws, gm_bs, lru_conv_w, lru_conv_b, lru_wa, lru_ba, lru_wx, lru_bx, lru_lambda, gm_out_g, lru_out_g, w_out, norm2_g, ffn_w_up, ffn_conv_w, ffn_conv_b, ffn_w_down, final_g, loss_target, m_norm1_g, m_w_in, m_gm_v_g, m_gm_v_b, m_gm_ws, m_gm_bs, m_lru_conv_w, m_lru_conv_b, m_lru_wa, m_lru_ba, m_lru_wx, m_lru_bx, m_lru_lambda, m_gm_out_g, m_lru_out_g, m_w_out, m_norm2_g, m_ffn_w_up, m_ffn_conv_w, m_ffn_conv_b, m_ffn_w_down, m_final_g, v_norm1_g, v_w_in, v_gm_v_g, v_gm_v_b, v_gm_ws, v_gm_bs, v_lru_conv_w, v_lru_conv_b, v_lru_wa, v_lru_ba, v_lru_wx, v_lru_bx, v_lru_lambda, v_gm_out_g, v_lru_out_g, v_w_out, v_norm2_g, v_ffn_w_up, v_ffn_conv_w, v_ffn_conv_b, v_ffn_w_down, v_final_g):
    wts = dict(norm1_g=norm1_g, w_in=w_in, gm_v_g=gm_v_g, gm_v_b=gm_v_b, gm_ws=gm_ws, gm_bs=gm_bs,
               lru_conv_w=lru_conv_w, lru_conv_b=lru_conv_b, lru_wa=lru_wa, lru_ba=lru_ba, lru_wx=lru_wx,
               lru_bx=lru_bx, lru_lambda=lru_lambda, gm_out_g=gm_out_g, lru_out_g=lru_out_g, w_out=w_out,
               norm2_g=norm2_g, ffn_w_up=ffn_w_up, ffn_conv_w=ffn_conv_w, ffn_conv_b=ffn_conv_b,
               ffn_w_down=ffn_w_down, final_g=final_g)
    mom = dict(norm1_g=m_norm1_g, w_in=m_w_in, gm_v_g=m_gm_v_g, gm_v_b=m_gm_v_b, gm_ws=m_gm_ws, gm_bs=m_gm_bs,
               lru_conv_w=m_lru_conv_w, lru_conv_b=m_lru_conv_b, lru_wa=m_lru_wa, lru_ba=m_lru_ba, lru_wx=m_lru_wx,
               lru_bx=m_lru_bx, lru_lambda=m_lru_lambda, gm_out_g=m_gm_out_g, lru_out_g=m_lru_out_g, w_out=m_w_out,
               norm2_g=m_norm2_g, ffn_w_up=m_ffn_w_up, ffn_conv_w=m_ffn_conv_w, ffn_conv_b=m_ffn_conv_b,
               ffn_w_down=m_ffn_w_down, final_g=m_final_g)
    var = dict(norm1_g=v_norm1_g, w_in=v_w_in, gm_v_g=v_gm_v_g, gm_v_b=v_gm_v_b, gm_ws=v_gm_ws, gm_bs=v_gm_bs,
               lru_conv_w=v_lru_conv_w, lru_conv_b=v_lru_conv_b, lru_wa=v_lru_wa, lru_ba=v_lru_ba, lru_wx=v_lru_wx,
               lru_bx=v_lru_bx, lru_lambda=v_lru_lambda, gm_out_g=v_gm_out_g, lru_out_g=v_lru_out_g, w_out=v_w_out,
               norm2_g=v_norm2_g, ffn_w_up=v_ffn_w_up, ffn_conv_w=v_ffn_conv_w, ffn_conv_b=v_ffn_conv_b,
               ffn_w_down=v_ffn_w_down, final_g=v_final_g)

    xi, yi, ci = lax.axis_index("x"), lax.axis_index("y"), lax.axis_index("c")
    chip = 2 * xi + yi
    dev = 2 * chip + ci
    core_chip = jnp.stack([ci, chip]).astype(jnp.int32)

    xs = x[0]
    tgt = loss_target[0]
    S, D = xs.shape
    H = gm_ws.shape[1]
    W = H * CHUNK
    Fd = ffn_w_down.shape[1] * N_DEV
    lcw_cols = lru_conv_w.shape[2]
    fcw_cols = ffn_conv_w.shape[2]

    dev1 = jnp.reshape(dev, (1,)).astype(jnp.int32)

    def place_own(shards, name):
        return [_place_own_call(s, dev1, dt, "%s_own%d" % (name, k)) for k, (s, dt) in enumerate(shards)]

    def gather_start(shards, name, after=()):
        lands = place_own(shards, name)
        return _exchange_start([], lands, 4 * len(lands), _gather_stage1_copies(len(lands)), name + "_ici", after)

    def gather_forward(lands, name, after=()):
        return _exchange_start([], lands, 3 * len(lands), _gather_stage2_copies(len(lands)), name + "_d2d", after)

    def pair_start(g, name, after=()):
        return _exchange_start([g], [lax.empty((4,) + g.shape[1:], F32)], 4, _pair_copies(1), name, after)

    def chip_start(p16, name, after=()):
        return _exchange_start([p16], [lax.empty((3,) + p16.shape[1:], BF16)], 3, _chip_copies(1), name, after)

    vgm_g, vgm_b = gm_v_g, gm_v_b
    ws, wa, wx = gm_ws[0], lru_wa[0], lru_wx[0]
    bsb = jnp.broadcast_to(gm_bs[0][:, :, None], (H, CHUNK, CHUNK))
    ba, bx = lru_ba.reshape(1, W), lru_bx.reshape(1, W)
    fcb = ffn_conv_b
    fing = final_g.reshape(1, D)

    conv_pack = _pack([lru_conv_w[0], ffn_conv_w[0]], pad_to=8)
    lands = place_own([(w_in[0], BF16), (conv_pack, F32)], "gather_in")
    ga_pair = _exchange_start([], lands, 2, _gather_stage1_copies(2, to_chips=False), "gather_in_pair")
    ga1 = _exchange_start([], ga_pair.bufs, 6, _gather_stage1_copies(2, to_sibling=False), "gather_in_ici")
    h1 = _rmsnorm_call(xs, norm1_g, "norm1", deps=(ga1.token,))
    ga_pair.bufs = ga1.bufs
    _, la = _exchange_wait(ga_pair, after=(h1,))
    own_blocks = jnp.stack([dev, dev + 1 - 2 * ci]).astype(jnp.int32)
    other_blocks = ((2 * chip + 2 + jnp.arange(N_DEV - 2)) % N_DEV).astype(jnp.int32)
    p_own = _mm_some_blocks_call(h1, la[0], own_blocks, F32, "in_proj_own")
    ga1.bufs = la
    _, la = _exchange_wait(ga1, after=(p_own,))
    ga2 = gather_forward(la, "gather_in")
    gb1 = gather_start([(w_out[0], BF16)], "gather_out", after=(ga2.token,))
    gc1 = gather_start([(ffn_w_up[0], BF16)], "gather_up", after=(gb1.token,))
    gd1 = gather_start([(ffn_w_down[0], BF16)], "gather_down", after=(gc1.token,))
    _, (win_g, conv_g) = _exchange_wait(ga2, after=(gd1.token,))
    n_l = 4 * lcw_cols // LANES
    n_f = 3 * fcw_cols // LANES
    lcw = conv_g[:, :n_l].reshape(N_DEV, 4, lcw_cols).transpose(1, 0, 2).reshape(4, N_DEV * lcw_cols)
    fcw = conv_g[:, n_l:n_l + n_f].reshape(N_DEV, 3, fcw_cols).transpose(1, 0, 2).reshape(3, N_DEV * fcw_cols)

    p = _mm_some_blocks_call(h1, win_g, other_blocks, F32, "in_proj_rest", out_so_far=p_own)
    win_rows = _unblock_call(win_g, "w_in_rows")
    _, lb = _exchange_wait(gb1, after=(p,))
    gb2 = gather_forward(lb, "gather_out")
    y_half = _gm_fwd_call(p, vgm_g, vgm_b, ws, bsb, gm_out_g, "gmlp_fwd", deps=(gb2.token,))
    y, hs, lru_saved = _lru_fwd_call(p, lcw, lru_conv_b, wa, ba, wx, bx, lru_lambda, lru_out_g, y_half, "lru_fwd")
    _, lc = _exchange_wait(gc1, after=(y,))
    gc2 = gather_forward(lc, "gather_up")
    _, (wout_g,) = _exchange_wait(gb2, after=(y, gc2.token))
    wout_full = wout_g.reshape(D, D)
    x2 = _mm_out_call(xs, y, wout_full, "out_proj")
    h2 = _rmsnorm_call(x2, norm2_g, "norm2")
    _, (wup_g,) = _exchange_wait(gc2, after=(h2,))
    _, ld = _exchange_wait(gd1, after=(h2,))
    gd2 = gather_forward(ld, "gather_down")
    up3, upc3, f = _ffn_up_act_call(h2, wup_g, fcw, fcb, "ffn_up", deps=(gd2.token,))
    _, (wdown_g,) = _exchange_wait(gd2, after=(f,))
    wdown_full = wdown_g.reshape(Fd, D)
    dx3, dx3b, loss_acc, d_final = _mm_down_loss_call(x2, f, wdown_full, fing, tgt, "ffn_down_loss")

    g_wdown = _mm_tn_rows_call(f, dx3b, "ffn_down_dw").reshape((N_DEV,) + ffn_w_down.shape[1:])
    pd = pair_start(g_wdown, "pair_down")
    d_up3, dfcw_g, dfcw_v, dfcb_g, dfcb_v = _ffn_down_dx_act_bwd_call(dx3b, wdown_full, up3, upc3, fcw, "ffn_down_dx",
                                                                     deps=(pd.token,))
    (g_wdown,), (r1,) = _exchange_wait(pd, after=(d_up3,))
    own_down, p16 = _pair_add_call(g_wdown, r1, core_chip, "pair_add_down")
    cd = chip_start(p16, "chip_down")
    g_wup = _mm_tn_cols_call(h2, d_up3, N_DEV, ffn_w_up.shape[2], "ffn_up_dw", deps=(cd.token,))
    pu = pair_start(g_wup, "pair_up")
    dx2, dx2b, d_norm2 = _mm_dx_norm_call(d_up3, wup_g, dx3, x2, norm2_g, "ffn_up_dx", deps=(pu.token,))
    g_wout = _mm_tn_rows_call(y, dx2b, "out_proj_dw").reshape((N_DEV,) + w_out.shape[1:])
    po = pair_start(g_wout, "pair_out")
    d_y = _mm_nt_call(dx2b, wout_full, F32, "out_proj_dx", deps=(po.token,))
    (g_wup,), (r1,) = _exchange_wait(pu, after=(d_y,))
    own_up, p16 = _pair_add_call(g_wup, r1, core_chip, "pair_add_up")
    _, (r2_down,) = _exchange_wait(cd, after=(p16,))
    cu = chip_start(p16, "chip_up", after=(r2_down,))
    dp_half, d_vg, d_vb, d_ws, d_bs, d_gog = _gm_bwd_call(p, d_y, vgm_g, vgm_b, ws, bsb, gm_out_g, "gmlp_bwd",
                                                          deps=(cu.token,))
    d_p2, d_lcw, d_lcb, d_wa, d_ba, d_wx, d_bx, d_lam, d_log = _lru_bwd_call(
        p, hs, lru_saved, d_y, lcw, wa, wx, lru_lambda, lru_out_g, dp_half, "lru_bwd")
    d_p = d_p2[None]
    (g_wout,), (r1,) = _exchange_wait(po, after=(d_p,))
    own_out, p16_out = _pair_add_call(g_wout, r1, core_chip, "pair_add_out")
    g_win = _mm_tn_cols_call(h1, d_p, N_DEV, w_in.shape[2], "in_proj_dw")
    pi = pair_start(g_win, "pair_in")
    _, (r2_up,) = _exchange_wait(cu, after=(g_win,))
    co = chip_start(p16_out, "chip_out", after=(r2_up,))
    gx_a, dn_a = _mm_nt_norm_call(d_p[0], win_rows, dx2, xs, norm1_g, "in_proj_dx_a", deps=(co.token, pi.token),
                                  part=(0, 2))
    (g_win,), (r1,) = _exchange_wait(pi, after=(gx_a,))
    own_in, p16 = _pair_add_call(g_win, r1, core_chip, "pair_add_in")
    _, (r2_out,) = _exchange_wait(co, after=(p16,))
    ci_ = chip_start(p16, "chip_in", after=(r2_out,))
    grad_x, dn_b = _mm_nt_norm_call(d_p[0], win_rows, dx2, xs, norm1_g, "in_proj_dx_b", deps=(ci_.token,),
                                    part=(1, 2), dx_so_far=gx_a)

    small_g = dict(norm1_g=dn_a + dn_b, gm_v_g=d_vg, gm_v_b=d_vb, gm_ws=d_ws, gm_bs=d_bs[:, :, 0], lru_conv_b=d_lcb,
                   lru_wa=d_wa, lru_ba=d_ba, lru_wx=d_wx, lru_bx=d_bx, lru_lambda=d_lam, gm_out_g=d_gog,
                   lru_out_g=d_log, norm2_g=d_norm2,
                   ffn_conv_b=jnp.concatenate([dfcb_g, dfcb_v], axis=1), final_g=d_final)
    rep = _pack([small_g[n] for n in SMALL])
    conv_part = _pack([d_lcw, jnp.concatenate([dfcw_g, dfcw_v], axis=1)], pad_to=8)
    n_rep, n_conv = rep.shape[0], conv_part.shape[0]
    gs1 = gather_start([(jnp.concatenate([rep, conv_part], axis=0), F32)], "gather_small")

    def adamw_big(n, own, r2, deps=()):
        return _adamw_call(wts[n][0], mom[n][0], var[n][0], [(own, None), (r2, 0), (r2, 1), (r2, 2)], "adamw_" + n, deps)

    res = {}
    res["ffn_w_down"] = adamw_big("ffn_w_down", own_down, r2_down, (gs1.token,))
    res["ffn_w_up"] = adamw_big("ffn_w_up", own_up, r2_up, (gs1.token,))
    res["w_out"] = adamw_big("w_out", own_out, r2_out, (gs1.token,))
    _, ls = _exchange_wait(gs1, after=(res["w_out"][0], res["ffn_w_up"][0], res["ffn_w_down"][0]))
    gs2 = gather_forward(ls, "gather_small")
    _, (r2_in,) = _exchange_wait(ci_, after=(gs2.token,))
    res["w_in"] = adamw_big("w_in", own_in, r2_in)
    _, (parts,) = _exchange_wait(gs2, after=(res["w_in"][0],))
    g_rep, d_rep, m_rep, v_rep = _adamw_call(
        _pack([wts[n] for n in SMALL]), _pack([mom[n] for n in SMALL]), _pack([var[n] for n in SMALL]),
        [(parts, k) for k in range(N_DEV)], "adamw_small")
    shapes = [wts[n].shape for n in SMALL]
    for n, g_, d_, m_, v_ in zip(SMALL, _unpack(g_rep, shapes), _unpack(d_rep, shapes), _unpack(m_rep, shapes),
                                 _unpack(v_rep, shapes)):
        res[n] = (g_, d_, m_, v_)
    conv_sum = _sum_call(parts, n_rep, n_conv, "sum_conv_grads")
    g_lcw = conv_sum[:4 * W // LANES].reshape(4, W)
    g_fcw = conv_sum[4 * W // LANES:4 * W // LANES + 6 * Fd // LANES].reshape(3, 2 * Fd)
    for n, full in (("lru_conv_w", g_lcw), ("ffn_conv_w", g_fcw)):
        cols = wts[n].shape[2]
        mine = lax.dynamic_slice_in_dim(full, dev * cols, cols, axis=1)
        res[n] = _adamw_call(wts[n][0], mom[n][0], var[n][0], [(mine, None)], "adamw_" + n)

    loss = lax.psum(loss_acc[0, 0], ("x", "y", "c"))
    outs = [[], [], [], []]
    for n in WEIGHTS:
        for k in range(4):
            outs[k].append(res[n][k].reshape(wts[n].shape))
    return (loss, grad_x[None], *outs[0], *outs[1], *outs[2], *outs[3])
```

```python
import functools
import math

import jax
import jax.numpy as jnp
from jax import lax
from jax.experimental import pallas as pl
from jax.experimental.pallas import tpu as pltpu

F32 = jnp.float32
BF16 = jnp.bfloat16

RMS_EPS = 1e-6
LN_EPS = 1e-5
LRU_C = 8.0
CHUNK = 128
ADAM_LR = 0.001
ADAM_B1 = 0.9
ADAM_B2 = 0.999
ADAM_EPS = 1e-08
ADAM_WD = 0.01
ADAM_STEP = 10
N_DEV = 8
LANES = 128
MIB = 1024 * 1024

WEIGHTS = ['norm1_g', 'w_in', 'gm_v_g', 'gm_v_b', 'gm_ws', 'gm_bs', 'lru_conv_w', 'lru_conv_b', 'lru_wa', 'lru_ba',
           'lru_wx', 'lru_bx', 'lru_lambda', 'gm_out_g', 'lru_out_g', 'w_out', 'norm2_g', 'ffn_w_up', 'ffn_conv_w',
           'ffn_conv_b', 'ffn_w_down', 'final_g']
BIG = ['w_in', 'w_out', 'ffn_w_up', 'ffn_w_down']
CONV = ['lru_conv_w', 'ffn_conv_w']
SMALL = [n for n in WEIGHTS if n not in BIG and n not in CONV]

_DN_NT = (((1,), (1,)), ((), ()))
_DN_TN = (((0,), (0,)), ((), ()))
_GELU_C = 0.7978845608028654


def _cp(n_axes, vmem_mib=48):
    return pltpu.CompilerParams(dimension_semantics=("arbitrary",) * n_axes, vmem_limit_bytes=vmem_mib * MIB)


def _tile(n, pref, mult=8):
    t = min(pref, n)
    t -= t % mult
    while t >= mult:
        if n % t == 0:
            return t
        t -= mult
    return n


def _gelu_gate(z, z2):
    return 0.5 * jnp.tanh(z * ((_GELU_C * 0.044715) * z2 + _GELU_C)) + 0.5


def _gelu(z):
    return z * _gelu_gate(z, z * z)


def _gelu_parts(z):
    z2 = z * z
    s = _gelu_gate(z, z2)
    g = z * s
    dg = s + g * (1.0 - s) * ((6.0 * _GELU_C * 0.044715) * z2 + 2.0 * _GELU_C)
    return g, dg


def _sigmoid(z):
    return 0.5 + 0.5 * jnp.tanh(0.5 * z)


def _softplus(z):
    t = jnp.exp(-jnp.abs(z))
    u = 1.0 + t
    log1p = jnp.where(u == 1.0, t, jnp.log(u) * t / (u - 1.0))
    return jnp.maximum(z, 0.0) + log1p


def _rows_mean(v):
    return jnp.mean(v, axis=-1, keepdims=True)


def _col_sum(v):
    return jnp.sum(v, axis=0, keepdims=True)


def _shift_down(prev8, cur, k):
    if k == 0:
        return cur
    z = jnp.concatenate([prev8, cur], axis=0)
    return pltpu.roll(z, k, 0)[8:]


def _shift_up(cur, next8, k):
    if k == 0:
        return cur
    n = cur.shape[0]
    z = jnp.concatenate([cur, next8], axis=0)
    return pltpu.roll(z, n + 8 - k, 0)[:n]


def _mesh_pos():
    return lax.axis_index("x"), lax.axis_index("y"), lax.axis_index("c")


def _any_specs(n):
    return [pl.BlockSpec(memory_space=pl.ANY)] * n


def _pallas(body, n_in, deps, **kw):
    nd = len(deps)
    if not nd:
        return pl.pallas_call(body, **kw)

    def ordered(*refs):
        body(*refs[:n_in], *refs[n_in + nd:])

    kw["in_specs"] = list(kw["in_specs"]) + _any_specs(nd)
    return pl.pallas_call(ordered, **kw)


_HBM = pl.BlockSpec(memory_space=pltpu.HBM)
_SEM = pl.BlockSpec(memory_space=pltpu.SEMAPHORE)
_EFFECT = pltpu.SideEffectType.DATAFLOW_SIDE_EFFECTING


class _InFlight:
    def __init__(self, sems, bufs, token, n_src, n_copies, make_copies, name):
        self.sems, self.bufs, self.token = sems, bufs, token
        self.n_src, self.n_copies, self.make_copies, self.name = n_src, n_copies, make_copies, name


def _exchange_start(srcs, lands, n_copies, make_copies, name, after=()):
    bufs = list(srcs) + list(lands)
    nb, na = len(bufs), len(after)
    ns = len(srcs)

    def body(*refs):
        b_refs = refs[:nb]
        outs = refs[nb + na:]
        send, recv = outs[:n_copies], outs[n_copies:2 * n_copies]
        token = outs[-1]
        for cp in make_copies(b_refs[:ns], b_refs[ns:], send, recv):
            cp.start()
        token[...] = jnp.zeros_like(token)

    out = pl.pallas_call(
        body, name=name,
        out_shape=[pltpu.SemaphoreType.DMA(())] * (2 * n_copies) + [pltpu.HBM(b.shape, b.dtype) for b in bufs]
        + [jax.ShapeDtypeStruct((8, LANES), F32)],
        in_specs=[_HBM] * nb + _any_specs(na),
        out_specs=[_SEM] * (2 * n_copies) + [_HBM] * nb + [pl.BlockSpec(memory_space=pltpu.VMEM)],
        input_output_aliases={i: 2 * n_copies + i for i in range(nb)},
        compiler_params=pltpu.CompilerParams(has_side_effects=_EFFECT),
    )(*[pltpu.with_memory_space_constraint(b, pltpu.HBM) for b in bufs], *after)
    return _InFlight(out[:2 * n_copies], out[2 * n_copies:2 * n_copies + nb], out[-1], ns, n_copies, make_copies, name)


def _exchange_wait(fl, after=()):
    nb, na, nc, ns = len(fl.bufs), len(after), fl.n_copies, fl.n_src

    def body(*refs):
        b_refs = refs[:nb]
        sems = refs[nb:nb + 2 * nc]
        copies = fl.make_copies(b_refs[:ns], b_refs[ns:], sems[:nc], sems[nc:])
        for cp in copies:
            cp.wait_send()
        for cp in copies:
            cp.wait_recv()

    out = pl.pallas_call(
        body, name=fl.name + "_wait",
        out_shape=[pltpu.HBM(b.shape, b.dtype) for b in fl.bufs],
        in_specs=[_HBM] * nb + [_SEM] * (2 * nc) + _any_specs(na),
        out_specs=[_HBM] * nb,
        input_output_aliases={i: i for i in range(nb)},
        compiler_params=pltpu.CompilerParams(has_side_effects=_EFFECT),
    )(*fl.bufs, *fl.sems, *after)
    return list(out[:ns]), list(out[ns:])


def _remote(src, dst, send_sem, recv_sem, to):
    return pltpu.make_async_remote_copy(src_ref=src, dst_ref=dst, send_sem=send_sem, recv_sem=recv_sem,
                                        device_id=to, device_id_type=pl.DeviceIdType.MESH)


def _gather_stage1_copies(n, to_sibling=True, to_chips=True):
    def make(s_refs, l_refs, send, recv):
        x, y, c = _mesh_pos()
        own = 4 * x + 2 * y + c
        targets = ([(x, y, 1 - c)] if to_sibling else []) + (
            [(1 - x, y, c), (x, 1 - y, c), (1 - x, 1 - y, c)] if to_chips else [])
        m = len(targets)
        return [_remote(l_refs[a].at[own], l_refs[a].at[own], send[m * a + k], recv[m * a + k], to)
                for a in range(n) for k, to in enumerate(targets)]
    return make


def _gather_stage2_copies(n):
    def make(s_refs, l_refs, send, recv):
        x, y, c = _mesh_pos()
        blocks = [4 * (1 - x) + 2 * y + c, 4 * x + 2 * (1 - y) + c, 4 * (1 - x) + 2 * (1 - y) + c]
        return [_remote(l_refs[a].at[b], l_refs[a].at[b], send[3 * a + j], recv[3 * a + j], (x, y, 1 - c))
                for a in range(n) for j, b in enumerate(blocks)]
    return make


def _pair_copies(n):
    def make(s_refs, l_refs, send, recv):
        x, y, c = _mesh_pos()
        return [_remote(s_refs[a].at[2 * k + 1 - c], l_refs[a].at[k], send[4 * a + k], recv[4 * a + k], (x, y, 1 - c))
                for a in range(n) for k in range(4)]
    return make


def _chip_copies(n):
    def make(s_refs, l_refs, send, recv):
        x, y, c = _mesh_pos()
        chips = [(1 - x, y), (x, 1 - y), (1 - x, 1 - y)]
        return [_remote(s_refs[a].at[2 * ch[0] + ch[1]], l_refs[a].at[j], send[3 * a + j], recv[3 * a + j], (*ch, c))
                for a in range(n) for j, ch in enumerate(chips)]
    return make


def _place_own_call(shard, dev, dtype, name):
    R, C = shard.shape
    tr = _tile(R, max(16, MIB // (C * 4)), 16)

    def body(d_ref, s_ref, o_ref):
        o_ref[...] = s_ref[...].astype(dtype)

    grid_spec = pltpu.PrefetchScalarGridSpec(
        num_scalar_prefetch=1, grid=(R // tr,),
        in_specs=[pl.BlockSpec((tr, C), lambda r, d: (r, 0))],
        out_specs=pl.BlockSpec((None, tr, C), lambda r, d: (d[0], r, 0)))
    return pl.pallas_call(body, name=name, grid_spec=grid_spec,
                          out_shape=jax.ShapeDtypeStruct((N_DEV, R, C), dtype), compiler_params=_cp(1))(dev, shard)


def _pair_add_call(g, r1, core_chip, name):
    _, R, C = g.shape
    tr = _tile(R, max(16, (2 * MIB) // (C * 4)), 16)

    def body(cc_ref, g_ref, r_ref, p32_ref, p16_ref):
        s = g_ref[...] + r_ref[...]
        p16_ref[...] = s.astype(BF16)

        @pl.when(pl.program_id(1) == cc_ref[1])
        def _():
            p32_ref[...] = s

    grid_spec = pltpu.PrefetchScalarGridSpec(
        num_scalar_prefetch=1, grid=(R // tr, 4),
        in_specs=[pl.BlockSpec((None, tr, C), lambda r, k, cc: (2 * k + cc[0], r, 0)),
                  pl.BlockSpec((None, tr, C), lambda r, k, cc: (k, r, 0))],
        out_specs=[pl.BlockSpec((tr, C), lambda r, k, cc: (r, 0)),
                   pl.BlockSpec((None, tr, C), lambda r, k, cc: (k, r, 0))])
    return pl.pallas_call(
        body, name=name, grid_spec=grid_spec,
        out_shape=[jax.ShapeDtypeStruct((R, C), F32), jax.ShapeDtypeStruct((4, R, C), BF16)],
        compiler_params=_cp(2))(core_chip, g, r1)


def _adamw_call(w, m, v, addends, name, deps=()):
    R, C = w.shape
    tr = _tile(R, max(8, (MIB // 2) // (C * 4)), 16)
    na = len(addends)
    c1 = 1.0 - ADAM_B1 ** ADAM_STEP
    c2 = 1.0 - ADAM_B2 ** ADAM_STEP

    def body(*refs):
        w_ref, m_ref, v_ref = refs[:3]
        a_refs = refs[3:3 + na]
        g_ref, d_ref, nm_ref, nv_ref = refs[3 + na:]
        g = a_refs[0][...].astype(F32)
        for a_ref in a_refs[1:]:
            g = g + a_ref[...].astype(F32)
        nm = ADAM_B1 * m_ref[...] + (1.0 - ADAM_B1) * g
        nv = ADAM_B2 * v_ref[...] + (1.0 - ADAM_B2) * (g * g)
        g_ref[...] = g
        nm_ref[...] = nm
        nv_ref[...] = nv
        d_ref[...] = -ADAM_LR * ((nm / c1) / (jnp.sqrt(nv / c2) + ADAM_EPS) + ADAM_WD * w_ref[...])

    flat = pl.BlockSpec((tr, C), lambda r: (r, 0))
    a_specs = [flat if k is None else pl.BlockSpec((None, tr, C), functools.partial(lambda r, kk: (kk, r, 0), kk=k))
               for _, k in addends]
    out = jax.ShapeDtypeStruct((R, C), F32)
    return _pallas(
        body, 3 + na, deps, name=name, grid=(R // tr,),
        in_specs=[flat, flat, flat] + a_specs, out_specs=[flat] * 4, out_shape=[out] * 4,
        compiler_params=_cp(1))(w, m, v, *[a for a, _ in addends], *deps)


def _sum_call(parts, row0, rows, name):
    n = parts.shape[0]
    tr = _tile(math.gcd(row0, rows), 256, 8)
    b0 = row0 // tr

    def body(p_ref, o_ref):
        s = p_ref[0]
        for k in range(1, n):
            s = s + p_ref[k]
        o_ref[...] = s

    return pl.pallas_call(
        body, name=name, grid=(rows // tr,),
        in_specs=[pl.BlockSpec((n, tr, LANES), lambda r: (0, r + b0, 0))],
        out_specs=pl.BlockSpec((tr, LANES), lambda r: (r, 0)),
        out_shape=jax.ShapeDtypeStruct((rows, LANES), F32), compiler_params=_cp(1))(parts)


def _rmsnorm_call(x, g, name, deps=()):
    S, D = x.shape
    tm = _tile(S, 512, 16)

    def body(x_ref, g_ref, o_ref):
        xv = x_ref[...]
        r = lax.rsqrt(_rows_mean(xv * xv) + RMS_EPS)
        o_ref[...] = (xv * r * g_ref[...]).astype(BF16)

    return _pallas(
        body, 2, deps, name=name, grid=(S // tm,),
        in_specs=[pl.BlockSpec((tm, D), lambda i: (i, 0)), pl.BlockSpec((1, D), lambda i: (0, 0))],
        out_specs=pl.BlockSpec((tm, D), lambda i: (i, 0)),
        out_shape=jax.ShapeDtypeStruct((S, D), BF16), compiler_params=_cp(1))(x, g, *deps)


def _mm_some_blocks_call(a, wg, blocks, out_dtype, name, out_so_far=None):
    S, K = a.shape
    nb, _, bn = wg.shape
    tm = _tile(S, 1024, 16)

    def body(b_ref, a_ref, w_ref, *rest):
        rest[-1][...] = jnp.dot(a_ref[...], w_ref[...], preferred_element_type=F32).astype(out_dtype)

    in_specs = [pl.BlockSpec((tm, K), lambda i, j, b: (i, 0)), pl.BlockSpec((None, K, bn), lambda i, j, b: (b[j], 0, 0))]
    operands = [a, wg]
    aliases = {}
    if out_so_far is not None:
        in_specs.append(pl.BlockSpec(memory_space=pl.ANY))
        operands.append(out_so_far)
        aliases = {3: 0}
    grid_spec = pltpu.PrefetchScalarGridSpec(
        num_scalar_prefetch=1, grid=(S // tm, blocks.shape[0]), in_specs=in_specs,
        out_specs=pl.BlockSpec((tm, bn), lambda i, j, b: (i, b[j])))
    return pl.pallas_call(body, name=name, grid_spec=grid_spec,
                          out_shape=jax.ShapeDtypeStruct((S, nb * bn), out_dtype), input_output_aliases=aliases,
                          compiler_params=_cp(2))(blocks, *operands)


def _mm_out_call(x, y, w, name):
    S, D = x.shape
    tm = _tile(S, 512, 16)

    def body(x_ref, y_ref, w_ref, o_ref):
        o_ref[...] = x_ref[...] + jnp.dot(y_ref[...], w_ref[...], preferred_element_type=F32)

    return pl.pallas_call(
        body, name=name, grid=(S // tm,),
        in_specs=[pl.BlockSpec((tm, D), lambda i: (i, 0)), pl.BlockSpec((tm, D), lambda i: (i, 0)),
                  pl.BlockSpec((D, D), lambda i: (0, 0))],
        out_specs=pl.BlockSpec((tm, D), lambda i: (i, 0)),
        out_shape=jax.ShapeDtypeStruct((S, D), F32), compiler_params=_cp(1))(x, y, w)


def _mm_nt_call(a, w, out_dtype, name, deps=()):
    S, K = a.shape
    N = w.shape[0]
    tm = _tile(S, 1024, 16)
    tn = _tile(N, 768, LANES)

    def body(a_ref, w_ref, o_ref):
        o_ref[...] = lax.dot_general(a_ref[...], w_ref[...], _DN_NT, preferred_element_type=F32).astype(out_dtype)

    return _pallas(
        body, 2, deps, name=name, grid=(S // tm, N // tn),
        in_specs=[pl.BlockSpec((tm, K), lambda i, j: (i, 0)), pl.BlockSpec((tn, K), lambda i, j: (j, 0))],
        out_specs=pl.BlockSpec((tm, tn), lambda i, j: (i, j)),
        out_shape=jax.ShapeDtypeStruct((S, N), out_dtype), compiler_params=_cp(2))(a, w, *deps)


def _mm_down_loss_call(x2, f, w, final_g, target, name):
    S, D = x2.shape
    Fd = f.shape[1]
    tm = _tile(S, 512, 16)
    tk = _tile(Fd, 768, LANES)
    nk = Fd // tk

    def body(x_ref, f_ref, w_ref, g_ref, t_ref, dx_ref, dxb_ref, loss_ref, dg_ref, acc):
        i, k = pl.program_id(0), pl.program_id(1)

        @pl.when(jnp.logical_and(i == 0, k == 0))
        def _():
            loss_ref[...] = jnp.zeros_like(loss_ref)
            dg_ref[...] = jnp.zeros_like(dg_ref)

        @pl.when(k == 0)
        def _():
            acc[...] = jnp.zeros_like(acc)

        acc[...] += jnp.dot(f_ref[...], w_ref[...], preferred_element_type=F32)

        @pl.when(k == nk - 1)
        def _():
            x3 = x_ref[...] + acc[...]
            r = lax.rsqrt(_rows_mean(x3 * x3) + RMS_EPS)
            g = g_ref[...]
            xn = x3 * r
            diff = xn * g - t_ref[...]
            loss_ref[...] += 0.5 * jnp.sum(_rows_mean(diff * diff))
            dout = diff * (1.0 / D)
            dg_ref[...] += _col_sum(dout * xn)
            dyg = dout * g
            dx = r * (dyg - xn * _rows_mean(dyg * xn))
            dx_ref[...] = dx
            dxb_ref[...] = dx.astype(BF16)

    row = lambda i, k: (i, 0)
    return pl.pallas_call(
        body, name=name, grid=(S // tm, nk),
        in_specs=[pl.BlockSpec((tm, D), row), pl.BlockSpec((tm, tk), lambda i, k: (i, k)),
                  pl.BlockSpec((tk, D), lambda i, k: (k, 0)), pl.BlockSpec((1, D), lambda i, k: (0, 0)),
                  pl.BlockSpec((tm, D), row)],
        out_specs=[pl.BlockSpec((tm, D), row), pl.BlockSpec((tm, D), row),
                   pl.BlockSpec((8, LANES), lambda i, k: (0, 0)), pl.BlockSpec((1, D), lambda i, k: (0, 0))],
        out_shape=[jax.ShapeDtypeStruct((S, D), F32), jax.ShapeDtypeStruct((S, D), BF16),
                   jax.ShapeDtypeStruct((8, LANES), F32), jax.ShapeDtypeStruct((1, D), F32)],
        scratch_shapes=[pltpu.VMEM((tm, D), F32)], compiler_params=_cp(2, 56))(x2, f, w, final_g, target)


def _mm_dx_norm_call(a3, wg, resid, xin, g, name, deps=()):
    na, S, Fa = a3.shape
    nb, D, bn = wg.shape
    tm = _tile(S, 512, 16)
    tk = _tile(bn, 1536, LANES)
    nsub = bn // tk
    nka = Fa // tk
    nk = nb * nsub
    assert na * nka == nk

    def body(a_ref, w_ref, r_ref, x_ref, g_ref, dx_ref, dxb_ref, dg_ref, acc):
        i, k = pl.program_id(0), pl.program_id(1)

        @pl.when(jnp.logical_and(i == 0, k == 0))
        def _():
            dg_ref[...] = jnp.zeros_like(dg_ref)

        @pl.when(k == 0)
        def _():
            acc[...] = jnp.zeros_like(acc)

        acc[...] += lax.dot_general(a_ref[...], w_ref[...], _DN_NT, preferred_element_type=F32)

        @pl.when(k == nk - 1)
        def _():
            dh = acc[...]
            xv = x_ref[...]
            r = lax.rsqrt(_rows_mean(xv * xv) + RMS_EPS)
            xn = xv * r
            dg_ref[...] += _col_sum(dh * xn)
            dyg = dh * g_ref[...]
            dx = r_ref[...] + r * (dyg - xn * _rows_mean(dyg * xn))
            dx_ref[...] = dx
            dxb_ref[...] = dx.astype(BF16)

    row = lambda i, k: (i, 0)
    return _pallas(
        body, 5, deps, name=name, grid=(S // tm, nk),
        in_specs=[pl.BlockSpec((None, tm, tk), lambda i, k: (k // nka, i, k % nka)),
                  pl.BlockSpec((None, D, tk), lambda i, k: (k // nsub, 0, k % nsub)),
                  pl.BlockSpec((tm, D), row, pipeline_mode=pl.Buffered(1)),
                  pl.BlockSpec((tm, D), row, pipeline_mode=pl.Buffered(1)), pl.BlockSpec((1, D), lambda i, k: (0, 0))],
        out_specs=[pl.BlockSpec((tm, D), row), pl.BlockSpec((tm, D), row), pl.BlockSpec((1, D), lambda i, k: (0, 0))],
        out_shape=[jax.ShapeDtypeStruct((S, D), F32), jax.ShapeDtypeStruct((S, D), BF16),
                   jax.ShapeDtypeStruct((1, D), F32)],
        scratch_shapes=[pltpu.VMEM((tm, D), F32)], compiler_params=_cp(2, 56))(a3, wg, resid, xin, g, *deps)


def _unblock_call(wg, name):
    nb, K, bn = wg.shape

    def body(w_ref, o_ref):
        o_ref[...] = w_ref[...]

    return pl.pallas_call(
        body, name=name, grid=(nb,),
        in_specs=[pl.BlockSpec((None, K, bn), lambda o: (o, 0, 0))],
        out_specs=pl.BlockSpec((K, bn), lambda o: (0, o)),
        out_shape=jax.ShapeDtypeStruct((K, nb * bn), wg.dtype), compiler_params=_cp(1))(wg)


def _mm_nt_norm_call(a, w, resid, xin, g, name, deps=(), part=(0, 1), dx_so_far=None):
    S, K = a.shape
    D = w.shape[0]
    tm = _tile(S, 256, 16)
    tiles = (S // tm) // part[1]
    first = part[0] * tiles

    def body(a_ref, w_ref, r_ref, x_ref, g_ref, *rest):
        dx_ref, dg_ref = rest[-2:]

        @pl.when(pl.program_id(0) == 0)
        def _():
            dg_ref[...] = jnp.zeros_like(dg_ref)

        dh = lax.dot_general(a_ref[...], w_ref[...], _DN_NT, preferred_element_type=F32)
        xv = x_ref[...]
        r = lax.rsqrt(_rows_mean(xv * xv) + RMS_EPS)
        xn = xv * r
        dg_ref[...] += _col_sum(dh * xn)
        dyg = dh * g_ref[...]
        dx_ref[...] = r_ref[...] + r * (dyg - xn * _rows_mean(dyg * xn))

    row = lambda i: (i + first, 0)
    fixed = lambda i: (0, 0)
    in_specs = [pl.BlockSpec((tm, K), row), pl.BlockSpec((D, K), fixed, pipeline_mode=pl.Buffered(1)),
                pl.BlockSpec((tm, D), row), pl.BlockSpec((tm, D), row), pl.BlockSpec((1, D), fixed)]
    operands = [a, w, resid, xin, g]
    aliases = {}
    if dx_so_far is not None:
        in_specs.append(pl.BlockSpec(memory_space=pl.ANY))
        operands.append(dx_so_far)
        aliases = {5: 0}
    return _pallas(
        body, len(operands), deps, name=name, grid=(tiles,),
        in_specs=in_specs, out_specs=[pl.BlockSpec((tm, D), row), pl.BlockSpec((1, D), fixed)],
        out_shape=[jax.ShapeDtypeStruct((S, D), F32), jax.ShapeDtypeStruct((1, D), F32)],
        input_output_aliases=aliases, compiler_params=_cp(1, 56))(*operands, *deps)


def _mm_tn_cols_call(a, b3, nb, bn, name, deps=()):
    S, Ka = a.shape
    nh, _, Fb = b3.shape
    tm = _tile(S, 2048, 16)
    tn = _tile(bn, 768, LANES)
    nsub = bn // tn
    njb = Fb // tn
    J = nb * nsub
    assert nh * njb == J

    def body(a_ref, b_ref, o_ref):
        @pl.when(pl.program_id(1) == 0)
        def _():
            o_ref[...] = jnp.zeros_like(o_ref)

        o_ref[...] += lax.dot_general(a_ref[...], b_ref[...], _DN_TN, preferred_element_type=F32)

    return _pallas(
        body, 2, deps, name=name, grid=(J, S // tm),
        in_specs=[pl.BlockSpec((tm, Ka), lambda j, i: (i, 0)),
                  pl.BlockSpec((None, tm, tn), lambda j, i: (j // njb, i, j % njb))],
        out_specs=pl.BlockSpec((None, Ka, tn), lambda j, i: (j // nsub, 0, j % nsub)),
        out_shape=jax.ShapeDtypeStruct((nb, Ka, bn), F32), compiler_params=_cp(2, 56))(a, b3, *deps)


def _mm_tn_rows_call(a, b, name, deps=()):
    S, E = a.shape
    D = b.shape[1]
    tm = _tile(S, 2048, 16)
    te = _tile(E, 768, LANES)

    def body(a_ref, b_ref, o_ref):
        @pl.when(pl.program_id(1) == 0)
        def _():
            o_ref[...] = jnp.zeros_like(o_ref)

        o_ref[...] += lax.dot_general(a_ref[...], b_ref[...], _DN_TN, preferred_element_type=F32)

    return _pallas(
        body, 2, deps, name=name, grid=(E // te, S // tm),
        in_specs=[pl.BlockSpec((tm, te), lambda j, i: (i, j)), pl.BlockSpec((tm, D), lambda j, i: (i, 0))],
        out_specs=pl.BlockSpec((te, D), lambda j, i: (j, 0)),
        out_shape=jax.ShapeDtypeStruct((E, D), F32), compiler_params=_cp(2, 56))(a, b, *deps)


def _ffn_tiles(S, Fd):
    return _tile(S, 512, 16), _tile(Fd // (N_DEV // 2), 1536, LANES)


def _taps(cw_ref):
    return [cw_ref[k:k + 1, :] for k in range(cw_ref.shape[0])]


def _conv3(prev8, cur, taps):
    s1 = _shift_down(prev8, cur, 1)
    s2 = _shift_down(prev8, cur, 2)
    return taps[2] * cur + taps[1] * s1 + taps[0] * s2, s1, s2


GATE_LANES = 256
SUB_LANES = 1536


def _lane_taps(cw_ref, ls):
    return [cw_ref[k:k + 1, ls] for k in range(cw_ref.shape[0])]


def _ffn_up_act_call(h2, wg, cw, cb, name, deps=()):
    S, D = h2.shape
    nb, _, bn = wg.shape
    Fd = nb * bn // 2
    tm, tc = _ffn_tiles(S, Fd)
    nk = Fd // tc
    hb = tm // 16
    nsubw = bn // tc
    half = nb // 2
    sc = _tile(tc, SUB_LANES, LANES)

    def body(a_ref, ap_ref, wgate_ref, wval_ref, cwg_ref, cwv_ref, cbg_ref, cbv_ref, up_ref, upc_ref, f_ref):
        keep = jnp.where(pl.program_id(0) == 0, 0.0, 1.0)
        a_ext = jnp.concatenate([ap_ref[...], a_ref[...]], axis=0)
        nsub = tc // sc
        lanes = [slice(s * sc, (s + 1) * sc) for s in range(nsub)]

        def products(s):
            return [jnp.dot(a_ext, w_ref[:, lanes[s]], preferred_element_type=F32)
                    for w_ref in (wgate_ref, wval_ref)]

        ready = products(0)
        for s in range(nsub):
            ls = lanes[s]
            following = products(s + 1) if s + 1 < nsub else None

            def conv_half(u, cw_ref, cb_ref, slab):
                up_ref[slab, :, ls] = u[16:].astype(BF16)
                conv, _, _ = _conv3(u[8:16] * keep, u[16:], _lane_taps(cw_ref, ls))
                c = conv + cb_ref[:, ls]
                upc_ref[slab, :, ls] = c.astype(BF16)
                return c

            cg = conv_half(ready[0], cwg_ref, cbg_ref, 0)
            cv = conv_half(ready[1], cwv_ref, cbv_ref, 1)
            f_ref[:, ls] = (_gelu(cg) * cv).astype(BF16)
            ready = following

    return _pallas(
        body, 8, deps, name=name, grid=(S // tm, nk),
        in_specs=[pl.BlockSpec((tm, D), lambda i, k: (i, 0)),
                  pl.BlockSpec((16, D), lambda i, k: (jnp.maximum(i * hb - 1, 0), 0)),
                  pl.BlockSpec((None, D, tc), lambda i, k: (k // nsubw, 0, k % nsubw)),
                  pl.BlockSpec((None, D, tc), lambda i, k: (half + k // nsubw, 0, k % nsubw)),
                  pl.BlockSpec((3, tc), lambda i, k: (0, k)), pl.BlockSpec((3, tc), lambda i, k: (0, k + nk)),
                  pl.BlockSpec((1, tc), lambda i, k: (0, k)), pl.BlockSpec((1, tc), lambda i, k: (0, k + nk))],
        out_specs=[pl.BlockSpec((2, tm, tc), lambda i, k: (0, i, k)), pl.BlockSpec((2, tm, tc), lambda i, k: (0, i, k)),
                   pl.BlockSpec((tm, tc), lambda i, k: (i, k))],
        out_shape=[jax.ShapeDtypeStruct((2, S, Fd), BF16), jax.ShapeDtypeStruct((2, S, Fd), BF16),
                   jax.ShapeDtypeStruct((S, Fd), BF16)],
        compiler_params=_cp(2, 56))(h2, h2, wg, wg, cw, cw, cb, cb, *deps)


def _ffn_down_dx_act_bwd_call(dxb, w, up3, upc3, cw, name, deps=()):
    S, D = dxb.shape
    _, _, Fd = up3.shape
    tm, tc = _ffn_tiles(S, Fd)
    nj = Fd // tc
    ni = S // tm
    hb = tm // 16
    sc = _tile(tc, SUB_LANES, LANES)

    def body(a_ref, an_ref, w_ref, g_ref, v_ref, cg_ref, cv_ref, cgn_ref, cvn_ref, cwg_ref, cwv_ref,
             dup_ref, dcwg_ref, dcwv_ref, dcbg_ref, dcbv_ref):
        i = pl.program_id(1)
        keep_next = jnp.where(i == ni - 1, 0.0, 1.0)

        @pl.when(i == 0)
        def _():
            for r in (dcwg_ref, dcwv_ref, dcbg_ref, dcbv_ref):
                r[...] = jnp.zeros_like(r)

        after = (an_ref[...].astype(F32) * keep_next).astype(BF16)
        a_ext = jnp.concatenate([a_ref[...], after], axis=0)
        for s in range(tc // sc):
            ls = slice(s * sc, (s + 1) * sc)
            df_ext = lax.dot_general(a_ext, w_ref[s * sc:(s + 1) * sc, :], _DN_NT, preferred_element_type=F32)
            df = df_ext[:tm + 8]
            cg = jnp.concatenate([cg_ref[:, ls].astype(F32), cgn_ref[:, ls].astype(F32)[:8]], axis=0)
            cv = jnp.concatenate([cv_ref[:, ls].astype(F32), cvn_ref[:, ls].astype(F32)[:8]], axis=0)
            gel, dgel = _gelu_parts(cg)

            def back(d, cw_ref, x_ref, dcw_ref, dcb_ref, slab):
                taps = _lane_taps(cw_ref, ls)
                d0 = d[:tm]
                d1 = pltpu.roll(d, tm + 8 - 1, 0)[:tm]
                d2 = pltpu.roll(d, tm + 8 - 2, 0)[:tm]
                dup_ref[slab, :, ls] = (taps[2] * d0 + taps[1] * d1 + taps[0] * d2).astype(BF16)
                xv = x_ref[:, ls].astype(F32)
                dcw_ref[2:3, ls] += _col_sum(xv * d0)
                dcw_ref[1:2, ls] += _col_sum(xv * d1)
                dcw_ref[0:1, ls] += _col_sum(xv * d2)
                dcb_ref[:, ls] += _col_sum(d0)

            back(df * cv * dgel, cwg_ref, g_ref, dcwg_ref, dcbg_ref, 0)
            back(df * gel, cwv_ref, v_ref, dcwv_ref, dcbv_ref, 1)

    nxt = lambda j, i: jnp.minimum((i + 1) * hb, S // 16 - 1)
    main = lambda s: pl.BlockSpec((None, tm, tc), lambda j, i: (s, i, j))
    halo = lambda s: pl.BlockSpec((None, 16, tc), lambda j, i: (s, nxt(j, i), j))
    acc3 = pl.BlockSpec((3, tc), lambda j, i: (0, j))
    acc1 = pl.BlockSpec((1, tc), lambda j, i: (0, j))
    return _pallas(
        body, 11, deps, name=name, grid=(nj, ni),
        in_specs=[pl.BlockSpec((tm, D), lambda j, i: (i, 0)), pl.BlockSpec((16, D), lambda j, i: (nxt(j, i), 0)),
                  pl.BlockSpec((tc, D), lambda j, i: (j, 0)),
                  main(0), main(1), main(0), main(1), halo(0), halo(1),
                  pl.BlockSpec((3, tc), lambda j, i: (0, j)), pl.BlockSpec((3, tc), lambda j, i: (0, j + nj))],
        out_specs=[pl.BlockSpec((2, tm, tc), lambda j, i: (0, i, j)), acc3, acc3, acc1, acc1],
        out_shape=[jax.ShapeDtypeStruct((2, S, Fd), BF16), jax.ShapeDtypeStruct((3, Fd), F32),
                   jax.ShapeDtypeStruct((3, Fd), F32), jax.ShapeDtypeStruct((1, Fd), F32),
                   jax.ShapeDtypeStruct((1, Fd), F32)],
        compiler_params=_cp(2, 56))(dxb, dxb, w, up3, up3, upc3, upc3, upc3, upc3, cw, cw, *deps)


def _gm_forward_tile(pv, vg, vb, ws_ref, bsb_ref, mbuf, H, nc):
    W = H * CHUNK
    z, dz = _gelu_parts(pv)
    u, v0 = z[:, :W], z[:, W:]
    xc = v0 - _rows_mean(v0)
    rs = lax.rsqrt(_rows_mean(xc * xc) + LN_EPS)
    vh = xc * rs
    vnb = (vh * vg + vb).astype(BF16)
    mask = lax.broadcasted_iota(jnp.int32, (CHUNK, CHUNK), 0) >= lax.broadcasted_iota(jnp.int32, (CHUNK, CHUNK), 1)
    for h in range(H):
        cs = slice(h * CHUNK, (h + 1) * CHUNK)
        wm = jnp.where(mask, ws_ref[h], 0.0).astype(BF16)
        vcat = jnp.concatenate([vnb[c * CHUNK:(c + 1) * CHUNK, cs] for c in range(nc)], axis=1)
        mix = jnp.dot(wm, vcat, preferred_element_type=F32)
        for c in range(nc):
            mbuf[c * CHUNK:(c + 1) * CHUNK, cs] = mix[:, c * CHUNK:(c + 1) * CHUNK] + bsb_ref[h]
    return dz, u, vh, rs, vnb, mask


def _gm_fwd_call(p, v_g, v_b, ws, bsb, out_g, name, deps=()):
    S = p.shape[0]
    H = ws.shape[0]
    W = H * CHUNK
    tm = _tile(S, 512, CHUNK)
    nc = tm // CHUNK

    def body(p_ref, vg_ref, vb_ref, ws_ref, bsb_ref, og_ref, y_ref, mbuf):
        _, u, _, _, _, _ = _gm_forward_tile(p_ref[...], vg_ref[...], vb_ref[...], ws_ref, bsb_ref, mbuf, H, nc)
        yg = u * mbuf[...]
        r = lax.rsqrt(_rows_mean(yg * yg) + RMS_EPS)
        y_ref[...] = (yg * r * og_ref[...]).astype(BF16)

    vec = pl.BlockSpec((1, W), lambda i: (0, 0))
    mat = pl.BlockSpec((H, CHUNK, CHUNK), lambda i: (0, 0, 0))
    return _pallas(
        body, 6, deps, name=name, grid=(S // tm,),
        in_specs=[pl.BlockSpec((tm, 2 * W), lambda i: (i, 0)), vec, vec, mat, mat, vec],
        out_specs=pl.BlockSpec((tm, W), lambda i: (i, 0)),
        out_shape=jax.ShapeDtypeStruct((S, 2 * W), BF16),
        scratch_shapes=[pltpu.VMEM((tm, W), F32)], compiler_params=_cp(1))(p, v_g, v_b, ws, bsb, out_g, *deps)


def _gm_bwd_call(p, d_y, v_g, v_b, ws, bsb, out_g, name, deps=()):
    S = p.shape[0]
    H = ws.shape[0]
    W = H * CHUNK
    tm = _tile(S, 512, CHUNK)
    nc = tm // CHUNK
    ni = S // tm

    def body(p_ref, dy_ref, vg_ref, vb_ref, ws_ref, bsb_ref, og_ref,
             dp_ref, dvg_ref, dvb_ref, dws_ref, dbs_ref, dog_ref, mbuf, dvbuf):
        i = pl.program_id(0)

        @pl.when(i == 0)
        def _():
            for r in (dvg_ref, dvb_ref, dws_ref, dbs_ref, dog_ref):
                r[...] = jnp.zeros_like(r)

        vg = vg_ref[...]
        dz, u, vh, rs, vnb, mask = _gm_forward_tile(p_ref[...], vg, vb_ref[...], ws_ref, bsb_ref, mbuf, H, nc)
        mixed = mbuf[...]
        yg = u * mixed
        r = lax.rsqrt(_rows_mean(yg * yg) + RMS_EPS)
        yn = yg * r
        dya = dy_ref[...]
        dog_ref[...] += _col_sum(dya * yn)
        dyg = dya * og_ref[...]
        dygm = r * (dyg - yn * _rows_mean(dyg * yn))
        du = dygm * mixed
        dmix = dygm * u
        dmb = dmix.astype(BF16)
        for h in range(H):
            cs = slice(h * CHUNK, (h + 1) * CHUNK)
            wm = jnp.where(mask, ws_ref[h], 0.0).astype(BF16)
            dcat = jnp.concatenate([dmb[c * CHUNK:(c + 1) * CHUNK, cs] for c in range(nc)], axis=1)
            vcat = jnp.concatenate([vnb[c * CHUNK:(c + 1) * CHUNK, cs] for c in range(nc)], axis=1)
            dvn = lax.dot_general(wm, dcat, _DN_TN, preferred_element_type=F32)
            dws_ref[h] += jnp.where(mask, lax.dot_general(dcat, vcat, _DN_NT, preferred_element_type=F32), 0.0)
            dbs = dmix[0:CHUNK, cs]
            for c in range(1, nc):
                dbs = dbs + dmix[c * CHUNK:(c + 1) * CHUNK, cs]
            dbs_ref[h] += dbs
            for c in range(nc):
                dvbuf[c * CHUNK:(c + 1) * CHUNK, cs] = dvn[:, c * CHUNK:(c + 1) * CHUNK]
        dvn_all = dvbuf[...]
        dvg_ref[...] += _col_sum(dvn_all * vh)
        dvb_ref[...] += _col_sum(dvn_all)
        dvh = dvn_all * vg
        dv0 = rs * (dvh - _rows_mean(dvh) - vh * _rows_mean(dvh * vh))
        dp_ref[...] = (jnp.concatenate([du, dv0], axis=1) * dz).astype(BF16)

        @pl.when(i == ni - 1)
        def _():
            for h in range(H):
                dbs_ref[h] = jnp.broadcast_to(jnp.sum(dbs_ref[h], axis=1, keepdims=True), (CHUNK, CHUNK))

    vec = pl.BlockSpec((1, W), lambda i: (0, 0))
    mat = pl.BlockSpec((H, CHUNK, CHUNK), lambda i: (0, 0, 0))
    vshape = jax.ShapeDtypeStruct((1, W), F32)
    mshape = jax.ShapeDtypeStruct((H, CHUNK, CHUNK), F32)
    return _pallas(
        body, 7, deps, name=name, grid=(ni,),
        in_specs=[pl.BlockSpec((tm, 2 * W), lambda i: (i, 0)), pl.BlockSpec((tm, W), lambda i: (i, 0)),
                  vec, vec, mat, mat, vec],
        out_specs=[pl.BlockSpec((tm, 2 * W), lambda i: (i, 0)), vec, vec, mat, mat, vec],
        out_shape=[jax.ShapeDtypeStruct((S, 4 * W), BF16), vshape, vshape, mshape, mshape, vshape],
        scratch_shapes=[pltpu.VMEM((tm, W), F32), pltpu.VMEM((tm, W), F32)],
        compiler_params=_cp(1))(p, d_y, v_g, v_b, ws, bsb, out_g, *deps)


def _lru_gates(prev8, xl, cw, cb, wa_ref, ba, wx_ref, bx, lam, H):
    sh = [_shift_down(prev8, xl, k) for k in range(4)]
    xr = cw[3] * sh[0] + cw[2] * sh[1] + cw[1] * sh[2] + cw[0] * sh[3] + cb
    xrb = xr.astype(BF16)
    rp, ip = [], []
    for h in range(H):
        cs = slice(h * CHUNK, (h + 1) * CHUNK)
        rp.append(jnp.dot(xrb[:, cs], wa_ref[h].astype(BF16), preferred_element_type=F32))
        ip.append(jnp.dot(xrb[:, cs], wx_ref[h].astype(BF16), preferred_element_type=F32))
    r = _sigmoid(jnp.concatenate(rp, axis=1) + ba)
    ig = _sigmoid(jnp.concatenate(ip, axis=1) + bx)
    sp = _softplus(-lam)
    t = jnp.tanh((-LRU_C) * r * sp)
    q = lax.rsqrt(1.0 - t)
    a = jnp.sqrt(1.0 + t) * q
    mult = jnp.sqrt(-2.0 * t) * q
    a2_over_mult = (1.0 + t) * q * lax.rsqrt(-2.0 * t)
    return xr, xrb, r, ig, sp, a, mult, a2_over_mult, sh


def _lru_fwd_call(p, cw, cb, wa, ba, wx, bx, lam, out_g, y_half, name):
    S = p.shape[0]
    H = wa.shape[0]
    W = H * CHUNK
    tm = _tile(S, 256, 16)
    ng = tm // 8

    def body(pg_ref, px_ref, cw_ref, cb_ref, wa_ref, ba_ref, wx_ref, bx_ref, lam_ref, og_ref, y_in_ref,
             y_ref, h_ref, saved_ref, xprev, hcar, abuf, bbuf):
        @pl.when(pl.program_id(0) == 0)
        def _():
            xprev[...] = jnp.zeros_like(xprev)
            hcar[...] = jnp.zeros_like(hcar)

        xl = px_ref[...]
        xr, _, r_gate, ig, _, a, mult, a2m, _ = _lru_gates(xprev[...], xl, _taps(cw_ref), cb_ref[...], wa_ref,
                                                           ba_ref[...], wx_ref, bx_ref[...], lam_ref[...], H)
        for k, val in enumerate((xr, r_gate, ig, a, mult, a2m)):
            saved_ref[k] = val
        xprev[...] = xl[tm - 8:]
        b = mult * (ig * xr)
        sub = lax.broadcasted_iota(jnp.int32, (tm, W), 0) & 7
        for d in (1, 2, 4):
            m = sub >= d
            a_s = jnp.where(m, pltpu.roll(a, d, 0), 1.0)
            b_s = jnp.where(m, pltpu.roll(b, d, 0), 0.0)
            b = a * b_s + b
            a = a * a_s
        abuf[...] = a
        bbuf[...] = b

        def step(g, carry):
            r0 = pl.multiple_of(g * 8, 8)
            h_ref[pl.ds(r0, 8), :] = abuf[pl.ds(r0, 8), :] * carry + bbuf[pl.ds(r0, 8), :]
            return jnp.broadcast_to(h_ref[pl.ds(r0 + 7, 1), :], (8, W))

        hcar[...] = lax.fori_loop(0, ng, step, hcar[...])
        yl = h_ref[...] * _gelu(pg_ref[...])
        r = lax.rsqrt(_rows_mean(yl * yl) + RMS_EPS)
        y_ref[...] = (yl * r * og_ref[...]).astype(BF16)

    vec = pl.BlockSpec((1, W), lambda i: (0, 0))
    mat = pl.BlockSpec((H, CHUNK, CHUNK), lambda i: (0, 0, 0))
    return pl.pallas_call(
        body, name=name, grid=(S // tm,),
        in_specs=[pl.BlockSpec((tm, W), lambda i: (i, 2)), pl.BlockSpec((tm, W), lambda i: (i, 3)),
                  pl.BlockSpec((4, W), lambda i: (0, 0)), vec, mat, vec, mat, vec, vec, vec,
                  pl.BlockSpec(memory_space=pl.ANY)],
        out_specs=[pl.BlockSpec((tm, W), lambda i: (i, 1)), pl.BlockSpec((tm, W), lambda i: (i, 0)),
                   pl.BlockSpec((6, tm, W), lambda i: (0, i, 0))],
        out_shape=[jax.ShapeDtypeStruct((S, 2 * W), BF16), jax.ShapeDtypeStruct((S, W), F32),
                   jax.ShapeDtypeStruct((6, S, W), F32)],
        input_output_aliases={10: 0},
        scratch_shapes=[pltpu.VMEM((8, W), F32), pltpu.VMEM((8, W), F32), pltpu.VMEM((tm, W), F32),
                        pltpu.VMEM((tm, W), F32)],
        compiler_params=_cp(1))(p, p, cw, cb, wa, ba, wx, bx, lam, out_g, y_half)


def _lru_bwd_call(p, hs, saved, d_y, cw, wa, wx, lam, out_g, dp_half, name):
    S = p.shape[0]
    H = wa.shape[0]
    W = H * CHUNK
    tm = _tile(S, 256, 16)
    ng = tm // 8
    ni = S // tm
    hb = tm // 8

    def body(pg_ref, px_ref, saved_ref, h_ref, hp_ref, dy_ref, cw_ref, wa_ref, wx_ref, lam_ref, og_ref, dp_in_ref,
             dp_ref, dcw_ref, dcb_ref, dwa_ref, dba_ref, dwx_ref, dbx_ref, dlam_ref, dog_ref,
             a_next, e_next, dxr_next, abuf, bbuf, ebuf, dxr0, dprb_buf, dpib_buf, sums):
        i = pl.program_id(0)
        ri = ni - 1 - i

        @pl.when(i == 0)
        def _():
            for r in (dcw_ref, dcb_ref, dwa_ref, dba_ref, dwx_ref, dbx_ref, dlam_ref, dog_ref,
                      a_next, e_next, dxr_next):
                r[...] = jnp.zeros_like(r)

        keep_prev = jnp.where(ri == 0, 0.0, 1.0)
        cw_ = _taps(cw_ref)
        lam_ = lam_ref[...]
        xl = px_ref[...]
        a = saved_ref[3]
        sp = _softplus(-lam_)
        sums[...] = jnp.zeros_like(sums)
        og = og_ref[...]
        lane_groups = [slice(q * GATE_LANES, (q + 1) * GATE_LANES) for q in range(W // GATE_LANES)]

        def fold(v):
            return sum(v[:, k * LANES:(k + 1) * LANES] for k in range(GATE_LANES // LANES))

        def out_grads(g, carry):
            rows = pl.ds(pl.multiple_of(g * 16, 16), 16)
            s_yy = jnp.zeros((16, LANES), F32)
            s_dy = jnp.zeros((16, LANES), F32)
            for ls in lane_groups:
                yl = h_ref[rows, ls] * _gelu(pg_ref[rows, ls])
                s_yy = s_yy + fold(yl * yl)
                s_dy = s_dy + fold(dy_ref[rows, ls] * og[:, ls] * yl)
            rr = lax.rsqrt(jnp.sum(s_yy, axis=1, keepdims=True) * (1.0 / W) + RMS_EPS)
            c = rr * rr * jnp.sum(s_dy, axis=1, keepdims=True) * (1.0 / W)
            for ls in lane_groups:
                hq = h_ref[rows, ls]
                gg, dgg = _gelu_parts(pg_ref[rows, ls])
                yl = hq * gg
                dyb = dy_ref[rows, ls]
                sums[3, :, ls] += dyb * (yl * rr)
                dyl = rr * (dyb * og[:, ls] - yl * c)
                bbuf[rows, ls] = dyl * gg
                dp_ref[rows, ls] = (dyl * hq * dgg).astype(BF16)
            return carry

        lax.fori_loop(0, tm // 16, out_grads, 0, unroll=2)
        dog_ref[...] += _col_sum(sums[3])

        an = _shift_up(a, a_next[...], 1)
        eb = bbuf[...]
        sub = lax.broadcasted_iota(jnp.int32, (tm, W), 0) & 7
        for d in (1, 2, 4):
            m = sub < 8 - d
            a_s = jnp.where(m, pltpu.roll(an, tm - d, 0), 1.0)
            e_s = jnp.where(m, pltpu.roll(eb, tm - d, 0), 0.0)
            eb = an * e_s + eb
            an = an * a_s
        abuf[...] = an
        bbuf[...] = eb

        def step(g, carry):
            r0 = pl.multiple_of((ng - 1 - g) * 8, 8)
            ebuf[pl.ds(r0, 8), :] = abuf[pl.ds(r0, 8), :] * carry + bbuf[pl.ds(r0, 8), :]
            return jnp.broadcast_to(ebuf[pl.ds(r0, 1), :], (8, W))

        lax.fori_loop(0, ng, step, jnp.broadcast_to(e_next[0:1, :], (8, W)))
        a_next[...] = a[0:8]
        e_next[...] = ebuf[0:8, :]

        row16 = lax.broadcasted_iota(jnp.int32, (16, GATE_LANES), 0)
        h_before = hp_ref[7:8, :] * keep_prev
        r_scale = (-LRU_C) * sp

        def gate_grads(g, carry):
            r0 = pl.multiple_of(g * 16, 16)
            above = jnp.maximum(r0 - 1, 0)
            for q in range(W // GATE_LANES):
                ls = slice(q * GATE_LANES, (q + 1) * GATE_LANES)
                e = ebuf[pl.ds(r0, 16), ls]
                h_prev_row = jnp.where(g == 0, h_before[:, ls], h_ref[pl.ds(above, 1), ls])
                hm1 = jnp.where(row16 == 0, h_prev_row, pltpu.roll(h_ref[pl.ds(r0, 16), ls], 1, 0))
                xr_, r_, ig_, a_, mult_, a2m_ = [saved_ref[k, pl.ds(r0, 16), ls] for k in range(6)]
                em = e * mult_
                dxr0[pl.ds(r0, 16), ls] = em * ig_
                dla = e * hm1 * a_ - e * ig_ * xr_ * a2m_
                dpr = dla * r_scale[:, ls] * r_ * (1.0 - r_)
                dpi = em * xr_ * ig_ * (1.0 - ig_)
                sums[0, :, ls] += dla * r_
                sums[1, :, ls] += dpr
                sums[2, :, ls] += dpi
                dprb_buf[pl.ds(r0, 16), ls] = dpr.astype(BF16)
                dpib_buf[pl.ds(r0, 16), ls] = dpi.astype(BF16)
            return carry

        lax.fori_loop(0, tm // 16, gate_grads, 0)
        dlam_ref[...] += (-LRU_C) * _col_sum(sums[0])
        dba_ref[...] += _col_sum(sums[1])
        dbx_ref[...] += _col_sum(sums[2])
        dprb = dprb_buf[...]
        dpib = dpib_buf[...]
        dxr = dxr0[...]
        xrb = saved_ref[0].astype(BF16)
        back = []
        for h in range(H):
            cs = slice(h * CHUNK, (h + 1) * CHUNK)
            wab = wa_ref[h].astype(BF16)
            wxb = wx_ref[h].astype(BF16)
            back.append(lax.dot_general(dprb[:, cs], wab, _DN_NT, preferred_element_type=F32)
                        + lax.dot_general(dpib[:, cs], wxb, _DN_NT, preferred_element_type=F32))
            dwa_ref[h] += lax.dot_general(xrb[:, cs], dprb[:, cs], _DN_TN, preferred_element_type=F32)
            dwx_ref[h] += lax.dot_general(xrb[:, cs], dpib[:, cs], _DN_TN, preferred_element_type=F32)
        dxr = dxr + jnp.concatenate(back, axis=1)

        nxt = dxr_next[...]
        ahead = [_shift_up(dxr, nxt, j) for j in range(4)]
        dxl = cw_[3] * ahead[0] + cw_[2] * ahead[1] + cw_[1] * ahead[2] + cw_[0] * ahead[3]
        dxr_next[...] = dxr[0:8]
        for k in range(4):
            dcw_ref[k:k + 1, :] += _col_sum(xl * ahead[3 - k])
        dcb_ref[...] += _col_sum(dxr)
        dp_ref[:, W:] = dxl.astype(BF16)

        @pl.when(i == ni - 1)
        def _():
            dlam_ref[...] = -dlam_ref[...] * _sigmoid(-lam_)

    vec = pl.BlockSpec((1, W), lambda i: (0, 0))
    mat = pl.BlockSpec((H, CHUNK, CHUNK), lambda i: (0, 0, 0))
    rev = lambda i: ni - 1 - i
    prev = lambda i: jnp.maximum(rev(i) * hb - 1, 0)
    vshape = jax.ShapeDtypeStruct((1, W), F32)
    mshape = jax.ShapeDtypeStruct((H, CHUNK, CHUNK), F32)
    tile = lambda: pltpu.VMEM((tm, W), F32)
    car = lambda: pltpu.VMEM((8, W), F32)
    return pl.pallas_call(
        body, name=name, grid=(ni,),
        in_specs=[pl.BlockSpec((tm, W), lambda i: (rev(i), 2)), pl.BlockSpec((tm, W), lambda i: (rev(i), 3)),
                  pl.BlockSpec((6, tm, W), lambda i: (0, rev(i), 0)),
                  pl.BlockSpec((tm, W), lambda i: (rev(i), 0)), pl.BlockSpec((8, W), lambda i: (prev(i), 0)),
                  pl.BlockSpec((tm, W), lambda i: (rev(i), 1)),
                  pl.BlockSpec((4, W), lambda i: (0, 0)), mat, mat, vec, vec,
                  pl.BlockSpec(memory_space=pl.ANY)],
        out_specs=[pl.BlockSpec((tm, 2 * W), lambda i: (rev(i), 1)), pl.BlockSpec((4, W), lambda i: (0, 0)), vec,
                   mat, vec, mat, vec, vec, vec],
        out_shape=[jax.ShapeDtypeStruct((S, 4 * W), BF16), jax.ShapeDtypeStruct((4, W), F32), vshape,
                   mshape, vshape, mshape, vshape, vshape, vshape],
        input_output_aliases={11: 0},
        scratch_shapes=[car(), car(), car(), tile(), tile(), tile(), tile(), pltpu.VMEM((tm, W), BF16),
                        pltpu.VMEM((tm, W), BF16), pltpu.VMEM((4, 16, W), F32)],
        compiler_params=_cp(1, 56))(p, p, saved, hs, hs, d_y, cw, wa, wx, lam, out_g, dp_half)


def _rows128(a):
    return a.reshape(-1, LANES).astype(F32)


def _pack(arrays, pad_to=256):
    flat = jnp.concatenate([_rows128(a) for a in arrays], axis=0)
    pad = (-flat.shape[0]) % pad_to
    if pad:
        flat = jnp.concatenate([flat, jnp.zeros((pad, LANES), F32)], axis=0)
    return flat


def _unpack(flat, shapes):
    out, r = [], 0
    for s in shapes:
        n = 1
        for d in s:
            n *= d
        out.append(flat[r:r + n // LANES].reshape(s))
        r += n // LANES
    return out


def kernel(x, norm1_g, w_in, gm_v_g, gm_v_b, gm_ws, gm_bs, lru_conv_w, lru_conv_b, lru_wa, lru_ba, lru_wx, lru_bx, lru_lambda, gm_out_g, lru_out_g, w_out, norm2_g, ffn_w_up, ffn_conv_w, ffn_conv_b, ffn_w_down, final_g, loss_target, m_norm1_g, m_w_in, m_gm_v_g, m_gm_v_b, m_gm_ws, m_gm_bs, m_lru_conv_w, m_lru_conv_b, m_lru_wa, m_lru_ba, m_lru_wx, m_lru_bx, m_lru_lambda, m_gm_out_g, m_lru_out_g, m_w_out, m_norm2_g, m_ffn_w_up, m_ffn_conv_w, m_ffn_conv_b, m_ffn_w_down, m_final_g, v_norm1_g, v_w_in, v_gm_v_g, v_gm_v_b, v_gm_ws, v_gm_bs, v_lru_conv_w, v_lru_conv_b, v_lru_wa, v_lru_ba, v_lru_wx, v_lru_bx, v_lru_lambda, v_gm_out_g, v_lru_out_g, v_w_out, v_norm2_g, v_ffn_w_up, v_ffn_conv_w, v_ffn_conv_b, v_ffn_w_down, v_final_g):
    wts = dict(norm1_g=norm1_g, w_in=w_in, gm_v_g=gm_v_g, gm_v_b=gm_v_b, gm_ws=gm_ws, gm_bs=gm_bs,
               lru_conv_w=lru_conv_w, lru_conv_b=lru_conv_b, lru_wa=lru_wa, lru_ba=lru_ba, lru_wx=lru_wx,
               lru_bx=lru_bx, lru_lambda=lru_lambda, gm_out_g=gm_out_g, lru_out_g=lru_out_g, w_out=w_out,
               norm2_g=norm2_g, ffn_w_up=ffn_w_up, ffn_conv_w=ffn_conv_w, ffn_conv_b=ffn_conv_b,
               ffn_w_down=ffn_w_down, final_g=final_g)
    mom = dict(norm1_g=m_norm1_g, w_in=m_w_in, gm_v_g=m_gm_v_g, gm_v_b=m_gm_v_b, gm_ws=m_gm_ws, gm_bs=m_gm_bs,
               lru_conv_w=m_lru_conv_w, lru_conv_b=m_lru_conv_b, lru_wa=m_lru_wa, lru_ba=m_lru_ba, lru_wx=m_lru_wx,
               lru_bx=m_lru_bx, lru_lambda=m_lru_lambda, gm_out_g=m_gm_out_g, lru_out_g=m_lru_out_g, w_out=m_w_out,
               norm2_g=m_norm2_g, ffn_w_up=m_ffn_w_up, ffn_conv_w=m_ffn_conv_w, ffn_conv_b=m_ffn_conv_b,
               ffn_w_down=m_ffn_w_down, final_g=m_final_g)
    var = dict(norm1_g=v_norm1_g, w_in=v_w_in, gm_v_g=v_gm_v_g, gm_v_b=v_gm_v_b, gm_ws=v_gm_ws, gm_bs=v_gm_bs,
               lru_conv_w=v_lru_conv_w, lru_conv_b=v_lru_conv_b, lru_wa=v_lru_wa, lru_ba=v_lru_ba, lru_wx=v_lru_wx,
               lru_bx=v_lru_bx, lru_lambda=v_lru_lambda, gm_out_g=v_gm_out_g, lru_out_g=v_lru_out_g, w_out=v_w_out,
               norm2_g=v_norm2_g, ffn_w_up=v_ffn_w_up, ffn_conv_w=v_ffn_conv_w, ffn_conv_b=v_ffn_conv_b,
               ffn_w_down=v_ffn_w_down, final_g=v_final_g)

    xi, yi, ci = lax.axis_index("x"), lax.axis_index("y"), lax.axis_index("c")
    chip = 2 * xi + yi
    dev = 2 * chip + ci
    core_chip = jnp.stack([ci, chip]).astype(jnp.int32)

    xs = x[0]
    tgt = loss_target[0]
    S, D = xs.shape
    H = gm_ws.shape[1]
    W = H * CHUNK
    Fd = ffn_w_down.shape[1] * N_DEV
    lcw_cols = lru_conv_w.shape[2]
    fcw_cols = ffn_conv_w.shape[2]

    dev1 = jnp.reshape(dev, (1,)).astype(jnp.int32)

    def place_own(shards, name):
        return [_place_own_call(s, dev1, dt, "%s_own%d" % (name, k)) for k, (s, dt) in enumerate(shards)]

    def gather_start(shards, name, after=()):
        lands = place_own(shards, name)
        return _exchange_start([], lands, 4 * len(lands), _gather_stage1_copies(len(lands)), name + "_ici", after)

    def gather_forward(lands, name, after=()):
        return _exchange_start([], lands, 3 * len(lands), _gather_stage2_copies(len(lands)), name + "_d2d", after)

    def pair_start(g, name, after=()):
        return _exchange_start([g], [lax.empty((4,) + g.shape[1:], F32)], 4, _pair_copies(1), name, after)

    def chip_start(p16, name, after=()):
        return _exchange_start([p16], [lax.empty((3,) + p16.shape[1:], BF16)], 3, _chip_copies(1), name, after)

    vgm_g, vgm_b = gm_v_g, gm_v_b
    ws, wa, wx = gm_ws[0], lru_wa[0], lru_wx[0]
    bsb = jnp.broadcast_to(gm_bs[0][:, :, None], (H, CHUNK, CHUNK))
    ba, bx = lru_ba.reshape(1, W), lru_bx.reshape(1, W)
    fcb = ffn_conv_b
    fing = final_g.reshape(1, D)

    conv_pack = _pack([lru_conv_w[0], ffn_conv_w[0]], pad_to=8)
    lands = place_own([(w_in[0], BF16), (conv_pack, F32)], "gather_in")
    ga_pair = _exchange_start([], lands, 2, _gather_stage1_copies(2, to_chips=False), "gather_in_pair")
    ga1 = _exchange_start([], ga_pair.bufs, 6, _gather_stage1_copies(2, to_sibling=False), "gather_in_ici")
    h1 = _rmsnorm_call(xs, norm1_g, "norm1", deps=(ga1.token,))
    ga_pair.bufs = ga1.bufs
    _, la = _exchange_wait(ga_pair, after=(h1,))
    own_blocks = jnp.stack([dev, dev + 1 - 2 * ci]).astype(jnp.int32)
    other_blocks = ((2 * chip + 2 + jnp.arange(N_DEV - 2)) % N_DEV).astype(jnp.int32)
    p_own = _mm_some_blocks_call(h1, la[0], own_blocks, F32, "in_proj_own")
    ga1.bufs = la
    _, la = _exchange_wait(ga1, after=(p_own,))
    ga2 = gather_forward(la, "gather_in")
    gb1 = gather_start([(w_out[0], BF16)], "gather_out", after=(ga2.token,))
    gc1 = gather_start([(ffn_w_up[0], BF16)], "gather_up", after=(gb1.token,))
    gd1 = gather_start([(ffn_w_down[0], BF16)], "gather_down", after=(gc1.token,))
    _, (win_g, conv_g) = _exchange_wait(ga2, after=(gd1.token,))
    n_l = 4 * lcw_cols // LANES
    n_f = 3 * fcw_cols // LANES
    lcw = conv_g[:, :n_l].reshape(N_DEV, 4, lcw_cols).transpose(1, 0, 2).reshape(4, N_DEV * lcw_cols)
    fcw = conv_g[:, n_l:n_l + n_f].reshape(N_DEV, 3, fcw_cols).transpose(1, 0, 2).reshape(3, N_DEV * fcw_cols)

    p = _mm_some_blocks_call(h1, win_g, other_blocks, F32, "in_proj_rest", out_so_far=p_own)
    win_rows = _unblock_call(win_g, "w_in_rows")
    _, lb = _exchange_wait(gb1, after=(p,))
    gb2 = gather_forward(lb, "gather_out")
    y_half = _gm_fwd_call(p, vgm_g, vgm_b, ws, bsb, gm_out_g, "gmlp_fwd", deps=(gb2.token,))
    y, hs, lru_saved = _lru_fwd_call(p, lcw, lru_conv_b, wa, ba, wx, bx, lru_lambda, lru_out_g, y_half, "lru_fwd")
    _, lc = _exchange_wait(gc1, after=(y,))
    gc2 = gather_forward(lc, "gather_up")
    _, (wout_g,) = _exchange_wait(gb2, after=(y, gc2.token))
    wout_full = wout_g.reshape(D, D)
    x2 = _mm_out_call(xs, y, wout_full, "out_proj")
    h2 = _rmsnorm_call(x2, norm2_g, "norm2")
    _, (wup_g,) = _exchange_wait(gc2, after=(h2,))
    _, ld = _exchange_wait(gd1, after=(h2,))
    gd2 = gather_forward(ld, "gather_down")
    up3, upc3, f = _ffn_up_act_call(h2, wup_g, fcw, fcb, "ffn_up", deps=(gd2.token,))
    _, (wdown_g,) = _exchange_wait(gd2, after=(f,))
    wdown_full = wdown_g.reshape(Fd, D)
    dx3, dx3b, loss_acc, d_final = _mm_down_loss_call(x2, f, wdown_full, fing, tgt, "ffn_down_loss")

    g_wdown = _mm_tn_rows_call(f, dx3b, "ffn_down_dw").reshape((N_DEV,) + ffn_w_down.shape[1:])
    pd = pair_start(g_wdown, "pair_down")
    d_up3, dfcw_g, dfcw_v, dfcb_g, dfcb_v = _ffn_down_dx_act_bwd_call(dx3b, wdown_full, up3, upc3, fcw, "ffn_down_dx",
                                                                     deps=(pd.token,))
    (g_wdown,), (r1,) = _exchange_wait(pd, after=(d_up3,))
    own_down, p16 = _pair_add_call(g_wdown, r1, core_chip, "pair_add_down")
    cd = chip_start(p16, "chip_down")
    g_wup = _mm_tn_cols_call(h2, d_up3, N_DEV, ffn_w_up.shape[2], "ffn_up_dw", deps=(cd.token,))
    pu = pair_start(g_wup, "pair_up")
    dx2, dx2b, d_norm2 = _mm_dx_norm_call(d_up3, wup_g, dx3, x2, norm2_g, "ffn_up_dx", deps=(pu.token,))
    g_wout = _mm_tn_rows_call(y, dx2b, "out_proj_dw").reshape((N_DEV,) + w_out.shape[1:])
    po = pair_start(g_wout, "pair_out")
    d_y = _mm_nt_call(dx2b, wout_full, F32, "out_proj_dx", deps=(po.token,))
    (g_wup,), (r1,) = _exchange_wait(pu, after=(d_y,))
    own_up, p16 = _pair_add_call(g_wup, r1, core_chip, "pair_add_up")
    _, (r2_down,) = _exchange_wait(cd, after=(p16,))
    cu = chip_start(p16, "chip_up", after=(r2_down,))
    dp_half, d_vg, d_vb, d_ws, d_bs, d_gog = _gm_bwd_call(p, d_y, vgm_g, vgm_b, ws, bsb, gm_out_g, "gmlp_bwd",
                                                          deps=(cu.token,))
    d_p2, d_lcw, d_lcb, d_wa, d_ba, d_wx, d_bx, d_lam, d_log = _lru_bwd_call(
        p, hs, lru_saved, d_y, lcw, wa, wx, lru_lambda, lru_out_g, dp_half, "lru_bwd")
    d_p = d_p2[None]
    (g_wout,), (r1,) = _exchange_wait(po, after=(d_p,))
    own_out, p16_out = _pair_add_call(g_wout, r1, core_chip, "pair_add_out")
    g_win = _mm_tn_cols_call(h1, d_p, N_DEV, w_in.shape[2], "in_proj_dw")
    pi = pair_start(g_win, "pair_in")
    _, (r2_up,) = _exchange_wait(cu, after=(g_win,))
    co = chip_start(p16_out, "chip_out", after=(r2_up,))
    gx_a, dn_a = _mm_nt_norm_call(d_p[0], win_rows, dx2, xs, norm1_g, "in_proj_dx_a", deps=(co.token, pi.token),
                                  part=(0, 2))
    (g_win,), (r1,) = _exchange_wait(pi, after=(gx_a,))
    own_in, p16 = _pair_add_call(g_win, r1, core_chip, "pair_add_in")
    _, (r2_out,) = _exchange_wait(co, after=(p16,))
    ci_ = chip_start(p16, "chip_in", after=(r2_out,))
    grad_x, dn_b = _mm_nt_norm_call(d_p[0], win_rows, dx2, xs, norm1_g, "in_proj_dx_b", deps=(ci_.token,),
                                    part=(1, 2), dx_so_far=gx_a)

    small_g = dict(norm1_g=dn_a + dn_b, gm_v_g=d_vg, gm_v_b=d_vb, gm_ws=d_ws, gm_bs=d_bs[:, :, 0], lru_conv_b=d_lcb,
                   lru_wa=d_wa, lru_ba=d_ba, lru_wx=d_wx, lru_bx=d_bx, lru_lambda=d_lam, gm_out_g=d_gog,
                   lru_out_g=d_log, norm2_g=d_norm2,
                   ffn_conv_b=jnp.concatenate([dfcb_g, dfcb_v], axis=1), final_g=d_final)
    rep = _pack([small_g[n] for n in SMALL])
    conv_part = _pack([d_lcw, jnp.concatenate([dfcw_g, dfcw_v], axis=1)], pad_to=8)
    n_rep, n_conv = rep.shape[0], conv_part.shape[0]
    gs1 = gather_start([(jnp.concatenate([rep, conv_part], axis=0), F32)], "gather_small")

    def adamw_big(n, own, r2, deps=()):
        return _adamw_call(wts[n][0], mom[n][0], var[n][0], [(own, None), (r2, 0), (r2, 1), (r2, 2)], "adamw_" + n, deps)

    res = {}
    res["ffn_w_down"] = adamw_big("ffn_w_down", own_down, r2_down, (gs1.token,))
    res["ffn_w_up"] = adamw_big("ffn_w_up", own_up, r2_up, (gs1.token,))
    res["w_out"] = adamw_big("w_out", own_out, r2_out, (gs1.token,))
    _, ls = _exchange_wait(gs1, after=(res["w_out"][0], res["ffn_w_up"][0], res["ffn_w_down"][0]))
    gs2 = gather_forward(ls, "gather_small")
    _, (r2_in,) = _exchange_wait(ci_, after=(gs2.token,))
    res["w_in"] = adamw_big("w_in", own_in, r2_in)
    _, (parts,) = _exchange_wait(gs2, after=(res["w_in"][0],))
    g_rep, d_rep, m_rep, v_rep = _adamw_call(
        _pack([wts[n] for n in SMALL]), _pack([mom[n] for n in SMALL]), _pack([var[n] for n in SMALL]),
        [(parts, k) for k in range(N_DEV)], "adamw_small")
    shapes = [wts[n].shape for n in SMALL]
    for n, g_, d_, m_, v_ in zip(SMALL, _unpack(g_rep, shapes), _unpack(d_rep, shapes), _unpack(m_rep, shapes),
                                 _unpack(v_rep, shapes)):
        res[n] = (g_, d_, m_, v_)
    conv_sum = _sum_call(parts, n_rep, n_conv, "sum_conv_grads")
    g_lcw = conv_sum[:4 * W // LANES].reshape(4, W)
    g_fcw = conv_sum[4 * W // LANES:4 * W // LANES + 6 * Fd // LANES].reshape(3, 2 * Fd)
    for n, full in (("lru_conv_w", g_lcw), ("ffn_conv_w", g_fcw)):
        cols = wts[n].shape[2]
        mine = lax.dynamic_slice_in_dim(full, dev * cols, cols, axis=1)
        res[n] = _adamw_call(wts[n][0], mom[n][0], var[n][0], [(mine, None)], "adamw_" + n)

    loss = lax.psum(loss_acc[0, 0], ("x", "y", "c"))
    outs = [[], [], [], []]
    for n in WEIGHTS:
        for k in range(4):
            outs[k].append(res[n][k].reshape(wts[n].shape))
    return (loss, grad_x[None], *outs[0], *outs[1], *outs[2], *outs[3])
```
